```python
import math
import jax, jax.numpy as jnp
from jax import lax
import numpy as np


D_MODEL = 1024
BATCH = 8
SEQ = 2048
DEPTH = 2
DEC_BATCH = 32
DEC_SEQ = 1
PAST_LEN = 16384
PAGE_SIZE = 128

MLA_HEADS = 8
NOPE_DIM = 64
ROPE_DIM = 32
V_DIM = 64
Q_RANK = 256
KV_RANK = 128
ROPE_THETA = 10000.0
Q_BLOCK = 128
ATTN_SCALE = 1.0 / math.sqrt(NOPE_DIM + ROPE_DIM)
GM_HEADS = 8
GM_HEAD_DIM = 64
GM_WIDTH = GM_HEADS * GM_HEAD_DIM
GM_CHUNK = 128
S5_GROUP_DIM = 16
S5_GROUPS = 32
S5_STATE = 64
S5_WIDTH = S5_GROUPS * S5_GROUP_DIM
DT_MIN = 1e-3
DT_MAX = 1e-1
FFN_HIDDEN = 4 * D_MODEL
DN_ALPHA = (2 * DEPTH) ** 0.25
DN_BETA = (8 * DEPTH) ** -0.25
LN_EPS = 1e-5
RMS_EPS = 1e-6
N_EVEN = (DEPTH + 1) // 2
N_ODD = DEPTH // 2
EVEN_IN = Q_RANK + KV_RANK + ROPE_DIM + 2 * GM_WIDTH
EVEN_MIX = MLA_HEADS * V_DIM + GM_WIDTH

kernel_name = 'hybrid_mla_gmlp_s5_decoder_step'


def layer_norm(x, g, b):
    xf = x.astype(jnp.float32)
    mu = xf.mean(-1, keepdims=True)
    var = jnp.square(xf - mu).mean(-1, keepdims=True)
    y = (xf - mu) * lax.rsqrt(var + LN_EPS) * g.astype(jnp.float32) + b.astype(jnp.float32)
    return y.astype(x.dtype)


def rms_norm(x, g):
    xf = x.astype(jnp.float32)
    y = xf * lax.rsqrt(jnp.square(xf).mean(-1, keepdims=True) + RMS_EPS) * g.astype(jnp.float32)
    return y.astype(x.dtype)


def rope(x, pos):
    half = ROPE_DIM // 2
    freqs = ROPE_THETA ** (-jnp.arange(half, dtype=jnp.float32) / half)
    ang = pos.astype(jnp.float32)[:, None] * freqs
    ang = ang.reshape((1, ang.shape[0]) + (1,) * (x.ndim - 3) + (half,))
    cos, sin = jnp.cos(ang), jnp.sin(ang)
    xf = x.astype(jnp.float32)
    x1, x2 = xf[..., :half], xf[..., half:]
    return jnp.concatenate([x1 * cos - x2 * sin, x1 * sin + x2 * cos], axis=-1).astype(x.dtype)


def latent_attention(q_lat, q_pe, ckv, kpe, q_pos, k_pos):
    s = jnp.einsum('bthr,bkr->bhtk', q_lat, ckv) + jnp.einsum('bthe,bke->bhtk', q_pe, kpe)
    s = s.astype(jnp.float32) * ATTN_SCALE
    causal = k_pos[None, :] <= q_pos[:, None]
    p = jax.nn.softmax(jnp.where(causal, s, -jnp.inf), axis=-1).astype(ckv.dtype)
    return jnp.einsum('bhtk,bkr->bthr', p, ckv)


def mla_attention(q_lat, q_pe, ckv, kpe, q_pos, k_pos):
    b, t, h, r = q_lat.shape
    if t > Q_BLOCK and t % Q_BLOCK == 0:
        nb = t // Q_BLOCK
        qb = q_lat.reshape(b, nb, Q_BLOCK, h, r).swapaxes(0, 1)
        pb = q_pe.reshape(b, nb, Q_BLOCK, h, ROPE_DIM).swapaxes(0, 1)
        posb = q_pos.reshape(nb, Q_BLOCK)
        o = lax.map(lambda blk: latent_attention(blk[0], blk[1], ckv, kpe, blk[2], k_pos), (qb, pb, posb))
        return o.swapaxes(0, 1).reshape(b, t, h, r)
    return latent_attention(q_lat, q_pe, ckv, kpe, q_pos, k_pos)


def chunk_spatial_gate(v, w_s, b_s):
    b, t, _ = v.shape
    pad = (-t) % GM_CHUNK
    vp = jnp.pad(v, ((0, 0), (0, pad), (0, 0)))
    nc = (t + pad) // GM_CHUNK
    vh = vp.reshape(b, nc, GM_CHUNK, GM_HEADS, GM_HEAD_DIM)
    causal = jnp.tril(jnp.ones((GM_CHUNK, GM_CHUNK), dtype=bool))
    w = jnp.where(causal[None], w_s, 0)
    mixed = jnp.einsum('gts,bnsgd->bntgd', w, vh) + b_s.T[None, None, :, :, None]
    return mixed.reshape(b, t + pad, GM_WIDTH)[:, :t]


def even_mixer(x, pos, past_ckv, past_kpe, w_in, q_norm, w_q_b, kv_norm, w_kv_b,
               gm_norm_g, gm_norm_b, gm_w_s, gm_b_s, w_out):
    b, t, _ = x.shape
    z = x @ w_in
    i1 = Q_RANK
    i2 = i1 + KV_RANK
    i3 = i2 + ROPE_DIM
    i4 = i3 + GM_WIDTH
    q_a, c_kv, k_pe, u, v = jnp.split(z, [i1, i2, i3, i4], axis=-1)
    q = jnp.einsum('btr,rhe->bthe', rms_norm(q_a, q_norm), w_q_b)
    q_nope = q[..., :NOPE_DIM]
    q_pe = rope(q[..., NOPE_DIM:], pos)
    c_kv = rms_norm(c_kv, kv_norm)
    k_pe = rope(k_pe, pos)
    q_lat = jnp.einsum('bthd,rhd->bthr', q_nope, w_kv_b[..., :NOPE_DIM])
    if past_ckv is None:
        ckv_all, kpe_all, k_pos = c_kv, k_pe, pos
    else:
        ckv_all = jnp.concatenate([past_ckv, c_kv], axis=1)
        kpe_all = jnp.concatenate([past_kpe, k_pe], axis=1)
        k_pos = jnp.arange(ckv_all.shape[1], dtype=jnp.int32)
    o_lat = mla_attention(q_lat, q_pe, ckv_all, kpe_all, pos, k_pos)
    attn = jnp.einsum('bthr,rhv->bthv', o_lat, w_kv_b[..., NOPE_DIM:]).reshape(b, t, MLA_HEADS * V_DIM)
    v_n = layer_norm(jax.nn.gelu(v), gm_norm_g, gm_norm_b)
    gate = jax.nn.gelu(u) * chunk_spatial_gate(v_n, gm_w_s, gm_b_s)
    out = jnp.concatenate([attn, gate], axis=-1) @ w_out
    start = ((t - 1) // GM_CHUNK) * GM_CHUNK
    return out, c_kv, k_pe, v_n[:, start:]


def ssm_combine(c1, c2):
    a1, b1 = c1
    a2, b2 = c2
    return a2 * a1, a2 * b1 + b2


def s5_mixer(x, h0, w_in, a_re, a_im, b_re, b_im, c_re, c_im, d, log_dt, w_glu, w_out):
    b, t, _ = x.shape
    u = (x @ w_in).astype(jnp.float32).reshape(b, t, S5_GROUPS, S5_GROUP_DIM)
    lam = lax.complex(a_re.astype(jnp.float32), a_im.astype(jnp.float32))
    lam_dt = lam * jnp.exp(log_dt.astype(jnp.float32))[:, None]
    a_bar = jnp.exp(lam_dt)
    b_c = lax.complex(b_re.astype(jnp.float32), b_im.astype(jnp.float32))
    b_bar = ((a_bar - 1.0) / lam)[..., None] * b_c
    bu = jnp.einsum('gpc,btgc->btgp', b_bar, u.astype(jnp.complex64))
    _, h = lax.associative_scan(ssm_combine, (jnp.broadcast_to(a_bar, bu.shape), bu), axis=1)
    if h0 is not None:
        steps = jnp.arange(1, t + 1, dtype=jnp.float32)
        h = h + jnp.exp(lam_dt[None] * steps[:, None, None])[None] * h0[:, None]
    c_c = lax.complex(c_re.astype(jnp.float32), c_im.astype(jnp.float32))
    y = jnp.real(jnp.einsum('gcp,btgp->btgc', c_c, h)) + d.astype(jnp.float32) * u
    y = jax.nn.gelu(y.reshape(b, t, S5_WIDTH))
    y = y * jax.nn.sigmoid(y @ w_glu.astype(jnp.float32))
    out = y.astype(x.dtype) @ w_out
    return out, h[:, -1]


def sq_relu_ffn(x, w1, w2):
    h = jax.nn.relu(x @ w1)
    return (h * h) @ w2


def setup_inputs(seed: int = 0) -> dict:
    key = jax.random.key(seed)
    ks = iter(jax.random.split(key, 48))
    f32 = jnp.float32

    def nrm(shape, scale):
        return scale * jax.random.normal(next(ks), shape, f32)

    n_pages = PAST_LEN // PAGE_SIZE
    n_used = DEC_BATCH * n_pages
    n_phys = n_used + n_used // 4
    page_table = jax.random.permutation(next(ks), n_phys)[:n_used].reshape(DEC_BATCH, n_pages).astype(jnp.int32)
    return {
        'x_prompt': nrm((BATCH, SEQ, D_MODEL), 1.0),
        'x_sample': nrm((DEC_BATCH, DEC_SEQ, D_MODEL), 1.0),
        'cache_ckv': nrm((N_EVEN, n_phys, PAGE_SIZE, KV_RANK), 1.0),
        'cache_kpe': nrm((N_EVEN, n_phys, PAGE_SIZE, ROPE_DIM), 1.0),
        'state_s5_re': nrm((N_ODD, DEC_BATCH, S5_GROUPS, S5_STATE), 0.1),
        'state_s5_im': nrm((N_ODD, DEC_BATCH, S5_GROUPS, S5_STATE), 0.1),
        'page_table': page_table,
        'w_in_even': nrm((N_EVEN, D_MODEL, EVEN_IN), D_MODEL ** -0.5),
        'mla_q_norm': 1.0 + nrm((N_EVEN, Q_RANK), 0.01),
        'mla_w_q_b': nrm((N_EVEN, Q_RANK, MLA_HEADS, NOPE_DIM + ROPE_DIM), Q_RANK ** -0.5),
        'mla_kv_norm': 1.0 + nrm((N_EVEN, KV_RANK), 0.01),
        'mla_w_kv_b': nrm((N_EVEN, KV_RANK, MLA_HEADS, NOPE_DIM + V_DIM), KV_RANK ** -0.5),
        'gm_norm_g': 1.0 + nrm((N_EVEN, GM_WIDTH), 0.01),
        'gm_norm_b': nrm((N_EVEN, GM_WIDTH), 0.01),
        'gm_w_s': nrm((N_EVEN, GM_HEADS, GM_CHUNK, GM_CHUNK), 0.02),
        'gm_b_s': 1.0 + nrm((N_EVEN, GM_HEADS, GM_CHUNK), 0.01),
        'w_out_even': nrm((N_EVEN, EVEN_MIX, D_MODEL), DN_BETA * EVEN_MIX ** -0.5),
        'w_in_odd': nrm((N_ODD, D_MODEL, S5_WIDTH), D_MODEL ** -0.5),
        's5_a_re': -0.5 * jnp.exp(nrm((N_ODD, S5_GROUPS, S5_STATE), 0.01)),
        's5_a_im': jnp.pi * jnp.arange(S5_STATE, dtype=f32) + nrm((N_ODD, S5_GROUPS, S5_STATE), 0.01),
        's5_b_re': nrm((N_ODD, S5_GROUPS, S5_STATE, S5_GROUP_DIM), (2 * S5_GROUP_DIM) ** -0.5),
        's5_b_im': nrm((N_ODD, S5_GROUPS, S5_STATE, S5_GROUP_DIM), (2 * S5_GROUP_DIM) ** -0.5),
        's5_c_re': nrm((N_ODD, S5_GROUPS, S5_GROUP_DIM, S5_STATE), S5_STATE ** -0.5),
        's5_c_im': nrm((N_ODD, S5_GROUPS, S5_GROUP_DIM, S5_STATE), S5_STATE ** -0.5),
        's5_d': nrm((N_ODD, S5_GROUPS, S5_GROUP_DIM), 1.0),
        's5_log_dt': jax.random.uniform(next(ks), (N_ODD, S5_GROUPS), f32, math.log(DT_MIN), math.log(DT_MAX)),
        's5_w_glu': nrm((N_ODD, S5_WIDTH, S5_WIDTH), S5_WIDTH ** -0.5),
        'w_out_odd': nrm((N_ODD, S5_WIDTH, D_MODEL), DN_BETA * S5_WIDTH ** -0.5),
        'ln_mix_g': 1.0 + nrm((DEPTH, D_MODEL), 0.01),
        'ln_mix_b': nrm((DEPTH, D_MODEL), 0.01),
        'ln_ffn_g': 1.0 + nrm((DEPTH, D_MODEL), 0.01),
        'ln_ffn_b': nrm((DEPTH, D_MODEL), 0.01),
        'ffn_w1': nrm((DEPTH, D_MODEL, FFN_HIDDEN), D_MODEL ** -0.5),
        'ffn_w2': nrm((DEPTH, FFN_HIDDEN, D_MODEL), DN_BETA * FFN_HIDDEN ** -0.5),
    }


def reference(x_prompt, x_sample, cache_ckv, cache_kpe, state_s5_re, state_s5_im, page_table,
              w_in_even, mla_q_norm, mla_w_q_b, mla_kv_norm, mla_w_kv_b,
              gm_norm_g, gm_norm_b, gm_w_s, gm_b_s, w_out_even,
              w_in_odd, s5_a_re, s5_a_im, s5_b_re, s5_b_im, s5_c_re, s5_c_im, s5_d, s5_log_dt,
              s5_w_glu, w_out_odd, ln_mix_g, ln_mix_b, ln_ffn_g, ln_ffn_b, ffn_w1, ffn_w2):
    dec_batch, n_pages = page_table.shape
    pos_p = jnp.arange(x_prompt.shape[1], dtype=jnp.int32)
    pos_s = PAST_LEN + jnp.arange(x_sample.shape[1], dtype=jnp.int32)
    xp, xs = x_prompt, x_sample
    ckv_p, kpe_p, ckv_s, kpe_s, gmv_p, gmv_s = [], [], [], [], [], []
    s5re_p, s5im_p, s5re_s, s5im_s = [], [], [], []
    for layer in range(DEPTH):
        if layer % 2 == 0:
            e = layer // 2
            w = (w_in_even[e], mla_q_norm[e], mla_w_q_b[e], mla_kv_norm[e], mla_w_kv_b[e],
                 gm_norm_g[e], gm_norm_b[e], gm_w_s[e], gm_b_s[e], w_out_even[e])
            past_ckv = cache_ckv[e][page_table].reshape(dec_batch, n_pages * PAGE_SIZE, KV_RANK)
            past_kpe = cache_kpe[e][page_table].reshape(dec_batch, n_pages * PAGE_SIZE, ROPE_DIM)
            fp, c_p, k_p, v_p = even_mixer(xp, pos_p, None, None, *w)
            fs, c_s, k_s, v_s = even_mixer(xs, pos_s, past_ckv, past_kpe, *w)
            ckv_p.append(c_p)
            kpe_p.append(k_p)
            ckv_s.append(c_s)
            kpe_s.append(k_s)
            gmv_p.append(v_p)
            gmv_s.append(v_s)
        else:
            o = layer // 2
            w = (w_in_odd[o], s5_a_re[o], s5_a_im[o], s5_b_re[o], s5_b_im[o], s5_c_re[o], s5_c_im[o],
                 s5_d[o], s5_log_dt[o], s5_w_glu[o], w_out_odd[o])
            h0 = lax.complex(state_s5_re[o].astype(jnp.float32), state_s5_im[o].astype(jnp.float32))
            fp, hp = s5_mixer(xp, None, *w)
            fs, hs = s5_mixer(xs, h0, *w)
            s5re_p.append(jnp.real(hp))
            s5im_p.append(jnp.imag(hp))
            s5re_s.append(jnp.real(hs))
            s5im_s.append(jnp.imag(hs))
        xp = layer_norm(DN_ALPHA * xp + fp, ln_mix_g[layer], ln_mix_b[layer])
        xs = layer_norm(DN_ALPHA * xs + fs, ln_mix_g[layer], ln_mix_b[layer])
        xp = layer_norm(DN_ALPHA * xp + sq_relu_ffn(xp, ffn_w1[layer], ffn_w2[layer]), ln_ffn_g[layer], ln_ffn_b[layer])
        xs = layer_norm(DN_ALPHA * xs + sq_relu_ffn(xs, ffn_w1[layer], ffn_w2[layer]), ln_ffn_g[layer], ln_ffn_b[layer])
    return (xp, xs,
            jnp.stack(ckv_p), jnp.stack(kpe_p), jnp.stack(ckv_s), jnp.stack(kpe_s),
            jnp.stack(gmv_p), jnp.stack(gmv_s),
            jnp.stack(s5re_p), jnp.stack(s5im_p), jnp.stack(s5re_s), jnp.stack(s5im_s))
```

```python
import functools
import math

import jax
import jax.numpy as jnp
from jax import lax
from jax.experimental import pallas as pl
from jax.experimental.pallas import tpu as pltpu

F32 = jnp.float32
BF16 = jnp.bfloat16

D_MODEL = 1024
DEPTH = 2
PAST_LEN = 16384
PAGE_SIZE = 128
MLA_HEADS = 8
NOPE_DIM = 64
ROPE_DIM = 32
ROPE_HALF = ROPE_DIM // 2
V_DIM = 64
Q_RANK = 256
KV_RANK = 128
ROPE_THETA = 10000.0
ATTN_SCALE = 1.0 / math.sqrt(NOPE_DIM + ROPE_DIM)
GM_HEADS = 8
GM_HEAD_DIM = 64
GM_WIDTH = GM_HEADS * GM_HEAD_DIM
GM_CHUNK = 128
S5_GROUP_DIM = 16
S5_GROUPS = 32
S5_STATE = 64
S5_WIDTH = S5_GROUPS * S5_GROUP_DIM
S5_STATES = S5_GROUPS * S5_STATE
FFN_HIDDEN = 4 * D_MODEL
DN_ALPHA = (2 * DEPTH) ** 0.25
LN_EPS = 1e-5
RMS_EPS = 1e-6

LANES = 128
SUBLANES = 8
VMEM_LIMIT_BYTES = 56 * 1024 * 1024

INPROJ_TM = 512
ATTN_TQ = 256
FFN_TM = 512
FFN_HC = 1024
S5_L = 64
DEC_PG = 16
S5_KB = 8
S5_NB = S5_GROUPS // S5_KB
S5_SLABS = 2 * S5_STATES // LANES


def _cparams(sem):
    return pltpu.CompilerParams(dimension_semantics=sem, vmem_limit_bytes=VMEM_LIMIT_BYTES)


def _const_spec(shape):
    n = len(shape)
    return pl.BlockSpec(shape, lambda *_: (0,) * n, pipeline_mode=pl.Buffered(1))


def _layer_norm(x, g, b):
    mu = jnp.mean(x, axis=-1, keepdims=True)
    xc = x - mu
    var = jnp.mean(xc * xc, axis=-1, keepdims=True)
    return xc * lax.rsqrt(var + LN_EPS) * g + b


def _rms_norm(x, g):
    return x * lax.rsqrt(jnp.mean(x * x, axis=-1, keepdims=True) + RMS_EPS) * g


def _rope128(x, cos_t, sin_a, sin_b):
    return (x * cos_t + pltpu.roll(x, ROPE_HALF, 1) * sin_a
            + pltpu.roll(x, LANES - ROPE_HALF, 1) * sin_b)


def _dot(a, b):
    return jnp.dot(a, b, preferred_element_type=F32)


def _dot_nt(a, b):
    return lax.dot_general(a, b, (((1,), (1,)), ((), ())), preferred_element_type=F32)


def _even_inproj_kernel(x_ref, w_ref, kvn_ref, gmg_ref, gmb_ref, ws_ref, bs_ref,
                        cos_ref, sina_ref, sinb_ref,
                        qa_ref, ckv_ref, kpe_ref, kcat_ref, gate_ref, gmv_ref, *, tiles_per_seq):
    tm = x_ref.shape[0]
    z = _dot(x_ref[...].astype(BF16), w_ref[...])
    o_c, o_u, o_v, o_k = Q_RANK, Q_RANK + KV_RANK, Q_RANK + KV_RANK + GM_WIDTH, Q_RANK + KV_RANK + 2 * GM_WIDTH
    qa_ref[...] = z[:, :o_c]
    c_n = _rms_norm(z[:, o_c:o_u], kvn_ref[...])
    ckv_ref[...] = c_n
    kp = _rope128(z[:, o_k:o_k + LANES], cos_ref[...], sina_ref[...], sinb_ref[...])
    kpe_ref[...] = kp[:, :ROPE_DIM]
    kcat_ref[...] = jnp.concatenate([c_n, kp], axis=1).astype(BF16)

    gu = jax.nn.gelu(z[:, o_u:o_v])
    v_n = _layer_norm(jax.nn.gelu(z[:, o_v:o_k]), gmg_ref[...], gmb_ref[...])

    @pl.when(pl.program_id(0) % tiles_per_seq == tiles_per_seq - 1)
    def _():
        gmv_ref[0] = v_n[tm - GM_CHUNK:, :]

    lane = lax.broadcasted_iota(jnp.int32, (GM_CHUNK, LANES), 1)
    lo = lane < GM_HEAD_DIM
    bs = bs_ref[...]
    for ci in range(tm // GM_CHUNK):
        rows = slice(ci * GM_CHUNK, (ci + 1) * GM_CHUNK)
        pieces = []
        for pr in range(GM_WIDTH // LANES):
            r = v_n[rows, pr * LANES:(pr + 1) * LANES]
            m0 = _dot(ws_ref[2 * pr], jnp.where(lo, r, 0.0).astype(BF16))
            m1 = _dot(ws_ref[2 * pr + 1], jnp.where(lo, 0.0, r).astype(BF16))
            pieces.append(m0 + m1)
        mixed = jnp.concatenate(pieces, axis=1) + bs
        gate_ref[rows, :] = (gu[rows, :] * mixed).astype(BF16)


def _even_inproj(x2d, w_in_p, kv_norm, gm_g, gm_b, ws_m, bs_full, cos_t, sin_a, sin_b, batch, seq):
    n_tok = x2d.shape[0]
    tm = INPROJ_TM
    tiles_per_seq = seq // tm
    row = lambda w: pl.BlockSpec((tm, w), lambda i: (i, 0))
    tab = pl.BlockSpec((tm, LANES), lambda i: (i % tiles_per_seq, 0))
    n_in = w_in_p.shape[1]
    return pl.pallas_call(
        functools.partial(_even_inproj_kernel, tiles_per_seq=tiles_per_seq),
        grid=(n_tok // tm,),
        in_specs=[row(D_MODEL), _const_spec((D_MODEL, n_in)), _const_spec((1, KV_RANK)),
                  _const_spec((1, GM_WIDTH)), _const_spec((1, GM_WIDTH)),
                  _const_spec((GM_HEADS, GM_CHUNK, GM_CHUNK)), _const_spec((GM_CHUNK, GM_WIDTH)),
                  tab, tab, tab],
        out_specs=[row(Q_RANK), row(KV_RANK), row(ROPE_DIM), row(2 * LANES), row(GM_WIDTH),
                   pl.BlockSpec((1, GM_CHUNK, GM_WIDTH), lambda i: (i // tiles_per_seq, 0, 0))],
        out_shape=[jax.ShapeDtypeStruct((n_tok, Q_RANK), F32),
                   jax.ShapeDtypeStruct((n_tok, KV_RANK), F32),
                   jax.ShapeDtypeStruct((n_tok, ROPE_DIM), F32),
                   jax.ShapeDtypeStruct((n_tok, 2 * LANES), BF16),
                   jax.ShapeDtypeStruct((n_tok, GM_WIDTH), BF16),
                   jax.ShapeDtypeStruct((batch, GM_CHUNK, GM_WIDTH), F32)],
        compiler_params=_cparams(("arbitrary",)),
        name="even_inproj",
    )(x2d, w_in_p, kv_norm, gm_g, gm_b, ws_m, bs_full, cos_t, sin_a, sin_b)


def _queries(q_a, qn_g, wq_nope, wq_pe, w_uk, cos_t, sin_a, sin_b):
    qn = _rms_norm(q_a, qn_g).astype(BF16)
    nope = _dot(qn, wq_nope).astype(BF16)
    lat = _dot(nope, w_uk)
    pe = _dot(qn, wq_pe)
    pes = [_rope128(pe[:, h * LANES:(h + 1) * LANES], cos_t, sin_a, sin_b) for h in range(MLA_HEADS)]
    return lat, pes


def _mla_attn_kernel(qa_ref, kcat_ref, qn_ref, wqn_ref, wqp_ref, wuk_ref, wuv_ref,
                     cos_ref, sina_ref, sinb_ref, out_ref, qcat_s, m_s, l_s, acc_s):
    tq = qa_ref.shape[0]
    i = pl.program_id(1)
    lat, pes = _queries(qa_ref[...], qn_ref[...], wqn_ref[...], wqp_ref[...], wuk_ref[...],
                        cos_ref[...], sina_ref[...], sinb_ref[...])
    for h in range(MLA_HEADS):
        qcat_s[h * tq:(h + 1) * tq, :] = jnp.concatenate(
            [lat[:, h * LANES:(h + 1) * LANES], pes[h]], axis=1).astype(BF16)
    m_s[...] = jnp.full(m_s.shape, -jnp.inf, F32)
    l_s[...] = jnp.zeros(l_s.shape, F32)
    acc_s[...] = jnp.zeros(acc_s.shape, F32)

    row_id = lax.broadcasted_iota(jnp.int32, (tq, tq), 0)
    col_id = lax.broadcasted_iota(jnp.int32, (tq, tq), 1)
    causal = col_id <= row_id

    def tile(j, masked):
        k = kcat_ref[pl.ds(pl.multiple_of(j * tq, tq), tq), :]
        v = k[:, :KV_RANK]
        for h in range(MLA_HEADS):
            rows = slice(h * tq, (h + 1) * tq)
            s = _dot_nt(qcat_s[rows, :], k) * ATTN_SCALE
            if masked:
                s = jnp.where(causal, s, -jnp.inf)
            m_old = m_s[rows, :]
            m_new = jnp.maximum(m_old, jnp.max(s, axis=-1, keepdims=True))
            alpha = jnp.exp(m_old - m_new)
            p = jnp.exp(s - m_new)
            l_s[rows, :] = alpha * l_s[rows, :] + jnp.sum(p, axis=-1, keepdims=True)
            acc_s[rows, :] = alpha * acc_s[rows, :] + _dot(p.astype(BF16), v)
            m_s[rows, :] = m_new

    tile(i, True)

    def body(j, carry):
        tile(j, False)
        return carry

    lax.fori_loop(0, i, body, 0)

    o_all = jnp.concatenate(
        [acc_s[h * tq:(h + 1) * tq, :] / l_s[h * tq:(h + 1) * tq, :] for h in range(MLA_HEADS)],
        axis=1).astype(BF16)
    out_ref[...] = _dot(o_all, wuv_ref[...]).astype(BF16)


def _mla_attn(q_a, kcat3, qn_g, wq_nope, wq_pe, w_uk, w_uv, cos_t, sin_a, sin_b):
    batch, seq, _ = kcat3.shape
    tq = ATTN_TQ
    nq = seq // tq
    tab = pl.BlockSpec((tq, LANES), lambda b, i: (i, 0))
    return pl.pallas_call(
        _mla_attn_kernel,
        grid=(batch, nq),
        in_specs=[pl.BlockSpec((tq, Q_RANK), lambda b, i: (b * nq + i, 0)),
                  pl.BlockSpec((None, seq, 2 * LANES), lambda b, i: (b, 0, 0)),
                  _const_spec((1, Q_RANK)), _const_spec(wq_nope.shape), _const_spec(wq_pe.shape),
                  _const_spec(w_uk.shape), _const_spec(w_uv.shape), tab, tab, tab],
        out_specs=pl.BlockSpec((tq, MLA_HEADS * V_DIM), lambda b, i: (b * nq + i, 0)),
        out_shape=jax.ShapeDtypeStruct((batch * seq, MLA_HEADS * V_DIM), BF16),
        scratch_shapes=[pltpu.VMEM((MLA_HEADS * tq, 2 * LANES), BF16),
                        pltpu.VMEM((MLA_HEADS * tq, 1), F32),
                        pltpu.VMEM((MLA_HEADS * tq, 1), F32),
                        pltpu.VMEM((MLA_HEADS * tq, KV_RANK), F32)],
        compiler_params=_cparams(("arbitrary", "arbitrary")),
        name="mla_attn",
    )(q_a, kcat3, qn_g, wq_nope, wq_pe, w_uk, w_uv, cos_t, sin_a, sin_b)


def _mix_ffn_kernel(*refs, n_act, has_pre):
    x_ref = refs[0]
    act_refs = refs[1:1 + n_act]
    k = 1 + n_act
    pre_ref = refs[k] if has_pre else None
    k += int(has_pre)
    wo_refs = refs[k:k + n_act]
    k += n_act
    g1_ref, b1_ref, g2_ref, b2_ref, w1_ref, w2_ref, out_ref = refs[k:k + 7]

    x = x_ref[...]
    f = None
    for idx in range(n_act):
        a = act_refs[idx][...]
        if has_pre and idx == 0:
            a = _dot(a.astype(BF16), pre_ref[...])
        part = _dot(a.astype(BF16), wo_refs[idx][...])
        f = part if f is None else f + part
    x1 = _layer_norm(DN_ALPHA * x + f, g1_ref[...], b1_ref[...])
    x1b = x1.astype(BF16)
    acc = None
    for c in range(FFN_HIDDEN // FFN_HC):
        cols = slice(c * FFN_HC, (c + 1) * FFN_HC)
        h = jnp.maximum(_dot(x1b, w1_ref[:, cols]), 0.0)
        part = _dot((h * h).astype(BF16), w2_ref[cols, :])
        acc = part if acc is None else acc + part
    out_ref[...] = _layer_norm(DN_ALPHA * x1 + acc, g2_ref[...], b2_ref[...])


def _mix_ffn(x2d, acts, pre_w, wos, ln1_g, ln1_b, ln2_g, ln2_b, w1, w2, name):
    n_tok = x2d.shape[0]
    tm = min(FFN_TM, n_tok)
    row = lambda w: pl.BlockSpec((tm, w), lambda i: (i, 0))
    has_pre = pre_w is not None
    in_specs = [row(D_MODEL)] + [row(a.shape[1]) for a in acts]
    args = [x2d] + list(acts)
    if has_pre:
        in_specs.append(_const_spec(pre_w.shape))
        args.append(pre_w)
    in_specs += [_const_spec(w.shape) for w in wos]
    args += list(wos)
    in_specs += [_const_spec((1, D_MODEL))] * 4 + [_const_spec(w1.shape), _const_spec(w2.shape)]
    args += [ln1_g, ln1_b, ln2_g, ln2_b, w1, w2]
    return pl.pallas_call(
        functools.partial(_mix_ffn_kernel, n_act=len(acts), has_pre=has_pre),
        grid=(n_tok // tm,),
        in_specs=in_specs,
        out_specs=row(D_MODEL),
        out_shape=jax.ShapeDtypeStruct((n_tok, D_MODEL), F32),
        compiler_params=_cparams(("arbitrary",)),
        name=name,
    )(*args)


def _even_inproj_sample_kernel(x_ref, w_ref, kvn_ref, gmg_ref, gmb_ref, ws0_ref, bs0_ref,
                               cos_ref, sina_ref, sinb_ref, qn_ref, wqn_ref, wqp_ref, wuk_ref,
                               ckv_ref, kpe_ref, gate_ref, vn_ref, qlat_ref, qpe_ref):
    z = _dot(x_ref[...].astype(BF16), w_ref[...])
    o_c, o_u, o_v, o_k = Q_RANK, Q_RANK + KV_RANK, Q_RANK + KV_RANK + GM_WIDTH, Q_RANK + KV_RANK + 2 * GM_WIDTH
    cos_t, sin_a, sin_b = cos_ref[...], sina_ref[...], sinb_ref[...]
    c_n = _rms_norm(z[:, o_c:o_u], kvn_ref[...])
    ckv_ref[...] = c_n
    kp = _rope128(z[:, o_k:o_k + LANES], cos_t, sin_a, sin_b)
    kpe_ref[...] = kp[:, :ROPE_DIM]
    gu = jax.nn.gelu(z[:, o_u:o_v])
    v_n = _layer_norm(jax.nn.gelu(z[:, o_v:o_k]), gmg_ref[...], gmb_ref[...])
    vn_ref[...] = v_n
    mixed = ws0_ref[...].astype(F32) * v_n.astype(BF16).astype(F32) + bs0_ref[...]
    gate_ref[...] = (gu * mixed).astype(BF16)
    lat, pes = _queries(z[:, :o_c], qn_ref[...], wqn_ref[...], wqp_ref[...], wuk_ref[...],
                        cos_t, sin_a, sin_b)
    qlat_ref[...] = lat
    qpe_ref[...] = jnp.concatenate(pes, axis=1)


def _even_inproj_sample(xs, w_in_p, kv_norm, gm_g, gm_b, ws0, bs0, cos_t, sin_a, sin_b,
                        qn_g, wq_nope, wq_pe, w_uk):
    n = xs.shape[0]
    args = (xs, w_in_p, kv_norm, gm_g, gm_b, ws0, bs0, cos_t, sin_a, sin_b, qn_g, wq_nope, wq_pe, w_uk)
    full = lambda shape: pl.BlockSpec(shape, lambda i: (0,) * len(shape))
    widths = [(KV_RANK, F32), (ROPE_DIM, F32), (GM_WIDTH, BF16), (GM_WIDTH, F32),
              (MLA_HEADS * LANES, F32), (MLA_HEADS * LANES, F32)]
    return pl.pallas_call(
        _even_inproj_sample_kernel,
        grid=(1,),
        in_specs=[full(a.shape) for a in args],
        out_specs=[full((n, w)) for w, _ in widths],
        out_shape=[jax.ShapeDtypeStruct((n, w), dt) for w, dt in widths],
        compiler_params=_cparams(("arbitrary",)),
        name="even_inproj_sample",
    )(*args)


def _decode_attn_kernel(pt_ref, qlat_ref, qpe_ref, ckvs_ref, kpes_ref, *rest, n_pg):
    ckv_refs = rest[:n_pg]
    kpe_refs = rest[n_pg:2 * n_pg]
    out_ref, m_s, l_s, acc_s = rest[2 * n_pg:]
    g = pl.program_id(1)

    @pl.when(g == 0)
    def _():
        m_s[...] = jnp.full(m_s.shape, -jnp.inf, F32)
        l_s[...] = jnp.zeros(l_s.shape, F32)
        acc_s[...] = jnp.zeros(acc_s.shape, F32)

    qlat = qlat_ref[...].astype(BF16)
    qpe = qpe_ref[...][:, :ROPE_DIM].astype(BF16)
    cks, ss = [], []
    for k in range(n_pg):
        ck = ckv_refs[k][...].astype(BF16)
        kp = kpe_refs[k][...].astype(BF16)
        cks.append(ck)
        ss.append(_dot_nt(qlat, ck) + _dot_nt(qpe, kp))
    s = jnp.concatenate(ss, axis=1) * ATTN_SCALE
    m_old = m_s[...]
    m_new = jnp.maximum(m_old, jnp.max(s, axis=-1, keepdims=True))
    alpha = jnp.exp(m_old - m_new)
    p = jnp.exp(s - m_new)
    l_s[...] = alpha * l_s[...] + jnp.sum(p, axis=-1, keepdims=True)
    pb = p.astype(BF16)
    pv = None
    for k in range(n_pg):
        part = _dot(pb[:, k * PAGE_SIZE:(k + 1) * PAGE_SIZE], cks[k])
        pv = part if pv is None else pv + part
    acc_s[...] = alpha * acc_s[...] + pv
    m_s[...] = m_new

    @pl.when(g == pl.num_programs(1) - 1)
    def _():
        r = lambda a: a.astype(BF16).astype(F32)
        kv = r(ckvs_ref[...])
        s_self = (jnp.sum(r(qlat_ref[...]) * kv, axis=-1, keepdims=True)
                  + jnp.sum(r(qpe_ref[...]) * r(kpes_ref[...]), axis=-1, keepdims=True)) * ATTN_SCALE
        m_o = m_s[...]
        m_n = jnp.maximum(m_o, s_self)
        a = jnp.exp(m_o - m_n)
        p_self = jnp.exp(s_self - m_n)
        l_f = a * l_s[...] + p_self
        out_ref[...] = (a * acc_s[...] + r(p_self) * kv) / l_f


def _decode_attn(page_table, qlat2, qpe2, ckv_s3, kpe_s3, cache_ckv_e, cache_kpe_e):
    dec_batch, n_pages = page_table.shape
    n_pg = DEC_PG
    qspec = pl.BlockSpec((MLA_HEADS, LANES), lambda b, g, pt: (b, 0))
    self_spec = pl.BlockSpec((None, 1, LANES), lambda b, g, pt: (b, 0, 0))

    def page_spec(width, k):
        return pl.BlockSpec((None, PAGE_SIZE, width), lambda b, g, pt: (pt[b, g * n_pg + k], 0, 0))

    grid_spec = pltpu.PrefetchScalarGridSpec(
        num_scalar_prefetch=1,
        grid=(dec_batch, n_pages // n_pg),
        in_specs=[qspec, qspec, self_spec, self_spec]
                 + [page_spec(KV_RANK, k) for k in range(n_pg)]
                 + [page_spec(ROPE_DIM, k) for k in range(n_pg)],
        out_specs=qspec,
        scratch_shapes=[pltpu.VMEM((MLA_HEADS, 1), F32), pltpu.VMEM((MLA_HEADS, 1), F32),
                        pltpu.VMEM((MLA_HEADS, KV_RANK), F32)],
    )
    return pl.pallas_call(
        functools.partial(_decode_attn_kernel, n_pg=n_pg),
        grid_spec=grid_spec,
        out_shape=jax.ShapeDtypeStruct((dec_batch * MLA_HEADS, KV_RANK), F32),
        compiler_params=_cparams(("arbitrary", "arbitrary")),
        name="decode_attn",
    )(page_table, qlat2, qpe2, ckv_s3, kpe_s3, *([cache_ckv_e] * n_pg), *([cache_kpe_e] * n_pg))


def _s5_readout(hcat_blocks, u, cw_ref, d_ref, wglu_ref):
    y = jnp.concatenate([_dot(hcat_blocks[k], cw_ref[k]) for k in range(S5_NB)], axis=1)
    y = jax.nn.gelu(y + d_ref[...] * u)
    return (y * jax.nn.sigmoid(_dot(y.astype(BF16), wglu_ref[...]))).astype(BF16)


def _s5_prompt_kernel(x_ref, win_ref, bw_ref, a_ref, cw_ref, d_ref, wglu_ref,
                      yg_ref, hre_ref, him_ref, u_s, bu_s, h_s):
    n_b, seg, _ = x_ref.shape
    half = S5_SLABS // 2

    @pl.when(pl.program_id(0) == 0)
    def _():
        h_s[...] = jnp.zeros(h_s.shape, F32)

    for b in range(n_b):
        u = _dot(x_ref[b].astype(BF16), win_ref[...])
        u_s[b] = u
        ub = u.astype(BF16)
        for k in range(S5_NB):
            r = _dot(ub[:, k * LANES:(k + 1) * LANES], bw_ref[k])
            for q in range(4):
                bu_s[4 * k + q, pl.ds(b, seg, stride=n_b), :] = r[:, q * LANES:(q + 1) * LANES]
                bu_s[half + 4 * k + q, pl.ds(b, seg, stride=n_b), :] = r[:, (4 + q) * LANES:(5 + q) * LANES]

    def step(t, h):
        rows = pl.ds(pl.multiple_of(t * n_b, n_b), n_b)
        hs = bu_s[:, rows, :]
        ar, ai = a_ref[:half], a_ref[half:]
        hr, hi = h[:half], h[half:]
        hn = jnp.concatenate([ar * hr - ai * hi + hs[:half], ar * hi + ai * hr + hs[half:]], axis=0)
        bu_s[:, rows, :] = hn
        return hn

    h_fin = lax.fori_loop(0, seg, step, h_s[...])
    h_s[...] = h_fin
    hre_ref[...] = jnp.concatenate([h_fin[s] for s in range(half)], axis=1)
    him_ref[...] = jnp.concatenate([h_fin[half + s] for s in range(half)], axis=1)

    for b in range(n_b):
        blocks = []
        for k in range(S5_NB):
            slabs = [bu_s[4 * k + q, pl.ds(b, seg, stride=n_b), :] for q in range(4)]
            slabs += [bu_s[half + 4 * k + q, pl.ds(b, seg, stride=n_b), :] for q in range(4)]
            blocks.append(jnp.concatenate(slabs, axis=1).astype(BF16))
        yg_ref[b] = _s5_readout(blocks, u_s[b], cw_ref, d_ref, wglu_ref)


def _s5_prompt(x3, w_in, bw, a_b, cw, d_row, w_glu):
    batch, seq, _ = x3.shape
    seg = S5_L
    return pl.pallas_call(
        _s5_prompt_kernel,
        grid=(seq // seg,),
        in_specs=[pl.BlockSpec((batch, seg, D_MODEL), lambda c: (0, c, 0)),
                  _const_spec(w_in.shape), _const_spec(bw.shape), _const_spec(a_b.shape),
                  _const_spec(cw.shape), _const_spec(d_row.shape), _const_spec(w_glu.shape)],
        out_specs=[pl.BlockSpec((batch, seg, S5_WIDTH), lambda c: (0, c, 0)),
                   pl.BlockSpec((batch, S5_STATES), lambda c: (0, 0)),
                   pl.BlockSpec((batch, S5_STATES), lambda c: (0, 0))],
        out_shape=[jax.ShapeDtypeStruct((batch, seq, S5_WIDTH), BF16),
                   jax.ShapeDtypeStruct((batch, S5_STATES), F32),
                   jax.ShapeDtypeStruct((batch, S5_STATES), F32)],
        scratch_shapes=[pltpu.VMEM((batch, seg, S5_WIDTH), F32),
                        pltpu.VMEM((S5_SLABS, seg * batch, LANES), F32),
                        pltpu.VMEM((S5_SLABS, batch, LANES), F32)],
        compiler_params=_cparams(("arbitrary",)),
        name="s5_prompt",
    )(x3, w_in, bw, a_b, cw, d_row, w_glu)


def _s5_sample_kernel(x_ref, h0r_ref, h0i_ref, win_ref, bw_ref, ar_ref, ai_ref, cw_ref, d_ref, wglu_ref,
                      yg_ref, hre_ref, him_ref):
    u = _dot(x_ref[...].astype(BF16), win_ref[...])
    ub = u.astype(BF16)
    bu = [_dot(ub[:, k * LANES:(k + 1) * LANES], bw_ref[k]) for k in range(S5_NB)]
    w = S5_KB * S5_STATE
    bu_r = jnp.concatenate([r[:, :w] for r in bu], axis=1)
    bu_i = jnp.concatenate([r[:, w:] for r in bu], axis=1)
    ar, ai, h0r, h0i = ar_ref[...], ai_ref[...], h0r_ref[...], h0i_ref[...]
    hr = bu_r + (ar * h0r - ai * h0i)
    hi = bu_i + (ar * h0i + ai * h0r)
    hre_ref[...] = hr
    him_ref[...] = hi
    blocks = [jnp.concatenate([hr[:, k * w:(k + 1) * w], hi[:, k * w:(k + 1) * w]], axis=1).astype(BF16)
              for k in range(S5_NB)]
    yg_ref[...] = _s5_readout(blocks, u, cw_ref, d_ref, wglu_ref)


def _s5_sample(xs, h0r, h0i, w_in, bw, a_r, a_i, cw, d_row, w_glu):
    n = xs.shape[0]
    args = (xs, h0r, h0i, w_in, bw, a_r, a_i, cw, d_row, w_glu)
    full = lambda shape: pl.BlockSpec(shape, lambda i: (0,) * len(shape))
    return pl.pallas_call(
        _s5_sample_kernel,
        grid=(1,),
        in_specs=[full(a.shape) for a in args],
        out_specs=[full((n, S5_WIDTH)), full((n, S5_STATES)), full((n, S5_STATES))],
        out_shape=[jax.ShapeDtypeStruct((n, S5_WIDTH), BF16),
                   jax.ShapeDtypeStruct((n, S5_STATES), F32),
                   jax.ShapeDtypeStruct((n, S5_STATES), F32)],
        compiler_params=_cparams(("arbitrary",)),
        name="s5_sample",
    )(*args)


def _rope_tables(pos):
    freqs = ROPE_THETA ** (-jnp.arange(ROPE_HALF, dtype=F32) / ROPE_HALF)
    ang = pos.astype(F32)[:, None] * freqs
    cos, sin = jnp.cos(ang), jnp.sin(ang)
    zero = jnp.zeros_like(cos)
    pad = jnp.zeros((pos.shape[0], LANES - ROPE_DIM), F32)
    cos_t = jnp.concatenate([cos, cos, pad], axis=1)
    sin_a = jnp.concatenate([zero, sin, pad], axis=1)
    sin_b = jnp.concatenate([-sin, zero, pad], axis=1)
    return cos_t, sin_a, sin_b


def _block_diag(blocks):
    n, r, c = blocks.shape
    eye = jnp.eye(n, dtype=blocks.dtype)
    return jnp.einsum('nrc,nm->nrmc', blocks, eye).reshape(n * r, n * c)


def _even_params(w_in, q_norm, w_q_b, kv_norm, w_kv_b, gm_g, gm_b, gm_w_s, gm_b_s, w_out):
    i1, i2, i3, i4 = Q_RANK, Q_RANK + KV_RANK, Q_RANK + KV_RANK + ROPE_DIM, Q_RANK + KV_RANK + ROPE_DIM + GM_WIDTH
    w_in_p = jnp.concatenate([w_in[:, :i2], w_in[:, i3:], w_in[:, i2:i3],
                              jnp.zeros((D_MODEL, LANES - ROPE_DIM), F32)], axis=1).astype(BF16)
    wq_nope = w_q_b[:, :, :NOPE_DIM].reshape(Q_RANK, MLA_HEADS * NOPE_DIM).astype(BF16)
    pe = w_q_b[:, :, NOPE_DIM:]
    wq_pe = jnp.concatenate([pe, jnp.zeros((Q_RANK, MLA_HEADS, LANES - ROPE_DIM), F32)],
                            axis=2).reshape(Q_RANK, MLA_HEADS * LANES).astype(BF16)
    w_uk = _block_diag(jnp.transpose(w_kv_b[:, :, :NOPE_DIM], (1, 2, 0))).astype(BF16)
    w_uv = _block_diag(jnp.transpose(w_kv_b[:, :, NOPE_DIM:], (1, 0, 2))).astype(BF16)
    causal = jnp.tril(jnp.ones((GM_CHUNK, GM_CHUNK), dtype=bool))
    ws_m = jnp.where(causal[None], gm_w_s, 0).astype(BF16)
    bs_full = jnp.repeat(gm_b_s.T, GM_HEAD_DIM, axis=1)
    ws0 = jnp.repeat(gm_w_s[:, 0, 0], GM_HEAD_DIM)[None, :].astype(BF16)
    bs0 = bs_full[:1]
    n_attn = MLA_HEADS * V_DIM
    return dict(w_in_p=w_in_p, qn=q_norm[None, :], wq_nope=wq_nope, wq_pe=wq_pe, w_uk=w_uk, w_uv=w_uv,
                kvn=kv_norm[None, :], gm_g=gm_g[None, :], gm_b=gm_b[None, :], ws_m=ws_m, bs_full=bs_full,
                ws0=ws0, bs0=bs0, wo_attn=w_out[:n_attn].astype(BF16), wo_gate=w_out[n_attn:].astype(BF16))


def _s5_params(w_in, a_re, a_im, b_re, b_im, c_re, c_im, d, log_dt, w_glu, w_out, batch):
    dt = jnp.exp(log_dt)[:, None]
    ld_r, ld_i = a_re * dt, a_im * dt
    mag = jnp.exp(ld_r)
    ab_r, ab_i = mag * jnp.cos(ld_i), mag * jnp.sin(ld_i)
    den = a_re * a_re + a_im * a_im
    cr = ((ab_r - 1.0) * a_re + ab_i * a_im) / den
    ci = (ab_i * a_re - (ab_r - 1.0) * a_im) / den
    bb_r = cr[..., None] * b_re - ci[..., None] * b_im
    bb_i = cr[..., None] * b_im + ci[..., None] * b_re

    def in_blocks(bb):
        return jnp.stack([_block_diag(jnp.transpose(bb[k * S5_KB:(k + 1) * S5_KB], (0, 2, 1)))
                          for k in range(S5_NB)])

    def out_blocks(cc):
        return jnp.stack([_block_diag(jnp.transpose(cc[k * S5_KB:(k + 1) * S5_KB], (0, 2, 1)))
                          for k in range(S5_NB)])

    bw = jnp.concatenate([in_blocks(bb_r), in_blocks(bb_i)], axis=2).astype(BF16)
    cw = jnp.concatenate([out_blocks(c_re), out_blocks(-c_im)], axis=1).astype(BF16)
    a_r, a_i = ab_r.reshape(1, S5_STATES), ab_i.reshape(1, S5_STATES)
    half = S5_SLABS // 2
    a_b = jnp.concatenate([jnp.broadcast_to(a_r.reshape(half, 1, LANES), (half, batch, LANES)),
                           jnp.broadcast_to(a_i.reshape(half, 1, LANES), (half, batch, LANES))], axis=0)
    return dict(w_in=w_in.astype(BF16), bw=bw, cw=cw, a_r=a_r, a_i=a_i, a_b=a_b,
                d=d.reshape(1, S5_WIDTH), w_glu=w_glu.astype(BF16), w_out=w_out.astype(BF16))


def kernel(x_prompt, x_sample, cache_ckv, cache_kpe, state_s5_re, state_s5_im, page_table, w_in_even, mla_q_norm, mla_w_q_b, mla_kv_norm, mla_w_kv_b, gm_norm_g, gm_norm_b, gm_w_s, gm_b_s, w_out_even, w_in_odd, s5_a_re, s5_a_im, s5_b_re, s5_b_im, s5_c_re, s5_c_im, s5_d, s5_log_dt, s5_w_glu, w_out_odd, ln_mix_g, ln_mix_b, ln_ffn_g, ln_ffn_b, ffn_w1, ffn_w2):
    batch, seq, _ = x_prompt.shape
    dec_batch, dec_seq, _ = x_sample.shape
    assert dec_seq == 1 and seq % INPROJ_TM == 0 and seq % ATTN_TQ == 0 and seq % S5_L == 0
    assert (batch * seq) % FFN_TM == 0 and page_table.shape[1] % DEC_PG == 0
    assert batch == SUBLANES

    xp = x_prompt.reshape(batch * seq, D_MODEL)
    xs = x_sample.reshape(dec_batch, D_MODEL)
    tabs_p = _rope_tables(jnp.arange(seq, dtype=jnp.int32))
    tabs_s = _rope_tables(PAST_LEN + jnp.arange(dec_seq, dtype=jnp.int32))

    outs = {k: [] for k in ("ckv_p", "kpe_p", "ckv_s", "kpe_s", "gmv_p", "gmv_s",
                            "s5re_p", "s5im_p", "s5re_s", "s5im_s")}
    for layer in range(DEPTH):
        ln = (ln_mix_g[layer][None, :], ln_mix_b[layer][None, :],
              ln_ffn_g[layer][None, :], ln_ffn_b[layer][None, :],
              ffn_w1[layer].astype(BF16), ffn_w2[layer].astype(BF16))
        if layer % 2 == 0:
            e = layer // 2
            p = _even_params(w_in_even[e], mla_q_norm[e], mla_w_q_b[e], mla_kv_norm[e], mla_w_kv_b[e],
                             gm_norm_g[e], gm_norm_b[e], gm_w_s[e], gm_b_s[e], w_out_even[e])
            q_a, ckv, kpe, kcat, gate, gmv = _even_inproj(
                xp, p["w_in_p"], p["kvn"], p["gm_g"], p["gm_b"], p["ws_m"], p["bs_full"], *tabs_p, batch, seq)
            attn = _mla_attn(q_a, kcat.reshape(batch, seq, 2 * LANES), p["qn"], p["wq_nope"], p["wq_pe"],
                             p["w_uk"], p["w_uv"], *tabs_p)
            xp = _mix_ffn(xp, [attn, gate], None, [p["wo_attn"], p["wo_gate"]], *ln, name="even_ffn_prompt")
            outs["ckv_p"].append(ckv.reshape(batch, seq, KV_RANK))
            outs["kpe_p"].append(kpe.reshape(batch, seq, ROPE_DIM))
            outs["gmv_p"].append(gmv)
            ckv_s, kpe_s, gate_s, vn_s, qlat_s, qpe_s = _even_inproj_sample(
                xs, p["w_in_p"], p["kvn"], p["gm_g"], p["gm_b"], p["ws0"], p["bs0"], *tabs_s,
                p["qn"], p["wq_nope"], p["wq_pe"], p["w_uk"])
            kpe_pad = jnp.concatenate([kpe_s, jnp.zeros((dec_batch, LANES - ROPE_DIM), F32)], axis=1)
            o_lat = _decode_attn(page_table,
                                 qlat_s.reshape(dec_batch * MLA_HEADS, LANES),
                                 qpe_s.reshape(dec_batch * MLA_HEADS, LANES),
                                 ckv_s.reshape(dec_batch, 1, KV_RANK), kpe_pad.reshape(dec_batch, 1, LANES),
                                 cache_ckv[e], cache_kpe[e])
            xs = _mix_ffn(xs, [o_lat.reshape(dec_batch, MLA_HEADS * KV_RANK), gate_s], p["w_uv"],
                          [p["wo_attn"], p["wo_gate"]], *ln, name="even_ffn_sample")
            outs["ckv_s"].append(ckv_s.reshape(dec_batch, 1, KV_RANK))
            outs["kpe_s"].append(kpe_s.reshape(dec_batch, 1, ROPE_DIM))
            outs["gmv_s"].append(vn_s.reshape(dec_batch, 1, GM_WIDTH))
        else:
            o = layer // 2
            p = _s5_params(w_in_odd[o], s5_a_re[o], s5_a_im[o], s5_b_re[o], s5_b_im[o], s5_c_re[o],
                           s5_c_im[o], s5_d[o], s5_log_dt[o], s5_w_glu[o], w_out_odd[o], batch)
            yg, hre, him = _s5_prompt(xp.reshape(batch, seq, D_MODEL), p["w_in"], p["bw"], p["a_b"],
                                      p["cw"], p["d"], p["w_glu"])
            xp = _mix_ffn(xp, [yg.reshape(batch * seq, S5_WIDTH)], None, [p["w_out"]], *ln,
                          name="odd_ffn_prompt")
            outs["s5re_p"].append(hre.reshape(batch, S5_GROUPS, S5_STATE))
            outs["s5im_p"].append(him.reshape(batch, S5_GROUPS, S5_STATE))
            yg_s, hre_s, him_s = _s5_sample(
                xs, state_s5_re[o].reshape(dec_batch, S5_STATES), state_s5_im[o].reshape(dec_batch, S5_STATES),
                p["w_in"], p["bw"], p["a_r"], p["a_i"], p["cw"], p["d"], p["w_glu"])
            xs = _mix_ffn(xs, [yg_s], None, [p["w_out"]], *ln, name="odd_ffn_sample")
            outs["s5re_s"].append(hre_s.reshape(dec_batch, S5_GROUPS, S5_STATE))
            outs["s5im_s"].append(him_s.reshape(dec_batch, S5_GROUPS, S5_STATE))

    st = jnp.stack
    return (xp.reshape(batch, seq, D_MODEL), xs.reshape(dec_batch, dec_seq, D_MODEL),
            st(outs["ckv_p"]), st(outs["kpe_p"]), st(outs["ckv_s"]), st(outs["kpe_s"]),
            st(outs["gmv_p"]), st(outs["gmv_s"]),
            st(outs["s5re_p"]), st(outs["s5im_p"]), st(outs["s5re_s"]), st(outs["s5im_s"]))
```

```python
import functools
import math

import jax
import jax.numpy as jnp
from jax import lax
from jax.experimental import pallas as pl
from jax.experimental.pallas import tpu as pltpu

F32 = jnp.float32
BF16 = jnp.bfloat16

D_MODEL = 1024
DEPTH = 2
PAST_LEN = 16384
PAGE_SIZE = 128
MLA_HEADS = 8
NOPE_DIM = 64
ROPE_DIM = 32
ROPE_HALF = ROPE_DIM // 2
V_DIM = 64
Q_RANK = 256
KV_RANK = 128
ROPE_THETA = 10000.0
ATTN_SCALE = 1.0 / math.sqrt(NOPE_DIM + ROPE_DIM)
GM_HEADS = 8
GM_HEAD_DIM = 64
GM_WIDTH = GM_HEADS * GM_HEAD_DIM
GM_CHUNK = 128
S5_GROUP_DIM = 16
S5_GROUPS = 32
S5_STATE = 64
S5_WIDTH = S5_GROUPS * S5_GROUP_DIM
S5_STATES = S5_GROUPS * S5_STATE
FFN_HIDDEN = 4 * D_MODEL
DN_ALPHA = (2 * DEPTH) ** 0.25
LN_EPS = 1e-5
RMS_EPS = 1e-6

LANES = 128
SUBLANES = 8
VMEM_LIMIT_BYTES = 56 * 1024 * 1024

INPROJ_TM = 512
ATTN_TQ = 256
FFN_TM = 512
FFN_HC = 1024
S5_L = 64
DEC_PG = 16
S5_KB = 8
S5_NB = S5_GROUPS // S5_KB
S5_SLABS = 2 * S5_STATES // LANES


def _cparams(sem):
    return pltpu.CompilerParams(dimension_semantics=sem, vmem_limit_bytes=VMEM_LIMIT_BYTES)


def _const_spec(shape):
    n = len(shape)
    return pl.BlockSpec(shape, lambda *_: (0,) * n, pipeline_mode=pl.Buffered(1))


def _layer_norm(x, g, b):
    mu = jnp.mean(x, axis=-1, keepdims=True)
    xc = x - mu
    var = jnp.mean(xc * xc, axis=-1, keepdims=True)
    return xc * lax.rsqrt(var + LN_EPS) * g + b


def _rms_norm(x, g):
    return x * lax.rsqrt(jnp.mean(x * x, axis=-1, keepdims=True) + RMS_EPS) * g


def _rope128(x, cos_t, sin_a, sin_b):
    return (x * cos_t + pltpu.roll(x, ROPE_HALF, 1) * sin_a
            + pltpu.roll(x, LANES - ROPE_HALF, 1) * sin_b)


def _dot(a, b):
    return jnp.dot(a, b, preferred_element_type=F32)


def _dot_nt(a, b):
    return lax.dot_general(a, b, (((1,), (1,)), ((), ())), preferred_element_type=F32)


def _even_inproj_kernel(x_ref, w_ref, kvn_ref, gmg_ref, gmb_ref, ws_ref, bs_ref,
                        cos_ref, sina_ref, sinb_ref,
                        qa_ref, ckv_ref, kpe_ref, kcat_ref, gate_ref, gmv_ref, *, tiles_per_seq):
    tm = x_ref.shape[0]
    z = _dot(x_ref[...].astype(BF16), w_ref[...])
    o_c, o_u, o_v, o_k = Q_RANK, Q_RANK + KV_RANK, Q_RANK + KV_RANK + GM_WIDTH, Q_RANK + KV_RANK + 2 * GM_WIDTH
    qa_ref[...] = z[:, :o_c]
    c_n = _rms_norm(z[:, o_c:o_u], kvn_ref[...])
    ckv_ref[...] = c_n
    kp = _rope128(z[:, o_k:o_k + LANES], cos_ref[...], sina_ref[...], sinb_ref[...])
    kpe_ref[...] = kp[:, :ROPE_DIM]
    kcat_ref[...] = jnp.concatenate([c_n, kp], axis=1).astype(BF16)

    gu = jax.nn.gelu(z[:, o_u:o_v])
    v_n = _layer_norm(jax.nn.gelu(z[:, o_v:o_k]), gmg_ref[...], gmb_ref[...])

    @pl.when(pl.program_id(0) % tiles_per_seq == tiles_per_seq - 1)
    def _():
        gmv_ref[0] = v_n[tm - GM_CHUNK:, :]

    lane = lax.broadcasted_iota(jnp.int32, (GM_CHUNK, LANES), 1)
    lo = lane < GM_HEAD_DIM
    bs = bs_ref[...]
    for ci in range(tm // GM_CHUNK):
        rows = slice(ci * GM_CHUNK, (ci + 1) * GM_CHUNK)
        pieces = []
        for pr in range(GM_WIDTH // LANES):
            r = v_n[rows, pr * LANES:(pr + 1) * LANES]
            m0 = _dot(ws_ref[2 * pr], jnp.where(lo, r, 0.0).astype(BF16))
            m1 = _dot(ws_ref[2 * pr + 1], jnp.where(lo, 0.0, r).astype(BF16))
            pieces.append(m0 + m1)
        mixed = jnp.concatenate(pieces, axis=1) + bs
        gate_ref[rows, :] = (gu[rows, :] * mixed).astype(BF16)


def _even_inproj(x2d, w_in_p, kv_norm, gm_g, gm_b, ws_m, bs_full, cos_t, sin_a, sin_b, batch, seq):
    n_tok = x2d.shape[0]
    tm = INPROJ_TM
    tiles_per_seq = seq // tm
    row = lambda w: pl.BlockSpec((tm, w), lambda i: (i, 0))
    tab = pl.BlockSpec((tm, LANES), lambda i: (i % tiles_per_seq, 0))
    n_in = w_in_p.shape[1]
    return pl.pallas_call(
        functools.partial(_even_inproj_kernel, tiles_per_seq=tiles_per_seq),
        grid=(n_tok // tm,),
        in_specs=[row(D_MODEL), _const_spec((D_MODEL, n_in)), _const_spec((1, KV_RANK)),
                  _const_spec((1, GM_WIDTH)), _const_spec((1, GM_WIDTH)),
                  _const_spec((GM_HEADS, GM_CHUNK, GM_CHUNK)), _const_spec((GM_CHUNK, GM_WIDTH)),
                  tab, tab, tab],
        out_specs=[row(Q_RANK), row(KV_RANK), row(ROPE_DIM), row(2 * LANES), row(GM_WIDTH),
                   pl.BlockSpec((1, GM_CHUNK, GM_WIDTH), lambda i: (i // tiles_per_seq, 0, 0))],
        out_shape=[jax.ShapeDtypeStruct((n_tok, Q_RANK), F32),
                   jax.ShapeDtypeStruct((n_tok, KV_RANK), F32),
                   jax.ShapeDtypeStruct((n_tok, ROPE_DIM), F32),
                   jax.ShapeDtypeStruct((n_tok, 2 * LANES), BF16),
                   jax.ShapeDtypeStruct((n_tok, GM_WIDTH), BF16),
                   jax.ShapeDtypeStruct((batch, GM_CHUNK, GM_WIDTH), F32)],
        compiler_params=_cparams(("arbitrary",)),
        name="even_inproj",
    )(x2d, w_in_p, kv_norm, gm_g, gm_b, ws_m, bs_full, cos_t, sin_a, sin_b)


def _queries(q_a, qn_g, wq_nope, wq_pe, w_uk, cos_t, sin_a, sin_b):
    qn = _rms_norm(q_a, qn_g).astype(BF16)
    nope = _dot(qn, wq_nope).astype(BF16)
    lat = _dot(nope, w_uk)
    pe = _dot(qn, wq_pe)
    pes = [_rope128(pe[:, h * LANES:(h + 1) * LANES], cos_t, sin_a, sin_b) for h in range(MLA_HEADS)]
    return lat, pes


def _mla_attn_kernel(qa_ref, kcat_ref, qn_ref, wqn_ref, wqp_ref, wuk_ref, wuv_ref,
                     cos_ref, sina_ref, sinb_ref, out_ref, qcat_s, m_s, acc_s):
    tq = qa_ref.shape[0]
    n_rows = MLA_HEADS * tq
    i = pl.program_id(1)
    lat, pes = _queries(qa_ref[...], qn_ref[...], wqn_ref[...], wqp_ref[...], wuk_ref[...],
                        cos_ref[...], sina_ref[...], sinb_ref[...])
    for h in range(MLA_HEADS):
        qcat_s[h * tq:(h + 1) * tq, :] = jnp.concatenate(
            [lat[:, h * LANES:(h + 1) * LANES], pes[h]], axis=1).astype(BF16)

    ones = jnp.ones((tq, LANES), BF16)
    twice = lambda a: jnp.concatenate([a, a], axis=1)

    def scores(j):
        k = kcat_ref[pl.ds(pl.multiple_of(j * tq, tq), tq), :]
        s = _dot_nt(qcat_s[...], k) * ATTN_SCALE
        return s, jnp.concatenate([k[:, :KV_RANK], ones], axis=1)

    s, v1 = scores(i)
    q_pos = lax.broadcasted_iota(jnp.int32, (n_rows, tq), 0) & (tq - 1)
    k_pos = lax.broadcasted_iota(jnp.int32, (n_rows, tq), 1)
    s = jnp.where(k_pos <= q_pos, s, -jnp.inf)
    m0 = jnp.broadcast_to(jnp.max(s, axis=-1, keepdims=True), (n_rows, LANES))
    m_s[...] = m0
    acc_s[...] = _dot(jnp.exp(s - twice(m0)).astype(BF16), v1)

    def body(j, carry):
        s, v1 = scores(j)
        m_old = m_s[...]
        m_new = jnp.maximum(m_old, jnp.max(s, axis=-1, keepdims=True))
        alpha = jnp.exp(m_old - m_new)
        p = jnp.exp(s - twice(m_new))
        acc_s[...] = twice(alpha) * acc_s[...] + _dot(p.astype(BF16), v1)
        m_s[...] = m_new
        return carry

    lax.fori_loop(0, i, body, 0)

    o_all = jnp.concatenate(
        [acc_s[h * tq:(h + 1) * tq, :KV_RANK] / acc_s[h * tq:(h + 1) * tq, KV_RANK:]
         for h in range(MLA_HEADS)], axis=1).astype(BF16)
    out_ref[...] = _dot(o_all, wuv_ref[...]).astype(BF16)


def _mla_attn(q_a, kcat3, qn_g, wq_nope, wq_pe, w_uk, w_uv, cos_t, sin_a, sin_b):
    batch, seq, _ = kcat3.shape
    tq = ATTN_TQ
    nq = seq // tq
    tab = pl.BlockSpec((tq, LANES), lambda b, i: (i, 0))
    return pl.pallas_call(
        _mla_attn_kernel,
        grid=(batch, nq),
        in_specs=[pl.BlockSpec((tq, Q_RANK), lambda b, i: (b * nq + i, 0)),
                  pl.BlockSpec((None, seq, 2 * LANES), lambda b, i: (b, 0, 0)),
                  _const_spec((1, Q_RANK)), _const_spec(wq_nope.shape), _const_spec(wq_pe.shape),
                  _const_spec(w_uk.shape), _const_spec(w_uv.shape), tab, tab, tab],
        out_specs=pl.BlockSpec((tq, MLA_HEADS * V_DIM), lambda b, i: (b * nq + i, 0)),
        out_shape=jax.ShapeDtypeStruct((batch * seq, MLA_HEADS * V_DIM), BF16),
        scratch_shapes=[pltpu.VMEM((MLA_HEADS * tq, 2 * LANES), BF16),
                        pltpu.VMEM((MLA_HEADS * tq, LANES), F32),
                        pltpu.VMEM((MLA_HEADS * tq, 2 * KV_RANK), F32)],
        compiler_params=_cparams(("arbitrary", "arbitrary")),
        name="mla_attn",
    )(q_a, kcat3, qn_g, wq_nope, wq_pe, w_uk, w_uv, cos_t, sin_a, sin_b)


def _mix_ffn_kernel(*refs, n_act, has_pre):
    x_ref = refs[0]
    act_refs = refs[1:1 + n_act]
    k = 1 + n_act
    pre_ref = refs[k] if has_pre else None
    k += int(has_pre)
    wo_refs = refs[k:k + n_act]
    k += n_act
    g1_ref, b1_ref, g2_ref, b2_ref, w1_ref, w2_ref, out_ref = refs[k:k + 7]

    x = x_ref[...]
    f = None
    for idx in range(n_act):
        a = act_refs[idx][...]
        if has_pre and idx == 0:
            a = _dot(a.astype(BF16), pre_ref[...])
        part = _dot(a.astype(BF16), wo_refs[idx][...])
        f = part if f is None else f + part
    x1 = _layer_norm(DN_ALPHA * x + f, g1_ref[...], b1_ref[...])
    x1b = x1.astype(BF16)
    acc = None
    for c in range(FFN_HIDDEN // FFN_HC):
        cols = slice(c * FFN_HC, (c + 1) * FFN_HC)
        h = jnp.maximum(_dot(x1b, w1_ref[:, cols]), 0.0)
        part = _dot((h * h).astype(BF16), w2_ref[cols, :])
        acc = part if acc is None else acc + part
    out_ref[...] = _layer_norm(DN_ALPHA * x1 + acc, g2_ref[...], b2_ref[...])


def _mix_ffn(x2d, acts, pre_w, wos, ln1_g, ln1_b, ln2_g, ln2_b, w1, w2, name):
    n_tok = x2d.shape[0]
    tm = min(FFN_TM, n_tok)
    row = lambda w: pl.BlockSpec((tm, w), lambda i: (i, 0))
    has_pre = pre_w is not None
    in_specs = [row(D_MODEL)] + [row(a.shape[1]) for a in acts]
    args = [x2d] + list(acts)
    if has_pre:
        in_specs.append(_const_spec(pre_w.shape))
        args.append(pre_w)
    in_specs += [_const_spec(w.shape) for w in wos]
    args += list(wos)
    in_specs += [_const_spec((1, D_MODEL))] * 4 + [_const_spec(w1.shape), _const_spec(w2.shape)]
    args += [ln1_g, ln1_b, ln2_g, ln2_b, w1, w2]
    return pl.pallas_call(
        functools.partial(_mix_ffn_kernel, n_act=len(acts), has_pre=has_pre),
        grid=(n_tok // tm,),
        in_specs=in_specs,
        out_specs=row(D_MODEL),
        out_shape=jax.ShapeDtypeStruct((n_tok, D_MODEL), F32),
        compiler_params=_cparams(("arbitrary",)),
        name=name,
    )(*args)


def _even_inproj_sample_kernel(x_ref, w_ref, kvn_ref, gmg_ref, gmb_ref, ws0_ref, bs0_ref,
                               cos_ref, sina_ref, sinb_ref, qn_ref, wqn_ref, wqp_ref, wuk_ref,
                               ckv_ref, kpe_ref, gate_ref, vn_ref, qlat_ref, qpe_ref):
    z = _dot(x_ref[...].astype(BF16), w_ref[...])
    o_c, o_u, o_v, o_k = Q_RANK, Q_RANK + KV_RANK, Q_RANK + KV_RANK + GM_WIDTH, Q_RANK + KV_RANK + 2 * GM_WIDTH
    cos_t, sin_a, sin_b = cos_ref[...], sina_ref[...], sinb_ref[...]
    c_n = _rms_norm(z[:, o_c:o_u], kvn_ref[...])
    ckv_ref[...] = c_n
    kp = _rope128(z[:, o_k:o_k + LANES], cos_t, sin_a, sin_b)
    kpe_ref[...] = kp[:, :ROPE_DIM]
    gu = jax.nn.gelu(z[:, o_u:o_v])
    v_n = _layer_norm(jax.nn.gelu(z[:, o_v:o_k]), gmg_ref[...], gmb_ref[...])
    vn_ref[...] = v_n
    mixed = ws0_ref[...].astype(F32) * v_n.astype(BF16).astype(F32) + bs0_ref[...]
    gate_ref[...] = (gu * mixed).astype(BF16)
    lat, pes = _queries(z[:, :o_c], qn_ref[...], wqn_ref[...], wqp_ref[...], wuk_ref[...],
                        cos_t, sin_a, sin_b)
    qlat_ref[...] = lat
    qpe_ref[...] = jnp.concatenate(pes, axis=1)


def _even_inproj_sample(xs, w_in_p, kv_norm, gm_g, gm_b, ws0, bs0, cos_t, sin_a, sin_b,
                        qn_g, wq_nope, wq_pe, w_uk):
    n = xs.shape[0]
    args = (xs, w_in_p, kv_norm, gm_g, gm_b, ws0, bs0, cos_t, sin_a, sin_b, qn_g, wq_nope, wq_pe, w_uk)
    full = lambda shape: pl.BlockSpec(shape, lambda i: (0,) * len(shape))
    widths = [(KV_RANK, F32), (ROPE_DIM, F32), (GM_WIDTH, BF16), (GM_WIDTH, F32),
              (MLA_HEADS * LANES, F32), (MLA_HEADS * LANES, F32)]
    return pl.pallas_call(
        _even_inproj_sample_kernel,
        grid=(1,),
        in_specs=[full(a.shape) for a in args],
        out_specs=[full((n, w)) for w, _ in widths],
        out_shape=[jax.ShapeDtypeStruct((n, w), dt) for w, dt in widths],
        compiler_params=_cparams(("arbitrary",)),
        name="even_inproj_sample",
    )(*args)


def _decode_attn_kernel(pt_ref, qlat_ref, qpe_ref, ckvs_ref, kpes_ref, *rest, n_pg):
    ckv_refs = rest[:n_pg]
    kpe_refs = rest[n_pg:2 * n_pg]
    out_ref, m_s, l_s, acc_s = rest[2 * n_pg:]
    g = pl.program_id(1)

    @pl.when(g == 0)
    def _():
        m_s[...] = jnp.full(m_s.shape, -jnp.inf, F32)
        l_s[...] = jnp.zeros(l_s.shape, F32)
        acc_s[...] = jnp.zeros(acc_s.shape, F32)

    qlat = qlat_ref[...].astype(BF16)
    qpe = qpe_ref[...][:, :ROPE_DIM].astype(BF16)
    cks, ss = [], []
    for k in range(n_pg):
        ck = ckv_refs[k][...].astype(BF16)
        kp = kpe_refs[k][...].astype(BF16)
        cks.append(ck)
        ss.append(_dot_nt(qlat, ck) + _dot(qpe, kp))
    s = jnp.concatenate(ss, axis=1) * ATTN_SCALE
    m_old = m_s[...]
    m_new = jnp.maximum(m_old, jnp.max(s, axis=-1, keepdims=True))
    alpha = jnp.exp(m_old - m_new)
    p = jnp.exp(s - m_new)
    l_s[...] = alpha * l_s[...] + jnp.sum(p, axis=-1, keepdims=True)
    pb = p.astype(BF16)
    pv = None
    for k in range(n_pg):
        part = _dot(pb[:, k * PAGE_SIZE:(k + 1) * PAGE_SIZE], cks[k])
        pv = part if pv is None else pv + part
    acc_s[...] = alpha * acc_s[...] + pv
    m_s[...] = m_new

    @pl.when(g == pl.num_programs(1) - 1)
    def _():
        r = lambda a: a.astype(BF16).astype(F32)
        kv = r(ckvs_ref[...])
        s_self = (jnp.sum(r(qlat_ref[...]) * kv, axis=-1, keepdims=True)
                  + jnp.sum(r(qpe_ref[...]) * r(kpes_ref[...]), axis=-1, keepdims=True)) * ATTN_SCALE
        m_o = m_s[...]
        m_n = jnp.maximum(m_o, s_self)
        a = jnp.exp(m_o - m_n)
        p_self = jnp.exp(s_self - m_n)
        l_f = a * l_s[...] + p_self
        out_ref[...] = (a * acc_s[...] + r(p_self) * kv) / l_f


def _decode_attn(page_table, qlat2, qpe2, ckv_s3, kpe_s3, cache_ckv_e, cache_kpe_t):
    dec_batch, n_pages = page_table.shape
    n_pg = DEC_PG
    qspec = pl.BlockSpec((MLA_HEADS, LANES), lambda b, g, pt: (b, 0))
    self_spec = pl.BlockSpec((None, 1, LANES), lambda b, g, pt: (b, 0, 0))

    def page_spec(page_shape, k):
        return pl.BlockSpec((None,) + page_shape, lambda b, g, pt: (pt[b, g * n_pg + k], 0, 0))

    grid_spec = pltpu.PrefetchScalarGridSpec(
        num_scalar_prefetch=1,
        grid=(dec_batch, n_pages // n_pg),
        in_specs=[qspec, qspec, self_spec, self_spec]
                 + [page_spec((PAGE_SIZE, KV_RANK), k) for k in range(n_pg)]
                 + [page_spec((ROPE_DIM, PAGE_SIZE), k) for k in range(n_pg)],
        out_specs=qspec,
        scratch_shapes=[pltpu.VMEM((MLA_HEADS, 1), F32), pltpu.VMEM((MLA_HEADS, 1), F32),
                        pltpu.VMEM((MLA_HEADS, KV_RANK), F32)],
    )
    return pl.pallas_call(
        functools.partial(_decode_attn_kernel, n_pg=n_pg),
        grid_spec=grid_spec,
        out_shape=jax.ShapeDtypeStruct((dec_batch * MLA_HEADS, KV_RANK), F32),
        compiler_params=_cparams(("arbitrary", "arbitrary")),
        name="decode_attn",
    )(page_table, qlat2, qpe2, ckv_s3, kpe_s3, *([cache_ckv_e] * n_pg), *([cache_kpe_t] * n_pg))


def _s5_readout(hcat_blocks, u, cw_ref, d_ref, wglu_ref):
    y = jnp.concatenate([_dot(hcat_blocks[k], cw_ref[k]) for k in range(S5_NB)], axis=1)
    y = jax.nn.gelu(y + d_ref[...] * u)
    return (y * jax.nn.sigmoid(_dot(y.astype(BF16), wglu_ref[...]))).astype(BF16)


def _s5_prompt_kernel(x_ref, win_ref, bw_ref, a_ref, cw_ref, d_ref, wglu_ref,
                      yg_ref, hre_ref, him_ref, u_s, bu_s, h_s):
    n_b, seg, _ = x_ref.shape
    half = S5_SLABS // 2

    @pl.when(pl.program_id(0) == 0)
    def _():
        h_s[...] = jnp.zeros(h_s.shape, F32)

    for b in range(n_b):
        u = _dot(x_ref[b].astype(BF16), win_ref[...])
        u_s[b] = u
        ub = u.astype(BF16)
        for k in range(S5_NB):
            r = _dot(ub[:, k * LANES:(k + 1) * LANES], bw_ref[k])
            for q in range(4):
                bu_s[4 * k + q, pl.ds(b, seg, stride=n_b), :] = r[:, q * LANES:(q + 1) * LANES]
                bu_s[half + 4 * k + q, pl.ds(b, seg, stride=n_b), :] = r[:, (4 + q) * LANES:(5 + q) * LANES]

    def step(t, h):
        rows = pl.ds(pl.multiple_of(t * n_b, n_b), n_b)
        hs = bu_s[:, rows, :]
        ar, ai = a_ref[:half], a_ref[half:]
        hr, hi = h[:half], h[half:]
        hn = jnp.concatenate([ar * hr - ai * hi + hs[:half], ar * hi + ai * hr + hs[half:]], axis=0)
        bu_s[:, rows, :] = hn
        return hn

    h_fin = lax.fori_loop(0, seg, step, h_s[...])
    h_s[...] = h_fin
    hre_ref[...] = jnp.concatenate([h_fin[s] for s in range(half)], axis=1)
    him_ref[...] = jnp.concatenate([h_fin[half + s] for s in range(half)], axis=1)

    for b in range(n_b):
        blocks = []
        for k in range(S5_NB):
            slabs = [bu_s[4 * k + q, pl.ds(b, seg, stride=n_b), :] for q in range(4)]
            slabs += [bu_s[half + 4 * k + q, pl.ds(b, seg, stride=n_b), :] for q in range(4)]
            blocks.append(jnp.concatenate(slabs, axis=1).astype(BF16))
        yg_ref[b] = _s5_readout(blocks, u_s[b], cw_ref, d_ref, wglu_ref)


def _s5_prompt(x3, w_in, bw, a_b, cw, d_row, w_glu):
    batch, seq, _ = x3.shape
    seg = S5_L
    return pl.pallas_call(
        _s5_prompt_kernel,
        grid=(seq // seg,),
        in_specs=[pl.BlockSpec((batch, seg, D_MODEL), lambda c: (0, c, 0)),
                  _const_spec(w_in.shape), _const_spec(bw.shape), _const_spec(a_b.shape),
                  _const_spec(cw.shape), _const_spec(d_row.shape), _const_spec(w_glu.shape)],
        out_specs=[pl.BlockSpec((batch, seg, S5_WIDTH), lambda c: (0, c, 0)),
                   pl.BlockSpec((batch, S5_STATES), lambda c: (0, 0)),
                   pl.BlockSpec((batch, S5_STATES), lambda c: (0, 0))],
        out_shape=[jax.ShapeDtypeStruct((batch, seq, S5_WIDTH), BF16),
                   jax.ShapeDtypeStruct((batch, S5_STATES), F32),
                   jax.ShapeDtypeStruct((batch, S5_STATES), F32)],
        scratch_shapes=[pltpu.VMEM((batch, seg, S5_WIDTH), F32),
                        pltpu.VMEM((S5_SLABS, seg * batch, LANES), F32),
                        pltpu.VMEM((S5_SLABS, batch, LANES), F32)],
        compiler_params=_cparams(("arbitrary",)),
        name="s5_prompt",
    )(x3, w_in, bw, a_b, cw, d_row, w_glu)


def _s5_sample_kernel(x_ref, h0r_ref, h0i_ref, win_ref, bw_ref, ar_ref, ai_ref, cw_ref, d_ref, wglu_ref,
                      yg_ref, hre_ref, him_ref):
    u = _dot(x_ref[...].astype(BF16), win_ref[...])
    ub = u.astype(BF16)
    bu = [_dot(ub[:, k * LANES:(k + 1) * LANES], bw_ref[k]) for k in range(S5_NB)]
    w = S5_KB * S5_STATE
    bu_r = jnp.concatenate([r[:, :w] for r in bu], axis=1)
    bu_i = jnp.concatenate([r[:, w:] for r in bu], axis=1)
    ar, ai, h0r, h0i = ar_ref[...], ai_ref[...], h0r_ref[...], h0i_ref[...]
    hr = bu_r + (ar * h0r - ai * h0i)
    hi = bu_i + (ar * h0i + ai * h0r)
    hre_ref[...] = hr
    him_ref[...] = hi
    blocks = [jnp.concatenate([hr[:, k * w:(k + 1) * w], hi[:, k * w:(k + 1) * w]], axis=1).astype(BF16)
              for k in range(S5_NB)]
    yg_ref[...] = _s5_readout(blocks, u, cw_ref, d_ref, wglu_ref)


def _s5_sample(xs, h0r, h0i, w_in, bw, a_r, a_i, cw, d_row, w_glu):
    n = xs.shape[0]
    args = (xs, h0r, h0i, w_in, bw, a_r, a_i, cw, d_row, w_glu)
    full = lambda shape: pl.BlockSpec(shape, lambda i: (0,) * len(shape))
    return pl.pallas_call(
        _s5_sample_kernel,
        grid=(1,),
        in_specs=[full(a.shape) for a in args],
        out_specs=[full((n, S5_WIDTH)), full((n, S5_STATES)), full((n, S5_STATES))],
        out_shape=[jax.ShapeDtypeStruct((n, S5_WIDTH), BF16),
                   jax.ShapeDtypeStruct((n, S5_STATES), F32),
                   jax.ShapeDtypeStruct((n, S5_STATES), F32)],
        compiler_params=_cparams(("arbitrary",)),
        name="s5_sample",
    )(*args)


def _rope_tables(pos):
    freqs = ROPE_THETA ** (-jnp.arange(ROPE_HALF, dtype=F32) / ROPE_HALF)
    ang = pos.astype(F32)[:, None] * freqs
    cos, sin = jnp.cos(ang), jnp.sin(ang)
    zero = jnp.zeros_like(cos)
    pad = jnp.zeros((pos.shape[0], LANES - ROPE_DIM), F32)
    cos_t = jnp.concatenate([cos, cos, pad], axis=1)
    sin_a = jnp.concatenate([zero, sin, pad], axis=1)
    sin_b = jnp.concatenate([-sin, zero, pad], axis=1)
    return cos_t, sin_a, sin_b


def _block_diag(blocks):
    n, r, c = blocks.shape
    eye = jnp.eye(n, dtype=blocks.dtype)
    return jnp.einsum('nrc,nm->nrmc', blocks, eye).reshape(n * r, n * c)


def _even_params(w_in, q_norm, w_q_b, kv_norm, w_kv_b, gm_g, gm_b, gm_w_s, gm_b_s, w_out):
    i1, i2, i3, i4 = Q_RANK, Q_RANK + KV_RANK, Q_RANK + KV_RANK + ROPE_DIM, Q_RANK + KV_RANK + ROPE_DIM + GM_WIDTH
    w_in_p = jnp.concatenate([w_in[:, :i2], w_in[:, i3:], w_in[:, i2:i3],
                              jnp.zeros((D_MODEL, LANES - ROPE_DIM), F32)], axis=1).astype(BF16)
    wq_nope = w_q_b[:, :, :NOPE_DIM].reshape(Q_RANK, MLA_HEADS * NOPE_DIM).astype(BF16)
    pe = w_q_b[:, :, NOPE_DIM:]
    wq_pe = jnp.concatenate([pe, jnp.zeros((Q_RANK, MLA_HEADS, LANES - ROPE_DIM), F32)],
                            axis=2).reshape(Q_RANK, MLA_HEADS * LANES).astype(BF16)
    w_uk = _block_diag(jnp.transpose(w_kv_b[:, :, :NOPE_DIM], (1, 2, 0))).astype(BF16)
    w_uv = _block_diag(jnp.transpose(w_kv_b[:, :, NOPE_DIM:], (1, 0, 2))).astype(BF16)
    causal = jnp.tril(jnp.ones((GM_CHUNK, GM_CHUNK), dtype=bool))
    ws_m = jnp.where(causal[None], gm_w_s, 0).astype(BF16)
    bs_full = jnp.repeat(gm_b_s.T, GM_HEAD_DIM, axis=1)
    ws0 = jnp.repeat(gm_w_s[:, 0, 0], GM_HEAD_DIM)[None, :].astype(BF16)
    bs0 = bs_full[:1]
    n_attn = MLA_HEADS * V_DIM
    return dict(w_in_p=w_in_p, qn=q_norm[None, :], wq_nope=wq_nope, wq_pe=wq_pe, w_uk=w_uk, w_uv=w_uv,
                kvn=kv_norm[None, :], gm_g=gm_g[None, :], gm_b=gm_b[None, :], ws_m=ws_m, bs_full=bs_full,
                ws0=ws0, bs0=bs0, wo_attn=w_out[:n_attn].astype(BF16), wo_gate=w_out[n_attn:].astype(BF16))


def _s5_params(w_in, a_re, a_im, b_re, b_im, c_re, c_im, d, log_dt, w_glu, w_out, batch):
    dt = jnp.exp(log_dt)[:, None]
    ld_r, ld_i = a_re * dt, a_im * dt
    mag = jnp.exp(ld_r)
    ab_r, ab_i = mag * jnp.cos(ld_i), mag * jnp.sin(ld_i)
    den = a_re * a_re + a_im * a_im
    cr = ((ab_r - 1.0) * a_re + ab_i * a_im) / den
    ci = (ab_i * a_re - (ab_r - 1.0) * a_im) / den
    bb_r = cr[..., None] * b_re - ci[..., None] * b_im
    bb_i = cr[..., None] * b_im + ci[..., None] * b_re

    def in_blocks(bb):
        return jnp.stack([_block_diag(jnp.transpose(bb[k * S5_KB:(k + 1) * S5_KB], (0, 2, 1)))
                          for k in range(S5_NB)])

    def out_blocks(cc):
        return jnp.stack([_block_diag(jnp.transpose(cc[k * S5_KB:(k + 1) * S5_KB], (0, 2, 1)))
                          for k in range(S5_NB)])

    bw = jnp.concatenate([in_blocks(bb_r), in_blocks(bb_i)], axis=2).astype(BF16)
    cw = jnp.concatenate([out_blocks(c_re), out_blocks(-c_im)], axis=1).astype(BF16)
    a_r, a_i = ab_r.reshape(1, S5_STATES), ab_i.reshape(1, S5_STATES)
    half = S5_SLABS // 2
    a_b = jnp.concatenate([jnp.broadcast_to(a_r.reshape(half, 1, LANES), (half, batch, LANES)),
                           jnp.broadcast_to(a_i.reshape(half, 1, LANES), (half, batch, LANES))], axis=0)
    return dict(w_in=w_in.astype(BF16), bw=bw, cw=cw, a_r=a_r, a_i=a_i, a_b=a_b,
                d=d.reshape(1, S5_WIDTH), w_glu=w_glu.astype(BF16), w_out=w_out.astype(BF16))


def kernel(x_prompt, x_sample, cache_ckv, cache_kpe, state_s5_re, state_s5_im, page_table, w_in_even, mla_q_norm, mla_w_q_b, mla_kv_norm, mla_w_kv_b, gm_norm_g, gm_norm_b, gm_w_s, gm_b_s, w_out_even, w_in_odd, s5_a_re, s5_a_im, s5_b_re, s5_b_im, s5_c_re, s5_c_im, s5_d, s5_log_dt, s5_w_glu, w_out_odd, ln_mix_g, ln_mix_b, ln_ffn_g, ln_ffn_b, ffn_w1, ffn_w2):
    batch, seq, _ = x_prompt.shape
    dec_batch, dec_seq, _ = x_sample.shape
    assert dec_seq == 1 and seq % INPROJ_TM == 0 and seq % ATTN_TQ == 0 and seq % S5_L == 0
    assert (batch * seq) % FFN_TM == 0 and page_table.shape[1] % DEC_PG == 0
    assert batch == SUBLANES

    xp = x_prompt.reshape(batch * seq, D_MODEL)
    xs = x_sample.reshape(dec_batch, D_MODEL)
    tabs_p = _rope_tables(jnp.arange(seq, dtype=jnp.int32))
    tabs_s = _rope_tables(PAST_LEN + jnp.arange(dec_seq, dtype=jnp.int32))

    outs = {k: [] for k in ("ckv_p", "kpe_p", "ckv_s", "kpe_s", "gmv_p", "gmv_s",
                            "s5re_p", "s5im_p", "s5re_s", "s5im_s")}
    for layer in range(DEPTH):
        ln = (ln_mix_g[layer][None, :], ln_mix_b[layer][None, :],
              ln_ffn_g[layer][None, :], ln_ffn_b[layer][None, :],
              ffn_w1[layer].astype(BF16), ffn_w2[layer].astype(BF16))
        if layer % 2 == 0:
            e = layer // 2
            p = _even_params(w_in_even[e], mla_q_norm[e], mla_w_q_b[e], mla_kv_norm[e], mla_w_kv_b[e],
                             gm_norm_g[e], gm_norm_b[e], gm_w_s[e], gm_b_s[e], w_out_even[e])
            q_a, ckv, kpe, kcat, gate, gmv = _even_inproj(
                xp, p["w_in_p"], p["kvn"], p["gm_g"], p["gm_b"], p["ws_m"], p["bs_full"], *tabs_p, batch, seq)
            attn = _mla_attn(q_a, kcat.reshape(batch, seq, 2 * LANES), p["qn"], p["wq_nope"], p["wq_pe"],
                             p["w_uk"], p["w_uv"], *tabs_p)
            xp = _mix_ffn(xp, [attn, gate], None, [p["wo_attn"], p["wo_gate"]], *ln, name="even_ffn_prompt")
            outs["ckv_p"].append(ckv.reshape(batch, seq, KV_RANK))
            outs["kpe_p"].append(kpe.reshape(batch, seq, ROPE_DIM))
            outs["gmv_p"].append(gmv)
            ckv_s, kpe_s, gate_s, vn_s, qlat_s, qpe_s = _even_inproj_sample(
                xs, p["w_in_p"], p["kvn"], p["gm_g"], p["gm_b"], p["ws0"], p["bs0"], *tabs_s,
                p["qn"], p["wq_nope"], p["wq_pe"], p["w_uk"])
            kpe_pad = jnp.concatenate([kpe_s, jnp.zeros((dec_batch, LANES - ROPE_DIM), F32)], axis=1)
            o_lat = _decode_attn(page_table,
                                 qlat_s.reshape(dec_batch * MLA_HEADS, LANES),
                                 qpe_s.reshape(dec_batch * MLA_HEADS, LANES),
                                 ckv_s.reshape(dec_batch, 1, KV_RANK), kpe_pad.reshape(dec_batch, 1, LANES),
                                 cache_ckv[e], jnp.swapaxes(cache_kpe[e], 1, 2))
            xs = _mix_ffn(xs, [o_lat.reshape(dec_batch, MLA_HEADS * KV_RANK), gate_s], p["w_uv"],
                          [p["wo_attn"], p["wo_gate"]], *ln, name="even_ffn_sample")
            outs["ckv_s"].append(ckv_s.reshape(dec_batch, 1, KV_RANK))
            outs["kpe_s"].append(kpe_s.reshape(dec_batch, 1, ROPE_DIM))
            outs["gmv_s"].append(vn_s.reshape(dec_batch, 1, GM_WIDTH))
        else:
            o = layer // 2
            p = _s5_params(w_in_odd[o], s5_a_re[o], s5_a_im[o], s5_b_re[o], s5_b_im[o], s5_c_re[o],
                           s5_c_im[o], s5_d[o], s5_log_dt[o], s5_w_glu[o], w_out_odd[o], batch)
            yg, hre, him = _s5_prompt(xp.reshape(batch, seq, D_MODEL), p["w_in"], p["bw"], p["a_b"],
                                      p["cw"], p["d"], p["w_glu"])
            xp = _mix_ffn(xp, [yg.reshape(batch * seq, S5_WIDTH)], None, [p["w_out"]], *ln,
                          name="odd_ffn_prompt")
            outs["s5re_p"].append(hre.reshape(batch, S5_GROUPS, S5_STATE))
            outs["s5im_p"].append(him.reshape(batch, S5_GROUPS, S5_STATE))
            yg_s, hre_s, him_s = _s5_sample(
                xs, state_s5_re[o].reshape(dec_batch, S5_STATES), state_s5_im[o].reshape(dec_batch, S5_STATES),
                p["w_in"], p["bw"], p["a_r"], p["a_i"], p["cw"], p["d"], p["w_glu"])
            xs = _mix_ffn(xs, [yg_s], None, [p["w_out"]], *ln, name="odd_ffn_sample")
            outs["s5re_s"].append(hre_s.reshape(dec_batch, S5_GROUPS, S5_STATE))
            outs["s5im_s"].append(him_s.reshape(dec_batch, S5_GROUPS, S5_STATE))

    st = jnp.stack
    return (xp.reshape(batch, seq, D_MODEL), xs.reshape(dec_batch, dec_seq, D_MODEL),
            st(outs["ckv_p"]), st(outs["kpe_p"]), st(outs["ckv_s"]), st(outs["kpe_s"]),
            st(outs["gmv_p"]), st(outs["gmv_s"]),
            st(outs["s5re_p"]), st(outs["s5im_p"]), st(outs["s5re_s"]), st(outs["s5im_s"]))
```

```python
import functools
import math

import jax
import jax.numpy as jnp
from jax import lax
from jax.experimental import pallas as pl
from jax.experimental.pallas import tpu as pltpu

F32 = jnp.float32
BF16 = jnp.bfloat16

D_MODEL = 1024
DEPTH = 2
PAST_LEN = 16384
PAGE_SIZE = 128
MLA_HEADS = 8
NOPE_DIM = 64
ROPE_DIM = 32
ROPE_HALF = ROPE_DIM // 2
V_DIM = 64
Q_RANK = 256
KV_RANK = 128
ROPE_THETA = 10000.0
ATTN_SCALE = 1.0 / math.sqrt(NOPE_DIM + ROPE_DIM)
GM_HEADS = 8
GM_HEAD_DIM = 64
GM_WIDTH = GM_HEADS * GM_HEAD_DIM
GM_CHUNK = 128
S5_GROUP_DIM = 16
S5_GROUPS = 32
S5_STATE = 64
S5_WIDTH = S5_GROUPS * S5_GROUP_DIM
S5_STATES = S5_GROUPS * S5_STATE
FFN_HIDDEN = 4 * D_MODEL
DN_ALPHA = (2 * DEPTH) ** 0.25
LN_EPS = 1e-5
RMS_EPS = 1e-6

LANES = 128
SUBLANES = 8
VMEM_LIMIT_BYTES = 56 * 1024 * 1024

INPROJ_TM = 512
ATTN_TQ = 256
FFN_TM = 512
FFN_HC = 1024
S5_L = 64
DEC_PG = 16
S5_KB = 8
S5_NB = S5_GROUPS // S5_KB
S5_SLABS = 2 * S5_STATES // LANES


def _cparams(sem):
    return pltpu.CompilerParams(dimension_semantics=sem, vmem_limit_bytes=VMEM_LIMIT_BYTES)


def _const_spec(shape):
    n = len(shape)
    return pl.BlockSpec(shape, lambda *_: (0,) * n, pipeline_mode=pl.Buffered(1))


def _layer_norm(x, g, b):
    mu = jnp.mean(x, axis=-1, keepdims=True)
    xc = x - mu
    var = jnp.mean(xc * xc, axis=-1, keepdims=True)
    return xc * lax.rsqrt(var + LN_EPS) * g + b


def _rms_norm(x, g):
    return x * lax.rsqrt(jnp.mean(x * x, axis=-1, keepdims=True) + RMS_EPS) * g


def _rope128(x, cos_t, sin_a, sin_b):
    return (x * cos_t + pltpu.roll(x, ROPE_HALF, 1) * sin_a
            + pltpu.roll(x, LANES - ROPE_HALF, 1) * sin_b)


def _dot(a, b):
    return jnp.dot(a, b, preferred_element_type=F32)


def _dot_nt(a, b):
    return lax.dot_general(a, b, (((1,), (1,)), ((), ())), preferred_element_type=F32)


def _even_inproj_kernel(x_ref, w_ref, kvn_ref, gmg_ref, gmb_ref, ws_ref, bs_ref,
                        cos_ref, sina_ref, sinb_ref,
                        qa_ref, ckv_ref, kpe_ref, kcat_ref, gate_ref, gmv_ref, *, tiles_per_seq):
    tm = x_ref.shape[0]
    z = _dot(x_ref[...].astype(BF16), w_ref[...])
    o_c, o_u, o_v, o_k = Q_RANK, Q_RANK + KV_RANK, Q_RANK + KV_RANK + GM_WIDTH, Q_RANK + KV_RANK + 2 * GM_WIDTH
    qa_ref[...] = z[:, :o_c]
    c_n = _rms_norm(z[:, o_c:o_u], kvn_ref[...])
    ckv_ref[...] = c_n
    kp = _rope128(z[:, o_k:o_k + LANES], cos_ref[...], sina_ref[...], sinb_ref[...])
    kpe_ref[...] = kp[:, :ROPE_DIM]
    kcat_ref[...] = jnp.concatenate([c_n, kp], axis=1).astype(BF16)

    gu = jax.nn.gelu(z[:, o_u:o_v])
    v_n = _layer_norm(jax.nn.gelu(z[:, o_v:o_k]), gmg_ref[...], gmb_ref[...])

    @pl.when(pl.program_id(0) % tiles_per_seq == tiles_per_seq - 1)
    def _():
        gmv_ref[0] = v_n[tm - GM_CHUNK:, :]

    lane = lax.broadcasted_iota(jnp.int32, (GM_CHUNK, LANES), 1)
    lo = lane < GM_HEAD_DIM
    bs = bs_ref[...]
    for ci in range(tm // GM_CHUNK):
        rows = slice(ci * GM_CHUNK, (ci + 1) * GM_CHUNK)
        pieces = []
        for pr in range(GM_WIDTH // LANES):
            r = v_n[rows, pr * LANES:(pr + 1) * LANES]
            m0 = _dot(ws_ref[2 * pr], jnp.where(lo, r, 0.0).astype(BF16))
            m1 = _dot(ws_ref[2 * pr + 1], jnp.where(lo, 0.0, r).astype(BF16))
            pieces.append(m0 + m1)
        mixed = jnp.concatenate(pieces, axis=1) + bs
        gate_ref[rows, :] = (gu[rows, :] * mixed).astype(BF16)


def _even_inproj(x2d, w_in_p, kv_norm, gm_g, gm_b, ws_m, bs_full, cos_t, sin_a, sin_b, batch, seq):
    n_tok = x2d.shape[0]
    tm = INPROJ_TM
    tiles_per_seq = seq // tm
    row = lambda w: pl.BlockSpec((tm, w), lambda i: (i, 0))
    tab = pl.BlockSpec((tm, LANES), lambda i: (i % tiles_per_seq, 0))
    n_in = w_in_p.shape[1]
    return pl.pallas_call(
        functools.partial(_even_inproj_kernel, tiles_per_seq=tiles_per_seq),
        grid=(n_tok // tm,),
        in_specs=[row(D_MODEL), _const_spec((D_MODEL, n_in)), _const_spec((1, KV_RANK)),
                  _const_spec((1, GM_WIDTH)), _const_spec((1, GM_WIDTH)),
                  _const_spec((GM_HEADS, GM_CHUNK, GM_CHUNK)), _const_spec((GM_CHUNK, GM_WIDTH)),
                  tab, tab, tab],
        out_specs=[row(Q_RANK), row(KV_RANK), row(ROPE_DIM), row(2 * LANES), row(GM_WIDTH),
                   pl.BlockSpec((1, GM_CHUNK, GM_WIDTH), lambda i: (i // tiles_per_seq, 0, 0))],
        out_shape=[jax.ShapeDtypeStruct((n_tok, Q_RANK), F32),
                   jax.ShapeDtypeStruct((n_tok, KV_RANK), F32),
                   jax.ShapeDtypeStruct((n_tok, ROPE_DIM), F32),
                   jax.ShapeDtypeStruct((n_tok, 2 * LANES), BF16),
                   jax.ShapeDtypeStruct((n_tok, GM_WIDTH), BF16),
                   jax.ShapeDtypeStruct((batch, GM_CHUNK, GM_WIDTH), F32)],
        compiler_params=_cparams(("arbitrary",)),
        name="even_inproj",
    )(x2d, w_in_p, kv_norm, gm_g, gm_b, ws_m, bs_full, cos_t, sin_a, sin_b)


def _queries(q_a, qn_g, wq_nope, wq_pe, w_uk, cos_t, sin_a, sin_b):
    qn = _rms_norm(q_a, qn_g).astype(BF16)
    nope = _dot(qn, wq_nope).astype(BF16)
    lat = _dot(nope, w_uk)
    pe = _dot(qn, wq_pe)
    pes = [_rope128(pe[:, h * LANES:(h + 1) * LANES], cos_t, sin_a, sin_b) for h in range(MLA_HEADS)]
    return lat, pes


def _mla_attn_kernel(qa_ref, kcat_ref, qn_ref, wqn_ref, wqp_ref, wuk_ref, wuv_ref,
                     cos_ref, sina_ref, sinb_ref, out_ref, qcat_s, m_s, acc_s):
    tq = qa_ref.shape[0]
    n_rows = MLA_HEADS * tq
    i = pl.program_id(1)
    lat, pes = _queries(qa_ref[...], qn_ref[...], wqn_ref[...], wqp_ref[...], wuk_ref[...],
                        cos_ref[...], sina_ref[...], sinb_ref[...])
    for h in range(MLA_HEADS):
        qcat_s[h * tq:(h + 1) * tq, :] = jnp.concatenate(
            [lat[:, h * LANES:(h + 1) * LANES], pes[h]], axis=1).astype(BF16)

    ones = jnp.ones((tq, LANES), BF16)
    twice = lambda a: jnp.concatenate([a, a], axis=1)

    def scores(j):
        k = kcat_ref[pl.ds(pl.multiple_of(j * tq, tq), tq), :]
        s = _dot_nt(qcat_s[...], k) * ATTN_SCALE
        return s, jnp.concatenate([k[:, :KV_RANK], ones], axis=1)

    s, v1 = scores(i)
    q_pos = lax.broadcasted_iota(jnp.int32, (n_rows, tq), 0) & (tq - 1)
    k_pos = lax.broadcasted_iota(jnp.int32, (n_rows, tq), 1)
    s = jnp.where(k_pos <= q_pos, s, -jnp.inf)
    m0 = jnp.broadcast_to(jnp.max(s, axis=-1, keepdims=True), (n_rows, LANES))
    m_s[...] = m0
    acc_s[...] = _dot(jnp.exp(s - twice(m0)).astype(BF16), v1)

    def body(j, carry):
        s, v1 = scores(j)
        m_old = m_s[...]
        m_new = jnp.maximum(m_old, jnp.max(s, axis=-1, keepdims=True))
        alpha = jnp.exp(m_old - m_new)
        p = jnp.exp(s - twice(m_new))
        acc_s[...] = twice(alpha) * acc_s[...] + _dot(p.astype(BF16), v1)
        m_s[...] = m_new
        return carry

    lax.fori_loop(0, i, body, 0)

    o_all = jnp.concatenate(
        [acc_s[h * tq:(h + 1) * tq, :KV_RANK] / acc_s[h * tq:(h + 1) * tq, KV_RANK:]
         for h in range(MLA_HEADS)], axis=1).astype(BF16)
    out_ref[...] = _dot(o_all, wuv_ref[...]).astype(BF16)


def _mla_attn(q_a, kcat3, qn_g, wq_nope, wq_pe, w_uk, w_uv, cos_t, sin_a, sin_b):
    batch, seq, _ = kcat3.shape
    tq = ATTN_TQ
    nq = seq // tq
    tab = pl.BlockSpec((tq, LANES), lambda b, i: (i, 0))
    return pl.pallas_call(
        _mla_attn_kernel,
        grid=(batch, nq),
        in_specs=[pl.BlockSpec((tq, Q_RANK), lambda b, i: (b * nq + i, 0)),
                  pl.BlockSpec((None, seq, 2 * LANES), lambda b, i: (b, 0, 0)),
                  _const_spec((1, Q_RANK)), _const_spec(wq_nope.shape), _const_spec(wq_pe.shape),
                  _const_spec(w_uk.shape), _const_spec(w_uv.shape), tab, tab, tab],
        out_specs=pl.BlockSpec((tq, MLA_HEADS * V_DIM), lambda b, i: (b * nq + i, 0)),
        out_shape=jax.ShapeDtypeStruct((batch * seq, MLA_HEADS * V_DIM), BF16),
        scratch_shapes=[pltpu.VMEM((MLA_HEADS * tq, 2 * LANES), BF16),
                        pltpu.VMEM((MLA_HEADS * tq, LANES), F32),
                        pltpu.VMEM((MLA_HEADS * tq, 2 * KV_RANK), F32)],
        compiler_params=_cparams(("arbitrary", "arbitrary")),
        name="mla_attn",
    )(q_a, kcat3, qn_g, wq_nope, wq_pe, w_uk, w_uv, cos_t, sin_a, sin_b)


def _mix_ffn_kernel(*refs, n_act, has_pre):
    x_ref = refs[0]
    act_refs = refs[1:1 + n_act]
    k = 1 + n_act
    pre_ref = refs[k] if has_pre else None
    k += int(has_pre)
    wo_refs = refs[k:k + n_act]
    k += n_act
    g1_ref, b1_ref, g2_ref, b2_ref, w1_ref, w2_ref, out_ref = refs[k:k + 7]

    x = x_ref[...]
    f = None
    for idx in range(n_act):
        a = act_refs[idx][...]
        if has_pre and idx == 0:
            a = _dot(a.astype(BF16), pre_ref[...])
        part = _dot(a.astype(BF16), wo_refs[idx][...])
        f = part if f is None else f + part
    x1 = _layer_norm(DN_ALPHA * x + f, g1_ref[...], b1_ref[...])
    x1b = x1.astype(BF16)
    acc = None
    for c in range(FFN_HIDDEN // FFN_HC):
        cols = slice(c * FFN_HC, (c + 1) * FFN_HC)
        h = jnp.maximum(_dot(x1b, w1_ref[:, cols]), 0.0)
        part = _dot((h * h).astype(BF16), w2_ref[cols, :])
        acc = part if acc is None else acc + part
    out_ref[...] = _layer_norm(DN_ALPHA * x1 + acc, g2_ref[...], b2_ref[...])


def _mix_ffn(x2d, acts, pre_w, wos, ln1_g, ln1_b, ln2_g, ln2_b, w1, w2, name):
    n_tok = x2d.shape[0]
    tm = min(FFN_TM, n_tok)
    row = lambda w: pl.BlockSpec((tm, w), lambda i: (i, 0))
    has_pre = pre_w is not None
    in_specs = [row(D_MODEL)] + [row(a.shape[1]) for a in acts]
    args = [x2d] + list(acts)
    if has_pre:
        in_specs.append(_const_spec(pre_w.shape))
        args.append(pre_w)
    in_specs += [_const_spec(w.shape) for w in wos]
    args += list(wos)
    in_specs += [_const_spec((1, D_MODEL))] * 4 + [_const_spec(w1.shape), _const_spec(w2.shape)]
    args += [ln1_g, ln1_b, ln2_g, ln2_b, w1, w2]
    return pl.pallas_call(
        functools.partial(_mix_ffn_kernel, n_act=len(acts), has_pre=has_pre),
        grid=(n_tok // tm,),
        in_specs=in_specs,
        out_specs=row(D_MODEL),
        out_shape=jax.ShapeDtypeStruct((n_tok, D_MODEL), F32),
        compiler_params=_cparams(("arbitrary",)),
        name=name,
    )(*args)


def _even_inproj_sample_kernel(x_ref, w_ref, kvn_ref, gmg_ref, gmb_ref, ws0_ref, bs0_ref,
                               cos_ref, sina_ref, sinb_ref, qn_ref, wqn_ref, wqp_ref, wuk_ref,
                               ckv_ref, kpe_ref, gate_ref, vn_ref, qlat_ref, qpe_ref):
    z = _dot(x_ref[...].astype(BF16), w_ref[...])
    o_c, o_u, o_v, o_k = Q_RANK, Q_RANK + KV_RANK, Q_RANK + KV_RANK + GM_WIDTH, Q_RANK + KV_RANK + 2 * GM_WIDTH
    cos_t, sin_a, sin_b = cos_ref[...], sina_ref[...], sinb_ref[...]
    c_n = _rms_norm(z[:, o_c:o_u], kvn_ref[...])
    ckv_ref[...] = c_n
    kp = _rope128(z[:, o_k:o_k + LANES], cos_t, sin_a, sin_b)
    kpe_ref[...] = kp[:, :ROPE_DIM]
    gu = jax.nn.gelu(z[:, o_u:o_v])
    v_n = _layer_norm(jax.nn.gelu(z[:, o_v:o_k]), gmg_ref[...], gmb_ref[...])
    vn_ref[...] = v_n
    mixed = ws0_ref[...].astype(F32) * v_n.astype(BF16).astype(F32) + bs0_ref[...]
    gate_ref[...] = (gu * mixed).astype(BF16)
    lat, pes = _queries(z[:, :o_c], qn_ref[...], wqn_ref[...], wqp_ref[...], wuk_ref[...],
                        cos_t, sin_a, sin_b)
    qlat_ref[...] = lat
    qpe_ref[...] = jnp.concatenate(pes, axis=1)


def _even_inproj_sample(xs, w_in_p, kv_norm, gm_g, gm_b, ws0, bs0, cos_t, sin_a, sin_b,
                        qn_g, wq_nope, wq_pe, w_uk):
    n = xs.shape[0]
    args = (xs, w_in_p, kv_norm, gm_g, gm_b, ws0, bs0, cos_t, sin_a, sin_b, qn_g, wq_nope, wq_pe, w_uk)
    full = lambda shape: pl.BlockSpec(shape, lambda i: (0,) * len(shape))
    widths = [(KV_RANK, F32), (ROPE_DIM, F32), (GM_WIDTH, BF16), (GM_WIDTH, F32),
              (MLA_HEADS * LANES, F32), (MLA_HEADS * LANES, F32)]
    return pl.pallas_call(
        _even_inproj_sample_kernel,
        grid=(1,),
        in_specs=[full(a.shape) for a in args],
        out_specs=[full((n, w)) for w, _ in widths],
        out_shape=[jax.ShapeDtypeStruct((n, w), dt) for w, dt in widths],
        compiler_params=_cparams(("arbitrary",)),
        name="even_inproj_sample",
    )(*args)


def _decode_attn_kernel(pt_ref, qlat_ref, qpe_ref, ckvs_ref, kpes_ref, ckv_hbm, kpe_hbm, out_ref,
                        ckv_buf, kpe_buf, sem, *, n_pg, n_groups):
    b = pl.program_id(0)
    n_b = pl.num_programs(0)

    def group_copies(seq, grp, slot):
        cps = []
        for k in range(n_pg):
            page = pt_ref[seq, grp * n_pg + k]
            cps.append(pltpu.make_async_copy(ckv_hbm.at[page], ckv_buf.at[slot, k], sem.at[0, slot]))
            cps.append(pltpu.make_async_copy(kpe_hbm.at[page], kpe_buf.at[slot, k], sem.at[1, slot]))
        return cps

    @pl.when(b == 0)
    def _():
        for cp in group_copies(0, 0, 0):
            cp.start()

    qlat = qlat_ref[...].astype(BF16)
    qpe = qpe_ref[...][:, :ROPE_DIM].astype(BF16)
    m = l = acc = None
    for g in range(n_groups):
        slot = g % 2
        if g + 1 < n_groups:
            for cp in group_copies(b, g + 1, 1 - slot):
                cp.start()
        else:
            assert n_groups % 2 == 0

            @pl.when(b + 1 < n_b)
            def _():
                for cp in group_copies(b + 1, 0, 0):
                    cp.start()
        for cp in group_copies(b, g, slot):
            cp.wait()

        cks, ss = [], []
        for k in range(n_pg):
            ck = ckv_buf[slot, k].astype(BF16)
            kp = kpe_buf[slot, k].astype(BF16)
            cks.append(ck)
            ss.append(_dot_nt(qlat, ck) + _dot(qpe, kp))
        s = jnp.concatenate(ss, axis=1) * ATTN_SCALE
        m_cur = jnp.max(s, axis=-1, keepdims=True)
        m_new = m_cur if m is None else jnp.maximum(m, m_cur)
        p = jnp.exp(s - m_new)
        pb = p.astype(BF16)
        pv = None
        for k in range(n_pg):
            part = _dot(pb[:, k * PAGE_SIZE:(k + 1) * PAGE_SIZE], cks[k])
            pv = part if pv is None else pv + part
        p_sum = jnp.sum(p, axis=-1, keepdims=True)
        if m is None:
            l, acc = p_sum, pv
        else:
            alpha = jnp.exp(m - m_new)
            l, acc = alpha * l + p_sum, alpha * acc + pv
        m = m_new

    r = lambda a: a.astype(BF16).astype(F32)
    kv = r(ckvs_ref[...])
    s_self = (jnp.sum(r(qlat_ref[...]) * kv, axis=-1, keepdims=True)
              + jnp.sum(r(qpe_ref[...]) * r(kpes_ref[...]), axis=-1, keepdims=True)) * ATTN_SCALE
    m_n = jnp.maximum(m, s_self)
    a = jnp.exp(m - m_n)
    p_self = jnp.exp(s_self - m_n)
    out_ref[...] = (a * acc + r(p_self) * kv) / (a * l + p_self)


def _decode_attn(page_table, qlat2, qpe2, ckv_s3, kpe_s3, cache_ckv_e, cache_kpe_t):
    dec_batch, n_pages = page_table.shape
    n_pg = DEC_PG
    qspec = pl.BlockSpec((MLA_HEADS, LANES), lambda b, pt: (b, 0))
    self_spec = pl.BlockSpec((None, 1, LANES), lambda b, pt: (b, 0, 0))
    hbm = pl.BlockSpec(memory_space=pl.ANY)
    grid_spec = pltpu.PrefetchScalarGridSpec(
        num_scalar_prefetch=1,
        grid=(dec_batch,),
        in_specs=[qspec, qspec, self_spec, self_spec, hbm, hbm],
        out_specs=qspec,
        scratch_shapes=[pltpu.VMEM((2, n_pg, PAGE_SIZE, KV_RANK), F32),
                        pltpu.VMEM((2, n_pg, ROPE_DIM, PAGE_SIZE), F32),
                        pltpu.SemaphoreType.DMA((2, 2))],
    )
    return pl.pallas_call(
        functools.partial(_decode_attn_kernel, n_pg=n_pg, n_groups=n_pages // n_pg),
        grid_spec=grid_spec,
        out_shape=jax.ShapeDtypeStruct((dec_batch * MLA_HEADS, KV_RANK), F32),
        compiler_params=_cparams(("arbitrary",)),
        name="decode_attn",
    )(page_table, qlat2, qpe2, ckv_s3, kpe_s3, cache_ckv_e, cache_kpe_t)


def _s5_readout(hcat_blocks, u, cw_ref, d_ref, wglu_ref):
    y = jnp.concatenate([_dot(hcat_blocks[k], cw_ref[k]) for k in range(S5_NB)], axis=1)
    y = jax.nn.gelu(y + d_ref[...] * u)
    return (y * jax.nn.sigmoid(_dot(y.astype(BF16), wglu_ref[...]))).astype(BF16)


def _s5_prompt_kernel(x_ref, win_ref, bw_ref, a_ref, cw_ref, d_ref, wglu_ref,
                      yg_ref, hre_ref, him_ref, u_s, bu_s, h_s):
    n_b, seg, _ = x_ref.shape
    half = S5_SLABS // 2

    @pl.when(pl.program_id(0) == 0)
    def _():
        h_s[...] = jnp.zeros(h_s.shape, F32)

    for b in range(n_b):
        u = _dot(x_ref[b].astype(BF16), win_ref[...])
        u_s[b] = u
        ub = u.astype(BF16)
        for k in range(S5_NB):
            r = _dot(ub[:, k * LANES:(k + 1) * LANES], bw_ref[k])
            for q in range(4):
                bu_s[4 * k + q, pl.ds(b, seg, stride=n_b), :] = r[:, q * LANES:(q + 1) * LANES]
                bu_s[half + 4 * k + q, pl.ds(b, seg, stride=n_b), :] = r[:, (4 + q) * LANES:(5 + q) * LANES]

    def step(t, h):
        rows = pl.ds(pl.multiple_of(t * n_b, n_b), n_b)
        hs = bu_s[:, rows, :]
        ar, ai = a_ref[:half], a_ref[half:]
        hr, hi = h[:half], h[half:]
        hn = jnp.concatenate([ar * hr - ai * hi + hs[:half], ar * hi + ai * hr + hs[half:]], axis=0)
        bu_s[:, rows, :] = hn
        return hn

    h_fin = lax.fori_loop(0, seg, step, h_s[...])
    h_s[...] = h_fin
    hre_ref[...] = jnp.concatenate([h_fin[s] for s in range(half)], axis=1)
    him_ref[...] = jnp.concatenate([h_fin[half + s] for s in range(half)], axis=1)

    for b in range(n_b):
        blocks = []
        for k in range(S5_NB):
            slabs = [bu_s[4 * k + q, pl.ds(b, seg, stride=n_b), :] for q in range(4)]
            slabs += [bu_s[half + 4 * k + q, pl.ds(b, seg, stride=n_b), :] for q in range(4)]
            blocks.append(jnp.concatenate(slabs, axis=1).astype(BF16))
        yg_ref[b] = _s5_readout(blocks, u_s[b], cw_ref, d_ref, wglu_ref)


def _s5_prompt(x3, w_in, bw, a_b, cw, d_row, w_glu):
    batch, seq, _ = x3.shape
    seg = S5_L
    return pl.pallas_call(
        _s5_prompt_kernel,
        grid=(seq // seg,),
        in_specs=[pl.BlockSpec((batch, seg, D_MODEL), lambda c: (0, c, 0)),
                  _const_spec(w_in.shape), _const_spec(bw.shape), _const_spec(a_b.shape),
                  _const_spec(cw.shape), _const_spec(d_row.shape), _const_spec(w_glu.shape)],
        out_specs=[pl.BlockSpec((batch, seg, S5_WIDTH), lambda c: (0, c, 0)),
                   pl.BlockSpec((batch, S5_STATES), lambda c: (0, 0)),
                   pl.BlockSpec((batch, S5_STATES), lambda c: (0, 0))],
        out_shape=[jax.ShapeDtypeStruct((batch, seq, S5_WIDTH), BF16),
                   jax.ShapeDtypeStruct((batch, S5_STATES), F32),
                   jax.ShapeDtypeStruct((batch, S5_STATES), F32)],
        scratch_shapes=[pltpu.VMEM((batch, seg, S5_WIDTH), F32),
                        pltpu.VMEM((S5_SLABS, seg * batch, LANES), F32),
                        pltpu.VMEM((S5_SLABS, batch, LANES), F32)],
        compiler_params=_cparams(("arbitrary",)),
        name="s5_prompt",
    )(x3, w_in, bw, a_b, cw, d_row, w_glu)


def _s5_sample_kernel(x_ref, h0r_ref, h0i_ref, win_ref, bw_ref, ar_ref, ai_ref, cw_ref, d_ref, wglu_ref,
                      yg_ref, hre_ref, him_ref):
    u = _dot(x_ref[...].astype(BF16), win_ref[...])
    ub = u.astype(BF16)
    bu = [_dot(ub[:, k * LANES:(k + 1) * LANES], bw_ref[k]) for k in range(S5_NB)]
    w = S5_KB * S5_STATE
    bu_r = jnp.concatenate([r[:, :w] for r in bu], axis=1)
    bu_i = jnp.concatenate([r[:, w:] for r in bu], axis=1)
    ar, ai, h0r, h0i = ar_ref[...], ai_ref[...], h0r_ref[...], h0i_ref[...]
    hr = bu_r + (ar * h0r - ai * h0i)
    hi = bu_i + (ar * h0i + ai * h0r)
    hre_ref[...] = hr
    him_ref[...] = hi
    blocks = [jnp.concatenate([hr[:, k * w:(k + 1) * w], hi[:, k * w:(k + 1) * w]], axis=1).astype(BF16)
              for k in range(S5_NB)]
    yg_ref[...] = _s5_readout(blocks, u, cw_ref, d_ref, wglu_ref)


def _s5_sample(xs, h0r, h0i, w_in, bw, a_r, a_i, cw, d_row, w_glu):
    n = xs.shape[0]
    args = (xs, h0r, h0i, w_in, bw, a_r, a_i, cw, d_row, w_glu)
    full = lambda shape: pl.BlockSpec(shape, lambda i: (0,) * len(shape))
    return pl.pallas_call(
        _s5_sample_kernel,
        grid=(1,),
        in_specs=[full(a.shape) for a in args],
        out_specs=[full((n, S5_WIDTH)), full((n, S5_STATES)), full((n, S5_STATES))],
        out_shape=[jax.ShapeDtypeStruct((n, S5_WIDTH), BF16),
                   jax.ShapeDtypeStruct((n, S5_STATES), F32),
                   jax.ShapeDtypeStruct((n, S5_STATES), F32)],
        compiler_params=_cparams(("arbitrary",)),
        name="s5_sample",
    )(*args)


def _rope_tables(pos):
    freqs = ROPE_THETA ** (-jnp.arange(ROPE_HALF, dtype=F32) / ROPE_HALF)
    ang = pos.astype(F32)[:, None] * freqs
    cos, sin = jnp.cos(ang), jnp.sin(ang)
    zero = jnp.zeros_like(cos)
    pad = jnp.zeros((pos.shape[0], LANES - ROPE_DIM), F32)
    cos_t = jnp.concatenate([cos, cos, pad], axis=1)
    sin_a = jnp.concatenate([zero, sin, pad], axis=1)
    sin_b = jnp.concatenate([-sin, zero, pad], axis=1)
    return cos_t, sin_a, sin_b


def _block_diag(blocks):
    n, r, c = blocks.shape
    eye = jnp.eye(n, dtype=blocks.dtype)
    return jnp.einsum('nrc,nm->nrmc', blocks, eye).reshape(n * r, n * c)


def _even_params(w_in, q_norm, w_q_b, kv_norm, w_kv_b, gm_g, gm_b, gm_w_s, gm_b_s, w_out):
    i1, i2, i3, i4 = Q_RANK, Q_RANK + KV_RANK, Q_RANK + KV_RANK + ROPE_DIM, Q_RANK + KV_RANK + ROPE_DIM + GM_WIDTH
    w_in_p = jnp.concatenate([w_in[:, :i2], w_in[:, i3:], w_in[:, i2:i3],
                              jnp.zeros((D_MODEL, LANES - ROPE_DIM), F32)], axis=1).astype(BF16)
    wq_nope = w_q_b[:, :, :NOPE_DIM].reshape(Q_RANK, MLA_HEADS * NOPE_DIM).astype(BF16)
    pe = w_q_b[:, :, NOPE_DIM:]
    wq_pe = jnp.concatenate([pe, jnp.zeros((Q_RANK, MLA_HEADS, LANES - ROPE_DIM), F32)],
                            axis=2).reshape(Q_RANK, MLA_HEADS * LANES).astype(BF16)
    w_uk = _block_diag(jnp.transpose(w_kv_b[:, :, :NOPE_DIM], (1, 2, 0))).astype(BF16)
    w_uv = _block_diag(jnp.transpose(w_kv_b[:, :, NOPE_DIM:], (1, 0, 2))).astype(BF16)
    causal = jnp.tril(jnp.ones((GM_CHUNK, GM_CHUNK), dtype=bool))
    ws_m = jnp.where(causal[None], gm_w_s, 0).astype(BF16)
    bs_full = jnp.repeat(gm_b_s.T, GM_HEAD_DIM, axis=1)
    ws0 = jnp.repeat(gm_w_s[:, 0, 0], GM_HEAD_DIM)[None, :].astype(BF16)
    bs0 = bs_full[:1]
    n_attn = MLA_HEADS * V_DIM
    return dict(w_in_p=w_in_p, qn=q_norm[None, :], wq_nope=wq_nope, wq_pe=wq_pe, w_uk=w_uk, w_uv=w_uv,
                kvn=kv_norm[None, :], gm_g=gm_g[None, :], gm_b=gm_b[None, :], ws_m=ws_m, bs_full=bs_full,
                ws0=ws0, bs0=bs0, wo_attn=w_out[:n_attn].astype(BF16), wo_gate=w_out[n_attn:].astype(BF16))


def _s5_params(w_in, a_re, a_im, b_re, b_im, c_re, c_im, d, log_dt, w_glu, w_out, batch):
    dt = jnp.exp(log_dt)[:, None]
    ld_r, ld_i = a_re * dt, a_im * dt
    mag = jnp.exp(ld_r)
    ab_r, ab_i = mag * jnp.cos(ld_i), mag * jnp.sin(ld_i)
    den = a_re * a_re + a_im * a_im
    cr = ((ab_r - 1.0) * a_re + ab_i * a_im) / den
    ci = (ab_i * a_re - (ab_r - 1.0) * a_im) / den
    bb_r = cr[..., None] * b_re - ci[..., None] * b_im
    bb_i = cr[..., None] * b_im + ci[..., None] * b_re

    def in_blocks(bb):
        return jnp.stack([_block_diag(jnp.transpose(bb[k * S5_KB:(k + 1) * S5_KB], (0, 2, 1)))
                          for k in range(S5_NB)])

    def out_blocks(cc):
        return jnp.stack([_block_diag(jnp.transpose(cc[k * S5_KB:(k + 1) * S5_KB], (0, 2, 1)))
                          for k in range(S5_NB)])

    bw = jnp.concatenate([in_blocks(bb_r), in_blocks(bb_i)], axis=2).astype(BF16)
    cw = jnp.concatenate([out_blocks(c_re), out_blocks(-c_im)], axis=1).astype(BF16)
    a_r, a_i = ab_r.reshape(1, S5_STATES), ab_i.reshape(1, S5_STATES)
    half = S5_SLABS // 2
    a_b = jnp.concatenate([jnp.broadcast_to(a_r.reshape(half, 1, LANES), (half, batch, LANES)),
                           jnp.broadcast_to(a_i.reshape(half, 1, LANES), (half, batch, LANES))], axis=0)
    return dict(w_in=w_in.astype(BF16), bw=bw, cw=cw, a_r=a_r, a_i=a_i, a_b=a_b,
                d=d.reshape(1, S5_WIDTH), w_glu=w_glu.astype(BF16), w_out=w_out.astype(BF16))


def kernel(x_prompt, x_sample, cache_ckv, cache_kpe, state_s5_re, state_s5_im, page_table, w_in_even, mla_q_norm, mla_w_q_b, mla_kv_norm, mla_w_kv_b, gm_norm_g, gm_norm_b, gm_w_s, gm_b_s, w_out_even, w_in_odd, s5_a_re, s5_a_im, s5_b_re, s5_b_im, s5_c_re, s5_c_im, s5_d, s5_log_dt, s5_w_glu, w_out_odd, ln_mix_g, ln_mix_b, ln_ffn_g, ln_ffn_b, ffn_w1, ffn_w2):
    batch, seq, _ = x_prompt.shape
    dec_batch, dec_seq, _ = x_sample.shape
    assert dec_seq == 1 and seq % INPROJ_TM == 0 and seq % ATTN_TQ == 0 and seq % S5_L == 0
    assert (batch * seq) % FFN_TM == 0 and page_table.shape[1] % DEC_PG == 0
    assert batch == SUBLANES

    xp = x_prompt.reshape(batch * seq, D_MODEL)
    xs = x_sample.reshape(dec_batch, D_MODEL)
    tabs_p = _rope_tables(jnp.arange(seq, dtype=jnp.int32))
    tabs_s = _rope_tables(PAST_LEN + jnp.arange(dec_seq, dtype=jnp.int32))

    outs = {k: [] for k in ("ckv_p", "kpe_p", "ckv_s", "kpe_s", "gmv_p", "gmv_s",
                            "s5re_p", "s5im_p", "s5re_s", "s5im_s")}
    for layer in range(DEPTH):
        ln = (ln_mix_g[layer][None, :], ln_mix_b[layer][None, :],
              ln_ffn_g[layer][None, :], ln_ffn_b[layer][None, :],
              ffn_w1[layer].astype(BF16), ffn_w2[layer].astype(BF16))
        if layer % 2 == 0:
            e = layer // 2
            p = _even_params(w_in_even[e], mla_q_norm[e], mla_w_q_b[e], mla_kv_norm[e], mla_w_kv_b[e],
                             gm_norm_g[e], gm_norm_b[e], gm_w_s[e], gm_b_s[e], w_out_even[e])
            q_a, ckv, kpe, kcat, gate, gmv = _even_inproj(
                xp, p["w_in_p"], p["kvn"], p["gm_g"], p["gm_b"], p["ws_m"], p["bs_full"], *tabs_p, batch, seq)
            attn = _mla_attn(q_a, kcat.reshape(batch, seq, 2 * LANES), p["qn"], p["wq_nope"], p["wq_pe"],
                             p["w_uk"], p["w_uv"], *tabs_p)
            xp = _mix_ffn(xp, [attn, gate], None, [p["wo_attn"], p["wo_gate"]], *ln, name="even_ffn_prompt")
            outs["ckv_p"].append(ckv.reshape(batch, seq, KV_RANK))
            outs["kpe_p"].append(kpe.reshape(batch, seq, ROPE_DIM))
            outs["gmv_p"].append(gmv)
            ckv_s, kpe_s, gate_s, vn_s, qlat_s, qpe_s = _even_inproj_sample(
                xs, p["w_in_p"], p["kvn"], p["gm_g"], p["gm_b"], p["ws0"], p["bs0"], *tabs_s,
                p["qn"], p["wq_nope"], p["wq_pe"], p["w_uk"])
            kpe_pad = jnp.concatenate([kpe_s, jnp.zeros((dec_batch, LANES - ROPE_DIM), F32)], axis=1)
            o_lat = _decode_attn(page_table,
                                 qlat_s.reshape(dec_batch * MLA_HEADS, LANES),
                                 qpe_s.reshape(dec_batch * MLA_HEADS, LANES),
                                 ckv_s.reshape(dec_batch, 1, KV_RANK), kpe_pad.reshape(dec_batch, 1, LANES),
                                 cache_ckv[e], jnp.swapaxes(cache_kpe[e], 1, 2))
            xs = _mix_ffn(xs, [o_lat.reshape(dec_batch, MLA_HEADS * KV_RANK), gate_s], p["w_uv"],
                          [p["wo_attn"], p["wo_gate"]], *ln, name="even_ffn_sample")
            outs["ckv_s"].append(ckv_s.reshape(dec_batch, 1, KV_RANK))
            outs["kpe_s"].append(kpe_s.reshape(dec_batch, 1, ROPE_DIM))
            outs["gmv_s"].append(vn_s.reshape(dec_batch, 1, GM_WIDTH))
        else:
            o = layer // 2
            p = _s5_params(w_in_odd[o], s5_a_re[o], s5_a_im[o], s5_b_re[o], s5_b_im[o], s5_c_re[o],
                           s5_c_im[o], s5_d[o], s5_log_dt[o], s5_w_glu[o], w_out_odd[o], batch)
            yg, hre, him = _s5_prompt(xp.reshape(batch, seq, D_MODEL), p["w_in"], p["bw"], p["a_b"],
                                      p["cw"], p["d"], p["w_glu"])
            xp = _mix_ffn(xp, [yg.reshape(batch * seq, S5_WIDTH)], None, [p["w_out"]], *ln,
                          name="odd_ffn_prompt")
            outs["s5re_p"].append(hre.reshape(batch, S5_GROUPS, S5_STATE))
            outs["s5im_p"].append(him.reshape(batch, S5_GROUPS, S5_STATE))
            yg_s, hre_s, him_s = _s5_sample(
                xs, state_s5_re[o].reshape(dec_batch, S5_STATES), state_s5_im[o].reshape(dec_batch, S5_STATES),
                p["w_in"], p["bw"], p["a_r"], p["a_i"], p["cw"], p["d"], p["w_glu"])
            xs = _mix_ffn(xs, [yg_s], None, [p["w_out"]], *ln, name="odd_ffn_sample")
            outs["s5re_s"].append(hre_s.reshape(dec_batch, S5_GROUPS, S5_STATE))
            outs["s5im_s"].append(him_s.reshape(dec_batch, S5_GROUPS, S5_STATE))

    st = jnp.stack
    return (xp.reshape(batch, seq, D_MODEL), xs.reshape(dec_batch, dec_seq, D_MODEL),
            st(outs["ckv_p"]), st(outs["kpe_p"]), st(outs["ckv_s"]), st(outs["kpe_s"]),
            st(outs["gmv_p"]), st(outs["gmv_s"]),
            st(outs["s5re_p"]), st(outs["s5im_p"]), st(outs["s5re_s"]), st(outs["s5im_s"]))
```

```python
import functools
import math

import jax
import jax.numpy as jnp
from jax import lax
from jax.experimental import pallas as pl
from jax.experimental.pallas import tpu as pltpu

F32 = jnp.float32
BF16 = jnp.bfloat16

D_MODEL = 1024
DEPTH = 2
PAST_LEN = 16384
PAGE_SIZE = 128
MLA_HEADS = 8
NOPE_DIM = 64
ROPE_DIM = 32
ROPE_HALF = ROPE_DIM // 2
V_DIM = 64
Q_RANK = 256
KV_RANK = 128
ROPE_THETA = 10000.0
ATTN_SCALE = 1.0 / math.sqrt(NOPE_DIM + ROPE_DIM)
ATTN_SCALE_LOG2 = ATTN_SCALE * math.log2(math.e)
GM_HEADS = 8
GM_HEAD_DIM = 64
GM_WIDTH = GM_HEADS * GM_HEAD_DIM
GM_CHUNK = 128
S5_GROUP_DIM = 16
S5_GROUPS = 32
S5_STATE = 64
S5_WIDTH = S5_GROUPS * S5_GROUP_DIM
S5_STATES = S5_GROUPS * S5_STATE
FFN_HIDDEN = 4 * D_MODEL
DN_ALPHA = (2 * DEPTH) ** 0.25
LN_EPS = 1e-5
RMS_EPS = 1e-6

LANES = 128
SUBLANES = 8
VMEM_LIMIT_BYTES = 56 * 1024 * 1024

INPROJ_TM = 512
ATTN_TQ = 256
ATTN_ROW_BLOCK = 256
FFN_TM = 512
FFN_HC = 1024
S5_L = 64
DEC_PG = 16
S5_KB = 8
S5_NB = S5_GROUPS // S5_KB
S5_SLABS = 2 * S5_STATES // LANES


def _cparams(sem):
    return pltpu.CompilerParams(dimension_semantics=sem, vmem_limit_bytes=VMEM_LIMIT_BYTES)


def _const_spec(shape):
    n = len(shape)
    return pl.BlockSpec(shape, lambda *_: (0,) * n, pipeline_mode=pl.Buffered(1))


def _layer_norm(x, g, b):
    mu = jnp.mean(x, axis=-1, keepdims=True)
    xc = x - mu
    var = jnp.mean(xc * xc, axis=-1, keepdims=True)
    return xc * lax.rsqrt(var + LN_EPS) * g + b


def _rms_norm(x, g):
    return x * lax.rsqrt(jnp.mean(x * x, axis=-1, keepdims=True) + RMS_EPS) * g


def _rope128(x, cos_t, sin_a, sin_b):
    return (x * cos_t + pltpu.roll(x, ROPE_HALF, 1) * sin_a
            + pltpu.roll(x, LANES - ROPE_HALF, 1) * sin_b)


def _dot(a, b):
    return jnp.dot(a, b, preferred_element_type=F32)


def _dot_nt(a, b):
    return lax.dot_general(a, b, (((1,), (1,)), ((), ())), preferred_element_type=F32)


def _even_inproj_kernel(x_ref, w_ref, kvn_ref, gmg_ref, gmb_ref, ws_ref, bs_ref,
                        cos_ref, sina_ref, sinb_ref,
                        qa_ref, ckv_ref, kpe_ref, kcat_ref, gate_ref, gmv_ref, *, tiles_per_seq):
    tm = x_ref.shape[0]
    z = _dot(x_ref[...].astype(BF16), w_ref[...])
    o_c, o_u, o_v, o_k = Q_RANK, Q_RANK + KV_RANK, Q_RANK + KV_RANK + GM_WIDTH, Q_RANK + KV_RANK + 2 * GM_WIDTH
    qa_ref[...] = z[:, :o_c]
    c_n = _rms_norm(z[:, o_c:o_u], kvn_ref[...])
    ckv_ref[...] = c_n
    kp = _rope128(z[:, o_k:o_k + LANES], cos_ref[...], sina_ref[...], sinb_ref[...])
    kpe_ref[...] = kp[:, :ROPE_DIM]
    kcat_ref[...] = jnp.concatenate([c_n, kp], axis=1).astype(BF16)

    gu = jax.nn.gelu(z[:, o_u:o_v])
    v_n = _layer_norm(jax.nn.gelu(z[:, o_v:o_k]), gmg_ref[...], gmb_ref[...])

    @pl.when(pl.program_id(0) % tiles_per_seq == tiles_per_seq - 1)
    def _():
        gmv_ref[0] = v_n[tm - GM_CHUNK:, :]

    lane = lax.broadcasted_iota(jnp.int32, (GM_CHUNK, LANES), 1)
    lo = lane < GM_HEAD_DIM
    bs = bs_ref[...]
    for ci in range(tm // GM_CHUNK):
        rows = slice(ci * GM_CHUNK, (ci + 1) * GM_CHUNK)
        pieces = []
        for pr in range(GM_WIDTH // LANES):
            r = v_n[rows, pr * LANES:(pr + 1) * LANES]
            m0 = _dot(ws_ref[2 * pr], jnp.where(lo, r, 0.0).astype(BF16))
            m1 = _dot(ws_ref[2 * pr + 1], jnp.where(lo, 0.0, r).astype(BF16))
            pieces.append(m0 + m1)
        mixed = jnp.concatenate(pieces, axis=1) + bs
        gate_ref[rows, :] = (gu[rows, :] * mixed).astype(BF16)


def _even_inproj(x2d, w_in_p, kv_norm, gm_g, gm_b, ws_m, bs_full, cos_t, sin_a, sin_b, batch, seq):
    n_tok = x2d.shape[0]
    tm = INPROJ_TM
    tiles_per_seq = seq // tm
    row = lambda w: pl.BlockSpec((tm, w), lambda i: (i, 0))
    tab = pl.BlockSpec((tm, LANES), lambda i: (i % tiles_per_seq, 0))
    n_in = w_in_p.shape[1]
    return pl.pallas_call(
        functools.partial(_even_inproj_kernel, tiles_per_seq=tiles_per_seq),
        grid=(n_tok // tm,),
        in_specs=[row(D_MODEL), _const_spec((D_MODEL, n_in)), _const_spec((1, KV_RANK)),
                  _const_spec((1, GM_WIDTH)), _const_spec((1, GM_WIDTH)),
                  _const_spec((GM_HEADS, GM_CHUNK, GM_CHUNK)), _const_spec((GM_CHUNK, GM_WIDTH)),
                  tab, tab, tab],
        out_specs=[row(Q_RANK), row(KV_RANK), row(ROPE_DIM), row(2 * LANES), row(GM_WIDTH),
                   pl.BlockSpec((1, GM_CHUNK, GM_WIDTH), lambda i: (i // tiles_per_seq, 0, 0))],
        out_shape=[jax.ShapeDtypeStruct((n_tok, Q_RANK), F32),
                   jax.ShapeDtypeStruct((n_tok, KV_RANK), F32),
                   jax.ShapeDtypeStruct((n_tok, ROPE_DIM), F32),
                   jax.ShapeDtypeStruct((n_tok, 2 * LANES), BF16),
                   jax.ShapeDtypeStruct((n_tok, GM_WIDTH), BF16),
                   jax.ShapeDtypeStruct((batch, GM_CHUNK, GM_WIDTH), F32)],
        compiler_params=_cparams(("arbitrary",)),
        name="even_inproj",
    )(x2d, w_in_p, kv_norm, gm_g, gm_b, ws_m, bs_full, cos_t, sin_a, sin_b)


def _queries(q_a, qn_g, wq_nope, wq_pe, wuk_ref, cos_t, sin_a, sin_b):
    qn = _rms_norm(q_a, qn_g).astype(BF16)
    nope = _dot(qn, wq_nope).astype(BF16)
    lat = jnp.concatenate([_dot(nope[:, pr * LANES:(pr + 1) * LANES], wuk_ref[pr])
                           for pr in range(MLA_HEADS // 2)], axis=1)
    pe = _dot(qn, wq_pe)
    n = MLA_HEADS * LANES
    sin_s = sin_a + sin_b
    pes = [pe[:, h * LANES:(h + 1) * LANES] * cos_t + pe[:, n + h * LANES:n + (h + 1) * LANES] * sin_s
           for h in range(MLA_HEADS)]
    return lat, pes


def _mla_attn_kernel(qa_lo_ref, qa_hi_ref, kcat_ref, qn_ref, wqn_ref, wqp_ref, wuk_ref, wuv_ref,
                     cos_lo_ref, sina_lo_ref, sinb_lo_ref, cos_hi_ref, sina_hi_ref, sinb_hi_ref,
                     out_hbm, qcat_s, m_s, acc_s, obuf, osem, *, nq):
    tq = qa_lo_ref.shape[0]
    n_rows = MLA_HEADS * tq
    b, i = pl.program_id(0), pl.program_id(1)
    step = b * pl.num_programs(1) + i
    n_steps = pl.num_programs(0) * pl.num_programs(1)
    q_tiles = (i, nq - 1 - i)

    def out_copies():
        return [pltpu.make_async_copy(
            obuf.at[t], out_hbm.at[pl.ds(pl.multiple_of((b * nq + q_tiles[t]) * tq, tq), tq), :], osem.at[t])
            for t in range(2)]

    qa = (qa_lo_ref, qa_hi_ref)
    tabs = ((cos_lo_ref, sina_lo_ref, sinb_lo_ref), (cos_hi_ref, sina_hi_ref, sinb_hi_ref))
    for t in range(2):
        lat, pes = _queries(qa[t][...], qn_ref[...], wqn_ref[...], wqp_ref[...], wuk_ref,
                            *(r[...] for r in tabs[t]))
        for h in range(MLA_HEADS):
            qcat_s[t, h * tq:(h + 1) * tq, :] = (ATTN_SCALE_LOG2 * jnp.concatenate(
                [lat[:, h * LANES:(h + 1) * LANES], pes[h]], axis=1)).astype(BF16)

    lanes_x = lambda a, n: jnp.concatenate([a] * n, axis=1)
    rb = ATTN_ROW_BLOCK
    ones = jnp.ones((tq, LANES), BF16)

    def keys(kv_tile):
        k = kcat_ref[pl.ds(pl.multiple_of(kv_tile * tq, tq), tq), :]
        return k, jnp.concatenate([k[:, :KV_RANK], ones], axis=1)

    q_pos = lax.broadcasted_iota(jnp.int32, (rb, tq), 0) & (tq - 1)
    k_pos = lax.broadcasted_iota(jnp.int32, (rb, tq), 1)
    causal = k_pos <= q_pos
    for t in range(2):
        k, v1 = keys(q_tiles[t])
        for r0 in range(0, n_rows, rb):
            rows = slice(r0, r0 + rb)
            s = jnp.where(causal, _dot_nt(qcat_s[t, rows, :], k), -jnp.inf)
            m0 = jnp.broadcast_to(jnp.max(s, axis=-1, keepdims=True), (rb, LANES))
            m_s[t, rows, :] = m0
            acc_s[t, rows, :] = _dot(jnp.exp2(s - lanes_x(m0, tq // LANES)).astype(BF16), v1)

    for u in range(nq - 1):
        hi = u >= i
        t = hi.astype(jnp.int32)
        k, v1 = keys(jnp.where(hi, u - i, u))
        for r0 in range(0, n_rows, rb):
            rows = slice(r0, r0 + rb)
            s = _dot_nt(qcat_s[t, rows, :], k)
            m_old = m_s[t, rows, :]
            m_new = jnp.maximum(m_old, jnp.max(s, axis=-1, keepdims=True))
            alpha = jnp.exp2(m_old - m_new)
            p = jnp.exp2(s - lanes_x(m_new, tq // LANES))
            acc_s[t, rows, :] = lanes_x(alpha, 2) * acc_s[t, rows, :] + _dot(p.astype(BF16), v1)
            m_s[t, rows, :] = m_new

    @pl.when(step > 0)
    def _():
        for cp in out_copies():
            cp.wait()

    for t in range(2):
        o_all = jnp.concatenate(
            [acc_s[t, h * tq:(h + 1) * tq, :KV_RANK] / acc_s[t, h * tq:(h + 1) * tq, KV_RANK:]
             for h in range(MLA_HEADS)], axis=1).astype(BF16)
        obuf[t] = jnp.concatenate(
            [_dot(o_all[:, pr * 2 * LANES:(pr + 1) * 2 * LANES], wuv_ref[pr]) for pr in range(MLA_HEADS // 2)],
            axis=1).astype(BF16)
    for cp in out_copies():
        cp.start()

    @pl.when(step == n_steps - 1)
    def _():
        for cp in out_copies():
            cp.wait()


def _mla_attn(q_a, kcat, qn_g, wq_nope, wq_pe, w_uk, w_uv, cos_t, sin_a, sin_b, batch, seq):
    tq = ATTN_TQ
    nq = seq // tq
    n_attn = MLA_HEADS * V_DIM
    lo = lambda b, i: i
    hi = lambda b, i: nq - 1 - i
    tab = lambda sel: pl.BlockSpec((tq, LANES), lambda b, i: (sel(b, i), 0))
    qa_spec = lambda sel: pl.BlockSpec((tq, Q_RANK), lambda b, i: (b * nq + sel(b, i), 0))
    return pl.pallas_call(
        functools.partial(_mla_attn_kernel, nq=nq),
        grid=(batch, nq // 2),
        in_specs=[qa_spec(lo), qa_spec(hi),
                  pl.BlockSpec((None, seq, 2 * LANES), lambda b, i: (b, 0, 0)),
                  _const_spec((1, Q_RANK)), _const_spec(wq_nope.shape), _const_spec(wq_pe.shape),
                  _const_spec(w_uk.shape), _const_spec(w_uv.shape),
                  tab(lo), tab(lo), tab(lo), tab(hi), tab(hi), tab(hi)],
        out_specs=pl.BlockSpec(memory_space=pl.ANY),
        out_shape=jax.ShapeDtypeStruct((batch * seq, n_attn), BF16),
        scratch_shapes=[pltpu.VMEM((2, MLA_HEADS * tq, 2 * LANES), BF16),
                        pltpu.VMEM((2, MLA_HEADS * tq, LANES), F32),
                        pltpu.VMEM((2, MLA_HEADS * tq, 2 * KV_RANK), F32),
                        pltpu.VMEM((2, tq, n_attn), BF16),
                        pltpu.SemaphoreType.DMA((2,))],
        compiler_params=_cparams(("arbitrary", "arbitrary")),
        name="mla_attn",
    )(q_a, q_a, kcat.reshape(batch, seq, 2 * LANES), qn_g, wq_nope, wq_pe, w_uk, w_uv,
      cos_t, sin_a, sin_b, cos_t, sin_a, sin_b)


def _mix_ffn_kernel(*refs, n_act, has_pre):
    x_ref = refs[0]
    act_refs = refs[1:1 + n_act]
    k = 1 + n_act
    pre_ref = refs[k] if has_pre else None
    k += int(has_pre)
    wo_refs = refs[k:k + n_act]
    k += n_act
    g1_ref, b1_ref, g2_ref, b2_ref, w1_ref, w2_ref, out_ref = refs[k:k + 7]

    x = x_ref[...]
    f = None
    for idx in range(n_act):
        a = act_refs[idx][...]
        if has_pre and idx == 0:
            a = _dot(a.astype(BF16), pre_ref[...])
        part = _dot(a.astype(BF16), wo_refs[idx][...])
        f = part if f is None else f + part
    x1 = _layer_norm(DN_ALPHA * x + f, g1_ref[...], b1_ref[...])
    x1b = x1.astype(BF16)
    acc = None
    for c in range(FFN_HIDDEN // FFN_HC):
        cols = slice(c * FFN_HC, (c + 1) * FFN_HC)
        h = jnp.maximum(_dot(x1b, w1_ref[:, cols]), 0.0)
        part = _dot((h * h).astype(BF16), w2_ref[cols, :])
        acc = part if acc is None else acc + part
    out_ref[...] = _layer_norm(DN_ALPHA * x1 + acc, g2_ref[...], b2_ref[...])


def _mix_ffn(x2d, acts, pre_w, wos, ln1_g, ln1_b, ln2_g, ln2_b, w1, w2, name):
    n_tok = x2d.shape[0]
    tm = min(FFN_TM, n_tok)
    row = lambda w: pl.BlockSpec((tm, w), lambda i: (i, 0))
    has_pre = pre_w is not None
    in_specs = [row(D_MODEL)] + [row(a.shape[1]) for a in acts]
    args = [x2d] + list(acts)
    if has_pre:
        in_specs.append(_const_spec(pre_w.shape))
        args.append(pre_w)
    in_specs += [_const_spec(w.shape) for w in wos]
    args += list(wos)
    in_specs += [_const_spec((1, D_MODEL))] * 4 + [_const_spec(w1.shape), _const_spec(w2.shape)]
    args += [ln1_g, ln1_b, ln2_g, ln2_b, w1, w2]
    return pl.pallas_call(
        functools.partial(_mix_ffn_kernel, n_act=len(acts), has_pre=has_pre),
        grid=(n_tok // tm,),
        in_specs=in_specs,
        out_specs=row(D_MODEL),
        out_shape=jax.ShapeDtypeStruct((n_tok, D_MODEL), F32),
        compiler_params=_cparams(("arbitrary",)),
        name=name,
    )(*args)


def _even_inproj_sample_kernel(x_ref, w_ref, kvn_ref, gmg_ref, gmb_ref, ws0_ref, bs0_ref,
                               cos_ref, sina_ref, sinb_ref, qn_ref, wqn_ref, wqp_ref, wuk_ref,
                               ckv_ref, kpe_ref, gate_ref, vn_ref, qlat_ref, qpe_ref):
    z = _dot(x_ref[...].astype(BF16), w_ref[...])
    o_c, o_u, o_v, o_k = Q_RANK, Q_RANK + KV_RANK, Q_RANK + KV_RANK + GM_WIDTH, Q_RANK + KV_RANK + 2 * GM_WIDTH
    cos_t, sin_a, sin_b = cos_ref[...], sina_ref[...], sinb_ref[...]
    c_n = _rms_norm(z[:, o_c:o_u], kvn_ref[...])
    ckv_ref[...] = c_n
    kp = _rope128(z[:, o_k:o_k + LANES], cos_t, sin_a, sin_b)
    kpe_ref[...] = kp[:, :ROPE_DIM]
    gu = jax.nn.gelu(z[:, o_u:o_v])
    v_n = _layer_norm(jax.nn.gelu(z[:, o_v:o_k]), gmg_ref[...], gmb_ref[...])
    vn_ref[...] = v_n
    mixed = ws0_ref[...].astype(F32) * v_n.astype(BF16).astype(F32) + bs0_ref[...]
    gate_ref[...] = (gu * mixed).astype(BF16)
    lat, pes = _queries(z[:, :o_c], qn_ref[...], wqn_ref[...], wqp_ref[...], wuk_ref,
                        cos_t, sin_a, sin_b)
    qlat_ref[...] = lat
    qpe_ref[...] = jnp.concatenate(pes, axis=1)


def _even_inproj_sample(xs, w_in_p, kv_norm, gm_g, gm_b, ws0, bs0, cos_t, sin_a, sin_b,
                        qn_g, wq_nope, wq_pe, w_uk):
    n = xs.shape[0]
    args = (xs, w_in_p, kv_norm, gm_g, gm_b, ws0, bs0, cos_t, sin_a, sin_b, qn_g, wq_nope, wq_pe, w_uk)
    full = lambda shape: pl.BlockSpec(shape, lambda i: (0,) * len(shape))
    widths = [(KV_RANK, F32), (ROPE_DIM, F32), (GM_WIDTH, BF16), (GM_WIDTH, F32),
              (MLA_HEADS * LANES, F32), (MLA_HEADS * LANES, F32)]
    return pl.pallas_call(
        _even_inproj_sample_kernel,
        grid=(1,),
        in_specs=[full(a.shape) for a in args],
        out_specs=[full((n, w)) for w, _ in widths],
        out_shape=[jax.ShapeDtypeStruct((n, w), dt) for w, dt in widths],
        compiler_params=_cparams(("arbitrary",)),
        name="even_inproj_sample",
    )(*args)


def _decode_attn_kernel(pt_ref, qlat_ref, qpe_ref, ckvs_ref, kpes_ref, ckv_hbm, kpe_hbm, out_ref,
                        ckv_buf, kpe_buf, sem, *, n_pg, n_groups):
    b = pl.program_id(0)
    n_b = pl.num_programs(0)

    def group_copies(seq, grp, slot):
        cps = []
        for k in range(n_pg):
            page = pt_ref[seq, grp * n_pg + k]
            cps.append(pltpu.make_async_copy(ckv_hbm.at[page], ckv_buf.at[slot, k], sem.at[0, slot]))
            cps.append(pltpu.make_async_copy(kpe_hbm.at[page], kpe_buf.at[slot, k], sem.at[1, slot]))
        return cps

    @pl.when(b == 0)
    def _():
        for cp in group_copies(0, 0, 0):
            cp.start()

    qlat = qlat_ref[...].astype(BF16)
    qpe = qpe_ref[...][:, :ROPE_DIM].astype(BF16)
    m = l = acc = None
    for g in range(n_groups):
        slot = g % 2
        if g + 1 < n_groups:
            for cp in group_copies(b, g + 1, 1 - slot):
                cp.start()
        else:
            assert n_groups % 2 == 0

            @pl.when(b + 1 < n_b)
            def _():
                for cp in group_copies(b + 1, 0, 0):
                    cp.start()
        for cp in group_copies(b, g, slot):
            cp.wait()

        cks, ss = [], []
        for k in range(n_pg):
            ck = ckv_buf[slot, k].astype(BF16)
            kp = kpe_buf[slot, k].astype(BF16)
            cks.append(ck)
            ss.append(_dot_nt(qlat, ck) + _dot(qpe, kp))
        s = jnp.concatenate(ss, axis=1) * ATTN_SCALE
        m_cur = jnp.max(s, axis=-1, keepdims=True)
        m_new = m_cur if m is None else jnp.maximum(m, m_cur)
        p = jnp.exp(s - m_new)
        pb = p.astype(BF16)
        pv = None
        for k in range(n_pg):
            part = _dot(pb[:, k * PAGE_SIZE:(k + 1) * PAGE_SIZE], cks[k])
            pv = part if pv is None else pv + part
        p_sum = jnp.sum(p, axis=-1, keepdims=True)
        if m is None:
            l, acc = p_sum, pv
        else:
            alpha = jnp.exp(m - m_new)
            l, acc = alpha * l + p_sum, alpha * acc + pv
        m = m_new

    r = lambda a: a.astype(BF16).astype(F32)
    kv = r(ckvs_ref[...])
    s_self = (jnp.sum(r(qlat_ref[...]) * kv, axis=-1, keepdims=True)
              + jnp.sum(r(qpe_ref[...]) * r(kpes_ref[...]), axis=-1, keepdims=True)) * ATTN_SCALE
    m_n = jnp.maximum(m, s_self)
    a = jnp.exp(m - m_n)
    p_self = jnp.exp(s_self - m_n)
    out_ref[...] = (a * acc + r(p_self) * kv) / (a * l + p_self)


def _decode_attn(page_table, qlat2, qpe2, ckv_s3, kpe_s3, cache_ckv_e, cache_kpe_t):
    dec_batch, n_pages = page_table.shape
    n_pg = DEC_PG
    qspec = pl.BlockSpec((MLA_HEADS, LANES), lambda b, pt: (b, 0))
    self_spec = pl.BlockSpec((None, 1, LANES), lambda b, pt: (b, 0, 0))
    hbm = pl.BlockSpec(memory_space=pl.ANY)
    grid_spec = pltpu.PrefetchScalarGridSpec(
        num_scalar_prefetch=1,
        grid=(dec_batch,),
        in_specs=[qspec, qspec, self_spec, self_spec, hbm, hbm],
        out_specs=qspec,
        scratch_shapes=[pltpu.VMEM((2, n_pg, PAGE_SIZE, KV_RANK), F32),
                        pltpu.VMEM((2, n_pg, ROPE_DIM, PAGE_SIZE), F32),
                        pltpu.SemaphoreType.DMA((2, 2))],
    )
    return pl.pallas_call(
        functools.partial(_decode_attn_kernel, n_pg=n_pg, n_groups=n_pages // n_pg),
        grid_spec=grid_spec,
        out_shape=jax.ShapeDtypeStruct((dec_batch * MLA_HEADS, KV_RANK), F32),
        compiler_params=_cparams(("arbitrary",)),
        name="decode_attn",
    )(page_table, qlat2, qpe2, ckv_s3, kpe_s3, cache_ckv_e, cache_kpe_t)


def _s5_readout(hcat_blocks, u, cw_ref, d_ref, wglu_ref):
    y = jnp.concatenate([_dot(hcat_blocks[k], cw_ref[k]) for k in range(S5_NB)], axis=1)
    y = jax.nn.gelu(y + d_ref[...] * u)
    return (y * jax.nn.sigmoid(_dot(y.astype(BF16), wglu_ref[...]))).astype(BF16)


def _s5_prompt_kernel(x_ref, win_ref, bw_ref, a_ref, cw_ref, d_ref, wglu_ref,
                      yg_ref, hre_ref, him_ref, u_s, bu_s, h_s):
    n_b, seg, _ = x_ref.shape
    half = S5_SLABS // 2

    @pl.when(pl.program_id(0) == 0)
    def _():
        h_s[...] = jnp.zeros(h_s.shape, F32)

    for b in range(n_b):
        u = _dot(x_ref[b].astype(BF16), win_ref[...])
        u_s[b] = u
        ub = u.astype(BF16)
        for k in range(S5_NB):
            r = _dot(ub[:, k * LANES:(k + 1) * LANES], bw_ref[k])
            for q in range(4):
                bu_s[4 * k + q, pl.ds(b, seg, stride=n_b), :] = r[:, q * LANES:(q + 1) * LANES]
                bu_s[half + 4 * k + q, pl.ds(b, seg, stride=n_b), :] = r[:, (4 + q) * LANES:(5 + q) * LANES]

    def step(t, h):
        rows = pl.ds(pl.multiple_of(t * n_b, n_b), n_b)
        hs = bu_s[:, rows, :]
        ar, ai = a_ref[:half], a_ref[half:]
        hr, hi = h[:half], h[half:]
        hn = jnp.concatenate([ar * hr - ai * hi + hs[:half], ar * hi + ai * hr + hs[half:]], axis=0)
        bu_s[:, rows, :] = hn
        return hn

    h_fin = lax.fori_loop(0, seg, step, h_s[...])
    h_s[...] = h_fin
    hre_ref[...] = jnp.concatenate([h_fin[s] for s in range(half)], axis=1)
    him_ref[...] = jnp.concatenate([h_fin[half + s] for s in range(half)], axis=1)

    for b in range(n_b):
        blocks = []
        for k in range(S5_NB):
            slabs = [bu_s[4 * k + q, pl.ds(b, seg, stride=n_b), :] for q in range(4)]
            slabs += [bu_s[half + 4 * k + q, pl.ds(b, seg, stride=n_b), :] for q in range(4)]
            blocks.append(jnp.concatenate(slabs, axis=1).astype(BF16))
        yg_ref[b] = _s5_readout(blocks, u_s[b], cw_ref, d_ref, wglu_ref)


def _s5_prompt(x3, w_in, bw, a_b, cw, d_row, w_glu):
    batch, seq, _ = x3.shape
    seg = S5_L
    return pl.pallas_call(
        _s5_prompt_kernel,
        grid=(seq // seg,),
        in_specs=[pl.BlockSpec((batch, seg, D_MODEL), lambda c: (0, c, 0)),
                  _const_spec(w_in.shape), _const_spec(bw.shape), _const_spec(a_b.shape),
                  _const_spec(cw.shape), _const_spec(d_row.shape), _const_spec(w_glu.shape)],
        out_specs=[pl.BlockSpec((batch, seg, S5_WIDTH), lambda c: (0, c, 0)),
                   pl.BlockSpec((batch, S5_STATES), lambda c: (0, 0)),
                   pl.BlockSpec((batch, S5_STATES), lambda c: (0, 0))],
        out_shape=[jax.ShapeDtypeStruct((batch, seq, S5_WIDTH), BF16),
                   jax.ShapeDtypeStruct((batch, S5_STATES), F32),
                   jax.ShapeDtypeStruct((batch, S5_STATES), F32)],
        scratch_shapes=[pltpu.VMEM((batch, seg, S5_WIDTH), F32),
                        pltpu.VMEM((S5_SLABS, seg * batch, LANES), F32),
                        pltpu.VMEM((S5_SLABS, batch, LANES), F32)],
        compiler_params=_cparams(("arbitrary",)),
        name="s5_prompt",
    )(x3, w_in, bw, a_b, cw, d_row, w_glu)


def _s5_sample_kernel(x_ref, h0r_ref, h0i_ref, win_ref, bw_ref, ar_ref, ai_ref, cw_ref, d_ref, wglu_ref,
                      yg_ref, hre_ref, him_ref):
    u = _dot(x_ref[...].astype(BF16), win_ref[...])
    ub = u.astype(BF16)
    bu = [_dot(ub[:, k * LANES:(k + 1) * LANES], bw_ref[k]) for k in range(S5_NB)]
    w = S5_KB * S5_STATE
    bu_r = jnp.concatenate([r[:, :w] for r in bu], axis=1)
    bu_i = jnp.concatenate([r[:, w:] for r in bu], axis=1)
    ar, ai, h0r, h0i = ar_ref[...], ai_ref[...], h0r_ref[...], h0i_ref[...]
    hr = bu_r + (ar * h0r - ai * h0i)
    hi = bu_i + (ar * h0i + ai * h0r)
    hre_ref[...] = hr
    him_ref[...] = hi
    blocks = [jnp.concatenate([hr[:, k * w:(k + 1) * w], hi[:, k * w:(k + 1) * w]], axis=1).astype(BF16)
              for k in range(S5_NB)]
    yg_ref[...] = _s5_readout(blocks, u, cw_ref, d_ref, wglu_ref)


def _s5_sample(xs, h0r, h0i, w_in, bw, a_r, a_i, cw, d_row, w_glu):
    n = xs.shape[0]
    args = (xs, h0r, h0i, w_in, bw, a_r, a_i, cw, d_row, w_glu)
    full = lambda shape: pl.BlockSpec(shape, lambda i: (0,) * len(shape))
    return pl.pallas_call(
        _s5_sample_kernel,
        grid=(1,),
        in_specs=[full(a.shape) for a in args],
        out_specs=[full((n, S5_WIDTH)), full((n, S5_STATES)), full((n, S5_STATES))],
        out_shape=[jax.ShapeDtypeStruct((n, S5_WIDTH), BF16),
                   jax.ShapeDtypeStruct((n, S5_STATES), F32),
                   jax.ShapeDtypeStruct((n, S5_STATES), F32)],
        compiler_params=_cparams(("arbitrary",)),
        name="s5_sample",
    )(*args)


def _rope_tables(pos):
    freqs = ROPE_THETA ** (-jnp.arange(ROPE_HALF, dtype=F32) / ROPE_HALF)
    ang = pos.astype(F32)[:, None] * freqs
    cos, sin = jnp.cos(ang), jnp.sin(ang)
    zero = jnp.zeros_like(cos)
    pad = jnp.zeros((pos.shape[0], LANES - ROPE_DIM), F32)
    cos_t = jnp.concatenate([cos, cos, pad], axis=1)
    sin_a = jnp.concatenate([zero, sin, pad], axis=1)
    sin_b = jnp.concatenate([-sin, zero, pad], axis=1)
    return cos_t, sin_a, sin_b


def _block_diag(blocks):
    n, r, c = blocks.shape
    eye = jnp.eye(n, dtype=blocks.dtype)
    return jnp.einsum('nrc,nm->nrmc', blocks, eye).reshape(n * r, n * c)


def _even_params(w_in, q_norm, w_q_b, kv_norm, w_kv_b, gm_g, gm_b, gm_w_s, gm_b_s, w_out):
    i1, i2, i3, i4 = Q_RANK, Q_RANK + KV_RANK, Q_RANK + KV_RANK + ROPE_DIM, Q_RANK + KV_RANK + ROPE_DIM + GM_WIDTH
    w_in_p = jnp.concatenate([w_in[:, :i2], w_in[:, i3:], w_in[:, i2:i3],
                              jnp.zeros((D_MODEL, LANES - ROPE_DIM), F32)], axis=1).astype(BF16)
    wq_nope = w_q_b[:, :, :NOPE_DIM].reshape(Q_RANK, MLA_HEADS * NOPE_DIM).astype(BF16)
    pe = w_q_b[:, :, NOPE_DIM:]
    pe_sw = jnp.concatenate([pe[:, :, ROPE_HALF:], pe[:, :, :ROPE_HALF]], axis=2)
    lane_pad = jnp.zeros((Q_RANK, MLA_HEADS, LANES - ROPE_DIM), F32)
    wq_pe = jnp.concatenate(
        [jnp.concatenate([w, lane_pad], axis=2).reshape(Q_RANK, MLA_HEADS * LANES) for w in (pe, pe_sw)],
        axis=1).astype(BF16)
    uk = jnp.transpose(w_kv_b[:, :, :NOPE_DIM], (1, 2, 0))
    uv = jnp.transpose(w_kv_b[:, :, NOPE_DIM:], (1, 0, 2))
    pairs = range(MLA_HEADS // 2)
    w_uk = jnp.stack([_block_diag(uk[2 * pr:2 * pr + 2]) for pr in pairs]).astype(BF16)
    w_uv_p = jnp.stack([_block_diag(uv[2 * pr:2 * pr + 2]) for pr in pairs]).astype(BF16)
    w_uv = _block_diag(uv).astype(BF16)
    causal = jnp.tril(jnp.ones((GM_CHUNK, GM_CHUNK), dtype=bool))
    ws_m = jnp.where(causal[None], gm_w_s, 0).astype(BF16)
    bs_full = jnp.repeat(gm_b_s.T, GM_HEAD_DIM, axis=1)
    ws0 = jnp.repeat(gm_w_s[:, 0, 0], GM_HEAD_DIM)[None, :].astype(BF16)
    bs0 = bs_full[:1]
    n_attn = MLA_HEADS * V_DIM
    return dict(w_in_p=w_in_p, qn=q_norm[None, :], wq_nope=wq_nope, wq_pe=wq_pe, w_uk=w_uk, w_uv=w_uv, w_uv_p=w_uv_p,
                kvn=kv_norm[None, :], gm_g=gm_g[None, :], gm_b=gm_b[None, :], ws_m=ws_m, bs_full=bs_full,
                ws0=ws0, bs0=bs0, wo_attn=w_out[:n_attn].astype(BF16), wo_gate=w_out[n_attn:].astype(BF16))


def _s5_params(w_in, a_re, a_im, b_re, b_im, c_re, c_im, d, log_dt, w_glu, w_out, batch):
    dt = jnp.exp(log_dt)[:, None]
    ld_r, ld_i = a_re * dt, a_im * dt
    mag = jnp.exp(ld_r)
    ab_r, ab_i = mag * jnp.cos(ld_i), mag * jnp.sin(ld_i)
    den = a_re * a_re + a_im * a_im
    cr = ((ab_r - 1.0) * a_re + ab_i * a_im) / den
    ci = (ab_i * a_re - (ab_r - 1.0) * a_im) / den
    bb_r = cr[..., None] * b_re - ci[..., None] * b_im
    bb_i = cr[..., None] * b_im + ci[..., None] * b_re

    def in_blocks(bb):
        return jnp.stack([_block_diag(jnp.transpose(bb[k * S5_KB:(k + 1) * S5_KB], (0, 2, 1)))
                          for k in range(S5_NB)])

    def out_blocks(cc):
        return jnp.stack([_block_diag(jnp.transpose(cc[k * S5_KB:(k + 1) * S5_KB], (0, 2, 1)))
                          for k in range(S5_NB)])

    bw = jnp.concatenate([in_blocks(bb_r), in_blocks(bb_i)], axis=2).astype(BF16)
    cw = jnp.concatenate([out_blocks(c_re), out_blocks(-c_im)], axis=1).astype(BF16)
    a_r, a_i = ab_r.reshape(1, S5_STATES), ab_i.reshape(1, S5_STATES)
    half = S5_SLABS // 2
    a_b = jnp.concatenate([jnp.broadcast_to(a_r.reshape(half, 1, LANES), (half, batch, LANES)),
                           jnp.broadcast_to(a_i.reshape(half, 1, LANES), (half, batch, LANES))], axis=0)
    return dict(w_in=w_in.astype(BF16), bw=bw, cw=cw, a_r=a_r, a_i=a_i, a_b=a_b,
                d=d.reshape(1, S5_WIDTH), w_glu=w_glu.astype(BF16), w_out=w_out.astype(BF16))


def kernel(x_prompt, x_sample, cache_ckv, cache_kpe, state_s5_re, state_s5_im, page_table, w_in_even, mla_q_norm, mla_w_q_b, mla_kv_norm, mla_w_kv_b, gm_norm_g, gm_norm_b, gm_w_s, gm_b_s, w_out_even, w_in_odd, s5_a_re, s5_a_im, s5_b_re, s5_b_im, s5_c_re, s5_c_im, s5_d, s5_log_dt, s5_w_glu, w_out_odd, ln_mix_g, ln_mix_b, ln_ffn_g, ln_ffn_b, ffn_w1, ffn_w2):
    batch, seq, _ = x_prompt.shape
    dec_batch, dec_seq, _ = x_sample.shape
    assert dec_seq == 1 and seq % INPROJ_TM == 0 and seq % ATTN_TQ == 0 and seq % S5_L == 0
    assert (seq // ATTN_TQ) % 2 == 0
    assert (batch * seq) % FFN_TM == 0 and page_table.shape[1] % DEC_PG == 0
    assert batch == SUBLANES

    xp = x_prompt.reshape(batch * seq, D_MODEL)
    xs = x_sample.reshape(dec_batch, D_MODEL)
    tabs_p = _rope_tables(jnp.arange(seq, dtype=jnp.int32))
    tabs_s = _rope_tables(PAST_LEN + jnp.arange(dec_seq, dtype=jnp.int32))

    outs = {k: [] for k in ("ckv_p", "kpe_p", "ckv_s", "kpe_s", "gmv_p", "gmv_s",
                            "s5re_p", "s5im_p", "s5re_s", "s5im_s")}
    for layer in range(DEPTH):
        ln = (ln_mix_g[layer][None, :], ln_mix_b[layer][None, :],
              ln_ffn_g[layer][None, :], ln_ffn_b[layer][None, :],
              ffn_w1[layer].astype(BF16), ffn_w2[layer].astype(BF16))
        if layer % 2 == 0:
            e = layer // 2
            p = _even_params(w_in_even[e], mla_q_norm[e], mla_w_q_b[e], mla_kv_norm[e], mla_w_kv_b[e],
                             gm_norm_g[e], gm_norm_b[e], gm_w_s[e], gm_b_s[e], w_out_even[e])
            q_a, ckv, kpe, kcat, gate, gmv = _even_inproj(
                xp, p["w_in_p"], p["kvn"], p["gm_g"], p["gm_b"], p["ws_m"], p["bs_full"], *tabs_p, batch, seq)
            attn = _mla_attn(q_a, kcat, p["qn"], p["wq_nope"], p["wq_pe"], p["w_uk"], p["w_uv_p"], *tabs_p,
                             batch, seq)
            xp = _mix_ffn(xp, [attn, gate], None, [p["wo_attn"], p["wo_gate"]], *ln, name="even_ffn_prompt")
            outs["ckv_p"].append(ckv.reshape(batch, seq, KV_RANK))
            outs["kpe_p"].append(kpe.reshape(batch, seq, ROPE_DIM))
            outs["gmv_p"].append(gmv)
            ckv_s, kpe_s, gate_s, vn_s, qlat_s, qpe_s = _even_inproj_sample(
                xs, p["w_in_p"], p["kvn"], p["gm_g"], p["gm_b"], p["ws0"], p["bs0"], *tabs_s,
                p["qn"], p["wq_nope"], p["wq_pe"], p["w_uk"])
            kpe_pad = jnp.concatenate([kpe_s, jnp.zeros((dec_batch, LANES - ROPE_DIM), F32)], axis=1)
            o_lat = _decode_attn(page_table,
                                 qlat_s.reshape(dec_batch * MLA_HEADS, LANES),
                                 qpe_s.reshape(dec_batch * MLA_HEADS, LANES),
                                 ckv_s.reshape(dec_batch, 1, KV_RANK), kpe_pad.reshape(dec_batch, 1, LANES),
                                 cache_ckv[e], jnp.swapaxes(cache_kpe[e], 1, 2))
            xs = _mix_ffn(xs, [o_lat.reshape(dec_batch, MLA_HEADS * KV_RANK), gate_s], p["w_uv"],
                          [p["wo_attn"], p["wo_gate"]], *ln, name="even_ffn_sample")
            outs["ckv_s"].append(ckv_s.reshape(dec_batch, 1, KV_RANK))
            outs["kpe_s"].append(kpe_s.reshape(dec_batch, 1, ROPE_DIM))
            outs["gmv_s"].append(vn_s.reshape(dec_batch, 1, GM_WIDTH))
        else:
            o = layer // 2
            p = _s5_params(w_in_odd[o], s5_a_re[o], s5_a_im[o], s5_b_re[o], s5_b_im[o], s5_c_re[o],
                           s5_c_im[o], s5_d[o], s5_log_dt[o], s5_w_glu[o], w_out_odd[o], batch)
            yg, hre, him = _s5_prompt(xp.reshape(batch, seq, D_MODEL), p["w_in"], p["bw"], p["a_b"],
                                      p["cw"], p["d"], p["w_glu"])
            xp = _mix_ffn(xp, [yg.reshape(batch * seq, S5_WIDTH)], None, [p["w_out"]], *ln,
                          name="odd_ffn_prompt")
            outs["s5re_p"].append(hre.reshape(batch, S5_GROUPS, S5_STATE))
            outs["s5im_p"].append(him.reshape(batch, S5_GROUPS, S5_STATE))
            yg_s, hre_s, him_s = _s5_sample(
                xs, state_s5_re[o].reshape(dec_batch, S5_STATES), state_s5_im[o].reshape(dec_batch, S5_STATES),
                p["w_in"], p["bw"], p["a_r"], p["a_i"], p["cw"], p["d"], p["w_glu"])
            xs = _mix_ffn(xs, [yg_s], None, [p["w_out"]], *ln, name="odd_ffn_sample")
            outs["s5re_s"].append(hre_s.reshape(dec_batch, S5_GROUPS, S5_STATE))
            outs["s5im_s"].append(him_s.reshape(dec_batch, S5_GROUPS, S5_STATE))

    st = jnp.stack
    return (xp.reshape(batch, seq, D_MODEL), xs.reshape(dec_batch, dec_seq, D_MODEL),
            st(outs["ckv_p"]), st(outs["kpe_p"]), st(outs["ckv_s"]), st(outs["kpe_s"]),
            st(outs["gmv_p"]), st(outs["gmv_s"]),
            st(outs["s5re_p"]), st(outs["s5im_p"]), st(outs["s5re_s"]), st(outs["s5im_s"]))
```

```python
import functools
import math

import jax
import jax.numpy as jnp
from jax import lax
from jax.experimental import pallas as pl
from jax.experimental.pallas import tpu as pltpu

F32 = jnp.float32
BF16 = jnp.bfloat16

D_MODEL = 1024
DEPTH = 2
PAST_LEN = 16384
PAGE_SIZE = 128
MLA_HEADS = 8
NOPE_DIM = 64
ROPE_DIM = 32
ROPE_HALF = ROPE_DIM // 2
V_DIM = 64
Q_RANK = 256
KV_RANK = 128
ROPE_THETA = 10000.0
ATTN_SCALE = 1.0 / math.sqrt(NOPE_DIM + ROPE_DIM)
ATTN_SCALE_LOG2 = ATTN_SCALE * math.log2(math.e)
GM_HEADS = 8
GM_HEAD_DIM = 64
GM_WIDTH = GM_HEADS * GM_HEAD_DIM
GM_CHUNK = 128
S5_GROUP_DIM = 16
S5_GROUPS = 32
S5_STATE = 64
S5_WIDTH = S5_GROUPS * S5_GROUP_DIM
S5_STATES = S5_GROUPS * S5_STATE
FFN_HIDDEN = 4 * D_MODEL
DN_ALPHA = (2 * DEPTH) ** 0.25
LN_EPS = 1e-5
RMS_EPS = 1e-6

LANES = 128
SUBLANES = 8
VMEM_LIMIT_BYTES = 56 * 1024 * 1024

INPROJ_TM = 512
ATTN_TQ = 256
ATTN_ROW_BLOCK = 256
FFN_TM = 512
FFN_HC = 1024
S5_L = 128
S5_SUB = 16
DEC_PG = 16
S5_KB = 8
S5_NB = S5_GROUPS // S5_KB
S5_SLABS = 2 * S5_STATES // LANES


def _cparams(sem):
    return pltpu.CompilerParams(dimension_semantics=sem, vmem_limit_bytes=VMEM_LIMIT_BYTES)


def _const_spec(shape):
    n = len(shape)
    return pl.BlockSpec(shape, lambda *_: (0,) * n, pipeline_mode=pl.Buffered(1))


def _layer_norm(x, g, b):
    mu = jnp.mean(x, axis=-1, keepdims=True)
    xc = x - mu
    var = jnp.mean(xc * xc, axis=-1, keepdims=True)
    return xc * lax.rsqrt(var + LN_EPS) * g + b


def _rms_norm(x, g):
    return x * lax.rsqrt(jnp.mean(x * x, axis=-1, keepdims=True) + RMS_EPS) * g


def _rope128(x, cos_t, sin_a, sin_b):
    return (x * cos_t + pltpu.roll(x, ROPE_HALF, 1) * sin_a
            + pltpu.roll(x, LANES - ROPE_HALF, 1) * sin_b)


def _dot(a, b):
    return jnp.dot(a, b, preferred_element_type=F32)


def _dot_nt(a, b):
    return lax.dot_general(a, b, (((1,), (1,)), ((), ())), preferred_element_type=F32)


def _even_inproj_kernel(x_ref, w_ref, kvn_ref, gmg_ref, gmb_ref, ws_ref, bs_ref,
                        cos_ref, sina_ref, sinb_ref,
                        qa_ref, ckv_ref, kpe_ref, kcat_ref, gate_ref, gmv_ref, *, tiles_per_seq):
    tm = x_ref.shape[0]
    z = _dot(x_ref[...].astype(BF16), w_ref[...])
    o_c, o_u, o_v, o_k = Q_RANK, Q_RANK + KV_RANK, Q_RANK + KV_RANK + GM_WIDTH, Q_RANK + KV_RANK + 2 * GM_WIDTH
    qa_ref[...] = z[:, :o_c]
    c_n = _rms_norm(z[:, o_c:o_u], kvn_ref[...])
    ckv_ref[...] = c_n
    kp = _rope128(z[:, o_k:o_k + LANES], cos_ref[...], sina_ref[...], sinb_ref[...])
    kpe_ref[...] = kp[:, :ROPE_DIM]
    kcat_ref[...] = jnp.concatenate([c_n, kp], axis=1).astype(BF16)

    gu = jax.nn.gelu(z[:, o_u:o_v])
    v_n = _layer_norm(jax.nn.gelu(z[:, o_v:o_k]), gmg_ref[...], gmb_ref[...])

    @pl.when(pl.program_id(0) % tiles_per_seq == tiles_per_seq - 1)
    def _():
        gmv_ref[0] = v_n[tm - GM_CHUNK:, :]

    lane = lax.broadcasted_iota(jnp.int32, (GM_CHUNK, LANES), 1)
    lo = lane < GM_HEAD_DIM
    bs = bs_ref[...]
    for ci in range(tm // GM_CHUNK):
        rows = slice(ci * GM_CHUNK, (ci + 1) * GM_CHUNK)
        pieces = []
        for pr in range(GM_WIDTH // LANES):
            r = v_n[rows, pr * LANES:(pr + 1) * LANES]
            m0 = _dot(ws_ref[2 * pr], jnp.where(lo, r, 0.0).astype(BF16))
            m1 = _dot(ws_ref[2 * pr + 1], jnp.where(lo, 0.0, r).astype(BF16))
            pieces.append(m0 + m1)
        mixed = jnp.concatenate(pieces, axis=1) + bs
        gate_ref[rows, :] = (gu[rows, :] * mixed).astype(BF16)


def _even_inproj(x2d, w_in_p, kv_norm, gm_g, gm_b, ws_m, bs_full, cos_t, sin_a, sin_b, batch, seq):
    n_tok = x2d.shape[0]
    tm = INPROJ_TM
    tiles_per_seq = seq // tm
    row = lambda w: pl.BlockSpec((tm, w), lambda i: (i, 0))
    tab = pl.BlockSpec((tm, LANES), lambda i: (i % tiles_per_seq, 0))
    n_in = w_in_p.shape[1]
    return pl.pallas_call(
        functools.partial(_even_inproj_kernel, tiles_per_seq=tiles_per_seq),
        grid=(n_tok // tm,),
        in_specs=[row(D_MODEL), _const_spec((D_MODEL, n_in)), _const_spec((1, KV_RANK)),
                  _const_spec((1, GM_WIDTH)), _const_spec((1, GM_WIDTH)),
                  _const_spec((GM_HEADS, GM_CHUNK, GM_CHUNK)), _const_spec((GM_CHUNK, GM_WIDTH)),
                  tab, tab, tab],
        out_specs=[row(Q_RANK), row(KV_RANK), row(ROPE_DIM), row(2 * LANES), row(GM_WIDTH),
                   pl.BlockSpec((1, GM_CHUNK, GM_WIDTH), lambda i: (i // tiles_per_seq, 0, 0))],
        out_shape=[jax.ShapeDtypeStruct((n_tok, Q_RANK), F32),
                   jax.ShapeDtypeStruct((n_tok, KV_RANK), F32),
                   jax.ShapeDtypeStruct((n_tok, ROPE_DIM), F32),
                   jax.ShapeDtypeStruct((n_tok, 2 * LANES), BF16),
                   jax.ShapeDtypeStruct((n_tok, GM_WIDTH), BF16),
                   jax.ShapeDtypeStruct((batch, GM_CHUNK, GM_WIDTH), F32)],
        compiler_params=_cparams(("arbitrary",)),
        name="even_inproj",
    )(x2d, w_in_p, kv_norm, gm_g, gm_b, ws_m, bs_full, cos_t, sin_a, sin_b)


def _queries(q_a, qn_g, wq_nope, wq_pe, wuk_ref, cos_t, sin_a, sin_b):
    qn = _rms_norm(q_a, qn_g).astype(BF16)
    nope = _dot(qn, wq_nope).astype(BF16)
    lat = jnp.concatenate([_dot(nope[:, pr * LANES:(pr + 1) * LANES], wuk_ref[pr])
                           for pr in range(MLA_HEADS // 2)], axis=1)
    pe = _dot(qn, wq_pe)
    n = MLA_HEADS * LANES
    sin_s = sin_a + sin_b
    pes = [pe[:, h * LANES:(h + 1) * LANES] * cos_t + pe[:, n + h * LANES:n + (h + 1) * LANES] * sin_s
           for h in range(MLA_HEADS)]
    return lat, pes


def _mla_attn_kernel(qa_lo_ref, qa_hi_ref, kcat_ref, qn_ref, wqn_ref, wqp_ref, wuk_ref, wuv_ref,
                     cos_lo_ref, sina_lo_ref, sinb_lo_ref, cos_hi_ref, sina_hi_ref, sinb_hi_ref,
                     out_hbm, qcat_s, m_s, acc_s, obuf, osem, *, nq):
    tq = qa_lo_ref.shape[0]
    n_rows = MLA_HEADS * tq
    b, i = pl.program_id(0), pl.program_id(1)
    step = b * pl.num_programs(1) + i
    n_steps = pl.num_programs(0) * pl.num_programs(1)
    q_tiles = (i, nq - 1 - i)

    def out_copies():
        return [pltpu.make_async_copy(
            obuf.at[t], out_hbm.at[pl.ds(pl.multiple_of((b * nq + q_tiles[t]) * tq, tq), tq), :], osem.at[t])
            for t in range(2)]

    qa = (qa_lo_ref, qa_hi_ref)
    tabs = ((cos_lo_ref, sina_lo_ref, sinb_lo_ref), (cos_hi_ref, sina_hi_ref, sinb_hi_ref))
    for t in range(2):
        lat, pes = _queries(qa[t][...], qn_ref[...], wqn_ref[...], wqp_ref[...], wuk_ref,
                            *(r[...] for r in tabs[t]))
        for h in range(MLA_HEADS):
            qcat_s[t, h * tq:(h + 1) * tq, :] = (ATTN_SCALE_LOG2 * jnp.concatenate(
                [lat[:, h * LANES:(h + 1) * LANES], pes[h]], axis=1)).astype(BF16)

    lanes_x = lambda a, n: jnp.concatenate([a] * n, axis=1)
    rb = ATTN_ROW_BLOCK
    ones = jnp.ones((tq, LANES), BF16)

    def keys(kv_tile):
        k = kcat_ref[pl.ds(pl.multiple_of(kv_tile * tq, tq), tq), :]
        return k, jnp.concatenate([k[:, :KV_RANK], ones], axis=1)

    q_pos = lax.broadcasted_iota(jnp.int32, (rb, tq), 0) & (tq - 1)
    k_pos = lax.broadcasted_iota(jnp.int32, (rb, tq), 1)
    causal = k_pos <= q_pos
    for t in range(2):
        k, v1 = keys(q_tiles[t])
        for r0 in range(0, n_rows, rb):
            rows = slice(r0, r0 + rb)
            s = jnp.where(causal, _dot_nt(qcat_s[t, rows, :], k), -jnp.inf)
            m0 = jnp.broadcast_to(jnp.max(s, axis=-1, keepdims=True), (rb, LANES))
            m_s[t, rows, :] = m0
            acc_s[t, rows, :] = _dot(jnp.exp2(s - lanes_x(m0, tq // LANES)).astype(BF16), v1)

    for u in range(nq - 1):
        hi = u >= i
        t = hi.astype(jnp.int32)
        k, v1 = keys(jnp.where(hi, u - i, u))
        for r0 in range(0, n_rows, rb):
            rows = slice(r0, r0 + rb)
            s = _dot_nt(qcat_s[t, rows, :], k)
            m_old = m_s[t, rows, :]
            m_new = jnp.maximum(m_old, jnp.max(s, axis=-1, keepdims=True))
            alpha = jnp.exp2(m_old - m_new)
            p = jnp.exp2(s - lanes_x(m_new, tq // LANES))
            acc_s[t, rows, :] = lanes_x(alpha, 2) * acc_s[t, rows, :] + _dot(p.astype(BF16), v1)
            m_s[t, rows, :] = m_new

    @pl.when(step > 0)
    def _():
        for cp in out_copies():
            cp.wait()

    for t in range(2):
        o_all = jnp.concatenate(
            [acc_s[t, h * tq:(h + 1) * tq, :KV_RANK] / acc_s[t, h * tq:(h + 1) * tq, KV_RANK:]
             for h in range(MLA_HEADS)], axis=1).astype(BF16)
        obuf[t] = jnp.concatenate(
            [_dot(o_all[:, pr * 2 * LANES:(pr + 1) * 2 * LANES], wuv_ref[pr]) for pr in range(MLA_HEADS // 2)],
            axis=1).astype(BF16)
    for cp in out_copies():
        cp.start()

    @pl.when(step == n_steps - 1)
    def _():
        for cp in out_copies():
            cp.wait()


def _mla_attn(q_a, kcat, qn_g, wq_nope, wq_pe, w_uk, w_uv, cos_t, sin_a, sin_b, batch, seq):
    tq = ATTN_TQ
    nq = seq // tq
    n_attn = MLA_HEADS * V_DIM
    lo = lambda b, i: i
    hi = lambda b, i: nq - 1 - i
    tab = lambda sel: pl.BlockSpec((tq, LANES), lambda b, i: (sel(b, i), 0))
    qa_spec = lambda sel: pl.BlockSpec((tq, Q_RANK), lambda b, i: (b * nq + sel(b, i), 0))
    return pl.pallas_call(
        functools.partial(_mla_attn_kernel, nq=nq),
        grid=(batch, nq // 2),
        in_specs=[qa_spec(lo), qa_spec(hi),
                  pl.BlockSpec((None, seq, 2 * LANES), lambda b, i: (b, 0, 0)),
                  _const_spec((1, Q_RANK)), _const_spec(wq_nope.shape), _const_spec(wq_pe.shape),
                  _const_spec(w_uk.shape), _const_spec(w_uv.shape),
                  tab(lo), tab(lo), tab(lo), tab(hi), tab(hi), tab(hi)],
        out_specs=pl.BlockSpec(memory_space=pl.ANY),
        out_shape=jax.ShapeDtypeStruct((batch * seq, n_attn), BF16),
        scratch_shapes=[pltpu.VMEM((2, MLA_HEADS * tq, 2 * LANES), BF16),
                        pltpu.VMEM((2, MLA_HEADS * tq, LANES), F32),
                        pltpu.VMEM((2, MLA_HEADS * tq, 2 * KV_RANK), F32),
                        pltpu.VMEM((2, tq, n_attn), BF16),
                        pltpu.SemaphoreType.DMA((2,))],
        compiler_params=_cparams(("arbitrary", "arbitrary")),
        name="mla_attn",
    )(q_a, q_a, kcat.reshape(batch, seq, 2 * LANES), qn_g, wq_nope, wq_pe, w_uk, w_uv,
      cos_t, sin_a, sin_b, cos_t, sin_a, sin_b)


def _mix_ffn_kernel(*refs, n_act, has_pre):
    x_ref = refs[0]
    act_refs = refs[1:1 + n_act]
    k = 1 + n_act
    pre_ref = refs[k] if has_pre else None
    k += int(has_pre)
    wo_refs = refs[k:k + n_act]
    k += n_act
    g1_ref, b1_ref, g2_ref, b2_ref, w1_ref, w2_ref, out_ref = refs[k:k + 7]

    x = x_ref[...]
    f = None
    for idx in range(n_act):
        a = act_refs[idx][...]
        if has_pre and idx == 0:
            a = _dot(a.astype(BF16), pre_ref[...])
        part = _dot(a.astype(BF16), wo_refs[idx][...])
        f = part if f is None else f + part
    x1 = _layer_norm(DN_ALPHA * x + f, g1_ref[...], b1_ref[...])
    x1b = x1.astype(BF16)
    acc = None
    for c in range(FFN_HIDDEN // FFN_HC):
        cols = slice(c * FFN_HC, (c + 1) * FFN_HC)
        h = jnp.maximum(_dot(x1b, w1_ref[:, cols]), 0.0)
        part = _dot((h * h).astype(BF16), w2_ref[cols, :])
        acc = part if acc is None else acc + part
    out_ref[...] = _layer_norm(DN_ALPHA * x1 + acc, g2_ref[...], b2_ref[...])


def _mix_ffn(x2d, acts, pre_w, wos, ln1_g, ln1_b, ln2_g, ln2_b, w1, w2, name):
    n_tok = x2d.shape[0]
    tm = min(FFN_TM, n_tok)
    row = lambda w: pl.BlockSpec((tm, w), lambda i: (i, 0))
    has_pre = pre_w is not None
    in_specs = [row(D_MODEL)] + [row(a.shape[1]) for a in acts]
    args = [x2d] + list(acts)
    if has_pre:
        in_specs.append(_const_spec(pre_w.shape))
        args.append(pre_w)
    in_specs += [_const_spec(w.shape) for w in wos]
    args += list(wos)
    in_specs += [_const_spec((1, D_MODEL))] * 4 + [_const_spec(w1.shape), _const_spec(w2.shape)]
    args += [ln1_g, ln1_b, ln2_g, ln2_b, w1, w2]
    return pl.pallas_call(
        functools.partial(_mix_ffn_kernel, n_act=len(acts), has_pre=has_pre),
        grid=(n_tok // tm,),
        in_specs=in_specs,
        out_specs=row(D_MODEL),
        out_shape=jax.ShapeDtypeStruct((n_tok, D_MODEL), F32),
        compiler_params=_cparams(("arbitrary",)),
        name=name,
    )(*args)


def _even_inproj_sample_kernel(x_ref, w_ref, kvn_ref, gmg_ref, gmb_ref, ws0_ref, bs0_ref,
                               cos_ref, sina_ref, sinb_ref, qn_ref, wqn_ref, wqp_ref, wuk_ref,
                               ckv_ref, kpe_ref, gate_ref, vn_ref, qlat_ref, qpe_ref):
    z = _dot(x_ref[...].astype(BF16), w_ref[...])
    o_c, o_u, o_v, o_k = Q_RANK, Q_RANK + KV_RANK, Q_RANK + KV_RANK + GM_WIDTH, Q_RANK + KV_RANK + 2 * GM_WIDTH
    cos_t, sin_a, sin_b = cos_ref[...], sina_ref[...], sinb_ref[...]
    c_n = _rms_norm(z[:, o_c:o_u], kvn_ref[...])
    ckv_ref[...] = c_n
    kp = _rope128(z[:, o_k:o_k + LANES], cos_t, sin_a, sin_b)
    kpe_ref[...] = kp[:, :ROPE_DIM]
    gu = jax.nn.gelu(z[:, o_u:o_v])
    v_n = _layer_norm(jax.nn.gelu(z[:, o_v:o_k]), gmg_ref[...], gmb_ref[...])
    vn_ref[...] = v_n
    mixed = ws0_ref[...].astype(F32) * v_n.astype(BF16).astype(F32) + bs0_ref[...]
    gate_ref[...] = (gu * mixed).astype(BF16)
    lat, pes = _queries(z[:, :o_c], qn_ref[...], wqn_ref[...], wqp_ref[...], wuk_ref,
                        cos_t, sin_a, sin_b)
    qlat_ref[...] = lat
    qpe_ref[...] = jnp.concatenate(pes, axis=1)


def _even_inproj_sample(xs, w_in_p, kv_norm, gm_g, gm_b, ws0, bs0, cos_t, sin_a, sin_b,
                        qn_g, wq_nope, wq_pe, w_uk):
    n = xs.shape[0]
    args = (xs, w_in_p, kv_norm, gm_g, gm_b, ws0, bs0, cos_t, sin_a, sin_b, qn_g, wq_nope, wq_pe, w_uk)
    full = lambda shape: pl.BlockSpec(shape, lambda i: (0,) * len(shape))
    widths = [(KV_RANK, F32), (ROPE_DIM, F32), (GM_WIDTH, BF16), (GM_WIDTH, F32),
              (MLA_HEADS * LANES, F32), (MLA_HEADS * LANES, F32)]
    return pl.pallas_call(
        _even_inproj_sample_kernel,
        grid=(1,),
        in_specs=[full(a.shape) for a in args],
        out_specs=[full((n, w)) for w, _ in widths],
        out_shape=[jax.ShapeDtypeStruct((n, w), dt) for w, dt in widths],
        compiler_params=_cparams(("arbitrary",)),
        name="even_inproj_sample",
    )(*args)


def _decode_attn_kernel(pt_ref, qlat_ref, qpe_ref, ckvs_ref, kpes_ref, ckv_hbm, kpe_hbm, out_ref,
                        ckv_buf, kpe_buf, sem, *, n_pg, n_groups):
    b = pl.program_id(0)
    n_b = pl.num_programs(0)

    def group_copies(seq, grp, slot):
        cps = []
        for k in range(n_pg):
            page = pt_ref[seq, grp * n_pg + k]
            cps.append(pltpu.make_async_copy(ckv_hbm.at[page], ckv_buf.at[slot, k], sem.at[0, slot]))
            cps.append(pltpu.make_async_copy(kpe_hbm.at[page], kpe_buf.at[slot, k], sem.at[1, slot]))
        return cps

    @pl.when(b == 0)
    def _():
        for cp in group_copies(0, 0, 0):
            cp.start()

    qlat = qlat_ref[...].astype(BF16)
    qpe = qpe_ref[...][:, :ROPE_DIM].astype(BF16)
    m = l = acc = None
    for g in range(n_groups):
        slot = g % 2
        if g + 1 < n_groups:
            for cp in group_copies(b, g + 1, 1 - slot):
                cp.start()
        else:
            assert n_groups % 2 == 0

            @pl.when(b + 1 < n_b)
            def _():
                for cp in group_copies(b + 1, 0, 0):
                    cp.start()
        for cp in group_copies(b, g, slot):
            cp.wait()

        cks, ss = [], []
        for k in range(n_pg):
            ck = ckv_buf[slot, k].astype(BF16)
            kp = kpe_buf[slot, k].astype(BF16)
            cks.append(ck)
            ss.append(_dot_nt(qlat, ck) + _dot(qpe, kp))
        s = jnp.concatenate(ss, axis=1) * ATTN_SCALE
        m_cur = jnp.max(s, axis=-1, keepdims=True)
        m_new = m_cur if m is None else jnp.maximum(m, m_cur)
        p = jnp.exp(s - m_new)
        pb = p.astype(BF16)
        pv = None
        for k in range(n_pg):
            part = _dot(pb[:, k * PAGE_SIZE:(k + 1) * PAGE_SIZE], cks[k])
            pv = part if pv is None else pv + part
        p_sum = jnp.sum(p, axis=-1, keepdims=True)
        if m is None:
            l, acc = p_sum, pv
        else:
            alpha = jnp.exp(m - m_new)
            l, acc = alpha * l + p_sum, alpha * acc + pv
        m = m_new

    r = lambda a: a.astype(BF16).astype(F32)
    kv = r(ckvs_ref[...])
    s_self = (jnp.sum(r(qlat_ref[...]) * kv, axis=-1, keepdims=True)
              + jnp.sum(r(qpe_ref[...]) * r(kpes_ref[...]), axis=-1, keepdims=True)) * ATTN_SCALE
    m_n = jnp.maximum(m, s_self)
    a = jnp.exp(m - m_n)
    p_self = jnp.exp(s_self - m_n)
    out_ref[...] = (a * acc + r(p_self) * kv) / (a * l + p_self)


def _decode_attn(page_table, qlat2, qpe2, ckv_s3, kpe_s3, cache_ckv_e, cache_kpe_t):
    dec_batch, n_pages = page_table.shape
    n_pg = DEC_PG
    qspec = pl.BlockSpec((MLA_HEADS, LANES), lambda b, pt: (b, 0))
    self_spec = pl.BlockSpec((None, 1, LANES), lambda b, pt: (b, 0, 0))
    hbm = pl.BlockSpec(memory_space=pl.ANY)
    grid_spec = pltpu.PrefetchScalarGridSpec(
        num_scalar_prefetch=1,
        grid=(dec_batch,),
        in_specs=[qspec, qspec, self_spec, self_spec, hbm, hbm],
        out_specs=qspec,
        scratch_shapes=[pltpu.VMEM((2, n_pg, PAGE_SIZE, KV_RANK), F32),
                        pltpu.VMEM((2, n_pg, ROPE_DIM, PAGE_SIZE), F32),
                        pltpu.SemaphoreType.DMA((2, 2))],
    )
    return pl.pallas_call(
        functools.partial(_decode_attn_kernel, n_pg=n_pg, n_groups=n_pages // n_pg),
        grid_spec=grid_spec,
        out_shape=jax.ShapeDtypeStruct((dec_batch * MLA_HEADS, KV_RANK), F32),
        compiler_params=_cparams(("arbitrary",)),
        name="decode_attn",
    )(page_table, qlat2, qpe2, ckv_s3, kpe_s3, cache_ckv_e, cache_kpe_t)


def _s5_readout(hcat_blocks, u, cw_ref, d_ref, wglu_ref):
    y = jnp.concatenate([_dot(hcat_blocks[k], cw_ref[k]) for k in range(S5_NB)], axis=1)
    y = jax.nn.gelu(y + d_ref[...] * u)
    return y * jax.nn.sigmoid(_dot(y.astype(BF16), wglu_ref[...]))


def _s5_prompt_kernel(x_ref, win_ref, bw_ref, a_ref, cw_ref, d_ref, wglu_ref,
                      yg_ref, hre_ref, him_ref, u_s, y_s, bu_s, h_s):
    n_b, seg, _ = x_ref.shape
    half = S5_SLABS // 2
    sub = S5_SUB
    n_sub = seg // sub
    rows_sub = sub * n_b
    q_slabs = S5_WIDTH // LANES

    @pl.when(pl.program_id(0) == 0)
    def _():
        h_s[...] = jnp.zeros(h_s.shape, F32)

    for s in range(n_sub):
        rows = slice(s * rows_sub, (s + 1) * rows_sub)
        xs = jnp.concatenate([x_ref[b, s * sub:(s + 1) * sub, :] for b in range(n_b)], axis=0)
        u = _dot(xs.astype(BF16), win_ref[...])
        for b in range(n_b):
            for q in range(q_slabs):
                u_s[s, q, pl.ds(b, sub, stride=n_b), :] = u[b * sub:(b + 1) * sub, q * LANES:(q + 1) * LANES]
        for k in range(S5_NB):
            r = _dot(u_s[s, k].astype(BF16), bw_ref[k])
            for q in range(4):
                bu_s[4 * k + q, rows, :] = r[:, q * LANES:(q + 1) * LANES]
                bu_s[half + 4 * k + q, rows, :] = r[:, (4 + q) * LANES:(5 + q) * LANES]

    h = h_s[...]
    for t in range(seg):
        rows = slice(t * n_b, (t + 1) * n_b)
        hs = bu_s[:, rows, :]
        ar, ai = a_ref[:half], a_ref[half:]
        hr, hi = h[:half], h[half:]
        h = jnp.concatenate([ar * hr - ai * hi + hs[:half], ar * hi + ai * hr + hs[half:]], axis=0)
        bu_s[:, rows, :] = h
    h_s[...] = h
    hre_ref[...] = jnp.concatenate([h[s] for s in range(half)], axis=1)
    him_ref[...] = jnp.concatenate([h[half + s] for s in range(half)], axis=1)

    for s in range(n_sub):
        rows = slice(s * rows_sub, (s + 1) * rows_sub)
        blocks = [jnp.concatenate([bu_s[4 * k + q, rows, :] for q in range(4)]
                                  + [bu_s[half + 4 * k + q, rows, :] for q in range(4)], axis=1).astype(BF16)
                  for k in range(S5_NB)]
        u_t = jnp.concatenate([u_s[s, q] for q in range(q_slabs)], axis=1)
        y = _s5_readout(blocks, u_t, cw_ref, d_ref, wglu_ref)
        for q in range(q_slabs):
            y_s[s, q] = y[:, q * LANES:(q + 1) * LANES]
        for b in range(n_b):
            yg_ref[b, s * sub:(s + 1) * sub, :] = jnp.concatenate(
                [y_s[s, q, pl.ds(b, sub, stride=n_b), :] for q in range(q_slabs)], axis=1).astype(BF16)


def _s5_prompt(x3, w_in, bw, a_b, cw, d_row, w_glu):
    batch, seq, _ = x3.shape
    seg = S5_L
    slab_rows = (seg // S5_SUB, S5_WIDTH // LANES, S5_SUB * batch, LANES)
    return pl.pallas_call(
        _s5_prompt_kernel,
        grid=(seq // seg,),
        in_specs=[pl.BlockSpec((batch, seg, D_MODEL), lambda c: (0, c, 0)),
                  _const_spec(w_in.shape), _const_spec(bw.shape), _const_spec(a_b.shape),
                  _const_spec(cw.shape), _const_spec(d_row.shape), _const_spec(w_glu.shape)],
        out_specs=[pl.BlockSpec((batch, seg, S5_WIDTH), lambda c: (0, c, 0)),
                   pl.BlockSpec((batch, S5_STATES), lambda c: (0, 0)),
                   pl.BlockSpec((batch, S5_STATES), lambda c: (0, 0))],
        out_shape=[jax.ShapeDtypeStruct((batch, seq, S5_WIDTH), BF16),
                   jax.ShapeDtypeStruct((batch, S5_STATES), F32),
                   jax.ShapeDtypeStruct((batch, S5_STATES), F32)],
        scratch_shapes=[pltpu.VMEM(slab_rows, F32), pltpu.VMEM(slab_rows, F32),
                        pltpu.VMEM((S5_SLABS, seg * batch, LANES), F32),
                        pltpu.VMEM((S5_SLABS, batch, LANES), F32)],
        compiler_params=_cparams(("arbitrary",)),
        name="s5_prompt",
    )(x3, w_in, bw, a_b, cw, d_row, w_glu)


def _s5_sample_kernel(x_ref, h0r_ref, h0i_ref, win_ref, bw_ref, ar_ref, ai_ref, cw_ref, d_ref, wglu_ref,
                      yg_ref, hre_ref, him_ref):
    u = _dot(x_ref[...].astype(BF16), win_ref[...])
    ub = u.astype(BF16)
    bu = [_dot(ub[:, k * LANES:(k + 1) * LANES], bw_ref[k]) for k in range(S5_NB)]
    w = S5_KB * S5_STATE
    bu_r = jnp.concatenate([r[:, :w] for r in bu], axis=1)
    bu_i = jnp.concatenate([r[:, w:] for r in bu], axis=1)
    ar, ai, h0r, h0i = ar_ref[...], ai_ref[...], h0r_ref[...], h0i_ref[...]
    hr = bu_r + (ar * h0r - ai * h0i)
    hi = bu_i + (ar * h0i + ai * h0r)
    hre_ref[...] = hr
    him_ref[...] = hi
    blocks = [jnp.concatenate([hr[:, k * w:(k + 1) * w], hi[:, k * w:(k + 1) * w]], axis=1).astype(BF16)
              for k in range(S5_NB)]
    yg_ref[...] = _s5_readout(blocks, u, cw_ref, d_ref, wglu_ref).astype(BF16)


def _s5_sample(xs, h0r, h0i, w_in, bw, a_r, a_i, cw, d_row, w_glu):
    n = xs.shape[0]
    args = (xs, h0r, h0i, w_in, bw, a_r, a_i, cw, d_row, w_glu)
    full = lambda shape: pl.BlockSpec(shape, lambda i: (0,) * len(shape))
    return pl.pallas_call(
        _s5_sample_kernel,
        grid=(1,),
        in_specs=[full(a.shape) for a in args],
        out_specs=[full((n, S5_WIDTH)), full((n, S5_STATES)), full((n, S5_STATES))],
        out_shape=[jax.ShapeDtypeStruct((n, S5_WIDTH), BF16),
                   jax.ShapeDtypeStruct((n, S5_STATES), F32),
                   jax.ShapeDtypeStruct((n, S5_STATES), F32)],
        compiler_params=_cparams(("arbitrary",)),
        name="s5_sample",
    )(*args)


def _rope_tables(pos):
    freqs = ROPE_THETA ** (-jnp.arange(ROPE_HALF, dtype=F32) / ROPE_HALF)
    ang = pos.astype(F32)[:, None] * freqs
    cos, sin = jnp.cos(ang), jnp.sin(ang)
    zero = jnp.zeros_like(cos)
    pad = jnp.zeros((pos.shape[0], LANES - ROPE_DIM), F32)
    cos_t = jnp.concatenate([cos, cos, pad], axis=1)
    sin_a = jnp.concatenate([zero, sin, pad], axis=1)
    sin_b = jnp.concatenate([-sin, zero, pad], axis=1)
    return cos_t, sin_a, sin_b


def _block_diag(blocks):
    n, r, c = blocks.shape
    eye = jnp.eye(n, dtype=blocks.dtype)
    return jnp.einsum('nrc,nm->nrmc', blocks, eye).reshape(n * r, n * c)


def _even_params(w_in, q_norm, w_q_b, kv_norm, w_kv_b, gm_g, gm_b, gm_w_s, gm_b_s, w_out):
    i1, i2, i3, i4 = Q_RANK, Q_RANK + KV_RANK, Q_RANK + KV_RANK + ROPE_DIM, Q_RANK + KV_RANK + ROPE_DIM + GM_WIDTH
    w_in_p = jnp.concatenate([w_in[:, :i2], w_in[:, i3:], w_in[:, i2:i3],
                              jnp.zeros((D_MODEL, LANES - ROPE_DIM), F32)], axis=1).astype(BF16)
    wq_nope = w_q_b[:, :, :NOPE_DIM].reshape(Q_RANK, MLA_HEADS * NOPE_DIM).astype(BF16)
    pe = w_q_b[:, :, NOPE_DIM:]
    pe_sw = jnp.concatenate([pe[:, :, ROPE_HALF:], pe[:, :, :ROPE_HALF]], axis=2)
    lane_pad = jnp.zeros((Q_RANK, MLA_HEADS, LANES - ROPE_DIM), F32)
    wq_pe = jnp.concatenate(
        [jnp.concatenate([w, lane_pad], axis=2).reshape(Q_RANK, MLA_HEADS * LANES) for w in (pe, pe_sw)],
        axis=1).astype(BF16)
    uk = jnp.transpose(w_kv_b[:, :, :NOPE_DIM], (1, 2, 0))
    uv = jnp.transpose(w_kv_b[:, :, NOPE_DIM:], (1, 0, 2))
    pairs = range(MLA_HEADS // 2)
    w_uk = jnp.stack([_block_diag(uk[2 * pr:2 * pr + 2]) for pr in pairs]).astype(BF16)
    w_uv_p = jnp.stack([_block_diag(uv[2 * pr:2 * pr + 2]) for pr in pairs]).astype(BF16)
    w_uv = _block_diag(uv).astype(BF16)
    causal = jnp.tril(jnp.ones((GM_CHUNK, GM_CHUNK), dtype=bool))
    ws_m = jnp.where(causal[None], gm_w_s, 0).astype(BF16)
    bs_full = jnp.repeat(gm_b_s.T, GM_HEAD_DIM, axis=1)
    ws0 = jnp.repeat(gm_w_s[:, 0, 0], GM_HEAD_DIM)[None, :].astype(BF16)
    bs0 = bs_full[:1]
    n_attn = MLA_HEADS * V_DIM
    return dict(w_in_p=w_in_p, qn=q_norm[None, :], wq_nope=wq_nope, wq_pe=wq_pe, w_uk=w_uk, w_uv=w_uv, w_uv_p=w_uv_p,
                kvn=kv_norm[None, :], gm_g=gm_g[None, :], gm_b=gm_b[None, :], ws_m=ws_m, bs_full=bs_full,
                ws0=ws0, bs0=bs0, wo_attn=w_out[:n_attn].astype(BF16), wo_gate=w_out[n_attn:].astype(BF16))


def _s5_params(w_in, a_re, a_im, b_re, b_im, c_re, c_im, d, log_dt, w_glu, w_out, batch):
    dt = jnp.exp(log_dt)[:, None]
    ld_r, ld_i = a_re * dt, a_im * dt
    mag = jnp.exp(ld_r)
    ab_r, ab_i = mag * jnp.cos(ld_i), mag * jnp.sin(ld_i)
    den = a_re * a_re + a_im * a_im
    cr = ((ab_r - 1.0) * a_re + ab_i * a_im) / den
    ci = (ab_i * a_re - (ab_r - 1.0) * a_im) / den
    bb_r = cr[..., None] * b_re - ci[..., None] * b_im
    bb_i = cr[..., None] * b_im + ci[..., None] * b_re

    def in_blocks(bb):
        return jnp.stack([_block_diag(jnp.transpose(bb[k * S5_KB:(k + 1) * S5_KB], (0, 2, 1)))
                          for k in range(S5_NB)])

    def out_blocks(cc):
        return jnp.stack([_block_diag(jnp.transpose(cc[k * S5_KB:(k + 1) * S5_KB], (0, 2, 1)))
                          for k in range(S5_NB)])

    bw = jnp.concatenate([in_blocks(bb_r), in_blocks(bb_i)], axis=2).astype(BF16)
    cw = jnp.concatenate([out_blocks(c_re), out_blocks(-c_im)], axis=1).astype(BF16)
    a_r, a_i = ab_r.reshape(1, S5_STATES), ab_i.reshape(1, S5_STATES)
    half = S5_SLABS // 2
    a_b = jnp.concatenate([jnp.broadcast_to(a_r.reshape(half, 1, LANES), (half, batch, LANES)),
                           jnp.broadcast_to(a_i.reshape(half, 1, LANES), (half, batch, LANES))], axis=0)
    return dict(w_in=w_in.astype(BF16), bw=bw, cw=cw, a_r=a_r, a_i=a_i, a_b=a_b,
                d=d.reshape(1, S5_WIDTH), w_glu=w_glu.astype(BF16), w_out=w_out.astype(BF16))


def kernel(x_prompt, x_sample, cache_ckv, cache_kpe, state_s5_re, state_s5_im, page_table, w_in_even, mla_q_norm, mla_w_q_b, mla_kv_norm, mla_w_kv_b, gm_norm_g, gm_norm_b, gm_w_s, gm_b_s, w_out_even, w_in_odd, s5_a_re, s5_a_im, s5_b_re, s5_b_im, s5_c_re, s5_c_im, s5_d, s5_log_dt, s5_w_glu, w_out_odd, ln_mix_g, ln_mix_b, ln_ffn_g, ln_ffn_b, ffn_w1, ffn_w2):
    batch, seq, _ = x_prompt.shape
    dec_batch, dec_seq, _ = x_sample.shape
    assert dec_seq == 1 and seq % INPROJ_TM == 0 and seq % ATTN_TQ == 0 and seq % S5_L == 0
    assert (seq // ATTN_TQ) % 2 == 0
    assert (batch * seq) % FFN_TM == 0 and page_table.shape[1] % DEC_PG == 0
    assert batch == SUBLANES

    xp = x_prompt.reshape(batch * seq, D_MODEL)
    xs = x_sample.reshape(dec_batch, D_MODEL)
    tabs_p = _rope_tables(jnp.arange(seq, dtype=jnp.int32))
    tabs_s = _rope_tables(PAST_LEN + jnp.arange(dec_seq, dtype=jnp.int32))

    outs = {k: [] for k in ("ckv_p", "kpe_p", "ckv_s", "kpe_s", "gmv_p", "gmv_s",
                            "s5re_p", "s5im_p", "s5re_s", "s5im_s")}
    for layer in range(DEPTH):
        ln = (ln_mix_g[layer][None, :], ln_mix_b[layer][None, :],
              ln_ffn_g[layer][None, :], ln_ffn_b[layer][None, :],
              ffn_w1[layer].astype(BF16), ffn_w2[layer].astype(BF16))
        if layer % 2 == 0:
            e = layer // 2
            p = _even_params(w_in_even[e], mla_q_norm[e], mla_w_q_b[e], mla_kv_norm[e], mla_w_kv_b[e],
                             gm_norm_g[e], gm_norm_b[e], gm_w_s[e], gm_b_s[e], w_out_even[e])
            q_a, ckv, kpe, kcat, gate, gmv = _even_inproj(
                xp, p["w_in_p"], p["kvn"], p["gm_g"], p["gm_b"], p["ws_m"], p["bs_full"], *tabs_p, batch, seq)
            attn = _mla_attn(q_a, kcat, p["qn"], p["wq_nope"], p["wq_pe"], p["w_uk"], p["w_uv_p"], *tabs_p,
                             batch, seq)
            xp = _mix_ffn(xp, [attn, gate], None, [p["wo_attn"], p["wo_gate"]], *ln, name="even_ffn_prompt")
            outs["ckv_p"].append(ckv.reshape(batch, seq, KV_RANK))
            outs["kpe_p"].append(kpe.reshape(batch, seq, ROPE_DIM))
            outs["gmv_p"].append(gmv)
            ckv_s, kpe_s, gate_s, vn_s, qlat_s, qpe_s = _even_inproj_sample(
                xs, p["w_in_p"], p["kvn"], p["gm_g"], p["gm_b"], p["ws0"], p["bs0"], *tabs_s,
                p["qn"], p["wq_nope"], p["wq_pe"], p["w_uk"])
            kpe_pad = jnp.concatenate([kpe_s, jnp.zeros((dec_batch, LANES - ROPE_DIM), F32)], axis=1)
            o_lat = _decode_attn(page_table,
                                 qlat_s.reshape(dec_batch * MLA_HEADS, LANES),
                                 qpe_s.reshape(dec_batch * MLA_HEADS, LANES),
                                 ckv_s.reshape(dec_batch, 1, KV_RANK), kpe_pad.reshape(dec_batch, 1, LANES),
                                 cache_ckv[e], jnp.swapaxes(cache_kpe[e], 1, 2))
            xs = _mix_ffn(xs, [o_lat.reshape(dec_batch, MLA_HEADS * KV_RANK), gate_s], p["w_uv"],
                          [p["wo_attn"], p["wo_gate"]], *ln, name="even_ffn_sample")
            outs["ckv_s"].append(ckv_s.reshape(dec_batch, 1, KV_RANK))
            outs["kpe_s"].append(kpe_s.reshape(dec_batch, 1, ROPE_DIM))
            outs["gmv_s"].append(vn_s.reshape(dec_batch, 1, GM_WIDTH))
        else:
            o = layer // 2
            p = _s5_params(w_in_odd[o], s5_a_re[o], s5_a_im[o], s5_b_re[o], s5_b_im[o], s5_c_re[o],
                           s5_c_im[o], s5_d[o], s5_log_dt[o], s5_w_glu[o], w_out_odd[o], batch)
            yg, hre, him = _s5_prompt(xp.reshape(batch, seq, D_MODEL), p["w_in"], p["bw"], p["a_b"],
                                      p["cw"], p["d"], p["w_glu"])
            xp = _mix_ffn(xp, [yg.reshape(batch * seq, S5_WIDTH)], None, [p["w_out"]], *ln,
                          name="odd_ffn_prompt")
            outs["s5re_p"].append(hre.reshape(batch, S5_GROUPS, S5_STATE))
            outs["s5im_p"].append(him.reshape(batch, S5_GROUPS, S5_STATE))
            yg_s, hre_s, him_s = _s5_sample(
                xs, state_s5_re[o].reshape(dec_batch, S5_STATES), state_s5_im[o].reshape(dec_batch, S5_STATES),
                p["w_in"], p["bw"], p["a_r"], p["a_i"], p["cw"], p["d"], p["w_glu"])
            xs = _mix_ffn(xs, [yg_s], None, [p["w_out"]], *ln, name="odd_ffn_sample")
            outs["s5re_s"].append(hre_s.reshape(dec_batch, S5_GROUPS, S5_STATE))
            outs["s5im_s"].append(him_s.reshape(dec_batch, S5_GROUPS, S5_STATE))

    st = jnp.stack
    return (xp.reshape(batch, seq, D_MODEL), xs.reshape(dec_batch, dec_seq, D_MODEL),
            st(outs["ckv_p"]), st(outs["kpe_p"]), st(outs["ckv_s"]), st(outs["kpe_s"]),
            st(outs["gmv_p"]), st(outs["gmv_s"]),
            st(outs["s5re_p"]), st(outs["s5im_p"]), st(outs["s5re_s"]), st(outs["s5im_s"]))
```

```python
import functools
import math

import jax
import jax.numpy as jnp
from jax import lax
from jax.experimental import pallas as pl
from jax.experimental.pallas import tpu as pltpu

F32 = jnp.float32
BF16 = jnp.bfloat16

D_MODEL = 1024
DEPTH = 2
PAST_LEN = 16384
PAGE_SIZE = 128
MLA_HEADS = 8
NOPE_DIM = 64
ROPE_DIM = 32
ROPE_HALF = ROPE_DIM // 2
V_DIM = 64
Q_RANK = 256
KV_RANK = 128
ROPE_THETA = 10000.0
ATTN_SCALE = 1.0 / math.sqrt(NOPE_DIM + ROPE_DIM)
ATTN_SCALE_LOG2 = ATTN_SCALE * math.log2(math.e)
GM_HEADS = 8
GM_HEAD_DIM = 64
GM_WIDTH = GM_HEADS * GM_HEAD_DIM
GM_CHUNK = 128
S5_GROUP_DIM = 16
S5_GROUPS = 32
S5_STATE = 64
S5_WIDTH = S5_GROUPS * S5_GROUP_DIM
S5_STATES = S5_GROUPS * S5_STATE
FFN_HIDDEN = 4 * D_MODEL
DN_ALPHA = (2 * DEPTH) ** 0.25
LN_EPS = 1e-5
RMS_EPS = 1e-6

LANES = 128
SUBLANES = 8
VMEM_LIMIT_BYTES = 56 * 1024 * 1024

INPROJ_TM = 512
ATTN_TQ = 256
ATTN_ROW_BLOCK = 256
FFN_TM = 512
FFN_HC = 1024
S5_L = 128
S5_SUB = 16
DEC_PG = 16
DEC_SLOTS = 4
DEC_AHEAD = 2
S5_KB = 8
S5_NB = S5_GROUPS // S5_KB
S5_SLABS = 2 * S5_STATES // LANES


def _cparams(sem):
    return pltpu.CompilerParams(dimension_semantics=sem, vmem_limit_bytes=VMEM_LIMIT_BYTES)


def _const_spec(shape):
    n = len(shape)
    return pl.BlockSpec(shape, lambda *_: (0,) * n, pipeline_mode=pl.Buffered(1))


def _layer_norm(x, g, b):
    mu = jnp.mean(x, axis=-1, keepdims=True)
    xc = x - mu
    var = jnp.mean(xc * xc, axis=-1, keepdims=True)
    return xc * lax.rsqrt(var + LN_EPS) * g + b


def _rms_norm(x, g):
    return x * lax.rsqrt(jnp.mean(x * x, axis=-1, keepdims=True) + RMS_EPS) * g


def _rope128(x, cos_t, sin_a, sin_b):
    return (x * cos_t + pltpu.roll(x, ROPE_HALF, 1) * sin_a
            + pltpu.roll(x, LANES - ROPE_HALF, 1) * sin_b)


def _dot(a, b):
    return jnp.dot(a, b, preferred_element_type=F32)


def _dot_nt(a, b):
    return lax.dot_general(a, b, (((1,), (1,)), ((), ())), preferred_element_type=F32)


def _even_inproj_kernel(x_ref, w_ref, kvn_ref, gmg_ref, gmb_ref, ws_ref, bs_ref,
                        cos_ref, sina_ref, sinb_ref,
                        qa_ref, ckv_ref, kpe_ref, kcat_ref, gate_ref, gmv_ref, *, tiles_per_seq):
    tm = x_ref.shape[0]
    z = _dot(x_ref[...].astype(BF16), w_ref[...])
    o_c, o_u, o_v, o_k = Q_RANK, Q_RANK + KV_RANK, Q_RANK + KV_RANK + GM_WIDTH, Q_RANK + KV_RANK + 2 * GM_WIDTH
    qa_ref[...] = z[:, :o_c]
    c_n = _rms_norm(z[:, o_c:o_u], kvn_ref[...])
    ckv_ref[...] = c_n
    kp = _rope128(z[:, o_k:o_k + LANES], cos_ref[...], sina_ref[...], sinb_ref[...])
    kpe_ref[...] = kp[:, :ROPE_DIM]
    kcat_ref[...] = jnp.concatenate([c_n, kp], axis=1).astype(BF16)

    gu = jax.nn.gelu(z[:, o_u:o_v])
    v_n = _layer_norm(jax.nn.gelu(z[:, o_v:o_k]), gmg_ref[...], gmb_ref[...])

    @pl.when(pl.program_id(0) % tiles_per_seq == tiles_per_seq - 1)
    def _():
        gmv_ref[0] = v_n[tm - GM_CHUNK:, :]

    lane = lax.broadcasted_iota(jnp.int32, (GM_CHUNK, LANES), 1)
    lo = lane < GM_HEAD_DIM
    bs = bs_ref[...]
    for ci in range(tm // GM_CHUNK):
        rows = slice(ci * GM_CHUNK, (ci + 1) * GM_CHUNK)
        pieces = []
        for pr in range(GM_WIDTH // LANES):
            r = v_n[rows, pr * LANES:(pr + 1) * LANES]
            m0 = _dot(ws_ref[2 * pr], jnp.where(lo, r, 0.0).astype(BF16))
            m1 = _dot(ws_ref[2 * pr + 1], jnp.where(lo, 0.0, r).astype(BF16))
            pieces.append(m0 + m1)
        mixed = jnp.concatenate(pieces, axis=1) + bs
        gate_ref[rows, :] = (gu[rows, :] * mixed).astype(BF16)


def _even_inproj(x2d, w_in_p, kv_norm, gm_g, gm_b, ws_m, bs_full, cos_t, sin_a, sin_b, batch, seq):
    n_tok = x2d.shape[0]
    tm = INPROJ_TM
    tiles_per_seq = seq // tm
    row = lambda w: pl.BlockSpec((tm, w), lambda i: (i, 0))
    tab = pl.BlockSpec((tm, LANES), lambda i: (i % tiles_per_seq, 0))
    n_in = w_in_p.shape[1]
    return pl.pallas_call(
        functools.partial(_even_inproj_kernel, tiles_per_seq=tiles_per_seq),
        grid=(n_tok // tm,),
        in_specs=[row(D_MODEL), _const_spec((D_MODEL, n_in)), _const_spec((1, KV_RANK)),
                  _const_spec((1, GM_WIDTH)), _const_spec((1, GM_WIDTH)),
                  _const_spec((GM_HEADS, GM_CHUNK, GM_CHUNK)), _const_spec((GM_CHUNK, GM_WIDTH)),
                  tab, tab, tab],
        out_specs=[row(Q_RANK), row(KV_RANK), row(ROPE_DIM), row(2 * LANES), row(GM_WIDTH),
                   pl.BlockSpec((1, GM_CHUNK, GM_WIDTH), lambda i: (i // tiles_per_seq, 0, 0))],
        out_shape=[jax.ShapeDtypeStruct((n_tok, Q_RANK), F32),
                   jax.ShapeDtypeStruct((n_tok, KV_RANK), F32),
                   jax.ShapeDtypeStruct((n_tok, ROPE_DIM), F32),
                   jax.ShapeDtypeStruct((n_tok, 2 * LANES), BF16),
                   jax.ShapeDtypeStruct((n_tok, GM_WIDTH), BF16),
                   jax.ShapeDtypeStruct((batch, GM_CHUNK, GM_WIDTH), F32)],
        compiler_params=_cparams(("arbitrary",)),
        name="even_inproj",
    )(x2d, w_in_p, kv_norm, gm_g, gm_b, ws_m, bs_full, cos_t, sin_a, sin_b)


def _queries(q_a, qn_g, wq_nope, wq_pe, wuk_ref, cos_t, sin_a, sin_b):
    qn = _rms_norm(q_a, qn_g).astype(BF16)
    nope = _dot(qn, wq_nope).astype(BF16)
    lat = jnp.concatenate([_dot(nope[:, pr * LANES:(pr + 1) * LANES], wuk_ref[pr])
                           for pr in range(MLA_HEADS // 2)], axis=1)
    pe = _dot(qn, wq_pe)
    n = MLA_HEADS * LANES
    sin_s = sin_a + sin_b
    pes = [pe[:, h * LANES:(h + 1) * LANES] * cos_t + pe[:, n + h * LANES:n + (h + 1) * LANES] * sin_s
           for h in range(MLA_HEADS)]
    return lat, pes


def _mla_attn_kernel(qa_lo_ref, qa_hi_ref, kcat_ref, qn_ref, wqn_ref, wqp_ref, wuk_ref, wuv_ref,
                     cos_lo_ref, sina_lo_ref, sinb_lo_ref, cos_hi_ref, sina_hi_ref, sinb_hi_ref,
                     out_hbm, qcat_s, m_s, acc_s, obuf, osem, *, nq):
    tq = qa_lo_ref.shape[0]
    n_rows = MLA_HEADS * tq
    b, i = pl.program_id(0), pl.program_id(1)
    step = b * pl.num_programs(1) + i
    n_steps = pl.num_programs(0) * pl.num_programs(1)
    q_tiles = (i, nq - 1 - i)

    def out_copies():
        return [pltpu.make_async_copy(
            obuf.at[t], out_hbm.at[pl.ds(pl.multiple_of((b * nq + q_tiles[t]) * tq, tq), tq), :], osem.at[t])
            for t in range(2)]

    qa = (qa_lo_ref, qa_hi_ref)
    tabs = ((cos_lo_ref, sina_lo_ref, sinb_lo_ref), (cos_hi_ref, sina_hi_ref, sinb_hi_ref))
    for t in range(2):
        lat, pes = _queries(qa[t][...], qn_ref[...], wqn_ref[...], wqp_ref[...], wuk_ref,
                            *(r[...] for r in tabs[t]))
        for h in range(MLA_HEADS):
            qcat_s[t, h * tq:(h + 1) * tq, :] = (ATTN_SCALE_LOG2 * jnp.concatenate(
                [lat[:, h * LANES:(h + 1) * LANES], pes[h]], axis=1)).astype(BF16)

    lanes_x = lambda a, n: jnp.concatenate([a] * n, axis=1)
    rb = ATTN_ROW_BLOCK
    ones = jnp.ones((tq, LANES), BF16)

    def keys(kv_tile):
        k = kcat_ref[pl.ds(pl.multiple_of(kv_tile * tq, tq), tq), :]
        return k, jnp.concatenate([k[:, :KV_RANK], ones], axis=1)

    q_pos = lax.broadcasted_iota(jnp.int32, (rb, tq), 0) & (tq - 1)
    k_pos = lax.broadcasted_iota(jnp.int32, (rb, tq), 1)
    causal = k_pos <= q_pos
    for t in range(2):
        k, v1 = keys(q_tiles[t])
        for r0 in range(0, n_rows, rb):
            rows = slice(r0, r0 + rb)
            s = jnp.where(causal, _dot_nt(qcat_s[t, rows, :], k), -jnp.inf)
            m0 = jnp.broadcast_to(jnp.max(s, axis=-1, keepdims=True), (rb, LANES))
            m_s[t, rows, :] = m0
            acc_s[t, rows, :] = _dot(jnp.exp2(s - lanes_x(m0, tq // LANES)).astype(BF16), v1)

    for u in range(nq - 1):
        hi = u >= i
        t = hi.astype(jnp.int32)
        k, v1 = keys(jnp.where(hi, u - i, u))
        for r0 in range(0, n_rows, rb):
            rows = slice(r0, r0 + rb)
            s = _dot_nt(qcat_s[t, rows, :], k)
            m_old = m_s[t, rows, :]
            m_new = jnp.maximum(m_old, jnp.max(s, axis=-1, keepdims=True))
            alpha = jnp.exp2(m_old - m_new)
            p = jnp.exp2(s - lanes_x(m_new, tq // LANES))
            acc_s[t, rows, :] = lanes_x(alpha, 2) * acc_s[t, rows, :] + _dot(p.astype(BF16), v1)
            m_s[t, rows, :] = m_new

    @pl.when(step > 0)
    def _():
        for cp in out_copies():
            cp.wait()

    for t in range(2):
        o_all = jnp.concatenate(
            [acc_s[t, h * tq:(h + 1) * tq, :KV_RANK] / acc_s[t, h * tq:(h + 1) * tq, KV_RANK:]
             for h in range(MLA_HEADS)], axis=1).astype(BF16)
        obuf[t] = jnp.concatenate(
            [_dot(o_all[:, pr * 2 * LANES:(pr + 1) * 2 * LANES], wuv_ref[pr]) for pr in range(MLA_HEADS // 2)],
            axis=1).astype(BF16)
    for cp in out_copies():
        cp.start()

    @pl.when(step == n_steps - 1)
    def _():
        for cp in out_copies():
            cp.wait()


def _mla_attn(q_a, kcat, qn_g, wq_nope, wq_pe, w_uk, w_uv, cos_t, sin_a, sin_b, batch, seq):
    tq = ATTN_TQ
    nq = seq // tq
    n_attn = MLA_HEADS * V_DIM
    lo = lambda b, i: i
    hi = lambda b, i: nq - 1 - i
    tab = lambda sel: pl.BlockSpec((tq, LANES), lambda b, i: (sel(b, i), 0))
    qa_spec = lambda sel: pl.BlockSpec((tq, Q_RANK), lambda b, i: (b * nq + sel(b, i), 0))
    return pl.pallas_call(
        functools.partial(_mla_attn_kernel, nq=nq),
        grid=(batch, nq // 2),
        in_specs=[qa_spec(lo), qa_spec(hi),
                  pl.BlockSpec((None, seq, 2 * LANES), lambda b, i: (b, 0, 0)),
                  _const_spec((1, Q_RANK)), _const_spec(wq_nope.shape), _const_spec(wq_pe.shape),
                  _const_spec(w_uk.shape), _const_spec(w_uv.shape),
                  tab(lo), tab(lo), tab(lo), tab(hi), tab(hi), tab(hi)],
        out_specs=pl.BlockSpec(memory_space=pl.ANY),
        out_shape=jax.ShapeDtypeStruct((batch * seq, n_attn), BF16),
        scratch_shapes=[pltpu.VMEM((2, MLA_HEADS * tq, 2 * LANES), BF16),
                        pltpu.VMEM((2, MLA_HEADS * tq, LANES), F32),
                        pltpu.VMEM((2, MLA_HEADS * tq, 2 * KV_RANK), F32),
                        pltpu.VMEM((2, tq, n_attn), BF16),
                        pltpu.SemaphoreType.DMA((2,))],
        compiler_params=_cparams(("arbitrary", "arbitrary")),
        name="mla_attn",
    )(q_a, q_a, kcat.reshape(batch, seq, 2 * LANES), qn_g, wq_nope, wq_pe, w_uk, w_uv,
      cos_t, sin_a, sin_b, cos_t, sin_a, sin_b)


def _mix_ffn_kernel(*refs, n_act, has_pre):
    x_ref = refs[0]
    act_refs = refs[1:1 + n_act]
    k = 1 + n_act
    pre_ref = refs[k] if has_pre else None
    k += int(has_pre)
    wo_refs = refs[k:k + n_act]
    k += n_act
    g1_ref, b1_ref, g2_ref, b2_ref, w1_ref, w2_ref, out_ref = refs[k:k + 7]

    x = x_ref[...]
    f = None
    for idx in range(n_act):
        a = act_refs[idx][...]
        if has_pre and idx == 0:
            a = _dot(a.astype(BF16), pre_ref[...])
        part = _dot(a.astype(BF16), wo_refs[idx][...])
        f = part if f is None else f + part
    x1 = _layer_norm(DN_ALPHA * x + f, g1_ref[...], b1_ref[...])
    x1b = x1.astype(BF16)
    acc = None
    for c in range(FFN_HIDDEN // FFN_HC):
        cols = slice(c * FFN_HC, (c + 1) * FFN_HC)
        h = jnp.maximum(_dot(x1b, w1_ref[:, cols]), 0.0)
        part = _dot((h * h).astype(BF16), w2_ref[cols, :])
        acc = part if acc is None else acc + part
    out_ref[...] = _layer_norm(DN_ALPHA * x1 + acc, g2_ref[...], b2_ref[...])


def _mix_ffn(x2d, acts, pre_w, wos, ln1_g, ln1_b, ln2_g, ln2_b, w1, w2, name):
    n_tok = x2d.shape[0]
    tm = min(FFN_TM, n_tok)
    row = lambda w: pl.BlockSpec((tm, w), lambda i: (i, 0))
    has_pre = pre_w is not None
    in_specs = [row(D_MODEL)] + [row(a.shape[1]) for a in acts]
    args = [x2d] + list(acts)
    if has_pre:
        in_specs.append(_const_spec(pre_w.shape))
        args.append(pre_w)
    in_specs += [_const_spec(w.shape) for w in wos]
    args += list(wos)
    in_specs += [_const_spec((1, D_MODEL))] * 4 + [_const_spec(w1.shape), _const_spec(w2.shape)]
    args += [ln1_g, ln1_b, ln2_g, ln2_b, w1, w2]
    return pl.pallas_call(
        functools.partial(_mix_ffn_kernel, n_act=len(acts), has_pre=has_pre),
        grid=(n_tok // tm,),
        in_specs=in_specs,
        out_specs=row(D_MODEL),
        out_shape=jax.ShapeDtypeStruct((n_tok, D_MODEL), F32),
        compiler_params=_cparams(("arbitrary",)),
        name=name,
    )(*args)


def _even_inproj_sample_kernel(x_ref, w_ref, kvn_ref, gmg_ref, gmb_ref, ws0_ref, bs0_ref,
                               cos_ref, sina_ref, sinb_ref, qn_ref, wqn_ref, wqp_ref, wuk_ref,
                               ckv_ref, kpe_ref, gate_ref, vn_ref, qlat_ref, qpe_ref):
    z = _dot(x_ref[...].astype(BF16), w_ref[...])
    o_c, o_u, o_v, o_k = Q_RANK, Q_RANK + KV_RANK, Q_RANK + KV_RANK + GM_WIDTH, Q_RANK + KV_RANK + 2 * GM_WIDTH
    cos_t, sin_a, sin_b = cos_ref[...], sina_ref[...], sinb_ref[...]
    c_n = _rms_norm(z[:, o_c:o_u], kvn_ref[...])
    ckv_ref[...] = c_n
    kp = _rope128(z[:, o_k:o_k + LANES], cos_t, sin_a, sin_b)
    kpe_ref[...] = kp[:, :ROPE_DIM]
    gu = jax.nn.gelu(z[:, o_u:o_v])
    v_n = _layer_norm(jax.nn.gelu(z[:, o_v:o_k]), gmg_ref[...], gmb_ref[...])
    vn_ref[...] = v_n
    mixed = ws0_ref[...].astype(F32) * v_n.astype(BF16).astype(F32) + bs0_ref[...]
    gate_ref[...] = (gu * mixed).astype(BF16)
    lat, pes = _queries(z[:, :o_c], qn_ref[...], wqn_ref[...], wqp_ref[...], wuk_ref,
                        cos_t, sin_a, sin_b)
    qlat_ref[...] = lat
    qpe_ref[...] = jnp.concatenate(pes, axis=1)


def _even_inproj_sample(xs, w_in_p, kv_norm, gm_g, gm_b, ws0, bs0, cos_t, sin_a, sin_b,
                        qn_g, wq_nope, wq_pe, w_uk):
    n = xs.shape[0]
    args = (xs, w_in_p, kv_norm, gm_g, gm_b, ws0, bs0, cos_t, sin_a, sin_b, qn_g, wq_nope, wq_pe, w_uk)
    full = lambda shape: pl.BlockSpec(shape, lambda i: (0,) * len(shape))
    widths = [(KV_RANK, F32), (ROPE_DIM, F32), (GM_WIDTH, BF16), (GM_WIDTH, F32),
              (MLA_HEADS * LANES, F32), (MLA_HEADS * LANES, F32)]
    return pl.pallas_call(
        _even_inproj_sample_kernel,
        grid=(1,),
        in_specs=[full(a.shape) for a in args],
        out_specs=[full((n, w)) for w, _ in widths],
        out_shape=[jax.ShapeDtypeStruct((n, w), dt) for w, dt in widths],
        compiler_params=_cparams(("arbitrary",)),
        name="even_inproj_sample",
    )(*args)


def _decode_attn_kernel(pt_ref, qlat_ref, qpe_ref, ckvs_ref, kpes_ref, ckv_hbm, kpe_hbm, out_ref,
                        ckv_buf, kpe_buf, sem, *, n_pg, n_groups):
    b = pl.program_id(0)
    n_b = pl.num_programs(0)
    assert n_groups % DEC_SLOTS == 0 and DEC_AHEAD + 2 <= DEC_SLOTS

    def group_copies(seq, grp):
        slot = grp % DEC_SLOTS
        cps = []
        for k in range(n_pg):
            page = pt_ref[seq, grp * n_pg + k]
            keys = pl.ds(k * PAGE_SIZE, PAGE_SIZE)
            cps.append(pltpu.make_async_copy(ckv_hbm.at[page], ckv_buf.at[slot, keys, :], sem.at[0, slot]))
            cps.append(pltpu.make_async_copy(kpe_hbm.at[page], kpe_buf.at[slot, :, keys], sem.at[1, slot]))
        return cps

    @pl.when(b == 0)
    def _():
        for g in range(DEC_AHEAD):
            for cp in group_copies(0, g):
                cp.start()

    qlat = qlat_ref[...].astype(BF16)
    qpe = qpe_ref[...][:, :ROPE_DIM].astype(BF16)

    def values(p_bf16, grp):
        return _dot(p_bf16, ckv_buf[grp % DEC_SLOTS].astype(BF16))

    m = l = acc = None
    pend = None
    for g in range(n_groups):
        if g + DEC_AHEAD < n_groups:
            for cp in group_copies(b, g + DEC_AHEAD):
                cp.start()
        else:
            @pl.when(b + 1 < n_b)
            def _():
                for cp in group_copies(b + 1, g + DEC_AHEAD - n_groups):
                    cp.start()
        for cp in group_copies(b, g):
            cp.wait()

        slot = g % DEC_SLOTS
        s = (_dot_nt(qlat, ckv_buf[slot].astype(BF16))
             + _dot(qpe, kpe_buf[slot].astype(BF16))) * ATTN_SCALE
        if pend is not None:
            p_prev, m_prev = pend
            pv = values(p_prev.astype(BF16), g - 1)
            p_sum = jnp.sum(p_prev, axis=-1, keepdims=True)
            if m is None:
                l, acc = p_sum, pv
            else:
                alpha = jnp.exp(m - m_prev)
                l, acc = alpha * l + p_sum, alpha * acc + pv
            m = m_prev
        m_cur = jnp.max(s, axis=-1, keepdims=True)
        m_new = m_cur if m is None and pend is None else jnp.maximum(pend[1], m_cur)
        pend = (jnp.exp(s - m_new), m_new)

    p_prev, m_prev = pend
    pv = values(p_prev.astype(BF16), n_groups - 1)
    alpha = jnp.exp(m - m_prev)
    l = alpha * l + jnp.sum(p_prev, axis=-1, keepdims=True)
    acc = alpha * acc + pv
    m = m_prev

    r = lambda a: a.astype(BF16).astype(F32)
    kv = r(ckvs_ref[...])
    s_self = (jnp.sum(r(qlat_ref[...]) * kv, axis=-1, keepdims=True)
              + jnp.sum(r(qpe_ref[...]) * r(kpes_ref[...]), axis=-1, keepdims=True)) * ATTN_SCALE
    m_n = jnp.maximum(m, s_self)
    a = jnp.exp(m - m_n)
    p_self = jnp.exp(s_self - m_n)
    out_ref[...] = (a * acc + r(p_self) * kv) / (a * l + p_self)


def _decode_attn(page_table, qlat2, qpe2, ckv_s3, kpe_s3, cache_ckv_e, cache_kpe_t):
    dec_batch, n_pages = page_table.shape
    n_pg = DEC_PG
    qspec = pl.BlockSpec((MLA_HEADS, LANES), lambda b, pt: (b, 0))
    self_spec = pl.BlockSpec((None, 1, LANES), lambda b, pt: (b, 0, 0))
    hbm = pl.BlockSpec(memory_space=pl.ANY)
    grid_spec = pltpu.PrefetchScalarGridSpec(
        num_scalar_prefetch=1,
        grid=(dec_batch,),
        in_specs=[qspec, qspec, self_spec, self_spec, hbm, hbm],
        out_specs=qspec,
        scratch_shapes=[pltpu.VMEM((DEC_SLOTS, n_pg * PAGE_SIZE, KV_RANK), F32),
                        pltpu.VMEM((DEC_SLOTS, ROPE_DIM, n_pg * PAGE_SIZE), F32),
                        pltpu.SemaphoreType.DMA((2, DEC_SLOTS))],
    )
    return pl.pallas_call(
        functools.partial(_decode_attn_kernel, n_pg=n_pg, n_groups=n_pages // n_pg),
        grid_spec=grid_spec,
        out_shape=jax.ShapeDtypeStruct((dec_batch * MLA_HEADS, KV_RANK), F32),
        compiler_params=_cparams(("arbitrary",)),
        name="decode_attn",
    )(page_table, qlat2, qpe2, ckv_s3, kpe_s3, cache_ckv_e, cache_kpe_t)


def _s5_readout(hcat_blocks, u, cw_ref, d_ref, wglu_ref):
    y = jnp.concatenate([_dot(hcat_blocks[k], cw_ref[k]) for k in range(S5_NB)], axis=1)
    y = jax.nn.gelu(y + d_ref[...] * u)
    return y * jax.nn.sigmoid(_dot(y.astype(BF16), wglu_ref[...]))


def _s5_prompt_kernel(x_ref, win_ref, bw_ref, a_ref, cw_ref, d_ref, wglu_ref,
                      yg_ref, hre_ref, him_ref, u_s, y_s, bu_s, h_s):
    n_b, seg, _ = x_ref.shape
    half = S5_SLABS // 2
    sub = S5_SUB
    n_sub = seg // sub
    rows_sub = sub * n_b
    q_slabs = S5_WIDTH // LANES

    @pl.when(pl.program_id(0) == 0)
    def _():
        h_s[...] = jnp.zeros(h_s.shape, F32)

    for s in range(n_sub):
        rows = slice(s * rows_sub, (s + 1) * rows_sub)
        xs = jnp.concatenate([x_ref[b, s * sub:(s + 1) * sub, :] for b in range(n_b)], axis=0)
        u = _dot(xs.astype(BF16), win_ref[...])
        for b in range(n_b):
            for q in range(q_slabs):
                u_s[s, q, pl.ds(b, sub, stride=n_b), :] = u[b * sub:(b + 1) * sub, q * LANES:(q + 1) * LANES]
        for k in range(S5_NB):
            r = _dot(u_s[s, k].astype(BF16), bw_ref[k])
            for q in range(4):
                bu_s[4 * k + q, rows, :] = r[:, q * LANES:(q + 1) * LANES]
                bu_s[half + 4 * k + q, rows, :] = r[:, (4 + q) * LANES:(5 + q) * LANES]

    h = h_s[...]
    for t in range(seg):
        rows = slice(t * n_b, (t + 1) * n_b)
        hs = bu_s[:, rows, :]
        ar, ai = a_ref[:half], a_ref[half:]
        hr, hi = h[:half], h[half:]
        h = jnp.concatenate([ar * hr - ai * hi + hs[:half], ar * hi + ai * hr + hs[half:]], axis=0)
        bu_s[:, rows, :] = h
    h_s[...] = h
    hre_ref[...] = jnp.concatenate([h[s] for s in range(half)], axis=1)
    him_ref[...] = jnp.concatenate([h[half + s] for s in range(half)], axis=1)

    for s in range(n_sub):
        rows = slice(s * rows_sub, (s + 1) * rows_sub)
        blocks = [jnp.concatenate([bu_s[4 * k + q, rows, :] for q in range(4)]
                                  + [bu_s[half + 4 * k + q, rows, :] for q in range(4)], axis=1).astype(BF16)
                  for k in range(S5_NB)]
        u_t = jnp.concatenate([u_s[s, q] for q in range(q_slabs)], axis=1)
        y = _s5_readout(blocks, u_t, cw_ref, d_ref, wglu_ref)
        for q in range(q_slabs):
            y_s[s, q] = y[:, q * LANES:(q + 1) * LANES]
        for b in range(n_b):
            yg_ref[b, s * sub:(s + 1) * sub, :] = jnp.concatenate(
                [y_s[s, q, pl.ds(b, sub, stride=n_b), :] for q in range(q_slabs)], axis=1).astype(BF16)


def _s5_prompt(x3, w_in, bw, a_b, cw, d_row, w_glu):
    batch, seq, _ = x3.shape
    seg = S5_L
    slab_rows = (seg // S5_SUB, S5_WIDTH // LANES, S5_SUB * batch, LANES)
    return pl.pallas_call(
        _s5_prompt_kernel,
        grid=(seq // seg,),
        in_specs=[pl.BlockSpec((batch, seg, D_MODEL), lambda c: (0, c, 0)),
                  _const_spec(w_in.shape), _const_spec(bw.shape), _const_spec(a_b.shape),
                  _const_spec(cw.shape), _const_spec(d_row.shape), _const_spec(w_glu.shape)],
        out_specs=[pl.BlockSpec((batch, seg, S5_WIDTH), lambda c: (0, c, 0)),
                   pl.BlockSpec((batch, S5_STATES), lambda c: (0, 0)),
                   pl.BlockSpec((batch, S5_STATES), lambda c: (0, 0))],
        out_shape=[jax.ShapeDtypeStruct((batch, seq, S5_WIDTH), BF16),
                   jax.ShapeDtypeStruct((batch, S5_STATES), F32),
                   jax.ShapeDtypeStruct((batch, S5_STATES), F32)],
        scratch_shapes=[pltpu.VMEM(slab_rows, F32), pltpu.VMEM(slab_rows, F32),
                        pltpu.VMEM((S5_SLABS, seg * batch, LANES), F32),
                        pltpu.VMEM((S5_SLABS, batch, LANES), F32)],
        compiler_params=_cparams(("arbitrary",)),
        name="s5_prompt",
    )(x3, w_in, bw, a_b, cw, d_row, w_glu)


def _s5_sample_kernel(x_ref, h0r_ref, h0i_ref, win_ref, bw_ref, ar_ref, ai_ref, cw_ref, d_ref, wglu_ref,
                      yg_ref, hre_ref, him_ref):
    u = _dot(x_ref[...].astype(BF16), win_ref[...])
    ub = u.astype(BF16)
    bu = [_dot(ub[:, k * LANES:(k + 1) * LANES], bw_ref[k]) for k in range(S5_NB)]
    w = S5_KB * S5_STATE
    bu_r = jnp.concatenate([r[:, :w] for r in bu], axis=1)
    bu_i = jnp.concatenate([r[:, w:] for r in bu], axis=1)
    ar, ai, h0r, h0i = ar_ref[...], ai_ref[...], h0r_ref[...], h0i_ref[...]
    hr = bu_r + (ar * h0r - ai * h0i)
    hi = bu_i + (ar * h0i + ai * h0r)
    hre_ref[...] = hr
    him_ref[...] = hi
    blocks = [jnp.concatenate([hr[:, k * w:(k + 1) * w], hi[:, k * w:(k + 1) * w]], axis=1).astype(BF16)
              for k in range(S5_NB)]
    yg_ref[...] = _s5_readout(blocks, u, cw_ref, d_ref, wglu_ref).astype(BF16)


def _s5_sample(xs, h0r, h0i, w_in, bw, a_r, a_i, cw, d_row, w_glu):
    n = xs.shape[0]
    args = (xs, h0r, h0i, w_in, bw, a_r, a_i, cw, d_row, w_glu)
    full = lambda shape: pl.BlockSpec(shape, lambda i: (0,) * len(shape))
    return pl.pallas_call(
        _s5_sample_kernel,
        grid=(1,),
        in_specs=[full(a.shape) for a in args],
        out_specs=[full((n, S5_WIDTH)), full((n, S5_STATES)), full((n, S5_STATES))],
        out_shape=[jax.ShapeDtypeStruct((n, S5_WIDTH), BF16),
                   jax.ShapeDtypeStruct((n, S5_STATES), F32),
                   jax.ShapeDtypeStruct((n, S5_STATES), F32)],
        compiler_params=_cparams(("arbitrary",)),
        name="s5_sample",
    )(*args)


def _rope_tables(pos):
    freqs = ROPE_THETA ** (-jnp.arange(ROPE_HALF, dtype=F32) / ROPE_HALF)
    ang = pos.astype(F32)[:, None] * freqs
    cos, sin = jnp.cos(ang), jnp.sin(ang)
    zero = jnp.zeros_like(cos)
    pad = jnp.zeros((pos.shape[0], LANES - ROPE_DIM), F32)
    cos_t = jnp.concatenate([cos, cos, pad], axis=1)
    sin_a = jnp.concatenate([zero, sin, pad], axis=1)
    sin_b = jnp.concatenate([-sin, zero, pad], axis=1)
    return cos_t, sin_a, sin_b


def _block_diag(blocks):
    n, r, c = blocks.shape
    eye = jnp.eye(n, dtype=blocks.dtype)
    return jnp.einsum('nrc,nm->nrmc', blocks, eye).reshape(n * r, n * c)


def _even_params(w_in, q_norm, w_q_b, kv_norm, w_kv_b, gm_g, gm_b, gm_w_s, gm_b_s, w_out):
    i1, i2, i3, i4 = Q_RANK, Q_RANK + KV_RANK, Q_RANK + KV_RANK + ROPE_DIM, Q_RANK + KV_RANK + ROPE_DIM + GM_WIDTH
    w_in_p = jnp.concatenate([w_in[:, :i2], w_in[:, i3:], w_in[:, i2:i3],
                              jnp.zeros((D_MODEL, LANES - ROPE_DIM), F32)], axis=1).astype(BF16)
    wq_nope = w_q_b[:, :, :NOPE_DIM].reshape(Q_RANK, MLA_HEADS * NOPE_DIM).astype(BF16)
    pe = w_q_b[:, :, NOPE_DIM:]
    pe_sw = jnp.concatenate([pe[:, :, ROPE_HALF:], pe[:, :, :ROPE_HALF]], axis=2)
    lane_pad = jnp.zeros((Q_RANK, MLA_HEADS, LANES - ROPE_DIM), F32)
    wq_pe = jnp.concatenate(
        [jnp.concatenate([w, lane_pad], axis=2).reshape(Q_RANK, MLA_HEADS * LANES) for w in (pe, pe_sw)],
        axis=1).astype(BF16)
    uk = jnp.transpose(w_kv_b[:, :, :NOPE_DIM], (1, 2, 0))
    uv = jnp.transpose(w_kv_b[:, :, NOPE_DIM:], (1, 0, 2))
    pairs = range(MLA_HEADS // 2)
    w_uk = jnp.stack([_block_diag(uk[2 * pr:2 * pr + 2]) for pr in pairs]).astype(BF16)
    w_uv_p = jnp.stack([_block_diag(uv[2 * pr:2 * pr + 2]) for pr in pairs]).astype(BF16)
    w_uv = _block_diag(uv).astype(BF16)
    causal = jnp.tril(jnp.ones((GM_CHUNK, GM_CHUNK), dtype=bool))
    ws_m = jnp.where(causal[None], gm_w_s, 0).astype(BF16)
    bs_full = jnp.repeat(gm_b_s.T, GM_HEAD_DIM, axis=1)
    ws0 = jnp.repeat(gm_w_s[:, 0, 0], GM_HEAD_DIM)[None, :].astype(BF16)
    bs0 = bs_full[:1]
    n_attn = MLA_HEADS * V_DIM
    return dict(w_in_p=w_in_p, qn=q_norm[None, :], wq_nope=wq_nope, wq_pe=wq_pe, w_uk=w_uk, w_uv=w_uv, w_uv_p=w_uv_p,
                kvn=kv_norm[None, :], gm_g=gm_g[None, :], gm_b=gm_b[None, :], ws_m=ws_m, bs_full=bs_full,
                ws0=ws0, bs0=bs0, wo_attn=w_out[:n_attn].astype(BF16), wo_gate=w_out[n_attn:].astype(BF16))


def _s5_params(w_in, a_re, a_im, b_re, b_im, c_re, c_im, d, log_dt, w_glu, w_out, batch):
    dt = jnp.exp(log_dt)[:, None]
    ld_r, ld_i = a_re * dt, a_im * dt
    mag = jnp.exp(ld_r)
    ab_r, ab_i = mag * jnp.cos(ld_i), mag * jnp.sin(ld_i)
    den = a_re * a_re + a_im * a_im
    cr = ((ab_r - 1.0) * a_re + ab_i * a_im) / den
    ci = (ab_i * a_re - (ab_r - 1.0) * a_im) / den
    bb_r = cr[..., None] * b_re - ci[..., None] * b_im
    bb_i = cr[..., None] * b_im + ci[..., None] * b_re

    def in_blocks(bb):
        return jnp.stack([_block_diag(jnp.transpose(bb[k * S5_KB:(k + 1) * S5_KB], (0, 2, 1)))
                          for k in range(S5_NB)])

    def out_blocks(cc):
        return jnp.stack([_block_diag(jnp.transpose(cc[k * S5_KB:(k + 1) * S5_KB], (0, 2, 1)))
                          for k in range(S5_NB)])

    bw = jnp.concatenate([in_blocks(bb_r), in_blocks(bb_i)], axis=2).astype(BF16)
    cw = jnp.concatenate([out_blocks(c_re), out_blocks(-c_im)], axis=1).astype(BF16)
    a_r, a_i = ab_r.reshape(1, S5_STATES), ab_i.reshape(1, S5_STATES)
    half = S5_SLABS // 2
    a_b = jnp.concatenate([jnp.broadcast_to(a_r.reshape(half, 1, LANES), (half, batch, LANES)),
                           jnp.broadcast_to(a_i.reshape(half, 1, LANES), (half, batch, LANES))], axis=0)
    return dict(w_in=w_in.astype(BF16), bw=bw, cw=cw, a_r=a_r, a_i=a_i, a_b=a_b,
                d=d.reshape(1, S5_WIDTH), w_glu=w_glu.astype(BF16), w_out=w_out.astype(BF16))


def kernel(x_prompt, x_sample, cache_ckv, cache_kpe, state_s5_re, state_s5_im, page_table, w_in_even, mla_q_norm, mla_w_q_b, mla_kv_norm, mla_w_kv_b, gm_norm_g, gm_norm_b, gm_w_s, gm_b_s, w_out_even, w_in_odd, s5_a_re, s5_a_im, s5_b_re, s5_b_im, s5_c_re, s5_c_im, s5_d, s5_log_dt, s5_w_glu, w_out_odd, ln_mix_g, ln_mix_b, ln_ffn_g, ln_ffn_b, ffn_w1, ffn_w2):
    batch, seq, _ = x_prompt.shape
    dec_batch, dec_seq, _ = x_sample.shape
    assert dec_seq == 1 and seq % INPROJ_TM == 0 and seq % ATTN_TQ == 0 and seq % S5_L == 0
    assert (seq // ATTN_TQ) % 2 == 0
    assert (batch * seq) % FFN_TM == 0 and page_table.shape[1] % DEC_PG == 0
    assert batch == SUBLANES

    xp = x_prompt.reshape(batch * seq, D_MODEL)
    xs = x_sample.reshape(dec_batch, D_MODEL)
    tabs_p = _rope_tables(jnp.arange(seq, dtype=jnp.int32))
    tabs_s = _rope_tables(PAST_LEN + jnp.arange(dec_seq, dtype=jnp.int32))

    outs = {k: [] for k in ("ckv_p", "kpe_p", "ckv_s", "kpe_s", "gmv_p", "gmv_s",
                            "s5re_p", "s5im_p", "s5re_s", "s5im_s")}
    for layer in range(DEPTH):
        ln = (ln_mix_g[layer][None, :], ln_mix_b[layer][None, :],
              ln_ffn_g[layer][None, :], ln_ffn_b[layer][None, :],
              ffn_w1[layer].astype(BF16), ffn_w2[layer].astype(BF16))
        if layer % 2 == 0:
            e = layer // 2
            p = _even_params(w_in_even[e], mla_q_norm[e], mla_w_q_b[e], mla_kv_norm[e], mla_w_kv_b[e],
                             gm_norm_g[e], gm_norm_b[e], gm_w_s[e], gm_b_s[e], w_out_even[e])
            q_a, ckv, kpe, kcat, gate, gmv = _even_inproj(
                xp, p["w_in_p"], p["kvn"], p["gm_g"], p["gm_b"], p["ws_m"], p["bs_full"], *tabs_p, batch, seq)
            attn = _mla_attn(q_a, kcat, p["qn"], p["wq_nope"], p["wq_pe"], p["w_uk"], p["w_uv_p"], *tabs_p,
                             batch, seq)
            xp = _mix_ffn(xp, [attn, gate], None, [p["wo_attn"], p["wo_gate"]], *ln, name="even_ffn_prompt")
            outs["ckv_p"].append(ckv.reshape(batch, seq, KV_RANK))
            outs["kpe_p"].append(kpe.reshape(batch, seq, ROPE_DIM))
            outs["gmv_p"].append(gmv)
            ckv_s, kpe_s, gate_s, vn_s, qlat_s, qpe_s = _even_inproj_sample(
                xs, p["w_in_p"], p["kvn"], p["gm_g"], p["gm_b"], p["ws0"], p["bs0"], *tabs_s,
                p["qn"], p["wq_nope"], p["wq_pe"], p["w_uk"])
            kpe_pad = jnp.concatenate([kpe_s, jnp.zeros((dec_batch, LANES - ROPE_DIM), F32)], axis=1)
            o_lat = _decode_attn(page_table,
                                 qlat_s.reshape(dec_batch * MLA_HEADS, LANES),
                                 qpe_s.reshape(dec_batch * MLA_HEADS, LANES),
                                 ckv_s.reshape(dec_batch, 1, KV_RANK), kpe_pad.reshape(dec_batch, 1, LANES),
                                 cache_ckv[e], jnp.swapaxes(cache_kpe[e], 1, 2))
            xs = _mix_ffn(xs, [o_lat.reshape(dec_batch, MLA_HEADS * KV_RANK), gate_s], p["w_uv"],
                          [p["wo_attn"], p["wo_gate"]], *ln, name="even_ffn_sample")
            outs["ckv_s"].append(ckv_s.reshape(dec_batch, 1, KV_RANK))
            outs["kpe_s"].append(kpe_s.reshape(dec_batch, 1, ROPE_DIM))
            outs["gmv_s"].append(vn_s.reshape(dec_batch, 1, GM_WIDTH))
        else:
            o = layer // 2
            p = _s5_params(w_in_odd[o], s5_a_re[o], s5_a_im[o], s5_b_re[o], s5_b_im[o], s5_c_re[o],
                           s5_c_im[o], s5_d[o], s5_log_dt[o], s5_w_glu[o], w_out_odd[o], batch)
            yg, hre, him = _s5_prompt(xp.reshape(batch, seq, D_MODEL), p["w_in"], p["bw"], p["a_b"],
                                      p["cw"], p["d"], p["w_glu"])
            xp = _mix_ffn(xp, [yg.reshape(batch * seq, S5_WIDTH)], None, [p["w_out"]], *ln,
                          name="odd_ffn_prompt")
            outs["s5re_p"].append(hre.reshape(batch, S5_GROUPS, S5_STATE))
            outs["s5im_p"].append(him.reshape(batch, S5_GROUPS, S5_STATE))
            yg_s, hre_s, him_s = _s5_sample(
                xs, state_s5_re[o].reshape(dec_batch, S5_STATES), state_s5_im[o].reshape(dec_batch, S5_STATES),
                p["w_in"], p["bw"], p["a_r"], p["a_i"], p["cw"], p["d"], p["w_glu"])
            xs = _mix_ffn(xs, [yg_s], None, [p["w_out"]], *ln, name="odd_ffn_sample")
            outs["s5re_s"].append(hre_s.reshape(dec_batch, S5_GROUPS, S5_STATE))
            outs["s5im_s"].append(him_s.reshape(dec_batch, S5_GROUPS, S5_STATE))

    st = jnp.stack
    return (xp.reshape(batch, seq, D_MODEL), xs.reshape(dec_batch, dec_seq, D_MODEL),
            st(outs["ckv_p"]), st(outs["kpe_p"]), st(outs["ckv_s"]), st(outs["kpe_s"]),
            st(outs["gmv_p"]), st(outs["gmv_s"]),
            st(outs["s5re_p"]), st(outs["s5im_p"]), st(outs["s5re_s"]), st(outs["s5im_s"]))
```

```python
import functools
import math

import jax
import jax.numpy as jnp
from jax import lax
from jax.experimental import pallas as pl
from jax.experimental.pallas import tpu as pltpu

F32 = jnp.float32
BF16 = jnp.bfloat16

D_MODEL = 1024
DEPTH = 2
PAST_LEN = 16384
PAGE_SIZE = 128
MLA_HEADS = 8
NOPE_DIM = 64
ROPE_DIM = 32
ROPE_HALF = ROPE_DIM // 2
V_DIM = 64
Q_RANK = 256
KV_RANK = 128
ROPE_THETA = 10000.0
ATTN_SCALE = 1.0 / math.sqrt(NOPE_DIM + ROPE_DIM)
ATTN_SCALE_LOG2 = ATTN_SCALE * math.log2(math.e)
GM_HEADS = 8
GM_HEAD_DIM = 64
GM_WIDTH = GM_HEADS * GM_HEAD_DIM
GM_CHUNK = 128
S5_GROUP_DIM = 16
S5_GROUPS = 32
S5_STATE = 64
S5_WIDTH = S5_GROUPS * S5_GROUP_DIM
S5_STATES = S5_GROUPS * S5_STATE
FFN_HIDDEN = 4 * D_MODEL
DN_ALPHA = (2 * DEPTH) ** 0.25
LN_EPS = 1e-5
RMS_EPS = 1e-6

LANES = 128
SUBLANES = 8
VMEM_LIMIT_BYTES = 56 * 1024 * 1024

INPROJ_TM = 512
ATTN_TQ = 256
ATTN_ROW_BLOCK = 256
FFN_TM = 1024
FFN_HC = 1024
FFN_ROW_BLOCK = 256
S5_L = 128
S5_SUB = 16
DEC_PG = 16
DEC_SLOTS = 4
DEC_AHEAD = 2
S5_KB = 8
S5_NB = S5_GROUPS // S5_KB
S5_SLABS = 2 * S5_STATES // LANES


def _cparams(sem):
    return pltpu.CompilerParams(dimension_semantics=sem, vmem_limit_bytes=VMEM_LIMIT_BYTES)


def _const_spec(shape):
    n = len(shape)
    return pl.BlockSpec(shape, lambda *_: (0,) * n, pipeline_mode=pl.Buffered(1))


def _layer_norm(x, g, b):
    mu = jnp.mean(x, axis=-1, keepdims=True)
    xc = x - mu
    var = jnp.mean(xc * xc, axis=-1, keepdims=True)
    return xc * lax.rsqrt(var + LN_EPS) * g + b


def _rms_norm(x, g):
    return x * lax.rsqrt(jnp.mean(x * x, axis=-1, keepdims=True) + RMS_EPS) * g


def _rope128(x, cos_t, sin_a, sin_b):
    return (x * cos_t + pltpu.roll(x, ROPE_HALF, 1) * sin_a
            + pltpu.roll(x, LANES - ROPE_HALF, 1) * sin_b)


def _dot(a, b):
    return jnp.dot(a, b, preferred_element_type=F32)


def _dot_nt(a, b):
    return lax.dot_general(a, b, (((1,), (1,)), ((), ())), preferred_element_type=F32)


def _even_inproj_kernel(x_ref, w_ref, kvn_ref, gmg_ref, gmb_ref, ws_ref, bs_ref,
                        cos_ref, sina_ref, sinb_ref,
                        qa_ref, ckv_ref, kpe_ref, kcat_ref, gate_ref, gmv_ref, *, tiles_per_seq):
    tm = x_ref.shape[0]
    z = _dot(x_ref[...].astype(BF16), w_ref[...])
    o_c, o_u, o_v, o_k = Q_RANK, Q_RANK + KV_RANK, Q_RANK + KV_RANK + GM_WIDTH, Q_RANK + KV_RANK + 2 * GM_WIDTH
    qa_ref[...] = z[:, :o_c]
    c_n = _rms_norm(z[:, o_c:o_u], kvn_ref[...])
    ckv_ref[...] = c_n
    kp = _rope128(z[:, o_k:o_k + LANES], cos_ref[...], sina_ref[...], sinb_ref[...])
    kpe_ref[...] = kp[:, :ROPE_DIM]
    kcat_ref[...] = jnp.concatenate([c_n, kp], axis=1).astype(BF16)

    gu = jax.nn.gelu(z[:, o_u:o_v])
    v_n = _layer_norm(jax.nn.gelu(z[:, o_v:o_k]), gmg_ref[...], gmb_ref[...])

    @pl.when(pl.program_id(0) % tiles_per_seq == tiles_per_seq - 1)
    def _():
        gmv_ref[0] = v_n[tm - GM_CHUNK:, :]

    lane = lax.broadcasted_iota(jnp.int32, (GM_CHUNK, LANES), 1)
    lo = lane < GM_HEAD_DIM
    bs = bs_ref[...]
    for ci in range(tm // GM_CHUNK):
        rows = slice(ci * GM_CHUNK, (ci + 1) * GM_CHUNK)
        pieces = []
        for pr in range(GM_WIDTH // LANES):
            r = v_n[rows, pr * LANES:(pr + 1) * LANES]
            m0 = _dot(ws_ref[2 * pr], jnp.where(lo, r, 0.0).astype(BF16))
            m1 = _dot(ws_ref[2 * pr + 1], jnp.where(lo, 0.0, r).astype(BF16))
            pieces.append(m0 + m1)
        mixed = jnp.concatenate(pieces, axis=1) + bs
        gate_ref[rows, :] = (gu[rows, :] * mixed).astype(BF16)


def _even_inproj(x2d, w_in_p, kv_norm, gm_g, gm_b, ws_m, bs_full, cos_t, sin_a, sin_b, batch, seq):
    n_tok = x2d.shape[0]
    tm = INPROJ_TM
    tiles_per_seq = seq // tm
    row = lambda w: pl.BlockSpec((tm, w), lambda i: (i, 0))
    tab = pl.BlockSpec((tm, LANES), lambda i: (i % tiles_per_seq, 0))
    n_in = w_in_p.shape[1]
    return pl.pallas_call(
        functools.partial(_even_inproj_kernel, tiles_per_seq=tiles_per_seq),
        grid=(n_tok // tm,),
        in_specs=[row(D_MODEL), _const_spec((D_MODEL, n_in)), _const_spec((1, KV_RANK)),
                  _const_spec((1, GM_WIDTH)), _const_spec((1, GM_WIDTH)),
                  _const_spec((GM_HEADS, GM_CHUNK, GM_CHUNK)), _const_spec((GM_CHUNK, GM_WIDTH)),
                  tab, tab, tab],
        out_specs=[row(Q_RANK), row(KV_RANK), row(ROPE_DIM), row(2 * LANES), row(GM_WIDTH),
                   pl.BlockSpec((1, GM_CHUNK, GM_WIDTH), lambda i: (i // tiles_per_seq, 0, 0))],
        out_shape=[jax.ShapeDtypeStruct((n_tok, Q_RANK), F32),
                   jax.ShapeDtypeStruct((n_tok, KV_RANK), F32),
                   jax.ShapeDtypeStruct((n_tok, ROPE_DIM), F32),
                   jax.ShapeDtypeStruct((n_tok, 2 * LANES), BF16),
                   jax.ShapeDtypeStruct((n_tok, GM_WIDTH), BF16),
                   jax.ShapeDtypeStruct((batch, GM_CHUNK, GM_WIDTH), F32)],
        compiler_params=_cparams(("arbitrary",)),
        name="even_inproj",
    )(x2d, w_in_p, kv_norm, gm_g, gm_b, ws_m, bs_full, cos_t, sin_a, sin_b)


def _queries(q_a, qn_g, wq_nope, wq_pe, wuk_ref, cos_t, sin_a, sin_b):
    qn = _rms_norm(q_a, qn_g).astype(BF16)
    nope = _dot(qn, wq_nope).astype(BF16)
    lat = jnp.concatenate([_dot(nope[:, pr * LANES:(pr + 1) * LANES], wuk_ref[pr])
                           for pr in range(MLA_HEADS // 2)], axis=1)
    pe = _dot(qn, wq_pe)
    n = MLA_HEADS * LANES
    sin_s = sin_a + sin_b
    pes = [pe[:, h * LANES:(h + 1) * LANES] * cos_t + pe[:, n + h * LANES:n + (h + 1) * LANES] * sin_s
           for h in range(MLA_HEADS)]
    return lat, pes


def _mla_attn_kernel(qa_lo_ref, qa_hi_ref, kcat_ref, qn_ref, wqn_ref, wqp_ref, wuk_ref, wuv_ref,
                     cos_lo_ref, sina_lo_ref, sinb_lo_ref, cos_hi_ref, sina_hi_ref, sinb_hi_ref,
                     out_hbm, qcat_s, m_s, acc_s, obuf, osem, *, nq):
    tq = qa_lo_ref.shape[0]
    n_rows = MLA_HEADS * tq
    b, i = pl.program_id(0), pl.program_id(1)
    step = b * pl.num_programs(1) + i
    n_steps = pl.num_programs(0) * pl.num_programs(1)
    q_tiles = (i, nq - 1 - i)

    def out_copies():
        return [pltpu.make_async_copy(
            obuf.at[t], out_hbm.at[pl.ds(pl.multiple_of((b * nq + q_tiles[t]) * tq, tq), tq), :], osem.at[t])
            for t in range(2)]

    qa = (qa_lo_ref, qa_hi_ref)
    tabs = ((cos_lo_ref, sina_lo_ref, sinb_lo_ref), (cos_hi_ref, sina_hi_ref, sinb_hi_ref))
    for t in range(2):
        lat, pes = _queries(qa[t][...], qn_ref[...], wqn_ref[...], wqp_ref[...], wuk_ref,
                            *(r[...] for r in tabs[t]))
        for h in range(MLA_HEADS):
            qcat_s[t, h * tq:(h + 1) * tq, :] = (ATTN_SCALE_LOG2 * jnp.concatenate(
                [lat[:, h * LANES:(h + 1) * LANES], pes[h]], axis=1)).astype(BF16)

    lanes_x = lambda a, n: jnp.concatenate([a] * n, axis=1)
    rb = ATTN_ROW_BLOCK
    ones = jnp.ones((tq, LANES), BF16)

    def keys(kv_tile):
        k = kcat_ref[pl.ds(pl.multiple_of(kv_tile * tq, tq), tq), :]
        return k, jnp.concatenate([k[:, :KV_RANK], ones], axis=1)

    q_pos = lax.broadcasted_iota(jnp.int32, (rb, tq), 0) & (tq - 1)
    k_pos = lax.broadcasted_iota(jnp.int32, (rb, tq), 1)
    causal = k_pos <= q_pos
    for t in range(2):
        k, v1 = keys(q_tiles[t])
        for r0 in range(0, n_rows, rb):
            rows = slice(r0, r0 + rb)
            s = jnp.where(causal, _dot_nt(qcat_s[t, rows, :], k), -jnp.inf)
            m0 = jnp.broadcast_to(jnp.max(s, axis=-1, keepdims=True), (rb, LANES))
            m_s[t, rows, :] = m0
            acc_s[t, rows, :] = _dot(jnp.exp2(s - lanes_x(m0, tq // LANES)).astype(BF16), v1)

    for u in range(nq - 1):
        hi = u >= i
        t = hi.astype(jnp.int32)
        k, v1 = keys(jnp.where(hi, u - i, u))
        for r0 in range(0, n_rows, rb):
            rows = slice(r0, r0 + rb)
            s = _dot_nt(qcat_s[t, rows, :], k)
            m_old = m_s[t, rows, :]
            m_new = jnp.maximum(m_old, jnp.max(s, axis=-1, keepdims=True))
            alpha = jnp.exp2(m_old - m_new)
            p = jnp.exp2(s - lanes_x(m_new, tq // LANES))
            acc_s[t, rows, :] = lanes_x(alpha, 2) * acc_s[t, rows, :] + _dot(p.astype(BF16), v1)
            m_s[t, rows, :] = m_new

    @pl.when(step > 0)
    def _():
        for cp in out_copies():
            cp.wait()

    for t in range(2):
        o_all = jnp.concatenate(
            [acc_s[t, h * tq:(h + 1) * tq, :KV_RANK] / acc_s[t, h * tq:(h + 1) * tq, KV_RANK:]
             for h in range(MLA_HEADS)], axis=1).astype(BF16)
        obuf[t] = jnp.concatenate(
            [_dot(o_all[:, pr * 2 * LANES:(pr + 1) * 2 * LANES], wuv_ref[pr]) for pr in range(MLA_HEADS // 2)],
            axis=1).astype(BF16)
    for cp in out_copies():
        cp.start()

    @pl.when(step == n_steps - 1)
    def _():
        for cp in out_copies():
            cp.wait()


def _mla_attn(q_a, kcat, qn_g, wq_nope, wq_pe, w_uk, w_uv, cos_t, sin_a, sin_b, batch, seq):
    tq = ATTN_TQ
    nq = seq // tq
    n_attn = MLA_HEADS * V_DIM
    lo = lambda b, i: i
    hi = lambda b, i: nq - 1 - i
    tab = lambda sel: pl.BlockSpec((tq, LANES), lambda b, i: (sel(b, i), 0))
    qa_spec = lambda sel: pl.BlockSpec((tq, Q_RANK), lambda b, i: (b * nq + sel(b, i), 0))
    return pl.pallas_call(
        functools.partial(_mla_attn_kernel, nq=nq),
        grid=(batch, nq // 2),
        in_specs=[qa_spec(lo), qa_spec(hi),
                  pl.BlockSpec((None, seq, 2 * LANES), lambda b, i: (b, 0, 0)),
                  _const_spec((1, Q_RANK)), _const_spec(wq_nope.shape), _const_spec(wq_pe.shape),
                  _const_spec(w_uk.shape), _const_spec(w_uv.shape),
                  tab(lo), tab(lo), tab(lo), tab(hi), tab(hi), tab(hi)],
        out_specs=pl.BlockSpec(memory_space=pl.ANY),
        out_shape=jax.ShapeDtypeStruct((batch * seq, n_attn), BF16),
        scratch_shapes=[pltpu.VMEM((2, MLA_HEADS * tq, 2 * LANES), BF16),
                        pltpu.VMEM((2, MLA_HEADS * tq, LANES), F32),
                        pltpu.VMEM((2, MLA_HEADS * tq, 2 * KV_RANK), F32),
                        pltpu.VMEM((2, tq, n_attn), BF16),
                        pltpu.SemaphoreType.DMA((2,))],
        compiler_params=_cparams(("arbitrary", "arbitrary")),
        name="mla_attn",
    )(q_a, q_a, kcat.reshape(batch, seq, 2 * LANES), qn_g, wq_nope, wq_pe, w_uk, w_uv,
      cos_t, sin_a, sin_b, cos_t, sin_a, sin_b)


def _mix_ffn_kernel(*refs, n_act, has_pre):
    x_ref = refs[0]
    act_refs = refs[1:1 + n_act]
    k = 1 + n_act
    pre_ref = refs[k] if has_pre else None
    k += int(has_pre)
    wo_refs = refs[k:k + n_act]
    k += n_act
    g1_ref, b1_ref, g2_ref, b2_ref, w1_ref, w2_ref, out_ref = refs[k:k + 7]

    tm = x_ref.shape[0]
    rb = min(FFN_ROW_BLOCK, tm)
    blocks = [slice(r0, r0 + rb) for r0 in range(0, tm, rb)]
    fs = []
    for rows in blocks:
        f = None
        for idx in range(n_act):
            a = act_refs[idx][rows, :]
            if has_pre and idx == 0:
                a = _dot(a.astype(BF16), pre_ref[...])
            part = _dot(a.astype(BF16), wo_refs[idx][...])
            f = part if f is None else f + part
        fs.append(f)
    x1s = [_layer_norm(DN_ALPHA * x_ref[rows, :] + f, g1_ref[...], b1_ref[...]) for rows, f in zip(blocks, fs)]
    x1bs = [x1.astype(BF16) for x1 in x1s]
    accs = [None] * len(blocks)
    for c in range(FFN_HIDDEN // FFN_HC):
        cols = slice(c * FFN_HC, (c + 1) * FFN_HC)
        for i, x1b in enumerate(x1bs):
            h = jnp.maximum(_dot(x1b, w1_ref[:, cols]), 0.0)
            part = _dot((h * h).astype(BF16), w2_ref[cols, :])
            accs[i] = part if accs[i] is None else accs[i] + part
    for rows, x1, acc in zip(blocks, x1s, accs):
        out_ref[rows, :] = _layer_norm(DN_ALPHA * x1 + acc, g2_ref[...], b2_ref[...])


def _mix_ffn(x2d, acts, pre_w, wos, ln1_g, ln1_b, ln2_g, ln2_b, w1, w2, name):
    n_tok = x2d.shape[0]
    tm = min(FFN_TM, n_tok)
    row = lambda w: pl.BlockSpec((tm, w), lambda i: (i, 0))
    has_pre = pre_w is not None
    in_specs = [row(D_MODEL)] + [row(a.shape[1]) for a in acts]
    args = [x2d] + list(acts)
    if has_pre:
        in_specs.append(_const_spec(pre_w.shape))
        args.append(pre_w)
    in_specs += [_const_spec(w.shape) for w in wos]
    args += list(wos)
    in_specs += [_const_spec((1, D_MODEL))] * 4 + [_const_spec(w1.shape), _const_spec(w2.shape)]
    args += [ln1_g, ln1_b, ln2_g, ln2_b, w1, w2]
    return pl.pallas_call(
        functools.partial(_mix_ffn_kernel, n_act=len(acts), has_pre=has_pre),
        grid=(n_tok // tm,),
        in_specs=in_specs,
        out_specs=row(D_MODEL),
        out_shape=jax.ShapeDtypeStruct((n_tok, D_MODEL), F32),
        compiler_params=_cparams(("arbitrary",)),
        name=name,
    )(*args)


def _even_inproj_sample_kernel(x_ref, w_ref, kvn_ref, gmg_ref, gmb_ref, ws0_ref, bs0_ref,
                               cos_ref, sina_ref, sinb_ref, qn_ref, wqn_ref, wqp_ref, wuk_ref,
                               ckv_ref, kpe_ref, gate_ref, vn_ref, qlat_ref, qpe_ref):
    z = _dot(x_ref[...].astype(BF16), w_ref[...])
    o_c, o_u, o_v, o_k = Q_RANK, Q_RANK + KV_RANK, Q_RANK + KV_RANK + GM_WIDTH, Q_RANK + KV_RANK + 2 * GM_WIDTH
    cos_t, sin_a, sin_b = cos_ref[...], sina_ref[...], sinb_ref[...]
    c_n = _rms_norm(z[:, o_c:o_u], kvn_ref[...])
    ckv_ref[...] = c_n
    kp = _rope128(z[:, o_k:o_k + LANES], cos_t, sin_a, sin_b)
    kpe_ref[...] = kp[:, :ROPE_DIM]
    gu = jax.nn.gelu(z[:, o_u:o_v])
    v_n = _layer_norm(jax.nn.gelu(z[:, o_v:o_k]), gmg_ref[...], gmb_ref[...])
    vn_ref[...] = v_n
    mixed = ws0_ref[...].astype(F32) * v_n.astype(BF16).astype(F32) + bs0_ref[...]
    gate_ref[...] = (gu * mixed).astype(BF16)
    lat, pes = _queries(z[:, :o_c], qn_ref[...], wqn_ref[...], wqp_ref[...], wuk_ref,
                        cos_t, sin_a, sin_b)
    qlat_ref[...] = lat
    qpe_ref[...] = jnp.concatenate(pes, axis=1)


def _even_inproj_sample(xs, w_in_p, kv_norm, gm_g, gm_b, ws0, bs0, cos_t, sin_a, sin_b,
                        qn_g, wq_nope, wq_pe, w_uk):
    n = xs.shape[0]
    args = (xs, w_in_p, kv_norm, gm_g, gm_b, ws0, bs0, cos_t, sin_a, sin_b, qn_g, wq_nope, wq_pe, w_uk)
    full = lambda shape: pl.BlockSpec(shape, lambda i: (0,) * len(shape))
    widths = [(KV_RANK, F32), (ROPE_DIM, F32), (GM_WIDTH, BF16), (GM_WIDTH, F32),
              (MLA_HEADS * LANES, F32), (MLA_HEADS * LANES, F32)]
    return pl.pallas_call(
        _even_inproj_sample_kernel,
        grid=(1,),
        in_specs=[full(a.shape) for a in args],
        out_specs=[full((n, w)) for w, _ in widths],
        out_shape=[jax.ShapeDtypeStruct((n, w), dt) for w, dt in widths],
        compiler_params=_cparams(("arbitrary",)),
        name="even_inproj_sample",
    )(*args)


def _decode_attn_kernel(pt_ref, qlat_ref, qpe_ref, ckvs_ref, kpes_ref, ckv_hbm, kpe_hbm, out_ref,
                        ckv_buf, kpe_buf, sem, *, n_pg, n_groups):
    b = pl.program_id(0)
    n_b = pl.num_programs(0)
    assert n_groups % DEC_SLOTS == 0 and DEC_AHEAD + 2 <= DEC_SLOTS

    def group_copies(seq, grp):
        slot = grp % DEC_SLOTS
        cps = []
        for k in range(n_pg):
            page = pt_ref[seq, grp * n_pg + k]
            keys = pl.ds(k * PAGE_SIZE, PAGE_SIZE)
            cps.append(pltpu.make_async_copy(ckv_hbm.at[page], ckv_buf.at[slot, keys, :], sem.at[0, slot]))
            cps.append(pltpu.make_async_copy(kpe_hbm.at[page], kpe_buf.at[slot, :, keys], sem.at[1, slot]))
        return cps

    @pl.when(b == 0)
    def _():
        for g in range(DEC_AHEAD):
            for cp in group_copies(0, g):
                cp.start()

    qlat = qlat_ref[...].astype(BF16)
    qpe = qpe_ref[...][:, :ROPE_DIM].astype(BF16)

    def values(p_bf16, grp):
        return _dot(p_bf16, ckv_buf[grp % DEC_SLOTS].astype(BF16))

    m = l = acc = None
    pend = None
    for g in range(n_groups):
        if g + DEC_AHEAD < n_groups:
            for cp in group_copies(b, g + DEC_AHEAD):
                cp.start()
        else:
            @pl.when(b + 1 < n_b)
            def _():
                for cp in group_copies(b + 1, g + DEC_AHEAD - n_groups):
                    cp.start()
        for cp in group_copies(b, g):
            cp.wait()

        slot = g % DEC_SLOTS
        s = (_dot_nt(qlat, ckv_buf[slot].astype(BF16))
             + _dot(qpe, kpe_buf[slot].astype(BF16))) * ATTN_SCALE
        if pend is not None:
            p_prev, m_prev = pend
            pv = values(p_prev.astype(BF16), g - 1)
            p_sum = jnp.sum(p_prev, axis=-1, keepdims=True)
            if m is None:
                l, acc = p_sum, pv
            else:
                alpha = jnp.exp(m - m_prev)
                l, acc = alpha * l + p_sum, alpha * acc + pv
            m = m_prev
        m_cur = jnp.max(s, axis=-1, keepdims=True)
        m_new = m_cur if m is None and pend is None else jnp.maximum(pend[1], m_cur)
        pend = (jnp.exp(s - m_new), m_new)

    p_prev, m_prev = pend
    pv = values(p_prev.astype(BF16), n_groups - 1)
    alpha = jnp.exp(m - m_prev)
    l = alpha * l + jnp.sum(p_prev, axis=-1, keepdims=True)
    acc = alpha * acc + pv
    m = m_prev

    r = lambda a: a.astype(BF16).astype(F32)
    kv = r(ckvs_ref[...])
    s_self = (jnp.sum(r(qlat_ref[...]) * kv, axis=-1, keepdims=True)
              + jnp.sum(r(qpe_ref[...]) * r(kpes_ref[...]), axis=-1, keepdims=True)) * ATTN_SCALE
    m_n = jnp.maximum(m, s_self)
    a = jnp.exp(m - m_n)
    p_self = jnp.exp(s_self - m_n)
    out_ref[...] = (a * acc + r(p_self) * kv) / (a * l + p_self)


def _decode_attn(page_table, qlat2, qpe2, ckv_s3, kpe_s3, cache_ckv_e, cache_kpe_t):
    dec_batch, n_pages = page_table.shape
    n_pg = DEC_PG
    qspec = pl.BlockSpec((MLA_HEADS, LANES), lambda b, pt: (b, 0))
    self_spec = pl.BlockSpec((None, 1, LANES), lambda b, pt: (b, 0, 0))
    hbm = pl.BlockSpec(memory_space=pl.ANY)
    grid_spec = pltpu.PrefetchScalarGridSpec(
        num_scalar_prefetch=1,
        grid=(dec_batch,),
        in_specs=[qspec, qspec, self_spec, self_spec, hbm, hbm],
        out_specs=qspec,
        scratch_shapes=[pltpu.VMEM((DEC_SLOTS, n_pg * PAGE_SIZE, KV_RANK), F32),
                        pltpu.VMEM((DEC_SLOTS, ROPE_DIM, n_pg * PAGE_SIZE), F32),
                        pltpu.SemaphoreType.DMA((2, DEC_SLOTS))],
    )
    return pl.pallas_call(
        functools.partial(_decode_attn_kernel, n_pg=n_pg, n_groups=n_pages // n_pg),
        grid_spec=grid_spec,
        out_shape=jax.ShapeDtypeStruct((dec_batch * MLA_HEADS, KV_RANK), F32),
        compiler_params=_cparams(("arbitrary",)),
        name="decode_attn",
    )(page_table, qlat2, qpe2, ckv_s3, kpe_s3, cache_ckv_e, cache_kpe_t)


def _s5_readout(hcat_blocks, u, cw_ref, d_ref, wglu_ref):
    y = jnp.concatenate([_dot(hcat_blocks[k], cw_ref[k]) for k in range(S5_NB)], axis=1)
    y = jax.nn.gelu(y + d_ref[...] * u)
    return y * jax.nn.sigmoid(_dot(y.astype(BF16), wglu_ref[...]))


def _s5_prompt_kernel(x_ref, win_ref, bw_ref, a_ref, cw_ref, d_ref, wglu_ref,
                      yg_ref, hre_ref, him_ref, u_s, y_s, bu_s, h_s):
    n_b, seg, _ = x_ref.shape
    half = S5_SLABS // 2
    sub = S5_SUB
    n_sub = seg // sub
    rows_sub = sub * n_b
    q_slabs = S5_WIDTH // LANES

    @pl.when(pl.program_id(0) == 0)
    def _():
        h_s[...] = jnp.zeros(h_s.shape, F32)

    for s in range(n_sub):
        rows = slice(s * rows_sub, (s + 1) * rows_sub)
        xs = jnp.concatenate([x_ref[b, s * sub:(s + 1) * sub, :] for b in range(n_b)], axis=0)
        u = _dot(xs.astype(BF16), win_ref[...])
        for b in range(n_b):
            for q in range(q_slabs):
                u_s[s, q, pl.ds(b, sub, stride=n_b), :] = u[b * sub:(b + 1) * sub, q * LANES:(q + 1) * LANES]
        for k in range(S5_NB):
            r = _dot(u_s[s, k].astype(BF16), bw_ref[k])
            for q in range(4):
                bu_s[4 * k + q, rows, :] = r[:, q * LANES:(q + 1) * LANES]
                bu_s[half + 4 * k + q, rows, :] = r[:, (4 + q) * LANES:(5 + q) * LANES]

    h = h_s[...]
    for t in range(seg):
        rows = slice(t * n_b, (t + 1) * n_b)
        hs = bu_s[:, rows, :]
        ar, ai = a_ref[:half], a_ref[half:]
        hr, hi = h[:half], h[half:]
        h = jnp.concatenate([ar * hr - ai * hi + hs[:half], ar * hi + ai * hr + hs[half:]], axis=0)
        bu_s[:, rows, :] = h
    h_s[...] = h
    hre_ref[...] = jnp.concatenate([h[s] for s in range(half)], axis=1)
    him_ref[...] = jnp.concatenate([h[half + s] for s in range(half)], axis=1)

    for s in range(n_sub):
        rows = slice(s * rows_sub, (s + 1) * rows_sub)
        blocks = [jnp.concatenate([bu_s[4 * k + q, rows, :] for q in range(4)]
                                  + [bu_s[half + 4 * k + q, rows, :] for q in range(4)], axis=1).astype(BF16)
                  for k in range(S5_NB)]
        u_t = jnp.concatenate([u_s[s, q] for q in range(q_slabs)], axis=1)
        y = _s5_readout(blocks, u_t, cw_ref, d_ref, wglu_ref)
        for q in range(q_slabs):
            y_s[s, q] = y[:, q * LANES:(q + 1) * LANES]
        for b in range(n_b):
            yg_ref[b, s * sub:(s + 1) * sub, :] = jnp.concatenate(
                [y_s[s, q, pl.ds(b, sub, stride=n_b), :] for q in range(q_slabs)], axis=1).astype(BF16)


def _s5_prompt(x3, w_in, bw, a_b, cw, d_row, w_glu):
    batch, seq, _ = x3.shape
    seg = S5_L
    slab_rows = (seg // S5_SUB, S5_WIDTH // LANES, S5_SUB * batch, LANES)
    return pl.pallas_call(
        _s5_prompt_kernel,
        grid=(seq // seg,),
        in_specs=[pl.BlockSpec((batch, seg, D_MODEL), lambda c: (0, c, 0)),
                  _const_spec(w_in.shape), _const_spec(bw.shape), _const_spec(a_b.shape),
                  _const_spec(cw.shape), _const_spec(d_row.shape), _const_spec(w_glu.shape)],
        out_specs=[pl.BlockSpec((batch, seg, S5_WIDTH), lambda c: (0, c, 0)),
                   pl.BlockSpec((batch, S5_STATES), lambda c: (0, 0)),
                   pl.BlockSpec((batch, S5_STATES), lambda c: (0, 0))],
        out_shape=[jax.ShapeDtypeStruct((batch, seq, S5_WIDTH), BF16),
                   jax.ShapeDtypeStruct((batch, S5_STATES), F32),
                   jax.ShapeDtypeStruct((batch, S5_STATES), F32)],
        scratch_shapes=[pltpu.VMEM(slab_rows, F32), pltpu.VMEM(slab_rows, F32),
                        pltpu.VMEM((S5_SLABS, seg * batch, LANES), F32),
                        pltpu.VMEM((S5_SLABS, batch, LANES), F32)],
        compiler_params=_cparams(("arbitrary",)),
        name="s5_prompt",
    )(x3, w_in, bw, a_b, cw, d_row, w_glu)


def _s5_sample_kernel(x_ref, h0r_ref, h0i_ref, win_ref, bw_ref, ar_ref, ai_ref, cw_ref, d_ref, wglu_ref,
                      yg_ref, hre_ref, him_ref):
    u = _dot(x_ref[...].astype(BF16), win_ref[...])
    ub = u.astype(BF16)
    bu = [_dot(ub[:, k * LANES:(k + 1) * LANES], bw_ref[k]) for k in range(S5_NB)]
    w = S5_KB * S5_STATE
    bu_r = jnp.concatenate([r[:, :w] for r in bu], axis=1)
    bu_i = jnp.concatenate([r[:, w:] for r in bu], axis=1)
    ar, ai, h0r, h0i = ar_ref[...], ai_ref[...], h0r_ref[...], h0i_ref[...]
    hr = bu_r + (ar * h0r - ai * h0i)
    hi = bu_i + (ar * h0i + ai * h0r)
    hre_ref[...] = hr
    him_ref[...] = hi
    blocks = [jnp.concatenate([hr[:, k * w:(k + 1) * w], hi[:, k * w:(k + 1) * w]], axis=1).astype(BF16)
              for k in range(S5_NB)]
    yg_ref[...] = _s5_readout(blocks, u, cw_ref, d_ref, wglu_ref).astype(BF16)


def _s5_sample(xs, h0r, h0i, w_in, bw, a_r, a_i, cw, d_row, w_glu):
    n = xs.shape[0]
    args = (xs, h0r, h0i, w_in, bw, a_r, a_i, cw, d_row, w_glu)
    full = lambda shape: pl.BlockSpec(shape, lambda i: (0,) * len(shape))
    return pl.pallas_call(
        _s5_sample_kernel,
        grid=(1,),
        in_specs=[full(a.shape) for a in args],
        out_specs=[full((n, S5_WIDTH)), full((n, S5_STATES)), full((n, S5_STATES))],
        out_shape=[jax.ShapeDtypeStruct((n, S5_WIDTH), BF16),
                   jax.ShapeDtypeStruct((n, S5_STATES), F32),
                   jax.ShapeDtypeStruct((n, S5_STATES), F32)],
        compiler_params=_cparams(("arbitrary",)),
        name="s5_sample",
    )(*args)


def _rope_tables(pos):
    freqs = ROPE_THETA ** (-jnp.arange(ROPE_HALF, dtype=F32) / ROPE_HALF)
    ang = pos.astype(F32)[:, None] * freqs
    cos, sin = jnp.cos(ang), jnp.sin(ang)
    zero = jnp.zeros_like(cos)
    pad = jnp.zeros((pos.shape[0], LANES - ROPE_DIM), F32)
    cos_t = jnp.concatenate([cos, cos, pad], axis=1)
    sin_a = jnp.concatenate([zero, sin, pad], axis=1)
    sin_b = jnp.concatenate([-sin, zero, pad], axis=1)
    return cos_t, sin_a, sin_b


def _block_diag(blocks):
    n, r, c = blocks.shape
    eye = jnp.eye(n, dtype=blocks.dtype)
    return jnp.einsum('nrc,nm->nrmc', blocks, eye).reshape(n * r, n * c)


def _even_params(w_in, q_norm, w_q_b, kv_norm, w_kv_b, gm_g, gm_b, gm_w_s, gm_b_s, w_out):
    i1, i2, i3, i4 = Q_RANK, Q_RANK + KV_RANK, Q_RANK + KV_RANK + ROPE_DIM, Q_RANK + KV_RANK + ROPE_DIM + GM_WIDTH
    w_in_p = jnp.concatenate([w_in[:, :i2], w_in[:, i3:], w_in[:, i2:i3],
                              jnp.zeros((D_MODEL, LANES - ROPE_DIM), F32)], axis=1).astype(BF16)
    wq_nope = w_q_b[:, :, :NOPE_DIM].reshape(Q_RANK, MLA_HEADS * NOPE_DIM).astype(BF16)
    pe = w_q_b[:, :, NOPE_DIM:]
    pe_sw = jnp.concatenate([pe[:, :, ROPE_HALF:], pe[:, :, :ROPE_HALF]], axis=2)
    lane_pad = jnp.zeros((Q_RANK, MLA_HEADS, LANES - ROPE_DIM), F32)
    wq_pe = jnp.concatenate(
        [jnp.concatenate([w, lane_pad], axis=2).reshape(Q_RANK, MLA_HEADS * LANES) for w in (pe, pe_sw)],
        axis=1).astype(BF16)
    uk = jnp.transpose(w_kv_b[:, :, :NOPE_DIM], (1, 2, 0))
    uv = jnp.transpose(w_kv_b[:, :, NOPE_DIM:], (1, 0, 2))
    pairs = range(MLA_HEADS // 2)
    w_uk = jnp.stack([_block_diag(uk[2 * pr:2 * pr + 2]) for pr in pairs]).astype(BF16)
    w_uv_p = jnp.stack([_block_diag(uv[2 * pr:2 * pr + 2]) for pr in pairs]).astype(BF16)
    w_uv = _block_diag(uv).astype(BF16)
    causal = jnp.tril(jnp.ones((GM_CHUNK, GM_CHUNK), dtype=bool))
    ws_m = jnp.where(causal[None], gm_w_s, 0).astype(BF16)
    bs_full = jnp.repeat(gm_b_s.T, GM_HEAD_DIM, axis=1)
    ws0 = jnp.repeat(gm_w_s[:, 0, 0], GM_HEAD_DIM)[None, :].astype(BF16)
    bs0 = bs_full[:1]
    n_attn = MLA_HEADS * V_DIM
    return dict(w_in_p=w_in_p, qn=q_norm[None, :], wq_nope=wq_nope, wq_pe=wq_pe, w_uk=w_uk, w_uv=w_uv, w_uv_p=w_uv_p,
                kvn=kv_norm[None, :], gm_g=gm_g[None, :], gm_b=gm_b[None, :], ws_m=ws_m, bs_full=bs_full,
                ws0=ws0, bs0=bs0, wo_attn=w_out[:n_attn].astype(BF16), wo_gate=w_out[n_attn:].astype(BF16))


def _s5_params(w_in, a_re, a_im, b_re, b_im, c_re, c_im, d, log_dt, w_glu, w_out, batch):
    dt = jnp.exp(log_dt)[:, None]
    ld_r, ld_i = a_re * dt, a_im * dt
    mag = jnp.exp(ld_r)
    ab_r, ab_i = mag * jnp.cos(ld_i), mag * jnp.sin(ld_i)
    den = a_re * a_re + a_im * a_im
    cr = ((ab_r - 1.0) * a_re + ab_i * a_im) / den
    ci = (ab_i * a_re - (ab_r - 1.0) * a_im) / den
    bb_r = cr[..., None] * b_re - ci[..., None] * b_im
    bb_i = cr[..., None] * b_im + ci[..., None] * b_re

    def in_blocks(bb):
        return jnp.stack([_block_diag(jnp.transpose(bb[k * S5_KB:(k + 1) * S5_KB], (0, 2, 1)))
                          for k in range(S5_NB)])

    def out_blocks(cc):
        return jnp.stack([_block_diag(jnp.transpose(cc[k * S5_KB:(k + 1) * S5_KB], (0, 2, 1)))
                          for k in range(S5_NB)])

    bw = jnp.concatenate([in_blocks(bb_r), in_blocks(bb_i)], axis=2).astype(BF16)
    cw = jnp.concatenate([out_blocks(c_re), out_blocks(-c_im)], axis=1).astype(BF16)
    a_r, a_i = ab_r.reshape(1, S5_STATES), ab_i.reshape(1, S5_STATES)
    half = S5_SLABS // 2
    a_b = jnp.concatenate([jnp.broadcast_to(a_r.reshape(half, 1, LANES), (half, batch, LANES)),
                           jnp.broadcast_to(a_i.reshape(half, 1, LANES), (half, batch, LANES))], axis=0)
    return dict(w_in=w_in.astype(BF16), bw=bw, cw=cw, a_r=a_r, a_i=a_i, a_b=a_b,
                d=d.reshape(1, S5_WIDTH), w_glu=w_glu.astype(BF16), w_out=w_out.astype(BF16))


def kernel(x_prompt, x_sample, cache_ckv, cache_kpe, state_s5_re, state_s5_im, page_table, w_in_even, mla_q_norm, mla_w_q_b, mla_kv_norm, mla_w_kv_b, gm_norm_g, gm_norm_b, gm_w_s, gm_b_s, w_out_even, w_in_odd, s5_a_re, s5_a_im, s5_b_re, s5_b_im, s5_c_re, s5_c_im, s5_d, s5_log_dt, s5_w_glu, w_out_odd, ln_mix_g, ln_mix_b, ln_ffn_g, ln_ffn_b, ffn_w1, ffn_w2):
    batch, seq, _ = x_prompt.shape
    dec_batch, dec_seq, _ = x_sample.shape
    assert dec_seq == 1 and seq % INPROJ_TM == 0 and seq % ATTN_TQ == 0 and seq % S5_L == 0
    assert (seq // ATTN_TQ) % 2 == 0
    assert (batch * seq) % FFN_TM == 0 and page_table.shape[1] % DEC_PG == 0
    assert batch == SUBLANES

    xp = x_prompt.reshape(batch * seq, D_MODEL)
    xs = x_sample.reshape(dec_batch, D_MODEL)
    tabs_p = _rope_tables(jnp.arange(seq, dtype=jnp.int32))
    tabs_s = _rope_tables(PAST_LEN + jnp.arange(dec_seq, dtype=jnp.int32))

    outs = {k: [] for k in ("ckv_p", "kpe_p", "ckv_s", "kpe_s", "gmv_p", "gmv_s",
                            "s5re_p", "s5im_p", "s5re_s", "s5im_s")}
    for layer in range(DEPTH):
        ln = (ln_mix_g[layer][None, :], ln_mix_b[layer][None, :],
              ln_ffn_g[layer][None, :], ln_ffn_b[layer][None, :],
              ffn_w1[layer].astype(BF16), ffn_w2[layer].astype(BF16))
        if layer % 2 == 0:
            e = layer // 2
            p = _even_params(w_in_even[e], mla_q_norm[e], mla_w_q_b[e], mla_kv_norm[e], mla_w_kv_b[e],
                             gm_norm_g[e], gm_norm_b[e], gm_w_s[e], gm_b_s[e], w_out_even[e])
            q_a, ckv, kpe, kcat, gate, gmv = _even_inproj(
                xp, p["w_in_p"], p["kvn"], p["gm_g"], p["gm_b"], p["ws_m"], p["bs_full"], *tabs_p, batch, seq)
            attn = _mla_attn(q_a, kcat, p["qn"], p["wq_nope"], p["wq_pe"], p["w_uk"], p["w_uv_p"], *tabs_p,
                             batch, seq)
            xp = _mix_ffn(xp, [attn, gate], None, [p["wo_attn"], p["wo_gate"]], *ln, name="even_ffn_prompt")
            outs["ckv_p"].append(ckv.reshape(batch, seq, KV_RANK))
            outs["kpe_p"].append(kpe.reshape(batch, seq, ROPE_DIM))
            outs["gmv_p"].append(gmv)
            ckv_s, kpe_s, gate_s, vn_s, qlat_s, qpe_s = _even_inproj_sample(
                xs, p["w_in_p"], p["kvn"], p["gm_g"], p["gm_b"], p["ws0"], p["bs0"], *tabs_s,
                p["qn"], p["wq_nope"], p["wq_pe"], p["w_uk"])
            kpe_pad = jnp.concatenate([kpe_s, jnp.zeros((dec_batch, LANES - ROPE_DIM), F32)], axis=1)
            o_lat = _decode_attn(page_table,
                                 qlat_s.reshape(dec_batch * MLA_HEADS, LANES),
                                 qpe_s.reshape(dec_batch * MLA_HEADS, LANES),
                                 ckv_s.reshape(dec_batch, 1, KV_RANK), kpe_pad.reshape(dec_batch, 1, LANES),
                                 cache_ckv[e], jnp.swapaxes(cache_kpe[e], 1, 2))
            xs = _mix_ffn(xs, [o_lat.reshape(dec_batch, MLA_HEADS * KV_RANK), gate_s], p["w_uv"],
                          [p["wo_attn"], p["wo_gate"]], *ln, name="even_ffn_sample")
            outs["ckv_s"].append(ckv_s.reshape(dec_batch, 1, KV_RANK))
            outs["kpe_s"].append(kpe_s.reshape(dec_batch, 1, ROPE_DIM))
            outs["gmv_s"].append(vn_s.reshape(dec_batch, 1, GM_WIDTH))
        else:
            o = layer // 2
            p = _s5_params(w_in_odd[o], s5_a_re[o], s5_a_im[o], s5_b_re[o], s5_b_im[o], s5_c_re[o],
                           s5_c_im[o], s5_d[o], s5_log_dt[o], s5_w_glu[o], w_out_odd[o], batch)
            yg, hre, him = _s5_prompt(xp.reshape(batch, seq, D_MODEL), p["w_in"], p["bw"], p["a_b"],
                                      p["cw"], p["d"], p["w_glu"])
            xp = _mix_ffn(xp, [yg.reshape(batch * seq, S5_WIDTH)], None, [p["w_out"]], *ln,
                          name="odd_ffn_prompt")
            outs["s5re_p"].append(hre.reshape(batch, S5_GROUPS, S5_STATE))
            outs["s5im_p"].append(him.reshape(batch, S5_GROUPS, S5_STATE))
            yg_s, hre_s, him_s = _s5_sample(
                xs, state_s5_re[o].reshape(dec_batch, S5_STATES), state_s5_im[o].reshape(dec_batch, S5_STATES),
                p["w_in"], p["bw"], p["a_r"], p["a_i"], p["cw"], p["d"], p["w_glu"])
            xs = _mix_ffn(xs, [yg_s], None, [p["w_out"]], *ln, name="odd_ffn_sample")
            outs["s5re_s"].append(hre_s.reshape(dec_batch, S5_GROUPS, S5_STATE))
            outs["s5im_s"].append(him_s.reshape(dec_batch, S5_GROUPS, S5_STATE))

    st = jnp.stack
    return (xp.reshape(batch, seq, D_MODEL), xs.reshape(dec_batch, dec_seq, D_MODEL),
            st(outs["ckv_p"]), st(outs["kpe_p"]), st(outs["ckv_s"]), st(outs["kpe_s"]),
            st(outs["gmv_p"]), st(outs["gmv_s"]),
            st(outs["s5re_p"]), st(outs["s5im_p"]), st(outs["s5re_s"]), st(outs["s5im_s"]))
```

```python
import functools
import math

import jax
import jax.numpy as jnp
from jax import lax
from jax.experimental import pallas as pl
from jax.experimental.pallas import tpu as pltpu

F32 = jnp.float32
BF16 = jnp.bfloat16

D_MODEL = 1024
DEPTH = 2
PAST_LEN = 16384
PAGE_SIZE = 128
MLA_HEADS = 8
NOPE_DIM = 64
ROPE_DIM = 32
ROPE_HALF = ROPE_DIM // 2
V_DIM = 64
Q_RANK = 256
KV_RANK = 128
ROPE_THETA = 10000.0
ATTN_SCALE = 1.0 / math.sqrt(NOPE_DIM + ROPE_DIM)
ATTN_SCALE_LOG2 = ATTN_SCALE * math.log2(math.e)
GM_HEADS = 8
GM_HEAD_DIM = 64
GM_WIDTH = GM_HEADS * GM_HEAD_DIM
GM_CHUNK = 128
S5_GROUP_DIM = 16
S5_GROUPS = 32
S5_STATE = 64
S5_WIDTH = S5_GROUPS * S5_GROUP_DIM
S5_STATES = S5_GROUPS * S5_STATE
FFN_HIDDEN = 4 * D_MODEL
DN_ALPHA = (2 * DEPTH) ** 0.25
LN_EPS = 1e-5
RMS_EPS = 1e-6

LANES = 128
SUBLANES = 8
VMEM_LIMIT_BYTES = 56 * 1024 * 1024

INPROJ_TM = 512
INPROJ_ROW_BLOCK = 256
ATTN_TQ = 256
ATTN_ROW_BLOCK = 256
FFN_TM = 1024
FFN_HC = 1024
FFN_ROW_BLOCK = 256
S5_L = 128
S5_SUB = 32
DEC_PG = 16
DEC_SLOTS = 4
DEC_AHEAD = 2
S5_KB = 8
S5_NB = S5_GROUPS // S5_KB
S5_SLABS = 2 * S5_STATES // LANES


def _cparams(sem):
    return pltpu.CompilerParams(dimension_semantics=sem, vmem_limit_bytes=VMEM_LIMIT_BYTES)


def _const_spec(shape):
    n = len(shape)
    return pl.BlockSpec(shape, lambda *_: (0,) * n, pipeline_mode=pl.Buffered(1))


def _layer_norm(x, g, b):
    mu = jnp.mean(x, axis=-1, keepdims=True)
    xc = x - mu
    var = jnp.mean(xc * xc, axis=-1, keepdims=True)
    return xc * lax.rsqrt(var + LN_EPS) * g + b


def _rms_norm(x, g):
    return x * lax.rsqrt(jnp.mean(x * x, axis=-1, keepdims=True) + RMS_EPS) * g


def _rope128(x, cos_t, sin_a, sin_b):
    return (x * cos_t + pltpu.roll(x, ROPE_HALF, 1) * sin_a
            + pltpu.roll(x, LANES - ROPE_HALF, 1) * sin_b)


def _dot(a, b):
    return jnp.dot(a, b, preferred_element_type=F32)


def _dot_nt(a, b):
    return lax.dot_general(a, b, (((1,), (1,)), ((), ())), preferred_element_type=F32)


def _even_inproj_kernel(x_ref, w_ref, kvn_ref, gmg_ref, gmb_ref, ws_ref, bs_ref,
                        cos_ref, sina_ref, sinb_ref,
                        qa_ref, ckv_ref, kpe_ref, kcat_ref, gate_ref, gmv_ref):
    tm = x_ref.shape[0]
    o_c, o_u, o_v, o_k = Q_RANK, Q_RANK + KV_RANK, Q_RANK + KV_RANK + GM_WIDTH, Q_RANK + KV_RANK + 2 * GM_WIDTH
    rb = min(INPROJ_ROW_BLOCK, tm)
    blocks = [slice(r0, r0 + rb) for r0 in range(0, tm, rb)]
    zs = [_dot(x_ref[rows, :].astype(BF16), w_ref[...]) for rows in blocks]

    gus, vns = [], []
    for rows, z in zip(blocks, zs):
        qa_ref[rows, :] = z[:, :o_c]
        c_n = _rms_norm(z[:, o_c:o_u], kvn_ref[...])
        ckv_ref[rows, :] = c_n
        kp = _rope128(z[:, o_k:o_k + LANES], cos_ref[rows, :], sina_ref[rows, :], sinb_ref[rows, :])
        kpe_ref[rows, :] = kp[:, :ROPE_DIM]
        kcat_ref[rows, :] = jnp.concatenate([c_n, kp], axis=1).astype(BF16)
        gus.append(jax.nn.gelu(z[:, o_u:o_v]))
        vns.append(_layer_norm(jax.nn.gelu(z[:, o_v:o_k]), gmg_ref[...], gmb_ref[...]))

    gmv_ref[0] = vns[-1][rb - GM_CHUNK:, :]

    lane = lax.broadcasted_iota(jnp.int32, (GM_CHUNK, LANES), 1)
    lo = lane < GM_HEAD_DIM
    bs = bs_ref[...]
    for rows, gu, v_n in zip(blocks, gus, vns):
        for ci in range(rb // GM_CHUNK):
            chunk = slice(ci * GM_CHUNK, (ci + 1) * GM_CHUNK)
            pieces = []
            for pr in range(GM_WIDTH // LANES):
                r = v_n[chunk, pr * LANES:(pr + 1) * LANES]
                m0 = _dot(ws_ref[2 * pr], jnp.where(lo, r, 0.0).astype(BF16))
                m1 = _dot(ws_ref[2 * pr + 1], jnp.where(lo, 0.0, r).astype(BF16))
                pieces.append(m0 + m1)
            mixed = jnp.concatenate(pieces, axis=1) + bs
            gate_ref[pl.ds(rows.start + ci * GM_CHUNK, GM_CHUNK), :] = (gu[chunk, :] * mixed).astype(BF16)


def _even_inproj(x2d, w_in_p, kv_norm, gm_g, gm_b, ws_m, bs_full, cos_t, sin_a, sin_b, batch, seq):
    n_tok = x2d.shape[0]
    tm = INPROJ_TM
    tiles_per_seq = seq // tm
    row = lambda w: pl.BlockSpec((tm, w), lambda i: (i, 0))
    tab = pl.BlockSpec((tm, LANES), lambda i: (i % tiles_per_seq, 0))
    n_in = w_in_p.shape[1]
    return pl.pallas_call(
        _even_inproj_kernel,
        grid=(n_tok // tm,),
        in_specs=[row(D_MODEL), _const_spec((D_MODEL, n_in)), _const_spec((1, KV_RANK)),
                  _const_spec((1, GM_WIDTH)), _const_spec((1, GM_WIDTH)),
                  _const_spec((GM_HEADS, GM_CHUNK, GM_CHUNK)), _const_spec((GM_CHUNK, GM_WIDTH)),
                  tab, tab, tab],
        out_specs=[row(Q_RANK), row(KV_RANK), row(ROPE_DIM), row(2 * LANES), row(GM_WIDTH),
                   pl.BlockSpec((1, GM_CHUNK, GM_WIDTH), lambda i: (i // tiles_per_seq, 0, 0))],
        out_shape=[jax.ShapeDtypeStruct((n_tok, Q_RANK), F32),
                   jax.ShapeDtypeStruct((n_tok, KV_RANK), F32),
                   jax.ShapeDtypeStruct((n_tok, ROPE_DIM), F32),
                   jax.ShapeDtypeStruct((n_tok, 2 * LANES), BF16),
                   jax.ShapeDtypeStruct((n_tok, GM_WIDTH), BF16),
                   jax.ShapeDtypeStruct((batch, GM_CHUNK, GM_WIDTH), F32)],
        compiler_params=_cparams(("arbitrary",)),
        name="even_inproj",
    )(x2d, w_in_p, kv_norm, gm_g, gm_b, ws_m, bs_full, cos_t, sin_a, sin_b)


def _queries(q_a, qn_g, wq_nope, wq_pe, wuk_ref, cos_t, sin_a, sin_b):
    qn = _rms_norm(q_a, qn_g).astype(BF16)
    nope = _dot(qn, wq_nope).astype(BF16)
    lat = jnp.concatenate([_dot(nope[:, pr * LANES:(pr + 1) * LANES], wuk_ref[pr])
                           for pr in range(MLA_HEADS // 2)], axis=1)
    pe = _dot(qn, wq_pe)
    n = MLA_HEADS * LANES
    sin_s = sin_a + sin_b
    pes = [pe[:, h * LANES:(h + 1) * LANES] * cos_t + pe[:, n + h * LANES:n + (h + 1) * LANES] * sin_s
           for h in range(MLA_HEADS)]
    return lat, pes


def _mla_attn_kernel(qa_lo_ref, qa_hi_ref, kcat_ref, qn_ref, wqn_ref, wqp_ref, wuk_ref, wuv_ref,
                     cos_lo_ref, sina_lo_ref, sinb_lo_ref, cos_hi_ref, sina_hi_ref, sinb_hi_ref,
                     out_hbm, qcat_s, m_s, acc_s, obuf, osem, *, nq):
    tq = qa_lo_ref.shape[0]
    n_rows = MLA_HEADS * tq
    b, i = pl.program_id(0), pl.program_id(1)
    step = b * pl.num_programs(1) + i
    n_steps = pl.num_programs(0) * pl.num_programs(1)
    q_tiles = (i, nq - 1 - i)

    def out_copies():
        return [pltpu.make_async_copy(
            obuf.at[t], out_hbm.at[pl.ds(pl.multiple_of((b * nq + q_tiles[t]) * tq, tq), tq), :], osem.at[t])
            for t in range(2)]

    qa = (qa_lo_ref, qa_hi_ref)
    tabs = ((cos_lo_ref, sina_lo_ref, sinb_lo_ref), (cos_hi_ref, sina_hi_ref, sinb_hi_ref))
    for t in range(2):
        lat, pes = _queries(qa[t][...], qn_ref[...], wqn_ref[...], wqp_ref[...], wuk_ref,
                            *(r[...] for r in tabs[t]))
        for h in range(MLA_HEADS):
            qcat_s[t, h * tq:(h + 1) * tq, :] = (ATTN_SCALE_LOG2 * jnp.concatenate(
                [lat[:, h * LANES:(h + 1) * LANES], pes[h]], axis=1)).astype(BF16)

    lanes_x = lambda a, n: jnp.concatenate([a] * n, axis=1)
    rb = ATTN_ROW_BLOCK
    ones = jnp.ones((tq, LANES), BF16)

    def keys(kv_tile):
        k = kcat_ref[pl.ds(pl.multiple_of(kv_tile * tq, tq), tq), :]
        return k, jnp.concatenate([k[:, :KV_RANK], ones], axis=1)

    q_pos = lax.broadcasted_iota(jnp.int32, (rb, tq), 0) & (tq - 1)
    k_pos = lax.broadcasted_iota(jnp.int32, (rb, tq), 1)
    causal = k_pos <= q_pos
    for t in range(2):
        k, v1 = keys(q_tiles[t])
        for r0 in range(0, n_rows, rb):
            rows = slice(r0, r0 + rb)
            s = jnp.where(causal, _dot_nt(qcat_s[t, rows, :], k), -jnp.inf)
            m0 = jnp.broadcast_to(jnp.max(s, axis=-1, keepdims=True), (rb, LANES))
            m_s[t, rows, :] = m0
            acc_s[t, rows, :] = _dot(jnp.exp2(s - lanes_x(m0, tq // LANES)).astype(BF16), v1)

    for u in range(nq - 1):
        hi = u >= i
        t = hi.astype(jnp.int32)
        k, v1 = keys(jnp.where(hi, u - i, u))
        for r0 in range(0, n_rows, rb):
            rows = slice(r0, r0 + rb)
            s = _dot_nt(qcat_s[t, rows, :], k)
            m_old = m_s[t, rows, :]
            m_new = jnp.maximum(m_old, jnp.max(s, axis=-1, keepdims=True))
            alpha = jnp.exp2(m_old - m_new)
            p = jnp.exp2(s - lanes_x(m_new, tq // LANES))
            acc_s[t, rows, :] = lanes_x(alpha, 2) * acc_s[t, rows, :] + _dot(p.astype(BF16), v1)
            m_s[t, rows, :] = m_new

    @pl.when(step > 0)
    def _():
        for cp in out_copies():
            cp.wait()

    for t in range(2):
        o_all = jnp.concatenate(
            [acc_s[t, h * tq:(h + 1) * tq, :KV_RANK] / acc_s[t, h * tq:(h + 1) * tq, KV_RANK:]
             for h in range(MLA_HEADS)], axis=1).astype(BF16)
        obuf[t] = jnp.concatenate(
            [_dot(o_all[:, pr * 2 * LANES:(pr + 1) * 2 * LANES], wuv_ref[pr]) for pr in range(MLA_HEADS // 2)],
            axis=1).astype(BF16)
    for cp in out_copies():
        cp.start()

    @pl.when(step == n_steps - 1)
    def _():
        for cp in out_copies():
            cp.wait()


def _mla_attn(q_a, kcat, qn_g, wq_nope, wq_pe, w_uk, w_uv, cos_t, sin_a, sin_b, batch, seq):
    tq = ATTN_TQ
    nq = seq // tq
    n_attn = MLA_HEADS * V_DIM
    lo = lambda b, i: i
    hi = lambda b, i: nq - 1 - i
    tab = lambda sel: pl.BlockSpec((tq, LANES), lambda b, i: (sel(b, i), 0))
    qa_spec = lambda sel: pl.BlockSpec((tq, Q_RANK), lambda b, i: (b * nq + sel(b, i), 0))
    return pl.pallas_call(
        functools.partial(_mla_attn_kernel, nq=nq),
        grid=(batch, nq // 2),
        in_specs=[qa_spec(lo), qa_spec(hi),
                  pl.BlockSpec((None, seq, 2 * LANES), lambda b, i: (b, 0, 0)),
                  _const_spec((1, Q_RANK)), _const_spec(wq_nope.shape), _const_spec(wq_pe.shape),
                  _const_spec(w_uk.shape), _const_spec(w_uv.shape),
                  tab(lo), tab(lo), tab(lo), tab(hi), tab(hi), tab(hi)],
        out_specs=pl.BlockSpec(memory_space=pl.ANY),
        out_shape=jax.ShapeDtypeStruct((batch * seq, n_attn), BF16),
        scratch_shapes=[pltpu.VMEM((2, MLA_HEADS * tq, 2 * LANES), BF16),
                        pltpu.VMEM((2, MLA_HEADS * tq, LANES), F32),
                        pltpu.VMEM((2, MLA_HEADS * tq, 2 * KV_RANK), F32),
                        pltpu.VMEM((2, tq, n_attn), BF16),
                        pltpu.SemaphoreType.DMA((2,))],
        compiler_params=_cparams(("arbitrary", "arbitrary")),
        name="mla_attn",
    )(q_a, q_a, kcat.reshape(batch, seq, 2 * LANES), qn_g, wq_nope, wq_pe, w_uk, w_uv,
      cos_t, sin_a, sin_b, cos_t, sin_a, sin_b)


def _mix_ffn_kernel(*refs, n_act, has_pre):
    x_ref = refs[0]
    act_refs = refs[1:1 + n_act]
    k = 1 + n_act
    pre_ref = refs[k] if has_pre else None
    k += int(has_pre)
    wo_refs = refs[k:k + n_act]
    k += n_act
    g1_ref, b1_ref, g2_ref, b2_ref, w1_ref, w2_ref, out_ref = refs[k:k + 7]

    tm = x_ref.shape[0]
    rb = min(FFN_ROW_BLOCK, tm)
    blocks = [slice(r0, r0 + rb) for r0 in range(0, tm, rb)]
    fs = []
    for rows in blocks:
        f = None
        for idx in range(n_act):
            a = act_refs[idx][rows, :]
            if has_pre and idx == 0:
                a = _dot(a.astype(BF16), pre_ref[...])
            part = _dot(a.astype(BF16), wo_refs[idx][...])
            f = part if f is None else f + part
        fs.append(f)
    x1s = [_layer_norm(DN_ALPHA * x_ref[rows, :] + f, g1_ref[...], b1_ref[...]) for rows, f in zip(blocks, fs)]
    x1bs = [x1.astype(BF16) for x1 in x1s]
    accs = [None] * len(blocks)
    for c in range(FFN_HIDDEN // FFN_HC):
        cols = slice(c * FFN_HC, (c + 1) * FFN_HC)
        for i, x1b in enumerate(x1bs):
            h = jnp.maximum(_dot(x1b, w1_ref[:, cols]), 0.0)
            part = _dot((h * h).astype(BF16), w2_ref[cols, :])
            accs[i] = part if accs[i] is None else accs[i] + part
    for rows, x1, acc in zip(blocks, x1s, accs):
        out_ref[rows, :] = _layer_norm(DN_ALPHA * x1 + acc, g2_ref[...], b2_ref[...])


def _mix_ffn(x2d, acts, pre_w, wos, ln1_g, ln1_b, ln2_g, ln2_b, w1, w2, name):
    n_tok = x2d.shape[0]
    tm = min(FFN_TM, n_tok)
    row = lambda w: pl.BlockSpec((tm, w), lambda i: (i, 0))
    has_pre = pre_w is not None
    in_specs = [row(D_MODEL)] + [row(a.shape[1]) for a in acts]
    args = [x2d] + list(acts)
    if has_pre:
        in_specs.append(_const_spec(pre_w.shape))
        args.append(pre_w)
    in_specs += [_const_spec(w.shape) for w in wos]
    args += list(wos)
    in_specs += [_const_spec((1, D_MODEL))] * 4 + [_const_spec(w1.shape), _const_spec(w2.shape)]
    args += [ln1_g, ln1_b, ln2_g, ln2_b, w1, w2]
    return pl.pallas_call(
        functools.partial(_mix_ffn_kernel, n_act=len(acts), has_pre=has_pre),
        grid=(n_tok // tm,),
        in_specs=in_specs,
        out_specs=row(D_MODEL),
        out_shape=jax.ShapeDtypeStruct((n_tok, D_MODEL), F32),
        compiler_params=_cparams(("arbitrary",)),
        name=name,
    )(*args)


def _even_inproj_sample_kernel(x_ref, w_ref, kvn_ref, gmg_ref, gmb_ref, ws0_ref, bs0_ref,
                               cos_ref, sina_ref, sinb_ref, qn_ref, wqn_ref, wqp_ref, wuk_ref,
                               ckv_ref, kpe_ref, gate_ref, vn_ref, qlat_ref, qpe_ref):
    z = _dot(x_ref[...].astype(BF16), w_ref[...])
    o_c, o_u, o_v, o_k = Q_RANK, Q_RANK + KV_RANK, Q_RANK + KV_RANK + GM_WIDTH, Q_RANK + KV_RANK + 2 * GM_WIDTH
    cos_t, sin_a, sin_b = cos_ref[...], sina_ref[...], sinb_ref[...]
    c_n = _rms_norm(z[:, o_c:o_u], kvn_ref[...])
    ckv_ref[...] = c_n
    kp = _rope128(z[:, o_k:o_k + LANES], cos_t, sin_a, sin_b)
    kpe_ref[...] = kp[:, :ROPE_DIM]
    gu = jax.nn.gelu(z[:, o_u:o_v])
    v_n = _layer_norm(jax.nn.gelu(z[:, o_v:o_k]), gmg_ref[...], gmb_ref[...])
    vn_ref[...] = v_n
    mixed = ws0_ref[...].astype(F32) * v_n.astype(BF16).astype(F32) + bs0_ref[...]
    gate_ref[...] = (gu * mixed).astype(BF16)
    lat, pes = _queries(z[:, :o_c], qn_ref[...], wqn_ref[...], wqp_ref[...], wuk_ref,
                        cos_t, sin_a, sin_b)
    qlat_ref[...] = lat
    qpe_ref[...] = jnp.concatenate(pes, axis=1)


def _even_inproj_sample(xs, w_in_p, kv_norm, gm_g, gm_b, ws0, bs0, cos_t, sin_a, sin_b,
                        qn_g, wq_nope, wq_pe, w_uk):
    n = xs.shape[0]
    args = (xs, w_in_p, kv_norm, gm_g, gm_b, ws0, bs0, cos_t, sin_a, sin_b, qn_g, wq_nope, wq_pe, w_uk)
    full = lambda shape: pl.BlockSpec(shape, lambda i: (0,) * len(shape))
    widths = [(KV_RANK, F32), (ROPE_DIM, F32), (GM_WIDTH, BF16), (GM_WIDTH, F32),
              (MLA_HEADS * LANES, F32), (MLA_HEADS * LANES, F32)]
    return pl.pallas_call(
        _even_inproj_sample_kernel,
        grid=(1,),
        in_specs=[full(a.shape) for a in args],
        out_specs=[full((n, w)) for w, _ in widths],
        out_shape=[jax.ShapeDtypeStruct((n, w), dt) for w, dt in widths],
        compiler_params=_cparams(("arbitrary",)),
        name="even_inproj_sample",
    )(*args)


def _decode_attn_kernel(pt_ref, qlat_ref, qpe_ref, ckvs_ref, kpes_ref, ckv_hbm, kpe_hbm, out_ref,
                        ckv_buf, kpe_buf, sem, *, n_pg, n_groups):
    b = pl.program_id(0)
    n_b = pl.num_programs(0)
    assert n_groups % DEC_SLOTS == 0 and DEC_AHEAD + 2 <= DEC_SLOTS

    def group_copies(seq, grp):
        slot = grp % DEC_SLOTS
        cps = []
        for k in range(n_pg):
            page = pt_ref[seq, grp * n_pg + k]
            keys = pl.ds(k * PAGE_SIZE, PAGE_SIZE)
            cps.append(pltpu.make_async_copy(ckv_hbm.at[page], ckv_buf.at[slot, keys, :], sem.at[0, slot]))
            cps.append(pltpu.make_async_copy(kpe_hbm.at[page], kpe_buf.at[slot, :, keys], sem.at[1, slot]))
        return cps

    @pl.when(b == 0)
    def _():
        for g in range(DEC_AHEAD):
            for cp in group_copies(0, g):
                cp.start()

    qlat = qlat_ref[...].astype(BF16)
    qpe = qpe_ref[...][:, :ROPE_DIM].astype(BF16)

    def values(p_bf16, grp):
        return _dot(p_bf16, ckv_buf[grp % DEC_SLOTS].astype(BF16))

    m = l = acc = None
    pend = None
    for g in range(n_groups):
        if g + DEC_AHEAD < n_groups:
            for cp in group_copies(b, g + DEC_AHEAD):
                cp.start()
        else:
            @pl.when(b + 1 < n_b)
            def _():
                for cp in group_copies(b + 1, g + DEC_AHEAD - n_groups):
                    cp.start()
        for cp in group_copies(b, g):
            cp.wait()

        slot = g % DEC_SLOTS
        s = (_dot_nt(qlat, ckv_buf[slot].astype(BF16))
             + _dot(qpe, kpe_buf[slot].astype(BF16))) * ATTN_SCALE
        if pend is not None:
            p_prev, m_prev = pend
            pv = values(p_prev.astype(BF16), g - 1)
            p_sum = jnp.sum(p_prev, axis=-1, keepdims=True)
            if m is None:
                l, acc = p_sum, pv
            else:
                alpha = jnp.exp(m - m_prev)
                l, acc = alpha * l + p_sum, alpha * acc + pv
            m = m_prev
        m_cur = jnp.max(s, axis=-1, keepdims=True)
        m_new = m_cur if m is None and pend is None else jnp.maximum(pend[1], m_cur)
        pend = (jnp.exp(s - m_new), m_new)

    p_prev, m_prev = pend
    pv = values(p_prev.astype(BF16), n_groups - 1)
    alpha = jnp.exp(m - m_prev)
    l = alpha * l + jnp.sum(p_prev, axis=-1, keepdims=True)
    acc = alpha * acc + pv
    m = m_prev

    r = lambda a: a.astype(BF16).astype(F32)
    kv = r(ckvs_ref[...])
    s_self = (jnp.sum(r(qlat_ref[...]) * kv, axis=-1, keepdims=True)
              + jnp.sum(r(qpe_ref[...]) * r(kpes_ref[...]), axis=-1, keepdims=True)) * ATTN_SCALE
    m_n = jnp.maximum(m, s_self)
    a = jnp.exp(m - m_n)
    p_self = jnp.exp(s_self - m_n)
    out_ref[...] = (a * acc + r(p_self) * kv) / (a * l + p_self)


def _decode_attn(page_table, qlat2, qpe2, ckv_s3, kpe_s3, cache_ckv_e, cache_kpe_t):
    dec_batch, n_pages = page_table.shape
    n_pg = DEC_PG
    qspec = pl.BlockSpec((MLA_HEADS, LANES), lambda b, pt: (b, 0))
    self_spec = pl.BlockSpec((None, 1, LANES), lambda b, pt: (b, 0, 0))
    hbm = pl.BlockSpec(memory_space=pl.ANY)
    grid_spec = pltpu.PrefetchScalarGridSpec(
        num_scalar_prefetch=1,
        grid=(dec_batch,),
        in_specs=[qspec, qspec, self_spec, self_spec, hbm, hbm],
        out_specs=qspec,
        scratch_shapes=[pltpu.VMEM((DEC_SLOTS, n_pg * PAGE_SIZE, KV_RANK), F32),
                        pltpu.VMEM((DEC_SLOTS, ROPE_DIM, n_pg * PAGE_SIZE), F32),
                        pltpu.SemaphoreType.DMA((2, DEC_SLOTS))],
    )
    return pl.pallas_call(
        functools.partial(_decode_attn_kernel, n_pg=n_pg, n_groups=n_pages // n_pg),
        grid_spec=grid_spec,
        out_shape=jax.ShapeDtypeStruct((dec_batch * MLA_HEADS, KV_RANK), F32),
        compiler_params=_cparams(("arbitrary",)),
        name="decode_attn",
    )(page_table, qlat2, qpe2, ckv_s3, kpe_s3, cache_ckv_e, cache_kpe_t)


def _s5_readout(hcat_blocks, u, cw_ref, d_ref, wglu_ref):
    y = jnp.concatenate([_dot(hcat_blocks[k], cw_ref[k]) for k in range(S5_NB)], axis=1)
    y = jax.nn.gelu(y + d_ref[...] * u)
    return y * jax.nn.sigmoid(_dot(y.astype(BF16), wglu_ref[...]))


def _s5_prompt_kernel(x_ref, win_ref, bw_ref, a_ref, cw_ref, d_ref, wglu_ref,
                      yg_ref, hre_ref, him_ref, u_s, y_s, bu_s, h_s):
    n_b, seg, _ = x_ref.shape
    half = S5_SLABS // 2
    sub = S5_SUB
    n_sub = seg // sub
    rows_sub = sub * n_b
    q_slabs = S5_WIDTH // LANES

    @pl.when(pl.program_id(0) == 0)
    def _():
        h_s[...] = jnp.zeros(h_s.shape, F32)

    def project(s):
        rows = slice(s * rows_sub, (s + 1) * rows_sub)
        xs = jnp.concatenate([x_ref[b, s * sub:(s + 1) * sub, :] for b in range(n_b)], axis=0)
        u = _dot(xs.astype(BF16), win_ref[...])
        for b in range(n_b):
            for q in range(q_slabs):
                u_s[s, q, pl.ds(b, sub, stride=n_b), :] = u[b * sub:(b + 1) * sub, q * LANES:(q + 1) * LANES]
        for k in range(S5_NB):
            r = _dot(u_s[s, k].astype(BF16), bw_ref[k])
            for q in range(4):
                bu_s[4 * k + q, rows, :] = r[:, q * LANES:(q + 1) * LANES]
                bu_s[half + 4 * k + q, rows, :] = r[:, (4 + q) * LANES:(5 + q) * LANES]

    def recur(s, h):
        for t in range(s * sub, (s + 1) * sub):
            rows = slice(t * n_b, (t + 1) * n_b)
            hs = bu_s[:, rows, :]
            ar, ai = a_ref[:half], a_ref[half:]
            hr, hi = h[:half], h[half:]
            h = jnp.concatenate([ar * hr - ai * hi + hs[:half], ar * hi + ai * hr + hs[half:]], axis=0)
            bu_s[:, rows, :] = h
        return h

    def read_out(s):
        rows = slice(s * rows_sub, (s + 1) * rows_sub)
        blocks = [jnp.concatenate([bu_s[4 * k + q, rows, :] for q in range(4)]
                                  + [bu_s[half + 4 * k + q, rows, :] for q in range(4)], axis=1).astype(BF16)
                  for k in range(S5_NB)]
        u_t = jnp.concatenate([u_s[s, q] for q in range(q_slabs)], axis=1)
        y = _s5_readout(blocks, u_t, cw_ref, d_ref, wglu_ref)
        for q in range(q_slabs):
            y_s[s, q] = y[:, q * LANES:(q + 1) * LANES]
        for b in range(n_b):
            yg_ref[b, s * sub:(s + 1) * sub, :] = jnp.concatenate(
                [y_s[s, q, pl.ds(b, sub, stride=n_b), :] for q in range(q_slabs)], axis=1).astype(BF16)

    h = h_s[...]
    project(0)
    for s in range(n_sub):
        if s + 1 < n_sub:
            project(s + 1)
        h = recur(s, h)
        if s >= 1:
            read_out(s - 1)
    read_out(n_sub - 1)
    h_s[...] = h
    hre_ref[...] = jnp.concatenate([h[s] for s in range(half)], axis=1)
    him_ref[...] = jnp.concatenate([h[half + s] for s in range(half)], axis=1)


def _s5_prompt(x3, w_in, bw, a_b, cw, d_row, w_glu):
    batch, seq, _ = x3.shape
    seg = S5_L
    slab_rows = (seg // S5_SUB, S5_WIDTH // LANES, S5_SUB * batch, LANES)
    return pl.pallas_call(
        _s5_prompt_kernel,
        grid=(seq // seg,),
        in_specs=[pl.BlockSpec((batch, seg, D_MODEL), lambda c: (0, c, 0)),
                  _const_spec(w_in.shape), _const_spec(bw.shape), _const_spec(a_b.shape),
                  _const_spec(cw.shape), _const_spec(d_row.shape), _const_spec(w_glu.shape)],
        out_specs=[pl.BlockSpec((batch, seg, S5_WIDTH), lambda c: (0, c, 0)),
                   pl.BlockSpec((batch, S5_STATES), lambda c: (0, 0)),
                   pl.BlockSpec((batch, S5_STATES), lambda c: (0, 0))],
        out_shape=[jax.ShapeDtypeStruct((batch, seq, S5_WIDTH), BF16),
                   jax.ShapeDtypeStruct((batch, S5_STATES), F32),
                   jax.ShapeDtypeStruct((batch, S5_STATES), F32)],
        scratch_shapes=[pltpu.VMEM(slab_rows, F32), pltpu.VMEM(slab_rows, F32),
                        pltpu.VMEM((S5_SLABS, seg * batch, LANES), F32),
                        pltpu.VMEM((S5_SLABS, batch, LANES), F32)],
        compiler_params=_cparams(("arbitrary",)),
        name="s5_prompt",
    )(x3, w_in, bw, a_b, cw, d_row, w_glu)


def _s5_sample_kernel(x_ref, h0r_ref, h0i_ref, win_ref, bw_ref, ar_ref, ai_ref, cw_ref, d_ref, wglu_ref,
                      yg_ref, hre_ref, him_ref):
    u = _dot(x_ref[...].astype(BF16), win_ref[...])
    ub = u.astype(BF16)
    bu = [_dot(ub[:, k * LANES:(k + 1) * LANES], bw_ref[k]) for k in range(S5_NB)]
    w = S5_KB * S5_STATE
    bu_r = jnp.concatenate([r[:, :w] for r in bu], axis=1)
    bu_i = jnp.concatenate([r[:, w:] for r in bu], axis=1)
    ar, ai, h0r, h0i = ar_ref[...], ai_ref[...], h0r_ref[...], h0i_ref[...]
    hr = bu_r + (ar * h0r - ai * h0i)
    hi = bu_i + (ar * h0i + ai * h0r)
    hre_ref[...] = hr
    him_ref[...] = hi
    blocks = [jnp.concatenate([hr[:, k * w:(k + 1) * w], hi[:, k * w:(k + 1) * w]], axis=1).astype(BF16)
              for k in range(S5_NB)]
    yg_ref[...] = _s5_readout(blocks, u, cw_ref, d_ref, wglu_ref).astype(BF16)


def _s5_sample(xs, h0r, h0i, w_in, bw, a_r, a_i, cw, d_row, w_glu):
    n = xs.shape[0]
    args = (xs, h0r, h0i, w_in, bw, a_r, a_i, cw, d_row, w_glu)
    full = lambda shape: pl.BlockSpec(shape, lambda i: (0,) * len(shape))
    return pl.pallas_call(
        _s5_sample_kernel,
        grid=(1,),
        in_specs=[full(a.shape) for a in args],
        out_specs=[full((n, S5_WIDTH)), full((n, S5_STATES)), full((n, S5_STATES))],
        out_shape=[jax.ShapeDtypeStruct((n, S5_WIDTH), BF16),
                   jax.ShapeDtypeStruct((n, S5_STATES), F32),
                   jax.ShapeDtypeStruct((n, S5_STATES), F32)],
        compiler_params=_cparams(("arbitrary",)),
        name="s5_sample",
    )(*args)


def _rope_tables(pos):
    freqs = ROPE_THETA ** (-jnp.arange(ROPE_HALF, dtype=F32) / ROPE_HALF)
    ang = pos.astype(F32)[:, None] * freqs
    cos, sin = jnp.cos(ang), jnp.sin(ang)
    zero = jnp.zeros_like(cos)
    pad = jnp.zeros((pos.shape[0], LANES - ROPE_DIM), F32)
    cos_t = jnp.concatenate([cos, cos, pad], axis=1)
    sin_a = jnp.concatenate([zero, sin, pad], axis=1)
    sin_b = jnp.concatenate([-sin, zero, pad], axis=1)
    return cos_t, sin_a, sin_b


def _block_diag(blocks):
    n, r, c = blocks.shape
    eye = jnp.eye(n, dtype=blocks.dtype)
    return jnp.einsum('nrc,nm->nrmc', blocks, eye).reshape(n * r, n * c)


def _even_params(w_in, q_norm, w_q_b, kv_norm, w_kv_b, gm_g, gm_b, gm_w_s, gm_b_s, w_out):
    i1, i2, i3, i4 = Q_RANK, Q_RANK + KV_RANK, Q_RANK + KV_RANK + ROPE_DIM, Q_RANK + KV_RANK + ROPE_DIM + GM_WIDTH
    w_in_p = jnp.concatenate([w_in[:, :i2], w_in[:, i3:], w_in[:, i2:i3],
                              jnp.zeros((D_MODEL, LANES - ROPE_DIM), F32)], axis=1).astype(BF16)
    wq_nope = w_q_b[:, :, :NOPE_DIM].reshape(Q_RANK, MLA_HEADS * NOPE_DIM).astype(BF16)
    pe = w_q_b[:, :, NOPE_DIM:]
    pe_sw = jnp.concatenate([pe[:, :, ROPE_HALF:], pe[:, :, :ROPE_HALF]], axis=2)
    lane_pad = jnp.zeros((Q_RANK, MLA_HEADS, LANES - ROPE_DIM), F32)
    wq_pe = jnp.concatenate(
        [jnp.concatenate([w, lane_pad], axis=2).reshape(Q_RANK, MLA_HEADS * LANES) for w in (pe, pe_sw)],
        axis=1).astype(BF16)
    uk = jnp.transpose(w_kv_b[:, :, :NOPE_DIM], (1, 2, 0))
    uv = jnp.transpose(w_kv_b[:, :, NOPE_DIM:], (1, 0, 2))
    pairs = range(MLA_HEADS // 2)
    w_uk = jnp.stack([_block_diag(uk[2 * pr:2 * pr + 2]) for pr in pairs]).astype(BF16)
    w_uv_p = jnp.stack([_block_diag(uv[2 * pr:2 * pr + 2]) for pr in pairs]).astype(BF16)
    w_uv = _block_diag(uv).astype(BF16)
    causal = jnp.tril(jnp.ones((GM_CHUNK, GM_CHUNK), dtype=bool))
    ws_m = jnp.where(causal[None], gm_w_s, 0).astype(BF16)
    bs_full = jnp.repeat(gm_b_s.T, GM_HEAD_DIM, axis=1)
    ws0 = jnp.repeat(gm_w_s[:, 0, 0], GM_HEAD_DIM)[None, :].astype(BF16)
    bs0 = bs_full[:1]
    n_attn = MLA_HEADS * V_DIM
    return dict(w_in_p=w_in_p, qn=q_norm[None, :], wq_nope=wq_nope, wq_pe=wq_pe, w_uk=w_uk, w_uv=w_uv, w_uv_p=w_uv_p,
                kvn=kv_norm[None, :], gm_g=gm_g[None, :], gm_b=gm_b[None, :], ws_m=ws_m, bs_full=bs_full,
                ws0=ws0, bs0=bs0, wo_attn=w_out[:n_attn].astype(BF16), wo_gate=w_out[n_attn:].astype(BF16))


def _s5_params(w_in, a_re, a_im, b_re, b_im, c_re, c_im, d, log_dt, w_glu, w_out, batch):
    dt = jnp.exp(log_dt)[:, None]
    ld_r, ld_i = a_re * dt, a_im * dt
    mag = jnp.exp(ld_r)
    ab_r, ab_i = mag * jnp.cos(ld_i), mag * jnp.sin(ld_i)
    den = a_re * a_re + a_im * a_im
    cr = ((ab_r - 1.0) * a_re + ab_i * a_im) / den
    ci = (ab_i * a_re - (ab_r - 1.0) * a_im) / den
    bb_r = cr[..., None] * b_re - ci[..., None] * b_im
    bb_i = cr[..., None] * b_im + ci[..., None] * b_re

    def in_blocks(bb):
        return jnp.stack([_block_diag(jnp.transpose(bb[k * S5_KB:(k + 1) * S5_KB], (0, 2, 1)))
                          for k in range(S5_NB)])

    def out_blocks(cc):
        return jnp.stack([_block_diag(jnp.transpose(cc[k * S5_KB:(k + 1) * S5_KB], (0, 2, 1)))
                          for k in range(S5_NB)])

    bw = jnp.concatenate([in_blocks(bb_r), in_blocks(bb_i)], axis=2).astype(BF16)
    cw = jnp.concatenate([out_blocks(c_re), out_blocks(-c_im)], axis=1).astype(BF16)
    a_r, a_i = ab_r.reshape(1, S5_STATES), ab_i.reshape(1, S5_STATES)
    half = S5_SLABS // 2
    a_b = jnp.concatenate([jnp.broadcast_to(a_r.reshape(half, 1, LANES), (half, batch, LANES)),
                           jnp.broadcast_to(a_i.reshape(half, 1, LANES), (half, batch, LANES))], axis=0)
    return dict(w_in=w_in.astype(BF16), bw=bw, cw=cw, a_r=a_r, a_i=a_i, a_b=a_b,
                d=d.reshape(1, S5_WIDTH), w_glu=w_glu.astype(BF16), w_out=w_out.astype(BF16))


def kernel(x_prompt, x_sample, cache_ckv, cache_kpe, state_s5_re, state_s5_im, page_table, w_in_even, mla_q_norm, mla_w_q_b, mla_kv_norm, mla_w_kv_b, gm_norm_g, gm_norm_b, gm_w_s, gm_b_s, w_out_even, w_in_odd, s5_a_re, s5_a_im, s5_b_re, s5_b_im, s5_c_re, s5_c_im, s5_d, s5_log_dt, s5_w_glu, w_out_odd, ln_mix_g, ln_mix_b, ln_ffn_g, ln_ffn_b, ffn_w1, ffn_w2):
    batch, seq, _ = x_prompt.shape
    dec_batch, dec_seq, _ = x_sample.shape
    assert dec_seq == 1 and seq % INPROJ_TM == 0 and seq % ATTN_TQ == 0 and seq % S5_L == 0
    assert (seq // ATTN_TQ) % 2 == 0
    assert (batch * seq) % FFN_TM == 0 and page_table.shape[1] % DEC_PG == 0
    assert batch == SUBLANES

    xp = x_prompt.reshape(batch * seq, D_MODEL)
    xs = x_sample.reshape(dec_batch, D_MODEL)
    tabs_p = _rope_tables(jnp.arange(seq, dtype=jnp.int32))
    tabs_s = _rope_tables(PAST_LEN + jnp.arange(dec_seq, dtype=jnp.int32))

    outs = {k: [] for k in ("ckv_p", "kpe_p", "ckv_s", "kpe_s", "gmv_p", "gmv_s",
                            "s5re_p", "s5im_p", "s5re_s", "s5im_s")}
    for layer in range(DEPTH):
        ln = (ln_mix_g[layer][None, :], ln_mix_b[layer][None, :],
              ln_ffn_g[layer][None, :], ln_ffn_b[layer][None, :],
              ffn_w1[layer].astype(BF16), ffn_w2[layer].astype(BF16))
        if layer % 2 == 0:
            e = layer // 2
            p = _even_params(w_in_even[e], mla_q_norm[e], mla_w_q_b[e], mla_kv_norm[e], mla_w_kv_b[e],
                             gm_norm_g[e], gm_norm_b[e], gm_w_s[e], gm_b_s[e], w_out_even[e])
            q_a, ckv, kpe, kcat, gate, gmv = _even_inproj(
                xp, p["w_in_p"], p["kvn"], p["gm_g"], p["gm_b"], p["ws_m"], p["bs_full"], *tabs_p, batch, seq)
            attn = _mla_attn(q_a, kcat, p["qn"], p["wq_nope"], p["wq_pe"], p["w_uk"], p["w_uv_p"], *tabs_p,
                             batch, seq)
            xp = _mix_ffn(xp, [attn, gate], None, [p["wo_attn"], p["wo_gate"]], *ln, name="even_ffn_prompt")
            outs["ckv_p"].append(ckv.reshape(batch, seq, KV_RANK))
            outs["kpe_p"].append(kpe.reshape(batch, seq, ROPE_DIM))
            outs["gmv_p"].append(gmv)
            ckv_s, kpe_s, gate_s, vn_s, qlat_s, qpe_s = _even_inproj_sample(
                xs, p["w_in_p"], p["kvn"], p["gm_g"], p["gm_b"], p["ws0"], p["bs0"], *tabs_s,
                p["qn"], p["wq_nope"], p["wq_pe"], p["w_uk"])
            kpe_pad = jnp.concatenate([kpe_s, jnp.zeros((dec_batch, LANES - ROPE_DIM), F32)], axis=1)
            o_lat = _decode_attn(page_table,
                                 qlat_s.reshape(dec_batch * MLA_HEADS, LANES),
                                 qpe_s.reshape(dec_batch * MLA_HEADS, LANES),
                                 ckv_s.reshape(dec_batch, 1, KV_RANK), kpe_pad.reshape(dec_batch, 1, LANES),
                                 cache_ckv[e], jnp.swapaxes(cache_kpe[e], 1, 2))
            xs = _mix_ffn(xs, [o_lat.reshape(dec_batch, MLA_HEADS * KV_RANK), gate_s], p["w_uv"],
                          [p["wo_attn"], p["wo_gate"]], *ln, name="even_ffn_sample")
            outs["ckv_s"].append(ckv_s.reshape(dec_batch, 1, KV_RANK))
            outs["kpe_s"].append(kpe_s.reshape(dec_batch, 1, ROPE_DIM))
            outs["gmv_s"].append(vn_s.reshape(dec_batch, 1, GM_WIDTH))
        else:
            o = layer // 2
            p = _s5_params(w_in_odd[o], s5_a_re[o], s5_a_im[o], s5_b_re[o], s5_b_im[o], s5_c_re[o],
                           s5_c_im[o], s5_d[o], s5_log_dt[o], s5_w_glu[o], w_out_odd[o], batch)
            yg, hre, him = _s5_prompt(xp.reshape(batch, seq, D_MODEL), p["w_in"], p["bw"], p["a_b"],
                                      p["cw"], p["d"], p["w_glu"])
            xp = _mix_ffn(xp, [yg.reshape(batch * seq, S5_WIDTH)], None, [p["w_out"]], *ln,
                          name="odd_ffn_prompt")
            outs["s5re_p"].append(hre.reshape(batch, S5_GROUPS, S5_STATE))
            outs["s5im_p"].append(him.reshape(batch, S5_GROUPS, S5_STATE))
            yg_s, hre_s, him_s = _s5_sample(
                xs, state_s5_re[o].reshape(dec_batch, S5_STATES), state_s5_im[o].reshape(dec_batch, S5_STATES),
                p["w_in"], p["bw"], p["a_r"], p["a_i"], p["cw"], p["d"], p["w_glu"])
            xs = _mix_ffn(xs, [yg_s], None, [p["w_out"]], *ln, name="odd_ffn_sample")
            outs["s5re_s"].append(hre_s.reshape(dec_batch, S5_GROUPS, S5_STATE))
            outs["s5im_s"].append(him_s.reshape(dec_batch, S5_GROUPS, S5_STATE))

    st = jnp.stack
    return (xp.reshape(batch, seq, D_MODEL), xs.reshape(dec_batch, dec_seq, D_MODEL),
            st(outs["ckv_p"]), st(outs["kpe_p"]), st(outs["ckv_s"]), st(outs["kpe_s"]),
            st(outs["gmv_p"]), st(outs["gmv_s"]),
            st(outs["s5re_p"]), st(outs["s5im_p"]), st(outs["s5re_s"]), st(outs["s5im_s"]))
```

```python
import functools
import math

import jax
import jax.numpy as jnp
from jax import lax
from jax.experimental import pallas as pl
from jax.experimental.pallas import tpu as pltpu

F32 = jnp.float32
BF16 = jnp.bfloat16

D_MODEL = 1024
DEPTH = 2
PAST_LEN = 16384
PAGE_SIZE = 128
MLA_HEADS = 8
NOPE_DIM = 64
ROPE_DIM = 32
ROPE_HALF = ROPE_DIM // 2
V_DIM = 64
Q_RANK = 256
KV_RANK = 128
ROPE_THETA = 10000.0
ATTN_SCALE = 1.0 / math.sqrt(NOPE_DIM + ROPE_DIM)
ATTN_SCALE_LOG2 = ATTN_SCALE * math.log2(math.e)
GM_HEADS = 8
GM_HEAD_DIM = 64
GM_WIDTH = GM_HEADS * GM_HEAD_DIM
GM_CHUNK = 128
S5_GROUP_DIM = 16
S5_GROUPS = 32
S5_STATE = 64
S5_WIDTH = S5_GROUPS * S5_GROUP_DIM
S5_STATES = S5_GROUPS * S5_STATE
FFN_HIDDEN = 4 * D_MODEL
DN_ALPHA = (2 * DEPTH) ** 0.25
LN_EPS = 1e-5
RMS_EPS = 1e-6

LANES = 128
SUBLANES = 8
VMEM_LIMIT_BYTES = 56 * 1024 * 1024

INPROJ_TM = 512
INPROJ_ROW_BLOCK = 256
ATTN_TQ = 256
ATTN_ROW_BLOCK = 256
FFN_TM = 1024
FFN_HC = 1024
FFN_ROW_BLOCK = 256
S5_L = 128
S5_SUB = 32
DEC_PG = 32
DEC_SLOTS = 4
DEC_AHEAD = 2
S5_KB = 8
S5_NB = S5_GROUPS // S5_KB
S5_SLABS = 2 * S5_STATES // LANES


def _cparams(sem):
    return pltpu.CompilerParams(dimension_semantics=sem, vmem_limit_bytes=VMEM_LIMIT_BYTES)


def _const_spec(shape):
    n = len(shape)
    return pl.BlockSpec(shape, lambda *_: (0,) * n, pipeline_mode=pl.Buffered(1))


def _layer_norm(x, g, b):
    mu = jnp.mean(x, axis=-1, keepdims=True)
    xc = x - mu
    var = jnp.mean(xc * xc, axis=-1, keepdims=True)
    return xc * lax.rsqrt(var + LN_EPS) * g + b


def _rms_norm(x, g):
    return x * lax.rsqrt(jnp.mean(x * x, axis=-1, keepdims=True) + RMS_EPS) * g


def _rope128(x, cos_t, sin_a, sin_b):
    return (x * cos_t + pltpu.roll(x, ROPE_HALF, 1) * sin_a
            + pltpu.roll(x, LANES - ROPE_HALF, 1) * sin_b)


def _dot(a, b):
    return jnp.dot(a, b, preferred_element_type=F32)


def _dot_nt(a, b):
    return lax.dot_general(a, b, (((1,), (1,)), ((), ())), preferred_element_type=F32)


def _even_inproj_kernel(x_ref, w_ref, kvn_ref, gmg_ref, gmb_ref, ws_ref, bs_ref,
                        cos_ref, sina_ref, sinb_ref,
                        qa_ref, ckv_ref, kpe_ref, kcat_ref, gate_ref, gmv_ref):
    tm = x_ref.shape[0]
    o_c, o_u, o_v, o_k = Q_RANK, Q_RANK + KV_RANK, Q_RANK + KV_RANK + GM_WIDTH, Q_RANK + KV_RANK + 2 * GM_WIDTH
    rb = min(INPROJ_ROW_BLOCK, tm)
    blocks = [slice(r0, r0 + rb) for r0 in range(0, tm, rb)]
    zs = [_dot(x_ref[rows, :].astype(BF16), w_ref[...]) for rows in blocks]

    gus, vns = [], []
    for rows, z in zip(blocks, zs):
        qa_ref[rows, :] = z[:, :o_c]
        c_n = _rms_norm(z[:, o_c:o_u], kvn_ref[...])
        ckv_ref[rows, :] = c_n
        kp = _rope128(z[:, o_k:o_k + LANES], cos_ref[rows, :], sina_ref[rows, :], sinb_ref[rows, :])
        kpe_ref[rows, :] = kp[:, :ROPE_DIM]
        kcat_ref[rows, :] = jnp.concatenate([c_n, kp], axis=1).astype(BF16)
        gus.append(jax.nn.gelu(z[:, o_u:o_v]))
        vns.append(_layer_norm(jax.nn.gelu(z[:, o_v:o_k]), gmg_ref[...], gmb_ref[...]))

    gmv_ref[0] = vns[-1][rb - GM_CHUNK:, :]

    lane = lax.broadcasted_iota(jnp.int32, (GM_CHUNK, LANES), 1)
    lo = lane < GM_HEAD_DIM
    bs = bs_ref[...]
    for rows, gu, v_n in zip(blocks, gus, vns):
        for ci in range(rb // GM_CHUNK):
            chunk = slice(ci * GM_CHUNK, (ci + 1) * GM_CHUNK)
            pieces = []
            for pr in range(GM_WIDTH // LANES):
                r = v_n[chunk, pr * LANES:(pr + 1) * LANES]
                m0 = _dot(ws_ref[2 * pr], jnp.where(lo, r, 0.0).astype(BF16))
                m1 = _dot(ws_ref[2 * pr + 1], jnp.where(lo, 0.0, r).astype(BF16))
                pieces.append(m0 + m1)
            mixed = jnp.concatenate(pieces, axis=1) + bs
            gate_ref[pl.ds(rows.start + ci * GM_CHUNK, GM_CHUNK), :] = (gu[chunk, :] * mixed).astype(BF16)


def _even_inproj(x2d, w_in_p, kv_norm, gm_g, gm_b, ws_m, bs_full, cos_t, sin_a, sin_b, batch, seq):
    n_tok = x2d.shape[0]
    tm = INPROJ_TM
    tiles_per_seq = seq // tm
    row = lambda w: pl.BlockSpec((tm, w), lambda i: (i, 0))
    tab = pl.BlockSpec((tm, LANES), lambda i: (i % tiles_per_seq, 0))
    n_in = w_in_p.shape[1]
    return pl.pallas_call(
        _even_inproj_kernel,
        grid=(n_tok // tm,),
        in_specs=[row(D_MODEL), _const_spec((D_MODEL, n_in)), _const_spec((1, KV_RANK)),
                  _const_spec((1, GM_WIDTH)), _const_spec((1, GM_WIDTH)),
                  _const_spec((GM_HEADS, GM_CHUNK, GM_CHUNK)), _const_spec((GM_CHUNK, GM_WIDTH)),
                  tab, tab, tab],
        out_specs=[row(Q_RANK), row(KV_RANK), row(ROPE_DIM), row(2 * LANES), row(GM_WIDTH),
                   pl.BlockSpec((1, GM_CHUNK, GM_WIDTH), lambda i: (i // tiles_per_seq, 0, 0))],
        out_shape=[jax.ShapeDtypeStruct((n_tok, Q_RANK), F32),
                   jax.ShapeDtypeStruct((n_tok, KV_RANK), F32),
                   jax.ShapeDtypeStruct((n_tok, ROPE_DIM), F32),
                   jax.ShapeDtypeStruct((n_tok, 2 * LANES), BF16),
                   jax.ShapeDtypeStruct((n_tok, GM_WIDTH), BF16),
                   jax.ShapeDtypeStruct((batch, GM_CHUNK, GM_WIDTH), F32)],
        compiler_params=_cparams(("arbitrary",)),
        name="even_inproj",
    )(x2d, w_in_p, kv_norm, gm_g, gm_b, ws_m, bs_full, cos_t, sin_a, sin_b)


def _queries(q_a, qn_g, wq_nope, wq_pe, wuk_ref, cos_t, sin_a, sin_b):
    qn = _rms_norm(q_a, qn_g).astype(BF16)
    nope = _dot(qn, wq_nope).astype(BF16)
    lat = jnp.concatenate([_dot(nope[:, pr * LANES:(pr + 1) * LANES], wuk_ref[pr])
                           for pr in range(MLA_HEADS // 2)], axis=1)
    pe = _dot(qn, wq_pe)
    n = MLA_HEADS * LANES
    sin_s = sin_a + sin_b
    pes = [pe[:, h * LANES:(h + 1) * LANES] * cos_t + pe[:, n + h * LANES:n + (h + 1) * LANES] * sin_s
           for h in range(MLA_HEADS)]
    return lat, pes


def _mla_attn_kernel(qa_lo_ref, qa_hi_ref, kcat_ref, qn_ref, wqn_ref, wqp_ref, wuk_ref, wuv_ref,
                     cos_lo_ref, sina_lo_ref, sinb_lo_ref, cos_hi_ref, sina_hi_ref, sinb_hi_ref,
                     out_hbm, qcat_s, m_s, acc_s, obuf, osem, *, nq):
    tq = qa_lo_ref.shape[0]
    n_rows = MLA_HEADS * tq
    b, i = pl.program_id(0), pl.program_id(1)
    step = b * pl.num_programs(1) + i
    n_steps = pl.num_programs(0) * pl.num_programs(1)
    q_tiles = (i, nq - 1 - i)

    def out_copies():
        return [pltpu.make_async_copy(
            obuf.at[t], out_hbm.at[pl.ds(pl.multiple_of((b * nq + q_tiles[t]) * tq, tq), tq), :], osem.at[t])
            for t in range(2)]

    qa = (qa_lo_ref, qa_hi_ref)
    tabs = ((cos_lo_ref, sina_lo_ref, sinb_lo_ref), (cos_hi_ref, sina_hi_ref, sinb_hi_ref))
    for t in range(2):
        lat, pes = _queries(qa[t][...], qn_ref[...], wqn_ref[...], wqp_ref[...], wuk_ref,
                            *(r[...] for r in tabs[t]))
        for h in range(MLA_HEADS):
            qcat_s[t, h * tq:(h + 1) * tq, :] = (ATTN_SCALE_LOG2 * jnp.concatenate(
                [lat[:, h * LANES:(h + 1) * LANES], pes[h]], axis=1)).astype(BF16)

    lanes_x = lambda a, n: jnp.concatenate([a] * n, axis=1)
    rb = ATTN_ROW_BLOCK
    ones = jnp.ones((tq, LANES), BF16)

    def keys(kv_tile):
        k = kcat_ref[pl.ds(pl.multiple_of(kv_tile * tq, tq), tq), :]
        return k, jnp.concatenate([k[:, :KV_RANK], ones], axis=1)

    q_pos = lax.broadcasted_iota(jnp.int32, (rb, tq), 0) & (tq - 1)
    k_pos = lax.broadcasted_iota(jnp.int32, (rb, tq), 1)
    causal = k_pos <= q_pos
    for t in range(2):
        k, v1 = keys(q_tiles[t])
        for r0 in range(0, n_rows, rb):
            rows = slice(r0, r0 + rb)
            s = jnp.where(causal, _dot_nt(qcat_s[t, rows, :], k), -jnp.inf)
            m0 = jnp.broadcast_to(jnp.max(s, axis=-1, keepdims=True), (rb, LANES))
            m_s[t, rows, :] = m0
            acc_s[t, rows, :] = _dot(jnp.exp2(s - lanes_x(m0, tq // LANES)).astype(BF16), v1)

    for u in range(nq - 1):
        hi = u >= i
        t = hi.astype(jnp.int32)
        k, v1 = keys(jnp.where(hi, u - i, u))
        for r0 in range(0, n_rows, rb):
            rows = slice(r0, r0 + rb)
            s = _dot_nt(qcat_s[t, rows, :], k)
            m_old = m_s[t, rows, :]
            m_new = jnp.maximum(m_old, jnp.max(s, axis=-1, keepdims=True))
            alpha = jnp.exp2(m_old - m_new)
            p = jnp.exp2(s - lanes_x(m_new, tq // LANES))
            acc_s[t, rows, :] = lanes_x(alpha, 2) * acc_s[t, rows, :] + _dot(p.astype(BF16), v1)
            m_s[t, rows, :] = m_new

    @pl.when(step > 0)
    def _():
        for cp in out_copies():
            cp.wait()

    for t in range(2):
        o_all = jnp.concatenate(
            [acc_s[t, h * tq:(h + 1) * tq, :KV_RANK] / acc_s[t, h * tq:(h + 1) * tq, KV_RANK:]
             for h in range(MLA_HEADS)], axis=1).astype(BF16)
        obuf[t] = jnp.concatenate(
            [_dot(o_all[:, pr * 2 * LANES:(pr + 1) * 2 * LANES], wuv_ref[pr]) for pr in range(MLA_HEADS // 2)],
            axis=1).astype(BF16)
    for cp in out_copies():
        cp.start()

    @pl.when(step == n_steps - 1)
    def _():
        for cp in out_copies():
            cp.wait()


def _mla_attn(q_a, kcat, qn_g, wq_nope, wq_pe, w_uk, w_uv, cos_t, sin_a, sin_b, batch, seq):
    tq = ATTN_TQ
    nq = seq // tq
    n_attn = MLA_HEADS * V_DIM
    lo = lambda b, i: i
    hi = lambda b, i: nq - 1 - i
    tab = lambda sel: pl.BlockSpec((tq, LANES), lambda b, i: (sel(b, i), 0))
    qa_spec = lambda sel: pl.BlockSpec((tq, Q_RANK), lambda b, i: (b * nq + sel(b, i), 0))
    return pl.pallas_call(
        functools.partial(_mla_attn_kernel, nq=nq),
        grid=(batch, nq // 2),
        in_specs=[qa_spec(lo), qa_spec(hi),
                  pl.BlockSpec((None, seq, 2 * LANES), lambda b, i: (b, 0, 0)),
                  _const_spec((1, Q_RANK)), _const_spec(wq_nope.shape), _const_spec(wq_pe.shape),
                  _const_spec(w_uk.shape), _const_spec(w_uv.shape),
                  tab(lo), tab(lo), tab(lo), tab(hi), tab(hi), tab(hi)],
        out_specs=pl.BlockSpec(memory_space=pl.ANY),
        out_shape=jax.ShapeDtypeStruct((batch * seq, n_attn), BF16),
        scratch_shapes=[pltpu.VMEM((2, MLA_HEADS * tq, 2 * LANES), BF16),
                        pltpu.VMEM((2, MLA_HEADS * tq, LANES), F32),
                        pltpu.VMEM((2, MLA_HEADS * tq, 2 * KV_RANK), F32),
                        pltpu.VMEM((2, tq, n_attn), BF16),
                        pltpu.SemaphoreType.DMA((2,))],
        compiler_params=_cparams(("arbitrary", "arbitrary")),
        name="mla_attn",
    )(q_a, q_a, kcat.reshape(batch, seq, 2 * LANES), qn_g, wq_nope, wq_pe, w_uk, w_uv,
      cos_t, sin_a, sin_b, cos_t, sin_a, sin_b)


def _mix_ffn_kernel(*refs, n_act, has_pre):
    x_ref = refs[0]
    act_refs = refs[1:1 + n_act]
    k = 1 + n_act
    pre_ref = refs[k] if has_pre else None
    k += int(has_pre)
    wo_refs = refs[k:k + n_act]
    k += n_act
    g1_ref, b1_ref, g2_ref, b2_ref, w1_ref, w2_ref, out_ref = refs[k:k + 7]

    tm = x_ref.shape[0]
    rb = min(FFN_ROW_BLOCK, tm)
    blocks = [slice(r0, r0 + rb) for r0 in range(0, tm, rb)]
    fs = []
    for rows in blocks:
        f = None
        for idx in range(n_act):
            a = act_refs[idx][rows, :]
            if has_pre and idx == 0:
                a = _dot(a.astype(BF16), pre_ref[...])
            part = _dot(a.astype(BF16), wo_refs[idx][...])
            f = part if f is None else f + part
        fs.append(f)
    x1s = [_layer_norm(DN_ALPHA * x_ref[rows, :] + f, g1_ref[...], b1_ref[...]) for rows, f in zip(blocks, fs)]
    x1bs = [x1.astype(BF16) for x1 in x1s]
    accs = [None] * len(blocks)
    for c in range(FFN_HIDDEN // FFN_HC):
        cols = slice(c * FFN_HC, (c + 1) * FFN_HC)
        for i, x1b in enumerate(x1bs):
            h = jnp.maximum(_dot(x1b, w1_ref[:, cols]), 0.0)
            part = _dot((h * h).astype(BF16), w2_ref[cols, :])
            accs[i] = part if accs[i] is None else accs[i] + part
    for rows, x1, acc in zip(blocks, x1s, accs):
        out_ref[rows, :] = _layer_norm(DN_ALPHA * x1 + acc, g2_ref[...], b2_ref[...])


def _mix_ffn(x2d, acts, pre_w, wos, ln1_g, ln1_b, ln2_g, ln2_b, w1, w2, name):
    n_tok = x2d.shape[0]
    tm = min(FFN_TM, n_tok)
    row = lambda w: pl.BlockSpec((tm, w), lambda i: (i, 0))
    has_pre = pre_w is not None
    in_specs = [row(D_MODEL)] + [row(a.shape[1]) for a in acts]
    args = [x2d] + list(acts)
    if has_pre:
        in_specs.append(_const_spec(pre_w.shape))
        args.append(pre_w)
    in_specs += [_const_spec(w.shape) for w in wos]
    args += list(wos)
    in_specs += [_const_spec((1, D_MODEL))] * 4 + [_const_spec(w1.shape), _const_spec(w2.shape)]
    args += [ln1_g, ln1_b, ln2_g, ln2_b, w1, w2]
    return pl.pallas_call(
        functools.partial(_mix_ffn_kernel, n_act=len(acts), has_pre=has_pre),
        grid=(n_tok // tm,),
        in_specs=in_specs,
        out_specs=row(D_MODEL),
        out_shape=jax.ShapeDtypeStruct((n_tok, D_MODEL), F32),
        compiler_params=_cparams(("arbitrary",)),
        name=name,
    )(*args)


def _even_inproj_sample_kernel(x_ref, w_ref, kvn_ref, gmg_ref, gmb_ref, ws0_ref, bs0_ref,
                               cos_ref, sina_ref, sinb_ref, qn_ref, wqn_ref, wqp_ref, wuk_ref,
                               ckv_ref, kpe_ref, gate_ref, vn_ref, qlat_ref, qpe_ref):
    z = _dot(x_ref[...].astype(BF16), w_ref[...])
    o_c, o_u, o_v, o_k = Q_RANK, Q_RANK + KV_RANK, Q_RANK + KV_RANK + GM_WIDTH, Q_RANK + KV_RANK + 2 * GM_WIDTH
    cos_t, sin_a, sin_b = cos_ref[...], sina_ref[...], sinb_ref[...]
    c_n = _rms_norm(z[:, o_c:o_u], kvn_ref[...])
    ckv_ref[...] = c_n
    kp = _rope128(z[:, o_k:o_k + LANES], cos_t, sin_a, sin_b)
    kpe_ref[...] = kp[:, :ROPE_DIM]
    gu = jax.nn.gelu(z[:, o_u:o_v])
    v_n = _layer_norm(jax.nn.gelu(z[:, o_v:o_k]), gmg_ref[...], gmb_ref[...])
    vn_ref[...] = v_n
    mixed = ws0_ref[...].astype(F32) * v_n.astype(BF16).astype(F32) + bs0_ref[...]
    gate_ref[...] = (gu * mixed).astype(BF16)
    lat, pes = _queries(z[:, :o_c], qn_ref[...], wqn_ref[...], wqp_ref[...], wuk_ref,
                        cos_t, sin_a, sin_b)
    qlat_ref[...] = lat
    qpe_ref[...] = jnp.concatenate(pes, axis=1)


def _even_inproj_sample(xs, w_in_p, kv_norm, gm_g, gm_b, ws0, bs0, cos_t, sin_a, sin_b,
                        qn_g, wq_nope, wq_pe, w_uk):
    n = xs.shape[0]
    args = (xs, w_in_p, kv_norm, gm_g, gm_b, ws0, bs0, cos_t, sin_a, sin_b, qn_g, wq_nope, wq_pe, w_uk)
    full = lambda shape: pl.BlockSpec(shape, lambda i: (0,) * len(shape))
    widths = [(KV_RANK, F32), (ROPE_DIM, F32), (GM_WIDTH, BF16), (GM_WIDTH, F32),
              (MLA_HEADS * LANES, F32), (MLA_HEADS * LANES, F32)]
    return pl.pallas_call(
        _even_inproj_sample_kernel,
        grid=(1,),
        in_specs=[full(a.shape) for a in args],
        out_specs=[full((n, w)) for w, _ in widths],
        out_shape=[jax.ShapeDtypeStruct((n, w), dt) for w, dt in widths],
        compiler_params=_cparams(("arbitrary",)),
        name="even_inproj_sample",
    )(*args)


def _decode_attn_kernel(pt_ref, qlat_ref, qpe_ref, ckvs_ref, kpes_ref, ckv_hbm, kpe_hbm, out_ref,
                        ckv_buf, kpe_buf, m_buf, sem, *, n_pg, n_groups):
    b = pl.program_id(0)
    n_b = pl.num_programs(0)
    assert n_groups % DEC_SLOTS == 0 and DEC_AHEAD + 2 <= DEC_SLOTS

    def group_copies(seq, grp):
        slot = grp % DEC_SLOTS
        cps = []
        for k in range(n_pg):
            page = pt_ref[seq, grp * n_pg + k]
            keys = pl.ds(k * PAGE_SIZE, PAGE_SIZE)
            cps.append(pltpu.make_async_copy(ckv_hbm.at[page], ckv_buf.at[slot, keys, :], sem.at[0, slot]))
            cps.append(pltpu.make_async_copy(kpe_hbm.at[page], kpe_buf.at[slot, :, keys], sem.at[1, slot]))
        return cps

    @pl.when(b == 0)
    def _():
        for g in range(DEC_AHEAD):
            for cp in group_copies(0, g):
                cp.start()

    heads, half_span = MLA_HEADS, 2 * PAGE_SIZE
    span = 2 * half_span
    n_span = n_pg * PAGE_SIZE // span
    qlat = qlat_ref[...]
    zero = jnp.zeros_like(qlat)
    q2 = jnp.concatenate([jnp.concatenate([qlat, zero], axis=1),
                          jnp.concatenate([zero, qlat], axis=1)], axis=0).astype(BF16)
    qpe = qpe_ref[...][:, :ROPE_DIM].astype(BF16)

    def scores(slot):
        s_pe = _dot(qpe, kpe_buf[slot].astype(BF16))
        pieces = []
        for j in range(n_span):
            m_buf[slot, j] = jnp.concatenate(
                [ckv_buf[slot, pl.ds(j * span, half_span), :],
                 ckv_buf[slot, pl.ds(j * span + half_span, half_span), :]], axis=1).astype(BF16)
            s2 = _dot_nt(q2, m_buf[slot, j])
            pieces += [s2[:heads] + s_pe[:, j * span:j * span + half_span],
                       s2[heads:] + s_pe[:, j * span + half_span:(j + 1) * span]]
        return jnp.concatenate(pieces, axis=1) * ATTN_SCALE

    def values(p, grp):
        slot = grp % DEC_SLOTS
        o2 = None
        for j in range(n_span):
            lhs = jnp.concatenate([p[:, j * span:j * span + half_span],
                                   p[:, j * span + half_span:(j + 1) * span]], axis=0).astype(BF16)
            part = _dot(lhs, m_buf[slot, j])
            o2 = part if o2 is None else o2 + part
        return o2[:heads, :KV_RANK] + o2[heads:, KV_RANK:]

    state = dict(m_run=None, m=None, l=None, acc=None)
    raw = {}
    probs = {}

    def softmax(k):
        s = raw.pop(k)
        m_cur = jnp.max(s, axis=-1, keepdims=True)
        m_new = m_cur if state["m_run"] is None else jnp.maximum(state["m_run"], m_cur)
        state["m_run"] = m_new
        probs[k] = (jnp.exp(s - m_new), m_new)

    def fold(k):
        p, m_k = probs.pop(k)
        pv = values(p, k)
        p_sum = jnp.sum(p, axis=-1, keepdims=True)
        if state["m"] is None:
            state["l"], state["acc"] = p_sum, pv
        else:
            alpha = jnp.exp(state["m"] - m_k)
            state["l"], state["acc"] = alpha * state["l"] + p_sum, alpha * state["acc"] + pv
        state["m"] = m_k

    for g in range(n_groups):
        if g + DEC_AHEAD < n_groups:
            for cp in group_copies(b, g + DEC_AHEAD):
                cp.start()
        else:
            @pl.when(b + 1 < n_b)
            def _():
                for cp in group_copies(b + 1, g + DEC_AHEAD - n_groups):
                    cp.start()
        for cp in group_copies(b, g):
            cp.wait()
        raw[g] = scores(g % DEC_SLOTS)
        if g >= 1:
            softmax(g - 1)
        if g >= 2:
            fold(g - 2)
    softmax(n_groups - 1)
    fold(n_groups - 2)
    fold(n_groups - 1)
    m, l, acc = state["m"], state["l"], state["acc"]

    r = lambda a: a.astype(BF16).astype(F32)
    kv = r(ckvs_ref[...])
    s_self = (jnp.sum(r(qlat_ref[...]) * kv, axis=-1, keepdims=True)
              + jnp.sum(r(qpe_ref[...]) * r(kpes_ref[...]), axis=-1, keepdims=True)) * ATTN_SCALE
    m_n = jnp.maximum(m, s_self)
    a = jnp.exp(m - m_n)
    p_self = jnp.exp(s_self - m_n)
    out_ref[...] = (a * acc + r(p_self) * kv) / (a * l + p_self)


def _decode_attn(page_table, qlat2, qpe2, ckv_s3, kpe_s3, cache_ckv_e, cache_kpe_t):
    dec_batch, n_pages = page_table.shape
    n_pg = DEC_PG
    qspec = pl.BlockSpec((MLA_HEADS, LANES), lambda b, pt: (b, 0))
    self_spec = pl.BlockSpec((None, 1, LANES), lambda b, pt: (b, 0, 0))
    hbm = pl.BlockSpec(memory_space=pl.ANY)
    grid_spec = pltpu.PrefetchScalarGridSpec(
        num_scalar_prefetch=1,
        grid=(dec_batch,),
        in_specs=[qspec, qspec, self_spec, self_spec, hbm, hbm],
        out_specs=qspec,
        scratch_shapes=[pltpu.VMEM((DEC_SLOTS, n_pg * PAGE_SIZE, KV_RANK), F32),
                        pltpu.VMEM((DEC_SLOTS, ROPE_DIM, n_pg * PAGE_SIZE), F32),
                        pltpu.VMEM((DEC_SLOTS, n_pg * PAGE_SIZE // (4 * PAGE_SIZE), 2 * PAGE_SIZE, 2 * KV_RANK), BF16),
                        pltpu.SemaphoreType.DMA((2, DEC_SLOTS))],
    )
    return pl.pallas_call(
        functools.partial(_decode_attn_kernel, n_pg=n_pg, n_groups=n_pages // n_pg),
        grid_spec=grid_spec,
        out_shape=jax.ShapeDtypeStruct((dec_batch * MLA_HEADS, KV_RANK), F32),
        compiler_params=_cparams(("arbitrary",)),
        name="decode_attn",
    )(page_table, qlat2, qpe2, ckv_s3, kpe_s3, cache_ckv_e, cache_kpe_t)


def _s5_readout(hcat_blocks, u, cw_ref, d_ref, wglu_ref):
    y = jnp.concatenate([_dot(hcat_blocks[k], cw_ref[k]) for k in range(S5_NB)], axis=1)
    y = jax.nn.gelu(y + d_ref[...] * u)
    return y * jax.nn.sigmoid(_dot(y.astype(BF16), wglu_ref[...]))


def _s5_prompt_kernel(x_ref, win_ref, bw_ref, a_ref, cw_ref, d_ref, wglu_ref,
                      yg_ref, hre_ref, him_ref, u_s, y_s, bu_s, h_s):
    n_b, seg, _ = x_ref.shape
    half = S5_SLABS // 2
    sub = S5_SUB
    n_sub = seg // sub
    rows_sub = sub * n_b
    q_slabs = S5_WIDTH // LANES

    @pl.when(pl.program_id(0) == 0)
    def _():
        h_s[...] = jnp.zeros(h_s.shape, F32)

    def project(s):
        rows = slice(s * rows_sub, (s + 1) * rows_sub)
        xs = jnp.concatenate([x_ref[b, s * sub:(s + 1) * sub, :] for b in range(n_b)], axis=0)
        u = _dot(xs.astype(BF16), win_ref[...])
        for b in range(n_b):
            for q in range(q_slabs):
                u_s[s, q, pl.ds(b, sub, stride=n_b), :] = u[b * sub:(b + 1) * sub, q * LANES:(q + 1) * LANES]
        for k in range(S5_NB):
            r = _dot(u_s[s, k].astype(BF16), bw_ref[k])
            for q in range(4):
                bu_s[4 * k + q, rows, :] = r[:, q * LANES:(q + 1) * LANES]
                bu_s[half + 4 * k + q, rows, :] = r[:, (4 + q) * LANES:(5 + q) * LANES]

    def recur(s, h):
        for t in range(s * sub, (s + 1) * sub):
            rows = slice(t * n_b, (t + 1) * n_b)
            hs = bu_s[:, rows, :]
            ar, ai = a_ref[:half], a_ref[half:]
            hr, hi = h[:half], h[half:]
            h = jnp.concatenate([ar * hr - ai * hi + hs[:half], ar * hi + ai * hr + hs[half:]], axis=0)
            bu_s[:, rows, :] = h
        return h

    def read_out(s):
        rows = slice(s * rows_sub, (s + 1) * rows_sub)
        blocks = [jnp.concatenate([bu_s[4 * k + q, rows, :] for q in range(4)]
                                  + [bu_s[half + 4 * k + q, rows, :] for q in range(4)], axis=1).astype(BF16)
                  for k in range(S5_NB)]
        u_t = jnp.concatenate([u_s[s, q] for q in range(q_slabs)], axis=1)
        y = _s5_readout(blocks, u_t, cw_ref, d_ref, wglu_ref)
        for q in range(q_slabs):
            y_s[s, q] = y[:, q * LANES:(q + 1) * LANES]
        for b in range(n_b):
            yg_ref[b, s * sub:(s + 1) * sub, :] = jnp.concatenate(
                [y_s[s, q, pl.ds(b, sub, stride=n_b), :] for q in range(q_slabs)], axis=1).astype(BF16)

    h = h_s[...]
    project(0)
    for s in range(n_sub):
        if s + 1 < n_sub:
            project(s + 1)
        h = recur(s, h)
        if s >= 1:
            read_out(s - 1)
    read_out(n_sub - 1)
    h_s[...] = h
    hre_ref[...] = jnp.concatenate([h[s] for s in range(half)], axis=1)
    him_ref[...] = jnp.concatenate([h[half + s] for s in range(half)], axis=1)


def _s5_prompt(x3, w_in, bw, a_b, cw, d_row, w_glu):
    batch, seq, _ = x3.shape
    seg = S5_L
    slab_rows = (seg // S5_SUB, S5_WIDTH // LANES, S5_SUB * batch, LANES)
    return pl.pallas_call(
        _s5_prompt_kernel,
        grid=(seq // seg,),
        in_specs=[pl.BlockSpec((batch, seg, D_MODEL), lambda c: (0, c, 0)),
                  _const_spec(w_in.shape), _const_spec(bw.shape), _const_spec(a_b.shape),
                  _const_spec(cw.shape), _const_spec(d_row.shape), _const_spec(w_glu.shape)],
        out_specs=[pl.BlockSpec((batch, seg, S5_WIDTH), lambda c: (0, c, 0)),
                   pl.BlockSpec((batch, S5_STATES), lambda c: (0, 0)),
                   pl.BlockSpec((batch, S5_STATES), lambda c: (0, 0))],
        out_shape=[jax.ShapeDtypeStruct((batch, seq, S5_WIDTH), BF16),
                   jax.ShapeDtypeStruct((batch, S5_STATES), F32),
                   jax.ShapeDtypeStruct((batch, S5_STATES), F32)],
        scratch_shapes=[pltpu.VMEM(slab_rows, F32), pltpu.VMEM(slab_rows, F32),
                        pltpu.VMEM((S5_SLABS, seg * batch, LANES), F32),
                        pltpu.VMEM((S5_SLABS, batch, LANES), F32)],
        compiler_params=_cparams(("arbitrary",)),
        name="s5_prompt",
    )(x3, w_in, bw, a_b, cw, d_row, w_glu)


def _s5_sample_kernel(x_ref, h0r_ref, h0i_ref, win_ref, bw_ref, ar_ref, ai_ref, cw_ref, d_ref, wglu_ref,
                      yg_ref, hre_ref, him_ref):
    u = _dot(x_ref[...].astype(BF16), win_ref[...])
    ub = u.astype(BF16)
    bu = [_dot(ub[:, k * LANES:(k + 1) * LANES], bw_ref[k]) for k in range(S5_NB)]
    w = S5_KB * S5_STATE
    bu_r = jnp.concatenate([r[:, :w] for r in bu], axis=1)
    bu_i = jnp.concatenate([r[:, w:] for r in bu], axis=1)
    ar, ai, h0r, h0i = ar_ref[...], ai_ref[...], h0r_ref[...], h0i_ref[...]
    hr = bu_r + (ar * h0r - ai * h0i)
    hi = bu_i + (ar * h0i + ai * h0r)
    hre_ref[...] = hr
    him_ref[...] = hi
    blocks = [jnp.concatenate([hr[:, k * w:(k + 1) * w], hi[:, k * w:(k + 1) * w]], axis=1).astype(BF16)
              for k in range(S5_NB)]
    yg_ref[...] = _s5_readout(blocks, u, cw_ref, d_ref, wglu_ref).astype(BF16)


def _s5_sample(xs, h0r, h0i, w_in, bw, a_r, a_i, cw, d_row, w_glu):
    n = xs.shape[0]
    args = (xs, h0r, h0i, w_in, bw, a_r, a_i, cw, d_row, w_glu)
    full = lambda shape: pl.BlockSpec(shape, lambda i: (0,) * len(shape))
    return pl.pallas_call(
        _s5_sample_kernel,
        grid=(1,),
        in_specs=[full(a.shape) for a in args],
        out_specs=[full((n, S5_WIDTH)), full((n, S5_STATES)), full((n, S5_STATES))],
        out_shape=[jax.ShapeDtypeStruct((n, S5_WIDTH), BF16),
                   jax.ShapeDtypeStruct((n, S5_STATES), F32),
                   jax.ShapeDtypeStruct((n, S5_STATES), F32)],
        compiler_params=_cparams(("arbitrary",)),
        name="s5_sample",
    )(*args)


def _rope_tables(pos):
    freqs = ROPE_THETA ** (-jnp.arange(ROPE_HALF, dtype=F32) / ROPE_HALF)
    ang = pos.astype(F32)[:, None] * freqs
    cos, sin = jnp.cos(ang), jnp.sin(ang)
    zero = jnp.zeros_like(cos)
    pad = jnp.zeros((pos.shape[0], LANES - ROPE_DIM), F32)
    cos_t = jnp.concatenate([cos, cos, pad], axis=1)
    sin_a = jnp.concatenate([zero, sin, pad], axis=1)
    sin_b = jnp.concatenate([-sin, zero, pad], axis=1)
    return cos_t, sin_a, sin_b


def _block_diag(blocks):
    n, r, c = blocks.shape
    eye = jnp.eye(n, dtype=blocks.dtype)
    return jnp.einsum('nrc,nm->nrmc', blocks, eye).reshape(n * r, n * c)


def _even_params(w_in, q_norm, w_q_b, kv_norm, w_kv_b, gm_g, gm_b, gm_w_s, gm_b_s, w_out):
    i1, i2, i3, i4 = Q_RANK, Q_RANK + KV_RANK, Q_RANK + KV_RANK + ROPE_DIM, Q_RANK + KV_RANK + ROPE_DIM + GM_WIDTH
    w_in_p = jnp.concatenate([w_in[:, :i2], w_in[:, i3:], w_in[:, i2:i3],
                              jnp.zeros((D_MODEL, LANES - ROPE_DIM), F32)], axis=1).astype(BF16)
    wq_nope = w_q_b[:, :, :NOPE_DIM].reshape(Q_RANK, MLA_HEADS * NOPE_DIM).astype(BF16)
    pe = w_q_b[:, :, NOPE_DIM:]
    pe_sw = jnp.concatenate([pe[:, :, ROPE_HALF:], pe[:, :, :ROPE_HALF]], axis=2)
    lane_pad = jnp.zeros((Q_RANK, MLA_HEADS, LANES - ROPE_DIM), F32)
    wq_pe = jnp.concatenate(
        [jnp.concatenate([w, lane_pad], axis=2).reshape(Q_RANK, MLA_HEADS * LANES) for w in (pe, pe_sw)],
        axis=1).astype(BF16)
    uk = jnp.transpose(w_kv_b[:, :, :NOPE_DIM], (1, 2, 0))
    uv = jnp.transpose(w_kv_b[:, :, NOPE_DIM:], (1, 0, 2))
    pairs = range(MLA_HEADS // 2)
    w_uk = jnp.stack([_block_diag(uk[2 * pr:2 * pr + 2]) for pr in pairs]).astype(BF16)
    w_uv_p = jnp.stack([_block_diag(uv[2 * pr:2 * pr + 2]) for pr in pairs]).astype(BF16)
    w_uv = _block_diag(uv).astype(BF16)
    causal = jnp.tril(jnp.ones((GM_CHUNK, GM_CHUNK), dtype=bool))
    ws_m = jnp.where(causal[None], gm_w_s, 0).astype(BF16)
    bs_full = jnp.repeat(gm_b_s.T, GM_HEAD_DIM, axis=1)
    ws0 = jnp.repeat(gm_w_s[:, 0, 0], GM_HEAD_DIM)[None, :].astype(BF16)
    bs0 = bs_full[:1]
    n_attn = MLA_HEADS * V_DIM
    return dict(w_in_p=w_in_p, qn=q_norm[None, :], wq_nope=wq_nope, wq_pe=wq_pe, w_uk=w_uk, w_uv=w_uv, w_uv_p=w_uv_p,
                kvn=kv_norm[None, :], gm_g=gm_g[None, :], gm_b=gm_b[None, :], ws_m=ws_m, bs_full=bs_full,
                ws0=ws0, bs0=bs0, wo_attn=w_out[:n_attn].astype(BF16), wo_gate=w_out[n_attn:].astype(BF16))


def _s5_params(w_in, a_re, a_im, b_re, b_im, c_re, c_im, d, log_dt, w_glu, w_out, batch):
    dt = jnp.exp(log_dt)[:, None]
    ld_r, ld_i = a_re * dt, a_im * dt
    mag = jnp.exp(ld_r)
    ab_r, ab_i = mag * jnp.cos(ld_i), mag * jnp.sin(ld_i)
    den = a_re * a_re + a_im * a_im
    cr = ((ab_r - 1.0) * a_re + ab_i * a_im) / den
    ci = (ab_i * a_re - (ab_r - 1.0) * a_im) / den
    bb_r = cr[..., None] * b_re - ci[..., None] * b_im
    bb_i = cr[..., None] * b_im + ci[..., None] * b_re

    def in_blocks(bb):
        return jnp.stack([_block_diag(jnp.transpose(bb[k * S5_KB:(k + 1) * S5_KB], (0, 2, 1)))
                          for k in range(S5_NB)])

    def out_blocks(cc):
        return jnp.stack([_block_diag(jnp.transpose(cc[k * S5_KB:(k + 1) * S5_KB], (0, 2, 1)))
                          for k in range(S5_NB)])

    bw = jnp.concatenate([in_blocks(bb_r), in_blocks(bb_i)], axis=2).astype(BF16)
    cw = jnp.concatenate([out_blocks(c_re), out_blocks(-c_im)], axis=1).astype(BF16)
    a_r, a_i = ab_r.reshape(1, S5_STATES), ab_i.reshape(1, S5_STATES)
    half = S5_SLABS // 2
    a_b = jnp.concatenate([jnp.broadcast_to(a_r.reshape(half, 1, LANES), (half, batch, LANES)),
                           jnp.broadcast_to(a_i.reshape(half, 1, LANES), (half, batch, LANES))], axis=0)
    return dict(w_in=w_in.astype(BF16), bw=bw, cw=cw, a_r=a_r, a_i=a_i, a_b=a_b,
                d=d.reshape(1, S5_WIDTH), w_glu=w_glu.astype(BF16), w_out=w_out.astype(BF16))


def kernel(x_prompt, x_sample, cache_ckv, cache_kpe, state_s5_re, state_s5_im, page_table, w_in_even, mla_q_norm, mla_w_q_b, mla_kv_norm, mla_w_kv_b, gm_norm_g, gm_norm_b, gm_w_s, gm_b_s, w_out_even, w_in_odd, s5_a_re, s5_a_im, s5_b_re, s5_b_im, s5_c_re, s5_c_im, s5_d, s5_log_dt, s5_w_glu, w_out_odd, ln_mix_g, ln_mix_b, ln_ffn_g, ln_ffn_b, ffn_w1, ffn_w2):
    batch, seq, _ = x_prompt.shape
    dec_batch, dec_seq, _ = x_sample.shape
    assert dec_seq == 1 and seq % INPROJ_TM == 0 and seq % ATTN_TQ == 0 and seq % S5_L == 0
    assert (seq // ATTN_TQ) % 2 == 0
    assert (batch * seq) % FFN_TM == 0 and page_table.shape[1] % DEC_PG == 0
    assert batch == SUBLANES

    xp = x_prompt.reshape(batch * seq, D_MODEL)
    xs = x_sample.reshape(dec_batch, D_MODEL)
    tabs_p = _rope_tables(jnp.arange(seq, dtype=jnp.int32))
    tabs_s = _rope_tables(PAST_LEN + jnp.arange(dec_seq, dtype=jnp.int32))

    outs = {k: [] for k in ("ckv_p", "kpe_p", "ckv_s", "kpe_s", "gmv_p", "gmv_s",
                            "s5re_p", "s5im_p", "s5re_s", "s5im_s")}
    for layer in range(DEPTH):
        ln = (ln_mix_g[layer][None, :], ln_mix_b[layer][None, :],
              ln_ffn_g[layer][None, :], ln_ffn_b[layer][None, :],
              ffn_w1[layer].astype(BF16), ffn_w2[layer].astype(BF16))
        if layer % 2 == 0:
            e = layer // 2
            p = _even_params(w_in_even[e], mla_q_norm[e], mla_w_q_b[e], mla_kv_norm[e], mla_w_kv_b[e],
                             gm_norm_g[e], gm_norm_b[e], gm_w_s[e], gm_b_s[e], w_out_even[e])
            q_a, ckv, kpe, kcat, gate, gmv = _even_inproj(
                xp, p["w_in_p"], p["kvn"], p["gm_g"], p["gm_b"], p["ws_m"], p["bs_full"], *tabs_p, batch, seq)
            attn = _mla_attn(q_a, kcat, p["qn"], p["wq_nope"], p["wq_pe"], p["w_uk"], p["w_uv_p"], *tabs_p,
                             batch, seq)
            xp = _mix_ffn(xp, [attn, gate], None, [p["wo_attn"], p["wo_gate"]], *ln, name="even_ffn_prompt")
            outs["ckv_p"].append(ckv.reshape(batch, seq, KV_RANK))
            outs["kpe_p"].append(kpe.reshape(batch, seq, ROPE_DIM))
            outs["gmv_p"].append(gmv)
            ckv_s, kpe_s, gate_s, vn_s, qlat_s, qpe_s = _even_inproj_sample(
                xs, p["w_in_p"], p["kvn"], p["gm_g"], p["gm_b"], p["ws0"], p["bs0"], *tabs_s,
                p["qn"], p["wq_nope"], p["wq_pe"], p["w_uk"])
            kpe_pad = jnp.concatenate([kpe_s, jnp.zeros((dec_batch, LANES - ROPE_DIM), F32)], axis=1)
            o_lat = _decode_attn(page_table,
                                 qlat_s.reshape(dec_batch * MLA_HEADS, LANES),
                                 qpe_s.reshape(dec_batch * MLA_HEADS, LANES),
                                 ckv_s.reshape(dec_batch, 1, KV_RANK), kpe_pad.reshape(dec_batch, 1, LANES),
                                 cache_ckv[e], jnp.swapaxes(cache_kpe[e], 1, 2))
            xs = _mix_ffn(xs, [o_lat.reshape(dec_batch, MLA_HEADS * KV_RANK), gate_s], p["w_uv"],
                          [p["wo_attn"], p["wo_gate"]], *ln, name="even_ffn_sample")
            outs["ckv_s"].append(ckv_s.reshape(dec_batch, 1, KV_RANK))
            outs["kpe_s"].append(kpe_s.reshape(dec_batch, 1, ROPE_DIM))
            outs["gmv_s"].append(vn_s.reshape(dec_batch, 1, GM_WIDTH))
        else:
            o = layer // 2
            p = _s5_params(w_in_odd[o], s5_a_re[o], s5_a_im[o], s5_b_re[o], s5_b_im[o], s5_c_re[o],
                           s5_c_im[o], s5_d[o], s5_log_dt[o], s5_w_glu[o], w_out_odd[o], batch)
            yg, hre, him = _s5_prompt(xp.reshape(batch, seq, D_MODEL), p["w_in"], p["bw"], p["a_b"],
                                      p["cw"], p["d"], p["w_glu"])
            xp = _mix_ffn(xp, [yg.reshape(batch * seq, S5_WIDTH)], None, [p["w_out"]], *ln,
                          name="odd_ffn_prompt")
            outs["s5re_p"].append(hre.reshape(batch, S5_GROUPS, S5_STATE))
            outs["s5im_p"].append(him.reshape(batch, S5_GROUPS, S5_STATE))
            yg_s, hre_s, him_s = _s5_sample(
                xs, state_s5_re[o].reshape(dec_batch, S5_STATES), state_s5_im[o].reshape(dec_batch, S5_STATES),
                p["w_in"], p["bw"], p["a_r"], p["a_i"], p["cw"], p["d"], p["w_glu"])
            xs = _mix_ffn(xs, [yg_s], None, [p["w_out"]], *ln, name="odd_ffn_sample")
            outs["s5re_s"].append(hre_s.reshape(dec_batch, S5_GROUPS, S5_STATE))
            outs["s5im_s"].append(him_s.reshape(dec_batch, S5_GROUPS, S5_STATE))

    st = jnp.stack
    return (xp.reshape(batch, seq, D_MODEL), xs.reshape(dec_batch, dec_seq, D_MODEL),
            st(outs["ckv_p"]), st(outs["kpe_p"]), st(outs["ckv_s"]), st(outs["kpe_s"]),
            st(outs["gmv_p"]), st(outs["gmv_s"]),
            st(outs["s5re_p"]), st(outs["s5im_p"]), st(outs["s5re_s"]), st(outs["s5im_s"]))
```

```python
import functools
import math

import jax
import jax.numpy as jnp
from jax import lax
from jax.experimental import pallas as pl
from jax.experimental.pallas import tpu as pltpu

F32 = jnp.float32
BF16 = jnp.bfloat16

D_MODEL = 1024
DEPTH = 2
PAST_LEN = 16384
PAGE_SIZE = 128
MLA_HEADS = 8
NOPE_DIM = 64
ROPE_DIM = 32
ROPE_HALF = ROPE_DIM // 2
V_DIM = 64
Q_RANK = 256
KV_RANK = 128
ROPE_THETA = 10000.0
ATTN_SCALE = 1.0 / math.sqrt(NOPE_DIM + ROPE_DIM)
ATTN_SCALE_LOG2 = ATTN_SCALE * math.log2(math.e)
GM_HEADS = 8
GM_HEAD_DIM = 64
GM_WIDTH = GM_HEADS * GM_HEAD_DIM
GM_CHUNK = 128
S5_GROUP_DIM = 16
S5_GROUPS = 32
S5_STATE = 64
S5_WIDTH = S5_GROUPS * S5_GROUP_DIM
S5_STATES = S5_GROUPS * S5_STATE
FFN_HIDDEN = 4 * D_MODEL
DN_ALPHA = (2 * DEPTH) ** 0.25
LN_EPS = 1e-5
RMS_EPS = 1e-6

LANES = 128
SUBLANES = 8
VMEM_LIMIT_BYTES = 56 * 1024 * 1024

INPROJ_TM = 512
INPROJ_ROW_BLOCK = 256
ATTN_TQ = 256
ATTN_ROW_BLOCK = 256
FFN_TM = 1024
FFN_HC = 1024
FFN_ROW_BLOCK = 256
S5_L = 128
S5_SUB = 32
DEC_PG = 32
DEC_SLOTS = 4
DEC_AHEAD = 2
DEC_SPAN_PAGES = 4
S5_KB = 8
S5_NB = S5_GROUPS // S5_KB
S5_SLABS = 2 * S5_STATES // LANES
S5_BLK_SLABS = S5_KB * S5_STATE // LANES


def _cparams(sem):
    return pltpu.CompilerParams(dimension_semantics=sem, vmem_limit_bytes=VMEM_LIMIT_BYTES)


def _const_spec(shape):
    n = len(shape)
    return pl.BlockSpec(shape, lambda *_: (0,) * n, pipeline_mode=pl.Buffered(1))


def _layer_norm(x, g, b):
    mu = jnp.mean(x, axis=-1, keepdims=True)
    xc = x - mu
    var = jnp.mean(xc * xc, axis=-1, keepdims=True)
    return xc * lax.rsqrt(var + LN_EPS) * g + b


def _rms_norm(x, g):
    return x * lax.rsqrt(jnp.mean(x * x, axis=-1, keepdims=True) + RMS_EPS) * g


def _rope128(x, cos_t, sin_a, sin_b):
    return (x * cos_t + pltpu.roll(x, ROPE_HALF, 1) * sin_a
            + pltpu.roll(x, LANES - ROPE_HALF, 1) * sin_b)


def _dot(a, b):
    return jnp.dot(a, b, preferred_element_type=F32)


def _dot_nt(a, b):
    return lax.dot_general(a, b, (((1,), (1,)), ((), ())), preferred_element_type=F32)


def _even_inproj_kernel(x_ref, w_ref, kvn_ref, gmg_ref, gmb_ref, ws_ref, bs_ref,
                        cos_ref, sina_ref, sinb_ref,
                        qa_ref, ckv_ref, kpe_ref, kcat_ref, gate_ref, gmv_ref):
    tm = x_ref.shape[0]
    o_c, o_u, o_v, o_k = Q_RANK, Q_RANK + KV_RANK, Q_RANK + KV_RANK + GM_WIDTH, Q_RANK + KV_RANK + 2 * GM_WIDTH
    rb = min(INPROJ_ROW_BLOCK, tm)
    blocks = [slice(r0, r0 + rb) for r0 in range(0, tm, rb)]
    zs = [_dot(x_ref[rows, :].astype(BF16), w_ref[...]) for rows in blocks]

    gus, vns = [], []
    for rows, z in zip(blocks, zs):
        qa_ref[rows, :] = z[:, :o_c]
        c_n = _rms_norm(z[:, o_c:o_u], kvn_ref[...])
        ckv_ref[rows, :] = c_n
        kp = _rope128(z[:, o_k:o_k + LANES], cos_ref[rows, :], sina_ref[rows, :], sinb_ref[rows, :])
        kpe_ref[rows, :] = kp[:, :ROPE_DIM]
        kcat_ref[rows, :] = jnp.concatenate([c_n, kp], axis=1).astype(BF16)
        gus.append(jax.nn.gelu(z[:, o_u:o_v]))
        vns.append(_layer_norm(jax.nn.gelu(z[:, o_v:o_k]), gmg_ref[...], gmb_ref[...]))

    gmv_ref[0] = vns[-1][rb - GM_CHUNK:, :]

    lane = lax.broadcasted_iota(jnp.int32, (GM_CHUNK, LANES), 1)
    lo = lane < GM_HEAD_DIM
    bs = bs_ref[...]
    for rows, gu, v_n in zip(blocks, gus, vns):
        for ci in range(rb // GM_CHUNK):
            chunk = slice(ci * GM_CHUNK, (ci + 1) * GM_CHUNK)
            pieces = []
            for pr in range(GM_WIDTH // LANES):
                r = v_n[chunk, pr * LANES:(pr + 1) * LANES]
                m0 = _dot(ws_ref[2 * pr], jnp.where(lo, r, 0.0).astype(BF16))
                m1 = _dot(ws_ref[2 * pr + 1], jnp.where(lo, 0.0, r).astype(BF16))
                pieces.append(m0 + m1)
            mixed = jnp.concatenate(pieces, axis=1) + bs
            gate_ref[pl.ds(rows.start + ci * GM_CHUNK, GM_CHUNK), :] = (gu[chunk, :] * mixed).astype(BF16)


def _even_inproj(x2d, w_in_p, kv_norm, gm_g, gm_b, ws_m, bs_full, cos_t, sin_a, sin_b, batch, seq):
    n_tok = x2d.shape[0]
    tm = INPROJ_TM
    tiles_per_seq = seq // tm
    row = lambda w: pl.BlockSpec((tm, w), lambda i: (i, 0))
    tab = pl.BlockSpec((tm, LANES), lambda i: (i % tiles_per_seq, 0))
    n_in = w_in_p.shape[1]
    return pl.pallas_call(
        _even_inproj_kernel,
        grid=(n_tok // tm,),
        in_specs=[row(D_MODEL), _const_spec((D_MODEL, n_in)), _const_spec((1, KV_RANK)),
                  _const_spec((1, GM_WIDTH)), _const_spec((1, GM_WIDTH)),
                  _const_spec((GM_HEADS, GM_CHUNK, GM_CHUNK)), _const_spec((GM_CHUNK, GM_WIDTH)),
                  tab, tab, tab],
        out_specs=[row(Q_RANK), row(KV_RANK), row(ROPE_DIM), row(2 * LANES), row(GM_WIDTH),
                   pl.BlockSpec((1, GM_CHUNK, GM_WIDTH), lambda i: (i // tiles_per_seq, 0, 0))],
        out_shape=[jax.ShapeDtypeStruct((n_tok, Q_RANK), F32),
                   jax.ShapeDtypeStruct((n_tok, KV_RANK), F32),
                   jax.ShapeDtypeStruct((n_tok, ROPE_DIM), F32),
                   jax.ShapeDtypeStruct((n_tok, 2 * LANES), BF16),
                   jax.ShapeDtypeStruct((n_tok, GM_WIDTH), BF16),
                   jax.ShapeDtypeStruct((batch, GM_CHUNK, GM_WIDTH), F32)],
        compiler_params=_cparams(("arbitrary",)),
        name="even_inproj",
    )(x2d, w_in_p, kv_norm, gm_g, gm_b, ws_m, bs_full, cos_t, sin_a, sin_b)


def _queries(q_a, qn_g, wq_nope, wq_pe, wuk_ref, cos_t, sin_a, sin_b, compact):
    qn = _rms_norm(q_a, qn_g).astype(BF16)
    nope = _dot(qn, wq_nope).astype(BF16)
    lat = jnp.concatenate([_dot(nope[:, pr * LANES:(pr + 1) * LANES], wuk_ref[pr])
                           for pr in range(MLA_HEADS // 2)], axis=1)
    pe = _dot(qn, wq_pe)
    n = wq_pe.shape[1] // 2
    sin_s = sin_a + sin_b
    roped = [pe[:, c * LANES:(c + 1) * LANES] * cos_t + pe[:, n + c * LANES:n + (c + 1) * LANES] * sin_s
             for c in range(n // LANES)]
    if not compact:
        return lat, roped
    per_block = LANES // ROPE_DIM
    lane_head = lax.broadcasted_iota(jnp.int32, roped[0].shape, 1) // ROPE_DIM
    return lat, [jnp.where(lane_head == h % per_block, roped[h // per_block], 0.0) for h in range(MLA_HEADS)]


def _mla_attn_kernel(qa_lo_ref, qa_hi_ref, kcat_ref, qn_ref, wqn_ref, wqp_ref, wuk_ref, wuv_ref,
                     cos_lo_ref, sina_lo_ref, sinb_lo_ref, cos_hi_ref, sina_hi_ref, sinb_hi_ref,
                     out_hbm, qcat_s, m_s, acc_s, obuf, osem, *, nq):
    tq = qa_lo_ref.shape[0]
    n_rows = MLA_HEADS * tq
    b, i = pl.program_id(0), pl.program_id(1)
    step = b * pl.num_programs(1) + i
    n_steps = pl.num_programs(0) * pl.num_programs(1)
    q_tiles = (i, nq - 1 - i)

    def out_copies():
        return [pltpu.make_async_copy(
            obuf.at[t], out_hbm.at[pl.ds(pl.multiple_of((b * nq + q_tiles[t]) * tq, tq), tq), :], osem.at[t])
            for t in range(2)]

    qa = (qa_lo_ref, qa_hi_ref)
    tabs = ((cos_lo_ref, sina_lo_ref, sinb_lo_ref), (cos_hi_ref, sina_hi_ref, sinb_hi_ref))
    for t in range(2):
        lat, pes = _queries(qa[t][...], qn_ref[...], wqn_ref[...], wqp_ref[...], wuk_ref,
                            *(r[...] for r in tabs[t]), compact=True)
        for h in range(MLA_HEADS):
            qcat_s[t, h * tq:(h + 1) * tq, :] = (ATTN_SCALE_LOG2 * jnp.concatenate(
                [lat[:, h * LANES:(h + 1) * LANES], pes[h]], axis=1)).astype(BF16)

    lanes_x = lambda a, n: jnp.concatenate([a] * n, axis=1)
    rb = ATTN_ROW_BLOCK
    ones = jnp.ones((tq, LANES), BF16)

    def keys(kv_tile):
        k = kcat_ref[pl.ds(pl.multiple_of(kv_tile * tq, tq), tq), :]
        return k, jnp.concatenate([k[:, :KV_RANK], ones], axis=1)

    q_pos = lax.broadcasted_iota(jnp.int32, (rb, tq), 0) & (tq - 1)
    k_pos = lax.broadcasted_iota(jnp.int32, (rb, tq), 1)
    causal = k_pos <= q_pos
    for t in range(2):
        k, v1 = keys(q_tiles[t])
        for r0 in range(0, n_rows, rb):
            rows = slice(r0, r0 + rb)
            s = jnp.where(causal, _dot_nt(qcat_s[t, rows, :], k), -jnp.inf)
            m0 = jnp.broadcast_to(jnp.max(s, axis=-1, keepdims=True), (rb, LANES))
            m_s[t, rows, :] = m0
            acc_s[t, rows, :] = _dot(jnp.exp2(s - lanes_x(m0, tq // LANES)).astype(BF16), v1)

    for u in range(nq - 1):
        hi = u >= i
        t = hi.astype(jnp.int32)
        k, v1 = keys(jnp.where(hi, u - i, u))
        for r0 in range(0, n_rows, rb):
            rows = slice(r0, r0 + rb)
            s = _dot_nt(qcat_s[t, rows, :], k)
            m_old = m_s[t, rows, :]
            m_new = jnp.maximum(m_old, jnp.max(s, axis=-1, keepdims=True))
            alpha = jnp.exp2(m_old - m_new)
            p = jnp.exp2(s - lanes_x(m_new, tq // LANES))
            acc_s[t, rows, :] = lanes_x(alpha, 2) * acc_s[t, rows, :] + _dot(p.astype(BF16), v1)
            m_s[t, rows, :] = m_new

    @pl.when(step > 0)
    def _():
        for cp in out_copies():
            cp.wait()

    for t in range(2):
        o_all = jnp.concatenate(
            [acc_s[t, h * tq:(h + 1) * tq, :KV_RANK] / acc_s[t, h * tq:(h + 1) * tq, KV_RANK:]
             for h in range(MLA_HEADS)], axis=1).astype(BF16)
        obuf[t] = jnp.concatenate(
            [_dot(o_all[:, pr * 2 * LANES:(pr + 1) * 2 * LANES], wuv_ref[pr]) for pr in range(MLA_HEADS // 2)],
            axis=1).astype(BF16)
    for cp in out_copies():
        cp.start()

    @pl.when(step == n_steps - 1)
    def _():
        for cp in out_copies():
            cp.wait()


def _mla_attn(q_a, kcat, qn_g, wq_nope, wq_pe, w_uk, w_uv, cos_t, sin_a, sin_b, batch, seq):
    tq = ATTN_TQ
    nq = seq // tq
    n_attn = MLA_HEADS * V_DIM
    lo = lambda b, i: i
    hi = lambda b, i: nq - 1 - i
    tab = lambda sel: pl.BlockSpec((tq, LANES), lambda b, i: (sel(b, i), 0))
    qa_spec = lambda sel: pl.BlockSpec((tq, Q_RANK), lambda b, i: (b * nq + sel(b, i), 0))
    return pl.pallas_call(
        functools.partial(_mla_attn_kernel, nq=nq),
        grid=(batch, nq // 2),
        in_specs=[qa_spec(lo), qa_spec(hi),
                  pl.BlockSpec((None, seq, 2 * LANES), lambda b, i: (b, 0, 0)),
                  _const_spec((1, Q_RANK)), _const_spec(wq_nope.shape), _const_spec(wq_pe.shape),
                  _const_spec(w_uk.shape), _const_spec(w_uv.shape),
                  tab(lo), tab(lo), tab(lo), tab(hi), tab(hi), tab(hi)],
        out_specs=pl.BlockSpec(memory_space=pl.ANY),
        out_shape=jax.ShapeDtypeStruct((batch * seq, n_attn), BF16),
        scratch_shapes=[pltpu.VMEM((2, MLA_HEADS * tq, 2 * LANES), BF16),
                        pltpu.VMEM((2, MLA_HEADS * tq, LANES), F32),
                        pltpu.VMEM((2, MLA_HEADS * tq, 2 * KV_RANK), F32),
                        pltpu.VMEM((2, tq, n_attn), BF16),
                        pltpu.SemaphoreType.DMA((2,))],
        compiler_params=_cparams(("arbitrary", "arbitrary")),
        name="mla_attn",
    )(q_a, q_a, kcat.reshape(batch, seq, 2 * LANES), qn_g, wq_nope, wq_pe, w_uk, w_uv,
      cos_t, sin_a, sin_b, cos_t, sin_a, sin_b)


def _mix_ffn_kernel(*refs, n_act, has_pre):
    x_ref = refs[0]
    act_refs = refs[1:1 + n_act]
    k = 1 + n_act
    pre_ref = refs[k] if has_pre else None
    k += int(has_pre)
    wo_refs = refs[k:k + n_act]
    k += n_act
    g1_ref, b1_ref, g2_ref, b2_ref, w1_ref, w2_ref, out_ref = refs[k:k + 7]

    tm = x_ref.shape[0]
    rb = min(FFN_ROW_BLOCK, tm)
    blocks = [slice(r0, r0 + rb) for r0 in range(0, tm, rb)]
    fs = []
    for rows in blocks:
        f = None
        for idx in range(n_act):
            a = act_refs[idx][rows, :]
            if has_pre and idx == 0:
                a = _dot(a.astype(BF16), pre_ref[...])
            part = _dot(a.astype(BF16), wo_refs[idx][...])
            f = part if f is None else f + part
        fs.append(f)
    x1s = [_layer_norm(DN_ALPHA * x_ref[rows, :] + f, g1_ref[...], b1_ref[...]) for rows, f in zip(blocks, fs)]
    x1bs = [x1.astype(BF16) for x1 in x1s]
    accs = [None] * len(blocks)
    for c in range(FFN_HIDDEN // FFN_HC):
        cols = slice(c * FFN_HC, (c + 1) * FFN_HC)
        for i, x1b in enumerate(x1bs):
            h = jnp.maximum(_dot(x1b, w1_ref[:, cols]), 0.0)
            part = _dot((h * h).astype(BF16), w2_ref[cols, :])
            accs[i] = part if accs[i] is None else accs[i] + part
    for rows, x1, acc in zip(blocks, x1s, accs):
        out_ref[rows, :] = _layer_norm(DN_ALPHA * x1 + acc, g2_ref[...], b2_ref[...])


def _mix_ffn(x2d, acts, pre_w, wos, ln1_g, ln1_b, ln2_g, ln2_b, w1, w2, name):
    n_tok = x2d.shape[0]
    tm = min(FFN_TM, n_tok)
    row = lambda w: pl.BlockSpec((tm, w), lambda i: (i, 0))
    has_pre = pre_w is not None
    in_specs = [row(D_MODEL)] + [row(a.shape[1]) for a in acts]
    args = [x2d] + list(acts)
    if has_pre:
        in_specs.append(_const_spec(pre_w.shape))
        args.append(pre_w)
    in_specs += [_const_spec(w.shape) for w in wos]
    args += list(wos)
    in_specs += [_const_spec((1, D_MODEL))] * 4 + [_const_spec(w1.shape), _const_spec(w2.shape)]
    args += [ln1_g, ln1_b, ln2_g, ln2_b, w1, w2]
    return pl.pallas_call(
        functools.partial(_mix_ffn_kernel, n_act=len(acts), has_pre=has_pre),
        grid=(n_tok // tm,),
        in_specs=in_specs,
        out_specs=row(D_MODEL),
        out_shape=jax.ShapeDtypeStruct((n_tok, D_MODEL), F32),
        compiler_params=_cparams(("arbitrary",)),
        name=name,
    )(*args)


def _even_inproj_sample_kernel(x_ref, w_ref, kvn_ref, gmg_ref, gmb_ref, ws0_ref, bs0_ref,
                               cos_ref, sina_ref, sinb_ref, qn_ref, wqn_ref, wqp_ref, wuk_ref,
                               ckv_ref, kpe_ref, gate_ref, vn_ref, qlat_ref, qpe_ref):
    z = _dot(x_ref[...].astype(BF16), w_ref[...])
    o_c, o_u, o_v, o_k = Q_RANK, Q_RANK + KV_RANK, Q_RANK + KV_RANK + GM_WIDTH, Q_RANK + KV_RANK + 2 * GM_WIDTH
    cos_t, sin_a, sin_b = cos_ref[...], sina_ref[...], sinb_ref[...]
    c_n = _rms_norm(z[:, o_c:o_u], kvn_ref[...])
    ckv_ref[...] = c_n
    kp = _rope128(z[:, o_k:o_k + LANES], cos_t, sin_a, sin_b)
    kpe_ref[...] = kp[:, :ROPE_DIM]
    gu = jax.nn.gelu(z[:, o_u:o_v])
    v_n = _layer_norm(jax.nn.gelu(z[:, o_v:o_k]), gmg_ref[...], gmb_ref[...])
    vn_ref[...] = v_n
    mixed = ws0_ref[...].astype(F32) * v_n.astype(BF16).astype(F32) + bs0_ref[...]
    gate_ref[...] = (gu * mixed).astype(BF16)
    lat, pes = _queries(z[:, :o_c], qn_ref[...], wqn_ref[...], wqp_ref[...], wuk_ref,
                        cos_t, sin_a, sin_b, compact=False)
    qlat_ref[...] = lat
    qpe_ref[...] = jnp.concatenate(pes, axis=1)


def _even_inproj_sample(xs, w_in_p, kv_norm, gm_g, gm_b, ws0, bs0, cos_t, sin_a, sin_b,
                        qn_g, wq_nope, wq_pe, w_uk):
    n = xs.shape[0]
    args = (xs, w_in_p, kv_norm, gm_g, gm_b, ws0, bs0, cos_t, sin_a, sin_b, qn_g, wq_nope, wq_pe, w_uk)
    full = lambda shape: pl.BlockSpec(shape, lambda i: (0,) * len(shape))
    widths = [(KV_RANK, F32), (ROPE_DIM, F32), (GM_WIDTH, BF16), (GM_WIDTH, F32),
              (MLA_HEADS * LANES, F32), (MLA_HEADS * LANES, F32)]
    return pl.pallas_call(
        _even_inproj_sample_kernel,
        grid=(1,),
        in_specs=[full(a.shape) for a in args],
        out_specs=[full((n, w)) for w, _ in widths],
        out_shape=[jax.ShapeDtypeStruct((n, w), dt) for w, dt in widths],
        compiler_params=_cparams(("arbitrary",)),
        name="even_inproj_sample",
    )(*args)


def _decode_attn_kernel(pt_ref, qlat_ref, qpe_ref, ckvs_ref, kpes_ref, ckv_hbm, kpe_hbm, out_ref,
                        ckv_buf, kpe_buf, m_buf, sem, *, n_pg, n_groups):
    b = pl.program_id(0)
    n_b = pl.num_programs(0)
    assert n_groups % DEC_SLOTS == 0 and DEC_AHEAD + 2 <= DEC_SLOTS

    def group_copies(seq, grp):
        slot = grp % DEC_SLOTS
        cps = []
        for k in range(n_pg):
            page = pt_ref[seq, grp * n_pg + k]
            keys = pl.ds(k * PAGE_SIZE, PAGE_SIZE)
            cps.append(pltpu.make_async_copy(ckv_hbm.at[page], ckv_buf.at[slot, keys, :], sem.at[0, slot]))
            cps.append(pltpu.make_async_copy(kpe_hbm.at[page], kpe_buf.at[slot, :, keys], sem.at[1, slot]))
        return cps

    @pl.when(b == 0)
    def _():
        for g in range(DEC_AHEAD):
            for cp in group_copies(0, g):
                cp.start()

    heads, span = MLA_HEADS, DEC_SPAN_PAGES * PAGE_SIZE
    half_span = span // 2
    n_span = n_pg // DEC_SPAN_PAGES
    qlat = qlat_ref[...]
    zero = jnp.zeros_like(qlat)
    q2 = jnp.concatenate([jnp.concatenate([qlat, zero], axis=1),
                          jnp.concatenate([zero, qlat], axis=1)], axis=0).astype(BF16)
    qpe = qpe_ref[...][:, :ROPE_DIM].astype(BF16)

    def scores(slot):
        s_pe = _dot(qpe, kpe_buf[slot].astype(BF16))
        pieces = []
        for j in range(n_span):
            m_buf[slot, j] = jnp.concatenate(
                [ckv_buf[slot, pl.ds(j * span, half_span), :],
                 ckv_buf[slot, pl.ds(j * span + half_span, half_span), :]], axis=1).astype(BF16)
            s2 = _dot_nt(q2, m_buf[slot, j])
            pieces += [s2[:heads] + s_pe[:, j * span:j * span + half_span],
                       s2[heads:] + s_pe[:, j * span + half_span:(j + 1) * span]]
        return jnp.concatenate(pieces, axis=1) * ATTN_SCALE

    def values(p, grp):
        slot = grp % DEC_SLOTS
        o2 = None
        for j in range(n_span):
            lhs = jnp.concatenate([p[:, j * span:j * span + half_span],
                                   p[:, j * span + half_span:(j + 1) * span]], axis=0).astype(BF16)
            part = _dot(lhs, m_buf[slot, j])
            o2 = part if o2 is None else o2 + part
        return o2[:heads, :KV_RANK] + o2[heads:, KV_RANK:]

    state = dict(m_run=None, m=None, l=None, acc=None)
    raw = {}
    probs = {}

    def softmax(k):
        s = raw.pop(k)
        m_cur = jnp.max(s, axis=-1, keepdims=True)
        m_new = m_cur if state["m_run"] is None else jnp.maximum(state["m_run"], m_cur)
        state["m_run"] = m_new
        probs[k] = (jnp.exp(s - m_new), m_new)

    def fold(k):
        p, m_k = probs.pop(k)
        pv = values(p, k)
        p_sum = jnp.sum(p, axis=-1, keepdims=True)
        if state["m"] is None:
            state["l"], state["acc"] = p_sum, pv
        else:
            alpha = jnp.exp(state["m"] - m_k)
            state["l"], state["acc"] = alpha * state["l"] + p_sum, alpha * state["acc"] + pv
        state["m"] = m_k

    for g in range(n_groups):
        if g + DEC_AHEAD < n_groups:
            for cp in group_copies(b, g + DEC_AHEAD):
                cp.start()
        else:
            @pl.when(b + 1 < n_b)
            def _():
                for cp in group_copies(b + 1, g + DEC_AHEAD - n_groups):
                    cp.start()
        for cp in group_copies(b, g):
            cp.wait()
        raw[g] = scores(g % DEC_SLOTS)
        if g >= 1:
            softmax(g - 1)
        if g >= 2:
            fold(g - 2)
    softmax(n_groups - 1)
    fold(n_groups - 2)
    fold(n_groups - 1)
    m, l, acc = state["m"], state["l"], state["acc"]

    r = lambda a: a.astype(BF16).astype(F32)
    kv = r(ckvs_ref[...])
    s_self = (jnp.sum(r(qlat_ref[...]) * kv, axis=-1, keepdims=True)
              + jnp.sum(r(qpe_ref[...]) * r(kpes_ref[...]), axis=-1, keepdims=True)) * ATTN_SCALE
    m_n = jnp.maximum(m, s_self)
    a = jnp.exp(m - m_n)
    p_self = jnp.exp(s_self - m_n)
    out_ref[...] = (a * acc + r(p_self) * kv) / (a * l + p_self)


def _decode_attn(page_table, qlat2, qpe2, ckv_s3, kpe_s3, cache_ckv_e, cache_kpe_t):
    dec_batch, n_pages = page_table.shape
    n_pg = DEC_PG
    qspec = pl.BlockSpec((MLA_HEADS, LANES), lambda b, pt: (b, 0))
    self_spec = pl.BlockSpec((None, 1, LANES), lambda b, pt: (b, 0, 0))
    hbm = pl.BlockSpec(memory_space=pl.ANY)
    grid_spec = pltpu.PrefetchScalarGridSpec(
        num_scalar_prefetch=1,
        grid=(dec_batch,),
        in_specs=[qspec, qspec, self_spec, self_spec, hbm, hbm],
        out_specs=qspec,
        scratch_shapes=[pltpu.VMEM((DEC_SLOTS, n_pg * PAGE_SIZE, KV_RANK), F32),
                        pltpu.VMEM((DEC_SLOTS, ROPE_DIM, n_pg * PAGE_SIZE), F32),
                        pltpu.VMEM((DEC_SLOTS, n_pg // DEC_SPAN_PAGES, DEC_SPAN_PAGES // 2 * PAGE_SIZE, 2 * KV_RANK),
                                   BF16),
                        pltpu.SemaphoreType.DMA((2, DEC_SLOTS))],
    )
    return pl.pallas_call(
        functools.partial(_decode_attn_kernel, n_pg=n_pg, n_groups=n_pages // n_pg),
        grid_spec=grid_spec,
        out_shape=jax.ShapeDtypeStruct((dec_batch * MLA_HEADS, KV_RANK), F32),
        compiler_params=_cparams(("arbitrary",)),
        name="decode_attn",
    )(page_table, qlat2, qpe2, ckv_s3, kpe_s3, cache_ckv_e, cache_kpe_t)


def _s5_readout(hcat_blocks, u, cw_ref, d_ref, wglu_ref):
    y = jnp.concatenate([_dot(hcat_blocks[k], cw_ref[k]) for k in range(S5_NB)], axis=1)
    y = jax.nn.gelu(y + d_ref[...] * u)
    return y * jax.nn.sigmoid(_dot(y.astype(BF16), wglu_ref[...]))


def _s5_prompt_kernel(x_ref, win_ref, bw_ref, a_ref, cw_ref, d_ref, wglu_ref,
                      yg_ref, hre_ref, him_ref, u_s, y_s, bu_s, h_s):
    n_b, seg, _ = x_ref.shape
    half = S5_SLABS // 2
    sub = S5_SUB
    n_sub = seg // sub
    rows_sub = sub * n_b
    q_slabs = S5_WIDTH // LANES

    @pl.when(pl.program_id(0) == 0)
    def _():
        h_s[...] = jnp.zeros(h_s.shape, F32)

    def project(s):
        rows = slice(s * rows_sub, (s + 1) * rows_sub)
        xs = jnp.concatenate([x_ref[b, s * sub:(s + 1) * sub, :] for b in range(n_b)], axis=0)
        u = _dot(xs.astype(BF16), win_ref[...])
        for b in range(n_b):
            for q in range(q_slabs):
                u_s[s, q, pl.ds(b, sub, stride=n_b), :] = u[b * sub:(b + 1) * sub, q * LANES:(q + 1) * LANES]
        for k in range(S5_NB):
            r = _dot(u_s[s, k].astype(BF16), bw_ref[k])
            for q in range(S5_BLK_SLABS):
                bu_s[S5_BLK_SLABS * k + q, rows, :] = r[:, q * LANES:(q + 1) * LANES]
                bu_s[half + S5_BLK_SLABS * k + q, rows, :] = r[:, (S5_BLK_SLABS + q) * LANES:
                                                                (S5_BLK_SLABS + q + 1) * LANES]

    def recur(s, h):
        for t in range(s * sub, (s + 1) * sub):
            rows = slice(t * n_b, (t + 1) * n_b)
            hs = bu_s[:, rows, :]
            ar, ai = a_ref[:half], a_ref[half:]
            hr, hi = h[:half], h[half:]
            h = jnp.concatenate([ar * hr - ai * hi + hs[:half], ar * hi + ai * hr + hs[half:]], axis=0)
            bu_s[:, rows, :] = h
        return h

    def read_out(s):
        rows = slice(s * rows_sub, (s + 1) * rows_sub)
        blocks = [jnp.concatenate([bu_s[S5_BLK_SLABS * k + q, rows, :] for q in range(S5_BLK_SLABS)]
                                  + [bu_s[half + S5_BLK_SLABS * k + q, rows, :] for q in range(S5_BLK_SLABS)],
                                  axis=1).astype(BF16)
                  for k in range(S5_NB)]
        u_t = jnp.concatenate([u_s[s, q] for q in range(q_slabs)], axis=1)
        y = _s5_readout(blocks, u_t, cw_ref, d_ref, wglu_ref)
        for q in range(q_slabs):
            y_s[s, q] = y[:, q * LANES:(q + 1) * LANES]
        for b in range(n_b):
            yg_ref[b, s * sub:(s + 1) * sub, :] = jnp.concatenate(
                [y_s[s, q, pl.ds(b, sub, stride=n_b), :] for q in range(q_slabs)], axis=1).astype(BF16)

    h = h_s[...]
    project(0)
    for s in range(n_sub):
        if s + 1 < n_sub:
            project(s + 1)
        h = recur(s, h)
        if s >= 1:
            read_out(s - 1)
    read_out(n_sub - 1)
    h_s[...] = h
    hre_ref[...] = jnp.concatenate([h[s] for s in range(half)], axis=1)
    him_ref[...] = jnp.concatenate([h[half + s] for s in range(half)], axis=1)


def _s5_prompt(x3, w_in, bw, a_b, cw, d_row, w_glu):
    batch, seq, _ = x3.shape
    seg = S5_L
    slab_rows = (seg // S5_SUB, S5_WIDTH // LANES, S5_SUB * batch, LANES)
    return pl.pallas_call(
        _s5_prompt_kernel,
        grid=(seq // seg,),
        in_specs=[pl.BlockSpec((batch, seg, D_MODEL), lambda c: (0, c, 0)),
                  _const_spec(w_in.shape), _const_spec(bw.shape), _const_spec(a_b.shape),
                  _const_spec(cw.shape), _const_spec(d_row.shape), _const_spec(w_glu.shape)],
        out_specs=[pl.BlockSpec((batch, seg, S5_WIDTH), lambda c: (0, c, 0)),
                   pl.BlockSpec((batch, S5_STATES), lambda c: (0, 0)),
                   pl.BlockSpec((batch, S5_STATES), lambda c: (0, 0))],
        out_shape=[jax.ShapeDtypeStruct((batch, seq, S5_WIDTH), BF16),
                   jax.ShapeDtypeStruct((batch, S5_STATES), F32),
                   jax.ShapeDtypeStruct((batch, S5_STATES), F32)],
        scratch_shapes=[pltpu.VMEM(slab_rows, F32), pltpu.VMEM(slab_rows, F32),
                        pltpu.VMEM((S5_SLABS, seg * batch, LANES), F32),
                        pltpu.VMEM((S5_SLABS, batch, LANES), F32)],
        compiler_params=_cparams(("arbitrary",)),
        name="s5_prompt",
    )(x3, w_in, bw, a_b, cw, d_row, w_glu)


def _s5_sample_kernel(x_ref, h0r_ref, h0i_ref, win_ref, bw_ref, ar_ref, ai_ref, cw_ref, d_ref, wglu_ref,
                      yg_ref, hre_ref, him_ref):
    u = _dot(x_ref[...].astype(BF16), win_ref[...])
    ub = u.astype(BF16)
    bu = [_dot(ub[:, k * LANES:(k + 1) * LANES], bw_ref[k]) for k in range(S5_NB)]
    w = S5_KB * S5_STATE
    bu_r = jnp.concatenate([r[:, :w] for r in bu], axis=1)
    bu_i = jnp.concatenate([r[:, w:] for r in bu], axis=1)
    ar, ai, h0r, h0i = ar_ref[...], ai_ref[...], h0r_ref[...], h0i_ref[...]
    hr = bu_r + (ar * h0r - ai * h0i)
    hi = bu_i + (ar * h0i + ai * h0r)
    hre_ref[...] = hr
    him_ref[...] = hi
    blocks = [jnp.concatenate([hr[:, k * w:(k + 1) * w], hi[:, k * w:(k + 1) * w]], axis=1).astype(BF16)
              for k in range(S5_NB)]
    yg_ref[...] = _s5_readout(blocks, u, cw_ref, d_ref, wglu_ref).astype(BF16)


def _s5_sample(xs, h0r, h0i, w_in, bw, a_r, a_i, cw, d_row, w_glu):
    n = xs.shape[0]
    args = (xs, h0r, h0i, w_in, bw, a_r, a_i, cw, d_row, w_glu)
    full = lambda shape: pl.BlockSpec(shape, lambda i: (0,) * len(shape))
    return pl.pallas_call(
        _s5_sample_kernel,
        grid=(1,),
        in_specs=[full(a.shape) for a in args],
        out_specs=[full((n, S5_WIDTH)), full((n, S5_STATES)), full((n, S5_STATES))],
        out_shape=[jax.ShapeDtypeStruct((n, S5_WIDTH), BF16),
                   jax.ShapeDtypeStruct((n, S5_STATES), F32),
                   jax.ShapeDtypeStruct((n, S5_STATES), F32)],
        compiler_params=_cparams(("arbitrary",)),
        name="s5_sample",
    )(*args)


def _rope_tables(pos):
    freqs = ROPE_THETA ** (-jnp.arange(ROPE_HALF, dtype=F32) / ROPE_HALF)
    ang = pos.astype(F32)[:, None] * freqs
    cos, sin = jnp.cos(ang), jnp.sin(ang)
    zero = jnp.zeros_like(cos)
    rep = lambda a, b: jnp.tile(jnp.concatenate([a, b], axis=1), (1, LANES // ROPE_DIM))
    return rep(cos, cos), rep(zero, sin), rep(-sin, zero)


def _block_diag(blocks):
    n, r, c = blocks.shape
    eye = jnp.eye(n, dtype=blocks.dtype)
    return jnp.einsum('nrc,nm->nrmc', blocks, eye).reshape(n * r, n * c)


def _even_params(w_in, q_norm, w_q_b, kv_norm, w_kv_b, gm_g, gm_b, gm_w_s, gm_b_s, w_out):
    i1, i2, i3, i4 = Q_RANK, Q_RANK + KV_RANK, Q_RANK + KV_RANK + ROPE_DIM, Q_RANK + KV_RANK + ROPE_DIM + GM_WIDTH
    w_in_p = jnp.concatenate([w_in[:, :i2], w_in[:, i3:]] + [w_in[:, i2:i3]] * (LANES // ROPE_DIM),
                             axis=1).astype(BF16)
    wq_nope = w_q_b[:, :, :NOPE_DIM].reshape(Q_RANK, MLA_HEADS * NOPE_DIM).astype(BF16)
    pe = w_q_b[:, :, NOPE_DIM:]
    pe_sw = jnp.concatenate([pe[:, :, ROPE_HALF:], pe[:, :, :ROPE_HALF]], axis=2)
    lane_pad = jnp.zeros((Q_RANK, MLA_HEADS, LANES - ROPE_DIM), F32)
    wq_pe = jnp.concatenate(
        [jnp.concatenate([w, lane_pad], axis=2).reshape(Q_RANK, MLA_HEADS * LANES) for w in (pe, pe_sw)],
        axis=1).astype(BF16)
    wq_pe_c = jnp.concatenate([w.reshape(Q_RANK, MLA_HEADS * ROPE_DIM) for w in (pe, pe_sw)],
                              axis=1).astype(BF16)
    uk = jnp.transpose(w_kv_b[:, :, :NOPE_DIM], (1, 2, 0))
    uv = jnp.transpose(w_kv_b[:, :, NOPE_DIM:], (1, 0, 2))
    pairs = range(MLA_HEADS // 2)
    w_uk = jnp.stack([_block_diag(uk[2 * pr:2 * pr + 2]) for pr in pairs]).astype(BF16)
    w_uv_p = jnp.stack([_block_diag(uv[2 * pr:2 * pr + 2]) for pr in pairs]).astype(BF16)
    w_uv = _block_diag(uv).astype(BF16)
    causal = jnp.tril(jnp.ones((GM_CHUNK, GM_CHUNK), dtype=bool))
    ws_m = jnp.where(causal[None], gm_w_s, 0).astype(BF16)
    bs_full = jnp.repeat(gm_b_s.T, GM_HEAD_DIM, axis=1)
    ws0 = jnp.repeat(gm_w_s[:, 0, 0], GM_HEAD_DIM)[None, :].astype(BF16)
    bs0 = bs_full[:1]
    n_attn = MLA_HEADS * V_DIM
    return dict(w_in_p=w_in_p, qn=q_norm[None, :], wq_nope=wq_nope, wq_pe=wq_pe, wq_pe_c=wq_pe_c, w_uk=w_uk, w_uv=w_uv, w_uv_p=w_uv_p,
                kvn=kv_norm[None, :], gm_g=gm_g[None, :], gm_b=gm_b[None, :], ws_m=ws_m, bs_full=bs_full,
                ws0=ws0, bs0=bs0, wo_attn=w_out[:n_attn].astype(BF16), wo_gate=w_out[n_attn:].astype(BF16))


def _s5_params(w_in, a_re, a_im, b_re, b_im, c_re, c_im, d, log_dt, w_glu, w_out, batch):
    dt = jnp.exp(log_dt)[:, None]
    ld_r, ld_i = a_re * dt, a_im * dt
    mag = jnp.exp(ld_r)
    ab_r, ab_i = mag * jnp.cos(ld_i), mag * jnp.sin(ld_i)
    den = a_re * a_re + a_im * a_im
    cr = ((ab_r - 1.0) * a_re + ab_i * a_im) / den
    ci = (ab_i * a_re - (ab_r - 1.0) * a_im) / den
    bb_r = cr[..., None] * b_re - ci[..., None] * b_im
    bb_i = cr[..., None] * b_im + ci[..., None] * b_re

    def in_blocks(bb):
        return jnp.stack([_block_diag(jnp.transpose(bb[k * S5_KB:(k + 1) * S5_KB], (0, 2, 1)))
                          for k in range(S5_NB)])

    def out_blocks(cc):
        return jnp.stack([_block_diag(jnp.transpose(cc[k * S5_KB:(k + 1) * S5_KB], (0, 2, 1)))
                          for k in range(S5_NB)])

    bw = jnp.concatenate([in_blocks(bb_r), in_blocks(bb_i)], axis=2).astype(BF16)
    cw = jnp.concatenate([out_blocks(c_re), out_blocks(-c_im)], axis=1).astype(BF16)
    a_r, a_i = ab_r.reshape(1, S5_STATES), ab_i.reshape(1, S5_STATES)
    half = S5_SLABS // 2
    a_b = jnp.concatenate([jnp.broadcast_to(a_r.reshape(half, 1, LANES), (half, batch, LANES)),
                           jnp.broadcast_to(a_i.reshape(half, 1, LANES), (half, batch, LANES))], axis=0)
    return dict(w_in=w_in.astype(BF16), bw=bw, cw=cw, a_r=a_r, a_i=a_i, a_b=a_b,
                d=d.reshape(1, S5_WIDTH), w_glu=w_glu.astype(BF16), w_out=w_out.astype(BF16))


def kernel(x_prompt, x_sample, cache_ckv, cache_kpe, state_s5_re, state_s5_im, page_table, w_in_even, mla_q_norm, mla_w_q_b, mla_kv_norm, mla_w_kv_b, gm_norm_g, gm_norm_b, gm_w_s, gm_b_s, w_out_even, w_in_odd, s5_a_re, s5_a_im, s5_b_re, s5_b_im, s5_c_re, s5_c_im, s5_d, s5_log_dt, s5_w_glu, w_out_odd, ln_mix_g, ln_mix_b, ln_ffn_g, ln_ffn_b, ffn_w1, ffn_w2):
    batch, seq, _ = x_prompt.shape
    dec_batch, dec_seq, _ = x_sample.shape
    assert dec_seq == 1 and seq % INPROJ_TM == 0 and seq % ATTN_TQ == 0 and seq % S5_L == 0
    assert (seq // ATTN_TQ) % 2 == 0
    assert (batch * seq) % FFN_TM == 0 and page_table.shape[1] % DEC_PG == 0
    assert batch == SUBLANES

    xp = x_prompt.reshape(batch * seq, D_MODEL)
    xs = x_sample.reshape(dec_batch, D_MODEL)
    tabs_p = _rope_tables(jnp.arange(seq, dtype=jnp.int32))
    tabs_s = _rope_tables(PAST_LEN + jnp.arange(dec_seq, dtype=jnp.int32))

    outs = {k: [] for k in ("ckv_p", "kpe_p", "ckv_s", "kpe_s", "gmv_p", "gmv_s",
                            "s5re_p", "s5im_p", "s5re_s", "s5im_s")}
    for layer in range(DEPTH):
        ln = (ln_mix_g[layer][None, :], ln_mix_b[layer][None, :],
              ln_ffn_g[layer][None, :], ln_ffn_b[layer][None, :],
              ffn_w1[layer].astype(BF16), ffn_w2[layer].astype(BF16))
        if layer % 2 == 0:
            e = layer // 2
            p = _even_params(w_in_even[e], mla_q_norm[e], mla_w_q_b[e], mla_kv_norm[e], mla_w_kv_b[e],
                             gm_norm_g[e], gm_norm_b[e], gm_w_s[e], gm_b_s[e], w_out_even[e])
            q_a, ckv, kpe, kcat, gate, gmv = _even_inproj(
                xp, p["w_in_p"], p["kvn"], p["gm_g"], p["gm_b"], p["ws_m"], p["bs_full"], *tabs_p, batch, seq)
            attn = _mla_attn(q_a, kcat, p["qn"], p["wq_nope"], p["wq_pe_c"], p["w_uk"], p["w_uv_p"], *tabs_p,
                             batch, seq)
            xp = _mix_ffn(xp, [attn, gate], None, [p["wo_attn"], p["wo_gate"]], *ln, name="even_ffn_prompt")
            outs["ckv_p"].append(ckv.reshape(batch, seq, KV_RANK))
            outs["kpe_p"].append(kpe.reshape(batch, seq, ROPE_DIM))
            outs["gmv_p"].append(gmv)
            ckv_s, kpe_s, gate_s, vn_s, qlat_s, qpe_s = _even_inproj_sample(
                xs, p["w_in_p"], p["kvn"], p["gm_g"], p["gm_b"], p["ws0"], p["bs0"], *tabs_s,
                p["qn"], p["wq_nope"], p["wq_pe"], p["w_uk"])
            kpe_pad = jnp.concatenate([kpe_s, jnp.zeros((dec_batch, LANES - ROPE_DIM), F32)], axis=1)
            o_lat = _decode_attn(page_table,
                                 qlat_s.reshape(dec_batch * MLA_HEADS, LANES),
                                 qpe_s.reshape(dec_batch * MLA_HEADS, LANES),
                                 ckv_s.reshape(dec_batch, 1, KV_RANK), kpe_pad.reshape(dec_batch, 1, LANES),
                                 cache_ckv[e], jnp.swapaxes(cache_kpe[e], 1, 2))
            xs = _mix_ffn(xs, [o_lat.reshape(dec_batch, MLA_HEADS * KV_RANK), gate_s], p["w_uv"],
                          [p["wo_attn"], p["wo_gate"]], *ln, name="even_ffn_sample")
            outs["ckv_s"].append(ckv_s.reshape(dec_batch, 1, KV_RANK))
            outs["kpe_s"].append(kpe_s.reshape(dec_batch, 1, ROPE_DIM))
            outs["gmv_s"].append(vn_s.reshape(dec_batch, 1, GM_WIDTH))
        else:
            o = layer // 2
            p = _s5_params(w_in_odd[o], s5_a_re[o], s5_a_im[o], s5_b_re[o], s5_b_im[o], s5_c_re[o],
                           s5_c_im[o], s5_d[o], s5_log_dt[o], s5_w_glu[o], w_out_odd[o], batch)
            yg, hre, him = _s5_prompt(xp.reshape(batch, seq, D_MODEL), p["w_in"], p["bw"], p["a_b"],
                                      p["cw"], p["d"], p["w_glu"])
            xp = _mix_ffn(xp, [yg.reshape(batch * seq, S5_WIDTH)], None, [p["w_out"]], *ln,
                          name="odd_ffn_prompt")
            outs["s5re_p"].append(hre.reshape(batch, S5_GROUPS, S5_STATE))
            outs["s5im_p"].append(him.reshape(batch, S5_GROUPS, S5_STATE))
            yg_s, hre_s, him_s = _s5_sample(
                xs, state_s5_re[o].reshape(dec_batch, S5_STATES), state_s5_im[o].reshape(dec_batch, S5_STATES),
                p["w_in"], p["bw"], p["a_r"], p["a_i"], p["cw"], p["d"], p["w_glu"])
            xs = _mix_ffn(xs, [yg_s], None, [p["w_out"]], *ln, name="odd_ffn_sample")
            outs["s5re_s"].append(hre_s.reshape(dec_batch, S5_GROUPS, S5_STATE))
            outs["s5im_s"].append(him_s.reshape(dec_batch, S5_GROUPS, S5_STATE))

    st = jnp.stack
    return (xp.reshape(batch, seq, D_MODEL), xs.reshape(dec_batch, dec_seq, D_MODEL),
            st(outs["ckv_p"]), st(outs["kpe_p"]), st(outs["ckv_s"]), st(outs["kpe_s"]),
            st(outs["gmv_p"]), st(outs["gmv_s"]),
            st(outs["s5re_p"]), st(outs["s5im_p"]), st(outs["s5re_s"]), st(outs["s5im_s"]))
```

```python
import functools
import math

import jax
import jax.numpy as jnp
from jax import lax
from jax.experimental import pallas as pl
from jax.experimental.pallas import tpu as pltpu

F32 = jnp.float32
BF16 = jnp.bfloat16

D_MODEL = 1024
DEPTH = 2
PAST_LEN = 16384
PAGE_SIZE = 128
MLA_HEADS = 8
NOPE_DIM = 64
ROPE_DIM = 32
ROPE_HALF = ROPE_DIM // 2
V_DIM = 64
Q_RANK = 256
KV_RANK = 128
ROPE_THETA = 10000.0
ATTN_SCALE = 1.0 / math.sqrt(NOPE_DIM + ROPE_DIM)
ATTN_SCALE_LOG2 = ATTN_SCALE * math.log2(math.e)
GM_HEADS = 8
GM_HEAD_DIM = 64
GM_WIDTH = GM_HEADS * GM_HEAD_DIM
GM_CHUNK = 128
S5_GROUP_DIM = 16
S5_GROUPS = 32
S5_STATE = 64
S5_WIDTH = S5_GROUPS * S5_GROUP_DIM
S5_STATES = S5_GROUPS * S5_STATE
FFN_HIDDEN = 4 * D_MODEL
DN_ALPHA = (2 * DEPTH) ** 0.25
LN_EPS = 1e-5
RMS_EPS = 1e-6

LANES = 128
SUBLANES = 8
VMEM_LIMIT_BYTES = 56 * 1024 * 1024

INPROJ_TM = 512
INPROJ_ROW_BLOCK = 256
ATTN_TQ = 256
ATTN_ROW_BLOCK = 256
FFN_TM = 1024
FFN_HC = 1024
FFN_ROW_BLOCK = 256
S5_L = 128
S5_SUB = 32
DEC_PG = 32
DEC_SLOTS = 4
DEC_AHEAD = 2
DEC_SPAN_PAGES = 4
S5_KB = 8
S5_NB = S5_GROUPS // S5_KB
S5_SLABS = 2 * S5_STATES // LANES
S5_BLK_SLABS = S5_KB * S5_STATE // LANES


def _cparams(sem):
    return pltpu.CompilerParams(dimension_semantics=sem, vmem_limit_bytes=VMEM_LIMIT_BYTES)


def _const_spec(shape):
    n = len(shape)
    return pl.BlockSpec(shape, lambda *_: (0,) * n, pipeline_mode=pl.Buffered(1))


def _layer_norm(x, g, b):
    mu = jnp.mean(x, axis=-1, keepdims=True)
    xc = x - mu
    var = jnp.mean(xc * xc, axis=-1, keepdims=True)
    return xc * lax.rsqrt(var + LN_EPS) * g + b


def _rms_norm(x, g):
    return x * lax.rsqrt(jnp.mean(x * x, axis=-1, keepdims=True) + RMS_EPS) * g


def _rope128(x, cos_t, sin_a, sin_b):
    return (x * cos_t + pltpu.roll(x, ROPE_HALF, 1) * sin_a
            + pltpu.roll(x, LANES - ROPE_HALF, 1) * sin_b)


def _dot(a, b):
    return jnp.dot(a, b, preferred_element_type=F32)


def _dot_nt(a, b):
    return lax.dot_general(a, b, (((1,), (1,)), ((), ())), preferred_element_type=F32)


def _even_inproj_kernel(x_ref, w_ref, kvn_ref, gmg_ref, gmb_ref, ws_ref, bs_ref,
                        cos_ref, sina_ref, sinb_ref,
                        qa_ref, ckv_ref, kpe_ref, kcat_ref, gate_ref, gmv_ref):
    tm = x_ref.shape[0]
    o_c, o_u, o_v, o_k = Q_RANK, Q_RANK + KV_RANK, Q_RANK + KV_RANK + GM_WIDTH, Q_RANK + KV_RANK + 2 * GM_WIDTH
    rb = min(INPROJ_ROW_BLOCK, tm)
    blocks = [slice(r0, r0 + rb) for r0 in range(0, tm, rb)]
    zs = [_dot(x_ref[rows, :].astype(BF16), w_ref[...]) for rows in blocks]

    gus, vns = [], []
    for rows, z in zip(blocks, zs):
        qa_ref[rows, :] = z[:, :o_c]
        c_n = _rms_norm(z[:, o_c:o_u], kvn_ref[...])
        ckv_ref[rows, :] = c_n
        kp = _rope128(z[:, o_k:o_k + LANES], cos_ref[rows, :], sina_ref[rows, :], sinb_ref[rows, :])
        kpe_ref[rows, :] = kp[:, :ROPE_DIM]
        kcat_ref[rows, :] = jnp.concatenate([c_n, kp], axis=1).astype(BF16)
        gus.append(jax.nn.gelu(z[:, o_u:o_v]))
        vns.append(_layer_norm(jax.nn.gelu(z[:, o_v:o_k]), gmg_ref[...], gmb_ref[...]))

    gmv_ref[0] = vns[-1][rb - GM_CHUNK:, :]

    lane = lax.broadcasted_iota(jnp.int32, (GM_CHUNK, LANES), 1)
    lo = lane < GM_HEAD_DIM
    bs = bs_ref[...]
    for rows, gu, v_n in zip(blocks, gus, vns):
        for ci in range(rb // GM_CHUNK):
            chunk = slice(ci * GM_CHUNK, (ci + 1) * GM_CHUNK)
            pieces = []
            for pr in range(GM_WIDTH // LANES):
                r = v_n[chunk, pr * LANES:(pr + 1) * LANES]
                m0 = _dot(ws_ref[2 * pr], jnp.where(lo, r, 0.0).astype(BF16))
                m1 = _dot(ws_ref[2 * pr + 1], jnp.where(lo, 0.0, r).astype(BF16))
                pieces.append(m0 + m1)
            mixed = jnp.concatenate(pieces, axis=1) + bs
            gate_ref[pl.ds(rows.start + ci * GM_CHUNK, GM_CHUNK), :] = (gu[chunk, :] * mixed).astype(BF16)


def _even_inproj(x2d, w_in_p, kv_norm, gm_g, gm_b, ws_m, bs_full, cos_t, sin_a, sin_b, batch, seq):
    n_tok = x2d.shape[0]
    tm = INPROJ_TM
    tiles_per_seq = seq // tm
    row = lambda w: pl.BlockSpec((tm, w), lambda i: (i, 0))
    tab = pl.BlockSpec((tm, LANES), lambda i: (i % tiles_per_seq, 0))
    n_in = w_in_p.shape[1]
    return pl.pallas_call(
        _even_inproj_kernel,
        grid=(n_tok // tm,),
        in_specs=[row(D_MODEL), _const_spec((D_MODEL, n_in)), _const_spec((1, KV_RANK)),
                  _const_spec((1, GM_WIDTH)), _const_spec((1, GM_WIDTH)),
                  _const_spec((GM_HEADS, GM_CHUNK, GM_CHUNK)), _const_spec((GM_CHUNK, GM_WIDTH)),
                  tab, tab, tab],
        out_specs=[row(Q_RANK), row(KV_RANK), row(ROPE_DIM), row(2 * LANES), row(GM_WIDTH),
                   pl.BlockSpec((1, GM_CHUNK, GM_WIDTH), lambda i: (i // tiles_per_seq, 0, 0))],
        out_shape=[jax.ShapeDtypeStruct((n_tok, Q_RANK), F32),
                   jax.ShapeDtypeStruct((n_tok, KV_RANK), F32),
                   jax.ShapeDtypeStruct((n_tok, ROPE_DIM), F32),
                   jax.ShapeDtypeStruct((n_tok, 2 * LANES), BF16),
                   jax.ShapeDtypeStruct((n_tok, GM_WIDTH), BF16),
                   jax.ShapeDtypeStruct((batch, GM_CHUNK, GM_WIDTH), F32)],
        compiler_params=_cparams(("arbitrary",)),
        name="even_inproj",
    )(x2d, w_in_p, kv_norm, gm_g, gm_b, ws_m, bs_full, cos_t, sin_a, sin_b)


def _queries(q_a, qn_g, wq_nope, wq_pe, wuk_ref, cos_t, sin_a, sin_b):
    qn = _rms_norm(q_a, qn_g).astype(BF16)
    nope = _dot(qn, wq_nope).astype(BF16)
    lat = jnp.concatenate([_dot(nope[:, pr * LANES:(pr + 1) * LANES], wuk_ref[pr])
                           for pr in range(MLA_HEADS // 2)], axis=1)
    pe = _dot(qn, wq_pe)
    n = MLA_HEADS * LANES
    sin_s = sin_a + sin_b
    pes = [pe[:, h * LANES:(h + 1) * LANES] * cos_t + pe[:, n + h * LANES:n + (h + 1) * LANES] * sin_s
           for h in range(MLA_HEADS)]
    return lat, pes


def _mla_attn_kernel(qa_lo_ref, qa_hi_ref, kcat_ref, qn_ref, wqn_ref, wqp_ref, wuk_ref, wuv_ref,
                     cos_lo_ref, sina_lo_ref, sinb_lo_ref, cos_hi_ref, sina_hi_ref, sinb_hi_ref,
                     out_hbm, qcat_s, m_s, acc_s, obuf, osem, *, nq):
    tq = qa_lo_ref.shape[0]
    n_rows = MLA_HEADS * tq
    b, i = pl.program_id(0), pl.program_id(1)
    step = b * pl.num_programs(1) + i
    n_steps = pl.num_programs(0) * pl.num_programs(1)
    q_tiles = (i, nq - 1 - i)

    def out_copies():
        return [pltpu.make_async_copy(
            obuf.at[t], out_hbm.at[pl.ds(pl.multiple_of((b * nq + q_tiles[t]) * tq, tq), tq), :], osem.at[t])
            for t in range(2)]

    qa = (qa_lo_ref, qa_hi_ref)
    tabs = ((cos_lo_ref, sina_lo_ref, sinb_lo_ref), (cos_hi_ref, sina_hi_ref, sinb_hi_ref))
    for t in range(2):
        lat, pes = _queries(qa[t][...], qn_ref[...], wqn_ref[...], wqp_ref[...], wuk_ref,
                            *(r[...] for r in tabs[t]))
        for h in range(MLA_HEADS):
            qcat_s[t, h * tq:(h + 1) * tq, :] = (ATTN_SCALE_LOG2 * jnp.concatenate(
                [lat[:, h * LANES:(h + 1) * LANES], pes[h]], axis=1)).astype(BF16)

    lanes_x = lambda a, n: jnp.concatenate([a] * n, axis=1)
    rb = ATTN_ROW_BLOCK
    ones = jnp.ones((tq, LANES), BF16)

    def keys(kv_tile):
        k = kcat_ref[pl.ds(pl.multiple_of(kv_tile * tq, tq), tq), :]
        return k, jnp.concatenate([k[:, :KV_RANK], ones], axis=1)

    q_pos = lax.broadcasted_iota(jnp.int32, (rb, tq), 0) & (tq - 1)
    k_pos = lax.broadcasted_iota(jnp.int32, (rb, tq), 1)
    causal = k_pos <= q_pos
    for t in range(2):
        k, v1 = keys(q_tiles[t])
        for r0 in range(0, n_rows, rb):
            rows = slice(r0, r0 + rb)
            s = jnp.where(causal, _dot_nt(qcat_s[t, rows, :], k), -jnp.inf)
            m0 = jnp.broadcast_to(jnp.max(s, axis=-1, keepdims=True), (rb, LANES))
            m_s[t, rows, :] = m0
            acc_s[t, rows, :] = _dot(jnp.exp2(s - lanes_x(m0, tq // LANES)).astype(BF16), v1)

    for u in range(nq - 1):
        hi = u >= i
        t = hi.astype(jnp.int32)
        k, v1 = keys(jnp.where(hi, u - i, u))
        for r0 in range(0, n_rows, rb):
            rows = slice(r0, r0 + rb)
            s = _dot_nt(qcat_s[t, rows, :], k)
            m_old = m_s[t, rows, :]
            m_new = jnp.maximum(m_old, jnp.max(s, axis=-1, keepdims=True))
            alpha = jnp.exp2(m_old - m_new)
            p = jnp.exp2(s - lanes_x(m_new, tq // LANES))
            acc_s[t, rows, :] = lanes_x(alpha, 2) * acc_s[t, rows, :] + _dot(p.astype(BF16), v1)
            m_s[t, rows, :] = m_new

    @pl.when(step > 0)
    def _():
        for cp in out_copies():
            cp.wait()

    for t in range(2):
        o_all = jnp.concatenate(
            [acc_s[t, h * tq:(h + 1) * tq, :KV_RANK] / acc_s[t, h * tq:(h + 1) * tq, KV_RANK:]
             for h in range(MLA_HEADS)], axis=1).astype(BF16)
        obuf[t] = jnp.concatenate(
            [_dot(o_all[:, pr * 2 * LANES:(pr + 1) * 2 * LANES], wuv_ref[pr]) for pr in range(MLA_HEADS // 2)],
            axis=1).astype(BF16)
    for cp in out_copies():
        cp.start()

    @pl.when(step == n_steps - 1)
    def _():
        for cp in out_copies():
            cp.wait()


def _mla_attn(q_a, kcat, qn_g, wq_nope, wq_pe, w_uk, w_uv, cos_t, sin_a, sin_b, batch, seq):
    tq = ATTN_TQ
    nq = seq // tq
    n_attn = MLA_HEADS * V_DIM
    lo = lambda b, i: i
    hi = lambda b, i: nq - 1 - i
    tab = lambda sel: pl.BlockSpec((tq, LANES), lambda b, i: (sel(b, i), 0))
    qa_spec = lambda sel: pl.BlockSpec((tq, Q_RANK), lambda b, i: (b * nq + sel(b, i), 0))
    return pl.pallas_call(
        functools.partial(_mla_attn_kernel, nq=nq),
        grid=(batch, nq // 2),
        in_specs=[qa_spec(lo), qa_spec(hi),
                  pl.BlockSpec((None, seq, 2 * LANES), lambda b, i: (b, 0, 0)),
                  _const_spec((1, Q_RANK)), _const_spec(wq_nope.shape), _const_spec(wq_pe.shape),
                  _const_spec(w_uk.shape), _const_spec(w_uv.shape),
                  tab(lo), tab(lo), tab(lo), tab(hi), tab(hi), tab(hi)],
        out_specs=pl.BlockSpec(memory_space=pl.ANY),
        out_shape=jax.ShapeDtypeStruct((batch * seq, n_attn), BF16),
        scratch_shapes=[pltpu.VMEM((2, MLA_HEADS * tq, 2 * LANES), BF16),
                        pltpu.VMEM((2, MLA_HEADS * tq, LANES), F32),
                        pltpu.VMEM((2, MLA_HEADS * tq, 2 * KV_RANK), F32),
                        pltpu.VMEM((2, tq, n_attn), BF16),
                        pltpu.SemaphoreType.DMA((2,))],
        compiler_params=_cparams(("arbitrary", "arbitrary")),
        name="mla_attn",
    )(q_a, q_a, kcat.reshape(batch, seq, 2 * LANES), qn_g, wq_nope, wq_pe, w_uk, w_uv,
      cos_t, sin_a, sin_b, cos_t, sin_a, sin_b)


def _mix_ffn_kernel(*refs, n_act, has_pre):
    x_ref = refs[0]
    act_refs = refs[1:1 + n_act]
    k = 1 + n_act
    pre_ref = refs[k] if has_pre else None
    k += int(has_pre)
    wo_refs = refs[k:k + n_act]
    k += n_act
    g1_ref, b1_ref, g2_ref, b2_ref, w1_ref, w2_ref, out_ref = refs[k:k + 7]

    tm = x_ref.shape[0]
    rb = min(FFN_ROW_BLOCK, tm)
    blocks = [slice(r0, r0 + rb) for r0 in range(0, tm, rb)]
    fs = []
    for rows in blocks:
        f = None
        for idx in range(n_act):
            a = act_refs[idx][rows, :]
            if has_pre and idx == 0:
                a = _dot(a.astype(BF16), pre_ref[...])
            part = _dot(a.astype(BF16), wo_refs[idx][...])
            f = part if f is None else f + part
        fs.append(f)
    x1s = [_layer_norm(DN_ALPHA * x_ref[rows, :] + f, g1_ref[...], b1_ref[...]) for rows, f in zip(blocks, fs)]
    x1bs = [x1.astype(BF16) for x1 in x1s]
    accs = [None] * len(blocks)
    for c in range(FFN_HIDDEN // FFN_HC):
        cols = slice(c * FFN_HC, (c + 1) * FFN_HC)
        for i, x1b in enumerate(x1bs):
            h = jnp.maximum(_dot(x1b, w1_ref[:, cols]), 0.0)
            part = _dot((h * h).astype(BF16), w2_ref[cols, :])
            accs[i] = part if accs[i] is None else accs[i] + part
    for rows, x1, acc in zip(blocks, x1s, accs):
        out_ref[rows, :] = _layer_norm(DN_ALPHA * x1 + acc, g2_ref[...], b2_ref[...])


def _mix_ffn(x2d, acts, pre_w, wos, ln1_g, ln1_b, ln2_g, ln2_b, w1_all, w2_all, layer, name):
    n_tok = x2d.shape[0]
    tm = min(FFN_TM, n_tok)
    row = lambda w: pl.BlockSpec((tm, w), lambda i: (i, 0))
    has_pre = pre_w is not None
    in_specs = [row(D_MODEL)] + [row(a.shape[1]) for a in acts]
    args = [x2d] + list(acts)
    if has_pre:
        in_specs.append(_const_spec(pre_w.shape))
        args.append(pre_w)
    in_specs += [_const_spec(w.shape) for w in wos]
    args += list(wos)
    layer_spec = lambda w: pl.BlockSpec((None,) + w.shape[1:], lambda i: (layer, 0, 0), pipeline_mode=pl.Buffered(1))
    in_specs += [_const_spec((1, D_MODEL))] * 4 + [layer_spec(w1_all), layer_spec(w2_all)]
    args += [ln1_g, ln1_b, ln2_g, ln2_b, w1_all, w2_all]
    return pl.pallas_call(
        functools.partial(_mix_ffn_kernel, n_act=len(acts), has_pre=has_pre),
        grid=(n_tok // tm,),
        in_specs=in_specs,
        out_specs=row(D_MODEL),
        out_shape=jax.ShapeDtypeStruct((n_tok, D_MODEL), F32),
        compiler_params=_cparams(("arbitrary",)),
        name=name,
    )(*args)


def _even_inproj_sample_kernel(x_ref, w_ref, kvn_ref, gmg_ref, gmb_ref, ws0_ref, bs0_ref,
                               cos_ref, sina_ref, sinb_ref, qn_ref, wqn_ref, wqp_ref, wuk_ref,
                               ckv_ref, kpe_ref, gate_ref, vn_ref, qlat_ref, qpe_ref):
    z = _dot(x_ref[...].astype(BF16), w_ref[...])
    o_c, o_u, o_v, o_k = Q_RANK, Q_RANK + KV_RANK, Q_RANK + KV_RANK + GM_WIDTH, Q_RANK + KV_RANK + 2 * GM_WIDTH
    cos_t, sin_a, sin_b = cos_ref[...], sina_ref[...], sinb_ref[...]
    c_n = _rms_norm(z[:, o_c:o_u], kvn_ref[...])
    ckv_ref[...] = c_n
    kp = _rope128(z[:, o_k:o_k + LANES], cos_t, sin_a, sin_b)
    kpe_ref[...] = kp[:, :ROPE_DIM]
    gu = jax.nn.gelu(z[:, o_u:o_v])
    v_n = _layer_norm(jax.nn.gelu(z[:, o_v:o_k]), gmg_ref[...], gmb_ref[...])
    vn_ref[...] = v_n
    mixed = ws0_ref[...].astype(F32) * v_n.astype(BF16).astype(F32) + bs0_ref[...]
    gate_ref[...] = (gu * mixed).astype(BF16)
    lat, pes = _queries(z[:, :o_c], qn_ref[...], wqn_ref[...], wqp_ref[...], wuk_ref,
                        cos_t, sin_a, sin_b)
    qlat_ref[...] = lat
    qpe_ref[...] = jnp.concatenate(pes, axis=1)


def _even_inproj_sample(xs, w_in_p, kv_norm, gm_g, gm_b, ws0, bs0, cos_t, sin_a, sin_b,
                        qn_g, wq_nope, wq_pe, w_uk):
    n = xs.shape[0]
    args = (xs, w_in_p, kv_norm, gm_g, gm_b, ws0, bs0, cos_t, sin_a, sin_b, qn_g, wq_nope, wq_pe, w_uk)
    full = lambda shape: pl.BlockSpec(shape, lambda i: (0,) * len(shape))
    widths = [(KV_RANK, F32), (ROPE_DIM, F32), (GM_WIDTH, BF16), (GM_WIDTH, F32),
              (MLA_HEADS * LANES, F32), (MLA_HEADS * LANES, F32)]
    return pl.pallas_call(
        _even_inproj_sample_kernel,
        grid=(1,),
        in_specs=[full(a.shape) for a in args],
        out_specs=[full((n, w)) for w, _ in widths],
        out_shape=[jax.ShapeDtypeStruct((n, w), dt) for w, dt in widths],
        compiler_params=_cparams(("arbitrary",)),
        name="even_inproj_sample",
    )(*args)


def _decode_attn_kernel(pt_ref, qlat_ref, qpe_ref, ckvs_ref, kpes_ref, ckv_hbm, kpe_hbm, out_ref,
                        ckv_buf, kpe_buf, m_buf, sem, *, n_pg, n_groups):
    b = pl.program_id(0)
    n_b = pl.num_programs(0)
    assert n_groups % DEC_SLOTS == 0 and DEC_AHEAD + 2 <= DEC_SLOTS

    def group_copies(seq, grp):
        slot = grp % DEC_SLOTS
        cps = []
        for k in range(n_pg):
            page = pt_ref[seq, grp * n_pg + k]
            keys = pl.ds(k * PAGE_SIZE, PAGE_SIZE)
            cps.append(pltpu.make_async_copy(ckv_hbm.at[page], ckv_buf.at[slot, keys, :], sem.at[0, slot]))
            cps.append(pltpu.make_async_copy(kpe_hbm.at[page], kpe_buf.at[slot, :, keys], sem.at[1, slot]))
        return cps

    @pl.when(b == 0)
    def _():
        for g in range(DEC_AHEAD):
            for cp in group_copies(0, g):
                cp.start()

    heads, span = MLA_HEADS, DEC_SPAN_PAGES * PAGE_SIZE
    half_span = span // 2
    n_span = n_pg // DEC_SPAN_PAGES
    qlat = qlat_ref[...]
    zero = jnp.zeros_like(qlat)
    q2 = jnp.concatenate([jnp.concatenate([qlat, zero], axis=1),
                          jnp.concatenate([zero, qlat], axis=1)], axis=0).astype(BF16)
    qpe = qpe_ref[...][:, :ROPE_DIM].astype(BF16)

    def scores(slot):
        s_pe = _dot(qpe, kpe_buf[slot].astype(BF16))
        pieces = []
        for j in range(n_span):
            m_buf[slot, j] = jnp.concatenate(
                [ckv_buf[slot, pl.ds(j * span, half_span), :],
                 ckv_buf[slot, pl.ds(j * span + half_span, half_span), :]], axis=1).astype(BF16)
            s2 = _dot_nt(q2, m_buf[slot, j])
            pieces += [s2[:heads] + s_pe[:, j * span:j * span + half_span],
                       s2[heads:] + s_pe[:, j * span + half_span:(j + 1) * span]]
        return jnp.concatenate(pieces, axis=1) * ATTN_SCALE

    def values(p, grp):
        slot = grp % DEC_SLOTS
        o2 = None
        for j in range(n_span):
            lhs = jnp.concatenate([p[:, j * span:j * span + half_span],
                                   p[:, j * span + half_span:(j + 1) * span]], axis=0).astype(BF16)
            part = _dot(lhs, m_buf[slot, j])
            o2 = part if o2 is None else o2 + part
        return o2[:heads, :KV_RANK] + o2[heads:, KV_RANK:]

    state = dict(m_run=None, m=None, l=None, acc=None)
    raw = {}
    probs = {}

    def softmax(k):
        s = raw.pop(k)
        m_cur = jnp.max(s, axis=-1, keepdims=True)
        m_new = m_cur if state["m_run"] is None else jnp.maximum(state["m_run"], m_cur)
        state["m_run"] = m_new
        probs[k] = (jnp.exp(s - m_new), m_new)

    def fold(k):
        p, m_k = probs.pop(k)
        pv = values(p, k)
        p_sum = jnp.sum(p, axis=-1, keepdims=True)
        if state["m"] is None:
            state["l"], state["acc"] = p_sum, pv
        else:
            alpha = jnp.exp(state["m"] - m_k)
            state["l"], state["acc"] = alpha * state["l"] + p_sum, alpha * state["acc"] + pv
        state["m"] = m_k

    for g in range(n_groups):
        if g + DEC_AHEAD < n_groups:
            for cp in group_copies(b, g + DEC_AHEAD):
                cp.start()
        else:
            @pl.when(b + 1 < n_b)
            def _():
                for cp in group_copies(b + 1, g + DEC_AHEAD - n_groups):
                    cp.start()
        for cp in group_copies(b, g):
            cp.wait()
        raw[g] = scores(g % DEC_SLOTS)
        if g >= 1:
            softmax(g - 1)
        if g >= 2:
            fold(g - 2)
    softmax(n_groups - 1)
    fold(n_groups - 2)
    fold(n_groups - 1)
    m, l, acc = state["m"], state["l"], state["acc"]

    r = lambda a: a.astype(BF16).astype(F32)
    kv = r(ckvs_ref[...])
    s_self = (jnp.sum(r(qlat_ref[...]) * kv, axis=-1, keepdims=True)
              + jnp.sum(r(qpe_ref[...]) * r(kpes_ref[...]), axis=-1, keepdims=True)) * ATTN_SCALE
    m_n = jnp.maximum(m, s_self)
    a = jnp.exp(m - m_n)
    p_self = jnp.exp(s_self - m_n)
    out_ref[...] = (a * acc + r(p_self) * kv) / (a * l + p_self)


def _decode_attn(page_table, qlat2, qpe2, ckv_s3, kpe_s3, cache_ckv_e, cache_kpe_t):
    dec_batch, n_pages = page_table.shape
    n_pg = DEC_PG
    qspec = pl.BlockSpec((MLA_HEADS, LANES), lambda b, pt: (b, 0))
    self_spec = pl.BlockSpec((None, 1, LANES), lambda b, pt: (b, 0, 0))
    hbm = pl.BlockSpec(memory_space=pl.ANY)
    grid_spec = pltpu.PrefetchScalarGridSpec(
        num_scalar_prefetch=1,
        grid=(dec_batch,),
        in_specs=[qspec, qspec, self_spec, self_spec, hbm, hbm],
        out_specs=qspec,
        scratch_shapes=[pltpu.VMEM((DEC_SLOTS, n_pg * PAGE_SIZE, KV_RANK), F32),
                        pltpu.VMEM((DEC_SLOTS, ROPE_DIM, n_pg * PAGE_SIZE), F32),
                        pltpu.VMEM((DEC_SLOTS, n_pg // DEC_SPAN_PAGES, DEC_SPAN_PAGES // 2 * PAGE_SIZE, 2 * KV_RANK),
                                   BF16),
                        pltpu.SemaphoreType.DMA((2, DEC_SLOTS))],
    )
    return pl.pallas_call(
        functools.partial(_decode_attn_kernel, n_pg=n_pg, n_groups=n_pages // n_pg),
        grid_spec=grid_spec,
        out_shape=jax.ShapeDtypeStruct((dec_batch * MLA_HEADS, KV_RANK), F32),
        compiler_params=_cparams(("arbitrary",)),
        name="decode_attn",
    )(page_table, qlat2, qpe2, ckv_s3, kpe_s3, cache_ckv_e, cache_kpe_t)


def _s5_readout(hcat_blocks, u, cw_ref, d_ref, wglu_ref):
    y = jnp.concatenate([_dot(hcat_blocks[k], cw_ref[k]) for k in range(S5_NB)], axis=1)
    y = jax.nn.gelu(y + d_ref[...] * u)
    return y * jax.nn.sigmoid(_dot(y.astype(BF16), wglu_ref[...]))


def _s5_prompt_kernel(x_ref, win_ref, bw_ref, a_ref, cw_ref, d_ref, wglu_ref,
                      yg_ref, hre_ref, him_ref, u_s, y_s, bu_s, h_s):
    n_b, seg, _ = x_ref.shape
    half = S5_SLABS // 2
    sub = S5_SUB
    n_sub = seg // sub
    rows_sub = sub * n_b
    q_slabs = S5_WIDTH // LANES

    @pl.when(pl.program_id(0) == 0)
    def _():
        h_s[...] = jnp.zeros(h_s.shape, F32)

    def project(s):
        rows = slice(s * rows_sub, (s + 1) * rows_sub)
        xs = jnp.concatenate([x_ref[b, s * sub:(s + 1) * sub, :] for b in range(n_b)], axis=0)
        u = _dot(xs.astype(BF16), win_ref[...])
        for b in range(n_b):
            for q in range(q_slabs):
                u_s[s, q, pl.ds(b, sub, stride=n_b), :] = u[b * sub:(b + 1) * sub, q * LANES:(q + 1) * LANES]
        for k in range(S5_NB):
            r = _dot(u_s[s, k].astype(BF16), bw_ref[k])
            for q in range(S5_BLK_SLABS):
                bu_s[S5_BLK_SLABS * k + q, rows, :] = r[:, q * LANES:(q + 1) * LANES]
                bu_s[half + S5_BLK_SLABS * k + q, rows, :] = r[:, (S5_BLK_SLABS + q) * LANES:
                                                                (S5_BLK_SLABS + q + 1) * LANES]

    def recur(s, h):
        for t in range(s * sub, (s + 1) * sub):
            rows = slice(t * n_b, (t + 1) * n_b)
            hs = bu_s[:, rows, :]
            ar, ai = a_ref[:half], a_ref[half:]
            hr, hi = h[:half], h[half:]
            h = jnp.concatenate([ar * hr - ai * hi + hs[:half], ar * hi + ai * hr + hs[half:]], axis=0)
            bu_s[:, rows, :] = h
        return h

    def read_out(s):
        rows = slice(s * rows_sub, (s + 1) * rows_sub)
        blocks = [jnp.concatenate([bu_s[S5_BLK_SLABS * k + q, rows, :] for q in range(S5_BLK_SLABS)]
                                  + [bu_s[half + S5_BLK_SLABS * k + q, rows, :] for q in range(S5_BLK_SLABS)],
                                  axis=1).astype(BF16)
                  for k in range(S5_NB)]
        u_t = jnp.concatenate([u_s[s, q] for q in range(q_slabs)], axis=1)
        y = _s5_readout(blocks, u_t, cw_ref, d_ref, wglu_ref)
        for q in range(q_slabs):
            y_s[s, q] = y[:, q * LANES:(q + 1) * LANES]
        for b in range(n_b):
            yg_ref[b, s * sub:(s + 1) * sub, :] = jnp.concatenate(
                [y_s[s, q, pl.ds(b, sub, stride=n_b), :] for q in range(q_slabs)], axis=1).astype(BF16)

    h = h_s[...]
    project(0)
    for s in range(n_sub):
        if s + 1 < n_sub:
            project(s + 1)
        h = recur(s, h)
        if s >= 1:
            read_out(s - 1)
    read_out(n_sub - 1)
    h_s[...] = h
    hre_ref[...] = jnp.concatenate([h[s] for s in range(half)], axis=1)
    him_ref[...] = jnp.concatenate([h[half + s] for s in range(half)], axis=1)


def _s5_prompt(x3, w_in, bw, a_b, cw, d_row, w_glu):
    batch, seq, _ = x3.shape
    seg = S5_L
    slab_rows = (seg // S5_SUB, S5_WIDTH // LANES, S5_SUB * batch, LANES)
    return pl.pallas_call(
        _s5_prompt_kernel,
        grid=(seq // seg,),
        in_specs=[pl.BlockSpec((batch, seg, D_MODEL), lambda c: (0, c, 0)),
                  _const_spec(w_in.shape), _const_spec(bw.shape), _const_spec(a_b.shape),
                  _const_spec(cw.shape), _const_spec(d_row.shape), _const_spec(w_glu.shape)],
        out_specs=[pl.BlockSpec((batch, seg, S5_WIDTH), lambda c: (0, c, 0)),
                   pl.BlockSpec((batch, S5_STATES), lambda c: (0, 0)),
                   pl.BlockSpec((batch, S5_STATES), lambda c: (0, 0))],
        out_shape=[jax.ShapeDtypeStruct((batch, seq, S5_WIDTH), BF16),
                   jax.ShapeDtypeStruct((batch, S5_STATES), F32),
                   jax.ShapeDtypeStruct((batch, S5_STATES), F32)],
        scratch_shapes=[pltpu.VMEM(slab_rows, F32), pltpu.VMEM(slab_rows, F32),
                        pltpu.VMEM((S5_SLABS, seg * batch, LANES), F32),
                        pltpu.VMEM((S5_SLABS, batch, LANES), F32)],
        compiler_params=_cparams(("arbitrary",)),
        name="s5_prompt",
    )(x3, w_in, bw, a_b, cw, d_row, w_glu)


def _s5_sample_kernel(x_ref, h0r_ref, h0i_ref, win_ref, bw_ref, ar_ref, ai_ref, cw_ref, d_ref, wglu_ref,
                      yg_ref, hre_ref, him_ref):
    u = _dot(x_ref[...].astype(BF16), win_ref[...])
    ub = u.astype(BF16)
    bu = [_dot(ub[:, k * LANES:(k + 1) * LANES], bw_ref[k]) for k in range(S5_NB)]
    w = S5_KB * S5_STATE
    bu_r = jnp.concatenate([r[:, :w] for r in bu], axis=1)
    bu_i = jnp.concatenate([r[:, w:] for r in bu], axis=1)
    ar, ai, h0r, h0i = ar_ref[...], ai_ref[...], h0r_ref[...], h0i_ref[...]
    hr = bu_r + (ar * h0r - ai * h0i)
    hi = bu_i + (ar * h0i + ai * h0r)
    hre_ref[...] = hr
    him_ref[...] = hi
    blocks = [jnp.concatenate([hr[:, k * w:(k + 1) * w], hi[:, k * w:(k + 1) * w]], axis=1).astype(BF16)
              for k in range(S5_NB)]
    yg_ref[...] = _s5_readout(blocks, u, cw_ref, d_ref, wglu_ref).astype(BF16)


def _s5_sample(xs, h0r, h0i, w_in, bw, a_r, a_i, cw, d_row, w_glu):
    n = xs.shape[0]
    args = (xs, h0r, h0i, w_in, bw, a_r, a_i, cw, d_row, w_glu)
    full = lambda shape: pl.BlockSpec(shape, lambda i: (0,) * len(shape))
    return pl.pallas_call(
        _s5_sample_kernel,
        grid=(1,),
        in_specs=[full(a.shape) for a in args],
        out_specs=[full((n, S5_WIDTH)), full((n, S5_STATES)), full((n, S5_STATES))],
        out_shape=[jax.ShapeDtypeStruct((n, S5_WIDTH), BF16),
                   jax.ShapeDtypeStruct((n, S5_STATES), F32),
                   jax.ShapeDtypeStruct((n, S5_STATES), F32)],
        compiler_params=_cparams(("arbitrary",)),
        name="s5_sample",
    )(*args)


def _rope_tables(pos):
    freqs = ROPE_THETA ** (-jnp.arange(ROPE_HALF, dtype=F32) / ROPE_HALF)
    ang = pos.astype(F32)[:, None] * freqs
    cos, sin = jnp.cos(ang), jnp.sin(ang)
    zero = jnp.zeros_like(cos)
    pad = jnp.zeros((pos.shape[0], LANES - ROPE_DIM), F32)
    cos_t = jnp.concatenate([cos, cos, pad], axis=1)
    sin_a = jnp.concatenate([zero, sin, pad], axis=1)
    sin_b = jnp.concatenate([-sin, zero, pad], axis=1)
    return cos_t, sin_a, sin_b


def _block_diag(blocks):
    *lead, n, r, c = blocks.shape
    eye = jnp.eye(n, dtype=blocks.dtype)
    return jnp.einsum('...nrc,nm->...nrmc', blocks, eye).reshape(*lead, n * r, n * c)


def _even_params(w_in, q_norm, w_q_b, kv_norm, w_kv_b, gm_g, gm_b, gm_w_s, gm_b_s, w_out):
    i1, i2, i3, i4 = Q_RANK, Q_RANK + KV_RANK, Q_RANK + KV_RANK + ROPE_DIM, Q_RANK + KV_RANK + ROPE_DIM + GM_WIDTH
    w_in_p = jnp.concatenate([w_in[:, :i2], w_in[:, i3:], w_in[:, i2:i3],
                              jnp.zeros((D_MODEL, LANES - ROPE_DIM), F32)], axis=1).astype(BF16)
    wq_nope = w_q_b[:, :, :NOPE_DIM].reshape(Q_RANK, MLA_HEADS * NOPE_DIM).astype(BF16)
    pe = w_q_b[:, :, NOPE_DIM:]
    pe_sw = jnp.concatenate([pe[:, :, ROPE_HALF:], pe[:, :, :ROPE_HALF]], axis=2)
    lane_pad = jnp.zeros((Q_RANK, MLA_HEADS, LANES - ROPE_DIM), F32)
    wq_pe = jnp.concatenate(
        [jnp.concatenate([w, lane_pad], axis=2).reshape(Q_RANK, MLA_HEADS * LANES) for w in (pe, pe_sw)],
        axis=1).astype(BF16)
    uk = jnp.transpose(w_kv_b[:, :, :NOPE_DIM], (1, 2, 0))
    uv = jnp.transpose(w_kv_b[:, :, NOPE_DIM:], (1, 0, 2))
    pairs = MLA_HEADS // 2
    w_uk = _block_diag(uk.reshape(pairs, 2, NOPE_DIM, KV_RANK)).astype(BF16)
    w_uv_p = _block_diag(uv.reshape(pairs, 2, KV_RANK, V_DIM)).astype(BF16)
    w_uv = _block_diag(uv).astype(BF16)
    causal = jnp.tril(jnp.ones((GM_CHUNK, GM_CHUNK), dtype=bool))
    ws_m = jnp.where(causal[None], gm_w_s, 0).astype(BF16)
    bs_full = jnp.repeat(gm_b_s.T, GM_HEAD_DIM, axis=1)
    ws0 = jnp.repeat(gm_w_s[:, 0, 0], GM_HEAD_DIM)[None, :].astype(BF16)
    bs0 = bs_full[:1]
    n_attn = MLA_HEADS * V_DIM
    return dict(w_in_p=w_in_p, qn=q_norm[None, :], wq_nope=wq_nope, wq_pe=wq_pe, w_uk=w_uk, w_uv=w_uv, w_uv_p=w_uv_p,
                kvn=kv_norm[None, :], gm_g=gm_g[None, :], gm_b=gm_b[None, :], ws_m=ws_m, bs_full=bs_full,
                ws0=ws0, bs0=bs0, wo_attn=w_out[:n_attn].astype(BF16), wo_gate=w_out[n_attn:].astype(BF16))


def _s5_params(w_in, a_re, a_im, b_re, b_im, c_re, c_im, d, log_dt, w_glu, w_out, batch):
    dt = jnp.exp(log_dt)[:, None]
    ld_r, ld_i = a_re * dt, a_im * dt
    mag = jnp.exp(ld_r)
    ab_r, ab_i = mag * jnp.cos(ld_i), mag * jnp.sin(ld_i)
    den = a_re * a_re + a_im * a_im
    cr = ((ab_r - 1.0) * a_re + ab_i * a_im) / den
    ci = (ab_i * a_re - (ab_r - 1.0) * a_im) / den
    bb_r = cr[..., None] * b_re - ci[..., None] * b_im
    bb_i = cr[..., None] * b_im + ci[..., None] * b_re

    def in_blocks(bb):
        return _block_diag(jnp.swapaxes(bb, 1, 2).reshape(S5_NB, S5_KB, S5_GROUP_DIM, S5_STATE))

    def out_blocks(cc):
        return _block_diag(jnp.swapaxes(cc, 1, 2).reshape(S5_NB, S5_KB, S5_STATE, S5_GROUP_DIM))

    bw = jnp.concatenate([in_blocks(bb_r), in_blocks(bb_i)], axis=2).astype(BF16)
    cw = jnp.concatenate([out_blocks(c_re), out_blocks(-c_im)], axis=1).astype(BF16)
    a_r, a_i = ab_r.reshape(1, S5_STATES), ab_i.reshape(1, S5_STATES)
    half = S5_SLABS // 2
    a_b = jnp.concatenate([jnp.broadcast_to(a_r.reshape(half, 1, LANES), (half, batch, LANES)),
                           jnp.broadcast_to(a_i.reshape(half, 1, LANES), (half, batch, LANES))], axis=0)
    return dict(w_in=w_in.astype(BF16), bw=bw, cw=cw, a_r=a_r, a_i=a_i, a_b=a_b,
                d=d.reshape(1, S5_WIDTH), w_glu=w_glu.astype(BF16), w_out=w_out.astype(BF16))


def kernel(x_prompt, x_sample, cache_ckv, cache_kpe, state_s5_re, state_s5_im, page_table, w_in_even, mla_q_norm, mla_w_q_b, mla_kv_norm, mla_w_kv_b, gm_norm_g, gm_norm_b, gm_w_s, gm_b_s, w_out_even, w_in_odd, s5_a_re, s5_a_im, s5_b_re, s5_b_im, s5_c_re, s5_c_im, s5_d, s5_log_dt, s5_w_glu, w_out_odd, ln_mix_g, ln_mix_b, ln_ffn_g, ln_ffn_b, ffn_w1, ffn_w2):
    batch, seq, _ = x_prompt.shape
    dec_batch, dec_seq, _ = x_sample.shape
    assert dec_seq == 1 and seq % INPROJ_TM == 0 and seq % ATTN_TQ == 0 and seq % S5_L == 0
    assert (seq // ATTN_TQ) % 2 == 0
    assert (batch * seq) % FFN_TM == 0 and page_table.shape[1] % DEC_PG == 0
    assert batch == SUBLANES

    xp = x_prompt.reshape(batch * seq, D_MODEL)
    xs = x_sample.reshape(dec_batch, D_MODEL)
    tabs_p = _rope_tables(jnp.arange(seq, dtype=jnp.int32))
    tabs_s = _rope_tables(PAST_LEN + jnp.arange(dec_seq, dtype=jnp.int32))

    outs = {k: [] for k in ("ckv_p", "kpe_p", "ckv_s", "kpe_s", "gmv_p", "gmv_s",
                            "s5re_p", "s5im_p", "s5re_s", "s5im_s")}
    ffn_w1_b, ffn_w2_b = ffn_w1.astype(BF16), ffn_w2.astype(BF16)
    for layer in range(DEPTH):
        ln = (ln_mix_g[layer][None, :], ln_mix_b[layer][None, :],
              ln_ffn_g[layer][None, :], ln_ffn_b[layer][None, :],
              ffn_w1_b, ffn_w2_b, layer)
        if layer % 2 == 0:
            e = layer // 2
            p = _even_params(w_in_even[e], mla_q_norm[e], mla_w_q_b[e], mla_kv_norm[e], mla_w_kv_b[e],
                             gm_norm_g[e], gm_norm_b[e], gm_w_s[e], gm_b_s[e], w_out_even[e])
            q_a, ckv, kpe, kcat, gate, gmv = _even_inproj(
                xp, p["w_in_p"], p["kvn"], p["gm_g"], p["gm_b"], p["ws_m"], p["bs_full"], *tabs_p, batch, seq)
            attn = _mla_attn(q_a, kcat, p["qn"], p["wq_nope"], p["wq_pe"], p["w_uk"], p["w_uv_p"], *tabs_p,
                             batch, seq)
            xp = _mix_ffn(xp, [attn, gate], None, [p["wo_attn"], p["wo_gate"]], *ln, name="even_ffn_prompt")
            outs["ckv_p"].append(ckv.reshape(batch, seq, KV_RANK))
            outs["kpe_p"].append(kpe.reshape(batch, seq, ROPE_DIM))
            outs["gmv_p"].append(gmv)
            ckv_s, kpe_s, gate_s, vn_s, qlat_s, qpe_s = _even_inproj_sample(
                xs, p["w_in_p"], p["kvn"], p["gm_g"], p["gm_b"], p["ws0"], p["bs0"], *tabs_s,
                p["qn"], p["wq_nope"], p["wq_pe"], p["w_uk"])
            kpe_pad = jnp.concatenate([kpe_s, jnp.zeros((dec_batch, LANES - ROPE_DIM), F32)], axis=1)
            o_lat = _decode_attn(page_table,
                                 qlat_s.reshape(dec_batch * MLA_HEADS, LANES),
                                 qpe_s.reshape(dec_batch * MLA_HEADS, LANES),
                                 ckv_s.reshape(dec_batch, 1, KV_RANK), kpe_pad.reshape(dec_batch, 1, LANES),
                                 cache_ckv[e], jnp.swapaxes(cache_kpe[e], 1, 2))
            xs = _mix_ffn(xs, [o_lat.reshape(dec_batch, MLA_HEADS * KV_RANK), gate_s], p["w_uv"],
                          [p["wo_attn"], p["wo_gate"]], *ln, name="even_ffn_sample")
            outs["ckv_s"].append(ckv_s.reshape(dec_batch, 1, KV_RANK))
            outs["kpe_s"].append(kpe_s.reshape(dec_batch, 1, ROPE_DIM))
            outs["gmv_s"].append(vn_s.reshape(dec_batch, 1, GM_WIDTH))
        else:
            o = layer // 2
            p = _s5_params(w_in_odd[o], s5_a_re[o], s5_a_im[o], s5_b_re[o], s5_b_im[o], s5_c_re[o],
                           s5_c_im[o], s5_d[o], s5_log_dt[o], s5_w_glu[o], w_out_odd[o], batch)
            yg, hre, him = _s5_prompt(xp.reshape(batch, seq, D_MODEL), p["w_in"], p["bw"], p["a_b"],
                                      p["cw"], p["d"], p["w_glu"])
            xp = _mix_ffn(xp, [yg.reshape(batch * seq, S5_WIDTH)], None, [p["w_out"]], *ln,
                          name="odd_ffn_prompt")
            outs["s5re_p"].append(hre.reshape(batch, S5_GROUPS, S5_STATE))
            outs["s5im_p"].append(him.reshape(batch, S5_GROUPS, S5_STATE))
            yg_s, hre_s, him_s = _s5_sample(
                xs, state_s5_re[o].reshape(dec_batch, S5_STATES), state_s5_im[o].reshape(dec_batch, S5_STATES),
                p["w_in"], p["bw"], p["a_r"], p["a_i"], p["cw"], p["d"], p["w_glu"])
            xs = _mix_ffn(xs, [yg_s], None, [p["w_out"]], *ln, name="odd_ffn_sample")
            outs["s5re_s"].append(hre_s.reshape(dec_batch, S5_GROUPS, S5_STATE))
            outs["s5im_s"].append(him_s.reshape(dec_batch, S5_GROUPS, S5_STATE))

    st = jnp.stack
    return (xp.reshape(batch, seq, D_MODEL), xs.reshape(dec_batch, dec_seq, D_MODEL),
            st(outs["ckv_p"]), st(outs["kpe_p"]), st(outs["ckv_s"]), st(outs["kpe_s"]),
            st(outs["gmv_p"]), st(outs["gmv_s"]),
            st(outs["s5re_p"]), st(outs["s5im_p"]), st(outs["s5re_s"]), st(outs["s5im_s"]))
```

```python
import functools
import math

import jax
import jax.numpy as jnp
from jax import lax
from jax.experimental import pallas as pl
from jax.experimental.pallas import tpu as pltpu

F32 = jnp.float32
BF16 = jnp.bfloat16

D_MODEL = 1024
DEPTH = 2
PAST_LEN = 16384
PAGE_SIZE = 128
MLA_HEADS = 8
NOPE_DIM = 64
ROPE_DIM = 32
ROPE_HALF = ROPE_DIM // 2
V_DIM = 64
Q_RANK = 256
KV_RANK = 128
ROPE_THETA = 10000.0
ATTN_SCALE = 1.0 / math.sqrt(NOPE_DIM + ROPE_DIM)
ATTN_SCALE_LOG2 = ATTN_SCALE * math.log2(math.e)
GM_HEADS = 8
GM_HEAD_DIM = 64
GM_WIDTH = GM_HEADS * GM_HEAD_DIM
GM_CHUNK = 128
S5_GROUP_DIM = 16
S5_GROUPS = 32
S5_STATE = 64
S5_WIDTH = S5_GROUPS * S5_GROUP_DIM
S5_STATES = S5_GROUPS * S5_STATE
FFN_HIDDEN = 4 * D_MODEL
DN_ALPHA = (2 * DEPTH) ** 0.25
LN_EPS = 1e-5
RMS_EPS = 1e-6

LANES = 128
SUBLANES = 8
VMEM_LIMIT_BYTES = 56 * 1024 * 1024

INPROJ_TM = 512
INPROJ_ROW_BLOCK = 256
ATTN_TQ = 256
ATTN_ROW_BLOCK = 256
FFN_TM = 1024
FFN_HC = 1024
FFN_ROW_BLOCK = 256
S5_L = 128
S5_SUB = 32
DEC_PG = 32
DEC_SLOTS = 4
DEC_AHEAD = 2
DEC_SPAN_PAGES = 4
S5_KB = 8
S5_NB = S5_GROUPS // S5_KB
S5_SLABS = 2 * S5_STATES // LANES
S5_BLK_SLABS = S5_KB * S5_STATE // LANES
S5_SLAB_PAD = SUBLANES


def _cparams(sem):
    return pltpu.CompilerParams(dimension_semantics=sem, vmem_limit_bytes=VMEM_LIMIT_BYTES)


def _const_spec(shape):
    n = len(shape)
    return pl.BlockSpec(shape, lambda *_: (0,) * n, pipeline_mode=pl.Buffered(1))


def _layer_norm(x, g, b):
    mu = jnp.mean(x, axis=-1, keepdims=True)
    xc = x - mu
    var = jnp.mean(xc * xc, axis=-1, keepdims=True)
    return xc * lax.rsqrt(var + LN_EPS) * g + b


def _rms_norm(x, g):
    return x * lax.rsqrt(jnp.mean(x * x, axis=-1, keepdims=True) + RMS_EPS) * g


def _rope128(x, cos_t, sin_a, sin_b):
    return (x * cos_t + pltpu.roll(x, ROPE_HALF, 1) * sin_a
            + pltpu.roll(x, LANES - ROPE_HALF, 1) * sin_b)


def _dot(a, b):
    return jnp.dot(a, b, preferred_element_type=F32)


def _dot_nt(a, b):
    return lax.dot_general(a, b, (((1,), (1,)), ((), ())), preferred_element_type=F32)


def _even_inproj_kernel(x_ref, w_ref, kvn_ref, gmg_ref, gmb_ref, ws_ref, bs_ref,
                        cos_ref, sina_ref, sinb_ref,
                        qa_ref, ckv_ref, kpe_ref, kcat_ref, gate_ref, gmv_ref):
    tm = x_ref.shape[0]
    o_c, o_u, o_v, o_k = Q_RANK, Q_RANK + KV_RANK, Q_RANK + KV_RANK + GM_WIDTH, Q_RANK + KV_RANK + 2 * GM_WIDTH
    rb = min(INPROJ_ROW_BLOCK, tm)
    blocks = [slice(r0, r0 + rb) for r0 in range(0, tm, rb)]
    zs = [_dot(x_ref[rows, :].astype(BF16), w_ref[...]) for rows in blocks]

    gus, vns = [], []
    for rows, z in zip(blocks, zs):
        qa_ref[rows, :] = z[:, :o_c]
        c_n = _rms_norm(z[:, o_c:o_u], kvn_ref[...])
        ckv_ref[rows, :] = c_n
        kp = _rope128(z[:, o_k:o_k + LANES], cos_ref[rows, :], sina_ref[rows, :], sinb_ref[rows, :])
        kpe_ref[rows, :] = kp[:, :ROPE_DIM]
        kcat_ref[rows, :] = jnp.concatenate([c_n, kp], axis=1).astype(BF16)
        gus.append(jax.nn.gelu(z[:, o_u:o_v]))
        vns.append(_layer_norm(jax.nn.gelu(z[:, o_v:o_k]), gmg_ref[...], gmb_ref[...]))

    gmv_ref[0] = vns[-1][rb - GM_CHUNK:, :]

    lane = lax.broadcasted_iota(jnp.int32, (GM_CHUNK, LANES), 1)
    lo = lane < GM_HEAD_DIM
    bs = bs_ref[...]
    for rows, gu, v_n in zip(blocks, gus, vns):
        for ci in range(rb // GM_CHUNK):
            chunk = slice(ci * GM_CHUNK, (ci + 1) * GM_CHUNK)
            pieces = []
            for pr in range(GM_WIDTH // LANES):
                r = v_n[chunk, pr * LANES:(pr + 1) * LANES]
                m0 = _dot(ws_ref[2 * pr], jnp.where(lo, r, 0.0).astype(BF16))
                m1 = _dot(ws_ref[2 * pr + 1], jnp.where(lo, 0.0, r).astype(BF16))
                pieces.append(m0 + m1)
            mixed = jnp.concatenate(pieces, axis=1) + bs
            gate_ref[pl.ds(rows.start + ci * GM_CHUNK, GM_CHUNK), :] = (gu[chunk, :] * mixed).astype(BF16)


def _even_inproj(x2d, w_in_p, kv_norm, gm_g, gm_b, ws_m, bs_full, cos_t, sin_a, sin_b, batch, seq):
    n_tok = x2d.shape[0]
    tm = INPROJ_TM
    tiles_per_seq = seq // tm
    row = lambda w: pl.BlockSpec((tm, w), lambda i: (i, 0))
    tab = pl.BlockSpec((tm, LANES), lambda i: (i % tiles_per_seq, 0))
    n_in = w_in_p.shape[1]
    return pl.pallas_call(
        _even_inproj_kernel,
        grid=(n_tok // tm,),
        in_specs=[row(D_MODEL), _const_spec((D_MODEL, n_in)), _const_spec((1, KV_RANK)),
                  _const_spec((1, GM_WIDTH)), _const_spec((1, GM_WIDTH)),
                  _const_spec((GM_HEADS, GM_CHUNK, GM_CHUNK)), _const_spec((GM_CHUNK, GM_WIDTH)),
                  tab, tab, tab],
        out_specs=[row(Q_RANK), row(KV_RANK), row(ROPE_DIM), row(2 * LANES), row(GM_WIDTH),
                   pl.BlockSpec((1, GM_CHUNK, GM_WIDTH), lambda i: (i // tiles_per_seq, 0, 0))],
        out_shape=[jax.ShapeDtypeStruct((n_tok, Q_RANK), F32),
                   jax.ShapeDtypeStruct((n_tok, KV_RANK), F32),
                   jax.ShapeDtypeStruct((n_tok, ROPE_DIM), F32),
                   jax.ShapeDtypeStruct((n_tok, 2 * LANES), BF16),
                   jax.ShapeDtypeStruct((n_tok, GM_WIDTH), BF16),
                   jax.ShapeDtypeStruct((batch, GM_CHUNK, GM_WIDTH), F32)],
        compiler_params=_cparams(("arbitrary",)),
        name="even_inproj",
    )(x2d, w_in_p, kv_norm, gm_g, gm_b, ws_m, bs_full, cos_t, sin_a, sin_b)


def _queries(q_a, qn_g, wq_nope, wq_pe, wuk_ref, cos_t, sin_a, sin_b):
    qn = _rms_norm(q_a, qn_g).astype(BF16)
    nope = _dot(qn, wq_nope).astype(BF16)
    lat = jnp.concatenate([_dot(nope[:, pr * LANES:(pr + 1) * LANES], wuk_ref[pr])
                           for pr in range(MLA_HEADS // 2)], axis=1)
    pe = _dot(qn, wq_pe)
    n = MLA_HEADS * LANES
    sin_s = sin_a + sin_b
    pes = [pe[:, h * LANES:(h + 1) * LANES] * cos_t + pe[:, n + h * LANES:n + (h + 1) * LANES] * sin_s
           for h in range(MLA_HEADS)]
    return lat, pes


def _mla_attn_kernel(qa_lo_ref, qa_hi_ref, kcat_ref, qn_ref, wqn_ref, wqp_ref, wuk_ref, wuv_ref,
                     cos_lo_ref, sina_lo_ref, sinb_lo_ref, cos_hi_ref, sina_hi_ref, sinb_hi_ref,
                     out_hbm, qcat_s, m_s, acc_s, obuf, osem, *, nq):
    tq = qa_lo_ref.shape[0]
    n_rows = MLA_HEADS * tq
    b, i = pl.program_id(0), pl.program_id(1)
    step = b * pl.num_programs(1) + i
    n_steps = pl.num_programs(0) * pl.num_programs(1)
    q_tiles = (i, nq - 1 - i)

    def out_copies():
        return [pltpu.make_async_copy(
            obuf.at[t], out_hbm.at[pl.ds(pl.multiple_of((b * nq + q_tiles[t]) * tq, tq), tq), :], osem.at[t])
            for t in range(2)]

    qa = (qa_lo_ref, qa_hi_ref)
    tabs = ((cos_lo_ref, sina_lo_ref, sinb_lo_ref), (cos_hi_ref, sina_hi_ref, sinb_hi_ref))
    for t in range(2):
        lat, pes = _queries(qa[t][...], qn_ref[...], wqn_ref[...], wqp_ref[...], wuk_ref,
                            *(r[...] for r in tabs[t]))
        for h in range(MLA_HEADS):
            qcat_s[t, h * tq:(h + 1) * tq, :] = (ATTN_SCALE_LOG2 * jnp.concatenate(
                [lat[:, h * LANES:(h + 1) * LANES], pes[h]], axis=1)).astype(BF16)

    lanes_x = lambda a, n: jnp.concatenate([a] * n, axis=1)
    rb = ATTN_ROW_BLOCK
    ones = jnp.ones((tq, LANES), BF16)

    def keys(kv_tile):
        k = kcat_ref[pl.ds(pl.multiple_of(kv_tile * tq, tq), tq), :]
        return k, jnp.concatenate([k[:, :KV_RANK], ones], axis=1)

    q_pos = lax.broadcasted_iota(jnp.int32, (rb, tq), 0) & (tq - 1)
    k_pos = lax.broadcasted_iota(jnp.int32, (rb, tq), 1)
    causal = k_pos <= q_pos
    for t in range(2):
        k, v1 = keys(q_tiles[t])
        for r0 in range(0, n_rows, rb):
            rows = slice(r0, r0 + rb)
            s = jnp.where(causal, _dot_nt(qcat_s[t, rows, :], k), -jnp.inf)
            m0 = jnp.broadcast_to(jnp.max(s, axis=-1, keepdims=True), (rb, LANES))
            m_s[t, rows, :] = m0
            acc_s[t, rows, :] = _dot(jnp.exp2(s - lanes_x(m0, tq // LANES)).astype(BF16), v1)

    for u in range(nq - 1):
        hi = u >= i
        t = hi.astype(jnp.int32)
        k, v1 = keys(jnp.where(hi, u - i, u))
        for r0 in range(0, n_rows, rb):
            rows = slice(r0, r0 + rb)
            s = _dot_nt(qcat_s[t, rows, :], k)
            m_old = m_s[t, rows, :]
            m_new = jnp.maximum(m_old, jnp.max(s, axis=-1, keepdims=True))
            alpha = jnp.exp2(m_old - m_new)
            p = jnp.exp2(s - lanes_x(m_new, tq // LANES))
            acc_s[t, rows, :] = lanes_x(alpha, 2) * acc_s[t, rows, :] + _dot(p.astype(BF16), v1)
            m_s[t, rows, :] = m_new

    @pl.when(step > 0)
    def _():
        for cp in out_copies():
            cp.wait()

    for t in range(2):
        o_all = jnp.concatenate(
            [acc_s[t, h * tq:(h + 1) * tq, :KV_RANK] / acc_s[t, h * tq:(h + 1) * tq, KV_RANK:]
             for h in range(MLA_HEADS)], axis=1).astype(BF16)
        obuf[t] = jnp.concatenate(
            [_dot(o_all[:, pr * 2 * LANES:(pr + 1) * 2 * LANES], wuv_ref[pr]) for pr in range(MLA_HEADS // 2)],
            axis=1).astype(BF16)
    for cp in out_copies():
        cp.start()

    @pl.when(step == n_steps - 1)
    def _():
        for cp in out_copies():
            cp.wait()


def _mla_attn(q_a, kcat, qn_g, wq_nope, wq_pe, w_uk, w_uv, cos_t, sin_a, sin_b, batch, seq):
    tq = ATTN_TQ
    nq = seq // tq
    n_attn = MLA_HEADS * V_DIM
    lo = lambda b, i: i
    hi = lambda b, i: nq - 1 - i
    tab = lambda sel: pl.BlockSpec((tq, LANES), lambda b, i: (sel(b, i), 0))
    qa_spec = lambda sel: pl.BlockSpec((tq, Q_RANK), lambda b, i: (b * nq + sel(b, i), 0))
    return pl.pallas_call(
        functools.partial(_mla_attn_kernel, nq=nq),
        grid=(batch, nq // 2),
        in_specs=[qa_spec(lo), qa_spec(hi),
                  pl.BlockSpec((None, seq, 2 * LANES), lambda b, i: (b, 0, 0)),
                  _const_spec((1, Q_RANK)), _const_spec(wq_nope.shape), _const_spec(wq_pe.shape),
                  _const_spec(w_uk.shape), _const_spec(w_uv.shape),
                  tab(lo), tab(lo), tab(lo), tab(hi), tab(hi), tab(hi)],
        out_specs=pl.BlockSpec(memory_space=pl.ANY),
        out_shape=jax.ShapeDtypeStruct((batch * seq, n_attn), BF16),
        scratch_shapes=[pltpu.VMEM((2, MLA_HEADS * tq, 2 * LANES), BF16),
                        pltpu.VMEM((2, MLA_HEADS * tq, LANES), F32),
                        pltpu.VMEM((2, MLA_HEADS * tq, 2 * KV_RANK), F32),
                        pltpu.VMEM((2, tq, n_attn), BF16),
                        pltpu.SemaphoreType.DMA((2,))],
        compiler_params=_cparams(("arbitrary", "arbitrary")),
        name="mla_attn",
    )(q_a, q_a, kcat.reshape(batch, seq, 2 * LANES), qn_g, wq_nope, wq_pe, w_uk, w_uv,
      cos_t, sin_a, sin_b, cos_t, sin_a, sin_b)


def _mix_ffn_kernel(*refs, n_act, has_pre):
    x_ref = refs[0]
    act_refs = refs[1:1 + n_act]
    k = 1 + n_act
    pre_ref = refs[k] if has_pre else None
    k += int(has_pre)
    wo_refs = refs[k:k + n_act]
    k += n_act
    g1_ref, b1_ref, g2_ref, b2_ref, w1_ref, w2_ref, out_ref = refs[k:k + 7]

    tm = x_ref.shape[0]
    rb = min(FFN_ROW_BLOCK, tm)
    blocks = [slice(r0, r0 + rb) for r0 in range(0, tm, rb)]
    fs = []
    for rows in blocks:
        f = None
        for idx in range(n_act):
            a = act_refs[idx][rows, :]
            if has_pre and idx == 0:
                a = _dot(a.astype(BF16), pre_ref[...])
            part = _dot(a.astype(BF16), wo_refs[idx][...])
            f = part if f is None else f + part
        fs.append(f)
    x1s = [_layer_norm(DN_ALPHA * x_ref[rows, :] + f, g1_ref[...], b1_ref[...]) for rows, f in zip(blocks, fs)]
    x1bs = [x1.astype(BF16) for x1 in x1s]
    accs = [None] * len(blocks)
    for c in range(FFN_HIDDEN // FFN_HC):
        cols = slice(c * FFN_HC, (c + 1) * FFN_HC)
        for i, x1b in enumerate(x1bs):
            h = jnp.maximum(_dot(x1b, w1_ref[:, cols]), 0.0)
            part = _dot((h * h).astype(BF16), w2_ref[cols, :])
            accs[i] = part if accs[i] is None else accs[i] + part
    for rows, x1, acc in zip(blocks, x1s, accs):
        out_ref[rows, :] = _layer_norm(DN_ALPHA * x1 + acc, g2_ref[...], b2_ref[...])


def _mix_ffn(x2d, acts, pre_w, wos, ln1_g, ln1_b, ln2_g, ln2_b, w1_all, w2_all, layer, name):
    n_tok = x2d.shape[0]
    tm = min(FFN_TM, n_tok)
    row = lambda w: pl.BlockSpec((tm, w), lambda i: (i, 0))
    has_pre = pre_w is not None
    in_specs = [row(D_MODEL)] + [row(a.shape[1]) for a in acts]
    args = [x2d] + list(acts)
    if has_pre:
        in_specs.append(_const_spec(pre_w.shape))
        args.append(pre_w)
    in_specs += [_const_spec(w.shape) for w in wos]
    args += list(wos)
    layer_spec = lambda w: pl.BlockSpec((None,) + w.shape[1:], lambda i: (layer, 0, 0), pipeline_mode=pl.Buffered(1))
    in_specs += [_const_spec((1, D_MODEL))] * 4 + [layer_spec(w1_all), layer_spec(w2_all)]
    args += [ln1_g, ln1_b, ln2_g, ln2_b, w1_all, w2_all]
    return pl.pallas_call(
        functools.partial(_mix_ffn_kernel, n_act=len(acts), has_pre=has_pre),
        grid=(n_tok // tm,),
        in_specs=in_specs,
        out_specs=row(D_MODEL),
        out_shape=jax.ShapeDtypeStruct((n_tok, D_MODEL), F32),
        compiler_params=_cparams(("arbitrary",)),
        name=name,
    )(*args)


def _even_inproj_sample_kernel(x_ref, w_ref, kvn_ref, gmg_ref, gmb_ref, ws0_ref, bs0_ref,
                               cos_ref, sina_ref, sinb_ref, qn_ref, wqn_ref, wqp_ref, wuk_ref,
                               ckv_ref, kpe_ref, gate_ref, vn_ref, qlat_ref, qpe_ref):
    z = _dot(x_ref[...].astype(BF16), w_ref[...])
    o_c, o_u, o_v, o_k = Q_RANK, Q_RANK + KV_RANK, Q_RANK + KV_RANK + GM_WIDTH, Q_RANK + KV_RANK + 2 * GM_WIDTH
    cos_t, sin_a, sin_b = cos_ref[...], sina_ref[...], sinb_ref[...]
    c_n = _rms_norm(z[:, o_c:o_u], kvn_ref[...])
    ckv_ref[...] = c_n
    kp = _rope128(z[:, o_k:o_k + LANES], cos_t, sin_a, sin_b)
    kpe_ref[...] = kp[:, :ROPE_DIM]
    gu = jax.nn.gelu(z[:, o_u:o_v])
    v_n = _layer_norm(jax.nn.gelu(z[:, o_v:o_k]), gmg_ref[...], gmb_ref[...])
    vn_ref[...] = v_n
    mixed = ws0_ref[...].astype(F32) * v_n.astype(BF16).astype(F32) + bs0_ref[...]
    gate_ref[...] = (gu * mixed).astype(BF16)
    lat, pes = _queries(z[:, :o_c], qn_ref[...], wqn_ref[...], wqp_ref[...], wuk_ref,
                        cos_t, sin_a, sin_b)
    qlat_ref[...] = lat
    qpe_ref[...] = jnp.concatenate(pes, axis=1)


def _even_inproj_sample(xs, w_in_p, kv_norm, gm_g, gm_b, ws0, bs0, cos_t, sin_a, sin_b,
                        qn_g, wq_nope, wq_pe, w_uk):
    n = xs.shape[0]
    args = (xs, w_in_p, kv_norm, gm_g, gm_b, ws0, bs0, cos_t, sin_a, sin_b, qn_g, wq_nope, wq_pe, w_uk)
    full = lambda shape: pl.BlockSpec(shape, lambda i: (0,) * len(shape))
    widths = [(KV_RANK, F32), (ROPE_DIM, F32), (GM_WIDTH, BF16), (GM_WIDTH, F32),
              (MLA_HEADS * LANES, F32), (MLA_HEADS * LANES, F32)]
    return pl.pallas_call(
        _even_inproj_sample_kernel,
        grid=(1,),
        in_specs=[full(a.shape) for a in args],
        out_specs=[full((n, w)) for w, _ in widths],
        out_shape=[jax.ShapeDtypeStruct((n, w), dt) for w, dt in widths],
        compiler_params=_cparams(("arbitrary",)),
        name="even_inproj_sample",
    )(*args)


def _decode_attn_kernel(pt_ref, qlat_ref, qpe_ref, ckvs_ref, kpes_ref, ckv_hbm, kpe_hbm, out_ref,
                        ckv_buf, kpe_buf, m_buf, sem, *, n_pg, n_groups):
    b = pl.program_id(0)
    n_b = pl.num_programs(0)
    assert n_groups % DEC_SLOTS == 0 and DEC_AHEAD + 2 <= DEC_SLOTS

    def group_copies(seq, grp):
        slot = grp % DEC_SLOTS
        cps = []
        for k in range(n_pg):
            page = pt_ref[seq, grp * n_pg + k]
            keys = pl.ds(k * PAGE_SIZE, PAGE_SIZE)
            cps.append(pltpu.make_async_copy(ckv_hbm.at[page], ckv_buf.at[slot, keys, :], sem.at[0, slot]))
            cps.append(pltpu.make_async_copy(kpe_hbm.at[page], kpe_buf.at[slot, :, keys], sem.at[1, slot]))
        return cps

    @pl.when(b == 0)
    def _():
        for g in range(DEC_AHEAD):
            for cp in group_copies(0, g):
                cp.start()

    heads, span = MLA_HEADS, DEC_SPAN_PAGES * PAGE_SIZE
    half_span = span // 2
    n_span = n_pg // DEC_SPAN_PAGES
    qlat = qlat_ref[...]
    zero = jnp.zeros_like(qlat)
    q2 = jnp.concatenate([jnp.concatenate([qlat, zero], axis=1),
                          jnp.concatenate([zero, qlat], axis=1)], axis=0).astype(BF16)
    qpe = qpe_ref[...][:, :ROPE_DIM].astype(BF16)

    def scores(slot):
        s_pe = _dot(qpe, kpe_buf[slot].astype(BF16))
        pieces = []
        for j in range(n_span):
            m_buf[slot, j] = jnp.concatenate(
                [ckv_buf[slot, pl.ds(j * span, half_span), :],
                 ckv_buf[slot, pl.ds(j * span + half_span, half_span), :]], axis=1).astype(BF16)
            s2 = _dot_nt(q2, m_buf[slot, j])
            pieces += [s2[:heads] + s_pe[:, j * span:j * span + half_span],
                       s2[heads:] + s_pe[:, j * span + half_span:(j + 1) * span]]
        return jnp.concatenate(pieces, axis=1) * ATTN_SCALE

    def values(p, grp):
        slot = grp % DEC_SLOTS
        o2 = None
        for j in range(n_span):
            lhs = jnp.concatenate([p[:, j * span:j * span + half_span],
                                   p[:, j * span + half_span:(j + 1) * span]], axis=0).astype(BF16)
            part = _dot(lhs, m_buf[slot, j])
            o2 = part if o2 is None else o2 + part
        return o2[:heads, :KV_RANK] + o2[heads:, KV_RANK:]

    state = dict(m_run=None, m=None, l=None, acc=None)
    raw = {}
    probs = {}

    def softmax(k):
        s = raw.pop(k)
        m_cur = jnp.max(s, axis=-1, keepdims=True)
        m_new = m_cur if state["m_run"] is None else jnp.maximum(state["m_run"], m_cur)
        state["m_run"] = m_new
        probs[k] = (jnp.exp(s - m_new), m_new)

    def fold(k):
        p, m_k = probs.pop(k)
        pv = values(p, k)
        p_sum = jnp.sum(p, axis=-1, keepdims=True)
        if state["m"] is None:
            state["l"], state["acc"] = p_sum, pv
        else:
            alpha = jnp.exp(state["m"] - m_k)
            state["l"], state["acc"] = alpha * state["l"] + p_sum, alpha * state["acc"] + pv
        state["m"] = m_k

    for g in range(n_groups):
        if g + DEC_AHEAD < n_groups:
            for cp in group_copies(b, g + DEC_AHEAD):
                cp.start()
        else:
            @pl.when(b + 1 < n_b)
            def _():
                for cp in group_copies(b + 1, g + DEC_AHEAD - n_groups):
                    cp.start()
        for cp in group_copies(b, g):
            cp.wait()
        raw[g] = scores(g % DEC_SLOTS)
        if g >= 1:
            softmax(g - 1)
        if g >= 2:
            fold(g - 2)
    softmax(n_groups - 1)
    fold(n_groups - 2)
    fold(n_groups - 1)
    m, l, acc = state["m"], state["l"], state["acc"]

    r = lambda a: a.astype(BF16).astype(F32)
    kv = r(ckvs_ref[...])
    s_self = (jnp.sum(r(qlat_ref[...]) * kv, axis=-1, keepdims=True)
              + jnp.sum(r(qpe_ref[...]) * r(kpes_ref[...]), axis=-1, keepdims=True)) * ATTN_SCALE
    m_n = jnp.maximum(m, s_self)
    a = jnp.exp(m - m_n)
    p_self = jnp.exp(s_self - m_n)
    out_ref[...] = (a * acc + r(p_self) * kv) / (a * l + p_self)


def _decode_attn(page_table, qlat2, qpe2, ckv_s3, kpe_s3, cache_ckv_e, cache_kpe_t):
    dec_batch, n_pages = page_table.shape
    n_pg = DEC_PG
    qspec = pl.BlockSpec((MLA_HEADS, LANES), lambda b, pt: (b, 0))
    self_spec = pl.BlockSpec((None, 1, LANES), lambda b, pt: (b, 0, 0))
    hbm = pl.BlockSpec(memory_space=pl.ANY)
    grid_spec = pltpu.PrefetchScalarGridSpec(
        num_scalar_prefetch=1,
        grid=(dec_batch,),
        in_specs=[qspec, qspec, self_spec, self_spec, hbm, hbm],
        out_specs=qspec,
        scratch_shapes=[pltpu.VMEM((DEC_SLOTS, n_pg * PAGE_SIZE, KV_RANK), F32),
                        pltpu.VMEM((DEC_SLOTS, ROPE_DIM, n_pg * PAGE_SIZE), F32),
                        pltpu.VMEM((DEC_SLOTS, n_pg // DEC_SPAN_PAGES, DEC_SPAN_PAGES // 2 * PAGE_SIZE, 2 * KV_RANK),
                                   BF16),
                        pltpu.SemaphoreType.DMA((2, DEC_SLOTS))],
    )
    return pl.pallas_call(
        functools.partial(_decode_attn_kernel, n_pg=n_pg, n_groups=n_pages // n_pg),
        grid_spec=grid_spec,
        out_shape=jax.ShapeDtypeStruct((dec_batch * MLA_HEADS, KV_RANK), F32),
        compiler_params=_cparams(("arbitrary",)),
        name="decode_attn",
    )(page_table, qlat2, qpe2, ckv_s3, kpe_s3, cache_ckv_e, cache_kpe_t)


def _s5_readout(hcat_blocks, u, cw_ref, d_ref, wglu_ref):
    y = jnp.concatenate([_dot(hcat_blocks[k], cw_ref[k]) for k in range(S5_NB)], axis=1)
    y = jax.nn.gelu(y + d_ref[...] * u)
    return y * jax.nn.sigmoid(_dot(y.astype(BF16), wglu_ref[...]))


def _s5_prompt_kernel(x_ref, win_ref, bw_ref, a_ref, cw_ref, d_ref, wglu_ref,
                      yg_ref, hre_ref, him_ref, u_s, y_s, bu_s, h_s):
    n_b, seg, _ = x_ref.shape
    half = S5_SLABS // 2
    sub = S5_SUB
    n_sub = seg // sub
    rows_sub = sub * n_b
    q_slabs = S5_WIDTH // LANES

    @pl.when(pl.program_id(0) == 0)
    def _():
        h_s[...] = jnp.zeros(h_s.shape, F32)

    def project(s):
        rows = slice(s * rows_sub, (s + 1) * rows_sub)
        xs = jnp.concatenate([x_ref[b, s * sub:(s + 1) * sub, :] for b in range(n_b)], axis=0)
        u = _dot(xs.astype(BF16), win_ref[...])
        for b in range(n_b):
            for q in range(q_slabs):
                u_s[s, q, pl.ds(b, sub, stride=n_b), :] = u[b * sub:(b + 1) * sub, q * LANES:(q + 1) * LANES]
        for k in range(S5_NB):
            r = _dot(u_s[s, k, :rows_sub, :].astype(BF16), bw_ref[k])
            for q in range(S5_BLK_SLABS):
                bu_s[S5_BLK_SLABS * k + q, rows, :] = r[:, q * LANES:(q + 1) * LANES]
                bu_s[half + S5_BLK_SLABS * k + q, rows, :] = r[:, (S5_BLK_SLABS + q) * LANES:
                                                                (S5_BLK_SLABS + q + 1) * LANES]

    def recur(s, h):
        for t in range(s * sub, (s + 1) * sub):
            rows = slice(t * n_b, (t + 1) * n_b)
            hs = bu_s[:, rows, :]
            ar, ai = a_ref[:half], a_ref[half:]
            hr, hi = h[:half], h[half:]
            h = jnp.concatenate([ar * hr - ai * hi + hs[:half], ar * hi + ai * hr + hs[half:]], axis=0)
            bu_s[:, rows, :] = h
        return h

    def read_out(s):
        rows = slice(s * rows_sub, (s + 1) * rows_sub)
        blocks = [jnp.concatenate([bu_s[S5_BLK_SLABS * k + q, rows, :] for q in range(S5_BLK_SLABS)]
                                  + [bu_s[half + S5_BLK_SLABS * k + q, rows, :] for q in range(S5_BLK_SLABS)],
                                  axis=1).astype(BF16)
                  for k in range(S5_NB)]
        u_t = jnp.concatenate([u_s[s, q, :rows_sub, :] for q in range(q_slabs)], axis=1)
        y = _s5_readout(blocks, u_t, cw_ref, d_ref, wglu_ref)
        for q in range(q_slabs):
            y_s[s, q, :rows_sub, :] = y[:, q * LANES:(q + 1) * LANES]
        for b in range(n_b):
            yg_ref[b, s * sub:(s + 1) * sub, :] = jnp.concatenate(
                [y_s[s, q, pl.ds(b, sub, stride=n_b), :] for q in range(q_slabs)], axis=1).astype(BF16)

    h = h_s[...]
    project(0)
    for s in range(n_sub):
        if s + 1 < n_sub:
            project(s + 1)
        h = recur(s, h)
        if s >= 1:
            read_out(s - 1)
    read_out(n_sub - 1)
    h_s[...] = h
    hre_ref[...] = jnp.concatenate([h[s] for s in range(half)], axis=1)
    him_ref[...] = jnp.concatenate([h[half + s] for s in range(half)], axis=1)


def _s5_prompt(x3, w_in, bw, a_b, cw, d_row, w_glu):
    batch, seq, _ = x3.shape
    seg = S5_L
    slab_rows = (seg // S5_SUB, S5_WIDTH // LANES, S5_SUB * batch + S5_SLAB_PAD, LANES)
    return pl.pallas_call(
        _s5_prompt_kernel,
        grid=(seq // seg,),
        in_specs=[pl.BlockSpec((batch, seg, D_MODEL), lambda c: (0, c, 0)),
                  _const_spec(w_in.shape), _const_spec(bw.shape), _const_spec(a_b.shape),
                  _const_spec(cw.shape), _const_spec(d_row.shape), _const_spec(w_glu.shape)],
        out_specs=[pl.BlockSpec((batch, seg, S5_WIDTH), lambda c: (0, c, 0)),
                   pl.BlockSpec((batch, S5_STATES), lambda c: (0, 0)),
                   pl.BlockSpec((batch, S5_STATES), lambda c: (0, 0))],
        out_shape=[jax.ShapeDtypeStruct((batch, seq, S5_WIDTH), BF16),
                   jax.ShapeDtypeStruct((batch, S5_STATES), F32),
                   jax.ShapeDtypeStruct((batch, S5_STATES), F32)],
        scratch_shapes=[pltpu.VMEM(slab_rows, F32), pltpu.VMEM(slab_rows, F32),
                        pltpu.VMEM((S5_SLABS, seg * batch + S5_SLAB_PAD, LANES), F32),
                        pltpu.VMEM((S5_SLABS, batch, LANES), F32)],
        compiler_params=_cparams(("arbitrary",)),
        name="s5_prompt",
    )(x3, w_in, bw, a_b, cw, d_row, w_glu)


def _s5_sample_kernel(x_ref, h0r_ref, h0i_ref, win_ref, bw_ref, ar_ref, ai_ref, cw_ref, d_ref, wglu_ref,
                      yg_ref, hre_ref, him_ref):
    u = _dot(x_ref[...].astype(BF16), win_ref[...])
    ub = u.astype(BF16)
    bu = [_dot(ub[:, k * LANES:(k + 1) * LANES], bw_ref[k]) for k in range(S5_NB)]
    w = S5_KB * S5_STATE
    bu_r = jnp.concatenate([r[:, :w] for r in bu], axis=1)
    bu_i = jnp.concatenate([r[:, w:] for r in bu], axis=1)
    ar, ai, h0r, h0i = ar_ref[...], ai_ref[...], h0r_ref[...], h0i_ref[...]
    hr = bu_r + (ar * h0r - ai * h0i)
    hi = bu_i + (ar * h0i + ai * h0r)
    hre_ref[...] = hr
    him_ref[...] = hi
    blocks = [jnp.concatenate([hr[:, k * w:(k + 1) * w], hi[:, k * w:(k + 1) * w]], axis=1).astype(BF16)
              for k in range(S5_NB)]
    yg_ref[...] = _s5_readout(blocks, u, cw_ref, d_ref, wglu_ref).astype(BF16)


def _s5_sample(xs, h0r, h0i, w_in, bw, a_r, a_i, cw, d_row, w_glu):
    n = xs.shape[0]
    args = (xs, h0r, h0i, w_in, bw, a_r, a_i, cw, d_row, w_glu)
    full = lambda shape: pl.BlockSpec(shape, lambda i: (0,) * len(shape))
    return pl.pallas_call(
        _s5_sample_kernel,
        grid=(1,),
        in_specs=[full(a.shape) for a in args],
        out_specs=[full((n, S5_WIDTH)), full((n, S5_STATES)), full((n, S5_STATES))],
        out_shape=[jax.ShapeDtypeStruct((n, S5_WIDTH), BF16),
                   jax.ShapeDtypeStruct((n, S5_STATES), F32),
                   jax.ShapeDtypeStruct((n, S5_STATES), F32)],
        compiler_params=_cparams(("arbitrary",)),
        name="s5_sample",
    )(*args)


def _rope_tables(pos):
    freqs = ROPE_THETA ** (-jnp.arange(ROPE_HALF, dtype=F32) / ROPE_HALF)
    ang = pos.astype(F32)[:, None] * freqs
    cos, sin = jnp.cos(ang), jnp.sin(ang)
    zero = jnp.zeros_like(cos)
    pad = jnp.zeros((pos.shape[0], LANES - ROPE_DIM), F32)
    cos_t = jnp.concatenate([cos, cos, pad], axis=1)
    sin_a = jnp.concatenate([zero, sin, pad], axis=1)
    sin_b = jnp.concatenate([-sin, zero, pad], axis=1)
    return cos_t, sin_a, sin_b


def _block_diag(blocks):
    *lead, n, r, c = blocks.shape
    eye = jnp.eye(n, dtype=blocks.dtype)
    return jnp.einsum('...nrc,nm->...nrmc', blocks, eye).reshape(*lead, n * r, n * c)


def _even_params(w_in, q_norm, w_q_b, kv_norm, w_kv_b, gm_g, gm_b, gm_w_s, gm_b_s, w_out):
    i1, i2, i3, i4 = Q_RANK, Q_RANK + KV_RANK, Q_RANK + KV_RANK + ROPE_DIM, Q_RANK + KV_RANK + ROPE_DIM + GM_WIDTH
    w_in_p = jnp.concatenate([w_in[:, :i2], w_in[:, i3:], w_in[:, i2:i3],
                              jnp.zeros((D_MODEL, LANES - ROPE_DIM), F32)], axis=1).astype(BF16)
    wq_nope = w_q_b[:, :, :NOPE_DIM].reshape(Q_RANK, MLA_HEADS * NOPE_DIM).astype(BF16)
    pe = w_q_b[:, :, NOPE_DIM:]
    pe_sw = jnp.concatenate([pe[:, :, ROPE_HALF:], pe[:, :, :ROPE_HALF]], axis=2)
    lane_pad = jnp.zeros((Q_RANK, MLA_HEADS, LANES - ROPE_DIM), F32)
    wq_pe = jnp.concatenate(
        [jnp.concatenate([w, lane_pad], axis=2).reshape(Q_RANK, MLA_HEADS * LANES) for w in (pe, pe_sw)],
        axis=1).astype(BF16)
    uk = jnp.transpose(w_kv_b[:, :, :NOPE_DIM], (1, 2, 0))
    uv = jnp.transpose(w_kv_b[:, :, NOPE_DIM:], (1, 0, 2))
    pairs = MLA_HEADS // 2
    w_uk = _block_diag(uk.reshape(pairs, 2, NOPE_DIM, KV_RANK)).astype(BF16)
    w_uv_p = _block_diag(uv.reshape(pairs, 2, KV_RANK, V_DIM)).astype(BF16)
    w_uv = _block_diag(uv).astype(BF16)
    causal = jnp.tril(jnp.ones((GM_CHUNK, GM_CHUNK), dtype=bool))
    ws_m = jnp.where(causal[None], gm_w_s, 0).astype(BF16)
    bs_full = jnp.repeat(gm_b_s.T, GM_HEAD_DIM, axis=1)
    ws0 = jnp.repeat(gm_w_s[:, 0, 0], GM_HEAD_DIM)[None, :].astype(BF16)
    bs0 = bs_full[:1]
    n_attn = MLA_HEADS * V_DIM
    return dict(w_in_p=w_in_p, qn=q_norm[None, :], wq_nope=wq_nope, wq_pe=wq_pe, w_uk=w_uk, w_uv=w_uv, w_uv_p=w_uv_p,
                kvn=kv_norm[None, :], gm_g=gm_g[None, :], gm_b=gm_b[None, :], ws_m=ws_m, bs_full=bs_full,
                ws0=ws0, bs0=bs0, wo_attn=w_out[:n_attn].astype(BF16), wo_gate=w_out[n_attn:].astype(BF16))


def _s5_params(w_in, a_re, a_im, b_re, b_im, c_re, c_im, d, log_dt, w_glu, w_out, batch):
    dt = jnp.exp(log_dt)[:, None]
    ld_r, ld_i = a_re * dt, a_im * dt
    mag = jnp.exp(ld_r)
    ab_r, ab_i = mag * jnp.cos(ld_i), mag * jnp.sin(ld_i)
    den = a_re * a_re + a_im * a_im
    cr = ((ab_r - 1.0) * a_re + ab_i * a_im) / den
    ci = (ab_i * a_re - (ab_r - 1.0) * a_im) / den
    bb_r = cr[..., None] * b_re - ci[..., None] * b_im
    bb_i = cr[..., None] * b_im + ci[..., None] * b_re

    def in_blocks(bb):
        return _block_diag(jnp.swapaxes(bb, 1, 2).reshape(S5_NB, S5_KB, S5_GROUP_DIM, S5_STATE))

    def out_blocks(cc):
        return _block_diag(jnp.swapaxes(cc, 1, 2).reshape(S5_NB, S5_KB, S5_STATE, S5_GROUP_DIM))

    bw = jnp.concatenate([in_blocks(bb_r), in_blocks(bb_i)], axis=2).astype(BF16)
    cw = jnp.concatenate([out_blocks(c_re), out_blocks(-c_im)], axis=1).astype(BF16)
    a_r, a_i = ab_r.reshape(1, S5_STATES), ab_i.reshape(1, S5_STATES)
    half = S5_SLABS // 2
    a_b = jnp.concatenate([jnp.broadcast_to(a_r.reshape(half, 1, LANES), (half, batch, LANES)),
                           jnp.broadcast_to(a_i.reshape(half, 1, LANES), (half, batch, LANES))], axis=0)
    return dict(w_in=w_in.astype(BF16), bw=bw, cw=cw, a_r=a_r, a_i=a_i, a_b=a_b,
                d=d.reshape(1, S5_WIDTH), w_glu=w_glu.astype(BF16), w_out=w_out.astype(BF16))


def kernel(x_prompt, x_sample, cache_ckv, cache_kpe, state_s5_re, state_s5_im, page_table, w_in_even, mla_q_norm, mla_w_q_b, mla_kv_norm, mla_w_kv_b, gm_norm_g, gm_norm_b, gm_w_s, gm_b_s, w_out_even, w_in_odd, s5_a_re, s5_a_im, s5_b_re, s5_b_im, s5_c_re, s5_c_im, s5_d, s5_log_dt, s5_w_glu, w_out_odd, ln_mix_g, ln_mix_b, ln_ffn_g, ln_ffn_b, ffn_w1, ffn_w2):
    batch, seq, _ = x_prompt.shape
    dec_batch, dec_seq, _ = x_sample.shape
    assert dec_seq == 1 and seq % INPROJ_TM == 0 and seq % ATTN_TQ == 0 and seq % S5_L == 0
    assert (seq // ATTN_TQ) % 2 == 0
    assert (batch * seq) % FFN_TM == 0 and page_table.shape[1] % DEC_PG == 0
    assert batch == SUBLANES

    xp = x_prompt.reshape(batch * seq, D_MODEL)
    xs = x_sample.reshape(dec_batch, D_MODEL)
    tabs_p = _rope_tables(jnp.arange(seq, dtype=jnp.int32))
    tabs_s = _rope_tables(PAST_LEN + jnp.arange(dec_seq, dtype=jnp.int32))

    outs = {k: [] for k in ("ckv_p", "kpe_p", "ckv_s", "kpe_s", "gmv_p", "gmv_s",
                            "s5re_p", "s5im_p", "s5re_s", "s5im_s")}
    ffn_w1_b, ffn_w2_b = ffn_w1.astype(BF16), ffn_w2.astype(BF16)
    for layer in range(DEPTH):
        ln = (ln_mix_g[layer][None, :], ln_mix_b[layer][None, :],
              ln_ffn_g[layer][None, :], ln_ffn_b[layer][None, :],
              ffn_w1_b, ffn_w2_b, layer)
        if layer % 2 == 0:
            e = layer // 2
            p = _even_params(w_in_even[e], mla_q_norm[e], mla_w_q_b[e], mla_kv_norm[e], mla_w_kv_b[e],
                             gm_norm_g[e], gm_norm_b[e], gm_w_s[e], gm_b_s[e], w_out_even[e])
            q_a, ckv, kpe, kcat, gate, gmv = _even_inproj(
                xp, p["w_in_p"], p["kvn"], p["gm_g"], p["gm_b"], p["ws_m"], p["bs_full"], *tabs_p, batch, seq)
            attn = _mla_attn(q_a, kcat, p["qn"], p["wq_nope"], p["wq_pe"], p["w_uk"], p["w_uv_p"], *tabs_p,
                             batch, seq)
            xp = _mix_ffn(xp, [attn, gate], None, [p["wo_attn"], p["wo_gate"]], *ln, name="even_ffn_prompt")
            outs["ckv_p"].append(ckv.reshape(batch, seq, KV_RANK))
            outs["kpe_p"].append(kpe.reshape(batch, seq, ROPE_DIM))
            outs["gmv_p"].append(gmv)
            ckv_s, kpe_s, gate_s, vn_s, qlat_s, qpe_s = _even_inproj_sample(
                xs, p["w_in_p"], p["kvn"], p["gm_g"], p["gm_b"], p["ws0"], p["bs0"], *tabs_s,
                p["qn"], p["wq_nope"], p["wq_pe"], p["w_uk"])
            kpe_pad = jnp.concatenate([kpe_s, jnp.zeros((dec_batch, LANES - ROPE_DIM), F32)], axis=1)
            o_lat = _decode_attn(page_table,
                                 qlat_s.reshape(dec_batch * MLA_HEADS, LANES),
                                 qpe_s.reshape(dec_batch * MLA_HEADS, LANES),
                                 ckv_s.reshape(dec_batch, 1, KV_RANK), kpe_pad.reshape(dec_batch, 1, LANES),
                                 cache_ckv[e], jnp.swapaxes(cache_kpe[e], 1, 2))
            xs = _mix_ffn(xs, [o_lat.reshape(dec_batch, MLA_HEADS * KV_RANK), gate_s], p["w_uv"],
                          [p["wo_attn"], p["wo_gate"]], *ln, name="even_ffn_sample")
            outs["ckv_s"].append(ckv_s.reshape(dec_batch, 1, KV_RANK))
            outs["kpe_s"].append(kpe_s.reshape(dec_batch, 1, ROPE_DIM))
            outs["gmv_s"].append(vn_s.reshape(dec_batch, 1, GM_WIDTH))
        else:
            o = layer // 2
            p = _s5_params(w_in_odd[o], s5_a_re[o], s5_a_im[o], s5_b_re[o], s5_b_im[o], s5_c_re[o],
                           s5_c_im[o], s5_d[o], s5_log_dt[o], s5_w_glu[o], w_out_odd[o], batch)
            yg, hre, him = _s5_prompt(xp.reshape(batch, seq, D_MODEL), p["w_in"], p["bw"], p["a_b"],
                                      p["cw"], p["d"], p["w_glu"])
            xp = _mix_ffn(xp, [yg.reshape(batch * seq, S5_WIDTH)], None, [p["w_out"]], *ln,
                          name="odd_ffn_prompt")
            outs["s5re_p"].append(hre.reshape(batch, S5_GROUPS, S5_STATE))
            outs["s5im_p"].append(him.reshape(batch, S5_GROUPS, S5_STATE))
            yg_s, hre_s, him_s = _s5_sample(
                xs, state_s5_re[o].reshape(dec_batch, S5_STATES), state_s5_im[o].reshape(dec_batch, S5_STATES),
                p["w_in"], p["bw"], p["a_r"], p["a_i"], p["cw"], p["d"], p["w_glu"])
            xs = _mix_ffn(xs, [yg_s], None, [p["w_out"]], *ln, name="odd_ffn_sample")
            outs["s5re_s"].append(hre_s.reshape(dec_batch, S5_GROUPS, S5_STATE))
            outs["s5im_s"].append(him_s.reshape(dec_batch, S5_GROUPS, S5_STATE))

    st = jnp.stack
    return (xp.reshape(batch, seq, D_MODEL), xs.reshape(dec_batch, dec_seq, D_MODEL),
            st(outs["ckv_p"]), st(outs["kpe_p"]), st(outs["ckv_s"]), st(outs["kpe_s"]),
            st(outs["gmv_p"]), st(outs["gmv_s"]),
            st(outs["s5re_p"]), st(outs["s5im_p"]), st(outs["s5re_s"]), st(outs["s5im_s"]))
```

```python
import functools
import math

import jax
import jax.numpy as jnp
from jax import lax
from jax.experimental import pallas as pl
from jax.experimental.pallas import tpu as pltpu

F32 = jnp.float32
BF16 = jnp.bfloat16

D_MODEL = 1024
DEPTH = 2
PAST_LEN = 16384
PAGE_SIZE = 128
MLA_HEADS = 8
NOPE_DIM = 64
ROPE_DIM = 32
ROPE_HALF = ROPE_DIM // 2
V_DIM = 64
Q_RANK = 256
KV_RANK = 128
ROPE_THETA = 10000.0
ATTN_SCALE = 1.0 / math.sqrt(NOPE_DIM + ROPE_DIM)
ATTN_SCALE_LOG2 = ATTN_SCALE * math.log2(math.e)
GM_HEADS = 8
GM_HEAD_DIM = 64
GM_WIDTH = GM_HEADS * GM_HEAD_DIM
GM_CHUNK = 128
S5_GROUP_DIM = 16
S5_GROUPS = 32
S5_STATE = 64
S5_WIDTH = S5_GROUPS * S5_GROUP_DIM
S5_STATES = S5_GROUPS * S5_STATE
FFN_HIDDEN = 4 * D_MODEL
DN_ALPHA = (2 * DEPTH) ** 0.25
LN_EPS = 1e-5
RMS_EPS = 1e-6

LANES = 128
SUBLANES = 8
VMEM_LIMIT_BYTES = 56 * 1024 * 1024

INPROJ_TM = 512
INPROJ_ROW_BLOCK = 256
ATTN_TQ = 256
ATTN_ROW_BLOCK = 256
FFN_TM = 1024
FFN_HC = 1024
FFN_ROW_BLOCK = 256
S5_L = 128
S5_SUB = 32
DEC_PG = 32
DEC_SLOTS = 4
DEC_AHEAD = 2
DEC_SPAN_PAGES = 4
S5_KB = 8
S5_NB = S5_GROUPS // S5_KB
S5_SLABS = 2 * S5_STATES // LANES
S5_BLK_SLABS = S5_KB * S5_STATE // LANES


def _cparams(sem):
    return pltpu.CompilerParams(dimension_semantics=sem, vmem_limit_bytes=VMEM_LIMIT_BYTES)


def _const_spec(shape):
    n = len(shape)
    return pl.BlockSpec(shape, lambda *_: (0,) * n, pipeline_mode=pl.Buffered(1))


def _layer_norm(x, g, b):
    mu = jnp.mean(x, axis=-1, keepdims=True)
    xc = x - mu
    var = jnp.mean(xc * xc, axis=-1, keepdims=True)
    return xc * lax.rsqrt(var + LN_EPS) * g + b


def _rms_norm(x, g):
    return x * lax.rsqrt(jnp.mean(x * x, axis=-1, keepdims=True) + RMS_EPS) * g


def _rope128(x, cos_t, sin_a, sin_b):
    return (x * cos_t + pltpu.roll(x, ROPE_HALF, 1) * sin_a
            + pltpu.roll(x, LANES - ROPE_HALF, 1) * sin_b)


def _dot(a, b):
    return jnp.dot(a, b, preferred_element_type=F32)


def _dot_nt(a, b):
    return lax.dot_general(a, b, (((1,), (1,)), ((), ())), preferred_element_type=F32)


def _even_inproj_kernel(x_ref, w_ref, kvn_ref, gmg_ref, gmb_ref, ws_ref, bs_ref,
                        cos_ref, sina_ref, sinb_ref,
                        qa_ref, ckv_ref, kpe_ref, kcat_ref, gate_ref, gmv_ref):
    tm = x_ref.shape[0]
    o_c, o_u, o_v, o_k = Q_RANK, Q_RANK + KV_RANK, Q_RANK + KV_RANK + GM_WIDTH, Q_RANK + KV_RANK + 2 * GM_WIDTH
    rb = min(INPROJ_ROW_BLOCK, tm)
    blocks = [slice(r0, r0 + rb) for r0 in range(0, tm, rb)]
    zs = [_dot(x_ref[rows, :].astype(BF16), w_ref[...]) for rows in blocks]

    gus, vns = [], []
    for rows, z in zip(blocks, zs):
        qa_ref[rows, :] = z[:, :o_c]
        c_n = _rms_norm(z[:, o_c:o_u], kvn_ref[...])
        ckv_ref[rows, :] = c_n
        kp = _rope128(z[:, o_k:o_k + LANES], cos_ref[rows, :], sina_ref[rows, :], sinb_ref[rows, :])
        kpe_ref[rows, :] = kp[:, :ROPE_DIM]
        kcat_ref[rows, :] = jnp.concatenate([c_n, kp], axis=1).astype(BF16)
        gus.append(jax.nn.gelu(z[:, o_u:o_v]))
        vns.append(_layer_norm(jax.nn.gelu(z[:, o_v:o_k]), gmg_ref[...], gmb_ref[...]))

    gmv_ref[0] = vns[-1][rb - GM_CHUNK:, :]

    lane = lax.broadcasted_iota(jnp.int32, (GM_CHUNK, LANES), 1)
    lo = lane < GM_HEAD_DIM
    bs = bs_ref[...]
    for rows, gu, v_n in zip(blocks, gus, vns):
        for ci in range(rb // GM_CHUNK):
            chunk = slice(ci * GM_CHUNK, (ci + 1) * GM_CHUNK)
            pieces = []
            for pr in range(GM_WIDTH // LANES):
                r = v_n[chunk, pr * LANES:(pr + 1) * LANES]
                m0 = _dot(ws_ref[2 * pr], jnp.where(lo, r, 0.0).astype(BF16))
                m1 = _dot(ws_ref[2 * pr + 1], jnp.where(lo, 0.0, r).astype(BF16))
                pieces.append(m0 + m1)
            mixed = jnp.concatenate(pieces, axis=1) + bs
            gate_ref[pl.ds(rows.start + ci * GM_CHUNK, GM_CHUNK), :] = (gu[chunk, :] * mixed).astype(BF16)


def _even_inproj(x2d, w_in_p, kv_norm, gm_g, gm_b, ws_m, bs_full, cos_t, sin_a, sin_b, batch, seq):
    n_tok = x2d.shape[0]
    tm = INPROJ_TM
    tiles_per_seq = seq // tm
    row = lambda w: pl.BlockSpec((tm, w), lambda i: (i, 0))
    tab = pl.BlockSpec((tm, LANES), lambda i: (i % tiles_per_seq, 0))
    n_in = w_in_p.shape[1]
    return pl.pallas_call(
        _even_inproj_kernel,
        grid=(n_tok // tm,),
        in_specs=[row(D_MODEL), _const_spec((D_MODEL, n_in)), _const_spec((1, KV_RANK)),
                  _const_spec((1, GM_WIDTH)), _const_spec((1, GM_WIDTH)),
                  _const_spec((GM_HEADS, GM_CHUNK, GM_CHUNK)), _const_spec((GM_CHUNK, GM_WIDTH)),
                  tab, tab, tab],
        out_specs=[row(Q_RANK), row(KV_RANK), row(ROPE_DIM), row(2 * LANES), row(GM_WIDTH),
                   pl.BlockSpec((1, GM_CHUNK, GM_WIDTH), lambda i: (i // tiles_per_seq, 0, 0))],
        out_shape=[jax.ShapeDtypeStruct((n_tok, Q_RANK), F32),
                   jax.ShapeDtypeStruct((n_tok, KV_RANK), F32),
                   jax.ShapeDtypeStruct((n_tok, ROPE_DIM), F32),
                   jax.ShapeDtypeStruct((n_tok, 2 * LANES), BF16),
                   jax.ShapeDtypeStruct((n_tok, GM_WIDTH), BF16),
                   jax.ShapeDtypeStruct((batch, GM_CHUNK, GM_WIDTH), F32)],
        compiler_params=_cparams(("arbitrary",)),
        name="even_inproj",
    )(x2d, w_in_p, kv_norm, gm_g, gm_b, ws_m, bs_full, cos_t, sin_a, sin_b)


def _queries(q_a, qn_g, wq_nope, wq_pe, wuk_ref, cos_t, sin_a, sin_b):
    qn = _rms_norm(q_a, qn_g).astype(BF16)
    nope = _dot(qn, wq_nope).astype(BF16)
    lat = jnp.concatenate([_dot(nope[:, pr * LANES:(pr + 1) * LANES], wuk_ref[pr])
                           for pr in range(MLA_HEADS // 2)], axis=1)
    pe = _dot(qn, wq_pe)
    n = MLA_HEADS * LANES
    sin_s = sin_a + sin_b
    pes = [pe[:, h * LANES:(h + 1) * LANES] * cos_t + pe[:, n + h * LANES:n + (h + 1) * LANES] * sin_s
           for h in range(MLA_HEADS)]
    return lat, pes


def _mla_attn_kernel(qa_lo_ref, qa_hi_ref, kcat_ref, qn_ref, wqn_ref, wqp_ref, wuk_ref, wuv_ref,
                     cos_lo_ref, sina_lo_ref, sinb_lo_ref, cos_hi_ref, sina_hi_ref, sinb_hi_ref,
                     out_hbm, qcat_s, m_s, acc_s, obuf, osem, *, nq):
    tq = qa_lo_ref.shape[0]
    n_rows = MLA_HEADS * tq
    b, i = pl.program_id(0), pl.program_id(1)
    step = b * pl.num_programs(1) + i
    n_steps = pl.num_programs(0) * pl.num_programs(1)
    q_tiles = (i, nq - 1 - i)

    def out_copies():
        return [pltpu.make_async_copy(
            obuf.at[t], out_hbm.at[pl.ds(pl.multiple_of((b * nq + q_tiles[t]) * tq, tq), tq), :], osem.at[t])
            for t in range(2)]

    qa = (qa_lo_ref, qa_hi_ref)
    tabs = ((cos_lo_ref, sina_lo_ref, sinb_lo_ref), (cos_hi_ref, sina_hi_ref, sinb_hi_ref))
    for t in range(2):
        lat, pes = _queries(qa[t][...], qn_ref[...], wqn_ref[...], wqp_ref[...], wuk_ref,
                            *(r[...] for r in tabs[t]))
        for h in range(MLA_HEADS):
            qcat_s[t, h * tq:(h + 1) * tq, :] = (ATTN_SCALE_LOG2 * jnp.concatenate(
                [lat[:, h * LANES:(h + 1) * LANES], pes[h]], axis=1)).astype(BF16)

    lanes_x = lambda a, n: jnp.concatenate([a] * n, axis=1)
    rb = ATTN_ROW_BLOCK
    ones = jnp.ones((tq, LANES), BF16)

    def keys(kv_tile):
        k = kcat_ref[pl.ds(pl.multiple_of(kv_tile * tq, tq), tq), :]
        return k, jnp.concatenate([k[:, :KV_RANK], ones], axis=1)

    q_pos = lax.broadcasted_iota(jnp.int32, (rb, tq), 0) & (tq - 1)
    k_pos = lax.broadcasted_iota(jnp.int32, (rb, tq), 1)
    causal = k_pos <= q_pos
    for t in range(2):
        k, v1 = keys(q_tiles[t])
        for r0 in range(0, n_rows, rb):
            rows = slice(r0, r0 + rb)
            s = jnp.where(causal, _dot_nt(qcat_s[t, rows, :], k), -jnp.inf)
            m0 = jnp.broadcast_to(jnp.max(s, axis=-1, keepdims=True), (rb, LANES))
            m_s[t, rows, :] = m0
            acc_s[t, rows, :] = _dot(jnp.exp2(s - lanes_x(m0, tq // LANES)).astype(BF16), v1)

    for u in range(nq - 1):
        hi = u >= i
        t = hi.astype(jnp.int32)
        k, v1 = keys(jnp.where(hi, u - i, u))
        for r0 in range(0, n_rows, rb):
            rows = slice(r0, r0 + rb)
            s = _dot_nt(qcat_s[t, rows, :], k)
            m_old = m_s[t, rows, :]
            m_new = jnp.maximum(m_old, jnp.max(s, axis=-1, keepdims=True))
            alpha = jnp.exp2(m_old - m_new)
            p = jnp.exp2(s - lanes_x(m_new, tq // LANES))
            acc_s[t, rows, :] = lanes_x(alpha, 2) * acc_s[t, rows, :] + _dot(p.astype(BF16), v1)
            m_s[t, rows, :] = m_new

    @pl.when(step > 0)
    def _():
        for cp in out_copies():
            cp.wait()

    for t in range(2):
        o_all = jnp.concatenate(
            [acc_s[t, h * tq:(h + 1) * tq, :KV_RANK] / acc_s[t, h * tq:(h + 1) * tq, KV_RANK:]
             for h in range(MLA_HEADS)], axis=1).astype(BF16)
        obuf[t] = jnp.concatenate(
            [_dot(o_all[:, pr * 2 * LANES:(pr + 1) * 2 * LANES], wuv_ref[pr]) for pr in range(MLA_HEADS // 2)],
            axis=1).astype(BF16)
    for cp in out_copies():
        cp.start()

    @pl.when(step == n_steps - 1)
    def _():
        for cp in out_copies():
            cp.wait()


def _mla_attn(q_a, kcat, qn_g, wq_nope, wq_pe, w_uk, w_uv, cos_t, sin_a, sin_b, batch, seq):
    tq = ATTN_TQ
    nq = seq // tq
    n_attn = MLA_HEADS * V_DIM
    lo = lambda b, i: i
    hi = lambda b, i: nq - 1 - i
    tab = lambda sel: pl.BlockSpec((tq, LANES), lambda b, i: (sel(b, i), 0))
    qa_spec = lambda sel: pl.BlockSpec((tq, Q_RANK), lambda b, i: (b * nq + sel(b, i), 0))
    return pl.pallas_call(
        functools.partial(_mla_attn_kernel, nq=nq),
        grid=(batch, nq // 2),
        in_specs=[qa_spec(lo), qa_spec(hi),
                  pl.BlockSpec((None, seq, 2 * LANES), lambda b, i: (b, 0, 0)),
                  _const_spec((1, Q_RANK)), _const_spec(wq_nope.shape), _const_spec(wq_pe.shape),
                  _const_spec(w_uk.shape), _const_spec(w_uv.shape),
                  tab(lo), tab(lo), tab(lo), tab(hi), tab(hi), tab(hi)],
        out_specs=pl.BlockSpec(memory_space=pl.ANY),
        out_shape=jax.ShapeDtypeStruct((batch * seq, n_attn), BF16),
        scratch_shapes=[pltpu.VMEM((2, MLA_HEADS * tq, 2 * LANES), BF16),
                        pltpu.VMEM((2, MLA_HEADS * tq, LANES), F32),
                        pltpu.VMEM((2, MLA_HEADS * tq, 2 * KV_RANK), F32),
                        pltpu.VMEM((2, tq, n_attn), BF16),
                        pltpu.SemaphoreType.DMA((2,))],
        compiler_params=_cparams(("arbitrary", "arbitrary")),
        name="mla_attn",
    )(q_a, q_a, kcat.reshape(batch, seq, 2 * LANES), qn_g, wq_nope, wq_pe, w_uk, w_uv,
      cos_t, sin_a, sin_b, cos_t, sin_a, sin_b)


def _mix_ffn_kernel(*refs, n_act, has_pre):
    x_ref = refs[0]
    act_refs = refs[1:1 + n_act]
    k = 1 + n_act
    pre_ref = refs[k] if has_pre else None
    k += int(has_pre)
    wo_refs = refs[k:k + n_act]
    k += n_act
    g1_ref, b1_ref, g2_ref, b2_ref, w1_ref, w2_ref, out_ref = refs[k:k + 7]

    tm = x_ref.shape[0]
    rb = min(FFN_ROW_BLOCK, tm)
    blocks = [slice(r0, r0 + rb) for r0 in range(0, tm, rb)]
    fs = []
    for rows in blocks:
        f = None
        for idx in range(n_act):
            a = act_refs[idx][rows, :]
            if has_pre and idx == 0:
                a = _dot(a.astype(BF16), pre_ref[...])
            part = _dot(a.astype(BF16), wo_refs[idx][...])
            f = part if f is None else f + part
        fs.append(f)
    x1s = [_layer_norm(DN_ALPHA * x_ref[rows, :] + f, g1_ref[...], b1_ref[...]) for rows, f in zip(blocks, fs)]
    x1bs = [x1.astype(BF16) for x1 in x1s]
    accs = [None] * len(blocks)
    for c in range(FFN_HIDDEN // FFN_HC):
        cols = slice(c * FFN_HC, (c + 1) * FFN_HC)
        for i, x1b in enumerate(x1bs):
            h = jnp.maximum(_dot(x1b, w1_ref[:, cols]), 0.0)
            part = _dot((h * h).astype(BF16), w2_ref[cols, :])
            accs[i] = part if accs[i] is None else accs[i] + part
    for rows, x1, acc in zip(blocks, x1s, accs):
        out_ref[rows, :] = _layer_norm(DN_ALPHA * x1 + acc, g2_ref[...], b2_ref[...])


def _mix_ffn(x2d, acts, pre_w, wos, ln1_g, ln1_b, ln2_g, ln2_b, w1, w2, name):
    n_tok = x2d.shape[0]
    tm = min(FFN_TM, n_tok)
    row = lambda w: pl.BlockSpec((tm, w), lambda i: (i, 0))
    has_pre = pre_w is not None
    in_specs = [row(D_MODEL)] + [row(a.shape[1]) for a in acts]
    args = [x2d] + list(acts)
    if has_pre:
        in_specs.append(_const_spec(pre_w.shape))
        args.append(pre_w)
    in_specs += [_const_spec(w.shape) for w in wos]
    args += list(wos)
    in_specs += [_const_spec((1, D_MODEL))] * 4 + [_const_spec(w1.shape), _const_spec(w2.shape)]
    args += [ln1_g, ln1_b, ln2_g, ln2_b, w1, w2]
    return pl.pallas_call(
        functools.partial(_mix_ffn_kernel, n_act=len(acts), has_pre=has_pre),
        grid=(n_tok // tm,),
        in_specs=in_specs,
        out_specs=row(D_MODEL),
        out_shape=jax.ShapeDtypeStruct((n_tok, D_MODEL), F32),
        compiler_params=_cparams(("arbitrary",)),
        name=name,
    )(*args)


def _mix_ffn_sample_kernel(*refs, n_act, has_pre):
    x_ref = refs[0]
    act_refs = refs[1:1 + n_act]
    k = 1 + n_act
    pre_ref = refs[k] if has_pre else None
    k += int(has_pre)
    wo_refs = refs[k:k + n_act]
    k += n_act
    g1_ref, b1_ref, g2_ref, b2_ref, w1_ref, w2_ref, out_ref, w1b_ref, w2b_ref, x1_s, acc_s = refs[k:k + 11]
    c = pl.program_id(0)

    @pl.when(c == 0)
    def _():
        f = None
        for idx in range(n_act):
            a = act_refs[idx][...]
            if has_pre and idx == 0:
                a = _dot(a.astype(BF16), pre_ref[...])
            part = _dot(a.astype(BF16), wo_refs[idx][...])
            f = part if f is None else f + part
        x1_s[...] = _layer_norm(DN_ALPHA * x_ref[...] + f, g1_ref[...], b1_ref[...])
        acc_s[...] = jnp.zeros(acc_s.shape, F32)

    w1c = w1_ref[...].astype(BF16)
    w2c = w2_ref[...].astype(BF16)
    w1b_ref[...] = w1c
    w2b_ref[...] = w2c
    h = jnp.maximum(_dot(x1_s[...].astype(BF16), w1c), 0.0)
    acc_s[...] += _dot((h * h).astype(BF16), w2c)

    @pl.when(c == pl.num_programs(0) - 1)
    def _():
        out_ref[...] = _layer_norm(DN_ALPHA * x1_s[...] + acc_s[...], g2_ref[...], b2_ref[...])


def _mix_ffn_sample(x2d, acts, pre_w, wos, ln1_g, ln1_b, ln2_g, ln2_b, w1_all, w2_all, layer, name):
    n_tok = x2d.shape[0]
    has_pre = pre_w is not None
    full = lambda a: pl.BlockSpec(a.shape, lambda c: (0,) * a.ndim)
    args = [x2d] + list(acts) + ([pre_w] if has_pre else []) + list(wos) + [ln1_g, ln1_b, ln2_g, ln2_b]
    in_specs = [full(a) for a in args]
    in_specs += [pl.BlockSpec((None, D_MODEL, FFN_HC), lambda c: (layer, 0, c)),
                 pl.BlockSpec((None, FFN_HC, D_MODEL), lambda c: (layer, c, 0))]
    args += [w1_all, w2_all]
    return pl.pallas_call(
        functools.partial(_mix_ffn_sample_kernel, n_act=len(acts), has_pre=has_pre),
        grid=(FFN_HIDDEN // FFN_HC,),
        in_specs=in_specs,
        out_specs=[pl.BlockSpec((n_tok, D_MODEL), lambda c: (0, 0)),
                   pl.BlockSpec((D_MODEL, FFN_HC), lambda c: (0, c)),
                   pl.BlockSpec((FFN_HC, D_MODEL), lambda c: (c, 0))],
        out_shape=[jax.ShapeDtypeStruct((n_tok, D_MODEL), F32),
                   jax.ShapeDtypeStruct((D_MODEL, FFN_HIDDEN), BF16),
                   jax.ShapeDtypeStruct((FFN_HIDDEN, D_MODEL), BF16)],
        scratch_shapes=[pltpu.VMEM((n_tok, D_MODEL), F32), pltpu.VMEM((n_tok, D_MODEL), F32)],
        compiler_params=_cparams(("arbitrary",)),
        name=name,
    )(*args)


def _even_inproj_sample_kernel(x_ref, w_ref, kvn_ref, gmg_ref, gmb_ref, ws0_ref, bs0_ref,
                               cos_ref, sina_ref, sinb_ref, qn_ref, wqn_ref, wqp_ref, wuk_ref,
                               ckv_ref, kpe_ref, gate_ref, vn_ref, qlat_ref, qpe_ref):
    z = _dot(x_ref[...].astype(BF16), w_ref[...])
    o_c, o_u, o_v, o_k = Q_RANK, Q_RANK + KV_RANK, Q_RANK + KV_RANK + GM_WIDTH, Q_RANK + KV_RANK + 2 * GM_WIDTH
    cos_t, sin_a, sin_b = cos_ref[...], sina_ref[...], sinb_ref[...]
    c_n = _rms_norm(z[:, o_c:o_u], kvn_ref[...])
    ckv_ref[...] = c_n
    kp = _rope128(z[:, o_k:o_k + LANES], cos_t, sin_a, sin_b)
    kpe_ref[...] = kp[:, :ROPE_DIM]
    gu = jax.nn.gelu(z[:, o_u:o_v])
    v_n = _layer_norm(jax.nn.gelu(z[:, o_v:o_k]), gmg_ref[...], gmb_ref[...])
    vn_ref[...] = v_n
    mixed = ws0_ref[...].astype(F32) * v_n.astype(BF16).astype(F32) + bs0_ref[...]
    gate_ref[...] = (gu * mixed).astype(BF16)
    lat, pes = _queries(z[:, :o_c], qn_ref[...], wqn_ref[...], wqp_ref[...], wuk_ref,
                        cos_t, sin_a, sin_b)
    qlat_ref[...] = lat
    qpe_ref[...] = jnp.concatenate(pes, axis=1)


def _even_inproj_sample(xs, w_in_p, kv_norm, gm_g, gm_b, ws0, bs0, cos_t, sin_a, sin_b,
                        qn_g, wq_nope, wq_pe, w_uk):
    n = xs.shape[0]
    args = (xs, w_in_p, kv_norm, gm_g, gm_b, ws0, bs0, cos_t, sin_a, sin_b, qn_g, wq_nope, wq_pe, w_uk)
    full = lambda shape: pl.BlockSpec(shape, lambda i: (0,) * len(shape))
    widths = [(KV_RANK, F32), (ROPE_DIM, F32), (GM_WIDTH, BF16), (GM_WIDTH, F32),
              (MLA_HEADS * LANES, F32), (MLA_HEADS * LANES, F32)]
    return pl.pallas_call(
        _even_inproj_sample_kernel,
        grid=(1,),
        in_specs=[full(a.shape) for a in args],
        out_specs=[full((n, w)) for w, _ in widths],
        out_shape=[jax.ShapeDtypeStruct((n, w), dt) for w, dt in widths],
        compiler_params=_cparams(("arbitrary",)),
        name="even_inproj_sample",
    )(*args)


def _decode_attn_kernel(pt_ref, qlat_ref, qpe_ref, ckvs_ref, kpes_ref, ckv_hbm, kpe_hbm, out_ref,
                        ckv_buf, kpe_buf, m_buf, sem, *, n_pg, n_groups):
    b = pl.program_id(0)
    n_b = pl.num_programs(0)
    assert n_groups % DEC_SLOTS == 0 and DEC_AHEAD + 2 <= DEC_SLOTS

    def group_copies(seq, grp):
        slot = grp % DEC_SLOTS
        cps = []
        for k in range(n_pg):
            page = pt_ref[seq, grp * n_pg + k]
            keys = pl.ds(k * PAGE_SIZE, PAGE_SIZE)
            cps.append(pltpu.make_async_copy(ckv_hbm.at[page], ckv_buf.at[slot, keys, :], sem.at[0, slot]))
            cps.append(pltpu.make_async_copy(kpe_hbm.at[page], kpe_buf.at[slot, :, keys], sem.at[1, slot]))
        return cps

    @pl.when(b == 0)
    def _():
        for g in range(DEC_AHEAD):
            for cp in group_copies(0, g):
                cp.start()

    heads, span = MLA_HEADS, DEC_SPAN_PAGES * PAGE_SIZE
    half_span = span // 2
    n_span = n_pg // DEC_SPAN_PAGES
    qlat = qlat_ref[...]
    zero = jnp.zeros_like(qlat)
    q2 = jnp.concatenate([jnp.concatenate([qlat, zero], axis=1),
                          jnp.concatenate([zero, qlat], axis=1)], axis=0).astype(BF16)
    qpe = qpe_ref[...][:, :ROPE_DIM].astype(BF16)

    def scores(slot):
        s_pe = _dot(qpe, kpe_buf[slot].astype(BF16))
        pieces = []
        for j in range(n_span):
            m_buf[slot, j] = jnp.concatenate(
                [ckv_buf[slot, pl.ds(j * span, half_span), :],
                 ckv_buf[slot, pl.ds(j * span + half_span, half_span), :]], axis=1).astype(BF16)
            s2 = _dot_nt(q2, m_buf[slot, j])
            pieces += [s2[:heads] + s_pe[:, j * span:j * span + half_span],
                       s2[heads:] + s_pe[:, j * span + half_span:(j + 1) * span]]
        return jnp.concatenate(pieces, axis=1) * ATTN_SCALE

    def values(p, grp):
        slot = grp % DEC_SLOTS
        o2 = None
        for j in range(n_span):
            lhs = jnp.concatenate([p[:, j * span:j * span + half_span],
                                   p[:, j * span + half_span:(j + 1) * span]], axis=0).astype(BF16)
            part = _dot(lhs, m_buf[slot, j])
            o2 = part if o2 is None else o2 + part
        return o2[:heads, :KV_RANK] + o2[heads:, KV_RANK:]

    state = dict(m_run=None, m=None, l=None, acc=None)
    raw = {}
    probs = {}

    def softmax(k):
        s = raw.pop(k)
        m_cur = jnp.max(s, axis=-1, keepdims=True)
        m_new = m_cur if state["m_run"] is None else jnp.maximum(state["m_run"], m_cur)
        state["m_run"] = m_new
        probs[k] = (jnp.exp(s - m_new), m_new)

    def fold(k):
        p, m_k = probs.pop(k)
        pv = values(p, k)
        p_sum = jnp.sum(p, axis=-1, keepdims=True)
        if state["m"] is None:
            state["l"], state["acc"] = p_sum, pv
        else:
            alpha = jnp.exp(state["m"] - m_k)
            state["l"], state["acc"] = alpha * state["l"] + p_sum, alpha * state["acc"] + pv
        state["m"] = m_k

    for g in range(n_groups):
        if g + DEC_AHEAD < n_groups:
            for cp in group_copies(b, g + DEC_AHEAD):
                cp.start()
        else:
            @pl.when(b + 1 < n_b)
            def _():
                for cp in group_copies(b + 1, g + DEC_AHEAD - n_groups):
                    cp.start()
        for cp in group_copies(b, g):
            cp.wait()
        raw[g] = scores(g % DEC_SLOTS)
        if g >= 1:
            softmax(g - 1)
        if g >= 2:
            fold(g - 2)
    softmax(n_groups - 1)
    fold(n_groups - 2)
    fold(n_groups - 1)
    m, l, acc = state["m"], state["l"], state["acc"]

    r = lambda a: a.astype(BF16).astype(F32)
    kv = r(ckvs_ref[...])
    s_self = (jnp.sum(r(qlat_ref[...]) * kv, axis=-1, keepdims=True)
              + jnp.sum(r(qpe_ref[...]) * r(kpes_ref[...]), axis=-1, keepdims=True)) * ATTN_SCALE
    m_n = jnp.maximum(m, s_self)
    a = jnp.exp(m - m_n)
    p_self = jnp.exp(s_self - m_n)
    out_ref[...] = (a * acc + r(p_self) * kv) / (a * l + p_self)


def _decode_attn(page_table, qlat2, qpe2, ckv_s3, kpe_s3, cache_ckv_e, cache_kpe_t):
    dec_batch, n_pages = page_table.shape
    n_pg = DEC_PG
    qspec = pl.BlockSpec((MLA_HEADS, LANES), lambda b, pt: (b, 0))
    self_spec = pl.BlockSpec((None, 1, LANES), lambda b, pt: (b, 0, 0))
    hbm = pl.BlockSpec(memory_space=pl.ANY)
    grid_spec = pltpu.PrefetchScalarGridSpec(
        num_scalar_prefetch=1,
        grid=(dec_batch,),
        in_specs=[qspec, qspec, self_spec, self_spec, hbm, hbm],
        out_specs=qspec,
        scratch_shapes=[pltpu.VMEM((DEC_SLOTS, n_pg * PAGE_SIZE, KV_RANK), F32),
                        pltpu.VMEM((DEC_SLOTS, ROPE_DIM, n_pg * PAGE_SIZE), F32),
                        pltpu.VMEM((DEC_SLOTS, n_pg // DEC_SPAN_PAGES, DEC_SPAN_PAGES // 2 * PAGE_SIZE, 2 * KV_RANK),
                                   BF16),
                        pltpu.SemaphoreType.DMA((2, DEC_SLOTS))],
    )
    return pl.pallas_call(
        functools.partial(_decode_attn_kernel, n_pg=n_pg, n_groups=n_pages // n_pg),
        grid_spec=grid_spec,
        out_shape=jax.ShapeDtypeStruct((dec_batch * MLA_HEADS, KV_RANK), F32),
        compiler_params=_cparams(("arbitrary",)),
        name="decode_attn",
    )(page_table, qlat2, qpe2, ckv_s3, kpe_s3, cache_ckv_e, cache_kpe_t)


def _s5_readout(hcat_blocks, u, cw_ref, d_ref, wglu_ref):
    y = jnp.concatenate([_dot(hcat_blocks[k], cw_ref[k]) for k in range(S5_NB)], axis=1)
    y = jax.nn.gelu(y + d_ref[...] * u)
    return y * jax.nn.sigmoid(_dot(y.astype(BF16), wglu_ref[...]))


def _s5_prompt_kernel(x_ref, win_ref, bw_ref, a_ref, cw_ref, d_ref, wglu_ref,
                      yg_ref, hre_ref, him_ref, u_s, y_s, bu_s, h_s):
    n_b, seg, _ = x_ref.shape
    half = S5_SLABS // 2
    sub = S5_SUB
    n_sub = seg // sub
    rows_sub = sub * n_b
    q_slabs = S5_WIDTH // LANES

    @pl.when(pl.program_id(0) == 0)
    def _():
        h_s[...] = jnp.zeros(h_s.shape, F32)

    def project(s):
        rows = slice(s * rows_sub, (s + 1) * rows_sub)
        xs = jnp.concatenate([x_ref[b, s * sub:(s + 1) * sub, :] for b in range(n_b)], axis=0)
        u = _dot(xs.astype(BF16), win_ref[...])
        for b in range(n_b):
            for q in range(q_slabs):
                u_s[s, q, pl.ds(b, sub, stride=n_b), :] = u[b * sub:(b + 1) * sub, q * LANES:(q + 1) * LANES]
        for k in range(S5_NB):
            r = _dot(u_s[s, k].astype(BF16), bw_ref[k])
            for q in range(S5_BLK_SLABS):
                bu_s[S5_BLK_SLABS * k + q, rows, :] = r[:, q * LANES:(q + 1) * LANES]
                bu_s[half + S5_BLK_SLABS * k + q, rows, :] = r[:, (S5_BLK_SLABS + q) * LANES:
                                                                (S5_BLK_SLABS + q + 1) * LANES]

    def recur(s, h):
        for t in range(s * sub, (s + 1) * sub):
            rows = slice(t * n_b, (t + 1) * n_b)
            hs = bu_s[:, rows, :]
            ar, ai = a_ref[:half], a_ref[half:]
            hr, hi = h[:half], h[half:]
            h = jnp.concatenate([ar * hr - ai * hi + hs[:half], ar * hi + ai * hr + hs[half:]], axis=0)
            bu_s[:, rows, :] = h
        return h

    def read_out(s):
        rows = slice(s * rows_sub, (s + 1) * rows_sub)
        blocks = [jnp.concatenate([bu_s[S5_BLK_SLABS * k + q, rows, :] for q in range(S5_BLK_SLABS)]
                                  + [bu_s[half + S5_BLK_SLABS * k + q, rows, :] for q in range(S5_BLK_SLABS)],
                                  axis=1).astype(BF16)
                  for k in range(S5_NB)]
        u_t = jnp.concatenate([u_s[s, q] for q in range(q_slabs)], axis=1)
        y = _s5_readout(blocks, u_t, cw_ref, d_ref, wglu_ref)
        for q in range(q_slabs):
            y_s[s, q] = y[:, q * LANES:(q + 1) * LANES]
        for b in range(n_b):
            yg_ref[b, s * sub:(s + 1) * sub, :] = jnp.concatenate(
                [y_s[s, q, pl.ds(b, sub, stride=n_b), :] for q in range(q_slabs)], axis=1).astype(BF16)

    h = h_s[...]
    project(0)
    for s in range(n_sub):
        if s + 1 < n_sub:
            project(s + 1)
        h = recur(s, h)
        if s >= 1:
            read_out(s - 1)
    read_out(n_sub - 1)
    h_s[...] = h
    hre_ref[...] = jnp.concatenate([h[s] for s in range(half)], axis=1)
    him_ref[...] = jnp.concatenate([h[half + s] for s in range(half)], axis=1)


def _s5_prompt(x3, w_in, bw, a_b, cw, d_row, w_glu):
    batch, seq, _ = x3.shape
    seg = S5_L
    slab_rows = (seg // S5_SUB, S5_WIDTH // LANES, S5_SUB * batch, LANES)
    return pl.pallas_call(
        _s5_prompt_kernel,
        grid=(seq // seg,),
        in_specs=[pl.BlockSpec((batch, seg, D_MODEL), lambda c: (0, c, 0)),
                  _const_spec(w_in.shape), _const_spec(bw.shape), _const_spec(a_b.shape),
                  _const_spec(cw.shape), _const_spec(d_row.shape), _const_spec(w_glu.shape)],
        out_specs=[pl.BlockSpec((batch, seg, S5_WIDTH), lambda c: (0, c, 0)),
                   pl.BlockSpec((batch, S5_STATES), lambda c: (0, 0)),
                   pl.BlockSpec((batch, S5_STATES), lambda c: (0, 0))],
        out_shape=[jax.ShapeDtypeStruct((batch, seq, S5_WIDTH), BF16),
                   jax.ShapeDtypeStruct((batch, S5_STATES), F32),
                   jax.ShapeDtypeStruct((batch, S5_STATES), F32)],
        scratch_shapes=[pltpu.VMEM(slab_rows, F32), pltpu.VMEM(slab_rows, F32),
                        pltpu.VMEM((S5_SLABS, seg * batch, LANES), F32),
                        pltpu.VMEM((S5_SLABS, batch, LANES), F32)],
        compiler_params=_cparams(("arbitrary",)),
        name="s5_prompt",
    )(x3, w_in, bw, a_b, cw, d_row, w_glu)


def _s5_sample_kernel(x_ref, h0r_ref, h0i_ref, win_ref, bw_ref, ar_ref, ai_ref, cw_ref, d_ref, wglu_ref,
                      yg_ref, hre_ref, him_ref):
    u = _dot(x_ref[...].astype(BF16), win_ref[...])
    ub = u.astype(BF16)
    bu = [_dot(ub[:, k * LANES:(k + 1) * LANES], bw_ref[k]) for k in range(S5_NB)]
    w = S5_KB * S5_STATE
    bu_r = jnp.concatenate([r[:, :w] for r in bu], axis=1)
    bu_i = jnp.concatenate([r[:, w:] for r in bu], axis=1)
    ar, ai, h0r, h0i = ar_ref[...], ai_ref[...], h0r_ref[...], h0i_ref[...]
    hr = bu_r + (ar * h0r - ai * h0i)
    hi = bu_i + (ar * h0i + ai * h0r)
    hre_ref[...] = hr
    him_ref[...] = hi
    blocks = [jnp.concatenate([hr[:, k * w:(k + 1) * w], hi[:, k * w:(k + 1) * w]], axis=1).astype(BF16)
              for k in range(S5_NB)]
    yg_ref[...] = _s5_readout(blocks, u, cw_ref, d_ref, wglu_ref).astype(BF16)


def _s5_sample(xs, h0r, h0i, w_in, bw, a_r, a_i, cw, d_row, w_glu):
    n = xs.shape[0]
    args = (xs, h0r, h0i, w_in, bw, a_r, a_i, cw, d_row, w_glu)
    full = lambda shape: pl.BlockSpec(shape, lambda i: (0,) * len(shape))
    return pl.pallas_call(
        _s5_sample_kernel,
        grid=(1,),
        in_specs=[full(a.shape) for a in args],
        out_specs=[full((n, S5_WIDTH)), full((n, S5_STATES)), full((n, S5_STATES))],
        out_shape=[jax.ShapeDtypeStruct((n, S5_WIDTH), BF16),
                   jax.ShapeDtypeStruct((n, S5_STATES), F32),
                   jax.ShapeDtypeStruct((n, S5_STATES), F32)],
        compiler_params=_cparams(("arbitrary",)),
        name="s5_sample",
    )(*args)


def _rope_tables(pos):
    freqs = ROPE_THETA ** (-jnp.arange(ROPE_HALF, dtype=F32) / ROPE_HALF)
    ang = pos.astype(F32)[:, None] * freqs
    cos, sin = jnp.cos(ang), jnp.sin(ang)
    zero = jnp.zeros_like(cos)
    pad = jnp.zeros((pos.shape[0], LANES - ROPE_DIM), F32)
    cos_t = jnp.concatenate([cos, cos, pad], axis=1)
    sin_a = jnp.concatenate([zero, sin, pad], axis=1)
    sin_b = jnp.concatenate([-sin, zero, pad], axis=1)
    return cos_t, sin_a, sin_b


def _block_diag(blocks):
    *lead, n, r, c = blocks.shape
    eye = jnp.eye(n, dtype=blocks.dtype)
    return jnp.einsum('...nrc,nm->...nrmc', blocks, eye).reshape(*lead, n * r, n * c)


def _even_params(w_in, q_norm, w_q_b, kv_norm, w_kv_b, gm_g, gm_b, gm_w_s, gm_b_s, w_out):
    i1, i2, i3, i4 = Q_RANK, Q_RANK + KV_RANK, Q_RANK + KV_RANK + ROPE_DIM, Q_RANK + KV_RANK + ROPE_DIM + GM_WIDTH
    w_in_p = jnp.concatenate([w_in[:, :i2], w_in[:, i3:], w_in[:, i2:i3],
                              jnp.zeros((D_MODEL, LANES - ROPE_DIM), F32)], axis=1).astype(BF16)
    wq_nope = w_q_b[:, :, :NOPE_DIM].reshape(Q_RANK, MLA_HEADS * NOPE_DIM).astype(BF16)
    pe = w_q_b[:, :, NOPE_DIM:]
    pe_sw = jnp.concatenate([pe[:, :, ROPE_HALF:], pe[:, :, :ROPE_HALF]], axis=2)
    lane_pad = jnp.zeros((Q_RANK, MLA_HEADS, LANES - ROPE_DIM), F32)
    wq_pe = jnp.concatenate(
        [jnp.concatenate([w, lane_pad], axis=2).reshape(Q_RANK, MLA_HEADS * LANES) for w in (pe, pe_sw)],
        axis=1).astype(BF16)
    uk = jnp.transpose(w_kv_b[:, :, :NOPE_DIM], (1, 2, 0))
    uv = jnp.transpose(w_kv_b[:, :, NOPE_DIM:], (1, 0, 2))
    pairs = MLA_HEADS // 2
    w_uk = _block_diag(uk.reshape(pairs, 2, NOPE_DIM, KV_RANK)).astype(BF16)
    w_uv_p = _block_diag(uv.reshape(pairs, 2, KV_RANK, V_DIM)).astype(BF16)
    w_uv = _block_diag(uv).astype(BF16)
    causal = jnp.tril(jnp.ones((GM_CHUNK, GM_CHUNK), dtype=bool))
    ws_m = jnp.where(causal[None], gm_w_s, 0).astype(BF16)
    bs_full = jnp.repeat(gm_b_s.T, GM_HEAD_DIM, axis=1)
    ws0 = jnp.repeat(gm_w_s[:, 0, 0], GM_HEAD_DIM)[None, :].astype(BF16)
    bs0 = bs_full[:1]
    n_attn = MLA_HEADS * V_DIM
    return dict(w_in_p=w_in_p, qn=q_norm[None, :], wq_nope=wq_nope, wq_pe=wq_pe, w_uk=w_uk, w_uv=w_uv, w_uv_p=w_uv_p,
                kvn=kv_norm[None, :], gm_g=gm_g[None, :], gm_b=gm_b[None, :], ws_m=ws_m, bs_full=bs_full,
                ws0=ws0, bs0=bs0, wo_attn=w_out[:n_attn].astype(BF16), wo_gate=w_out[n_attn:].astype(BF16))


def _s5_params(w_in, a_re, a_im, b_re, b_im, c_re, c_im, d, log_dt, w_glu, w_out, batch):
    dt = jnp.exp(log_dt)[:, None]
    ld_r, ld_i = a_re * dt, a_im * dt
    mag = jnp.exp(ld_r)
    ab_r, ab_i = mag * jnp.cos(ld_i), mag * jnp.sin(ld_i)
    den = a_re * a_re + a_im * a_im
    cr = ((ab_r - 1.0) * a_re + ab_i * a_im) / den
    ci = (ab_i * a_re - (ab_r - 1.0) * a_im) / den
    bb_r = cr[..., None] * b_re - ci[..., None] * b_im
    bb_i = cr[..., None] * b_im + ci[..., None] * b_re

    def in_blocks(bb):
        return _block_diag(jnp.swapaxes(bb, 1, 2).reshape(S5_NB, S5_KB, S5_GROUP_DIM, S5_STATE))

    def out_blocks(cc):
        return _block_diag(jnp.swapaxes(cc, 1, 2).reshape(S5_NB, S5_KB, S5_STATE, S5_GROUP_DIM))

    bw = jnp.concatenate([in_blocks(bb_r), in_blocks(bb_i)], axis=2).astype(BF16)
    cw = jnp.concatenate([out_blocks(c_re), out_blocks(-c_im)], axis=1).astype(BF16)
    a_r, a_i = ab_r.reshape(1, S5_STATES), ab_i.reshape(1, S5_STATES)
    half = S5_SLABS // 2
    a_b = jnp.concatenate([jnp.broadcast_to(a_r.reshape(half, 1, LANES), (half, batch, LANES)),
                           jnp.broadcast_to(a_i.reshape(half, 1, LANES), (half, batch, LANES))], axis=0)
    return dict(w_in=w_in.astype(BF16), bw=bw, cw=cw, a_r=a_r, a_i=a_i, a_b=a_b,
                d=d.reshape(1, S5_WIDTH), w_glu=w_glu.astype(BF16), w_out=w_out.astype(BF16))


def kernel(x_prompt, x_sample, cache_ckv, cache_kpe, state_s5_re, state_s5_im, page_table, w_in_even, mla_q_norm, mla_w_q_b, mla_kv_norm, mla_w_kv_b, gm_norm_g, gm_norm_b, gm_w_s, gm_b_s, w_out_even, w_in_odd, s5_a_re, s5_a_im, s5_b_re, s5_b_im, s5_c_re, s5_c_im, s5_d, s5_log_dt, s5_w_glu, w_out_odd, ln_mix_g, ln_mix_b, ln_ffn_g, ln_ffn_b, ffn_w1, ffn_w2):
    batch, seq, _ = x_prompt.shape
    dec_batch, dec_seq, _ = x_sample.shape
    assert dec_seq == 1 and seq % INPROJ_TM == 0 and seq % ATTN_TQ == 0 and seq % S5_L == 0
    assert (seq // ATTN_TQ) % 2 == 0
    assert (batch * seq) % FFN_TM == 0 and page_table.shape[1] % DEC_PG == 0
    assert batch == SUBLANES

    xp = x_prompt.reshape(batch * seq, D_MODEL)
    xs = x_sample.reshape(dec_batch, D_MODEL)
    tabs_p = _rope_tables(jnp.arange(seq, dtype=jnp.int32))
    tabs_s = _rope_tables(PAST_LEN + jnp.arange(dec_seq, dtype=jnp.int32))

    outs = {k: [] for k in ("ckv_p", "kpe_p", "ckv_s", "kpe_s", "gmv_p", "gmv_s",
                            "s5re_p", "s5im_p", "s5re_s", "s5im_s")}
    for layer in range(DEPTH):
        ln = (ln_mix_g[layer][None, :], ln_mix_b[layer][None, :],
              ln_ffn_g[layer][None, :], ln_ffn_b[layer][None, :])
        if layer % 2 == 0:
            e = layer // 2
            p = _even_params(w_in_even[e], mla_q_norm[e], mla_w_q_b[e], mla_kv_norm[e], mla_w_kv_b[e],
                             gm_norm_g[e], gm_norm_b[e], gm_w_s[e], gm_b_s[e], w_out_even[e])
            q_a, ckv, kpe, kcat, gate, gmv = _even_inproj(
                xp, p["w_in_p"], p["kvn"], p["gm_g"], p["gm_b"], p["ws_m"], p["bs_full"], *tabs_p, batch, seq)
            attn = _mla_attn(q_a, kcat, p["qn"], p["wq_nope"], p["wq_pe"], p["w_uk"], p["w_uv_p"], *tabs_p,
                             batch, seq)
            outs["ckv_p"].append(ckv.reshape(batch, seq, KV_RANK))
            outs["kpe_p"].append(kpe.reshape(batch, seq, ROPE_DIM))
            outs["gmv_p"].append(gmv)
            ckv_s, kpe_s, gate_s, vn_s, qlat_s, qpe_s = _even_inproj_sample(
                xs, p["w_in_p"], p["kvn"], p["gm_g"], p["gm_b"], p["ws0"], p["bs0"], *tabs_s,
                p["qn"], p["wq_nope"], p["wq_pe"], p["w_uk"])
            kpe_pad = jnp.concatenate([kpe_s, jnp.zeros((dec_batch, LANES - ROPE_DIM), F32)], axis=1)
            o_lat = _decode_attn(page_table,
                                 qlat_s.reshape(dec_batch * MLA_HEADS, LANES),
                                 qpe_s.reshape(dec_batch * MLA_HEADS, LANES),
                                 ckv_s.reshape(dec_batch, 1, KV_RANK), kpe_pad.reshape(dec_batch, 1, LANES),
                                 cache_ckv[e], jnp.swapaxes(cache_kpe[e], 1, 2))
            xs, w1_b, w2_b = _mix_ffn_sample(
                xs, [o_lat.reshape(dec_batch, MLA_HEADS * KV_RANK), gate_s], p["w_uv"],
                [p["wo_attn"], p["wo_gate"]], *ln, ffn_w1, ffn_w2, layer, name="even_ffn_sample")
            xp = _mix_ffn(xp, [attn, gate], None, [p["wo_attn"], p["wo_gate"]], *ln, w1_b, w2_b,
                          name="even_ffn_prompt")
            outs["ckv_s"].append(ckv_s.reshape(dec_batch, 1, KV_RANK))
            outs["kpe_s"].append(kpe_s.reshape(dec_batch, 1, ROPE_DIM))
            outs["gmv_s"].append(vn_s.reshape(dec_batch, 1, GM_WIDTH))
        else:
            o = layer // 2
            p = _s5_params(w_in_odd[o], s5_a_re[o], s5_a_im[o], s5_b_re[o], s5_b_im[o], s5_c_re[o],
                           s5_c_im[o], s5_d[o], s5_log_dt[o], s5_w_glu[o], w_out_odd[o], batch)
            yg, hre, him = _s5_prompt(xp.reshape(batch, seq, D_MODEL), p["w_in"], p["bw"], p["a_b"],
                                      p["cw"], p["d"], p["w_glu"])
            outs["s5re_p"].append(hre.reshape(batch, S5_GROUPS, S5_STATE))
            outs["s5im_p"].append(him.reshape(batch, S5_GROUPS, S5_STATE))
            yg_s, hre_s, him_s = _s5_sample(
                xs, state_s5_re[o].reshape(dec_batch, S5_STATES), state_s5_im[o].reshape(dec_batch, S5_STATES),
                p["w_in"], p["bw"], p["a_r"], p["a_i"], p["cw"], p["d"], p["w_glu"])
            xs, w1_b, w2_b = _mix_ffn_sample(xs, [yg_s], None, [p["w_out"]], *ln, ffn_w1, ffn_w2, layer,
                                             name="odd_ffn_sample")
            xp = _mix_ffn(xp, [yg.reshape(batch * seq, S5_WIDTH)], None, [p["w_out"]], *ln, w1_b, w2_b,
                          name="odd_ffn_prompt")
            outs["s5re_s"].append(hre_s.reshape(dec_batch, S5_GROUPS, S5_STATE))
            outs["s5im_s"].append(him_s.reshape(dec_batch, S5_GROUPS, S5_STATE))

    st = jnp.stack
    return (xp.reshape(batch, seq, D_MODEL), xs.reshape(dec_batch, dec_seq, D_MODEL),
            st(outs["ckv_p"]), st(outs["kpe_p"]), st(outs["ckv_s"]), st(outs["kpe_s"]),
            st(outs["gmv_p"]), st(outs["gmv_s"]),
            st(outs["s5re_p"]), st(outs["s5im_p"]), st(outs["s5re_s"]), st(outs["s5im_s"]))
```

```python
import functools
import math

import jax
import jax.numpy as jnp
from jax import lax
from jax.experimental import pallas as pl
from jax.experimental.pallas import tpu as pltpu

F32 = jnp.float32
BF16 = jnp.bfloat16

D_MODEL = 1024
DEPTH = 2
PAST_LEN = 16384
PAGE_SIZE = 128
MLA_HEADS = 8
NOPE_DIM = 64
ROPE_DIM = 32
ROPE_HALF = ROPE_DIM // 2
V_DIM = 64
Q_RANK = 256
KV_RANK = 128
ROPE_THETA = 10000.0
ATTN_SCALE = 1.0 / math.sqrt(NOPE_DIM + ROPE_DIM)
ATTN_SCALE_LOG2 = ATTN_SCALE * math.log2(math.e)
GM_HEADS = 8
GM_HEAD_DIM = 64
GM_WIDTH = GM_HEADS * GM_HEAD_DIM
GM_CHUNK = 128
S5_GROUP_DIM = 16
S5_GROUPS = 32
S5_STATE = 64
S5_WIDTH = S5_GROUPS * S5_GROUP_DIM
S5_STATES = S5_GROUPS * S5_STATE
FFN_HIDDEN = 4 * D_MODEL
DN_ALPHA = (2 * DEPTH) ** 0.25
LN_EPS = 1e-5
RMS_EPS = 1e-6

LANES = 128
SUBLANES = 8
VMEM_LIMIT_BYTES = 56 * 1024 * 1024

INPROJ_TM = 512
INPROJ_ROW_BLOCK = 256
ATTN_TQ = 256
ATTN_ROW_BLOCK = 256
FFN_TM = 1024
FFN_HC = 1024
FFN_SAMPLE_HC = 512
FFN_ROW_BLOCK = 256
S5_L = 128
S5_SUB = 32
DEC_PG = 32
DEC_SLOTS = 4
DEC_AHEAD = 2
DEC_SPAN_PAGES = 4
S5_KB = 8
S5_NB = S5_GROUPS // S5_KB
S5_SLABS = 2 * S5_STATES // LANES
S5_BLK_SLABS = S5_KB * S5_STATE // LANES


def _cparams(sem):
    return pltpu.CompilerParams(dimension_semantics=sem, vmem_limit_bytes=VMEM_LIMIT_BYTES)


def _const_spec(shape):
    n = len(shape)
    return pl.BlockSpec(shape, lambda *_: (0,) * n, pipeline_mode=pl.Buffered(1))


def _layer_norm(x, g, b):
    mu = jnp.mean(x, axis=-1, keepdims=True)
    xc = x - mu
    var = jnp.mean(xc * xc, axis=-1, keepdims=True)
    return xc * lax.rsqrt(var + LN_EPS) * g + b


def _rms_norm(x, g):
    return x * lax.rsqrt(jnp.mean(x * x, axis=-1, keepdims=True) + RMS_EPS) * g


def _rope128(x, cos_t, sin_a, sin_b):
    return (x * cos_t + pltpu.roll(x, ROPE_HALF, 1) * sin_a
            + pltpu.roll(x, LANES - ROPE_HALF, 1) * sin_b)


def _dot(a, b):
    return jnp.dot(a, b, preferred_element_type=F32)


def _dot_nt(a, b):
    return lax.dot_general(a, b, (((1,), (1,)), ((), ())), preferred_element_type=F32)


def _even_inproj_kernel(x_ref, w_ref, kvn_ref, gmg_ref, gmb_ref, ws_ref, bs_ref,
                        cos_ref, sina_ref, sinb_ref,
                        qa_ref, ckv_ref, kpe_ref, kcat_ref, gate_ref, gmv_ref):
    tm = x_ref.shape[0]
    o_c, o_u, o_v, o_k = Q_RANK, Q_RANK + KV_RANK, Q_RANK + KV_RANK + GM_WIDTH, Q_RANK + KV_RANK + 2 * GM_WIDTH
    rb = min(INPROJ_ROW_BLOCK, tm)
    blocks = [slice(r0, r0 + rb) for r0 in range(0, tm, rb)]
    zs = [_dot(x_ref[rows, :].astype(BF16), w_ref[...]) for rows in blocks]

    gus, vns = [], []
    for rows, z in zip(blocks, zs):
        qa_ref[rows, :] = z[:, :o_c]
        c_n = _rms_norm(z[:, o_c:o_u], kvn_ref[...])
        ckv_ref[rows, :] = c_n
        kp = _rope128(z[:, o_k:o_k + LANES], cos_ref[rows, :], sina_ref[rows, :], sinb_ref[rows, :])
        kpe_ref[rows, :] = kp[:, :ROPE_DIM]
        kcat_ref[rows, :] = jnp.concatenate([c_n, kp], axis=1).astype(BF16)
        gus.append(jax.nn.gelu(z[:, o_u:o_v]))
        vns.append(_layer_norm(jax.nn.gelu(z[:, o_v:o_k]), gmg_ref[...], gmb_ref[...]))

    gmv_ref[0] = vns[-1][rb - GM_CHUNK:, :]

    lane = lax.broadcasted_iota(jnp.int32, (GM_CHUNK, LANES), 1)
    lo = lane < GM_HEAD_DIM
    bs = bs_ref[...]
    for rows, gu, v_n in zip(blocks, gus, vns):
        for ci in range(rb // GM_CHUNK):
            chunk = slice(ci * GM_CHUNK, (ci + 1) * GM_CHUNK)
            pieces = []
            for pr in range(GM_WIDTH // LANES):
                r = v_n[chunk, pr * LANES:(pr + 1) * LANES]
                m0 = _dot(ws_ref[2 * pr], jnp.where(lo, r, 0.0).astype(BF16))
                m1 = _dot(ws_ref[2 * pr + 1], jnp.where(lo, 0.0, r).astype(BF16))
                pieces.append(m0 + m1)
            mixed = jnp.concatenate(pieces, axis=1) + bs
            gate_ref[pl.ds(rows.start + ci * GM_CHUNK, GM_CHUNK), :] = (gu[chunk, :] * mixed).astype(BF16)


def _even_inproj(x2d, w_in_p, kv_norm, gm_g, gm_b, ws_m, bs_full, cos_t, sin_a, sin_b, batch, seq):
    n_tok = x2d.shape[0]
    tm = INPROJ_TM
    tiles_per_seq = seq // tm
    row = lambda w: pl.BlockSpec((tm, w), lambda i: (i, 0))
    tab = pl.BlockSpec((tm, LANES), lambda i: (i % tiles_per_seq, 0))
    n_in = w_in_p.shape[1]
    return pl.pallas_call(
        _even_inproj_kernel,
        grid=(n_tok // tm,),
        in_specs=[row(D_MODEL), _const_spec((D_MODEL, n_in)), _const_spec((1, KV_RANK)),
                  _const_spec((1, GM_WIDTH)), _const_spec((1, GM_WIDTH)),
                  _const_spec((GM_HEADS, GM_CHUNK, GM_CHUNK)), _const_spec((GM_CHUNK, GM_WIDTH)),
                  tab, tab, tab],
        out_specs=[row(Q_RANK), row(KV_RANK), row(ROPE_DIM), row(2 * LANES), row(GM_WIDTH),
                   pl.BlockSpec((1, GM_CHUNK, GM_WIDTH), lambda i: (i // tiles_per_seq, 0, 0))],
        out_shape=[jax.ShapeDtypeStruct((n_tok, Q_RANK), F32),
                   jax.ShapeDtypeStruct((n_tok, KV_RANK), F32),
                   jax.ShapeDtypeStruct((n_tok, ROPE_DIM), F32),
                   jax.ShapeDtypeStruct((n_tok, 2 * LANES), BF16),
                   jax.ShapeDtypeStruct((n_tok, GM_WIDTH), BF16),
                   jax.ShapeDtypeStruct((batch, GM_CHUNK, GM_WIDTH), F32)],
        compiler_params=_cparams(("arbitrary",)),
        name="even_inproj",
    )(x2d, w_in_p, kv_norm, gm_g, gm_b, ws_m, bs_full, cos_t, sin_a, sin_b)


def _queries(q_a, qn_g, wq_nope, wq_pe, wuk_ref, cos_t, sin_a, sin_b):
    qn = _rms_norm(q_a, qn_g).astype(BF16)
    nope = _dot(qn, wq_nope).astype(BF16)
    lat = jnp.concatenate([_dot(nope[:, pr * LANES:(pr + 1) * LANES], wuk_ref[pr])
                           for pr in range(MLA_HEADS // 2)], axis=1)
    pe = _dot(qn, wq_pe)
    n = MLA_HEADS * LANES
    sin_s = sin_a + sin_b
    pes = [pe[:, h * LANES:(h + 1) * LANES] * cos_t + pe[:, n + h * LANES:n + (h + 1) * LANES] * sin_s
           for h in range(MLA_HEADS)]
    return lat, pes


def _mla_attn_kernel(qa_lo_ref, qa_hi_ref, kcat_ref, qn_ref, wqn_ref, wqp_ref, wuk_ref, wuv_ref,
                     cos_lo_ref, sina_lo_ref, sinb_lo_ref, cos_hi_ref, sina_hi_ref, sinb_hi_ref,
                     out_hbm, qcat_s, m_s, acc_s, obuf, osem, *, nq):
    tq = qa_lo_ref.shape[0]
    n_rows = MLA_HEADS * tq
    b, i = pl.program_id(0), pl.program_id(1)
    step = b * pl.num_programs(1) + i
    n_steps = pl.num_programs(0) * pl.num_programs(1)
    q_tiles = (i, nq - 1 - i)

    def out_copies():
        return [pltpu.make_async_copy(
            obuf.at[t], out_hbm.at[pl.ds(pl.multiple_of((b * nq + q_tiles[t]) * tq, tq), tq), :], osem.at[t])
            for t in range(2)]

    qa = (qa_lo_ref, qa_hi_ref)
    tabs = ((cos_lo_ref, sina_lo_ref, sinb_lo_ref), (cos_hi_ref, sina_hi_ref, sinb_hi_ref))
    for t in range(2):
        lat, pes = _queries(qa[t][...], qn_ref[...], wqn_ref[...], wqp_ref[...], wuk_ref,
                            *(r[...] for r in tabs[t]))
        for h in range(MLA_HEADS):
            qcat_s[t, h * tq:(h + 1) * tq, :] = (ATTN_SCALE_LOG2 * jnp.concatenate(
                [lat[:, h * LANES:(h + 1) * LANES], pes[h]], axis=1)).astype(BF16)

    lanes_x = lambda a, n: jnp.concatenate([a] * n, axis=1)
    rb = ATTN_ROW_BLOCK
    ones = jnp.ones((tq, LANES), BF16)

    def keys(kv_tile):
        k = kcat_ref[pl.ds(pl.multiple_of(kv_tile * tq, tq), tq), :]
        return k, jnp.concatenate([k[:, :KV_RANK], ones], axis=1)

    q_pos = lax.broadcasted_iota(jnp.int32, (rb, tq), 0) & (tq - 1)
    k_pos = lax.broadcasted_iota(jnp.int32, (rb, tq), 1)
    causal = k_pos <= q_pos
    for t in range(2):
        k, v1 = keys(q_tiles[t])
        for r0 in range(0, n_rows, rb):
            rows = slice(r0, r0 + rb)
            s = jnp.where(causal, _dot_nt(qcat_s[t, rows, :], k), -jnp.inf)
            m0 = jnp.broadcast_to(jnp.max(s, axis=-1, keepdims=True), (rb, LANES))
            m_s[t, rows, :] = m0
            acc_s[t, rows, :] = _dot(jnp.exp2(s - lanes_x(m0, tq // LANES)).astype(BF16), v1)

    for u in range(nq - 1):
        hi = u >= i
        t = hi.astype(jnp.int32)
        k, v1 = keys(jnp.where(hi, u - i, u))
        for r0 in range(0, n_rows, rb):
            rows = slice(r0, r0 + rb)
            s = _dot_nt(qcat_s[t, rows, :], k)
            m_old = m_s[t, rows, :]
            m_new = jnp.maximum(m_old, jnp.max(s, axis=-1, keepdims=True))
            alpha = jnp.exp2(m_old - m_new)
            p = jnp.exp2(s - lanes_x(m_new, tq // LANES))
            acc_s[t, rows, :] = lanes_x(alpha, 2) * acc_s[t, rows, :] + _dot(p.astype(BF16), v1)
            m_s[t, rows, :] = m_new

    @pl.when(step > 0)
    def _():
        for cp in out_copies():
            cp.wait()

    for t in range(2):
        o_all = jnp.concatenate(
            [acc_s[t, h * tq:(h + 1) * tq, :KV_RANK] / acc_s[t, h * tq:(h + 1) * tq, KV_RANK:]
             for h in range(MLA_HEADS)], axis=1).astype(BF16)
        obuf[t] = jnp.concatenate(
            [_dot(o_all[:, pr * 2 * LANES:(pr + 1) * 2 * LANES], wuv_ref[pr]) for pr in range(MLA_HEADS // 2)],
            axis=1).astype(BF16)
    for cp in out_copies():
        cp.start()

    @pl.when(step == n_steps - 1)
    def _():
        for cp in out_copies():
            cp.wait()


def _mla_attn(q_a, kcat, qn_g, wq_nope, wq_pe, w_uk, w_uv, cos_t, sin_a, sin_b, batch, seq):
    tq = ATTN_TQ
    nq = seq // tq
    n_attn = MLA_HEADS * V_DIM
    lo = lambda b, i: i
    hi = lambda b, i: nq - 1 - i
    tab = lambda sel: pl.BlockSpec((tq, LANES), lambda b, i: (sel(b, i), 0))
    qa_spec = lambda sel: pl.BlockSpec((tq, Q_RANK), lambda b, i: (b * nq + sel(b, i), 0))
    return pl.pallas_call(
        functools.partial(_mla_attn_kernel, nq=nq),
        grid=(batch, nq // 2),
        in_specs=[qa_spec(lo), qa_spec(hi),
                  pl.BlockSpec((None, seq, 2 * LANES), lambda b, i: (b, 0, 0)),
                  _const_spec((1, Q_RANK)), _const_spec(wq_nope.shape), _const_spec(wq_pe.shape),
                  _const_spec(w_uk.shape), _const_spec(w_uv.shape),
                  tab(lo), tab(lo), tab(lo), tab(hi), tab(hi), tab(hi)],
        out_specs=pl.BlockSpec(memory_space=pl.ANY),
        out_shape=jax.ShapeDtypeStruct((batch * seq, n_attn), BF16),
        scratch_shapes=[pltpu.VMEM((2, MLA_HEADS * tq, 2 * LANES), BF16),
                        pltpu.VMEM((2, MLA_HEADS * tq, LANES), F32),
                        pltpu.VMEM((2, MLA_HEADS * tq, 2 * KV_RANK), F32),
                        pltpu.VMEM((2, tq, n_attn), BF16),
                        pltpu.SemaphoreType.DMA((2,))],
        compiler_params=_cparams(("arbitrary", "arbitrary")),
        name="mla_attn",
    )(q_a, q_a, kcat.reshape(batch, seq, 2 * LANES), qn_g, wq_nope, wq_pe, w_uk, w_uv,
      cos_t, sin_a, sin_b, cos_t, sin_a, sin_b)


def _mix_ffn_kernel(*refs, n_act, has_pre):
    x_ref = refs[0]
    act_refs = refs[1:1 + n_act]
    k = 1 + n_act
    pre_ref = refs[k] if has_pre else None
    k += int(has_pre)
    wo_refs = refs[k:k + n_act]
    k += n_act
    g1_ref, b1_ref, g2_ref, b2_ref, w1_ref, w2_ref, out_ref = refs[k:k + 7]

    tm = x_ref.shape[0]
    rb = min(FFN_ROW_BLOCK, tm)
    blocks = [slice(r0, r0 + rb) for r0 in range(0, tm, rb)]
    fs = []
    for rows in blocks:
        f = None
        for idx in range(n_act):
            a = act_refs[idx][rows, :]
            if has_pre and idx == 0:
                a = _dot(a.astype(BF16), pre_ref[...])
            part = _dot(a.astype(BF16), wo_refs[idx][...])
            f = part if f is None else f + part
        fs.append(f)
    x1s = [_layer_norm(DN_ALPHA * x_ref[rows, :] + f, g1_ref[...], b1_ref[...]) for rows, f in zip(blocks, fs)]
    x1bs = [x1.astype(BF16) for x1 in x1s]
    accs = [None] * len(blocks)
    for c in range(FFN_HIDDEN // FFN_HC):
        cols = slice(c * FFN_HC, (c + 1) * FFN_HC)
        for i, x1b in enumerate(x1bs):
            h = jnp.maximum(_dot(x1b, w1_ref[:, cols]), 0.0)
            part = _dot((h * h).astype(BF16), w2_ref[cols, :])
            accs[i] = part if accs[i] is None else accs[i] + part
    for rows, x1, acc in zip(blocks, x1s, accs):
        out_ref[rows, :] = _layer_norm(DN_ALPHA * x1 + acc, g2_ref[...], b2_ref[...])


def _mix_ffn(x2d, acts, pre_w, wos, ln1_g, ln1_b, ln2_g, ln2_b, w1, w2, name):
    n_tok = x2d.shape[0]
    tm = min(FFN_TM, n_tok)
    row = lambda w: pl.BlockSpec((tm, w), lambda i: (i, 0))
    has_pre = pre_w is not None
    in_specs = [row(D_MODEL)] + [row(a.shape[1]) for a in acts]
    args = [x2d] + list(acts)
    if has_pre:
        in_specs.append(_const_spec(pre_w.shape))
        args.append(pre_w)
    in_specs += [_const_spec(w.shape) for w in wos]
    args += list(wos)
    in_specs += [_const_spec((1, D_MODEL))] * 4 + [_const_spec(w1.shape), _const_spec(w2.shape)]
    args += [ln1_g, ln1_b, ln2_g, ln2_b, w1, w2]
    return pl.pallas_call(
        functools.partial(_mix_ffn_kernel, n_act=len(acts), has_pre=has_pre),
        grid=(n_tok // tm,),
        in_specs=in_specs,
        out_specs=row(D_MODEL),
        out_shape=jax.ShapeDtypeStruct((n_tok, D_MODEL), F32),
        compiler_params=_cparams(("arbitrary",)),
        name=name,
    )(*args)


def _mix_ffn_sample_kernel(*refs, n_act, has_pre):
    x_ref = refs[0]
    act_refs = refs[1:1 + n_act]
    k = 1 + n_act
    pre_ref = refs[k] if has_pre else None
    k += int(has_pre)
    wo_refs = refs[k:k + n_act]
    k += n_act
    g1_ref, b1_ref, g2_ref, b2_ref, w1_ref, w2_ref, out_ref, w1b_ref, w2b_ref, x1_s, acc_s = refs[k:k + 11]
    c = pl.program_id(0)

    @pl.when(c == 0)
    def _():
        f = None
        for idx in range(n_act):
            a = act_refs[idx][...]
            if has_pre and idx == 0:
                a = _dot(a.astype(BF16), pre_ref[...])
            part = _dot(a.astype(BF16), wo_refs[idx][...])
            f = part if f is None else f + part
        x1_s[...] = _layer_norm(DN_ALPHA * x_ref[...] + f, g1_ref[...], b1_ref[...])
        acc_s[...] = jnp.zeros(acc_s.shape, F32)

    w1c = w1_ref[...].astype(BF16)
    w2c = w2_ref[...].astype(BF16)
    w1b_ref[...] = w1c
    w2b_ref[...] = w2c
    h = jnp.maximum(_dot(x1_s[...].astype(BF16), w1c), 0.0)
    acc_s[...] += _dot((h * h).astype(BF16), w2c)

    @pl.when(c == pl.num_programs(0) - 1)
    def _():
        out_ref[...] = _layer_norm(DN_ALPHA * x1_s[...] + acc_s[...], g2_ref[...], b2_ref[...])


def _mix_ffn_sample(x2d, acts, pre_w, wos, ln1_g, ln1_b, ln2_g, ln2_b, w1_all, w2_all, layer, name):
    n_tok = x2d.shape[0]
    has_pre = pre_w is not None
    full = lambda a: pl.BlockSpec(a.shape, lambda c: (0,) * a.ndim)
    args = [x2d] + list(acts) + ([pre_w] if has_pre else []) + list(wos) + [ln1_g, ln1_b, ln2_g, ln2_b]
    in_specs = [full(a) for a in args]
    in_specs += [pl.BlockSpec((None, D_MODEL, FFN_SAMPLE_HC), lambda c: (layer, 0, c)),
                 pl.BlockSpec((None, FFN_SAMPLE_HC, D_MODEL), lambda c: (layer, c, 0))]
    args += [w1_all, w2_all]
    return pl.pallas_call(
        functools.partial(_mix_ffn_sample_kernel, n_act=len(acts), has_pre=has_pre),
        grid=(FFN_HIDDEN // FFN_SAMPLE_HC,),
        in_specs=in_specs,
        out_specs=[pl.BlockSpec((n_tok, D_MODEL), lambda c: (0, 0)),
                   pl.BlockSpec((D_MODEL, FFN_SAMPLE_HC), lambda c: (0, c)),
                   pl.BlockSpec((FFN_SAMPLE_HC, D_MODEL), lambda c: (c, 0))],
        out_shape=[jax.ShapeDtypeStruct((n_tok, D_MODEL), F32),
                   jax.ShapeDtypeStruct((D_MODEL, FFN_HIDDEN), BF16),
                   jax.ShapeDtypeStruct((FFN_HIDDEN, D_MODEL), BF16)],
        scratch_shapes=[pltpu.VMEM((n_tok, D_MODEL), F32), pltpu.VMEM((n_tok, D_MODEL), F32)],
        compiler_params=_cparams(("arbitrary",)),
        name=name,
    )(*args)


def _even_inproj_sample_kernel(x_ref, w_ref, kvn_ref, gmg_ref, gmb_ref, ws0_ref, bs0_ref,
                               cos_ref, sina_ref, sinb_ref, qn_ref, wqn_ref, wqp_ref, wuk_ref,
                               ckv_ref, kpe_ref, gate_ref, vn_ref, qlat_ref, qpe_ref):
    z = _dot(x_ref[...].astype(BF16), w_ref[...])
    o_c, o_u, o_v, o_k = Q_RANK, Q_RANK + KV_RANK, Q_RANK + KV_RANK + GM_WIDTH, Q_RANK + KV_RANK + 2 * GM_WIDTH
    cos_t, sin_a, sin_b = cos_ref[...], sina_ref[...], sinb_ref[...]
    c_n = _rms_norm(z[:, o_c:o_u], kvn_ref[...])
    ckv_ref[...] = c_n
    kp = _rope128(z[:, o_k:o_k + LANES], cos_t, sin_a, sin_b)
    kpe_ref[...] = kp[:, :ROPE_DIM]
    gu = jax.nn.gelu(z[:, o_u:o_v])
    v_n = _layer_norm(jax.nn.gelu(z[:, o_v:o_k]), gmg_ref[...], gmb_ref[...])
    vn_ref[...] = v_n
    mixed = ws0_ref[...].astype(F32) * v_n.astype(BF16).astype(F32) + bs0_ref[...]
    gate_ref[...] = (gu * mixed).astype(BF16)
    lat, pes = _queries(z[:, :o_c], qn_ref[...], wqn_ref[...], wqp_ref[...], wuk_ref,
                        cos_t, sin_a, sin_b)
    qlat_ref[...] = lat
    qpe_ref[...] = jnp.concatenate(pes, axis=1)


def _even_inproj_sample(xs, w_in_p, kv_norm, gm_g, gm_b, ws0, bs0, cos_t, sin_a, sin_b,
                        qn_g, wq_nope, wq_pe, w_uk):
    n = xs.shape[0]
    args = (xs, w_in_p, kv_norm, gm_g, gm_b, ws0, bs0, cos_t, sin_a, sin_b, qn_g, wq_nope, wq_pe, w_uk)
    full = lambda shape: pl.BlockSpec(shape, lambda i: (0,) * len(shape))
    widths = [(KV_RANK, F32), (ROPE_DIM, F32), (GM_WIDTH, BF16), (GM_WIDTH, F32),
              (MLA_HEADS * LANES, F32), (MLA_HEADS * LANES, F32)]
    return pl.pallas_call(
        _even_inproj_sample_kernel,
        grid=(1,),
        in_specs=[full(a.shape) for a in args],
        out_specs=[full((n, w)) for w, _ in widths],
        out_shape=[jax.ShapeDtypeStruct((n, w), dt) for w, dt in widths],
        compiler_params=_cparams(("arbitrary",)),
        name="even_inproj_sample",
    )(*args)


def _decode_attn_kernel(pt_ref, qlat_ref, qpe_ref, ckvs_ref, kpes_ref, ckv_hbm, kpe_hbm, out_ref,
                        ckv_buf, kpe_buf, m_buf, sem, *, n_pg, n_groups):
    b = pl.program_id(0)
    n_b = pl.num_programs(0)
    assert n_groups % DEC_SLOTS == 0 and DEC_AHEAD + 2 <= DEC_SLOTS

    def group_copies(seq, grp):
        slot = grp % DEC_SLOTS
        cps = []
        for k in range(n_pg):
            page = pt_ref[seq, grp * n_pg + k]
            keys = pl.ds(k * PAGE_SIZE, PAGE_SIZE)
            cps.append(pltpu.make_async_copy(ckv_hbm.at[page], ckv_buf.at[slot, keys, :], sem.at[0, slot]))
            cps.append(pltpu.make_async_copy(kpe_hbm.at[page], kpe_buf.at[slot, :, keys], sem.at[1, slot]))
        return cps

    @pl.when(b == 0)
    def _():
        for g in range(DEC_AHEAD):
            for cp in group_copies(0, g):
                cp.start()

    heads, span = MLA_HEADS, DEC_SPAN_PAGES * PAGE_SIZE
    half_span = span // 2
    n_span = n_pg // DEC_SPAN_PAGES
    qlat = qlat_ref[...]
    zero = jnp.zeros_like(qlat)
    q2 = jnp.concatenate([jnp.concatenate([qlat, zero], axis=1),
                          jnp.concatenate([zero, qlat], axis=1)], axis=0).astype(BF16)
    qpe = qpe_ref[...][:, :ROPE_DIM].astype(BF16)

    def scores(slot):
        s_pe = _dot(qpe, kpe_buf[slot].astype(BF16))
        pieces = []
        for j in range(n_span):
            m_buf[slot, j] = jnp.concatenate(
                [ckv_buf[slot, pl.ds(j * span, half_span), :],
                 ckv_buf[slot, pl.ds(j * span + half_span, half_span), :]], axis=1).astype(BF16)
            s2 = _dot_nt(q2, m_buf[slot, j])
            pieces += [s2[:heads] + s_pe[:, j * span:j * span + half_span],
                       s2[heads:] + s_pe[:, j * span + half_span:(j + 1) * span]]
        return jnp.concatenate(pieces, axis=1) * ATTN_SCALE

    def values(p, grp):
        slot = grp % DEC_SLOTS
        o2 = None
        for j in range(n_span):
            lhs = jnp.concatenate([p[:, j * span:j * span + half_span],
                                   p[:, j * span + half_span:(j + 1) * span]], axis=0).astype(BF16)
            part = _dot(lhs, m_buf[slot, j])
            o2 = part if o2 is None else o2 + part
        return o2[:heads, :KV_RANK] + o2[heads:, KV_RANK:]

    state = dict(m_run=None, m=None, l=None, acc=None)
    raw = {}
    probs = {}

    def softmax(k):
        s = raw.pop(k)
        m_cur = jnp.max(s, axis=-1, keepdims=True)
        m_new = m_cur if state["m_run"] is None else jnp.maximum(state["m_run"], m_cur)
        state["m_run"] = m_new
        probs[k] = (jnp.exp(s - m_new), m_new)

    def fold(k):
        p, m_k = probs.pop(k)
        pv = values(p, k)
        p_sum = jnp.sum(p, axis=-1, keepdims=True)
        if state["m"] is None:
            state["l"], state["acc"] = p_sum, pv
        else:
            alpha = jnp.exp(state["m"] - m_k)
            state["l"], state["acc"] = alpha * state["l"] + p_sum, alpha * state["acc"] + pv
        state["m"] = m_k

    for g in range(n_groups):
        if g + DEC_AHEAD < n_groups:
            for cp in group_copies(b, g + DEC_AHEAD):
                cp.start()
        else:
            @pl.when(b + 1 < n_b)
            def _():
                for cp in group_copies(b + 1, g + DEC_AHEAD - n_groups):
                    cp.start()
        for cp in group_copies(b, g):
            cp.wait()
        raw[g] = scores(g % DEC_SLOTS)
        if g >= 1:
            softmax(g - 1)
        if g >= 2:
            fold(g - 2)
    softmax(n_groups - 1)
    fold(n_groups - 2)
    fold(n_groups - 1)
    m, l, acc = state["m"], state["l"], state["acc"]

    r = lambda a: a.astype(BF16).astype(F32)
    kv = r(ckvs_ref[...])
    s_self = (jnp.sum(r(qlat_ref[...]) * kv, axis=-1, keepdims=True)
              + jnp.sum(r(qpe_ref[...]) * r(kpes_ref[...]), axis=-1, keepdims=True)) * ATTN_SCALE
    m_n = jnp.maximum(m, s_self)
    a = jnp.exp(m - m_n)
    p_self = jnp.exp(s_self - m_n)
    out_ref[...] = (a * acc + r(p_self) * kv) / (a * l + p_self)


def _decode_attn(page_table, qlat2, qpe2, ckv_s3, kpe_s3, cache_ckv_e, cache_kpe_t):
    dec_batch, n_pages = page_table.shape
    n_pg = DEC_PG
    qspec = pl.BlockSpec((MLA_HEADS, LANES), lambda b, pt: (b, 0))
    self_spec = pl.BlockSpec((None, 1, LANES), lambda b, pt: (b, 0, 0))
    hbm = pl.BlockSpec(memory_space=pl.ANY)
    grid_spec = pltpu.PrefetchScalarGridSpec(
        num_scalar_prefetch=1,
        grid=(dec_batch,),
        in_specs=[qspec, qspec, self_spec, self_spec, hbm, hbm],
        out_specs=qspec,
        scratch_shapes=[pltpu.VMEM((DEC_SLOTS, n_pg * PAGE_SIZE, KV_RANK), F32),
                        pltpu.VMEM((DEC_SLOTS, ROPE_DIM, n_pg * PAGE_SIZE), F32),
                        pltpu.VMEM((DEC_SLOTS, n_pg // DEC_SPAN_PAGES, DEC_SPAN_PAGES // 2 * PAGE_SIZE, 2 * KV_RANK),
                                   BF16),
                        pltpu.SemaphoreType.DMA((2, DEC_SLOTS))],
    )
    return pl.pallas_call(
        functools.partial(_decode_attn_kernel, n_pg=n_pg, n_groups=n_pages // n_pg),
        grid_spec=grid_spec,
        out_shape=jax.ShapeDtypeStruct((dec_batch * MLA_HEADS, KV_RANK), F32),
        compiler_params=_cparams(("arbitrary",)),
        name="decode_attn",
    )(page_table, qlat2, qpe2, ckv_s3, kpe_s3, cache_ckv_e, cache_kpe_t)


def _s5_readout(hcat_blocks, u, cw_ref, d_ref, wglu_ref):
    y = jnp.concatenate([_dot(hcat_blocks[k], cw_ref[k]) for k in range(S5_NB)], axis=1)
    y = jax.nn.gelu(y + d_ref[...] * u)
    return y * jax.nn.sigmoid(_dot(y.astype(BF16), wglu_ref[...]))


def _s5_prompt_kernel(x_ref, win_ref, bw_ref, a_ref, cw_ref, d_ref, wglu_ref,
                      yg_ref, hre_ref, him_ref, u_s, y_s, bu_s, h_s):
    n_b, seg, _ = x_ref.shape
    half = S5_SLABS // 2
    sub = S5_SUB
    n_sub = seg // sub
    rows_sub = sub * n_b
    q_slabs = S5_WIDTH // LANES

    @pl.when(pl.program_id(0) == 0)
    def _():
        h_s[...] = jnp.zeros(h_s.shape, F32)

    def project(s):
        rows = slice(s * rows_sub, (s + 1) * rows_sub)
        xs = jnp.concatenate([x_ref[b, s * sub:(s + 1) * sub, :] for b in range(n_b)], axis=0)
        u = _dot(xs.astype(BF16), win_ref[...])
        for b in range(n_b):
            for q in range(q_slabs):
                u_s[s, q, pl.ds(b, sub, stride=n_b), :] = u[b * sub:(b + 1) * sub, q * LANES:(q + 1) * LANES]
        for k in range(S5_NB):
            r = _dot(u_s[s, k].astype(BF16), bw_ref[k])
            for q in range(S5_BLK_SLABS):
                bu_s[S5_BLK_SLABS * k + q, rows, :] = r[:, q * LANES:(q + 1) * LANES]
                bu_s[half + S5_BLK_SLABS * k + q, rows, :] = r[:, (S5_BLK_SLABS + q) * LANES:
                                                                (S5_BLK_SLABS + q + 1) * LANES]

    def recur(s, h):
        for t in range(s * sub, (s + 1) * sub):
            rows = slice(t * n_b, (t + 1) * n_b)
            hs = bu_s[:, rows, :]
            ar, ai = a_ref[:half], a_ref[half:]
            hr, hi = h[:half], h[half:]
            h = jnp.concatenate([ar * hr - ai * hi + hs[:half], ar * hi + ai * hr + hs[half:]], axis=0)
            bu_s[:, rows, :] = h
        return h

    def read_out(s):
        rows = slice(s * rows_sub, (s + 1) * rows_sub)
        blocks = [jnp.concatenate([bu_s[S5_BLK_SLABS * k + q, rows, :] for q in range(S5_BLK_SLABS)]
                                  + [bu_s[half + S5_BLK_SLABS * k + q, rows, :] for q in range(S5_BLK_SLABS)],
                                  axis=1).astype(BF16)
                  for k in range(S5_NB)]
        u_t = jnp.concatenate([u_s[s, q] for q in range(q_slabs)], axis=1)
        y = _s5_readout(blocks, u_t, cw_ref, d_ref, wglu_ref)
        for q in range(q_slabs):
            y_s[s, q] = y[:, q * LANES:(q + 1) * LANES]
        for b in range(n_b):
            yg_ref[b, s * sub:(s + 1) * sub, :] = jnp.concatenate(
                [y_s[s, q, pl.ds(b, sub, stride=n_b), :] for q in range(q_slabs)], axis=1).astype(BF16)

    h = h_s[...]
    project(0)
    for s in range(n_sub):
        if s + 1 < n_sub:
            project(s + 1)
        h = recur(s, h)
        if s >= 1:
            read_out(s - 1)
    read_out(n_sub - 1)
    h_s[...] = h
    hre_ref[...] = jnp.concatenate([h[s] for s in range(half)], axis=1)
    him_ref[...] = jnp.concatenate([h[half + s] for s in range(half)], axis=1)


def _s5_prompt(x3, w_in, bw, a_b, cw, d_row, w_glu):
    batch, seq, _ = x3.shape
    seg = S5_L
    slab_rows = (seg // S5_SUB, S5_WIDTH // LANES, S5_SUB * batch, LANES)
    return pl.pallas_call(
        _s5_prompt_kernel,
        grid=(seq // seg,),
        in_specs=[pl.BlockSpec((batch, seg, D_MODEL), lambda c: (0, c, 0)),
                  _const_spec(w_in.shape), _const_spec(bw.shape), _const_spec(a_b.shape),
                  _const_spec(cw.shape), _const_spec(d_row.shape), _const_spec(w_glu.shape)],
        out_specs=[pl.BlockSpec((batch, seg, S5_WIDTH), lambda c: (0, c, 0)),
                   pl.BlockSpec((batch, S5_STATES), lambda c: (0, 0)),
                   pl.BlockSpec((batch, S5_STATES), lambda c: (0, 0))],
        out_shape=[jax.ShapeDtypeStruct((batch, seq, S5_WIDTH), BF16),
                   jax.ShapeDtypeStruct((batch, S5_STATES), F32),
                   jax.ShapeDtypeStruct((batch, S5_STATES), F32)],
        scratch_shapes=[pltpu.VMEM(slab_rows, F32), pltpu.VMEM(slab_rows, F32),
                        pltpu.VMEM((S5_SLABS, seg * batch, LANES), F32),
                        pltpu.VMEM((S5_SLABS, batch, LANES), F32)],
        compiler_params=_cparams(("arbitrary",)),
        name="s5_prompt",
    )(x3, w_in, bw, a_b, cw, d_row, w_glu)


def _s5_sample_kernel(x_ref, h0r_ref, h0i_ref, win_ref, bw_ref, ar_ref, ai_ref, cw_ref, d_ref, wglu_ref,
                      yg_ref, hre_ref, him_ref):
    u = _dot(x_ref[...].astype(BF16), win_ref[...])
    ub = u.astype(BF16)
    bu = [_dot(ub[:, k * LANES:(k + 1) * LANES], bw_ref[k]) for k in range(S5_NB)]
    w = S5_KB * S5_STATE
    bu_r = jnp.concatenate([r[:, :w] for r in bu], axis=1)
    bu_i = jnp.concatenate([r[:, w:] for r in bu], axis=1)
    ar, ai, h0r, h0i = ar_ref[...], ai_ref[...], h0r_ref[...], h0i_ref[...]
    hr = bu_r + (ar * h0r - ai * h0i)
    hi = bu_i + (ar * h0i + ai * h0r)
    hre_ref[...] = hr
    him_ref[...] = hi
    blocks = [jnp.concatenate([hr[:, k * w:(k + 1) * w], hi[:, k * w:(k + 1) * w]], axis=1).astype(BF16)
              for k in range(S5_NB)]
    yg_ref[...] = _s5_readout(blocks, u, cw_ref, d_ref, wglu_ref).astype(BF16)


def _s5_sample(xs, h0r, h0i, w_in, bw, a_r, a_i, cw, d_row, w_glu):
    n = xs.shape[0]
    args = (xs, h0r, h0i, w_in, bw, a_r, a_i, cw, d_row, w_glu)
    full = lambda shape: pl.BlockSpec(shape, lambda i: (0,) * len(shape))
    return pl.pallas_call(
        _s5_sample_kernel,
        grid=(1,),
        in_specs=[full(a.shape) for a in args],
        out_specs=[full((n, S5_WIDTH)), full((n, S5_STATES)), full((n, S5_STATES))],
        out_shape=[jax.ShapeDtypeStruct((n, S5_WIDTH), BF16),
                   jax.ShapeDtypeStruct((n, S5_STATES), F32),
                   jax.ShapeDtypeStruct((n, S5_STATES), F32)],
        compiler_params=_cparams(("arbitrary",)),
        name="s5_sample",
    )(*args)


def _rope_tables(pos):
    freqs = ROPE_THETA ** (-jnp.arange(ROPE_HALF, dtype=F32) / ROPE_HALF)
    ang = pos.astype(F32)[:, None] * freqs
    cos, sin = jnp.cos(ang), jnp.sin(ang)
    zero = jnp.zeros_like(cos)
    pad = jnp.zeros((pos.shape[0], LANES - ROPE_DIM), F32)
    cos_t = jnp.concatenate([cos, cos, pad], axis=1)
    sin_a = jnp.concatenate([zero, sin, pad], axis=1)
    sin_b = jnp.concatenate([-sin, zero, pad], axis=1)
    return cos_t, sin_a, sin_b


def _block_diag(blocks):
    *lead, n, r, c = blocks.shape
    eye = jnp.eye(n, dtype=blocks.dtype)
    return jnp.einsum('...nrc,nm->...nrmc', blocks, eye).reshape(*lead, n * r, n * c)


def _even_params(w_in, q_norm, w_q_b, kv_norm, w_kv_b, gm_g, gm_b, gm_w_s, gm_b_s, w_out):
    i1, i2, i3, i4 = Q_RANK, Q_RANK + KV_RANK, Q_RANK + KV_RANK + ROPE_DIM, Q_RANK + KV_RANK + ROPE_DIM + GM_WIDTH
    w_in_p = jnp.concatenate([w_in[:, :i2], w_in[:, i3:], w_in[:, i2:i3],
                              jnp.zeros((D_MODEL, LANES - ROPE_DIM), F32)], axis=1).astype(BF16)
    wq_nope = w_q_b[:, :, :NOPE_DIM].reshape(Q_RANK, MLA_HEADS * NOPE_DIM).astype(BF16)
    pe = w_q_b[:, :, NOPE_DIM:]
    pe_sw = jnp.concatenate([pe[:, :, ROPE_HALF:], pe[:, :, :ROPE_HALF]], axis=2)
    lane_pad = jnp.zeros((Q_RANK, MLA_HEADS, LANES - ROPE_DIM), F32)
    wq_pe = jnp.concatenate(
        [jnp.concatenate([w, lane_pad], axis=2).reshape(Q_RANK, MLA_HEADS * LANES) for w in (pe, pe_sw)],
        axis=1).astype(BF16)
    uk = jnp.transpose(w_kv_b[:, :, :NOPE_DIM], (1, 2, 0))
    uv = jnp.transpose(w_kv_b[:, :, NOPE_DIM:], (1, 0, 2))
    pairs = MLA_HEADS // 2
    w_uk = _block_diag(uk.reshape(pairs, 2, NOPE_DIM, KV_RANK)).astype(BF16)
    w_uv_p = _block_diag(uv.reshape(pairs, 2, KV_RANK, V_DIM)).astype(BF16)
    w_uv = _block_diag(uv).astype(BF16)
    causal = jnp.tril(jnp.ones((GM_CHUNK, GM_CHUNK), dtype=bool))
    ws_m = jnp.where(causal[None], gm_w_s, 0).astype(BF16)
    bs_full = jnp.repeat(gm_b_s.T, GM_HEAD_DIM, axis=1)
    ws0 = jnp.repeat(gm_w_s[:, 0, 0], GM_HEAD_DIM)[None, :].astype(BF16)
    bs0 = bs_full[:1]
    n_attn = MLA_HEADS * V_DIM
    return dict(w_in_p=w_in_p, qn=q_norm[None, :], wq_nope=wq_nope, wq_pe=wq_pe, w_uk=w_uk, w_uv=w_uv, w_uv_p=w_uv_p,
                kvn=kv_norm[None, :], gm_g=gm_g[None, :], gm_b=gm_b[None, :], ws_m=ws_m, bs_full=bs_full,
                ws0=ws0, bs0=bs0, wo_attn=w_out[:n_attn].astype(BF16), wo_gate=w_out[n_attn:].astype(BF16))


def _s5_params(w_in, a_re, a_im, b_re, b_im, c_re, c_im, d, log_dt, w_glu, w_out, batch):
    dt = jnp.exp(log_dt)[:, None]
    ld_r, ld_i = a_re * dt, a_im * dt
    mag = jnp.exp(ld_r)
    ab_r, ab_i = mag * jnp.cos(ld_i), mag * jnp.sin(ld_i)
    den = a_re * a_re + a_im * a_im
    cr = ((ab_r - 1.0) * a_re + ab_i * a_im) / den
    ci = (ab_i * a_re - (ab_r - 1.0) * a_im) / den
    bb_r = cr[..., None] * b_re - ci[..., None] * b_im
    bb_i = cr[..., None] * b_im + ci[..., None] * b_re

    def in_blocks(bb):
        return _block_diag(jnp.swapaxes(bb, 1, 2).reshape(S5_NB, S5_KB, S5_GROUP_DIM, S5_STATE))

    def out_blocks(cc):
        return _block_diag(jnp.swapaxes(cc, 1, 2).reshape(S5_NB, S5_KB, S5_STATE, S5_GROUP_DIM))

    bw = jnp.concatenate([in_blocks(bb_r), in_blocks(bb_i)], axis=2).astype(BF16)
    cw = jnp.concatenate([out_blocks(c_re), out_blocks(-c_im)], axis=1).astype(BF16)
    a_r, a_i = ab_r.reshape(1, S5_STATES), ab_i.reshape(1, S5_STATES)
    half = S5_SLABS // 2
    a_b = jnp.concatenate([jnp.broadcast_to(a_r.reshape(half, 1, LANES), (half, batch, LANES)),
                           jnp.broadcast_to(a_i.reshape(half, 1, LANES), (half, batch, LANES))], axis=0)
    return dict(w_in=w_in.astype(BF16), bw=bw, cw=cw, a_r=a_r, a_i=a_i, a_b=a_b,
                d=d.reshape(1, S5_WIDTH), w_glu=w_glu.astype(BF16), w_out=w_out.astype(BF16))


def kernel(x_prompt, x_sample, cache_ckv, cache_kpe, state_s5_re, state_s5_im, page_table, w_in_even, mla_q_norm, mla_w_q_b, mla_kv_norm, mla_w_kv_b, gm_norm_g, gm_norm_b, gm_w_s, gm_b_s, w_out_even, w_in_odd, s5_a_re, s5_a_im, s5_b_re, s5_b_im, s5_c_re, s5_c_im, s5_d, s5_log_dt, s5_w_glu, w_out_odd, ln_mix_g, ln_mix_b, ln_ffn_g, ln_ffn_b, ffn_w1, ffn_w2):
    batch, seq, _ = x_prompt.shape
    dec_batch, dec_seq, _ = x_sample.shape
    assert dec_seq == 1 and seq % INPROJ_TM == 0 and seq % ATTN_TQ == 0 and seq % S5_L == 0
    assert (seq // ATTN_TQ) % 2 == 0
    assert (batch * seq) % FFN_TM == 0 and page_table.shape[1] % DEC_PG == 0
    assert batch == SUBLANES

    xp = x_prompt.reshape(batch * seq, D_MODEL)
    xs = x_sample.reshape(dec_batch, D_MODEL)
    tabs_p = _rope_tables(jnp.arange(seq, dtype=jnp.int32))
    tabs_s = _rope_tables(PAST_LEN + jnp.arange(dec_seq, dtype=jnp.int32))

    outs = {k: [] for k in ("ckv_p", "kpe_p", "ckv_s", "kpe_s", "gmv_p", "gmv_s",
                            "s5re_p", "s5im_p", "s5re_s", "s5im_s")}
    for layer in range(DEPTH):
        ln = (ln_mix_g[layer][None, :], ln_mix_b[layer][None, :],
              ln_ffn_g[layer][None, :], ln_ffn_b[layer][None, :])
        if layer % 2 == 0:
            e = layer // 2
            p = _even_params(w_in_even[e], mla_q_norm[e], mla_w_q_b[e], mla_kv_norm[e], mla_w_kv_b[e],
                             gm_norm_g[e], gm_norm_b[e], gm_w_s[e], gm_b_s[e], w_out_even[e])
            q_a, ckv, kpe, kcat, gate, gmv = _even_inproj(
                xp, p["w_in_p"], p["kvn"], p["gm_g"], p["gm_b"], p["ws_m"], p["bs_full"], *tabs_p, batch, seq)
            attn = _mla_attn(q_a, kcat, p["qn"], p["wq_nope"], p["wq_pe"], p["w_uk"], p["w_uv_p"], *tabs_p,
                             batch, seq)
            outs["ckv_p"].append(ckv.reshape(batch, seq, KV_RANK))
            outs["kpe_p"].append(kpe.reshape(batch, seq, ROPE_DIM))
            outs["gmv_p"].append(gmv)
            ckv_s, kpe_s, gate_s, vn_s, qlat_s, qpe_s = _even_inproj_sample(
                xs, p["w_in_p"], p["kvn"], p["gm_g"], p["gm_b"], p["ws0"], p["bs0"], *tabs_s,
                p["qn"], p["wq_nope"], p["wq_pe"], p["w_uk"])
            kpe_pad = jnp.concatenate([kpe_s, jnp.zeros((dec_batch, LANES - ROPE_DIM), F32)], axis=1)
            o_lat = _decode_attn(page_table,
                                 qlat_s.reshape(dec_batch * MLA_HEADS, LANES),
                                 qpe_s.reshape(dec_batch * MLA_HEADS, LANES),
                                 ckv_s.reshape(dec_batch, 1, KV_RANK), kpe_pad.reshape(dec_batch, 1, LANES),
                                 cache_ckv[e], jnp.swapaxes(cache_kpe[e], 1, 2))
            xs, w1_b, w2_b = _mix_ffn_sample(
                xs, [o_lat.reshape(dec_batch, MLA_HEADS * KV_RANK), gate_s], p["w_uv"],
                [p["wo_attn"], p["wo_gate"]], *ln, ffn_w1, ffn_w2, layer, name="even_ffn_sample")
            xp = _mix_ffn(xp, [attn, gate], None, [p["wo_attn"], p["wo_gate"]], *ln, w1_b, w2_b,
                          name="even_ffn_prompt")
            outs["ckv_s"].append(ckv_s.reshape(dec_batch, 1, KV_RANK))
            outs["kpe_s"].append(kpe_s.reshape(dec_batch, 1, ROPE_DIM))
            outs["gmv_s"].append(vn_s.reshape(dec_batch, 1, GM_WIDTH))
        else:
            o = layer // 2
            p = _s5_params(w_in_odd[o], s5_a_re[o], s5_a_im[o], s5_b_re[o], s5_b_im[o], s5_c_re[o],
                           s5_c_im[o], s5_d[o], s5_log_dt[o], s5_w_glu[o], w_out_odd[o], batch)
            yg, hre, him = _s5_prompt(xp.reshape(batch, seq, D_MODEL), p["w_in"], p["bw"], p["a_b"],
                                      p["cw"], p["d"], p["w_glu"])
            outs["s5re_p"].append(hre.reshape(batch, S5_GROUPS, S5_STATE))
            outs["s5im_p"].append(him.reshape(batch, S5_GROUPS, S5_STATE))
            yg_s, hre_s, him_s = _s5_sample(
                xs, state_s5_re[o].reshape(dec_batch, S5_STATES), state_s5_im[o].reshape(dec_batch, S5_STATES),
                p["w_in"], p["bw"], p["a_r"], p["a_i"], p["cw"], p["d"], p["w_glu"])
            xs, w1_b, w2_b = _mix_ffn_sample(xs, [yg_s], None, [p["w_out"]], *ln, ffn_w1, ffn_w2, layer,
                                             name="odd_ffn_sample")
            xp = _mix_ffn(xp, [yg.reshape(batch * seq, S5_WIDTH)], None, [p["w_out"]], *ln, w1_b, w2_b,
                          name="odd_ffn_prompt")
            outs["s5re_s"].append(hre_s.reshape(dec_batch, S5_GROUPS, S5_STATE))
            outs["s5im_s"].append(him_s.reshape(dec_batch, S5_GROUPS, S5_STATE))

    st = jnp.stack
    return (xp.reshape(batch, seq, D_MODEL), xs.reshape(dec_batch, dec_seq, D_MODEL),
            st(outs["ckv_p"]), st(outs["kpe_p"]), st(outs["ckv_s"]), st(outs["kpe_s"]),
            st(outs["gmv_p"]), st(outs["gmv_s"]),
            st(outs["s5re_p"]), st(outs["s5im_p"]), st(outs["s5re_s"]), st(outs["s5im_s"]))
```

```python
import functools
import math

import jax
import jax.numpy as jnp
from jax import lax
from jax.experimental import pallas as pl
from jax.experimental.pallas import tpu as pltpu

F32 = jnp.float32
BF16 = jnp.bfloat16

D_MODEL = 1024
DEPTH = 2
PAST_LEN = 16384
PAGE_SIZE = 128
MLA_HEADS = 8
NOPE_DIM = 64
ROPE_DIM = 32
ROPE_HALF = ROPE_DIM // 2
V_DIM = 64
Q_RANK = 256
KV_RANK = 128
ROPE_THETA = 10000.0
ATTN_SCALE = 1.0 / math.sqrt(NOPE_DIM + ROPE_DIM)
ATTN_SCALE_LOG2 = ATTN_SCALE * math.log2(math.e)
GM_HEADS = 8
GM_HEAD_DIM = 64
GM_WIDTH = GM_HEADS * GM_HEAD_DIM
GM_CHUNK = 128
S5_GROUP_DIM = 16
S5_GROUPS = 32
S5_STATE = 64
S5_WIDTH = S5_GROUPS * S5_GROUP_DIM
S5_STATES = S5_GROUPS * S5_STATE
FFN_HIDDEN = 4 * D_MODEL
DN_ALPHA = (2 * DEPTH) ** 0.25
LN_EPS = 1e-5
RMS_EPS = 1e-6

LANES = 128
SUBLANES = 8
VMEM_LIMIT_BYTES = 56 * 1024 * 1024

INPROJ_TM = 512
INPROJ_ROW_BLOCK = 256
ATTN_TQ = 256
ATTN_ROW_BLOCK = 256
FFN_TM = 1024
FFN_HC = 1024
FFN_ROW_BLOCK = 256
S5_L = 128
S5_SUB = 32
DEC_PG = 32
DEC_SLOTS = 4
DEC_AHEAD = 2
DEC_SPAN_PAGES = 4
S5_KB = 8
S5_NB = S5_GROUPS // S5_KB
S5_SLABS = 2 * S5_STATES // LANES
S5_BLK_SLABS = S5_KB * S5_STATE // LANES


def _cparams(sem):
    return pltpu.CompilerParams(dimension_semantics=sem, vmem_limit_bytes=VMEM_LIMIT_BYTES)


def _const_spec(shape):
    n = len(shape)
    return pl.BlockSpec(shape, lambda *_: (0,) * n, pipeline_mode=pl.Buffered(1))


def _layer_norm(x, g, b):
    mu = jnp.mean(x, axis=-1, keepdims=True)
    xc = x - mu
    var = jnp.mean(xc * xc, axis=-1, keepdims=True)
    return xc * lax.rsqrt(var + LN_EPS) * g + b


def _rms_norm(x, g):
    return x * lax.rsqrt(jnp.mean(x * x, axis=-1, keepdims=True) + RMS_EPS) * g


def _rope128(x, cos_t, sin_a, sin_b):
    return (x * cos_t + pltpu.roll(x, ROPE_HALF, 1) * sin_a
            + pltpu.roll(x, LANES - ROPE_HALF, 1) * sin_b)


def _dot(a, b):
    return jnp.dot(a, b, preferred_element_type=F32)


def _dot_nt(a, b):
    return lax.dot_general(a, b, (((1,), (1,)), ((), ())), preferred_element_type=F32)


def _even_inproj_kernel(x_ref, w_ref, kvn_ref, gmg_ref, gmb_ref, ws_ref, bs_ref,
                        cos_ref, sina_ref, sinb_ref,
                        qa_ref, ckv_ref, kpe_ref, kcat_ref, gate_ref, gmv_ref):
    tm = x_ref.shape[0]
    o_c, o_u, o_v, o_k = Q_RANK, Q_RANK + KV_RANK, Q_RANK + KV_RANK + GM_WIDTH, Q_RANK + KV_RANK + 2 * GM_WIDTH
    rb = min(INPROJ_ROW_BLOCK, tm)
    blocks = [slice(r0, r0 + rb) for r0 in range(0, tm, rb)]
    zs = [_dot(x_ref[rows, :].astype(BF16), w_ref[...]) for rows in blocks]

    gus, vns = [], []
    for rows, z in zip(blocks, zs):
        qa_ref[rows, :] = z[:, :o_c]
        c_n = _rms_norm(z[:, o_c:o_u], kvn_ref[...])
        ckv_ref[rows, :] = c_n
        kp = _rope128(z[:, o_k:o_k + LANES], cos_ref[rows, :], sina_ref[rows, :], sinb_ref[rows, :])
        kpe_ref[rows, :] = kp[:, :ROPE_DIM]
        kcat_ref[rows, :] = jnp.concatenate([c_n, kp], axis=1).astype(BF16)
        gus.append(jax.nn.gelu(z[:, o_u:o_v]))
        vns.append(_layer_norm(jax.nn.gelu(z[:, o_v:o_k]), gmg_ref[...], gmb_ref[...]))

    gmv_ref[0] = vns[-1][rb - GM_CHUNK:, :]

    lane = lax.broadcasted_iota(jnp.int32, (GM_CHUNK, LANES), 1)
    lo = lane < GM_HEAD_DIM
    bs = bs_ref[...]
    for rows, gu, v_n in zip(blocks, gus, vns):
        for ci in range(rb // GM_CHUNK):
            chunk = slice(ci * GM_CHUNK, (ci + 1) * GM_CHUNK)
            pieces = []
            for pr in range(GM_WIDTH // LANES):
                r = v_n[chunk, pr * LANES:(pr + 1) * LANES]
                m0 = _dot(ws_ref[2 * pr], jnp.where(lo, r, 0.0).astype(BF16))
                m1 = _dot(ws_ref[2 * pr + 1], jnp.where(lo, 0.0, r).astype(BF16))
                pieces.append(m0 + m1)
            mixed = jnp.concatenate(pieces, axis=1) + bs
            gate_ref[pl.ds(rows.start + ci * GM_CHUNK, GM_CHUNK), :] = (gu[chunk, :] * mixed).astype(BF16)


def _even_inproj(x2d, w_in_p, kv_norm, gm_g, gm_b, ws_m, bs_full, cos_t, sin_a, sin_b, batch, seq):
    n_tok = x2d.shape[0]
    tm = INPROJ_TM
    tiles_per_seq = seq // tm
    row = lambda w: pl.BlockSpec((tm, w), lambda i: (i, 0))
    tab = pl.BlockSpec((tm, LANES), lambda i: (i % tiles_per_seq, 0))
    n_in = w_in_p.shape[1]
    return pl.pallas_call(
        _even_inproj_kernel,
        grid=(n_tok // tm,),
        in_specs=[row(D_MODEL), _const_spec((D_MODEL, n_in)), _const_spec((1, KV_RANK)),
                  _const_spec((1, GM_WIDTH)), _const_spec((1, GM_WIDTH)),
                  _const_spec((GM_HEADS, GM_CHUNK, GM_CHUNK)), _const_spec((GM_CHUNK, GM_WIDTH)),
                  tab, tab, tab],
        out_specs=[row(Q_RANK), row(KV_RANK), row(ROPE_DIM), row(2 * LANES), row(GM_WIDTH),
                   pl.BlockSpec((1, GM_CHUNK, GM_WIDTH), lambda i: (i // tiles_per_seq, 0, 0))],
        out_shape=[jax.ShapeDtypeStruct((n_tok, Q_RANK), F32),
                   jax.ShapeDtypeStruct((n_tok, KV_RANK), F32),
                   jax.ShapeDtypeStruct((n_tok, ROPE_DIM), F32),
                   jax.ShapeDtypeStruct((n_tok, 2 * LANES), BF16),
                   jax.ShapeDtypeStruct((n_tok, GM_WIDTH), BF16),
                   jax.ShapeDtypeStruct((batch, GM_CHUNK, GM_WIDTH), F32)],
        compiler_params=_cparams(("arbitrary",)),
        name="even_inproj",
    )(x2d, w_in_p, kv_norm, gm_g, gm_b, ws_m, bs_full, cos_t, sin_a, sin_b)


def _queries(q_a, qn_g, wq_nope, wq_pe, wuk_ref, cos_t, sin_a, sin_b):
    qn = _rms_norm(q_a, qn_g).astype(BF16)
    nope = _dot(qn, wq_nope).astype(BF16)
    lat = jnp.concatenate([_dot(nope[:, pr * LANES:(pr + 1) * LANES], wuk_ref[pr])
                           for pr in range(MLA_HEADS // 2)], axis=1)
    pe = _dot(qn, wq_pe)
    n = MLA_HEADS * LANES
    sin_s = sin_a + sin_b
    pes = [pe[:, h * LANES:(h + 1) * LANES] * cos_t + pe[:, n + h * LANES:n + (h + 1) * LANES] * sin_s
           for h in range(MLA_HEADS)]
    return lat, pes


def _mla_attn_kernel(qa_lo_ref, qa_hi_ref, kcat_ref, qn_ref, wqn_ref, wqp_ref, wuk_ref, wuv_ref,
                     cos_lo_ref, sina_lo_ref, sinb_lo_ref, cos_hi_ref, sina_hi_ref, sinb_hi_ref,
                     out_hbm, qcat_s, m_s, acc_s, obuf, osem, *, nq):
    tq = qa_lo_ref.shape[0]
    n_rows = MLA_HEADS * tq
    b, i = pl.program_id(0), pl.program_id(1)
    step = b * pl.num_programs(1) + i
    n_steps = pl.num_programs(0) * pl.num_programs(1)
    q_tiles = (i, nq - 1 - i)

    def out_copies():
        return [pltpu.make_async_copy(
            obuf.at[t], out_hbm.at[pl.ds(pl.multiple_of((b * nq + q_tiles[t]) * tq, tq), tq), :], osem.at[t])
            for t in range(2)]

    qa = (qa_lo_ref, qa_hi_ref)
    tabs = ((cos_lo_ref, sina_lo_ref, sinb_lo_ref), (cos_hi_ref, sina_hi_ref, sinb_hi_ref))
    for t in range(2):
        lat, pes = _queries(qa[t][...], qn_ref[...], wqn_ref[...], wqp_ref[...], wuk_ref,
                            *(r[...] for r in tabs[t]))
        for h in range(MLA_HEADS):
            qcat_s[t, h * tq:(h + 1) * tq, :] = (ATTN_SCALE_LOG2 * jnp.concatenate(
                [lat[:, h * LANES:(h + 1) * LANES], pes[h]], axis=1)).astype(BF16)

    lanes_x = lambda a, n: jnp.concatenate([a] * n, axis=1)
    rb = ATTN_ROW_BLOCK
    ones = jnp.ones((tq, LANES), BF16)

    def keys(kv_tile):
        k = kcat_ref[pl.ds(pl.multiple_of(kv_tile * tq, tq), tq), :]
        return k, jnp.concatenate([k[:, :KV_RANK], ones], axis=1)

    q_pos = lax.broadcasted_iota(jnp.int32, (rb, tq), 0) & (tq - 1)
    k_pos = lax.broadcasted_iota(jnp.int32, (rb, tq), 1)
    causal = k_pos <= q_pos
    for t in range(2):
        k, v1 = keys(q_tiles[t])
        for r0 in range(0, n_rows, rb):
            rows = slice(r0, r0 + rb)
            s = jnp.where(causal, _dot_nt(qcat_s[t, rows, :], k), -jnp.inf)
            m0 = jnp.broadcast_to(jnp.max(s, axis=-1, keepdims=True), (rb, LANES))
            m_s[t, rows, :] = m0
            acc_s[t, rows, :] = _dot(jnp.exp2(s - lanes_x(m0, tq // LANES)).astype(BF16), v1)

    for u in range(nq - 1):
        hi = u >= i
        t = hi.astype(jnp.int32)
        k, v1 = keys(jnp.where(hi, u - i, u))
        for r0 in range(0, n_rows, rb):
            rows = slice(r0, r0 + rb)
            s = _dot_nt(qcat_s[t, rows, :], k)
            m_old = m_s[t, rows, :]
            m_new = jnp.maximum(m_old, jnp.max(s, axis=-1, keepdims=True))
            alpha = jnp.exp2(m_old - m_new)
            p = jnp.exp2(s - lanes_x(m_new, tq // LANES))
            acc_s[t, rows, :] = lanes_x(alpha, 2) * acc_s[t, rows, :] + _dot(p.astype(BF16), v1)
            m_s[t, rows, :] = m_new

    @pl.when(step > 0)
    def _():
        for cp in out_copies():
            cp.wait()

    for t in range(2):
        o_all = jnp.concatenate(
            [acc_s[t, h * tq:(h + 1) * tq, :KV_RANK] / acc_s[t, h * tq:(h + 1) * tq, KV_RANK:]
             for h in range(MLA_HEADS)], axis=1).astype(BF16)
        obuf[t] = jnp.concatenate(
            [_dot(o_all[:, pr * 2 * LANES:(pr + 1) * 2 * LANES], wuv_ref[pr]) for pr in range(MLA_HEADS // 2)],
            axis=1).astype(BF16)
    for cp in out_copies():
        cp.start()

    @pl.when(step == n_steps - 1)
    def _():
        for cp in out_copies():
            cp.wait()


def _mla_attn(q_a, kcat, qn_g, wq_nope, wq_pe, w_uk, w_uv, cos_t, sin_a, sin_b, batch, seq):
    tq = ATTN_TQ
    nq = seq // tq
    n_attn = MLA_HEADS * V_DIM
    lo = lambda b, i: i
    hi = lambda b, i: nq - 1 - i
    tab = lambda sel: pl.BlockSpec((tq, LANES), lambda b, i: (sel(b, i), 0))
    qa_spec = lambda sel: pl.BlockSpec((tq, Q_RANK), lambda b, i: (b * nq + sel(b, i), 0))
    return pl.pallas_call(
        functools.partial(_mla_attn_kernel, nq=nq),
        grid=(batch, nq // 2),
        in_specs=[qa_spec(lo), qa_spec(hi),
                  pl.BlockSpec((None, seq, 2 * LANES), lambda b, i: (b, 0, 0)),
                  _const_spec((1, Q_RANK)), _const_spec(wq_nope.shape), _const_spec(wq_pe.shape),
                  _const_spec(w_uk.shape), _const_spec(w_uv.shape),
                  tab(lo), tab(lo), tab(lo), tab(hi), tab(hi), tab(hi)],
        out_specs=pl.BlockSpec(memory_space=pl.ANY),
        out_shape=jax.ShapeDtypeStruct((batch * seq, n_attn), BF16),
        scratch_shapes=[pltpu.VMEM((2, MLA_HEADS * tq, 2 * LANES), BF16),
                        pltpu.VMEM((2, MLA_HEADS * tq, LANES), F32),
                        pltpu.VMEM((2, MLA_HEADS * tq, 2 * KV_RANK), F32),
                        pltpu.VMEM((2, tq, n_attn), BF16),
                        pltpu.SemaphoreType.DMA((2,))],
        compiler_params=_cparams(("arbitrary", "arbitrary")),
        name="mla_attn",
    )(q_a, q_a, kcat.reshape(batch, seq, 2 * LANES), qn_g, wq_nope, wq_pe, w_uk, w_uv,
      cos_t, sin_a, sin_b, cos_t, sin_a, sin_b)


def _mix_ffn_kernel(*refs, n_act, has_pre):
    x_ref = refs[0]
    act_refs = refs[1:1 + n_act]
    k = 1 + n_act
    pre_ref = refs[k] if has_pre else None
    k += int(has_pre)
    wo_refs = refs[k:k + n_act]
    k += n_act
    g1_ref, b1_ref, g2_ref, b2_ref, w1_ref, w2_ref, out_ref = refs[k:k + 7]

    tm = x_ref.shape[0]
    rb = min(FFN_ROW_BLOCK, tm)
    blocks = [slice(r0, r0 + rb) for r0 in range(0, tm, rb)]
    fs = []
    for rows in blocks:
        f = None
        for idx in range(n_act):
            a = act_refs[idx][rows, :]
            if has_pre and idx == 0:
                a = _dot(a.astype(BF16), pre_ref[...])
            part = _dot(a.astype(BF16), wo_refs[idx][...])
            f = part if f is None else f + part
        fs.append(f)
    x1s = [_layer_norm(DN_ALPHA * x_ref[rows, :] + f, g1_ref[...], b1_ref[...]) for rows, f in zip(blocks, fs)]
    x1bs = [x1.astype(BF16) for x1 in x1s]
    accs = [None] * len(blocks)
    for c in range(FFN_HIDDEN // FFN_HC):
        cols = slice(c * FFN_HC, (c + 1) * FFN_HC)
        for i, x1b in enumerate(x1bs):
            h = jnp.maximum(_dot(x1b, w1_ref[:, cols]), 0.0)
            part = _dot((h * h).astype(BF16), w2_ref[cols, :])
            accs[i] = part if accs[i] is None else accs[i] + part
    for rows, x1, acc in zip(blocks, x1s, accs):
        out_ref[rows, :] = _layer_norm(DN_ALPHA * x1 + acc, g2_ref[...], b2_ref[...])


def _mix_ffn(x2d, acts, pre_w, wos, ln1_g, ln1_b, ln2_g, ln2_b, w1, w2, name):
    n_tok = x2d.shape[0]
    tm = min(FFN_TM, n_tok)
    row = lambda w: pl.BlockSpec((tm, w), lambda i: (i, 0))
    has_pre = pre_w is not None
    in_specs = [row(D_MODEL)] + [row(a.shape[1]) for a in acts]
    args = [x2d] + list(acts)
    if has_pre:
        in_specs.append(_const_spec(pre_w.shape))
        args.append(pre_w)
    in_specs += [_const_spec(w.shape) for w in wos]
    args += list(wos)
    in_specs += [_const_spec((1, D_MODEL))] * 4 + [_const_spec(w1.shape), _const_spec(w2.shape)]
    args += [ln1_g, ln1_b, ln2_g, ln2_b, w1, w2]
    return pl.pallas_call(
        functools.partial(_mix_ffn_kernel, n_act=len(acts), has_pre=has_pre),
        grid=(n_tok // tm,),
        in_specs=in_specs,
        out_specs=row(D_MODEL),
        out_shape=jax.ShapeDtypeStruct((n_tok, D_MODEL), F32),
        compiler_params=_cparams(("arbitrary",)),
        name=name,
    )(*args)


def _mix_ffn_sample_kernel(*refs, n_act, has_pre):
    x_ref = refs[0]
    act_refs = refs[1:1 + n_act]
    k = 1 + n_act
    pre_ref = refs[k] if has_pre else None
    k += int(has_pre)
    wo_refs = refs[k:k + n_act]
    k += n_act
    g1_ref, b1_ref, g2_ref, b2_ref, w1_ref, w2_ref, out_ref, w1b_ref, w2b_ref, x1_s, acc_s = refs[k:k + 11]
    c = pl.program_id(0)

    @pl.when(c == 0)
    def _():
        f = None
        for idx in range(n_act):
            a = act_refs[idx][...]
            if has_pre and idx == 0:
                a = _dot(a.astype(BF16), pre_ref[...])
            part = _dot(a.astype(BF16), wo_refs[idx][...])
            f = part if f is None else f + part
        x1_s[...] = _layer_norm(DN_ALPHA * x_ref[...] + f, g1_ref[...], b1_ref[...])
        acc_s[...] = jnp.zeros(acc_s.shape, F32)

    w1c = w1_ref[...].astype(BF16)
    w2c = w2_ref[...].astype(BF16)
    w1b_ref[...] = w1c
    w2b_ref[...] = w2c
    h = jnp.maximum(_dot(x1_s[...].astype(BF16), w1c), 0.0)
    acc_s[...] += _dot((h * h).astype(BF16), w2c)

    @pl.when(c == pl.num_programs(0) - 1)
    def _():
        out_ref[...] = _layer_norm(DN_ALPHA * x1_s[...] + acc_s[...], g2_ref[...], b2_ref[...])


def _mix_ffn_sample(x2d, acts, pre_w, wos, ln1_g, ln1_b, ln2_g, ln2_b, w1_all, w2_all, layer, name):
    n_tok = x2d.shape[0]
    has_pre = pre_w is not None
    full = lambda a: pl.BlockSpec(a.shape, lambda c: (0,) * a.ndim)
    args = [x2d] + list(acts) + ([pre_w] if has_pre else []) + list(wos) + [ln1_g, ln1_b, ln2_g, ln2_b]
    in_specs = [full(a) for a in args]
    in_specs += [pl.BlockSpec((None, D_MODEL, FFN_HC), lambda c: (layer, 0, c)),
                 pl.BlockSpec((None, FFN_HC, D_MODEL), lambda c: (layer, c, 0))]
    args += [w1_all, w2_all]
    return pl.pallas_call(
        functools.partial(_mix_ffn_sample_kernel, n_act=len(acts), has_pre=has_pre),
        grid=(FFN_HIDDEN // FFN_HC,),
        in_specs=in_specs,
        out_specs=[pl.BlockSpec((n_tok, D_MODEL), lambda c: (0, 0)),
                   pl.BlockSpec((D_MODEL, FFN_HC), lambda c: (0, c)),
                   pl.BlockSpec((FFN_HC, D_MODEL), lambda c: (c, 0))],
        out_shape=[jax.ShapeDtypeStruct((n_tok, D_MODEL), F32),
                   jax.ShapeDtypeStruct((D_MODEL, FFN_HIDDEN), BF16),
                   jax.ShapeDtypeStruct((FFN_HIDDEN, D_MODEL), BF16)],
        scratch_shapes=[pltpu.VMEM((n_tok, D_MODEL), F32), pltpu.VMEM((n_tok, D_MODEL), F32)],
        compiler_params=_cparams(("arbitrary",)),
        name=name,
    )(*args)


def _even_inproj_sample_kernel(x_ref, w_ref, kvn_ref, gmg_ref, gmb_ref, ws0_ref, bs0_ref,
                               cos_ref, sina_ref, sinb_ref, qn_ref, wqn_ref, wqp_ref, wuk_ref,
                               ckv_ref, kpe_ref, gate_ref, vn_ref, qlat_ref, qpe_ref):
    z = _dot(x_ref[...].astype(BF16), w_ref[...])
    o_c, o_u, o_v, o_k = Q_RANK, Q_RANK + KV_RANK, Q_RANK + KV_RANK + GM_WIDTH, Q_RANK + KV_RANK + 2 * GM_WIDTH
    cos_t, sin_a, sin_b = cos_ref[...], sina_ref[...], sinb_ref[...]
    c_n = _rms_norm(z[:, o_c:o_u], kvn_ref[...])
    ckv_ref[...] = c_n
    kp = _rope128(z[:, o_k:o_k + LANES], cos_t, sin_a, sin_b)
    kpe_ref[...] = kp[:, :ROPE_DIM]
    gu = jax.nn.gelu(z[:, o_u:o_v])
    v_n = _layer_norm(jax.nn.gelu(z[:, o_v:o_k]), gmg_ref[...], gmb_ref[...])
    vn_ref[...] = v_n
    mixed = ws0_ref[...].astype(F32) * v_n.astype(BF16).astype(F32) + bs0_ref[...]
    gate_ref[...] = (gu * mixed).astype(BF16)
    lat, pes = _queries(z[:, :o_c], qn_ref[...], wqn_ref[...], wqp_ref[...], wuk_ref,
                        cos_t, sin_a, sin_b)
    qlat_ref[...] = lat
    qpe_ref[...] = jnp.concatenate(pes, axis=1)


def _even_inproj_sample(xs, w_in_p, kv_norm, gm_g, gm_b, ws0, bs0, cos_t, sin_a, sin_b,
                        qn_g, wq_nope, wq_pe, w_uk):
    n = xs.shape[0]
    args = (xs, w_in_p, kv_norm, gm_g, gm_b, ws0, bs0, cos_t, sin_a, sin_b, qn_g, wq_nope, wq_pe, w_uk)
    full = lambda shape: pl.BlockSpec(shape, lambda i: (0,) * len(shape))
    widths = [(KV_RANK, F32), (ROPE_DIM, F32), (GM_WIDTH, BF16), (GM_WIDTH, F32),
              (MLA_HEADS * LANES, F32), (MLA_HEADS * LANES, F32)]
    return pl.pallas_call(
        _even_inproj_sample_kernel,
        grid=(1,),
        in_specs=[full(a.shape) for a in args],
        out_specs=[full((n, w)) for w, _ in widths],
        out_shape=[jax.ShapeDtypeStruct((n, w), dt) for w, dt in widths],
        compiler_params=_cparams(("arbitrary",)),
        name="even_inproj_sample",
    )(*args)


def _decode_attn_kernel(pt_ref, qlat_ref, qpe_ref, ckvs_ref, kpes_ref, ckv_hbm, kpe_hbm, out_ref,
                        ckv_buf, kpe_buf, m_buf, sem, *, n_pg, n_groups):
    b = pl.program_id(0)
    n_b = pl.num_programs(0)
    assert n_groups % DEC_SLOTS == 0 and DEC_AHEAD + 2 <= DEC_SLOTS

    def group_copies(seq, grp):
        slot = grp % DEC_SLOTS
        cps = []
        for k in range(n_pg):
            page = pt_ref[seq, grp * n_pg + k]
            keys = pl.ds(k * PAGE_SIZE, PAGE_SIZE)
            cps.append(pltpu.make_async_copy(ckv_hbm.at[page], ckv_buf.at[slot, keys, :], sem.at[0, slot]))
            cps.append(pltpu.make_async_copy(kpe_hbm.at[page], kpe_buf.at[slot, :, keys], sem.at[1, slot]))
        return cps

    @pl.when(b == 0)
    def _():
        for g in range(DEC_AHEAD):
            for cp in group_copies(0, g):
                cp.start()

    heads, span = MLA_HEADS, DEC_SPAN_PAGES * PAGE_SIZE
    half_span = span // 2
    n_span = n_pg // DEC_SPAN_PAGES
    qlat = qlat_ref[...]
    zero = jnp.zeros_like(qlat)
    q2 = jnp.concatenate([jnp.concatenate([qlat, zero], axis=1),
                          jnp.concatenate([zero, qlat], axis=1)], axis=0).astype(BF16)
    qpe = qpe_ref[...][:, :ROPE_DIM].astype(BF16)

    def scores(slot):
        s_pe = _dot(qpe, kpe_buf[slot].astype(BF16))
        pieces = []
        for j in range(n_span):
            m_buf[slot, j] = jnp.concatenate(
                [ckv_buf[slot, pl.ds(j * span, half_span), :],
                 ckv_buf[slot, pl.ds(j * span + half_span, half_span), :]], axis=1).astype(BF16)
            s2 = _dot_nt(q2, m_buf[slot, j])
            pieces += [s2[:heads] + s_pe[:, j * span:j * span + half_span],
                       s2[heads:] + s_pe[:, j * span + half_span:(j + 1) * span]]
        return jnp.concatenate(pieces, axis=1) * ATTN_SCALE

    def values(p, grp):
        slot = grp % DEC_SLOTS
        o2 = None
        for j in range(n_span):
            lhs = jnp.concatenate([p[:, j * span:j * span + half_span],
                                   p[:, j * span + half_span:(j + 1) * span]], axis=0).astype(BF16)
            part = _dot(lhs, m_buf[slot, j])
            o2 = part if o2 is None else o2 + part
        return o2[:heads, :KV_RANK] + o2[heads:, KV_RANK:]

    state = dict(m_run=None, m=None, l=None, acc=None)
    raw = {}
    probs = {}

    def softmax(k):
        s = raw.pop(k)
        m_cur = jnp.max(s, axis=-1, keepdims=True)
        m_new = m_cur if state["m_run"] is None else jnp.maximum(state["m_run"], m_cur)
        state["m_run"] = m_new
        probs[k] = (jnp.exp(s - m_new), m_new)

    def fold(k):
        p, m_k = probs.pop(k)
        pv = values(p, k)
        p_sum = jnp.sum(p, axis=-1, keepdims=True)
        if state["m"] is None:
            state["l"], state["acc"] = p_sum, pv
        else:
            alpha = jnp.exp(state["m"] - m_k)
            state["l"], state["acc"] = alpha * state["l"] + p_sum, alpha * state["acc"] + pv
        state["m"] = m_k

    for g in range(n_groups):
        if g + DEC_AHEAD < n_groups:
            for cp in group_copies(b, g + DEC_AHEAD):
                cp.start()
        else:
            @pl.when(b + 1 < n_b)
            def _():
                for cp in group_copies(b + 1, g + DEC_AHEAD - n_groups):
                    cp.start()
        for cp in group_copies(b, g):
            cp.wait()
        raw[g] = scores(g % DEC_SLOTS)
        if g >= 1:
            softmax(g - 1)
        if g >= 2:
            fold(g - 2)
    softmax(n_groups - 1)
    fold(n_groups - 2)
    fold(n_groups - 1)
    m, l, acc = state["m"], state["l"], state["acc"]

    r = lambda a: a.astype(BF16).astype(F32)
    kv = r(ckvs_ref[...])
    s_self = (jnp.sum(r(qlat_ref[...]) * kv, axis=-1, keepdims=True)
              + jnp.sum(r(qpe_ref[...]) * r(kpes_ref[...]), axis=-1, keepdims=True)) * ATTN_SCALE
    m_n = jnp.maximum(m, s_self)
    a = jnp.exp(m - m_n)
    p_self = jnp.exp(s_self - m_n)
    out_ref[...] = (a * acc + r(p_self) * kv) / (a * l + p_self)


def _decode_attn(page_table, qlat2, qpe2, ckv_s3, kpe_s3, cache_ckv_e, cache_kpe_t):
    dec_batch, n_pages = page_table.shape
    n_pg = DEC_PG
    qspec = pl.BlockSpec((MLA_HEADS, LANES), lambda b, pt: (b, 0))
    self_spec = pl.BlockSpec((None, 1, LANES), lambda b, pt: (b, 0, 0))
    hbm = pl.BlockSpec(memory_space=pl.ANY)
    grid_spec = pltpu.PrefetchScalarGridSpec(
        num_scalar_prefetch=1,
        grid=(dec_batch,),
        in_specs=[qspec, qspec, self_spec, self_spec, hbm, hbm],
        out_specs=qspec,
        scratch_shapes=[pltpu.VMEM((DEC_SLOTS, n_pg * PAGE_SIZE, KV_RANK), F32),
                        pltpu.VMEM((DEC_SLOTS, ROPE_DIM, n_pg * PAGE_SIZE), F32),
                        pltpu.VMEM((DEC_SLOTS, n_pg // DEC_SPAN_PAGES, DEC_SPAN_PAGES // 2 * PAGE_SIZE, 2 * KV_RANK),
                                   BF16),
                        pltpu.SemaphoreType.DMA((2, DEC_SLOTS))],
    )
    return pl.pallas_call(
        functools.partial(_decode_attn_kernel, n_pg=n_pg, n_groups=n_pages // n_pg),
        grid_spec=grid_spec,
        out_shape=jax.ShapeDtypeStruct((dec_batch * MLA_HEADS, KV_RANK), F32),
        compiler_params=_cparams(("arbitrary",)),
        name="decode_attn",
    )(page_table, qlat2, qpe2, ckv_s3, kpe_s3, cache_ckv_e, cache_kpe_t)


def _s5_readout(hcat_blocks, u, cw_ref, d_ref, wglu_ref):
    y = jnp.concatenate([_dot(hcat_blocks[k], cw_ref[k]) for k in range(S5_NB)], axis=1)
    y = jax.nn.gelu(y + d_ref[...] * u)
    return y * jax.nn.sigmoid(_dot(y.astype(BF16), wglu_ref[...]))


def _s5_prompt_kernel(x_ref, win_ref, bw2_ref, a2_ref, cw2_ref, d_ref, wglu_ref,
                      yg_ref, hre_ref, him_ref, u_s, y_s, st_s, h_s):
    n_b, seg, _ = x_ref.shape
    half = S5_SLABS // 2
    sub = S5_SUB
    n_sub = seg // sub
    rows_sub = sub * n_b
    n_pair = sub // 2
    rows_pair = n_pair * n_b
    reg = rows_pair + n_b
    q_slabs = S5_WIDTH // LANES

    @pl.when(pl.program_id(0) == 0)
    def _():
        h_s[...] = jnp.zeros(h_s.shape, F32)

    def steps_of(slab_ref, s, q, parity):
        return jnp.concatenate([slab_ref[s, q, pl.ds((2 * j + parity) * n_b, n_b), :] for j in range(n_pair)],
                               axis=0)

    def project(s):
        xs = jnp.concatenate([x_ref[b, s * sub:(s + 1) * sub, :] for b in range(n_b)], axis=0)
        u = _dot(xs.astype(BF16), win_ref[...])
        for b in range(n_b):
            for q in range(q_slabs):
                u_s[s, q, pl.ds(b, sub, stride=n_b), :] = u[b * sub:(b + 1) * sub, q * LANES:(q + 1) * LANES]
        rows = slice(s * reg + n_b, (s + 1) * reg)
        for k in range(S5_NB):
            lhs = jnp.concatenate([steps_of(u_s, s, k, 0), steps_of(u_s, s, k, 1)], axis=1).astype(BF16)
            r = _dot(lhs, bw2_ref[k])
            for q in range(S5_BLK_SLABS):
                st_s[S5_BLK_SLABS * k + q, rows, :] = r[:, q * LANES:(q + 1) * LANES]
                st_s[half + S5_BLK_SLABS * k + q, rows, :] = r[:, (S5_BLK_SLABS + q) * LANES:
                                                                (S5_BLK_SLABS + q + 1) * LANES]

    def recur(s, h):
        st_s[:, s * reg:s * reg + n_b, :] = h
        for j in range(n_pair):
            rows = slice(s * reg + (j + 1) * n_b, s * reg + (j + 2) * n_b)
            inc = st_s[:, rows, :]
            ar, ai = a2_ref[:half], a2_ref[half:]
            hr, hi = h[:half], h[half:]
            h = jnp.concatenate([ar * hr - ai * hi + inc[:half], ar * hi + ai * hr + inc[half:]], axis=0)
            st_s[:, rows, :] = h
        return h

    def read_out(s):
        rows = slice(s * reg, s * reg + rows_pair)
        ys = []
        for k in range(S5_NB):
            lhs = jnp.concatenate(
                [st_s[S5_BLK_SLABS * k + q, rows, :] for q in range(S5_BLK_SLABS)]
                + [st_s[half + S5_BLK_SLABS * k + q, rows, :] for q in range(S5_BLK_SLABS)]
                + [steps_of(u_s, s, k, 0), steps_of(u_s, s, k, 1)], axis=1).astype(BF16)
            y2 = _dot(lhs, cw2_ref[k])
            ys.append(jnp.concatenate(
                [y2[j * n_b:(j + 1) * n_b, par * LANES:(par + 1) * LANES] for j in range(n_pair) for par in range(2)],
                axis=0))
        u_t = jnp.concatenate([u_s[s, q] for q in range(q_slabs)], axis=1)
        y = jax.nn.gelu(jnp.concatenate(ys, axis=1) + d_ref[...] * u_t)
        y = y * jax.nn.sigmoid(_dot(y.astype(BF16), wglu_ref[...]))
        for q in range(q_slabs):
            y_s[s, q] = y[:, q * LANES:(q + 1) * LANES]
        for b in range(n_b):
            yg_ref[b, s * sub:(s + 1) * sub, :] = jnp.concatenate(
                [y_s[s, q, pl.ds(b, sub, stride=n_b), :] for q in range(q_slabs)], axis=1).astype(BF16)

    h = h_s[...]
    project(0)
    for s in range(n_sub):
        if s + 1 < n_sub:
            project(s + 1)
        h = recur(s, h)
        if s >= 1:
            read_out(s - 1)
    read_out(n_sub - 1)
    h_s[...] = h
    hre_ref[...] = jnp.concatenate([h[s] for s in range(half)], axis=1)
    him_ref[...] = jnp.concatenate([h[half + s] for s in range(half)], axis=1)


def _s5_prompt(x3, w_in, bw2, a2_b, cw2, d_row, w_glu):
    batch, seq, _ = x3.shape
    seg = S5_L
    n_sub = seg // S5_SUB
    slab_rows = (n_sub, S5_WIDTH // LANES, S5_SUB * batch, LANES)
    return pl.pallas_call(
        _s5_prompt_kernel,
        grid=(seq // seg,),
        in_specs=[pl.BlockSpec((batch, seg, D_MODEL), lambda c: (0, c, 0)),
                  _const_spec(w_in.shape), _const_spec(bw2.shape), _const_spec(a2_b.shape),
                  _const_spec(cw2.shape), _const_spec(d_row.shape), _const_spec(w_glu.shape)],
        out_specs=[pl.BlockSpec((batch, seg, S5_WIDTH), lambda c: (0, c, 0)),
                   pl.BlockSpec((batch, S5_STATES), lambda c: (0, 0)),
                   pl.BlockSpec((batch, S5_STATES), lambda c: (0, 0))],
        out_shape=[jax.ShapeDtypeStruct((batch, seq, S5_WIDTH), BF16),
                   jax.ShapeDtypeStruct((batch, S5_STATES), F32),
                   jax.ShapeDtypeStruct((batch, S5_STATES), F32)],
        scratch_shapes=[pltpu.VMEM(slab_rows, F32), pltpu.VMEM(slab_rows, F32),
                        pltpu.VMEM((S5_SLABS, n_sub * (S5_SUB // 2 + 1) * batch, LANES), F32),
                        pltpu.VMEM((S5_SLABS, batch, LANES), F32)],
        compiler_params=_cparams(("arbitrary",)),
        name="s5_prompt",
    )(x3, w_in, bw2, a2_b, cw2, d_row, w_glu)


def _s5_sample_kernel(x_ref, h0r_ref, h0i_ref, win_ref, bw_ref, ar_ref, ai_ref, cw_ref, d_ref, wglu_ref,
                      yg_ref, hre_ref, him_ref):
    u = _dot(x_ref[...].astype(BF16), win_ref[...])
    ub = u.astype(BF16)
    bu = [_dot(ub[:, k * LANES:(k + 1) * LANES], bw_ref[k]) for k in range(S5_NB)]
    w = S5_KB * S5_STATE
    bu_r = jnp.concatenate([r[:, :w] for r in bu], axis=1)
    bu_i = jnp.concatenate([r[:, w:] for r in bu], axis=1)
    ar, ai, h0r, h0i = ar_ref[...], ai_ref[...], h0r_ref[...], h0i_ref[...]
    hr = bu_r + (ar * h0r - ai * h0i)
    hi = bu_i + (ar * h0i + ai * h0r)
    hre_ref[...] = hr
    him_ref[...] = hi
    blocks = [jnp.concatenate([hr[:, k * w:(k + 1) * w], hi[:, k * w:(k + 1) * w]], axis=1).astype(BF16)
              for k in range(S5_NB)]
    yg_ref[...] = _s5_readout(blocks, u, cw_ref, d_ref, wglu_ref).astype(BF16)


def _s5_sample(xs, h0r, h0i, w_in, bw, a_r, a_i, cw, d_row, w_glu):
    n = xs.shape[0]
    args = (xs, h0r, h0i, w_in, bw, a_r, a_i, cw, d_row, w_glu)
    full = lambda shape: pl.BlockSpec(shape, lambda i: (0,) * len(shape))
    return pl.pallas_call(
        _s5_sample_kernel,
        grid=(1,),
        in_specs=[full(a.shape) for a in args],
        out_specs=[full((n, S5_WIDTH)), full((n, S5_STATES)), full((n, S5_STATES))],
        out_shape=[jax.ShapeDtypeStruct((n, S5_WIDTH), BF16),
                   jax.ShapeDtypeStruct((n, S5_STATES), F32),
                   jax.ShapeDtypeStruct((n, S5_STATES), F32)],
        compiler_params=_cparams(("arbitrary",)),
        name="s5_sample",
    )(*args)


def _rope_tables(pos):
    freqs = ROPE_THETA ** (-jnp.arange(ROPE_HALF, dtype=F32) / ROPE_HALF)
    ang = pos.astype(F32)[:, None] * freqs
    cos, sin = jnp.cos(ang), jnp.sin(ang)
    zero = jnp.zeros_like(cos)
    pad = jnp.zeros((pos.shape[0], LANES - ROPE_DIM), F32)
    cos_t = jnp.concatenate([cos, cos, pad], axis=1)
    sin_a = jnp.concatenate([zero, sin, pad], axis=1)
    sin_b = jnp.concatenate([-sin, zero, pad], axis=1)
    return cos_t, sin_a, sin_b


def _block_diag(blocks):
    *lead, n, r, c = blocks.shape
    eye = jnp.eye(n, dtype=blocks.dtype)
    return jnp.einsum('...nrc,nm->...nrmc', blocks, eye).reshape(*lead, n * r, n * c)


def _even_params(w_in, q_norm, w_q_b, kv_norm, w_kv_b, gm_g, gm_b, gm_w_s, gm_b_s, w_out):
    i1, i2, i3, i4 = Q_RANK, Q_RANK + KV_RANK, Q_RANK + KV_RANK + ROPE_DIM, Q_RANK + KV_RANK + ROPE_DIM + GM_WIDTH
    w_in_p = jnp.concatenate([w_in[:, :i2], w_in[:, i3:], w_in[:, i2:i3],
                              jnp.zeros((D_MODEL, LANES - ROPE_DIM), F32)], axis=1).astype(BF16)
    wq_nope = w_q_b[:, :, :NOPE_DIM].reshape(Q_RANK, MLA_HEADS * NOPE_DIM).astype(BF16)
    pe = w_q_b[:, :, NOPE_DIM:]
    pe_sw = jnp.concatenate([pe[:, :, ROPE_HALF:], pe[:, :, :ROPE_HALF]], axis=2)
    lane_pad = jnp.zeros((Q_RANK, MLA_HEADS, LANES - ROPE_DIM), F32)
    wq_pe = jnp.concatenate(
        [jnp.concatenate([w, lane_pad], axis=2).reshape(Q_RANK, MLA_HEADS * LANES) for w in (pe, pe_sw)],
        axis=1).astype(BF16)
    uk = jnp.transpose(w_kv_b[:, :, :NOPE_DIM], (1, 2, 0))
    uv = jnp.transpose(w_kv_b[:, :, NOPE_DIM:], (1, 0, 2))
    pairs = MLA_HEADS // 2
    w_uk = _block_diag(uk.reshape(pairs, 2, NOPE_DIM, KV_RANK)).astype(BF16)
    w_uv_p = _block_diag(uv.reshape(pairs, 2, KV_RANK, V_DIM)).astype(BF16)
    w_uv = _block_diag(uv).astype(BF16)
    causal = jnp.tril(jnp.ones((GM_CHUNK, GM_CHUNK), dtype=bool))
    ws_m = jnp.where(causal[None], gm_w_s, 0).astype(BF16)
    bs_full = jnp.repeat(gm_b_s.T, GM_HEAD_DIM, axis=1)
    ws0 = jnp.repeat(gm_w_s[:, 0, 0], GM_HEAD_DIM)[None, :].astype(BF16)
    bs0 = bs_full[:1]
    n_attn = MLA_HEADS * V_DIM
    return dict(w_in_p=w_in_p, qn=q_norm[None, :], wq_nope=wq_nope, wq_pe=wq_pe, w_uk=w_uk, w_uv=w_uv, w_uv_p=w_uv_p,
                kvn=kv_norm[None, :], gm_g=gm_g[None, :], gm_b=gm_b[None, :], ws_m=ws_m, bs_full=bs_full,
                ws0=ws0, bs0=bs0, wo_attn=w_out[:n_attn].astype(BF16), wo_gate=w_out[n_attn:].astype(BF16))


def _s5_params(w_in, a_re, a_im, b_re, b_im, c_re, c_im, d, log_dt, w_glu, w_out, batch):
    dt = jnp.exp(log_dt)[:, None]
    ld_r, ld_i = a_re * dt, a_im * dt
    mag = jnp.exp(ld_r)
    ab_r, ab_i = mag * jnp.cos(ld_i), mag * jnp.sin(ld_i)
    den = a_re * a_re + a_im * a_im
    cr = ((ab_r - 1.0) * a_re + ab_i * a_im) / den
    ci = (ab_i * a_re - (ab_r - 1.0) * a_im) / den
    bb_r = cr[..., None] * b_re - ci[..., None] * b_im
    bb_i = cr[..., None] * b_im + ci[..., None] * b_re

    def in_blocks(bb):
        return _block_diag(jnp.swapaxes(bb, 1, 2).reshape(S5_NB, S5_KB, S5_GROUP_DIM, S5_STATE))

    def out_blocks(cc):
        return _block_diag(jnp.swapaxes(cc, 1, 2).reshape(S5_NB, S5_KB, S5_STATE, S5_GROUP_DIM))

    bw = jnp.concatenate([in_blocks(bb_r), in_blocks(bb_i)], axis=2).astype(BF16)
    cw = jnp.concatenate([out_blocks(c_re), out_blocks(-c_im)], axis=1).astype(BF16)
    a_r, a_i = ab_r.reshape(1, S5_STATES), ab_i.reshape(1, S5_STATES)
    half = S5_SLABS // 2
    cmul = lambda xr, xi, yr, yi: (xr * yr - xi * yi, xr * yi + xi * yr)
    a2_r, a2_i = cmul(ab_r, ab_i, ab_r, ab_i)
    abb_r, abb_i = cmul(ab_r[..., None], ab_i[..., None], bb_r, bb_i)
    ca_r, ca_i = cmul(c_re, c_im, ab_r[:, None, :], ab_i[:, None, :])
    ca2_r, ca2_i = cmul(c_re, c_im, a2_r[:, None, :], a2_i[:, None, :])
    cb = jnp.einsum('gcp,gpd->gcd', c_re, bb_r) - jnp.einsum('gcp,gpd->gcd', c_im, bb_i)
    cab = jnp.einsum('gcp,gpd->gcd', ca_r, bb_r) - jnp.einsum('gcp,gpd->gcd', ca_i, bb_i)
    io_blocks = lambda m: _block_diag(jnp.swapaxes(m, 1, 2).reshape(S5_NB, S5_KB, S5_GROUP_DIM, S5_GROUP_DIM))
    bw2 = jnp.concatenate([jnp.concatenate([in_blocks(abb_r), in_blocks(abb_i)], axis=2),
                           jnp.concatenate([in_blocks(bb_r), in_blocks(bb_i)], axis=2)],
                          axis=1).astype(BF16)
    cw2 = jnp.concatenate(
        [jnp.concatenate([out_blocks(ca_r), out_blocks(ca2_r)], axis=2),
         jnp.concatenate([out_blocks(-ca_i), out_blocks(-ca2_i)], axis=2),
         jnp.concatenate([io_blocks(cb), io_blocks(cab)], axis=2),
         jnp.concatenate([jnp.zeros((S5_NB, LANES, LANES), F32), io_blocks(cb)], axis=2)],
        axis=1).astype(BF16)
    a2_b = jnp.concatenate([jnp.broadcast_to(a2_r.reshape(half, 1, LANES), (half, batch, LANES)),
                            jnp.broadcast_to(a2_i.reshape(half, 1, LANES), (half, batch, LANES))], axis=0)
    return dict(w_in=w_in.astype(BF16), bw=bw, cw=cw, a_r=a_r, a_i=a_i, bw2=bw2, cw2=cw2, a2_b=a2_b,
                d=d.reshape(1, S5_WIDTH), w_glu=w_glu.astype(BF16), w_out=w_out.astype(BF16))


def kernel(x_prompt, x_sample, cache_ckv, cache_kpe, state_s5_re, state_s5_im, page_table, w_in_even, mla_q_norm, mla_w_q_b, mla_kv_norm, mla_w_kv_b, gm_norm_g, gm_norm_b, gm_w_s, gm_b_s, w_out_even, w_in_odd, s5_a_re, s5_a_im, s5_b_re, s5_b_im, s5_c_re, s5_c_im, s5_d, s5_log_dt, s5_w_glu, w_out_odd, ln_mix_g, ln_mix_b, ln_ffn_g, ln_ffn_b, ffn_w1, ffn_w2):
    batch, seq, _ = x_prompt.shape
    dec_batch, dec_seq, _ = x_sample.shape
    assert dec_seq == 1 and seq % INPROJ_TM == 0 and seq % ATTN_TQ == 0 and seq % S5_L == 0
    assert (seq // ATTN_TQ) % 2 == 0
    assert (batch * seq) % FFN_TM == 0 and page_table.shape[1] % DEC_PG == 0
    assert batch == SUBLANES

    xp = x_prompt.reshape(batch * seq, D_MODEL)
    xs = x_sample.reshape(dec_batch, D_MODEL)
    tabs_p = _rope_tables(jnp.arange(seq, dtype=jnp.int32))
    tabs_s = _rope_tables(PAST_LEN + jnp.arange(dec_seq, dtype=jnp.int32))

    outs = {k: [] for k in ("ckv_p", "kpe_p", "ckv_s", "kpe_s", "gmv_p", "gmv_s",
                            "s5re_p", "s5im_p", "s5re_s", "s5im_s")}
    for layer in range(DEPTH):
        ln = (ln_mix_g[layer][None, :], ln_mix_b[layer][None, :],
              ln_ffn_g[layer][None, :], ln_ffn_b[layer][None, :])
        if layer % 2 == 0:
            e = layer // 2
            p = _even_params(w_in_even[e], mla_q_norm[e], mla_w_q_b[e], mla_kv_norm[e], mla_w_kv_b[e],
                             gm_norm_g[e], gm_norm_b[e], gm_w_s[e], gm_b_s[e], w_out_even[e])
            q_a, ckv, kpe, kcat, gate, gmv = _even_inproj(
                xp, p["w_in_p"], p["kvn"], p["gm_g"], p["gm_b"], p["ws_m"], p["bs_full"], *tabs_p, batch, seq)
            attn = _mla_attn(q_a, kcat, p["qn"], p["wq_nope"], p["wq_pe"], p["w_uk"], p["w_uv_p"], *tabs_p,
                             batch, seq)
            outs["ckv_p"].append(ckv.reshape(batch, seq, KV_RANK))
            outs["kpe_p"].append(kpe.reshape(batch, seq, ROPE_DIM))
            outs["gmv_p"].append(gmv)
            ckv_s, kpe_s, gate_s, vn_s, qlat_s, qpe_s = _even_inproj_sample(
                xs, p["w_in_p"], p["kvn"], p["gm_g"], p["gm_b"], p["ws0"], p["bs0"], *tabs_s,
                p["qn"], p["wq_nope"], p["wq_pe"], p["w_uk"])
            kpe_pad = jnp.concatenate([kpe_s, jnp.zeros((dec_batch, LANES - ROPE_DIM), F32)], axis=1)
            o_lat = _decode_attn(page_table,
                                 qlat_s.reshape(dec_batch * MLA_HEADS, LANES),
                                 qpe_s.reshape(dec_batch * MLA_HEADS, LANES),
                                 ckv_s.reshape(dec_batch, 1, KV_RANK), kpe_pad.reshape(dec_batch, 1, LANES),
                                 cache_ckv[e], jnp.swapaxes(cache_kpe[e], 1, 2))
            xs, w1_b, w2_b = _mix_ffn_sample(
                xs, [o_lat.reshape(dec_batch, MLA_HEADS * KV_RANK), gate_s], p["w_uv"],
                [p["wo_attn"], p["wo_gate"]], *ln, ffn_w1, ffn_w2, layer, name="even_ffn_sample")
            xp = _mix_ffn(xp, [attn, gate], None, [p["wo_attn"], p["wo_gate"]], *ln, w1_b, w2_b,
                          name="even_ffn_prompt")
            outs["ckv_s"].append(ckv_s.reshape(dec_batch, 1, KV_RANK))
            outs["kpe_s"].append(kpe_s.reshape(dec_batch, 1, ROPE_DIM))
            outs["gmv_s"].append(vn_s.reshape(dec_batch, 1, GM_WIDTH))
        else:
            o = layer // 2
            p = _s5_params(w_in_odd[o], s5_a_re[o], s5_a_im[o], s5_b_re[o], s5_b_im[o], s5_c_re[o],
                           s5_c_im[o], s5_d[o], s5_log_dt[o], s5_w_glu[o], w_out_odd[o], batch)
            yg, hre, him = _s5_prompt(xp.reshape(batch, seq, D_MODEL), p["w_in"], p["bw2"], p["a2_b"],
                                      p["cw2"], p["d"], p["w_glu"])
            outs["s5re_p"].append(hre.reshape(batch, S5_GROUPS, S5_STATE))
            outs["s5im_p"].append(him.reshape(batch, S5_GROUPS, S5_STATE))
            yg_s, hre_s, him_s = _s5_sample(
                xs, state_s5_re[o].reshape(dec_batch, S5_STATES), state_s5_im[o].reshape(dec_batch, S5_STATES),
                p["w_in"], p["bw"], p["a_r"], p["a_i"], p["cw"], p["d"], p["w_glu"])
            xs, w1_b, w2_b = _mix_ffn_sample(xs, [yg_s], None, [p["w_out"]], *ln, ffn_w1, ffn_w2, layer,
                                             name="odd_ffn_sample")
            xp = _mix_ffn(xp, [yg.reshape(batch * seq, S5_WIDTH)], None, [p["w_out"]], *ln, w1_b, w2_b,
                          name="odd_ffn_prompt")
            outs["s5re_s"].append(hre_s.reshape(dec_batch, S5_GROUPS, S5_STATE))
            outs["s5im_s"].append(him_s.reshape(dec_batch, S5_GROUPS, S5_STATE))

    st = jnp.stack
    return (xp.reshape(batch, seq, D_MODEL), xs.reshape(dec_batch, dec_seq, D_MODEL),
            st(outs["ckv_p"]), st(outs["kpe_p"]), st(outs["ckv_s"]), st(outs["kpe_s"]),
            st(outs["gmv_p"]), st(outs["gmv_s"]),
            st(outs["s5re_p"]), st(outs["s5im_p"]), st(outs["s5re_s"]), st(outs["s5im_s"]))
```

```python
import functools
import math

import jax
import jax.numpy as jnp
from jax import lax
from jax.experimental import pallas as pl
from jax.experimental.pallas import tpu as pltpu

F32 = jnp.float32
BF16 = jnp.bfloat16

D_MODEL = 1024
DEPTH = 2
PAST_LEN = 16384
PAGE_SIZE = 128
MLA_HEADS = 8
NOPE_DIM = 64
ROPE_DIM = 32
ROPE_HALF = ROPE_DIM // 2
V_DIM = 64
Q_RANK = 256
KV_RANK = 128
ROPE_THETA = 10000.0
ATTN_SCALE = 1.0 / math.sqrt(NOPE_DIM + ROPE_DIM)
ATTN_SCALE_LOG2 = ATTN_SCALE * math.log2(math.e)
GM_HEADS = 8
GM_HEAD_DIM = 64
GM_WIDTH = GM_HEADS * GM_HEAD_DIM
GM_CHUNK = 128
S5_GROUP_DIM = 16
S5_GROUPS = 32
S5_STATE = 64
S5_WIDTH = S5_GROUPS * S5_GROUP_DIM
S5_STATES = S5_GROUPS * S5_STATE
FFN_HIDDEN = 4 * D_MODEL
DN_ALPHA = (2 * DEPTH) ** 0.25
LN_EPS = 1e-5
RMS_EPS = 1e-6

LANES = 128
SUBLANES = 8
VMEM_LIMIT_BYTES = 56 * 1024 * 1024

INPROJ_TM = 512
INPROJ_ROW_BLOCK = 256
ATTN_TQ = 256
ATTN_ROW_BLOCK = 256
FFN_TM = 1024
FFN_HC = 1024
FFN_ROW_BLOCK = 256
S5_L = 128
S5_SUB = 32
DEC_PG = 32
DEC_SLOTS = 4
DEC_AHEAD = 2
DEC_SPAN_PAGES = 4
S5_KB = 8
S5_NB = S5_GROUPS // S5_KB
S5_SLABS = 2 * S5_STATES // LANES
S5_BLK_SLABS = S5_KB * S5_STATE // LANES


def _cparams(sem):
    return pltpu.CompilerParams(dimension_semantics=sem, vmem_limit_bytes=VMEM_LIMIT_BYTES)


def _const_spec(shape):
    n = len(shape)
    return pl.BlockSpec(shape, lambda *_: (0,) * n, pipeline_mode=pl.Buffered(1))


def _layer_norm(x, g, b):
    mu = jnp.mean(x, axis=-1, keepdims=True)
    xc = x - mu
    var = jnp.mean(xc * xc, axis=-1, keepdims=True)
    return xc * lax.rsqrt(var + LN_EPS) * g + b


def _rms_norm(x, g):
    return x * lax.rsqrt(jnp.mean(x * x, axis=-1, keepdims=True) + RMS_EPS) * g


def _rope128(x, cos_t, sin_a, sin_b):
    return (x * cos_t + pltpu.roll(x, ROPE_HALF, 1) * sin_a
            + pltpu.roll(x, LANES - ROPE_HALF, 1) * sin_b)


def _dot(a, b):
    return jnp.dot(a, b, preferred_element_type=F32)


def _dot_nt(a, b):
    return lax.dot_general(a, b, (((1,), (1,)), ((), ())), preferred_element_type=F32)


def _even_inproj_kernel(x_ref, w_ref, kvn_ref, gmg_ref, gmb_ref, ws_ref, bs_ref,
                        cos_ref, sina_ref, sinb_ref,
                        qa_ref, ckv_ref, kpe_ref, kcat_ref, gate_ref, gmv_ref):
    tm = x_ref.shape[0]
    o_c, o_u, o_v, o_k = Q_RANK, Q_RANK + KV_RANK, Q_RANK + KV_RANK + GM_WIDTH, Q_RANK + KV_RANK + 2 * GM_WIDTH
    rb = min(INPROJ_ROW_BLOCK, tm)
    blocks = [slice(r0, r0 + rb) for r0 in range(0, tm, rb)]
    zs = [_dot(x_ref[rows, :].astype(BF16), w_ref[...]) for rows in blocks]

    gus, vns = [], []
    for rows, z in zip(blocks, zs):
        qa_ref[rows, :] = z[:, :o_c]
        c_n = _rms_norm(z[:, o_c:o_u], kvn_ref[...])
        ckv_ref[rows, :] = c_n
        kp = _rope128(z[:, o_k:o_k + LANES], cos_ref[rows, :], sina_ref[rows, :], sinb_ref[rows, :])
        kpe_ref[rows, :] = kp[:, :ROPE_DIM]
        kcat_ref[rows, :] = jnp.concatenate([c_n, kp], axis=1).astype(BF16)
        gus.append(jax.nn.gelu(z[:, o_u:o_v]))
        vns.append(_layer_norm(jax.nn.gelu(z[:, o_v:o_k]), gmg_ref[...], gmb_ref[...]))

    gmv_ref[0] = vns[-1][rb - GM_CHUNK:, :]

    lane = lax.broadcasted_iota(jnp.int32, (GM_CHUNK, LANES), 1)
    lo = lane < GM_HEAD_DIM
    bs = bs_ref[...]
    for rows, gu, v_n in zip(blocks, gus, vns):
        for ci in range(rb // GM_CHUNK):
            chunk = slice(ci * GM_CHUNK, (ci + 1) * GM_CHUNK)
            pieces = []
            for pr in range(GM_WIDTH // LANES):
                r = v_n[chunk, pr * LANES:(pr + 1) * LANES]
                m0 = _dot(ws_ref[2 * pr], jnp.where(lo, r, 0.0).astype(BF16))
                m1 = _dot(ws_ref[2 * pr + 1], jnp.where(lo, 0.0, r).astype(BF16))
                pieces.append(m0 + m1)
            mixed = jnp.concatenate(pieces, axis=1) + bs
            gate_ref[pl.ds(rows.start + ci * GM_CHUNK, GM_CHUNK), :] = (gu[chunk, :] * mixed).astype(BF16)


def _even_inproj(x2d, w_in_p, kv_norm, gm_g, gm_b, ws_m, bs_full, cos_t, sin_a, sin_b, batch, seq):
    n_tok = x2d.shape[0]
    tm = INPROJ_TM
    tiles_per_seq = seq // tm
    row = lambda w: pl.BlockSpec((tm, w), lambda i: (i, 0))
    tab = pl.BlockSpec((tm, LANES), lambda i: (i % tiles_per_seq, 0))
    n_in = w_in_p.shape[1]
    return pl.pallas_call(
        _even_inproj_kernel,
        grid=(n_tok // tm,),
        in_specs=[row(D_MODEL), _const_spec((D_MODEL, n_in)), _const_spec((1, KV_RANK)),
                  _const_spec((1, GM_WIDTH)), _const_spec((1, GM_WIDTH)),
                  _const_spec((GM_HEADS, GM_CHUNK, GM_CHUNK)), _const_spec((GM_CHUNK, GM_WIDTH)),
                  tab, tab, tab],
        out_specs=[row(Q_RANK), row(KV_RANK), row(ROPE_DIM), row(2 * LANES), row(GM_WIDTH),
                   pl.BlockSpec((1, GM_CHUNK, GM_WIDTH), lambda i: (i // tiles_per_seq, 0, 0))],
        out_shape=[jax.ShapeDtypeStruct((n_tok, Q_RANK), F32),
                   jax.ShapeDtypeStruct((n_tok, KV_RANK), F32),
                   jax.ShapeDtypeStruct((n_tok, ROPE_DIM), F32),
                   jax.ShapeDtypeStruct((n_tok, 2 * LANES), BF16),
                   jax.ShapeDtypeStruct((n_tok, GM_WIDTH), BF16),
                   jax.ShapeDtypeStruct((batch, GM_CHUNK, GM_WIDTH), F32)],
        compiler_params=_cparams(("arbitrary",)),
        name="even_inproj",
    )(x2d, w_in_p, kv_norm, gm_g, gm_b, ws_m, bs_full, cos_t, sin_a, sin_b)


def _queries(q_a, qn_g, wq_nope, wq_pe, wuk_ref, cos_t, sin_a, sin_b):
    qn = _rms_norm(q_a, qn_g).astype(BF16)
    nope = _dot(qn, wq_nope).astype(BF16)
    lat = jnp.concatenate([_dot(nope[:, pr * LANES:(pr + 1) * LANES], wuk_ref[pr])
                           for pr in range(MLA_HEADS // 2)], axis=1)
    pe = _dot(qn, wq_pe)
    n = MLA_HEADS * LANES
    sin_s = sin_a + sin_b
    pes = [pe[:, h * LANES:(h + 1) * LANES] * cos_t + pe[:, n + h * LANES:n + (h + 1) * LANES] * sin_s
           for h in range(MLA_HEADS)]
    return lat, pes


def _mla_attn_kernel(qa_lo_ref, qa_hi_ref, kcat_ref, qn_ref, wqn_ref, wqp_ref, wuk_ref, wuv_ref,
                     cos_lo_ref, sina_lo_ref, sinb_lo_ref, cos_hi_ref, sina_hi_ref, sinb_hi_ref,
                     out_hbm, qcat_s, m_s, acc_s, obuf, osem, *, nq):
    tq = qa_lo_ref.shape[0]
    n_rows = MLA_HEADS * tq
    b, i = pl.program_id(0), pl.program_id(1)
    step = b * pl.num_programs(1) + i
    n_steps = pl.num_programs(0) * pl.num_programs(1)
    q_tiles = (i, nq - 1 - i)

    def out_copies():
        return [pltpu.make_async_copy(
            obuf.at[t], out_hbm.at[pl.ds(pl.multiple_of((b * nq + q_tiles[t]) * tq, tq), tq), :], osem.at[t])
            for t in range(2)]

    qa = (qa_lo_ref, qa_hi_ref)
    tabs = ((cos_lo_ref, sina_lo_ref, sinb_lo_ref), (cos_hi_ref, sina_hi_ref, sinb_hi_ref))
    for t in range(2):
        lat, pes = _queries(qa[t][...], qn_ref[...], wqn_ref[...], wqp_ref[...], wuk_ref,
                            *(r[...] for r in tabs[t]))
        for h in range(MLA_HEADS):
            qcat_s[t, h * tq:(h + 1) * tq, :] = (ATTN_SCALE_LOG2 * jnp.concatenate(
                [lat[:, h * LANES:(h + 1) * LANES], pes[h]], axis=1)).astype(BF16)

    lanes_x = lambda a, n: jnp.concatenate([a] * n, axis=1)
    rb = ATTN_ROW_BLOCK
    ones = jnp.ones((tq, LANES), BF16)

    def keys(kv_tile):
        k = kcat_ref[pl.ds(pl.multiple_of(kv_tile * tq, tq), tq), :]
        return k, jnp.concatenate([k[:, :KV_RANK], ones], axis=1)

    q_pos = lax.broadcasted_iota(jnp.int32, (rb, tq), 0) & (tq - 1)
    k_pos = lax.broadcasted_iota(jnp.int32, (rb, tq), 1)
    causal = k_pos <= q_pos
    for t in range(2):
        k, v1 = keys(q_tiles[t])
        for r0 in range(0, n_rows, rb):
            rows = slice(r0, r0 + rb)
            s = jnp.where(causal, _dot_nt(qcat_s[t, rows, :], k), -jnp.inf)
            m0 = jnp.broadcast_to(jnp.max(s, axis=-1, keepdims=True), (rb, LANES))
            m_s[t, rows, :] = m0
            acc_s[t, rows, :] = _dot(jnp.exp2(s - lanes_x(m0, tq // LANES)).astype(BF16), v1)

    for u in range(nq - 1):
        hi = u >= i
        t = hi.astype(jnp.int32)
        k, v1 = keys(jnp.where(hi, u - i, u))
        for r0 in range(0, n_rows, rb):
            rows = slice(r0, r0 + rb)
            s = _dot_nt(qcat_s[t, rows, :], k)
            m_old = m_s[t, rows, :]
            m_new = jnp.maximum(m_old, jnp.max(s, axis=-1, keepdims=True))
            alpha = jnp.exp2(m_old - m_new)
            p = jnp.exp2(s - lanes_x(m_new, tq // LANES))
            acc_s[t, rows, :] = lanes_x(alpha, 2) * acc_s[t, rows, :] + _dot(p.astype(BF16), v1)
            m_s[t, rows, :] = m_new

    @pl.when(step > 0)
    def _():
        for cp in out_copies():
            cp.wait()

    for t in range(2):
        o_all = jnp.concatenate(
            [acc_s[t, h * tq:(h + 1) * tq, :KV_RANK] / acc_s[t, h * tq:(h + 1) * tq, KV_RANK:]
             for h in range(MLA_HEADS)], axis=1).astype(BF16)
        obuf[t] = jnp.concatenate(
            [_dot(o_all[:, pr * 2 * LANES:(pr + 1) * 2 * LANES], wuv_ref[pr]) for pr in range(MLA_HEADS // 2)],
            axis=1).astype(BF16)
    for cp in out_copies():
        cp.start()

    @pl.when(step == n_steps - 1)
    def _():
        for cp in out_copies():
            cp.wait()


def _mla_attn(q_a, kcat, qn_g, wq_nope, wq_pe, w_uk, w_uv, cos_t, sin_a, sin_b, batch, seq):
    tq = ATTN_TQ
    nq = seq // tq
    n_attn = MLA_HEADS * V_DIM
    lo = lambda b, i: i
    hi = lambda b, i: nq - 1 - i
    tab = lambda sel: pl.BlockSpec((tq, LANES), lambda b, i: (sel(b, i), 0))
    qa_spec = lambda sel: pl.BlockSpec((tq, Q_RANK), lambda b, i: (b * nq + sel(b, i), 0))
    return pl.pallas_call(
        functools.partial(_mla_attn_kernel, nq=nq),
        grid=(batch, nq // 2),
        in_specs=[qa_spec(lo), qa_spec(hi),
                  pl.BlockSpec((None, seq, 2 * LANES), lambda b, i: (b, 0, 0)),
                  _const_spec((1, Q_RANK)), _const_spec(wq_nope.shape), _const_spec(wq_pe.shape),
                  _const_spec(w_uk.shape), _const_spec(w_uv.shape),
                  tab(lo), tab(lo), tab(lo), tab(hi), tab(hi), tab(hi)],
        out_specs=pl.BlockSpec(memory_space=pl.ANY),
        out_shape=jax.ShapeDtypeStruct((batch * seq, n_attn), BF16),
        scratch_shapes=[pltpu.VMEM((2, MLA_HEADS * tq, 2 * LANES), BF16),
                        pltpu.VMEM((2, MLA_HEADS * tq, LANES), F32),
                        pltpu.VMEM((2, MLA_HEADS * tq, 2 * KV_RANK), F32),
                        pltpu.VMEM((2, tq, n_attn), BF16),
                        pltpu.SemaphoreType.DMA((2,))],
        compiler_params=_cparams(("arbitrary", "arbitrary")),
        name="mla_attn",
    )(q_a, q_a, kcat.reshape(batch, seq, 2 * LANES), qn_g, wq_nope, wq_pe, w_uk, w_uv,
      cos_t, sin_a, sin_b, cos_t, sin_a, sin_b)


def _mix_ffn_kernel(*refs, n_act, has_pre):
    x_ref = refs[0]
    act_refs = refs[1:1 + n_act]
    k = 1 + n_act
    pre_ref = refs[k] if has_pre else None
    k += int(has_pre)
    wo_refs = refs[k:k + n_act]
    k += n_act
    g1_ref, b1_ref, g2_ref, b2_ref, w1_ref, w2_ref, out_ref = refs[k:k + 7]

    tm = x_ref.shape[0]
    rb = min(FFN_ROW_BLOCK, tm)
    blocks = [slice(r0, r0 + rb) for r0 in range(0, tm, rb)]
    fs = []
    for rows in blocks:
        f = None
        for idx in range(n_act):
            a = act_refs[idx][rows, :]
            if has_pre and idx == 0:
                a = _dot(a.astype(BF16), pre_ref[...])
            part = _dot(a.astype(BF16), wo_refs[idx][...])
            f = part if f is None else f + part
        fs.append(f)
    x1s = [_layer_norm(DN_ALPHA * x_ref[rows, :] + f, g1_ref[...], b1_ref[...]) for rows, f in zip(blocks, fs)]
    x1bs = [x1.astype(BF16) for x1 in x1s]
    accs = [None] * len(blocks)
    for c in range(FFN_HIDDEN // FFN_HC):
        cols = slice(c * FFN_HC, (c + 1) * FFN_HC)
        for i, x1b in enumerate(x1bs):
            h = jnp.maximum(_dot(x1b, w1_ref[:, cols]), 0.0)
            part = _dot((h * h).astype(BF16), w2_ref[cols, :])
            accs[i] = part if accs[i] is None else accs[i] + part
    for rows, x1, acc in zip(blocks, x1s, accs):
        out_ref[rows, :] = _layer_norm(DN_ALPHA * x1 + acc, g2_ref[...], b2_ref[...])


def _mix_ffn(x2d, acts, pre_w, wos, ln1_g, ln1_b, ln2_g, ln2_b, w1, w2, name):
    n_tok = x2d.shape[0]
    tm = min(FFN_TM, n_tok)
    row = lambda w: pl.BlockSpec((tm, w), lambda i: (i, 0))
    has_pre = pre_w is not None
    in_specs = [row(D_MODEL)] + [row(a.shape[1]) for a in acts]
    args = [x2d] + list(acts)
    if has_pre:
        in_specs.append(_const_spec(pre_w.shape))
        args.append(pre_w)
    in_specs += [_const_spec(w.shape) for w in wos]
    args += list(wos)
    in_specs += [_const_spec((1, D_MODEL))] * 4 + [_const_spec(w1.shape), _const_spec(w2.shape)]
    args += [ln1_g, ln1_b, ln2_g, ln2_b, w1, w2]
    return pl.pallas_call(
        functools.partial(_mix_ffn_kernel, n_act=len(acts), has_pre=has_pre),
        grid=(n_tok // tm,),
        in_specs=in_specs,
        out_specs=row(D_MODEL),
        out_shape=jax.ShapeDtypeStruct((n_tok, D_MODEL), F32),
        compiler_params=_cparams(("arbitrary",)),
        name=name,
    )(*args)


def _mix_ffn_sample_kernel(*refs, n_act, has_pre):
    x_ref = refs[0]
    act_refs = refs[1:1 + n_act]
    k = 1 + n_act
    pre_ref = refs[k] if has_pre else None
    k += int(has_pre)
    wo_refs = refs[k:k + n_act]
    k += n_act
    g1_ref, b1_ref, g2_ref, b2_ref, w1_ref, w2_ref, out_ref, w1b_ref, w2b_ref, x1_s, acc_s = refs[k:k + 11]
    c = pl.program_id(0)

    @pl.when(c == 0)
    def _():
        f = None
        for idx in range(n_act):
            a = act_refs[idx][...]
            if has_pre and idx == 0:
                a = _dot(a.astype(BF16), pre_ref[...])
            part = _dot(a.astype(BF16), wo_refs[idx][...])
            f = part if f is None else f + part
        x1_s[...] = _layer_norm(DN_ALPHA * x_ref[...] + f, g1_ref[...], b1_ref[...])
        acc_s[...] = jnp.zeros(acc_s.shape, F32)

    w1c = w1_ref[...].astype(BF16)
    w2c = w2_ref[...].astype(BF16)
    w1b_ref[...] = w1c
    w2b_ref[...] = w2c
    h = jnp.maximum(_dot(x1_s[...].astype(BF16), w1c), 0.0)
    acc_s[...] += _dot((h * h).astype(BF16), w2c)

    @pl.when(c == pl.num_programs(0) - 1)
    def _():
        out_ref[...] = _layer_norm(DN_ALPHA * x1_s[...] + acc_s[...], g2_ref[...], b2_ref[...])


def _mix_ffn_sample(x2d, acts, pre_w, wos, ln1_g, ln1_b, ln2_g, ln2_b, w1_all, w2_all, layer, name):
    n_tok = x2d.shape[0]
    has_pre = pre_w is not None
    full = lambda a: pl.BlockSpec(a.shape, lambda c: (0,) * a.ndim)
    args = [x2d] + list(acts) + ([pre_w] if has_pre else []) + list(wos) + [ln1_g, ln1_b, ln2_g, ln2_b]
    in_specs = [full(a) for a in args]
    in_specs += [pl.BlockSpec((None, D_MODEL, FFN_HC), lambda c: (layer, 0, c)),
                 pl.BlockSpec((None, FFN_HC, D_MODEL), lambda c: (layer, c, 0))]
    args += [w1_all, w2_all]
    return pl.pallas_call(
        functools.partial(_mix_ffn_sample_kernel, n_act=len(acts), has_pre=has_pre),
        grid=(FFN_HIDDEN // FFN_HC,),
        in_specs=in_specs,
        out_specs=[pl.BlockSpec((n_tok, D_MODEL), lambda c: (0, 0)),
                   pl.BlockSpec((D_MODEL, FFN_HC), lambda c: (0, c)),
                   pl.BlockSpec((FFN_HC, D_MODEL), lambda c: (c, 0))],
        out_shape=[jax.ShapeDtypeStruct((n_tok, D_MODEL), F32),
                   jax.ShapeDtypeStruct((D_MODEL, FFN_HIDDEN), BF16),
                   jax.ShapeDtypeStruct((FFN_HIDDEN, D_MODEL), BF16)],
        scratch_shapes=[pltpu.VMEM((n_tok, D_MODEL), F32), pltpu.VMEM((n_tok, D_MODEL), F32)],
        compiler_params=_cparams(("arbitrary",)),
        name=name,
    )(*args)


def _even_inproj_sample_kernel(x_ref, w_ref, kvn_ref, gmg_ref, gmb_ref, ws0_ref, bs0_ref,
                               cos_ref, sina_ref, sinb_ref, qn_ref, wqn_ref, wqp_ref, wuk_ref,
                               ckv_ref, kpe_ref, gate_ref, vn_ref, qlat_ref, qpe_ref):
    z = _dot(x_ref[...].astype(BF16), w_ref[...])
    o_c, o_u, o_v, o_k = Q_RANK, Q_RANK + KV_RANK, Q_RANK + KV_RANK + GM_WIDTH, Q_RANK + KV_RANK + 2 * GM_WIDTH
    cos_t, sin_a, sin_b = cos_ref[...], sina_ref[...], sinb_ref[...]
    c_n = _rms_norm(z[:, o_c:o_u], kvn_ref[...])
    ckv_ref[...] = c_n
    kp = _rope128(z[:, o_k:o_k + LANES], cos_t, sin_a, sin_b)
    kpe_ref[...] = kp[:, :ROPE_DIM]
    gu = jax.nn.gelu(z[:, o_u:o_v])
    v_n = _layer_norm(jax.nn.gelu(z[:, o_v:o_k]), gmg_ref[...], gmb_ref[...])
    vn_ref[...] = v_n
    mixed = ws0_ref[...].astype(F32) * v_n.astype(BF16).astype(F32) + bs0_ref[...]
    gate_ref[...] = (gu * mixed).astype(BF16)
    lat, pes = _queries(z[:, :o_c], qn_ref[...], wqn_ref[...], wqp_ref[...], wuk_ref,
                        cos_t, sin_a, sin_b)
    qlat_ref[...] = lat
    qpe_ref[...] = jnp.concatenate(pes, axis=1)


def _even_inproj_sample(xs, w_in_p, kv_norm, gm_g, gm_b, ws0, bs0, cos_t, sin_a, sin_b,
                        qn_g, wq_nope, wq_pe, w_uk):
    n = xs.shape[0]
    args = (xs, w_in_p, kv_norm, gm_g, gm_b, ws0, bs0, cos_t, sin_a, sin_b, qn_g, wq_nope, wq_pe, w_uk)
    full = lambda shape: pl.BlockSpec(shape, lambda i: (0,) * len(shape))
    widths = [(KV_RANK, F32), (ROPE_DIM, F32), (GM_WIDTH, BF16), (GM_WIDTH, F32),
              (MLA_HEADS * LANES, F32), (MLA_HEADS * LANES, F32)]
    return pl.pallas_call(
        _even_inproj_sample_kernel,
        grid=(1,),
        in_specs=[full(a.shape) for a in args],
        out_specs=[full((n, w)) for w, _ in widths],
        out_shape=[jax.ShapeDtypeStruct((n, w), dt) for w, dt in widths],
        compiler_params=_cparams(("arbitrary",)),
        name="even_inproj_sample",
    )(*args)


def _decode_attn_kernel(pt_ref, qlat_ref, qpe_ref, ckvs_ref, kpes_ref, ckv_hbm, kpe_hbm, out_ref,
                        ckv_buf, kpe_buf, m_buf, sem, *, n_pg, n_groups):
    b = pl.program_id(0)
    n_b = pl.num_programs(0)
    assert n_groups % DEC_SLOTS == 0 and DEC_AHEAD + 2 <= DEC_SLOTS

    def group_copies(seq, grp):
        slot = grp % DEC_SLOTS
        cps = []
        for k in range(n_pg):
            page = pt_ref[seq, grp * n_pg + k]
            keys = pl.ds(k * PAGE_SIZE, PAGE_SIZE)
            cps.append(pltpu.make_async_copy(ckv_hbm.at[page], ckv_buf.at[slot, keys, :], sem.at[0, slot]))
            cps.append(pltpu.make_async_copy(kpe_hbm.at[page], kpe_buf.at[slot, :, keys], sem.at[1, slot]))
        return cps

    @pl.when(b == 0)
    def _():
        for g in range(DEC_AHEAD):
            for cp in group_copies(0, g):
                cp.start()

    heads, span = MLA_HEADS, DEC_SPAN_PAGES * PAGE_SIZE
    half_span = span // 2
    n_span = n_pg // DEC_SPAN_PAGES
    qlat = qlat_ref[...]
    zero = jnp.zeros_like(qlat)
    q2 = jnp.concatenate([jnp.concatenate([qlat, zero], axis=1),
                          jnp.concatenate([zero, qlat], axis=1)], axis=0).astype(BF16)
    qpe = qpe_ref[...][:, :ROPE_DIM].astype(BF16)

    def scores(slot):
        s_pe = _dot(qpe, kpe_buf[slot].astype(BF16))
        pieces = []
        for j in range(n_span):
            m_buf[slot, j] = jnp.concatenate(
                [ckv_buf[slot, pl.ds(j * span, half_span), :],
                 ckv_buf[slot, pl.ds(j * span + half_span, half_span), :]], axis=1).astype(BF16)
            s2 = _dot_nt(q2, m_buf[slot, j])
            pieces += [s2[:heads] + s_pe[:, j * span:j * span + half_span],
                       s2[heads:] + s_pe[:, j * span + half_span:(j + 1) * span]]
        return jnp.concatenate(pieces, axis=1) * ATTN_SCALE

    def values(p, grp):
        slot = grp % DEC_SLOTS
        o2 = None
        for j in range(n_span):
            lhs = jnp.concatenate([p[:, j * span:j * span + half_span],
                                   p[:, j * span + half_span:(j + 1) * span]], axis=0).astype(BF16)
            part = _dot(lhs, m_buf[slot, j])
            o2 = part if o2 is None else o2 + part
        return o2[:heads, :KV_RANK] + o2[heads:, KV_RANK:]

    state = dict(m_run=None, m=None, l=None, acc=None)
    raw = {}
    probs = {}

    def softmax(k):
        s = raw.pop(k)
        m_cur = jnp.max(s, axis=-1, keepdims=True)
        m_new = m_cur if state["m_run"] is None else jnp.maximum(state["m_run"], m_cur)
        state["m_run"] = m_new
        probs[k] = (jnp.exp(s - m_new), m_new)

    def fold(k):
        p, m_k = probs.pop(k)
        pv = values(p, k)
        p_sum = jnp.sum(p, axis=-1, keepdims=True)
        if state["m"] is None:
            state["l"], state["acc"] = p_sum, pv
        else:
            alpha = jnp.exp(state["m"] - m_k)
            state["l"], state["acc"] = alpha * state["l"] + p_sum, alpha * state["acc"] + pv
        state["m"] = m_k

    for g in range(n_groups):
        if g + DEC_AHEAD < n_groups:
            for cp in group_copies(b, g + DEC_AHEAD):
                cp.start()
        else:
            @pl.when(b + 1 < n_b)
            def _():
                for cp in group_copies(b + 1, g + DEC_AHEAD - n_groups):
                    cp.start()
        for cp in group_copies(b, g):
            cp.wait()
        raw[g] = scores(g % DEC_SLOTS)
        if g >= 1:
            softmax(g - 1)
        if g >= 2:
            fold(g - 2)
    softmax(n_groups - 1)
    fold(n_groups - 2)
    fold(n_groups - 1)
    m, l, acc = state["m"], state["l"], state["acc"]

    r = lambda a: a.astype(BF16).astype(F32)
    kv = r(ckvs_ref[...])
    s_self = (jnp.sum(r(qlat_ref[...]) * kv, axis=-1, keepdims=True)
              + jnp.sum(r(qpe_ref[...]) * r(kpes_ref[...]), axis=-1, keepdims=True)) * ATTN_SCALE
    m_n = jnp.maximum(m, s_self)
    a = jnp.exp(m - m_n)
    p_self = jnp.exp(s_self - m_n)
    out_ref[...] = (a * acc + r(p_self) * kv) / (a * l + p_self)


def _decode_attn(page_table, qlat2, qpe2, ckv_s3, kpe_s3, cache_ckv_e, cache_kpe_t):
    dec_batch, n_pages = page_table.shape
    n_pg = DEC_PG
    qspec = pl.BlockSpec((MLA_HEADS, LANES), lambda b, pt: (b, 0))
    self_spec = pl.BlockSpec((None, 1, LANES), lambda b, pt: (b, 0, 0))
    hbm = pl.BlockSpec(memory_space=pl.ANY)
    grid_spec = pltpu.PrefetchScalarGridSpec(
        num_scalar_prefetch=1,
        grid=(dec_batch,),
        in_specs=[qspec, qspec, self_spec, self_spec, hbm, hbm],
        out_specs=qspec,
        scratch_shapes=[pltpu.VMEM((DEC_SLOTS, n_pg * PAGE_SIZE, KV_RANK), F32),
                        pltpu.VMEM((DEC_SLOTS, ROPE_DIM, n_pg * PAGE_SIZE), F32),
                        pltpu.VMEM((DEC_SLOTS, n_pg // DEC_SPAN_PAGES, DEC_SPAN_PAGES // 2 * PAGE_SIZE, 2 * KV_RANK),
                                   BF16),
                        pltpu.SemaphoreType.DMA((2, DEC_SLOTS))],
    )
    return pl.pallas_call(
        functools.partial(_decode_attn_kernel, n_pg=n_pg, n_groups=n_pages // n_pg),
        grid_spec=grid_spec,
        out_shape=jax.ShapeDtypeStruct((dec_batch * MLA_HEADS, KV_RANK), F32),
        compiler_params=_cparams(("arbitrary",)),
        name="decode_attn",
    )(page_table, qlat2, qpe2, ckv_s3, kpe_s3, cache_ckv_e, cache_kpe_t)


def _s5_readout(hcat_blocks, u, cw_ref, d_ref, wglu_ref):
    y = jnp.concatenate([_dot(hcat_blocks[k], cw_ref[k]) for k in range(S5_NB)], axis=1)
    y = jax.nn.gelu(y + d_ref[...] * u)
    return y * jax.nn.sigmoid(_dot(y.astype(BF16), wglu_ref[...]))


def _s5_prompt_kernel(x_ref, win_ref, bw2_ref, a2_ref, cw2_ref, d_ref, wglu_ref,
                      yg_ref, hre_ref, him_ref, u_s, y_s, st_s, h_s):
    n_b, seg, _ = x_ref.shape
    half = S5_SLABS // 2
    sub = S5_SUB
    n_sub = seg // sub
    rows_sub = sub * n_b
    n_pair = sub // 2
    rows_pair = n_pair * n_b
    reg = rows_pair + n_b
    q_slabs = S5_WIDTH // LANES

    @pl.when(pl.program_id(0) == 0)
    def _():
        h_s[...] = jnp.zeros(h_s.shape, F32)

    def steps_of(slab_ref, s, q, parity):
        return jnp.concatenate([slab_ref[s, q, pl.ds((2 * j + parity) * n_b, n_b), :] for j in range(n_pair)],
                               axis=0)

    def project(s):
        xs = jnp.concatenate([x_ref[b, s * sub:(s + 1) * sub, :] for b in range(n_b)], axis=0)
        u = _dot(xs.astype(BF16), win_ref[...])
        for b in range(n_b):
            for q in range(q_slabs):
                u_s[s, q, pl.ds(b, sub, stride=n_b), :] = u[b * sub:(b + 1) * sub, q * LANES:(q + 1) * LANES]
        rows = slice(s * reg + n_b, (s + 1) * reg)
        for k in range(S5_NB):
            lhs = jnp.concatenate([steps_of(u_s, s, k, 0), steps_of(u_s, s, k, 1)], axis=1).astype(BF16)
            r = _dot(lhs, bw2_ref[k])
            for q in range(S5_BLK_SLABS):
                st_s[S5_BLK_SLABS * k + q, rows, :] = r[:, q * LANES:(q + 1) * LANES]
                st_s[half + S5_BLK_SLABS * k + q, rows, :] = r[:, (S5_BLK_SLABS + q) * LANES:
                                                                (S5_BLK_SLABS + q + 1) * LANES]

    def recur(s, h):
        st_s[:, s * reg:s * reg + n_b, :] = h
        for j in range(n_pair):
            rows = slice(s * reg + (j + 1) * n_b, s * reg + (j + 2) * n_b)
            inc = st_s[:, rows, :]
            ar, ai = a2_ref[:half], a2_ref[half:]
            hr, hi = h[:half], h[half:]
            h = jnp.concatenate([ar * hr - ai * hi + inc[:half], ar * hi + ai * hr + inc[half:]], axis=0)
            st_s[:, rows, :] = h
        return h

    def read_out(s):
        rows = slice(s * reg, s * reg + rows_pair)
        ys = []
        for k in range(S5_NB):
            lhs = jnp.concatenate(
                [st_s[S5_BLK_SLABS * k + q, rows, :] for q in range(S5_BLK_SLABS)]
                + [st_s[half + S5_BLK_SLABS * k + q, rows, :] for q in range(S5_BLK_SLABS)]
                + [steps_of(u_s, s, k, 0), steps_of(u_s, s, k, 1)], axis=1).astype(BF16)
            y2 = _dot(lhs, cw2_ref[k])
            ys.append(jnp.concatenate(
                [y2[j * n_b:(j + 1) * n_b, par * LANES:(par + 1) * LANES] for j in range(n_pair) for par in range(2)],
                axis=0))
        u_t = jnp.concatenate([u_s[s, q] for q in range(q_slabs)], axis=1)
        y = jax.nn.gelu(jnp.concatenate(ys, axis=1) + d_ref[...] * u_t)
        y = y * jax.nn.sigmoid(_dot(y.astype(BF16), wglu_ref[...]))
        for q in range(q_slabs):
            y_s[s, q] = y[:, q * LANES:(q + 1) * LANES]
        for b in range(n_b):
            yg_ref[b, s * sub:(s + 1) * sub, :] = jnp.concatenate(
                [y_s[s, q, pl.ds(b, sub, stride=n_b), :] for q in range(q_slabs)], axis=1).astype(BF16)

    h = h_s[...]
    project(0)
    for s in range(n_sub):
        if s + 1 < n_sub:
            project(s + 1)
        h = recur(s, h)
        if s >= 1:
            read_out(s - 1)
    read_out(n_sub - 1)
    h_s[...] = h
    hre_ref[...] = jnp.concatenate([h[s] for s in range(half)], axis=1)
    him_ref[...] = jnp.concatenate([h[half + s] for s in range(half)], axis=1)


def _s5_prompt(x3, w_in, bw2, a2_b, cw2, d_row, w_glu):
    batch, seq, _ = x3.shape
    seg = S5_L
    n_sub = seg // S5_SUB
    slab_rows = (n_sub, S5_WIDTH // LANES, S5_SUB * batch, LANES)
    return pl.pallas_call(
        _s5_prompt_kernel,
        grid=(seq // seg,),
        in_specs=[pl.BlockSpec((batch, seg, D_MODEL), lambda c: (0, c, 0)),
                  _const_spec(w_in.shape), _const_spec(bw2.shape), _const_spec(a2_b.shape),
                  _const_spec(cw2.shape), _const_spec(d_row.shape), _const_spec(w_glu.shape)],
        out_specs=[pl.BlockSpec((batch, seg, S5_WIDTH), lambda c: (0, c, 0)),
                   pl.BlockSpec((batch, S5_STATES), lambda c: (0, 0)),
                   pl.BlockSpec((batch, S5_STATES), lambda c: (0, 0))],
        out_shape=[jax.ShapeDtypeStruct((batch, seq, S5_WIDTH), BF16),
                   jax.ShapeDtypeStruct((batch, S5_STATES), F32),
                   jax.ShapeDtypeStruct((batch, S5_STATES), F32)],
        scratch_shapes=[pltpu.VMEM(slab_rows, F32), pltpu.VMEM(slab_rows, F32),
                        pltpu.VMEM((S5_SLABS, n_sub * (S5_SUB // 2 + 1) * batch, LANES), F32),
                        pltpu.VMEM((S5_SLABS, batch, LANES), F32)],
        compiler_params=_cparams(("arbitrary",)),
        name="s5_prompt",
    )(x3, w_in, bw2, a2_b, cw2, d_row, w_glu)


def _s5_sample_kernel(x_ref, h0r_ref, h0i_ref, win_ref, bw_ref, ar_ref, ai_ref, cw_ref, d_ref, wglu_ref,
                      yg_ref, hre_ref, him_ref):
    u = _dot(x_ref[...].astype(BF16), win_ref[...])
    ub = u.astype(BF16)
    bu = [_dot(ub[:, k * LANES:(k + 1) * LANES], bw_ref[k]) for k in range(S5_NB)]
    w = S5_KB * S5_STATE
    bu_r = jnp.concatenate([r[:, :w] for r in bu], axis=1)
    bu_i = jnp.concatenate([r[:, w:] for r in bu], axis=1)
    ar, ai, h0r, h0i = ar_ref[...], ai_ref[...], h0r_ref[...], h0i_ref[...]
    hr = bu_r + (ar * h0r - ai * h0i)
    hi = bu_i + (ar * h0i + ai * h0r)
    hre_ref[...] = hr
    him_ref[...] = hi
    blocks = [jnp.concatenate([hr[:, k * w:(k + 1) * w], hi[:, k * w:(k + 1) * w]], axis=1).astype(BF16)
              for k in range(S5_NB)]
    yg_ref[...] = _s5_readout(blocks, u, cw_ref, d_ref, wglu_ref).astype(BF16)


def _s5_sample(xs, h0r, h0i, w_in, bw, a_r, a_i, cw, d_row, w_glu):
    n = xs.shape[0]
    args = (xs, h0r, h0i, w_in, bw, a_r, a_i, cw, d_row, w_glu)
    full = lambda shape: pl.BlockSpec(shape, lambda i: (0,) * len(shape))
    return pl.pallas_call(
        _s5_sample_kernel,
        grid=(1,),
        in_specs=[full(a.shape) for a in args],
        out_specs=[full((n, S5_WIDTH)), full((n, S5_STATES)), full((n, S5_STATES))],
        out_shape=[jax.ShapeDtypeStruct((n, S5_WIDTH), BF16),
                   jax.ShapeDtypeStruct((n, S5_STATES), F32),
                   jax.ShapeDtypeStruct((n, S5_STATES), F32)],
        compiler_params=_cparams(("arbitrary",)),
        name="s5_sample",
    )(*args)


def _rope_tables(pos):
    freqs = ROPE_THETA ** (-jnp.arange(ROPE_HALF, dtype=F32) / ROPE_HALF)
    ang = pos.astype(F32)[:, None] * freqs
    cos, sin = jnp.cos(ang), jnp.sin(ang)
    zero = jnp.zeros_like(cos)
    pad = jnp.zeros((pos.shape[0], LANES - ROPE_DIM), F32)
    cos_t = jnp.concatenate([cos, cos, pad], axis=1)
    sin_a = jnp.concatenate([zero, sin, pad], axis=1)
    sin_b = jnp.concatenate([-sin, zero, pad], axis=1)
    return cos_t, sin_a, sin_b


def _block_diag(blocks):
    *lead, n, r, c = blocks.shape
    eye = jnp.eye(n, dtype=blocks.dtype)
    return jnp.einsum('...nrc,nm->...nrmc', blocks, eye).reshape(*lead, n * r, n * c)


def _even_params(w_in, q_norm, w_q_b, kv_norm, w_kv_b, gm_g, gm_b, gm_w_s, gm_b_s, w_out):
    i1, i2, i3, i4 = Q_RANK, Q_RANK + KV_RANK, Q_RANK + KV_RANK + ROPE_DIM, Q_RANK + KV_RANK + ROPE_DIM + GM_WIDTH
    w_in_p = jnp.concatenate([w_in[:, :i2], w_in[:, i3:], w_in[:, i2:i3],
                              jnp.zeros((D_MODEL, LANES - ROPE_DIM), F32)], axis=1).astype(BF16)
    wq_nope = w_q_b[:, :, :NOPE_DIM].reshape(Q_RANK, MLA_HEADS * NOPE_DIM).astype(BF16)
    pe = w_q_b[:, :, NOPE_DIM:]
    pe_sw = jnp.concatenate([pe[:, :, ROPE_HALF:], pe[:, :, :ROPE_HALF]], axis=2)
    lane_pad = jnp.zeros((Q_RANK, MLA_HEADS, LANES - ROPE_DIM), F32)
    wq_pe = jnp.concatenate(
        [jnp.concatenate([w, lane_pad], axis=2).reshape(Q_RANK, MLA_HEADS * LANES) for w in (pe, pe_sw)],
        axis=1).astype(BF16)
    uk = jnp.transpose(w_kv_b[:, :, :NOPE_DIM], (1, 2, 0))
    uv = jnp.transpose(w_kv_b[:, :, NOPE_DIM:], (1, 0, 2))
    pairs = MLA_HEADS // 2
    w_uk = _block_diag(uk.reshape(pairs, 2, NOPE_DIM, KV_RANK)).astype(BF16)
    w_uv_p = _block_diag(uv.reshape(pairs, 2, KV_RANK, V_DIM)).astype(BF16)
    w_uv = _block_diag(uv).astype(BF16)
    causal = jnp.tril(jnp.ones((GM_CHUNK, GM_CHUNK), dtype=bool))
    ws_m = jnp.where(causal[None], gm_w_s, 0).astype(BF16)
    bs_full = jnp.repeat(gm_b_s.T, GM_HEAD_DIM, axis=1)
    ws0 = jnp.repeat(gm_w_s[:, 0, 0], GM_HEAD_DIM)[None, :].astype(BF16)
    bs0 = bs_full[:1]
    n_attn = MLA_HEADS * V_DIM
    return dict(w_in_p=w_in_p, qn=q_norm[None, :], wq_nope=wq_nope, wq_pe=wq_pe, w_uk=w_uk, w_uv=w_uv, w_uv_p=w_uv_p,
                kvn=kv_norm[None, :], gm_g=gm_g[None, :], gm_b=gm_b[None, :], ws_m=ws_m, bs_full=bs_full,
                ws0=ws0, bs0=bs0, wo_attn=w_out[:n_attn].astype(BF16), wo_gate=w_out[n_attn:].astype(BF16))


def _s5_params(w_in, a_re, a_im, b_re, b_im, c_re, c_im, d, log_dt, w_glu, w_out, batch):
    dt = jnp.exp(log_dt)[:, None]
    ld_r, ld_i = a_re * dt, a_im * dt
    mag = jnp.exp(ld_r)
    ab_r, ab_i = mag * jnp.cos(ld_i), mag * jnp.sin(ld_i)
    den = a_re * a_re + a_im * a_im
    cr = ((ab_r - 1.0) * a_re + ab_i * a_im) / den
    ci = (ab_i * a_re - (ab_r - 1.0) * a_im) / den
    bb_r = cr[..., None] * b_re - ci[..., None] * b_im
    bb_i = cr[..., None] * b_im + ci[..., None] * b_re

    a_r, a_i = ab_r.reshape(1, S5_STATES), ab_i.reshape(1, S5_STATES)
    half = S5_SLABS // 2
    cmul = lambda xr, xi, yr, yi: (xr * yr - xi * yi, xr * yi + xi * yr)
    a2_r, a2_i = cmul(ab_r, ab_i, ab_r, ab_i)
    abb_r, abb_i = cmul(ab_r[..., None], ab_i[..., None], bb_r, bb_i)
    ca_r, ca_i = cmul(c_re, c_im, ab_r[:, None, :], ab_i[:, None, :])
    ca2_r, ca2_i = cmul(c_re, c_im, a2_r[:, None, :], a2_i[:, None, :])
    cc = (jnp.einsum('kgcp,gpd->kgcd', jnp.stack([c_re, ca_r]), bb_r)
          - jnp.einsum('kgcp,gpd->kgcd', jnp.stack([c_im, ca_i]), bb_i))

    def blocks(mats, rows, cols):
        m = jnp.swapaxes(jnp.stack(mats), 2, 3)
        return _block_diag(m.reshape(len(mats), S5_NB, S5_KB, rows, cols))

    b_r, b_i, ab_br, ab_bi = blocks([bb_r, bb_i, abb_r, abb_i], S5_GROUP_DIM, S5_STATE)
    c_r, c_i, ca_br, ca_bi, ca2_br, ca2_bi = blocks([c_re, -c_im, ca_r, -ca_i, ca2_r, -ca2_i],
                                                    S5_STATE, S5_GROUP_DIM)
    cb_b, cab_b = blocks([cc[0], cc[1]], S5_GROUP_DIM, S5_GROUP_DIM)
    cat = jnp.concatenate
    bw = cat([b_r, b_i], axis=2).astype(BF16)
    cw = cat([c_r, c_i], axis=1).astype(BF16)
    bw2 = cat([cat([ab_br, ab_bi], axis=2), cat([b_r, b_i], axis=2)], axis=1).astype(BF16)
    cw2 = cat([cat([ca_br, ca2_br], axis=2),
               cat([ca_bi, ca2_bi], axis=2),
               cat([cb_b, cab_b], axis=2),
               cat([jnp.zeros_like(cb_b), cb_b], axis=2)],
              axis=1).astype(BF16)
    a2_b = jnp.concatenate([jnp.broadcast_to(a2_r.reshape(half, 1, LANES), (half, batch, LANES)),
                            jnp.broadcast_to(a2_i.reshape(half, 1, LANES), (half, batch, LANES))], axis=0)
    return dict(w_in=w_in.astype(BF16), bw=bw, cw=cw, a_r=a_r, a_i=a_i, bw2=bw2, cw2=cw2, a2_b=a2_b,
                d=d.reshape(1, S5_WIDTH), w_glu=w_glu.astype(BF16), w_out=w_out.astype(BF16))


def kernel(x_prompt, x_sample, cache_ckv, cache_kpe, state_s5_re, state_s5_im, page_table, w_in_even, mla_q_norm, mla_w_q_b, mla_kv_norm, mla_w_kv_b, gm_norm_g, gm_norm_b, gm_w_s, gm_b_s, w_out_even, w_in_odd, s5_a_re, s5_a_im, s5_b_re, s5_b_im, s5_c_re, s5_c_im, s5_d, s5_log_dt, s5_w_glu, w_out_odd, ln_mix_g, ln_mix_b, ln_ffn_g, ln_ffn_b, ffn_w1, ffn_w2):
    batch, seq, _ = x_prompt.shape
    dec_batch, dec_seq, _ = x_sample.shape
    assert dec_seq == 1 and seq % INPROJ_TM == 0 and seq % ATTN_TQ == 0 and seq % S5_L == 0
    assert (seq // ATTN_TQ) % 2 == 0
    assert (batch * seq) % FFN_TM == 0 and page_table.shape[1] % DEC_PG == 0
    assert batch == SUBLANES

    xp = x_prompt.reshape(batch * seq, D_MODEL)
    xs = x_sample.reshape(dec_batch, D_MODEL)
    tabs_p = _rope_tables(jnp.arange(seq, dtype=jnp.int32))
    tabs_s = _rope_tables(PAST_LEN + jnp.arange(dec_seq, dtype=jnp.int32))

    outs = {k: [] for k in ("ckv_p", "kpe_p", "ckv_s", "kpe_s", "gmv_p", "gmv_s",
                            "s5re_p", "s5im_p", "s5re_s", "s5im_s")}
    for layer in range(DEPTH):
        ln = (ln_mix_g[layer][None, :], ln_mix_b[layer][None, :],
              ln_ffn_g[layer][None, :], ln_ffn_b[layer][None, :])
        if layer % 2 == 0:
            e = layer // 2
            p = _even_params(w_in_even[e], mla_q_norm[e], mla_w_q_b[e], mla_kv_norm[e], mla_w_kv_b[e],
                             gm_norm_g[e], gm_norm_b[e], gm_w_s[e], gm_b_s[e], w_out_even[e])
            q_a, ckv, kpe, kcat, gate, gmv = _even_inproj(
                xp, p["w_in_p"], p["kvn"], p["gm_g"], p["gm_b"], p["ws_m"], p["bs_full"], *tabs_p, batch, seq)
            attn = _mla_attn(q_a, kcat, p["qn"], p["wq_nope"], p["wq_pe"], p["w_uk"], p["w_uv_p"], *tabs_p,
                             batch, seq)
            outs["ckv_p"].append(ckv.reshape(batch, seq, KV_RANK))
            outs["kpe_p"].append(kpe.reshape(batch, seq, ROPE_DIM))
            outs["gmv_p"].append(gmv)
            ckv_s, kpe_s, gate_s, vn_s, qlat_s, qpe_s = _even_inproj_sample(
                xs, p["w_in_p"], p["kvn"], p["gm_g"], p["gm_b"], p["ws0"], p["bs0"], *tabs_s,
                p["qn"], p["wq_nope"], p["wq_pe"], p["w_uk"])
            kpe_pad = jnp.concatenate([kpe_s, jnp.zeros((dec_batch, LANES - ROPE_DIM), F32)], axis=1)
            o_lat = _decode_attn(page_table,
                                 qlat_s.reshape(dec_batch * MLA_HEADS, LANES),
                                 qpe_s.reshape(dec_batch * MLA_HEADS, LANES),
                                 ckv_s.reshape(dec_batch, 1, KV_RANK), kpe_pad.reshape(dec_batch, 1, LANES),
                                 cache_ckv[e], jnp.swapaxes(cache_kpe[e], 1, 2))
            xs, w1_b, w2_b = _mix_ffn_sample(
                xs, [o_lat.reshape(dec_batch, MLA_HEADS * KV_RANK), gate_s], p["w_uv"],
                [p["wo_attn"], p["wo_gate"]], *ln, ffn_w1, ffn_w2, layer, name="even_ffn_sample")
            xp = _mix_ffn(xp, [attn, gate], None, [p["wo_attn"], p["wo_gate"]], *ln, w1_b, w2_b,
                          name="even_ffn_prompt")
            outs["ckv_s"].append(ckv_s.reshape(dec_batch, 1, KV_RANK))
            outs["kpe_s"].append(kpe_s.reshape(dec_batch, 1, ROPE_DIM))
            outs["gmv_s"].append(vn_s.reshape(dec_batch, 1, GM_WIDTH))
        else:
            o = layer // 2
            p = _s5_params(w_in_odd[o], s5_a_re[o], s5_a_im[o], s5_b_re[o], s5_b_im[o], s5_c_re[o],
                           s5_c_im[o], s5_d[o], s5_log_dt[o], s5_w_glu[o], w_out_odd[o], batch)
            yg, hre, him = _s5_prompt(xp.reshape(batch, seq, D_MODEL), p["w_in"], p["bw2"], p["a2_b"],
                                      p["cw2"], p["d"], p["w_glu"])
            outs["s5re_p"].append(hre.reshape(batch, S5_GROUPS, S5_STATE))
            outs["s5im_p"].append(him.reshape(batch, S5_GROUPS, S5_STATE))
            yg_s, hre_s, him_s = _s5_sample(
                xs, state_s5_re[o].reshape(dec_batch, S5_STATES), state_s5_im[o].reshape(dec_batch, S5_STATES),
                p["w_in"], p["bw"], p["a_r"], p["a_i"], p["cw"], p["d"], p["w_glu"])
            xs, w1_b, w2_b = _mix_ffn_sample(xs, [yg_s], None, [p["w_out"]], *ln, ffn_w1, ffn_w2, layer,
                                             name="odd_ffn_sample")
            xp = _mix_ffn(xp, [yg.reshape(batch * seq, S5_WIDTH)], None, [p["w_out"]], *ln, w1_b, w2_b,
                          name="odd_ffn_prompt")
            outs["s5re_s"].append(hre_s.reshape(dec_batch, S5_GROUPS, S5_STATE))
            outs["s5im_s"].append(him_s.reshape(dec_batch, S5_GROUPS, S5_STATE))

    st = jnp.stack
    return (xp.reshape(batch, seq, D_MODEL), xs.reshape(dec_batch, dec_seq, D_MODEL),
            st(outs["ckv_p"]), st(outs["kpe_p"]), st(outs["ckv_s"]), st(outs["kpe_s"]),
            st(outs["gmv_p"]), st(outs["gmv_s"]),
            st(outs["s5re_p"]), st(outs["s5im_p"]), st(outs["s5re_s"]), st(outs["s5im_s"]))
```

```python
import functools
import math

import jax
import jax.numpy as jnp
from jax import lax
from jax.experimental import pallas as pl
from jax.experimental.pallas import tpu as pltpu

F32 = jnp.float32
BF16 = jnp.bfloat16

D_MODEL = 1024
DEPTH = 2
PAST_LEN = 16384
PAGE_SIZE = 128
MLA_HEADS = 8
NOPE_DIM = 64
ROPE_DIM = 32
ROPE_HALF = ROPE_DIM // 2
V_DIM = 64
Q_RANK = 256
KV_RANK = 128
ROPE_THETA = 10000.0
ATTN_SCALE = 1.0 / math.sqrt(NOPE_DIM + ROPE_DIM)
ATTN_SCALE_LOG2 = ATTN_SCALE * math.log2(math.e)
GM_HEADS = 8
GM_HEAD_DIM = 64
GM_WIDTH = GM_HEADS * GM_HEAD_DIM
GM_CHUNK = 128
S5_GROUP_DIM = 16
S5_GROUPS = 32
S5_STATE = 64
S5_WIDTH = S5_GROUPS * S5_GROUP_DIM
S5_STATES = S5_GROUPS * S5_STATE
FFN_HIDDEN = 4 * D_MODEL
DN_ALPHA = (2 * DEPTH) ** 0.25
LN_EPS = 1e-5
RMS_EPS = 1e-6

LANES = 128
SUBLANES = 8
VMEM_LIMIT_BYTES = 56 * 1024 * 1024

INPROJ_TM = 512
INPROJ_ROW_BLOCK = 256
ATTN_TQ = 256
ATTN_ROW_BLOCK = 256
FFN_TM = 1024
FFN_HC = 1024
FFN_ROW_BLOCK = 256
S5_L = 128
S5_SUB = 32
DEC_PG = 32
DEC_SLOTS = 4
DEC_AHEAD = 2
DEC_SPAN_PAGES = 4
S5_KB = 8
S5_NB = S5_GROUPS // S5_KB
S5_SLABS = 2 * S5_STATES // LANES
S5_BLK_SLABS = S5_KB * S5_STATE // LANES


def _cparams(sem):
    return pltpu.CompilerParams(dimension_semantics=sem, vmem_limit_bytes=VMEM_LIMIT_BYTES)


def _const_spec(shape):
    n = len(shape)
    return pl.BlockSpec(shape, lambda *_: (0,) * n, pipeline_mode=pl.Buffered(1))


def _layer_norm(x, g, b):
    mu = jnp.mean(x, axis=-1, keepdims=True)
    xc = x - mu
    var = jnp.mean(xc * xc, axis=-1, keepdims=True)
    return xc * lax.rsqrt(var + LN_EPS) * g + b


def _rms_norm(x, g):
    return x * lax.rsqrt(jnp.mean(x * x, axis=-1, keepdims=True) + RMS_EPS) * g


def _rope128(x, cos_t, sin_a, sin_b):
    return (x * cos_t + pltpu.roll(x, ROPE_HALF, 1) * sin_a
            + pltpu.roll(x, LANES - ROPE_HALF, 1) * sin_b)


def _dot(a, b):
    return jnp.dot(a, b, preferred_element_type=F32)


def _dot_nt(a, b):
    return lax.dot_general(a, b, (((1,), (1,)), ((), ())), preferred_element_type=F32)


def _even_inproj_kernel(x_ref, w_ref, kvn_ref, gmg_ref, gmb_ref, ws_ref, bs_ref,
                        cos_ref, sina_ref, sinb_ref,
                        qa_ref, ckv_ref, kpe_ref, kcat_ref, gate_ref, gmv_ref):
    tm = x_ref.shape[0]
    o_c, o_u, o_v, o_k = Q_RANK, Q_RANK + KV_RANK, Q_RANK + KV_RANK + GM_WIDTH, Q_RANK + KV_RANK + 2 * GM_WIDTH
    rb = min(INPROJ_ROW_BLOCK, tm)
    blocks = [slice(r0, r0 + rb) for r0 in range(0, tm, rb)]
    zs = [_dot(x_ref[rows, :].astype(BF16), w_ref[...]) for rows in blocks]

    gus, vns = [], []
    for rows, z in zip(blocks, zs):
        qa_ref[rows, :] = z[:, :o_c]
        c_n = _rms_norm(z[:, o_c:o_u], kvn_ref[...])
        ckv_ref[rows, :] = c_n
        kp = _rope128(z[:, o_k:o_k + LANES], cos_ref[rows, :], sina_ref[rows, :], sinb_ref[rows, :])
        kpe_ref[rows, :] = kp[:, :ROPE_DIM]
        kcat_ref[rows, :] = jnp.concatenate([c_n, kp], axis=1).astype(BF16)
        gus.append(jax.nn.gelu(z[:, o_u:o_v]))
        vns.append(_layer_norm(jax.nn.gelu(z[:, o_v:o_k]), gmg_ref[...], gmb_ref[...]))

    gmv_ref[0] = vns[-1][rb - GM_CHUNK:, :]

    lane = lax.broadcasted_iota(jnp.int32, (GM_CHUNK, LANES), 1)
    lo = lane < GM_HEAD_DIM
    bs = bs_ref[...]
    for rows, gu, v_n in zip(blocks, gus, vns):
        for ci in range(rb // GM_CHUNK):
            chunk = slice(ci * GM_CHUNK, (ci + 1) * GM_CHUNK)
            pieces = []
            for pr in range(GM_WIDTH // LANES):
                r = v_n[chunk, pr * LANES:(pr + 1) * LANES]
                m0 = _dot(ws_ref[2 * pr], jnp.where(lo, r, 0.0).astype(BF16))
                m1 = _dot(ws_ref[2 * pr + 1], jnp.where(lo, 0.0, r).astype(BF16))
                pieces.append(m0 + m1)
            mixed = jnp.concatenate(pieces, axis=1) + bs
            gate_ref[pl.ds(rows.start + ci * GM_CHUNK, GM_CHUNK), :] = (gu[chunk, :] * mixed).astype(BF16)


def _even_inproj(x2d, w_in_p, kv_norm, gm_g, gm_b, ws_m, bs_full, cos_t, sin_a, sin_b, batch, seq):
    n_tok = x2d.shape[0]
    tm = INPROJ_TM
    tiles_per_seq = seq // tm
    row = lambda w: pl.BlockSpec((tm, w), lambda i: (i, 0))
    tab = pl.BlockSpec((tm, LANES), lambda i: (i % tiles_per_seq, 0))
    n_in = w_in_p.shape[1]
    return pl.pallas_call(
        _even_inproj_kernel,
        grid=(n_tok // tm,),
        in_specs=[row(D_MODEL), _const_spec((D_MODEL, n_in)), _const_spec((1, KV_RANK)),
                  _const_spec((1, GM_WIDTH)), _const_spec((1, GM_WIDTH)),
                  _const_spec((GM_HEADS, GM_CHUNK, GM_CHUNK)), _const_spec((GM_CHUNK, GM_WIDTH)),
                  tab, tab, tab],
        out_specs=[row(Q_RANK), row(KV_RANK), row(ROPE_DIM), row(2 * LANES), row(GM_WIDTH),
                   pl.BlockSpec((1, GM_CHUNK, GM_WIDTH), lambda i: (i // tiles_per_seq, 0, 0))],
        out_shape=[jax.ShapeDtypeStruct((n_tok, Q_RANK), F32),
                   jax.ShapeDtypeStruct((n_tok, KV_RANK), F32),
                   jax.ShapeDtypeStruct((n_tok, ROPE_DIM), F32),
                   jax.ShapeDtypeStruct((n_tok, 2 * LANES), BF16),
                   jax.ShapeDtypeStruct((n_tok, GM_WIDTH), BF16),
                   jax.ShapeDtypeStruct((batch, GM_CHUNK, GM_WIDTH), F32)],
        compiler_params=_cparams(("arbitrary",)),
        name="even_inproj",
    )(x2d, w_in_p, kv_norm, gm_g, gm_b, ws_m, bs_full, cos_t, sin_a, sin_b)


def _queries(q_a, qn_g, wq_nope, wq_pe, wuk_ref, cos_t, sin_a, sin_b):
    qn = _rms_norm(q_a, qn_g).astype(BF16)
    nope = _dot(qn, wq_nope).astype(BF16)
    lat = jnp.concatenate([_dot(nope[:, pr * LANES:(pr + 1) * LANES], wuk_ref[pr])
                           for pr in range(MLA_HEADS // 2)], axis=1)
    pe = _dot(qn, wq_pe)
    n = MLA_HEADS * LANES
    sin_s = sin_a + sin_b
    pes = [pe[:, h * LANES:(h + 1) * LANES] * cos_t + pe[:, n + h * LANES:n + (h + 1) * LANES] * sin_s
           for h in range(MLA_HEADS)]
    return lat, pes


def _mla_attn_kernel(qa_lo_ref, qa_hi_ref, kcat_ref, qn_ref, wqn_ref, wqp_ref, wuk_ref, wuv_ref,
                     cos_lo_ref, sina_lo_ref, sinb_lo_ref, cos_hi_ref, sina_hi_ref, sinb_hi_ref,
                     out_hbm, qcat_s, m_s, acc_s, obuf, osem, *, nq):
    tq = qa_lo_ref.shape[0]
    n_rows = MLA_HEADS * tq
    b, i = pl.program_id(0), pl.program_id(1)
    step = b * pl.num_programs(1) + i
    n_steps = pl.num_programs(0) * pl.num_programs(1)
    q_tiles = (i, nq - 1 - i)

    def out_copies():
        return [pltpu.make_async_copy(
            obuf.at[t], out_hbm.at[pl.ds(pl.multiple_of((b * nq + q_tiles[t]) * tq, tq), tq), :], osem.at[t])
            for t in range(2)]

    qa = (qa_lo_ref, qa_hi_ref)
    tabs = ((cos_lo_ref, sina_lo_ref, sinb_lo_ref), (cos_hi_ref, sina_hi_ref, sinb_hi_ref))
    for t in range(2):
        lat, pes = _queries(qa[t][...], qn_ref[...], wqn_ref[...], wqp_ref[...], wuk_ref,
                            *(r[...] for r in tabs[t]))
        for h in range(MLA_HEADS):
            qcat_s[t, h * tq:(h + 1) * tq, :] = (ATTN_SCALE_LOG2 * jnp.concatenate(
                [lat[:, h * LANES:(h + 1) * LANES], pes[h]], axis=1)).astype(BF16)

    lanes_x = lambda a, n: jnp.concatenate([a] * n, axis=1)
    rb = ATTN_ROW_BLOCK
    ones = jnp.ones((tq, LANES), BF16)

    def keys(kv_tile):
        k = kcat_ref[pl.ds(pl.multiple_of(kv_tile * tq, tq), tq), :]
        return k, jnp.concatenate([k[:, :KV_RANK], ones], axis=1)

    q_pos = lax.broadcasted_iota(jnp.int32, (rb, tq), 0) & (tq - 1)
    k_pos = lax.broadcasted_iota(jnp.int32, (rb, tq), 1)
    causal = k_pos <= q_pos
    for t in range(2):
        k, v1 = keys(q_tiles[t])
        for r0 in range(0, n_rows, rb):
            rows = slice(r0, r0 + rb)
            s = jnp.where(causal, _dot_nt(qcat_s[t, rows, :], k), -jnp.inf)
            m0 = jnp.broadcast_to(jnp.max(s, axis=-1, keepdims=True), (rb, LANES))
            m_s[t, rows, :] = m0
            acc_s[t, rows, :] = _dot(jnp.exp2(s - lanes_x(m0, tq // LANES)).astype(BF16), v1)

    for u in range(nq - 1):
        hi = u >= i
        t = hi.astype(jnp.int32)
        k, v1 = keys(jnp.where(hi, u - i, u))
        for r0 in range(0, n_rows, rb):
            rows = slice(r0, r0 + rb)
            s = _dot_nt(qcat_s[t, rows, :], k)
            m_old = m_s[t, rows, :]
            m_new = jnp.maximum(m_old, jnp.max(s, axis=-1, keepdims=True))
            alpha = jnp.exp2(m_old - m_new)
            p = jnp.exp2(s - lanes_x(m_new, tq // LANES))
            acc_s[t, rows, :] = lanes_x(alpha, 2) * acc_s[t, rows, :] + _dot(p.astype(BF16), v1)
            m_s[t, rows, :] = m_new

    @pl.when(step > 0)
    def _():
        for cp in out_copies():
            cp.wait()

    for t in range(2):
        o_all = jnp.concatenate(
            [acc_s[t, h * tq:(h + 1) * tq, :KV_RANK] / acc_s[t, h * tq:(h + 1) * tq, KV_RANK:]
             for h in range(MLA_HEADS)], axis=1).astype(BF16)
        obuf[t] = jnp.concatenate(
            [_dot(o_all[:, pr * 2 * LANES:(pr + 1) * 2 * LANES], wuv_ref[pr]) for pr in range(MLA_HEADS // 2)],
            axis=1).astype(BF16)
    for cp in out_copies():
        cp.start()

    @pl.when(step == n_steps - 1)
    def _():
        for cp in out_copies():
            cp.wait()


def _mla_attn(q_a, kcat, qn_g, wq_nope, wq_pe, w_uk, w_uv, cos_t, sin_a, sin_b, batch, seq):
    tq = ATTN_TQ
    nq = seq // tq
    n_attn = MLA_HEADS * V_DIM
    lo = lambda b, i: i
    hi = lambda b, i: nq - 1 - i
    tab = lambda sel: pl.BlockSpec((tq, LANES), lambda b, i: (sel(b, i), 0))
    qa_spec = lambda sel: pl.BlockSpec((tq, Q_RANK), lambda b, i: (b * nq + sel(b, i), 0))
    return pl.pallas_call(
        functools.partial(_mla_attn_kernel, nq=nq),
        grid=(batch, nq // 2),
        in_specs=[qa_spec(lo), qa_spec(hi),
                  pl.BlockSpec((None, seq, 2 * LANES), lambda b, i: (b, 0, 0)),
                  _const_spec((1, Q_RANK)), _const_spec(wq_nope.shape), _const_spec(wq_pe.shape),
                  _const_spec(w_uk.shape), _const_spec(w_uv.shape),
                  tab(lo), tab(lo), tab(lo), tab(hi), tab(hi), tab(hi)],
        out_specs=pl.BlockSpec(memory_space=pl.ANY),
        out_shape=jax.ShapeDtypeStruct((batch * seq, n_attn), BF16),
        scratch_shapes=[pltpu.VMEM((2, MLA_HEADS * tq, 2 * LANES), BF16),
                        pltpu.VMEM((2, MLA_HEADS * tq, LANES), F32),
                        pltpu.VMEM((2, MLA_HEADS * tq, 2 * KV_RANK), F32),
                        pltpu.VMEM((2, tq, n_attn), BF16),
                        pltpu.SemaphoreType.DMA((2,))],
        compiler_params=_cparams(("arbitrary", "arbitrary")),
        name="mla_attn",
    )(q_a, q_a, kcat.reshape(batch, seq, 2 * LANES), qn_g, wq_nope, wq_pe, w_uk, w_uv,
      cos_t, sin_a, sin_b, cos_t, sin_a, sin_b)


def _mix_ffn_kernel(*refs, n_act, has_pre):
    x_ref = refs[0]
    act_refs = refs[1:1 + n_act]
    k = 1 + n_act
    pre_ref = refs[k] if has_pre else None
    k += int(has_pre)
    wo_refs = refs[k:k + n_act]
    k += n_act
    g1_ref, b1_ref, g2_ref, b2_ref, w1_ref, w2_ref, out_ref = refs[k:k + 7]

    tm = x_ref.shape[0]
    rb = min(FFN_ROW_BLOCK, tm)
    blocks = [slice(r0, r0 + rb) for r0 in range(0, tm, rb)]
    fs = []
    for rows in blocks:
        f = None
        for idx in range(n_act):
            a = act_refs[idx][rows, :]
            if has_pre and idx == 0:
                a = _dot(a.astype(BF16), pre_ref[...])
            part = _dot(a.astype(BF16), wo_refs[idx][...])
            f = part if f is None else f + part
        fs.append(f)
    x1s = [_layer_norm(DN_ALPHA * x_ref[rows, :] + f, g1_ref[...], b1_ref[...]) for rows, f in zip(blocks, fs)]
    x1bs = [x1.astype(BF16) for x1 in x1s]
    accs = [None] * len(blocks)
    for c in range(FFN_HIDDEN // FFN_HC):
        cols = slice(c * FFN_HC, (c + 1) * FFN_HC)
        for i, x1b in enumerate(x1bs):
            h = jnp.maximum(_dot(x1b, w1_ref[:, cols]), 0.0)
            part = _dot((h * h).astype(BF16), w2_ref[cols, :])
            accs[i] = part if accs[i] is None else accs[i] + part
    for rows, x1, acc in zip(blocks, x1s, accs):
        out_ref[rows, :] = _layer_norm(DN_ALPHA * x1 + acc, g2_ref[...], b2_ref[...])


def _mix_ffn(x2d, acts, pre_w, wos, ln1_g, ln1_b, ln2_g, ln2_b, w1, w2, name):
    n_tok = x2d.shape[0]
    tm = min(FFN_TM, n_tok)
    row = lambda w: pl.BlockSpec((tm, w), lambda i: (i, 0))
    has_pre = pre_w is not None
    in_specs = [row(D_MODEL)] + [row(a.shape[1]) for a in acts]
    args = [x2d] + list(acts)
    if has_pre:
        in_specs.append(_const_spec(pre_w.shape))
        args.append(pre_w)
    in_specs += [_const_spec(w.shape) for w in wos]
    args += list(wos)
    in_specs += [_const_spec((1, D_MODEL))] * 4 + [_const_spec(w1.shape), _const_spec(w2.shape)]
    args += [ln1_g, ln1_b, ln2_g, ln2_b, w1, w2]
    return pl.pallas_call(
        functools.partial(_mix_ffn_kernel, n_act=len(acts), has_pre=has_pre),
        grid=(n_tok // tm,),
        in_specs=in_specs,
        out_specs=row(D_MODEL),
        out_shape=jax.ShapeDtypeStruct((n_tok, D_MODEL), F32),
        compiler_params=_cparams(("arbitrary",)),
        name=name,
    )(*args)


def _mix_ffn_sample_kernel(*refs, n_act, has_pre):
    x_ref = refs[0]
    act_refs = refs[1:1 + n_act]
    k = 1 + n_act
    pre_ref = refs[k] if has_pre else None
    k += int(has_pre)
    wo_refs = refs[k:k + n_act]
    k += n_act
    g1_ref, b1_ref, g2_ref, b2_ref, w1_ref, w2_ref, out_ref, w1b_ref, w2b_ref, x1_s, acc_s = refs[k:k + 11]
    c = pl.program_id(0)

    @pl.when(c == 0)
    def _():
        f = None
        for idx in range(n_act):
            a = act_refs[idx][...]
            if has_pre and idx == 0:
                a = _dot(a.astype(BF16), pre_ref[...])
            part = _dot(a.astype(BF16), wo_refs[idx][...])
            f = part if f is None else f + part
        x1_s[...] = _layer_norm(DN_ALPHA * x_ref[...] + f, g1_ref[...], b1_ref[...])
        acc_s[...] = jnp.zeros(acc_s.shape, F32)

    w1c = w1_ref[...].astype(BF16)
    w2c = w2_ref[...].astype(BF16)
    w1b_ref[...] = w1c
    w2b_ref[...] = w2c
    h = jnp.maximum(_dot(x1_s[...].astype(BF16), w1c), 0.0)
    acc_s[...] += _dot((h * h).astype(BF16), w2c)

    @pl.when(c == pl.num_programs(0) - 1)
    def _():
        out_ref[...] = _layer_norm(DN_ALPHA * x1_s[...] + acc_s[...], g2_ref[...], b2_ref[...])


def _mix_ffn_sample(x2d, acts, pre_w, wos, ln1_g, ln1_b, ln2_g, ln2_b, w1_all, w2_all, layer, name):
    n_tok = x2d.shape[0]
    has_pre = pre_w is not None
    full = lambda a: pl.BlockSpec(a.shape, lambda c: (0,) * a.ndim)
    args = [x2d] + list(acts) + ([pre_w] if has_pre else []) + list(wos) + [ln1_g, ln1_b, ln2_g, ln2_b]
    in_specs = [full(a) for a in args]
    in_specs += [pl.BlockSpec((None, D_MODEL, FFN_HC), lambda c: (layer, 0, c)),
                 pl.BlockSpec((None, FFN_HC, D_MODEL), lambda c: (layer, c, 0))]
    args += [w1_all, w2_all]
    return pl.pallas_call(
        functools.partial(_mix_ffn_sample_kernel, n_act=len(acts), has_pre=has_pre),
        grid=(FFN_HIDDEN // FFN_HC,),
        in_specs=in_specs,
        out_specs=[pl.BlockSpec((n_tok, D_MODEL), lambda c: (0, 0)),
                   pl.BlockSpec((D_MODEL, FFN_HC), lambda c: (0, c)),
                   pl.BlockSpec((FFN_HC, D_MODEL), lambda c: (c, 0))],
        out_shape=[jax.ShapeDtypeStruct((n_tok, D_MODEL), F32),
                   jax.ShapeDtypeStruct((D_MODEL, FFN_HIDDEN), BF16),
                   jax.ShapeDtypeStruct((FFN_HIDDEN, D_MODEL), BF16)],
        scratch_shapes=[pltpu.VMEM((n_tok, D_MODEL), F32), pltpu.VMEM((n_tok, D_MODEL), F32)],
        compiler_params=_cparams(("arbitrary",)),
        name=name,
    )(*args)


def _even_inproj_sample_kernel(x_ref, w_ref, kvn_ref, gmg_ref, gmb_ref, ws0_ref, bs0_ref,
                               cos_ref, sina_ref, sinb_ref, qn_ref, wqn_ref, wqp_ref, wuk_ref,
                               ckv_ref, kpe_ref, gate_ref, vn_ref, qlat_ref, qpe_ref):
    z = _dot(x_ref[...].astype(BF16), w_ref[...])
    o_c, o_u, o_v, o_k = Q_RANK, Q_RANK + KV_RANK, Q_RANK + KV_RANK + GM_WIDTH, Q_RANK + KV_RANK + 2 * GM_WIDTH
    cos_t, sin_a, sin_b = cos_ref[...], sina_ref[...], sinb_ref[...]
    c_n = _rms_norm(z[:, o_c:o_u], kvn_ref[...])
    ckv_ref[...] = c_n
    kp = _rope128(z[:, o_k:o_k + LANES], cos_t, sin_a, sin_b)
    kpe_ref[...] = kp[:, :ROPE_DIM]
    gu = jax.nn.gelu(z[:, o_u:o_v])
    v_n = _layer_norm(jax.nn.gelu(z[:, o_v:o_k]), gmg_ref[...], gmb_ref[...])
    vn_ref[...] = v_n
    mixed = ws0_ref[...].astype(F32) * v_n.astype(BF16).astype(F32) + bs0_ref[...]
    gate_ref[...] = (gu * mixed).astype(BF16)
    lat, pes = _queries(z[:, :o_c], qn_ref[...], wqn_ref[...], wqp_ref[...], wuk_ref,
                        cos_t, sin_a, sin_b)
    qlat_ref[...] = lat
    qpe_ref[...] = jnp.concatenate(pes, axis=1)


def _even_inproj_sample(xs, w_in_p, kv_norm, gm_g, gm_b, ws0, bs0, cos_t, sin_a, sin_b,
                        qn_g, wq_nope, wq_pe, w_uk):
    n = xs.shape[0]
    args = (xs, w_in_p, kv_norm, gm_g, gm_b, ws0, bs0, cos_t, sin_a, sin_b, qn_g, wq_nope, wq_pe, w_uk)
    full = lambda shape: pl.BlockSpec(shape, lambda i: (0,) * len(shape))
    widths = [(KV_RANK, F32), (ROPE_DIM, F32), (GM_WIDTH, BF16), (GM_WIDTH, F32),
              (MLA_HEADS * LANES, F32), (MLA_HEADS * LANES, F32)]
    return pl.pallas_call(
        _even_inproj_sample_kernel,
        grid=(1,),
        in_specs=[full(a.shape) for a in args],
        out_specs=[full((n, w)) for w, _ in widths],
        out_shape=[jax.ShapeDtypeStruct((n, w), dt) for w, dt in widths],
        compiler_params=_cparams(("arbitrary",)),
        name="even_inproj_sample",
    )(*args)


def _decode_attn_kernel(pt_ref, qlat_ref, qpe_ref, ckvs_ref, kpes_ref, ckv_hbm, kpe_hbm, out_ref,
                        ckv_buf, kpe_buf, m_buf, sem, *, n_pg, n_groups):
    b = pl.program_id(0)
    n_b = pl.num_programs(0)
    assert n_groups % DEC_SLOTS == 0 and DEC_AHEAD + 2 <= DEC_SLOTS

    def group_copies(seq, grp):
        slot = grp % DEC_SLOTS
        cps = []
        for k in range(n_pg):
            page = pt_ref[seq, grp * n_pg + k]
            keys = pl.ds(k * PAGE_SIZE, PAGE_SIZE)
            cps.append(pltpu.make_async_copy(ckv_hbm.at[page], ckv_buf.at[slot, keys, :], sem.at[0, slot]))
            cps.append(pltpu.make_async_copy(kpe_hbm.at[page], kpe_buf.at[slot, :, keys], sem.at[1, slot]))
        return cps

    @pl.when(b == 0)
    def _():
        for g in range(DEC_AHEAD):
            for cp in group_copies(0, g):
                cp.start()

    heads, span = MLA_HEADS, DEC_SPAN_PAGES * PAGE_SIZE
    half_span = span // 2
    n_span = n_pg // DEC_SPAN_PAGES
    qlat = qlat_ref[...]
    zero = jnp.zeros_like(qlat)
    q2 = jnp.concatenate([jnp.concatenate([qlat, zero], axis=1),
                          jnp.concatenate([zero, qlat], axis=1)], axis=0).astype(BF16)
    qpe = qpe_ref[...][:, :ROPE_DIM].astype(BF16)

    def scores(slot):
        s_pe = _dot(qpe, kpe_buf[slot].astype(BF16))
        pieces = []
        for j in range(n_span):
            m_buf[slot, j] = jnp.concatenate(
                [ckv_buf[slot, pl.ds(j * span, half_span), :],
                 ckv_buf[slot, pl.ds(j * span + half_span, half_span), :]], axis=1).astype(BF16)
            s2 = _dot_nt(q2, m_buf[slot, j])
            pieces += [s2[:heads] + s_pe[:, j * span:j * span + half_span],
                       s2[heads:] + s_pe[:, j * span + half_span:(j + 1) * span]]
        return jnp.concatenate(pieces, axis=1) * ATTN_SCALE

    def values(p, grp):
        slot = grp % DEC_SLOTS
        o2 = None
        for j in range(n_span):
            lhs = jnp.concatenate([p[:, j * span:j * span + half_span],
                                   p[:, j * span + half_span:(j + 1) * span]], axis=0).astype(BF16)
            part = _dot(lhs, m_buf[slot, j])
            o2 = part if o2 is None else o2 + part
        return o2[:heads, :KV_RANK] + o2[heads:, KV_RANK:]

    state = dict(m_run=None, m=None, l=None, acc=None)
    raw = {}
    probs = {}

    def softmax(k):
        s = raw.pop(k)
        m_cur = jnp.max(s, axis=-1, keepdims=True)
        m_new = m_cur if state["m_run"] is None else jnp.maximum(state["m_run"], m_cur)
        state["m_run"] = m_new
        probs[k] = (jnp.exp(s - m_new), m_new)

    def fold(k):
        p, m_k = probs.pop(k)
        pv = values(p, k)
        p_sum = jnp.sum(p, axis=-1, keepdims=True)
        if state["m"] is None:
            state["l"], state["acc"] = p_sum, pv
        else:
            alpha = jnp.exp(state["m"] - m_k)
            state["l"], state["acc"] = alpha * state["l"] + p_sum, alpha * state["acc"] + pv
        state["m"] = m_k

    for g in range(n_groups):
        if g + DEC_AHEAD < n_groups:
            for cp in group_copies(b, g + DEC_AHEAD):
                cp.start()
        else:
            @pl.when(b + 1 < n_b)
            def _():
                for cp in group_copies(b + 1, g + DEC_AHEAD - n_groups):
                    cp.start()
        for cp in group_copies(b, g):
            cp.wait()
        raw[g] = scores(g % DEC_SLOTS)
        if g >= 1:
            softmax(g - 1)
        if g >= 2:
            fold(g - 2)
    softmax(n_groups - 1)
    fold(n_groups - 2)
    fold(n_groups - 1)
    m, l, acc = state["m"], state["l"], state["acc"]

    r = lambda a: a.astype(BF16).astype(F32)
    kv = r(ckvs_ref[...])
    s_self = (jnp.sum(r(qlat_ref[...]) * kv, axis=-1, keepdims=True)
              + jnp.sum(r(qpe_ref[...]) * r(kpes_ref[...]), axis=-1, keepdims=True)) * ATTN_SCALE
    m_n = jnp.maximum(m, s_self)
    a = jnp.exp(m - m_n)
    p_self = jnp.exp(s_self - m_n)
    out_ref[...] = (a * acc + r(p_self) * kv) / (a * l + p_self)


def _decode_attn(page_table, qlat2, qpe2, ckv_s3, kpe_s3, cache_ckv_e, cache_kpe_t):
    dec_batch, n_pages = page_table.shape
    n_pg = DEC_PG
    qspec = pl.BlockSpec((MLA_HEADS, LANES), lambda b, pt: (b, 0))
    self_spec = pl.BlockSpec((None, 1, LANES), lambda b, pt: (b, 0, 0))
    hbm = pl.BlockSpec(memory_space=pl.ANY)
    grid_spec = pltpu.PrefetchScalarGridSpec(
        num_scalar_prefetch=1,
        grid=(dec_batch,),
        in_specs=[qspec, qspec, self_spec, self_spec, hbm, hbm],
        out_specs=qspec,
        scratch_shapes=[pltpu.VMEM((DEC_SLOTS, n_pg * PAGE_SIZE, KV_RANK), F32),
                        pltpu.VMEM((DEC_SLOTS, ROPE_DIM, n_pg * PAGE_SIZE), F32),
                        pltpu.VMEM((DEC_SLOTS, n_pg // DEC_SPAN_PAGES, DEC_SPAN_PAGES // 2 * PAGE_SIZE, 2 * KV_RANK),
                                   BF16),
                        pltpu.SemaphoreType.DMA((2, DEC_SLOTS))],
    )
    return pl.pallas_call(
        functools.partial(_decode_attn_kernel, n_pg=n_pg, n_groups=n_pages // n_pg),
        grid_spec=grid_spec,
        out_shape=jax.ShapeDtypeStruct((dec_batch * MLA_HEADS, KV_RANK), F32),
        compiler_params=_cparams(("arbitrary",)),
        name="decode_attn",
    )(page_table, qlat2, qpe2, ckv_s3, kpe_s3, cache_ckv_e, cache_kpe_t)


def _s5_readout(hcat_blocks, u, cw_ref, d_ref, wglu_ref):
    y = jnp.concatenate([_dot(hcat_blocks[k], cw_ref[k]) for k in range(S5_NB)], axis=1)
    y = jax.nn.gelu(y + d_ref[...] * u)
    return y * jax.nn.sigmoid(_dot(y.astype(BF16), wglu_ref[...]))


def _s5_prompt_kernel(x_ref, win_ref, bw2_ref, a2_ref, cw2_ref, d_ref, wglu_ref,
                      yg_ref, hre_ref, him_ref, u_s, y_s, st_s, h_s):
    n_b, seg, _ = x_ref.shape
    half = S5_SLABS // 2
    sub = S5_SUB
    n_sub = seg // sub
    rows_sub = sub * n_b
    n_pair = sub // 2
    rows_pair = n_pair * n_b
    reg = rows_pair + n_b
    q_slabs = S5_WIDTH // LANES

    @pl.when(pl.program_id(0) == 0)
    def _():
        h_s[...] = jnp.zeros(h_s.shape, F32)

    def steps_of(slab_ref, s, q, parity):
        return jnp.concatenate([slab_ref[s, q, pl.ds((2 * j + parity) * n_b, n_b), :] for j in range(n_pair)],
                               axis=0)

    def project(s):
        xs = jnp.concatenate([x_ref[b, s * sub:(s + 1) * sub, :] for b in range(n_b)], axis=0)
        u = _dot(xs.astype(BF16), win_ref[...])
        for b in range(n_b):
            for q in range(q_slabs):
                u_s[s, q, pl.ds(b, sub, stride=n_b), :] = u[b * sub:(b + 1) * sub, q * LANES:(q + 1) * LANES]
        rows = slice(s * reg + n_b, (s + 1) * reg)
        for k in range(S5_NB):
            lhs = jnp.concatenate([steps_of(u_s, s, k, 0), steps_of(u_s, s, k, 1)], axis=1).astype(BF16)
            r = _dot(lhs, bw2_ref[k])
            for q in range(S5_BLK_SLABS):
                st_s[S5_BLK_SLABS * k + q, rows, :] = r[:, q * LANES:(q + 1) * LANES]
                st_s[half + S5_BLK_SLABS * k + q, rows, :] = r[:, (S5_BLK_SLABS + q) * LANES:
                                                                (S5_BLK_SLABS + q + 1) * LANES]

    def recur(s, h):
        st_s[:, s * reg:s * reg + n_b, :] = h
        for j in range(n_pair):
            rows = slice(s * reg + (j + 1) * n_b, s * reg + (j + 2) * n_b)
            inc = st_s[:, rows, :]
            ar, ai = a2_ref[:half], a2_ref[half:]
            hr, hi = h[:half], h[half:]
            h = jnp.concatenate([ar * hr - ai * hi + inc[:half], ar * hi + ai * hr + inc[half:]], axis=0)
            st_s[:, rows, :] = h
        return h

    def read_out(s):
        rows = slice(s * reg, s * reg + rows_pair)
        ys = []
        for k in range(S5_NB):
            lhs = jnp.concatenate(
                [st_s[S5_BLK_SLABS * k + q, rows, :] for q in range(S5_BLK_SLABS)]
                + [st_s[half + S5_BLK_SLABS * k + q, rows, :] for q in range(S5_BLK_SLABS)]
                + [steps_of(u_s, s, k, 0), steps_of(u_s, s, k, 1)], axis=1).astype(BF16)
            y2 = _dot(lhs, cw2_ref[k])
            ys.append(jnp.concatenate(
                [y2[j * n_b:(j + 1) * n_b, par * LANES:(par + 1) * LANES] for j in range(n_pair) for par in range(2)],
                axis=0))
        u_t = jnp.concatenate([u_s[s, q] for q in range(q_slabs)], axis=1)
        y = jax.nn.gelu(jnp.concatenate(ys, axis=1) + d_ref[...] * u_t)
        y = y * jax.nn.sigmoid(_dot(y.astype(BF16), wglu_ref[...]))
        for q in range(q_slabs):
            y_s[s, q] = y[:, q * LANES:(q + 1) * LANES]
        for b in range(n_b):
            yg_ref[b, s * sub:(s + 1) * sub, :] = jnp.concatenate(
                [y_s[s, q, pl.ds(b, sub, stride=n_b), :] for q in range(q_slabs)], axis=1).astype(BF16)

    h = h_s[...]
    project(0)
    for s in range(n_sub):
        if s + 1 < n_sub:
            project(s + 1)
        h = recur(s, h)
        if s >= 1:
            read_out(s - 1)
    read_out(n_sub - 1)
    h_s[...] = h
    hre_ref[...] = jnp.concatenate([h[s] for s in range(half)], axis=1)
    him_ref[...] = jnp.concatenate([h[half + s] for s in range(half)], axis=1)


def _s5_prompt(x3, w_in, bw2, a2_b, cw2, d_row, w_glu):
    batch, seq, _ = x3.shape
    seg = S5_L
    n_sub = seg // S5_SUB
    slab_rows = (n_sub, S5_WIDTH // LANES, S5_SUB * batch, LANES)
    return pl.pallas_call(
        _s5_prompt_kernel,
        grid=(seq // seg,),
        in_specs=[pl.BlockSpec((batch, seg, D_MODEL), lambda c: (0, c, 0)),
                  _const_spec(w_in.shape), _const_spec(bw2.shape), _const_spec(a2_b.shape),
                  _const_spec(cw2.shape), _const_spec(d_row.shape), _const_spec(w_glu.shape)],
        out_specs=[pl.BlockSpec((batch, seg, S5_WIDTH), lambda c: (0, c, 0)),
                   pl.BlockSpec((batch, S5_STATES), lambda c: (0, 0)),
                   pl.BlockSpec((batch, S5_STATES), lambda c: (0, 0))],
        out_shape=[jax.ShapeDtypeStruct((batch, seq, S5_WIDTH), BF16),
                   jax.ShapeDtypeStruct((batch, S5_STATES), F32),
                   jax.ShapeDtypeStruct((batch, S5_STATES), F32)],
        scratch_shapes=[pltpu.VMEM(slab_rows, F32), pltpu.VMEM(slab_rows, F32),
                        pltpu.VMEM((S5_SLABS, n_sub * (S5_SUB // 2 + 1) * batch, LANES), F32),
                        pltpu.VMEM((S5_SLABS, batch, LANES), F32)],
        compiler_params=_cparams(("arbitrary",)),
        name="s5_prompt",
    )(x3, w_in, bw2, a2_b, cw2, d_row, w_glu)


def _s5_sample_kernel(x_ref, h0r_ref, h0i_ref, win_ref, bw_ref, ar_ref, ai_ref, cw_ref, d_ref, wglu_ref,
                      yg_ref, hre_ref, him_ref):
    u = _dot(x_ref[...].astype(BF16), win_ref[...])
    ub = u.astype(BF16)
    bu = [_dot(ub[:, k * LANES:(k + 1) * LANES], bw_ref[k]) for k in range(S5_NB)]
    w = S5_KB * S5_STATE
    bu_r = jnp.concatenate([r[:, :w] for r in bu], axis=1)
    bu_i = jnp.concatenate([r[:, w:] for r in bu], axis=1)
    ar, ai, h0r, h0i = ar_ref[...], ai_ref[...], h0r_ref[...], h0i_ref[...]
    hr = bu_r + (ar * h0r - ai * h0i)
    hi = bu_i + (ar * h0i + ai * h0r)
    hre_ref[...] = hr
    him_ref[...] = hi
    blocks = [jnp.concatenate([hr[:, k * w:(k + 1) * w], hi[:, k * w:(k + 1) * w]], axis=1).astype(BF16)
              for k in range(S5_NB)]
    yg_ref[...] = _s5_readout(blocks, u, cw_ref, d_ref, wglu_ref).astype(BF16)


def _s5_sample(xs, h0r, h0i, w_in, bw, a_r, a_i, cw, d_row, w_glu):
    n = xs.shape[0]
    args = (xs, h0r, h0i, w_in, bw, a_r, a_i, cw, d_row, w_glu)
    full = lambda shape: pl.BlockSpec(shape, lambda i: (0,) * len(shape))
    return pl.pallas_call(
        _s5_sample_kernel,
        grid=(1,),
        in_specs=[full(a.shape) for a in args],
        out_specs=[full((n, S5_WIDTH)), full((n, S5_STATES)), full((n, S5_STATES))],
        out_shape=[jax.ShapeDtypeStruct((n, S5_WIDTH), BF16),
                   jax.ShapeDtypeStruct((n, S5_STATES), F32),
                   jax.ShapeDtypeStruct((n, S5_STATES), F32)],
        compiler_params=_cparams(("arbitrary",)),
        name="s5_sample",
    )(*args)


def _rope_tables(pos):
    freqs = ROPE_THETA ** (-jnp.arange(ROPE_HALF, dtype=F32) / ROPE_HALF)
    ang = pos.astype(F32)[:, None] * freqs
    cos, sin = jnp.cos(ang), jnp.sin(ang)
    zero = jnp.zeros_like(cos)
    pad = jnp.zeros((pos.shape[0], LANES - ROPE_DIM), F32)
    cos_t = jnp.concatenate([cos, cos, pad], axis=1)
    sin_a = jnp.concatenate([zero, sin, pad], axis=1)
    sin_b = jnp.concatenate([-sin, zero, pad], axis=1)
    return cos_t, sin_a, sin_b


def _block_diag(blocks):
    *lead, n, r, c = blocks.shape
    eye = jnp.eye(n, dtype=blocks.dtype)
    return jnp.einsum('...nrc,nm->...nrmc', blocks, eye).reshape(*lead, n * r, n * c)


def _even_params(w_in, q_norm, w_q_b, kv_norm, w_kv_b, gm_g, gm_b, gm_w_s, gm_b_s, w_out):
    i1, i2, i3, i4 = Q_RANK, Q_RANK + KV_RANK, Q_RANK + KV_RANK + ROPE_DIM, Q_RANK + KV_RANK + ROPE_DIM + GM_WIDTH
    w_in_p = jnp.concatenate([w_in[:, :i2], w_in[:, i3:], w_in[:, i2:i3],
                              jnp.zeros((D_MODEL, LANES - ROPE_DIM), F32)], axis=1).astype(BF16)
    wq_nope = w_q_b[:, :, :NOPE_DIM].reshape(Q_RANK, MLA_HEADS * NOPE_DIM).astype(BF16)
    pe = w_q_b[:, :, NOPE_DIM:]
    pe_sw = jnp.concatenate([pe[:, :, ROPE_HALF:], pe[:, :, :ROPE_HALF]], axis=2)
    lane_pad = jnp.zeros((Q_RANK, MLA_HEADS, LANES - ROPE_DIM), F32)
    wq_pe = jnp.concatenate(
        [jnp.concatenate([w, lane_pad], axis=2).reshape(Q_RANK, MLA_HEADS * LANES) for w in (pe, pe_sw)],
        axis=1).astype(BF16)
    uk = jnp.transpose(w_kv_b[:, :, :NOPE_DIM], (1, 2, 0))
    uv = jnp.transpose(w_kv_b[:, :, NOPE_DIM:], (1, 0, 2))
    pairs = MLA_HEADS // 2
    w_uk = _block_diag(uk.reshape(pairs, 2, NOPE_DIM, KV_RANK)).astype(BF16)
    w_uv_p = _block_diag(uv.reshape(pairs, 2, KV_RANK, V_DIM)).astype(BF16)
    w_uv = _block_diag(uv).astype(BF16)
    causal = jnp.tril(jnp.ones((GM_CHUNK, GM_CHUNK), dtype=bool))
    ws_m = jnp.where(causal[None], gm_w_s, 0).astype(BF16)
    bs_full = jnp.repeat(gm_b_s.T, GM_HEAD_DIM, axis=1)
    ws0 = jnp.repeat(gm_w_s[:, 0, 0], GM_HEAD_DIM)[None, :].astype(BF16)
    bs0 = bs_full[:1]
    n_attn = MLA_HEADS * V_DIM
    return dict(w_in_p=w_in_p, qn=q_norm[None, :], wq_nope=wq_nope, wq_pe=wq_pe, w_uk=w_uk, w_uv=w_uv, w_uv_p=w_uv_p,
                kvn=kv_norm[None, :], gm_g=gm_g[None, :], gm_b=gm_b[None, :], ws_m=ws_m, bs_full=bs_full,
                ws0=ws0, bs0=bs0, wo_attn=w_out[:n_attn].astype(BF16), wo_gate=w_out[n_attn:].astype(BF16))


def _s5_params(w_in, a_re, a_im, b_re, b_im, c_re, c_im, d, log_dt, w_glu, w_out, batch):
    dt = jnp.exp(log_dt)[:, None]
    ld_r, ld_i = a_re * dt, a_im * dt
    mag = jnp.exp(ld_r)
    ab_r, ab_i = mag * jnp.cos(ld_i), mag * jnp.sin(ld_i)
    den = a_re * a_re + a_im * a_im
    cr = ((ab_r - 1.0) * a_re + ab_i * a_im) / den
    ci = (ab_i * a_re - (ab_r - 1.0) * a_im) / den
    bb_r = cr[..., None] * b_re - ci[..., None] * b_im
    bb_i = cr[..., None] * b_im + ci[..., None] * b_re

    a_r, a_i = ab_r.reshape(1, S5_STATES), ab_i.reshape(1, S5_STATES)
    half = S5_SLABS // 2
    cmul = lambda xr, xi, yr, yi: (xr * yr - xi * yi, xr * yi + xi * yr)
    a2_r, a2_i = cmul(ab_r, ab_i, ab_r, ab_i)
    abb_r, abb_i = cmul(ab_r[..., None], ab_i[..., None], bb_r, bb_i)
    ca_r, ca_i = cmul(c_re, c_im, ab_r[:, None, :], ab_i[:, None, :])
    ca2_r, ca2_i = cmul(c_re, c_im, a2_r[:, None, :], a2_i[:, None, :])
    cc = (jnp.einsum('kgcp,gpd->kgcd', jnp.stack([c_re, ca_r]), bb_r)
          - jnp.einsum('kgcp,gpd->kgcd', jnp.stack([c_im, ca_i]), bb_i))

    def blocks(mats, rows, cols):
        m = jnp.swapaxes(jnp.stack(mats), 2, 3).astype(BF16)
        return _block_diag(m.reshape(len(mats), S5_NB, S5_KB, rows, cols))

    b_r, b_i, ab_br, ab_bi = blocks([bb_r, bb_i, abb_r, abb_i], S5_GROUP_DIM, S5_STATE)
    c_r, c_i, ca_br, ca_bi, ca2_br, ca2_bi = blocks([c_re, -c_im, ca_r, -ca_i, ca2_r, -ca2_i],
                                                    S5_STATE, S5_GROUP_DIM)
    cb_b, cab_b = blocks([cc[0], cc[1]], S5_GROUP_DIM, S5_GROUP_DIM)
    cat = jnp.concatenate
    bw = cat([b_r, b_i], axis=2)
    cw = cat([c_r, c_i], axis=1)
    bw2 = cat([cat([ab_br, ab_bi], axis=2), cat([b_r, b_i], axis=2)], axis=1)
    cw2 = cat([cat([ca_br, ca2_br], axis=2),
               cat([ca_bi, ca2_bi], axis=2),
               cat([cb_b, cab_b], axis=2),
               cat([jnp.zeros_like(cb_b), cb_b], axis=2)],
              axis=1)
    a2_b = jnp.concatenate([jnp.broadcast_to(a2_r.reshape(half, 1, LANES), (half, batch, LANES)),
                            jnp.broadcast_to(a2_i.reshape(half, 1, LANES), (half, batch, LANES))], axis=0)
    return dict(w_in=w_in.astype(BF16), bw=bw, cw=cw, a_r=a_r, a_i=a_i, bw2=bw2, cw2=cw2, a2_b=a2_b,
                d=d.reshape(1, S5_WIDTH), w_glu=w_glu.astype(BF16), w_out=w_out.astype(BF16))


def kernel(x_prompt, x_sample, cache_ckv, cache_kpe, state_s5_re, state_s5_im, page_table, w_in_even, mla_q_norm, mla_w_q_b, mla_kv_norm, mla_w_kv_b, gm_norm_g, gm_norm_b, gm_w_s, gm_b_s, w_out_even, w_in_odd, s5_a_re, s5_a_im, s5_b_re, s5_b_im, s5_c_re, s5_c_im, s5_d, s5_log_dt, s5_w_glu, w_out_odd, ln_mix_g, ln_mix_b, ln_ffn_g, ln_ffn_b, ffn_w1, ffn_w2):
    batch, seq, _ = x_prompt.shape
    dec_batch, dec_seq, _ = x_sample.shape
    assert dec_seq == 1 and seq % INPROJ_TM == 0 and seq % ATTN_TQ == 0 and seq % S5_L == 0
    assert (seq // ATTN_TQ) % 2 == 0
    assert (batch * seq) % FFN_TM == 0 and page_table.shape[1] % DEC_PG == 0
    assert batch == SUBLANES

    xp = x_prompt.reshape(batch * seq, D_MODEL)
    xs = x_sample.reshape(dec_batch, D_MODEL)
    tabs_p = _rope_tables(jnp.arange(seq, dtype=jnp.int32))
    tabs_s = _rope_tables(PAST_LEN + jnp.arange(dec_seq, dtype=jnp.int32))

    outs = {k: [] for k in ("ckv_p", "kpe_p", "ckv_s", "kpe_s", "gmv_p", "gmv_s",
                            "s5re_p", "s5im_p", "s5re_s", "s5im_s")}
    for layer in range(DEPTH):
        ln = (ln_mix_g[layer][None, :], ln_mix_b[layer][None, :],
              ln_ffn_g[layer][None, :], ln_ffn_b[layer][None, :])
        if layer % 2 == 0:
            e = layer // 2
            p = _even_params(w_in_even[e], mla_q_norm[e], mla_w_q_b[e], mla_kv_norm[e], mla_w_kv_b[e],
                             gm_norm_g[e], gm_norm_b[e], gm_w_s[e], gm_b_s[e], w_out_even[e])
            q_a, ckv, kpe, kcat, gate, gmv = _even_inproj(
                xp, p["w_in_p"], p["kvn"], p["gm_g"], p["gm_b"], p["ws_m"], p["bs_full"], *tabs_p, batch, seq)
            attn = _mla_attn(q_a, kcat, p["qn"], p["wq_nope"], p["wq_pe"], p["w_uk"], p["w_uv_p"], *tabs_p,
                             batch, seq)
            outs["ckv_p"].append(ckv.reshape(batch, seq, KV_RANK))
            outs["kpe_p"].append(kpe.reshape(batch, seq, ROPE_DIM))
            outs["gmv_p"].append(gmv)
            ckv_s, kpe_s, gate_s, vn_s, qlat_s, qpe_s = _even_inproj_sample(
                xs, p["w_in_p"], p["kvn"], p["gm_g"], p["gm_b"], p["ws0"], p["bs0"], *tabs_s,
                p["qn"], p["wq_nope"], p["wq_pe"], p["w_uk"])
            kpe_pad = jnp.concatenate([kpe_s, jnp.zeros((dec_batch, LANES - ROPE_DIM), F32)], axis=1)
            o_lat = _decode_attn(page_table,
                                 qlat_s.reshape(dec_batch * MLA_HEADS, LANES),
                                 qpe_s.reshape(dec_batch * MLA_HEADS, LANES),
                                 ckv_s.reshape(dec_batch, 1, KV_RANK), kpe_pad.reshape(dec_batch, 1, LANES),
                                 cache_ckv[e], jnp.swapaxes(cache_kpe[e], 1, 2))
            xs, w1_b, w2_b = _mix_ffn_sample(
                xs, [o_lat.reshape(dec_batch, MLA_HEADS * KV_RANK), gate_s], p["w_uv"],
                [p["wo_attn"], p["wo_gate"]], *ln, ffn_w1, ffn_w2, layer, name="even_ffn_sample")
            xp = _mix_ffn(xp, [attn, gate], None, [p["wo_attn"], p["wo_gate"]], *ln, w1_b, w2_b,
                          name="even_ffn_prompt")
            outs["ckv_s"].append(ckv_s.reshape(dec_batch, 1, KV_RANK))
            outs["kpe_s"].append(kpe_s.reshape(dec_batch, 1, ROPE_DIM))
            outs["gmv_s"].append(vn_s.reshape(dec_batch, 1, GM_WIDTH))
        else:
            o = layer // 2
            p = _s5_params(w_in_odd[o], s5_a_re[o], s5_a_im[o], s5_b_re[o], s5_b_im[o], s5_c_re[o],
                           s5_c_im[o], s5_d[o], s5_log_dt[o], s5_w_glu[o], w_out_odd[o], batch)
            yg, hre, him = _s5_prompt(xp.reshape(batch, seq, D_MODEL), p["w_in"], p["bw2"], p["a2_b"],
                                      p["cw2"], p["d"], p["w_glu"])
            outs["s5re_p"].append(hre.reshape(batch, S5_GROUPS, S5_STATE))
            outs["s5im_p"].append(him.reshape(batch, S5_GROUPS, S5_STATE))
            yg_s, hre_s, him_s = _s5_sample(
                xs, state_s5_re[o].reshape(dec_batch, S5_STATES), state_s5_im[o].reshape(dec_batch, S5_STATES),
                p["w_in"], p["bw"], p["a_r"], p["a_i"], p["cw"], p["d"], p["w_glu"])
            xs, w1_b, w2_b = _mix_ffn_sample(xs, [yg_s], None, [p["w_out"]], *ln, ffn_w1, ffn_w2, layer,
                                             name="odd_ffn_sample")
            xp = _mix_ffn(xp, [yg.reshape(batch * seq, S5_WIDTH)], None, [p["w_out"]], *ln, w1_b, w2_b,
                          name="odd_ffn_prompt")
            outs["s5re_s"].append(hre_s.reshape(dec_batch, S5_GROUPS, S5_STATE))
            outs["s5im_s"].append(him_s.reshape(dec_batch, S5_GROUPS, S5_STATE))

    st = jnp.stack
    return (xp.reshape(batch, seq, D_MODEL), xs.reshape(dec_batch, dec_seq, D_MODEL),
            st(outs["ckv_p"]), st(outs["kpe_p"]), st(outs["ckv_s"]), st(outs["kpe_s"]),
            st(outs["gmv_p"]), st(outs["gmv_s"]),
            st(outs["s5re_p"]), st(outs["s5im_p"]), st(outs["s5re_s"]), st(outs["s5im_s"]))
```

```python
import functools
import math

import jax
import jax.numpy as jnp
from jax import lax
from jax.experimental import pallas as pl
from jax.experimental.pallas import tpu as pltpu

F32 = jnp.float32
BF16 = jnp.bfloat16

D_MODEL = 1024
DEPTH = 2
PAST_LEN = 16384
PAGE_SIZE = 128
MLA_HEADS = 8
NOPE_DIM = 64
ROPE_DIM = 32
ROPE_HALF = ROPE_DIM // 2
V_DIM = 64
Q_RANK = 256
KV_RANK = 128
ROPE_THETA = 10000.0
ATTN_SCALE = 1.0 / math.sqrt(NOPE_DIM + ROPE_DIM)
ATTN_SCALE_LOG2 = ATTN_SCALE * math.log2(math.e)
GM_HEADS = 8
GM_HEAD_DIM = 64
GM_WIDTH = GM_HEADS * GM_HEAD_DIM
GM_CHUNK = 128
S5_GROUP_DIM = 16
S5_GROUPS = 32
S5_STATE = 64
S5_WIDTH = S5_GROUPS * S5_GROUP_DIM
S5_STATES = S5_GROUPS * S5_STATE
FFN_HIDDEN = 4 * D_MODEL
DN_ALPHA = (2 * DEPTH) ** 0.25
LN_EPS = 1e-5
RMS_EPS = 1e-6

LANES = 128
SUBLANES = 8
VMEM_LIMIT_BYTES = 56 * 1024 * 1024

INPROJ_TM = 512
INPROJ_ROW_BLOCK = 256
ATTN_TQ = 256
ATTN_ROW_BLOCK = 256
FFN_TM = 1024
FFN_HC = 1024
FFN_ROW_BLOCK = 256
S5_L = 128
S5_SUB = 32
DEC_PG = 32
DEC_SLOTS = 4
DEC_AHEAD = 2
DEC_SPAN_PAGES = 4
S5_KB = 8
S5_NB = S5_GROUPS // S5_KB
S5_SLABS = 2 * S5_STATES // LANES
S5_BLK_SLABS = S5_KB * S5_STATE // LANES


def _cparams(sem):
    return pltpu.CompilerParams(dimension_semantics=sem, vmem_limit_bytes=VMEM_LIMIT_BYTES)


def _const_spec(shape):
    n = len(shape)
    return pl.BlockSpec(shape, lambda *_: (0,) * n, pipeline_mode=pl.Buffered(1))


def _layer_norm(x, g, b):
    mu = jnp.mean(x, axis=-1, keepdims=True)
    xc = x - mu
    var = jnp.mean(xc * xc, axis=-1, keepdims=True)
    return xc * lax.rsqrt(var + LN_EPS) * g + b


def _rms_norm(x, g):
    return x * lax.rsqrt(jnp.mean(x * x, axis=-1, keepdims=True) + RMS_EPS) * g


def _rope128(x, cos_t, sin_a, sin_b):
    return (x * cos_t + pltpu.roll(x, ROPE_HALF, 1) * sin_a
            + pltpu.roll(x, LANES - ROPE_HALF, 1) * sin_b)


def _dot(a, b):
    return jnp.dot(a, b, preferred_element_type=F32)


def _dot_nt(a, b):
    return lax.dot_general(a, b, (((1,), (1,)), ((), ())), preferred_element_type=F32)


def _even_inproj_kernel(x_ref, w_ref, kvn_ref, gmg_ref, gmb_ref, ws_ref, bs_ref,
                        cos_ref, sina_ref, sinb_ref,
                        qa_ref, ckv_ref, kpe_ref, kcat_ref, gate_ref, gmv_ref):
    tm = x_ref.shape[0]
    o_c, o_u, o_v, o_k = Q_RANK, Q_RANK + KV_RANK, Q_RANK + KV_RANK + GM_WIDTH, Q_RANK + KV_RANK + 2 * GM_WIDTH
    rb = min(INPROJ_ROW_BLOCK, tm)
    blocks = [slice(r0, r0 + rb) for r0 in range(0, tm, rb)]
    zs = [_dot(x_ref[rows, :].astype(BF16), w_ref[...]) for rows in blocks]

    gus, vns = [], []
    for rows, z in zip(blocks, zs):
        qa_ref[rows, :] = z[:, :o_c]
        c_n = _rms_norm(z[:, o_c:o_u], kvn_ref[...])
        ckv_ref[rows, :] = c_n
        kp = _rope128(z[:, o_k:o_k + LANES], cos_ref[rows, :], sina_ref[rows, :], sinb_ref[rows, :])
        kpe_ref[:, rows] = kp.T[:ROPE_DIM, :]
        kcat_ref[rows, :] = jnp.concatenate([c_n, kp], axis=1).astype(BF16)
        gus.append(jax.nn.gelu(z[:, o_u:o_v]))
        vns.append(_layer_norm(jax.nn.gelu(z[:, o_v:o_k]), gmg_ref[...], gmb_ref[...]))

    gmv_ref[0] = vns[-1][rb - GM_CHUNK:, :]

    lane = lax.broadcasted_iota(jnp.int32, (GM_CHUNK, LANES), 1)
    lo = lane < GM_HEAD_DIM
    bs = bs_ref[...]
    for rows, gu, v_n in zip(blocks, gus, vns):
        for ci in range(rb // GM_CHUNK):
            chunk = slice(ci * GM_CHUNK, (ci + 1) * GM_CHUNK)
            pieces = []
            for pr in range(GM_WIDTH // LANES):
                r = v_n[chunk, pr * LANES:(pr + 1) * LANES]
                m0 = _dot(ws_ref[2 * pr], jnp.where(lo, r, 0.0).astype(BF16))
                m1 = _dot(ws_ref[2 * pr + 1], jnp.where(lo, 0.0, r).astype(BF16))
                pieces.append(m0 + m1)
            mixed = jnp.concatenate(pieces, axis=1) + bs
            gate_ref[pl.ds(rows.start + ci * GM_CHUNK, GM_CHUNK), :] = (gu[chunk, :] * mixed).astype(BF16)


def _even_inproj(x2d, w_in_p, kv_norm, gm_g, gm_b, ws_m, bs_full, cos_t, sin_a, sin_b, batch, seq):
    n_tok = x2d.shape[0]
    tm = INPROJ_TM
    tiles_per_seq = seq // tm
    row = lambda w: pl.BlockSpec((tm, w), lambda i: (i, 0))
    tab = pl.BlockSpec((tm, LANES), lambda i: (i % tiles_per_seq, 0))
    n_in = w_in_p.shape[1]
    return pl.pallas_call(
        _even_inproj_kernel,
        grid=(n_tok // tm,),
        in_specs=[row(D_MODEL), _const_spec((D_MODEL, n_in)), _const_spec((1, KV_RANK)),
                  _const_spec((1, GM_WIDTH)), _const_spec((1, GM_WIDTH)),
                  _const_spec((GM_HEADS, GM_CHUNK, GM_CHUNK)), _const_spec((GM_CHUNK, GM_WIDTH)),
                  tab, tab, tab],
        out_specs=[row(Q_RANK), row(KV_RANK),
                   pl.BlockSpec((None, ROPE_DIM, tm), lambda i: (i // tiles_per_seq, 0, i % tiles_per_seq)),
                   row(2 * LANES), row(GM_WIDTH),
                   pl.BlockSpec((1, GM_CHUNK, GM_WIDTH), lambda i: (i // tiles_per_seq, 0, 0))],
        out_shape=[jax.ShapeDtypeStruct((n_tok, Q_RANK), F32),
                   jax.ShapeDtypeStruct((n_tok, KV_RANK), F32),
                   jax.ShapeDtypeStruct((batch, ROPE_DIM, seq), F32),
                   jax.ShapeDtypeStruct((n_tok, 2 * LANES), BF16),
                   jax.ShapeDtypeStruct((n_tok, GM_WIDTH), BF16),
                   jax.ShapeDtypeStruct((batch, GM_CHUNK, GM_WIDTH), F32)],
        compiler_params=_cparams(("arbitrary",)),
        name="even_inproj",
    )(x2d, w_in_p, kv_norm, gm_g, gm_b, ws_m, bs_full, cos_t, sin_a, sin_b)


def _queries(q_a, qn_g, wq_nope, wq_pe, wuk_ref, cos_t, sin_a, sin_b):
    qn = _rms_norm(q_a, qn_g).astype(BF16)
    nope = _dot(qn, wq_nope).astype(BF16)
    lat = jnp.concatenate([_dot(nope[:, pr * LANES:(pr + 1) * LANES], wuk_ref[pr])
                           for pr in range(MLA_HEADS // 2)], axis=1)
    pe = _dot(qn, wq_pe)
    n = MLA_HEADS * LANES
    sin_s = sin_a + sin_b
    pes = [pe[:, h * LANES:(h + 1) * LANES] * cos_t + pe[:, n + h * LANES:n + (h + 1) * LANES] * sin_s
           for h in range(MLA_HEADS)]
    return lat, pes


def _mla_attn_kernel(qa_lo_ref, qa_hi_ref, kcat_ref, qn_ref, wqn_ref, wqp_ref, wuk_ref, wuv_ref,
                     cos_lo_ref, sina_lo_ref, sinb_lo_ref, cos_hi_ref, sina_hi_ref, sinb_hi_ref,
                     out_hbm, qcat_s, m_s, acc_s, obuf, osem, *, nq):
    tq = qa_lo_ref.shape[0]
    n_rows = MLA_HEADS * tq
    b, i = pl.program_id(0), pl.program_id(1)
    step = b * pl.num_programs(1) + i
    n_steps = pl.num_programs(0) * pl.num_programs(1)
    q_tiles = (i, nq - 1 - i)

    def out_copies():
        return [pltpu.make_async_copy(
            obuf.at[t], out_hbm.at[pl.ds(pl.multiple_of((b * nq + q_tiles[t]) * tq, tq), tq), :], osem.at[t])
            for t in range(2)]

    qa = (qa_lo_ref, qa_hi_ref)
    tabs = ((cos_lo_ref, sina_lo_ref, sinb_lo_ref), (cos_hi_ref, sina_hi_ref, sinb_hi_ref))
    for t in range(2):
        lat, pes = _queries(qa[t][...], qn_ref[...], wqn_ref[...], wqp_ref[...], wuk_ref,
                            *(r[...] for r in tabs[t]))
        for h in range(MLA_HEADS):
            qcat_s[t, h * tq:(h + 1) * tq, :] = (ATTN_SCALE_LOG2 * jnp.concatenate(
                [lat[:, h * LANES:(h + 1) * LANES], pes[h]], axis=1)).astype(BF16)

    lanes_x = lambda a, n: jnp.concatenate([a] * n, axis=1)
    rb = ATTN_ROW_BLOCK
    ones = jnp.ones((tq, LANES), BF16)

    def keys(kv_tile):
        k = kcat_ref[pl.ds(pl.multiple_of(kv_tile * tq, tq), tq), :]
        return k, jnp.concatenate([k[:, :KV_RANK], ones], axis=1)

    q_pos = lax.broadcasted_iota(jnp.int32, (rb, tq), 0) & (tq - 1)
    k_pos = lax.broadcasted_iota(jnp.int32, (rb, tq), 1)
    causal = k_pos <= q_pos
    for t in range(2):
        k, v1 = keys(q_tiles[t])
        for r0 in range(0, n_rows, rb):
            rows = slice(r0, r0 + rb)
            s = jnp.where(causal, _dot_nt(qcat_s[t, rows, :], k), -jnp.inf)
            m0 = jnp.broadcast_to(jnp.max(s, axis=-1, keepdims=True), (rb, LANES))
            m_s[t, rows, :] = m0
            acc_s[t, rows, :] = _dot(jnp.exp2(s - lanes_x(m0, tq // LANES)).astype(BF16), v1)

    for u in range(nq - 1):
        hi = u >= i
        t = hi.astype(jnp.int32)
        k, v1 = keys(jnp.where(hi, u - i, u))
        for r0 in range(0, n_rows, rb):
            rows = slice(r0, r0 + rb)
            s = _dot_nt(qcat_s[t, rows, :], k)
            m_old = m_s[t, rows, :]
            m_new = jnp.maximum(m_old, jnp.max(s, axis=-1, keepdims=True))
            alpha = jnp.exp2(m_old - m_new)
            p = jnp.exp2(s - lanes_x(m_new, tq // LANES))
            acc_s[t, rows, :] = lanes_x(alpha, 2) * acc_s[t, rows, :] + _dot(p.astype(BF16), v1)
            m_s[t, rows, :] = m_new

    @pl.when(step > 0)
    def _():
        for cp in out_copies():
            cp.wait()

    for t in range(2):
        o_all = jnp.concatenate(
            [acc_s[t, h * tq:(h + 1) * tq, :KV_RANK] / acc_s[t, h * tq:(h + 1) * tq, KV_RANK:]
             for h in range(MLA_HEADS)], axis=1).astype(BF16)
        obuf[t] = jnp.concatenate(
            [_dot(o_all[:, pr * 2 * LANES:(pr + 1) * 2 * LANES], wuv_ref[pr]) for pr in range(MLA_HEADS // 2)],
            axis=1).astype(BF16)
    for cp in out_copies():
        cp.start()

    @pl.when(step == n_steps - 1)
    def _():
        for cp in out_copies():
            cp.wait()


def _mla_attn(q_a, kcat, qn_g, wq_nope, wq_pe, w_uk, w_uv, cos_t, sin_a, sin_b, batch, seq):
    tq = ATTN_TQ
    nq = seq // tq
    n_attn = MLA_HEADS * V_DIM
    lo = lambda b, i: i
    hi = lambda b, i: nq - 1 - i
    tab = lambda sel: pl.BlockSpec((tq, LANES), lambda b, i: (sel(b, i), 0))
    qa_spec = lambda sel: pl.BlockSpec((tq, Q_RANK), lambda b, i: (b * nq + sel(b, i), 0))
    return pl.pallas_call(
        functools.partial(_mla_attn_kernel, nq=nq),
        grid=(batch, nq // 2),
        in_specs=[qa_spec(lo), qa_spec(hi),
                  pl.BlockSpec((None, seq, 2 * LANES), lambda b, i: (b, 0, 0)),
                  _const_spec((1, Q_RANK)), _const_spec(wq_nope.shape), _const_spec(wq_pe.shape),
                  _const_spec(w_uk.shape), _const_spec(w_uv.shape),
                  tab(lo), tab(lo), tab(lo), tab(hi), tab(hi), tab(hi)],
        out_specs=pl.BlockSpec(memory_space=pl.ANY),
        out_shape=jax.ShapeDtypeStruct((batch * seq, n_attn), BF16),
        scratch_shapes=[pltpu.VMEM((2, MLA_HEADS * tq, 2 * LANES), BF16),
                        pltpu.VMEM((2, MLA_HEADS * tq, LANES), F32),
                        pltpu.VMEM((2, MLA_HEADS * tq, 2 * KV_RANK), F32),
                        pltpu.VMEM((2, tq, n_attn), BF16),
                        pltpu.SemaphoreType.DMA((2,))],
        compiler_params=_cparams(("arbitrary", "arbitrary")),
        name="mla_attn",
    )(q_a, q_a, kcat.reshape(batch, seq, 2 * LANES), qn_g, wq_nope, wq_pe, w_uk, w_uv,
      cos_t, sin_a, sin_b, cos_t, sin_a, sin_b)


def _mix_ffn_kernel(*refs, n_act, has_pre):
    x_ref = refs[0]
    act_refs = refs[1:1 + n_act]
    k = 1 + n_act
    pre_ref = refs[k] if has_pre else None
    k += int(has_pre)
    wo_refs = refs[k:k + n_act]
    k += n_act
    g1_ref, b1_ref, g2_ref, b2_ref, w1_ref, w2_ref, out_ref = refs[k:k + 7]

    tm = x_ref.shape[0]
    rb = min(FFN_ROW_BLOCK, tm)
    blocks = [slice(r0, r0 + rb) for r0 in range(0, tm, rb)]
    fs = []
    for rows in blocks:
        f = None
        for idx in range(n_act):
            a = act_refs[idx][rows, :]
            if has_pre and idx == 0:
                a = _dot(a.astype(BF16), pre_ref[...])
            part = _dot(a.astype(BF16), wo_refs[idx][...])
            f = part if f is None else f + part
        fs.append(f)
    x1s = [_layer_norm(DN_ALPHA * x_ref[rows, :] + f, g1_ref[...], b1_ref[...]) for rows, f in zip(blocks, fs)]
    x1bs = [x1.astype(BF16) for x1 in x1s]
    accs = [None] * len(blocks)
    for c in range(FFN_HIDDEN // FFN_HC):
        cols = slice(c * FFN_HC, (c + 1) * FFN_HC)
        for i, x1b in enumerate(x1bs):
            h = jnp.maximum(_dot(x1b, w1_ref[:, cols]), 0.0)
            part = _dot((h * h).astype(BF16), w2_ref[cols, :])
            accs[i] = part if accs[i] is None else accs[i] + part
    for rows, x1, acc in zip(blocks, x1s, accs):
        out_ref[rows, :] = _layer_norm(DN_ALPHA * x1 + acc, g2_ref[...], b2_ref[...])


def _mix_ffn(x2d, acts, pre_w, wos, ln1_g, ln1_b, ln2_g, ln2_b, w1, w2, name):
    n_tok = x2d.shape[0]
    tm = min(FFN_TM, n_tok)
    row = lambda w: pl.BlockSpec((tm, w), lambda i: (i, 0))
    has_pre = pre_w is not None
    in_specs = [row(D_MODEL)] + [row(a.shape[1]) for a in acts]
    args = [x2d] + list(acts)
    if has_pre:
        in_specs.append(_const_spec(pre_w.shape))
        args.append(pre_w)
    in_specs += [_const_spec(w.shape) for w in wos]
    args += list(wos)
    in_specs += [_const_spec((1, D_MODEL))] * 4 + [_const_spec(w1.shape), _const_spec(w2.shape)]
    args += [ln1_g, ln1_b, ln2_g, ln2_b, w1, w2]
    return pl.pallas_call(
        functools.partial(_mix_ffn_kernel, n_act=len(acts), has_pre=has_pre),
        grid=(n_tok // tm,),
        in_specs=in_specs,
        out_specs=row(D_MODEL),
        out_shape=jax.ShapeDtypeStruct((n_tok, D_MODEL), F32),
        compiler_params=_cparams(("arbitrary",)),
        name=name,
    )(*args)


def _mix_ffn_sample_kernel(*refs, n_act, has_pre):
    x_ref = refs[0]
    act_refs = refs[1:1 + n_act]
    k = 1 + n_act
    pre_ref = refs[k] if has_pre else None
    k += int(has_pre)
    wo_refs = refs[k:k + n_act]
    k += n_act
    g1_ref, b1_ref, g2_ref, b2_ref, w1_ref, w2_ref, out_ref, w1b_ref, w2b_ref, x1_s, acc_s = refs[k:k + 11]
    c = pl.program_id(0)

    @pl.when(c == 0)
    def _():
        f = None
        for idx in range(n_act):
            a = act_refs[idx][...]
            if has_pre and idx == 0:
                a = _dot(a.astype(BF16), pre_ref[...])
            part = _dot(a.astype(BF16), wo_refs[idx][...])
            f = part if f is None else f + part
        x1_s[...] = _layer_norm(DN_ALPHA * x_ref[...] + f, g1_ref[...], b1_ref[...])
        acc_s[...] = jnp.zeros(acc_s.shape, F32)

    w1c = w1_ref[...].astype(BF16)
    w2c = w2_ref[...].astype(BF16)
    w1b_ref[...] = w1c
    w2b_ref[...] = w2c
    h = jnp.maximum(_dot(x1_s[...].astype(BF16), w1c), 0.0)
    acc_s[...] += _dot((h * h).astype(BF16), w2c)

    @pl.when(c == pl.num_programs(0) - 1)
    def _():
        out_ref[...] = _layer_norm(DN_ALPHA * x1_s[...] + acc_s[...], g2_ref[...], b2_ref[...])


def _mix_ffn_sample(x2d, acts, pre_w, wos, ln1_g, ln1_b, ln2_g, ln2_b, w1_all, w2_all, layer, name):
    n_tok = x2d.shape[0]
    has_pre = pre_w is not None
    full = lambda a: pl.BlockSpec(a.shape, lambda c: (0,) * a.ndim)
    args = [x2d] + list(acts) + ([pre_w] if has_pre else []) + list(wos) + [ln1_g, ln1_b, ln2_g, ln2_b]
    in_specs = [full(a) for a in args]
    in_specs += [pl.BlockSpec((None, D_MODEL, FFN_HC), lambda c: (layer, 0, c)),
                 pl.BlockSpec((None, FFN_HC, D_MODEL), lambda c: (layer, c, 0))]
    args += [w1_all, w2_all]
    return pl.pallas_call(
        functools.partial(_mix_ffn_sample_kernel, n_act=len(acts), has_pre=has_pre),
        grid=(FFN_HIDDEN // FFN_HC,),
        in_specs=in_specs,
        out_specs=[pl.BlockSpec((n_tok, D_MODEL), lambda c: (0, 0)),
                   pl.BlockSpec((D_MODEL, FFN_HC), lambda c: (0, c)),
                   pl.BlockSpec((FFN_HC, D_MODEL), lambda c: (c, 0))],
        out_shape=[jax.ShapeDtypeStruct((n_tok, D_MODEL), F32),
                   jax.ShapeDtypeStruct((D_MODEL, FFN_HIDDEN), BF16),
                   jax.ShapeDtypeStruct((FFN_HIDDEN, D_MODEL), BF16)],
        scratch_shapes=[pltpu.VMEM((n_tok, D_MODEL), F32), pltpu.VMEM((n_tok, D_MODEL), F32)],
        compiler_params=_cparams(("arbitrary",)),
        name=name,
    )(*args)


def _even_inproj_sample_kernel(x_ref, w_ref, kvn_ref, gmg_ref, gmb_ref, ws0_ref, bs0_ref,
                               cos_ref, sina_ref, sinb_ref, qn_ref, wqn_ref, wqp_ref, wuk_ref,
                               ckv_ref, kpe_ref, gate_ref, vn_ref, qlat_ref, qpe_ref):
    z = _dot(x_ref[...].astype(BF16), w_ref[...])
    o_c, o_u, o_v, o_k = Q_RANK, Q_RANK + KV_RANK, Q_RANK + KV_RANK + GM_WIDTH, Q_RANK + KV_RANK + 2 * GM_WIDTH
    cos_t, sin_a, sin_b = cos_ref[...], sina_ref[...], sinb_ref[...]
    c_n = _rms_norm(z[:, o_c:o_u], kvn_ref[...])
    ckv_ref[...] = c_n
    kp = _rope128(z[:, o_k:o_k + LANES], cos_t, sin_a, sin_b)
    kpe_ref[...] = kp[:, :ROPE_DIM]
    gu = jax.nn.gelu(z[:, o_u:o_v])
    v_n = _layer_norm(jax.nn.gelu(z[:, o_v:o_k]), gmg_ref[...], gmb_ref[...])
    vn_ref[...] = v_n
    mixed = ws0_ref[...].astype(F32) * v_n.astype(BF16).astype(F32) + bs0_ref[...]
    gate_ref[...] = (gu * mixed).astype(BF16)
    lat, pes = _queries(z[:, :o_c], qn_ref[...], wqn_ref[...], wqp_ref[...], wuk_ref,
                        cos_t, sin_a, sin_b)
    qlat_ref[...] = lat
    qpe_ref[...] = jnp.concatenate(pes, axis=1)


def _even_inproj_sample(xs, w_in_p, kv_norm, gm_g, gm_b, ws0, bs0, cos_t, sin_a, sin_b,
                        qn_g, wq_nope, wq_pe, w_uk):
    n = xs.shape[0]
    args = (xs, w_in_p, kv_norm, gm_g, gm_b, ws0, bs0, cos_t, sin_a, sin_b, qn_g, wq_nope, wq_pe, w_uk)
    full = lambda shape: pl.BlockSpec(shape, lambda i: (0,) * len(shape))
    widths = [(KV_RANK, F32), (ROPE_DIM, F32), (GM_WIDTH, BF16), (GM_WIDTH, F32),
              (MLA_HEADS * LANES, F32), (MLA_HEADS * LANES, F32)]
    return pl.pallas_call(
        _even_inproj_sample_kernel,
        grid=(1,),
        in_specs=[full(a.shape) for a in args],
        out_specs=[full((n, w)) for w, _ in widths],
        out_shape=[jax.ShapeDtypeStruct((n, w), dt) for w, dt in widths],
        compiler_params=_cparams(("arbitrary",)),
        name="even_inproj_sample",
    )(*args)


def _decode_attn_kernel(pt_ref, qlat_ref, qpe_ref, ckvs_ref, kpes_ref, ckv_hbm, kpe_hbm, out_ref,
                        ckv_buf, kpe_buf, m_buf, sem, *, n_pg, n_groups):
    b = pl.program_id(0)
    n_b = pl.num_programs(0)
    assert n_groups % DEC_SLOTS == 0 and DEC_AHEAD + 2 <= DEC_SLOTS

    def group_copies(seq, grp):
        slot = grp % DEC_SLOTS
        cps = []
        for k in range(n_pg):
            page = pt_ref[seq, grp * n_pg + k]
            keys = pl.ds(k * PAGE_SIZE, PAGE_SIZE)
            cps.append(pltpu.make_async_copy(ckv_hbm.at[page], ckv_buf.at[slot, keys, :], sem.at[0, slot]))
            cps.append(pltpu.make_async_copy(kpe_hbm.at[page], kpe_buf.at[slot, :, keys], sem.at[1, slot]))
        return cps

    @pl.when(b == 0)
    def _():
        for g in range(DEC_AHEAD):
            for cp in group_copies(0, g):
                cp.start()

    heads, span = MLA_HEADS, DEC_SPAN_PAGES * PAGE_SIZE
    half_span = span // 2
    n_span = n_pg // DEC_SPAN_PAGES
    qlat = qlat_ref[...]
    zero = jnp.zeros_like(qlat)
    q2 = jnp.concatenate([jnp.concatenate([qlat, zero], axis=1),
                          jnp.concatenate([zero, qlat], axis=1)], axis=0).astype(BF16)
    qpe = qpe_ref[...][:, :ROPE_DIM].astype(BF16)

    def scores(slot):
        s_pe = _dot(qpe, kpe_buf[slot].astype(BF16))
        pieces = []
        for j in range(n_span):
            m_buf[slot, j] = jnp.concatenate(
                [ckv_buf[slot, pl.ds(j * span, half_span), :],
                 ckv_buf[slot, pl.ds(j * span + half_span, half_span), :]], axis=1).astype(BF16)
            s2 = _dot_nt(q2, m_buf[slot, j])
            pieces += [s2[:heads] + s_pe[:, j * span:j * span + half_span],
                       s2[heads:] + s_pe[:, j * span + half_span:(j + 1) * span]]
        return jnp.concatenate(pieces, axis=1) * ATTN_SCALE

    def values(p, grp):
        slot = grp % DEC_SLOTS
        o2 = None
        for j in range(n_span):
            lhs = jnp.concatenate([p[:, j * span:j * span + half_span],
                                   p[:, j * span + half_span:(j + 1) * span]], axis=0).astype(BF16)
            part = _dot(lhs, m_buf[slot, j])
            o2 = part if o2 is None else o2 + part
        return o2[:heads, :KV_RANK] + o2[heads:, KV_RANK:]

    state = dict(m_run=None, m=None, l=None, acc=None)
    raw = {}
    probs = {}

    def softmax(k):
        s = raw.pop(k)
        m_cur = jnp.max(s, axis=-1, keepdims=True)
        m_new = m_cur if state["m_run"] is None else jnp.maximum(state["m_run"], m_cur)
        state["m_run"] = m_new
        probs[k] = (jnp.exp(s - m_new), m_new)

    def fold(k):
        p, m_k = probs.pop(k)
        pv = values(p, k)
        p_sum = jnp.sum(p, axis=-1, keepdims=True)
        if state["m"] is None:
            state["l"], state["acc"] = p_sum, pv
        else:
            alpha = jnp.exp(state["m"] - m_k)
            state["l"], state["acc"] = alpha * state["l"] + p_sum, alpha * state["acc"] + pv
        state["m"] = m_k

    for g in range(n_groups):
        if g + DEC_AHEAD < n_groups:
            for cp in group_copies(b, g + DEC_AHEAD):
                cp.start()
        else:
            @pl.when(b + 1 < n_b)
            def _():
                for cp in group_copies(b + 1, g + DEC_AHEAD - n_groups):
                    cp.start()
        for cp in group_copies(b, g):
            cp.wait()
        raw[g] = scores(g % DEC_SLOTS)
        if g >= 1:
            softmax(g - 1)
        if g >= 2:
            fold(g - 2)
    softmax(n_groups - 1)
    fold(n_groups - 2)
    fold(n_groups - 1)
    m, l, acc = state["m"], state["l"], state["acc"]

    r = lambda a: a.astype(BF16).astype(F32)
    kv = r(ckvs_ref[...])
    s_self = (jnp.sum(r(qlat_ref[...]) * kv, axis=-1, keepdims=True)
              + jnp.sum(r(qpe_ref[...]) * r(kpes_ref[...]), axis=-1, keepdims=True)) * ATTN_SCALE
    m_n = jnp.maximum(m, s_self)
    a = jnp.exp(m - m_n)
    p_self = jnp.exp(s_self - m_n)
    out_ref[...] = (a * acc + r(p_self) * kv) / (a * l + p_self)


def _decode_attn(page_table, qlat2, qpe2, ckv_s3, kpe_s3, cache_ckv_e, cache_kpe_t):
    dec_batch, n_pages = page_table.shape
    n_pg = DEC_PG
    qspec = pl.BlockSpec((MLA_HEADS, LANES), lambda b, pt: (b, 0))
    self_spec = pl.BlockSpec((None, 1, LANES), lambda b, pt: (b, 0, 0))
    hbm = pl.BlockSpec(memory_space=pl.ANY)
    grid_spec = pltpu.PrefetchScalarGridSpec(
        num_scalar_prefetch=1,
        grid=(dec_batch,),
        in_specs=[qspec, qspec, self_spec, self_spec, hbm, hbm],
        out_specs=qspec,
        scratch_shapes=[pltpu.VMEM((DEC_SLOTS, n_pg * PAGE_SIZE, KV_RANK), F32),
                        pltpu.VMEM((DEC_SLOTS, ROPE_DIM, n_pg * PAGE_SIZE), F32),
                        pltpu.VMEM((DEC_SLOTS, n_pg // DEC_SPAN_PAGES, DEC_SPAN_PAGES // 2 * PAGE_SIZE, 2 * KV_RANK),
                                   BF16),
                        pltpu.SemaphoreType.DMA((2, DEC_SLOTS))],
    )
    return pl.pallas_call(
        functools.partial(_decode_attn_kernel, n_pg=n_pg, n_groups=n_pages // n_pg),
        grid_spec=grid_spec,
        out_shape=jax.ShapeDtypeStruct((dec_batch * MLA_HEADS, KV_RANK), F32),
        compiler_params=_cparams(("arbitrary",)),
        name="decode_attn",
    )(page_table, qlat2, qpe2, ckv_s3, kpe_s3, cache_ckv_e, cache_kpe_t)


def _s5_readout(hcat_blocks, u, cw_ref, d_ref, wglu_ref):
    y = jnp.concatenate([_dot(hcat_blocks[k], cw_ref[k]) for k in range(S5_NB)], axis=1)
    y = jax.nn.gelu(y + d_ref[...] * u)
    return y * jax.nn.sigmoid(_dot(y.astype(BF16), wglu_ref[...]))


def _s5_prompt_kernel(x_ref, win_ref, bw2_ref, a2_ref, cw2_ref, d_ref, wglu_ref,
                      yg_ref, hre_ref, him_ref, u_s, y_s, st_s, h_s):
    n_b, seg, _ = x_ref.shape
    half = S5_SLABS // 2
    sub = S5_SUB
    n_sub = seg // sub
    rows_sub = sub * n_b
    n_pair = sub // 2
    rows_pair = n_pair * n_b
    reg = rows_pair + n_b
    q_slabs = S5_WIDTH // LANES

    @pl.when(pl.program_id(0) == 0)
    def _():
        h_s[...] = jnp.zeros(h_s.shape, F32)

    def steps_of(slab_ref, s, q, parity):
        return jnp.concatenate([slab_ref[s, q, pl.ds((2 * j + parity) * n_b, n_b), :] for j in range(n_pair)],
                               axis=0)

    def project(s):
        xs = jnp.concatenate([x_ref[b, s * sub:(s + 1) * sub, :] for b in range(n_b)], axis=0)
        u = _dot(xs.astype(BF16), win_ref[...])
        for b in range(n_b):
            for q in range(q_slabs):
                u_s[s, q, pl.ds(b, sub, stride=n_b), :] = u[b * sub:(b + 1) * sub, q * LANES:(q + 1) * LANES]
        rows = slice(s * reg + n_b, (s + 1) * reg)
        for k in range(S5_NB):
            lhs = jnp.concatenate([steps_of(u_s, s, k, 0), steps_of(u_s, s, k, 1)], axis=1).astype(BF16)
            r = _dot(lhs, bw2_ref[k])
            for q in range(S5_BLK_SLABS):
                st_s[S5_BLK_SLABS * k + q, rows, :] = r[:, q * LANES:(q + 1) * LANES]
                st_s[half + S5_BLK_SLABS * k + q, rows, :] = r[:, (S5_BLK_SLABS + q) * LANES:
                                                                (S5_BLK_SLABS + q + 1) * LANES]

    def recur(s, h):
        st_s[:, s * reg:s * reg + n_b, :] = h
        for j in range(n_pair):
            rows = slice(s * reg + (j + 1) * n_b, s * reg + (j + 2) * n_b)
            inc = st_s[:, rows, :]
            ar, ai = a2_ref[:half], a2_ref[half:]
            hr, hi = h[:half], h[half:]
            h = jnp.concatenate([ar * hr - ai * hi + inc[:half], ar * hi + ai * hr + inc[half:]], axis=0)
            st_s[:, rows, :] = h
        return h

    def read_out(s):
        rows = slice(s * reg, s * reg + rows_pair)
        ys = []
        for k in range(S5_NB):
            lhs = jnp.concatenate(
                [st_s[S5_BLK_SLABS * k + q, rows, :] for q in range(S5_BLK_SLABS)]
                + [st_s[half + S5_BLK_SLABS * k + q, rows, :] for q in range(S5_BLK_SLABS)]
                + [steps_of(u_s, s, k, 0), steps_of(u_s, s, k, 1)], axis=1).astype(BF16)
            y2 = _dot(lhs, cw2_ref[k])
            ys.append(jnp.concatenate(
                [y2[j * n_b:(j + 1) * n_b, par * LANES:(par + 1) * LANES] for j in range(n_pair) for par in range(2)],
                axis=0))
        u_t = jnp.concatenate([u_s[s, q] for q in range(q_slabs)], axis=1)
        y = jax.nn.gelu(jnp.concatenate(ys, axis=1) + d_ref[...] * u_t)
        y = y * jax.nn.sigmoid(_dot(y.astype(BF16), wglu_ref[...]))
        for q in range(q_slabs):
            y_s[s, q] = y[:, q * LANES:(q + 1) * LANES]
        for b in range(n_b):
            yg_ref[b, s * sub:(s + 1) * sub, :] = jnp.concatenate(
                [y_s[s, q, pl.ds(b, sub, stride=n_b), :] for q in range(q_slabs)], axis=1).astype(BF16)

    h = h_s[...]
    project(0)
    for s in range(n_sub):
        if s + 1 < n_sub:
            project(s + 1)
        h = recur(s, h)
        if s >= 1:
            read_out(s - 1)
    read_out(n_sub - 1)
    h_s[...] = h
    hre_ref[...] = jnp.concatenate([h[s] for s in range(half)], axis=1)
    him_ref[...] = jnp.concatenate([h[half + s] for s in range(half)], axis=1)


def _s5_prompt(x3, w_in, bw2, a2_b, cw2, d_row, w_glu):
    batch, seq, _ = x3.shape
    seg = S5_L
    n_sub = seg // S5_SUB
    slab_rows = (n_sub, S5_WIDTH // LANES, S5_SUB * batch, LANES)
    return pl.pallas_call(
        _s5_prompt_kernel,
        grid=(seq // seg,),
        in_specs=[pl.BlockSpec((batch, seg, D_MODEL), lambda c: (0, c, 0)),
                  _const_spec(w_in.shape), _const_spec(bw2.shape), _const_spec(a2_b.shape),
                  _const_spec(cw2.shape), _const_spec(d_row.shape), _const_spec(w_glu.shape)],
        out_specs=[pl.BlockSpec((batch, seg, S5_WIDTH), lambda c: (0, c, 0)),
                   pl.BlockSpec((batch, S5_STATES), lambda c: (0, 0)),
                   pl.BlockSpec((batch, S5_STATES), lambda c: (0, 0))],
        out_shape=[jax.ShapeDtypeStruct((batch, seq, S5_WIDTH), BF16),
                   jax.ShapeDtypeStruct((batch, S5_STATES), F32),
                   jax.ShapeDtypeStruct((batch, S5_STATES), F32)],
        scratch_shapes=[pltpu.VMEM(slab_rows, F32), pltpu.VMEM(slab_rows, F32),
                        pltpu.VMEM((S5_SLABS, n_sub * (S5_SUB // 2 + 1) * batch, LANES), F32),
                        pltpu.VMEM((S5_SLABS, batch, LANES), F32)],
        compiler_params=_cparams(("arbitrary",)),
        name="s5_prompt",
    )(x3, w_in, bw2, a2_b, cw2, d_row, w_glu)


def _s5_sample_kernel(x_ref, h0r_ref, h0i_ref, win_ref, bw_ref, ar_ref, ai_ref, cw_ref, d_ref, wglu_ref,
                      yg_ref, hre_ref, him_ref):
    u = _dot(x_ref[...].astype(BF16), win_ref[...])
    ub = u.astype(BF16)
    bu = [_dot(ub[:, k * LANES:(k + 1) * LANES], bw_ref[k]) for k in range(S5_NB)]
    w = S5_KB * S5_STATE
    bu_r = jnp.concatenate([r[:, :w] for r in bu], axis=1)
    bu_i = jnp.concatenate([r[:, w:] for r in bu], axis=1)
    ar, ai, h0r, h0i = ar_ref[...], ai_ref[...], h0r_ref[...], h0i_ref[...]
    hr = bu_r + (ar * h0r - ai * h0i)
    hi = bu_i + (ar * h0i + ai * h0r)
    hre_ref[...] = hr
    him_ref[...] = hi
    blocks = [jnp.concatenate([hr[:, k * w:(k + 1) * w], hi[:, k * w:(k + 1) * w]], axis=1).astype(BF16)
              for k in range(S5_NB)]
    yg_ref[...] = _s5_readout(blocks, u, cw_ref, d_ref, wglu_ref).astype(BF16)


def _s5_sample(xs, h0r, h0i, w_in, bw, a_r, a_i, cw, d_row, w_glu):
    n = xs.shape[0]
    args = (xs, h0r, h0i, w_in, bw, a_r, a_i, cw, d_row, w_glu)
    full = lambda shape: pl.BlockSpec(shape, lambda i: (0,) * len(shape))
    return pl.pallas_call(
        _s5_sample_kernel,
        grid=(1,),
        in_specs=[full(a.shape) for a in args],
        out_specs=[full((n, S5_WIDTH)), full((n, S5_STATES)), full((n, S5_STATES))],
        out_shape=[jax.ShapeDtypeStruct((n, S5_WIDTH), BF16),
                   jax.ShapeDtypeStruct((n, S5_STATES), F32),
                   jax.ShapeDtypeStruct((n, S5_STATES), F32)],
        compiler_params=_cparams(("arbitrary",)),
        name="s5_sample",
    )(*args)


def _rope_tables(pos):
    freqs = ROPE_THETA ** (-jnp.arange(ROPE_HALF, dtype=F32) / ROPE_HALF)
    ang = pos.astype(F32)[:, None] * freqs
    cos, sin = jnp.cos(ang), jnp.sin(ang)
    zero = jnp.zeros_like(cos)
    pad = jnp.zeros((pos.shape[0], LANES - ROPE_DIM), F32)
    cos_t = jnp.concatenate([cos, cos, pad], axis=1)
    sin_a = jnp.concatenate([zero, sin, pad], axis=1)
    sin_b = jnp.concatenate([-sin, zero, pad], axis=1)
    return cos_t, sin_a, sin_b


def _block_diag(blocks):
    *lead, n, r, c = blocks.shape
    eye = jnp.eye(n, dtype=blocks.dtype)
    return jnp.einsum('...nrc,nm->...nrmc', blocks, eye).reshape(*lead, n * r, n * c)


def _even_params(w_in, q_norm, w_q_b, kv_norm, w_kv_b, gm_g, gm_b, gm_w_s, gm_b_s, w_out):
    i1, i2, i3, i4 = Q_RANK, Q_RANK + KV_RANK, Q_RANK + KV_RANK + ROPE_DIM, Q_RANK + KV_RANK + ROPE_DIM + GM_WIDTH
    w_in_p = jnp.concatenate([w_in[:, :i2], w_in[:, i3:], w_in[:, i2:i3],
                              jnp.zeros((D_MODEL, LANES - ROPE_DIM), F32)], axis=1).astype(BF16)
    wq_nope = w_q_b[:, :, :NOPE_DIM].reshape(Q_RANK, MLA_HEADS * NOPE_DIM).astype(BF16)
    pe = w_q_b[:, :, NOPE_DIM:]
    pe_sw = jnp.concatenate([pe[:, :, ROPE_HALF:], pe[:, :, :ROPE_HALF]], axis=2)
    lane_pad = jnp.zeros((Q_RANK, MLA_HEADS, LANES - ROPE_DIM), F32)
    wq_pe = jnp.concatenate(
        [jnp.concatenate([w, lane_pad], axis=2).reshape(Q_RANK, MLA_HEADS * LANES) for w in (pe, pe_sw)],
        axis=1).astype(BF16)
    uk = jnp.transpose(w_kv_b[:, :, :NOPE_DIM], (1, 2, 0))
    uv = jnp.transpose(w_kv_b[:, :, NOPE_DIM:], (1, 0, 2))
    pairs = MLA_HEADS // 2
    w_uk = _block_diag(uk.reshape(pairs, 2, NOPE_DIM, KV_RANK)).astype(BF16)
    w_uv_p = _block_diag(uv.reshape(pairs, 2, KV_RANK, V_DIM)).astype(BF16)
    w_uv = _block_diag(uv).astype(BF16)
    causal = jnp.tril(jnp.ones((GM_CHUNK, GM_CHUNK), dtype=bool))
    ws_m = jnp.where(causal[None], gm_w_s, 0).astype(BF16)
    bs_full = jnp.repeat(gm_b_s.T, GM_HEAD_DIM, axis=1)
    ws0 = jnp.repeat(gm_w_s[:, 0, 0], GM_HEAD_DIM)[None, :].astype(BF16)
    bs0 = bs_full[:1]
    n_attn = MLA_HEADS * V_DIM
    return dict(w_in_p=w_in_p, qn=q_norm[None, :], wq_nope=wq_nope, wq_pe=wq_pe, w_uk=w_uk, w_uv=w_uv, w_uv_p=w_uv_p,
                kvn=kv_norm[None, :], gm_g=gm_g[None, :], gm_b=gm_b[None, :], ws_m=ws_m, bs_full=bs_full,
                ws0=ws0, bs0=bs0, wo_attn=w_out[:n_attn].astype(BF16), wo_gate=w_out[n_attn:].astype(BF16))


def _s5_params(w_in, a_re, a_im, b_re, b_im, c_re, c_im, d, log_dt, w_glu, w_out, batch):
    dt = jnp.exp(log_dt)[:, None]
    ld_r, ld_i = a_re * dt, a_im * dt
    mag = jnp.exp(ld_r)
    ab_r, ab_i = mag * jnp.cos(ld_i), mag * jnp.sin(ld_i)
    den = a_re * a_re + a_im * a_im
    cr = ((ab_r - 1.0) * a_re + ab_i * a_im) / den
    ci = (ab_i * a_re - (ab_r - 1.0) * a_im) / den
    bb_r = cr[..., None] * b_re - ci[..., None] * b_im
    bb_i = cr[..., None] * b_im + ci[..., None] * b_re

    def in_blocks(bb):
        return _block_diag(jnp.swapaxes(bb, 1, 2).reshape(S5_NB, S5_KB, S5_GROUP_DIM, S5_STATE))

    def out_blocks(cc):
        return _block_diag(jnp.swapaxes(cc, 1, 2).reshape(S5_NB, S5_KB, S5_STATE, S5_GROUP_DIM))

    bw = jnp.concatenate([in_blocks(bb_r), in_blocks(bb_i)], axis=2).astype(BF16)
    cw = jnp.concatenate([out_blocks(c_re), out_blocks(-c_im)], axis=1).astype(BF16)
    a_r, a_i = ab_r.reshape(1, S5_STATES), ab_i.reshape(1, S5_STATES)
    half = S5_SLABS // 2
    cmul = lambda xr, xi, yr, yi: (xr * yr - xi * yi, xr * yi + xi * yr)
    a2_r, a2_i = cmul(ab_r, ab_i, ab_r, ab_i)
    abb_r, abb_i = cmul(ab_r[..., None], ab_i[..., None], bb_r, bb_i)
    ca_r, ca_i = cmul(c_re, c_im, ab_r[:, None, :], ab_i[:, None, :])
    ca2_r, ca2_i = cmul(c_re, c_im, a2_r[:, None, :], a2_i[:, None, :])
    cb = jnp.einsum('gcp,gpd->gcd', c_re, bb_r) - jnp.einsum('gcp,gpd->gcd', c_im, bb_i)
    cab = jnp.einsum('gcp,gpd->gcd', ca_r, bb_r) - jnp.einsum('gcp,gpd->gcd', ca_i, bb_i)
    io_blocks = lambda m: _block_diag(jnp.swapaxes(m, 1, 2).reshape(S5_NB, S5_KB, S5_GROUP_DIM, S5_GROUP_DIM))
    bw2 = jnp.concatenate([jnp.concatenate([in_blocks(abb_r), in_blocks(abb_i)], axis=2),
                           jnp.concatenate([in_blocks(bb_r), in_blocks(bb_i)], axis=2)],
                          axis=1).astype(BF16)
    cw2 = jnp.concatenate(
        [jnp.concatenate([out_blocks(ca_r), out_blocks(ca2_r)], axis=2),
         jnp.concatenate([out_blocks(-ca_i), out_blocks(-ca2_i)], axis=2),
         jnp.concatenate([io_blocks(cb), io_blocks(cab)], axis=2),
         jnp.concatenate([jnp.zeros((S5_NB, LANES, LANES), F32), io_blocks(cb)], axis=2)],
        axis=1).astype(BF16)
    a2_b = jnp.concatenate([jnp.broadcast_to(a2_r.reshape(half, 1, LANES), (half, batch, LANES)),
                            jnp.broadcast_to(a2_i.reshape(half, 1, LANES), (half, batch, LANES))], axis=0)
    return dict(w_in=w_in.astype(BF16), bw=bw, cw=cw, a_r=a_r, a_i=a_i, bw2=bw2, cw2=cw2, a2_b=a2_b,
                d=d.reshape(1, S5_WIDTH), w_glu=w_glu.astype(BF16), w_out=w_out.astype(BF16))


def kernel(x_prompt, x_sample, cache_ckv, cache_kpe, state_s5_re, state_s5_im, page_table, w_in_even, mla_q_norm, mla_w_q_b, mla_kv_norm, mla_w_kv_b, gm_norm_g, gm_norm_b, gm_w_s, gm_b_s, w_out_even, w_in_odd, s5_a_re, s5_a_im, s5_b_re, s5_b_im, s5_c_re, s5_c_im, s5_d, s5_log_dt, s5_w_glu, w_out_odd, ln_mix_g, ln_mix_b, ln_ffn_g, ln_ffn_b, ffn_w1, ffn_w2):
    batch, seq, _ = x_prompt.shape
    dec_batch, dec_seq, _ = x_sample.shape
    assert dec_seq == 1 and seq % INPROJ_TM == 0 and seq % ATTN_TQ == 0 and seq % S5_L == 0
    assert (seq // ATTN_TQ) % 2 == 0
    assert (batch * seq) % FFN_TM == 0 and page_table.shape[1] % DEC_PG == 0
    assert batch == SUBLANES

    xp = x_prompt.reshape(batch * seq, D_MODEL)
    xs = x_sample.reshape(dec_batch, D_MODEL)
    tabs_p = _rope_tables(jnp.arange(seq, dtype=jnp.int32))
    tabs_s = _rope_tables(PAST_LEN + jnp.arange(dec_seq, dtype=jnp.int32))

    outs = {k: [] for k in ("ckv_p", "kpe_p", "ckv_s", "kpe_s", "gmv_p", "gmv_s",
                            "s5re_p", "s5im_p", "s5re_s", "s5im_s")}
    for layer in range(DEPTH):
        ln = (ln_mix_g[layer][None, :], ln_mix_b[layer][None, :],
              ln_ffn_g[layer][None, :], ln_ffn_b[layer][None, :])
        if layer % 2 == 0:
            e = layer // 2
            p = _even_params(w_in_even[e], mla_q_norm[e], mla_w_q_b[e], mla_kv_norm[e], mla_w_kv_b[e],
                             gm_norm_g[e], gm_norm_b[e], gm_w_s[e], gm_b_s[e], w_out_even[e])
            q_a, ckv, kpe, kcat, gate, gmv = _even_inproj(
                xp, p["w_in_p"], p["kvn"], p["gm_g"], p["gm_b"], p["ws_m"], p["bs_full"], *tabs_p, batch, seq)
            attn = _mla_attn(q_a, kcat, p["qn"], p["wq_nope"], p["wq_pe"], p["w_uk"], p["w_uv_p"], *tabs_p,
                             batch, seq)
            outs["ckv_p"].append(ckv.reshape(batch, seq, KV_RANK))
            outs["kpe_p"].append(jnp.swapaxes(kpe, 1, 2))
            outs["gmv_p"].append(gmv)
            ckv_s, kpe_s, gate_s, vn_s, qlat_s, qpe_s = _even_inproj_sample(
                xs, p["w_in_p"], p["kvn"], p["gm_g"], p["gm_b"], p["ws0"], p["bs0"], *tabs_s,
                p["qn"], p["wq_nope"], p["wq_pe"], p["w_uk"])
            kpe_pad = jnp.concatenate([kpe_s, jnp.zeros((dec_batch, LANES - ROPE_DIM), F32)], axis=1)
            o_lat = _decode_attn(page_table,
                                 qlat_s.reshape(dec_batch * MLA_HEADS, LANES),
                                 qpe_s.reshape(dec_batch * MLA_HEADS, LANES),
                                 ckv_s.reshape(dec_batch, 1, KV_RANK), kpe_pad.reshape(dec_batch, 1, LANES),
                                 cache_ckv[e], jnp.swapaxes(cache_kpe[e], 1, 2))
            xs, w1_b, w2_b = _mix_ffn_sample(
                xs, [o_lat.reshape(dec_batch, MLA_HEADS * KV_RANK), gate_s], p["w_uv"],
                [p["wo_attn"], p["wo_gate"]], *ln, ffn_w1, ffn_w2, layer, name="even_ffn_sample")
            xp = _mix_ffn(xp, [attn, gate], None, [p["wo_attn"], p["wo_gate"]], *ln, w1_b, w2_b,
                          name="even_ffn_prompt")
            outs["ckv_s"].append(ckv_s.reshape(dec_batch, 1, KV_RANK))
            outs["kpe_s"].append(kpe_s.reshape(dec_batch, 1, ROPE_DIM))
            outs["gmv_s"].append(vn_s.reshape(dec_batch, 1, GM_WIDTH))
        else:
            o = layer // 2
            p = _s5_params(w_in_odd[o], s5_a_re[o], s5_a_im[o], s5_b_re[o], s5_b_im[o], s5_c_re[o],
                           s5_c_im[o], s5_d[o], s5_log_dt[o], s5_w_glu[o], w_out_odd[o], batch)
            yg, hre, him = _s5_prompt(xp.reshape(batch, seq, D_MODEL), p["w_in"], p["bw2"], p["a2_b"],
                                      p["cw2"], p["d"], p["w_glu"])
            outs["s5re_p"].append(hre.reshape(batch, S5_GROUPS, S5_STATE))
            outs["s5im_p"].append(him.reshape(batch, S5_GROUPS, S5_STATE))
            yg_s, hre_s, him_s = _s5_sample(
                xs, state_s5_re[o].reshape(dec_batch, S5_STATES), state_s5_im[o].reshape(dec_batch, S5_STATES),
                p["w_in"], p["bw"], p["a_r"], p["a_i"], p["cw"], p["d"], p["w_glu"])
            xs, w1_b, w2_b = _mix_ffn_sample(xs, [yg_s], None, [p["w_out"]], *ln, ffn_w1, ffn_w2, layer,
                                             name="odd_ffn_sample")
            xp = _mix_ffn(xp, [yg.reshape(batch * seq, S5_WIDTH)], None, [p["w_out"]], *ln, w1_b, w2_b,
                          name="odd_ffn_prompt")
            outs["s5re_s"].append(hre_s.reshape(dec_batch, S5_GROUPS, S5_STATE))
            outs["s5im_s"].append(him_s.reshape(dec_batch, S5_GROUPS, S5_STATE))

    st = jnp.stack
    return (xp.reshape(batch, seq, D_MODEL), xs.reshape(dec_batch, dec_seq, D_MODEL),
            st(outs["ckv_p"]), st(outs["kpe_p"]), st(outs["ckv_s"]), st(outs["kpe_s"]),
            st(outs["gmv_p"]), st(outs["gmv_s"]),
            st(outs["s5re_p"]), st(outs["s5im_p"]), st(outs["s5re_s"]), st(outs["s5im_s"]))
```

```python
import functools
import math

import jax
import jax.numpy as jnp
from jax import lax
from jax.experimental import pallas as pl
from jax.experimental.pallas import tpu as pltpu

F32 = jnp.float32
BF16 = jnp.bfloat16

D_MODEL = 1024
DEPTH = 2
PAST_LEN = 16384
PAGE_SIZE = 128
MLA_HEADS = 8
NOPE_DIM = 64
ROPE_DIM = 32
ROPE_HALF = ROPE_DIM // 2
V_DIM = 64
Q_RANK = 256
KV_RANK = 128
ROPE_THETA = 10000.0
ATTN_SCALE = 1.0 / math.sqrt(NOPE_DIM + ROPE_DIM)
ATTN_SCALE_LOG2 = ATTN_SCALE * math.log2(math.e)
GM_HEADS = 8
GM_HEAD_DIM = 64
GM_WIDTH = GM_HEADS * GM_HEAD_DIM
GM_CHUNK = 128
S5_GROUP_DIM = 16
S5_GROUPS = 32
S5_STATE = 64
S5_WIDTH = S5_GROUPS * S5_GROUP_DIM
S5_STATES = S5_GROUPS * S5_STATE
FFN_HIDDEN = 4 * D_MODEL
DN_ALPHA = (2 * DEPTH) ** 0.25
LN_EPS = 1e-5
RMS_EPS = 1e-6

LANES = 128
SUBLANES = 8
VMEM_LIMIT_BYTES = 56 * 1024 * 1024

INPROJ_TM = 512
INPROJ_ROW_BLOCK = 256
ATTN_TQ = 256
ATTN_ROW_BLOCK = 256
FFN_TM = 1024
FFN_HC = 1024
FFN_ROW_BLOCK = 256
S5_L = 128
S5_SUB = 32
DEC_PG = 32
DEC_SLOTS = 4
DEC_AHEAD = 2
DEC_SPAN_PAGES = 4
DMA_PRIORITIES = 2
S5_KB = 8
S5_NB = S5_GROUPS // S5_KB
S5_SLABS = 2 * S5_STATES // LANES
S5_BLK_SLABS = S5_KB * S5_STATE // LANES


def _cparams(sem):
    return pltpu.CompilerParams(dimension_semantics=sem, vmem_limit_bytes=VMEM_LIMIT_BYTES)


def _const_spec(shape):
    n = len(shape)
    return pl.BlockSpec(shape, lambda *_: (0,) * n, pipeline_mode=pl.Buffered(1))


def _layer_norm(x, g, b):
    mu = jnp.mean(x, axis=-1, keepdims=True)
    xc = x - mu
    var = jnp.mean(xc * xc, axis=-1, keepdims=True)
    return xc * lax.rsqrt(var + LN_EPS) * g + b


def _rms_norm(x, g):
    return x * lax.rsqrt(jnp.mean(x * x, axis=-1, keepdims=True) + RMS_EPS) * g


def _rope128(x, cos_t, sin_a, sin_b):
    return (x * cos_t + pltpu.roll(x, ROPE_HALF, 1) * sin_a
            + pltpu.roll(x, LANES - ROPE_HALF, 1) * sin_b)


def _dot(a, b):
    return jnp.dot(a, b, preferred_element_type=F32)


def _dot_nt(a, b):
    return lax.dot_general(a, b, (((1,), (1,)), ((), ())), preferred_element_type=F32)


def _even_inproj_kernel(x_ref, w_ref, kvn_ref, gmg_ref, gmb_ref, ws_ref, bs_ref,
                        cos_ref, sina_ref, sinb_ref,
                        qa_ref, ckv_ref, kpe_ref, kcat_ref, gate_ref, gmv_ref):
    tm = x_ref.shape[0]
    o_c, o_u, o_v, o_k = Q_RANK, Q_RANK + KV_RANK, Q_RANK + KV_RANK + GM_WIDTH, Q_RANK + KV_RANK + 2 * GM_WIDTH
    rb = min(INPROJ_ROW_BLOCK, tm)
    blocks = [slice(r0, r0 + rb) for r0 in range(0, tm, rb)]
    zs = [_dot(x_ref[rows, :].astype(BF16), w_ref[...]) for rows in blocks]

    gus, vns = [], []
    for rows, z in zip(blocks, zs):
        qa_ref[rows, :] = z[:, :o_c]
        c_n = _rms_norm(z[:, o_c:o_u], kvn_ref[...])
        ckv_ref[rows, :] = c_n
        kp = _rope128(z[:, o_k:o_k + LANES], cos_ref[rows, :], sina_ref[rows, :], sinb_ref[rows, :])
        kpe_ref[:, rows] = kp.T[:ROPE_DIM, :]
        kcat_ref[rows, :] = jnp.concatenate([c_n, kp], axis=1).astype(BF16)
        gus.append(jax.nn.gelu(z[:, o_u:o_v]))
        vns.append(_layer_norm(jax.nn.gelu(z[:, o_v:o_k]), gmg_ref[...], gmb_ref[...]))

    gmv_ref[0] = vns[-1][rb - GM_CHUNK:, :]

    lane = lax.broadcasted_iota(jnp.int32, (GM_CHUNK, LANES), 1)
    lo = lane < GM_HEAD_DIM
    bs = bs_ref[...]
    for rows, gu, v_n in zip(blocks, gus, vns):
        for ci in range(rb // GM_CHUNK):
            chunk = slice(ci * GM_CHUNK, (ci + 1) * GM_CHUNK)
            pieces = []
            for pr in range(GM_WIDTH // LANES):
                r = v_n[chunk, pr * LANES:(pr + 1) * LANES]
                m0 = _dot(ws_ref[2 * pr], jnp.where(lo, r, 0.0).astype(BF16))
                m1 = _dot(ws_ref[2 * pr + 1], jnp.where(lo, 0.0, r).astype(BF16))
                pieces.append(m0 + m1)
            mixed = jnp.concatenate(pieces, axis=1) + bs
            gate_ref[pl.ds(rows.start + ci * GM_CHUNK, GM_CHUNK), :] = (gu[chunk, :] * mixed).astype(BF16)


def _even_inproj(x2d, w_in_p, kv_norm, gm_g, gm_b, ws_m, bs_full, cos_t, sin_a, sin_b, batch, seq):
    n_tok = x2d.shape[0]
    tm = INPROJ_TM
    tiles_per_seq = seq // tm
    row = lambda w: pl.BlockSpec((tm, w), lambda i: (i, 0))
    tab = pl.BlockSpec((tm, LANES), lambda i: (i % tiles_per_seq, 0))
    n_in = w_in_p.shape[1]
    return pl.pallas_call(
        _even_inproj_kernel,
        grid=(n_tok // tm,),
        in_specs=[row(D_MODEL), _const_spec((D_MODEL, n_in)), _const_spec((1, KV_RANK)),
                  _const_spec((1, GM_WIDTH)), _const_spec((1, GM_WIDTH)),
                  _const_spec((GM_HEADS, GM_CHUNK, GM_CHUNK)), _const_spec((GM_CHUNK, GM_WIDTH)),
                  tab, tab, tab],
        out_specs=[row(Q_RANK), row(KV_RANK),
                   pl.BlockSpec((None, ROPE_DIM, tm), lambda i: (i // tiles_per_seq, 0, i % tiles_per_seq)),
                   row(2 * LANES), row(GM_WIDTH),
                   pl.BlockSpec((1, GM_CHUNK, GM_WIDTH), lambda i: (i // tiles_per_seq, 0, 0))],
        out_shape=[jax.ShapeDtypeStruct((n_tok, Q_RANK), F32),
                   jax.ShapeDtypeStruct((n_tok, KV_RANK), F32),
                   jax.ShapeDtypeStruct((batch, ROPE_DIM, seq), F32),
                   jax.ShapeDtypeStruct((n_tok, 2 * LANES), BF16),
                   jax.ShapeDtypeStruct((n_tok, GM_WIDTH), BF16),
                   jax.ShapeDtypeStruct((batch, GM_CHUNK, GM_WIDTH), F32)],
        compiler_params=_cparams(("arbitrary",)),
        name="even_inproj",
    )(x2d, w_in_p, kv_norm, gm_g, gm_b, ws_m, bs_full, cos_t, sin_a, sin_b)


def _queries(q_a, qn_g, wq_nope, wq_pe, wuk_ref, cos_t, sin_a, sin_b):
    qn = _rms_norm(q_a, qn_g).astype(BF16)
    nope = _dot(qn, wq_nope).astype(BF16)
    lat = jnp.concatenate([_dot(nope[:, pr * LANES:(pr + 1) * LANES], wuk_ref[pr])
                           for pr in range(MLA_HEADS // 2)], axis=1)
    pe = _dot(qn, wq_pe)
    n = MLA_HEADS * LANES
    sin_s = sin_a + sin_b
    pes = [pe[:, h * LANES:(h + 1) * LANES] * cos_t + pe[:, n + h * LANES:n + (h + 1) * LANES] * sin_s
           for h in range(MLA_HEADS)]
    return lat, pes


def _mla_attn_kernel(qa_lo_ref, qa_hi_ref, kcat_ref, qn_ref, wqn_ref, wqp_ref, wuk_ref, wuv_ref,
                     cos_lo_ref, sina_lo_ref, sinb_lo_ref, cos_hi_ref, sina_hi_ref, sinb_hi_ref,
                     out_hbm, qcat_s, m_s, acc_s, obuf, osem, *, nq):
    tq = qa_lo_ref.shape[0]
    n_rows = MLA_HEADS * tq
    b, i = pl.program_id(0), pl.program_id(1)
    step = b * pl.num_programs(1) + i
    n_steps = pl.num_programs(0) * pl.num_programs(1)
    q_tiles = (i, nq - 1 - i)

    def out_copies():
        return [pltpu.make_async_copy(
            obuf.at[t], out_hbm.at[pl.ds(pl.multiple_of((b * nq + q_tiles[t]) * tq, tq), tq), :], osem.at[t])
            for t in range(2)]

    qa = (qa_lo_ref, qa_hi_ref)
    tabs = ((cos_lo_ref, sina_lo_ref, sinb_lo_ref), (cos_hi_ref, sina_hi_ref, sinb_hi_ref))
    for t in range(2):
        lat, pes = _queries(qa[t][...], qn_ref[...], wqn_ref[...], wqp_ref[...], wuk_ref,
                            *(r[...] for r in tabs[t]))
        for h in range(MLA_HEADS):
            qcat_s[t, h * tq:(h + 1) * tq, :] = (ATTN_SCALE_LOG2 * jnp.concatenate(
                [lat[:, h * LANES:(h + 1) * LANES], pes[h]], axis=1)).astype(BF16)

    lanes_x = lambda a, n: jnp.concatenate([a] * n, axis=1)
    rb = ATTN_ROW_BLOCK
    ones = jnp.ones((tq, LANES), BF16)

    def keys(kv_tile):
        k = kcat_ref[pl.ds(pl.multiple_of(kv_tile * tq, tq), tq), :]
        return k, jnp.concatenate([k[:, :KV_RANK], ones], axis=1)

    q_pos = lax.broadcasted_iota(jnp.int32, (rb, tq), 0) & (tq - 1)
    k_pos = lax.broadcasted_iota(jnp.int32, (rb, tq), 1)
    causal = k_pos <= q_pos
    for t in range(2):
        k, v1 = keys(q_tiles[t])
        for r0 in range(0, n_rows, rb):
            rows = slice(r0, r0 + rb)
            s = jnp.where(causal, _dot_nt(qcat_s[t, rows, :], k), -jnp.inf)
            m0 = jnp.broadcast_to(jnp.max(s, axis=-1, keepdims=True), (rb, LANES))
            m_s[t, rows, :] = m0
            acc_s[t, rows, :] = _dot(jnp.exp2(s - lanes_x(m0, tq // LANES)).astype(BF16), v1)

    for u in range(nq - 1):
        hi = u >= i
        t = hi.astype(jnp.int32)
        k, v1 = keys(jnp.where(hi, u - i, u))
        for r0 in range(0, n_rows, rb):
            rows = slice(r0, r0 + rb)
            s = _dot_nt(qcat_s[t, rows, :], k)
            m_old = m_s[t, rows, :]
            m_new = jnp.maximum(m_old, jnp.max(s, axis=-1, keepdims=True))
            alpha = jnp.exp2(m_old - m_new)
            p = jnp.exp2(s - lanes_x(m_new, tq // LANES))
            acc_s[t, rows, :] = lanes_x(alpha, 2) * acc_s[t, rows, :] + _dot(p.astype(BF16), v1)
            m_s[t, rows, :] = m_new

    @pl.when(step > 0)
    def _():
        for cp in out_copies():
            cp.wait()

    for t in range(2):
        o_all = jnp.concatenate(
            [acc_s[t, h * tq:(h + 1) * tq, :KV_RANK] / acc_s[t, h * tq:(h + 1) * tq, KV_RANK:]
             for h in range(MLA_HEADS)], axis=1).astype(BF16)
        obuf[t] = jnp.concatenate(
            [_dot(o_all[:, pr * 2 * LANES:(pr + 1) * 2 * LANES], wuv_ref[pr]) for pr in range(MLA_HEADS // 2)],
            axis=1).astype(BF16)
    for cp in out_copies():
        cp.start()

    @pl.when(step == n_steps - 1)
    def _():
        for cp in out_copies():
            cp.wait()


def _mla_attn(q_a, kcat, qn_g, wq_nope, wq_pe, w_uk, w_uv, cos_t, sin_a, sin_b, batch, seq):
    tq = ATTN_TQ
    nq = seq // tq
    n_attn = MLA_HEADS * V_DIM
    lo = lambda b, i: i
    hi = lambda b, i: nq - 1 - i
    tab = lambda sel: pl.BlockSpec((tq, LANES), lambda b, i: (sel(b, i), 0))
    qa_spec = lambda sel: pl.BlockSpec((tq, Q_RANK), lambda b, i: (b * nq + sel(b, i), 0))
    return pl.pallas_call(
        functools.partial(_mla_attn_kernel, nq=nq),
        grid=(batch, nq // 2),
        in_specs=[qa_spec(lo), qa_spec(hi),
                  pl.BlockSpec((None, seq, 2 * LANES), lambda b, i: (b, 0, 0)),
                  _const_spec((1, Q_RANK)), _const_spec(wq_nope.shape), _const_spec(wq_pe.shape),
                  _const_spec(w_uk.shape), _const_spec(w_uv.shape),
                  tab(lo), tab(lo), tab(lo), tab(hi), tab(hi), tab(hi)],
        out_specs=pl.BlockSpec(memory_space=pl.ANY),
        out_shape=jax.ShapeDtypeStruct((batch * seq, n_attn), BF16),
        scratch_shapes=[pltpu.VMEM((2, MLA_HEADS * tq, 2 * LANES), BF16),
                        pltpu.VMEM((2, MLA_HEADS * tq, LANES), F32),
                        pltpu.VMEM((2, MLA_HEADS * tq, 2 * KV_RANK), F32),
                        pltpu.VMEM((2, tq, n_attn), BF16),
                        pltpu.SemaphoreType.DMA((2,))],
        compiler_params=_cparams(("arbitrary", "arbitrary")),
        name="mla_attn",
    )(q_a, q_a, kcat.reshape(batch, seq, 2 * LANES), qn_g, wq_nope, wq_pe, w_uk, w_uv,
      cos_t, sin_a, sin_b, cos_t, sin_a, sin_b)


def _mix_ffn_kernel(*refs, n_act, has_pre):
    x_ref = refs[0]
    act_refs = refs[1:1 + n_act]
    k = 1 + n_act
    pre_ref = refs[k] if has_pre else None
    k += int(has_pre)
    wo_refs = refs[k:k + n_act]
    k += n_act
    g1_ref, b1_ref, g2_ref, b2_ref, w1_ref, w2_ref, out_ref = refs[k:k + 7]

    tm = x_ref.shape[0]
    rb = min(FFN_ROW_BLOCK, tm)
    blocks = [slice(r0, r0 + rb) for r0 in range(0, tm, rb)]
    fs = []
    for rows in blocks:
        f = None
        for idx in range(n_act):
            a = act_refs[idx][rows, :]
            if has_pre and idx == 0:
                a = _dot(a.astype(BF16), pre_ref[...])
            part = _dot(a.astype(BF16), wo_refs[idx][...])
            f = part if f is None else f + part
        fs.append(f)
    x1s = [_layer_norm(DN_ALPHA * x_ref[rows, :] + f, g1_ref[...], b1_ref[...]) for rows, f in zip(blocks, fs)]
    x1bs = [x1.astype(BF16) for x1 in x1s]
    accs = [None] * len(blocks)
    for c in range(FFN_HIDDEN // FFN_HC):
        cols = slice(c * FFN_HC, (c + 1) * FFN_HC)
        for i, x1b in enumerate(x1bs):
            h = jnp.maximum(_dot(x1b, w1_ref[:, cols]), 0.0)
            part = _dot((h * h).astype(BF16), w2_ref[cols, :])
            accs[i] = part if accs[i] is None else accs[i] + part
    for rows, x1, acc in zip(blocks, x1s, accs):
        out_ref[rows, :] = _layer_norm(DN_ALPHA * x1 + acc, g2_ref[...], b2_ref[...])


def _mix_ffn(x2d, acts, pre_w, wos, ln1_g, ln1_b, ln2_g, ln2_b, w1, w2, name):
    n_tok = x2d.shape[0]
    tm = min(FFN_TM, n_tok)
    row = lambda w: pl.BlockSpec((tm, w), lambda i: (i, 0))
    has_pre = pre_w is not None
    in_specs = [row(D_MODEL)] + [row(a.shape[1]) for a in acts]
    args = [x2d] + list(acts)
    if has_pre:
        in_specs.append(_const_spec(pre_w.shape))
        args.append(pre_w)
    in_specs += [_const_spec(w.shape) for w in wos]
    args += list(wos)
    in_specs += [_const_spec((1, D_MODEL))] * 4 + [_const_spec(w1.shape), _const_spec(w2.shape)]
    args += [ln1_g, ln1_b, ln2_g, ln2_b, w1, w2]
    return pl.pallas_call(
        functools.partial(_mix_ffn_kernel, n_act=len(acts), has_pre=has_pre),
        grid=(n_tok // tm,),
        in_specs=in_specs,
        out_specs=row(D_MODEL),
        out_shape=jax.ShapeDtypeStruct((n_tok, D_MODEL), F32),
        compiler_params=_cparams(("arbitrary",)),
        name=name,
    )(*args)


def _mix_ffn_sample_kernel(*refs, n_act, has_pre):
    x_ref = refs[0]
    act_refs = refs[1:1 + n_act]
    k = 1 + n_act
    pre_ref = refs[k] if has_pre else None
    k += int(has_pre)
    wo_refs = refs[k:k + n_act]
    k += n_act
    g1_ref, b1_ref, g2_ref, b2_ref, w1_ref, w2_ref, out_ref, w1b_ref, w2b_ref, x1_s, acc_s = refs[k:k + 11]
    c = pl.program_id(0)

    @pl.when(c == 0)
    def _():
        f = None
        for idx in range(n_act):
            a = act_refs[idx][...]
            if has_pre and idx == 0:
                a = _dot(a.astype(BF16), pre_ref[...])
            part = _dot(a.astype(BF16), wo_refs[idx][...])
            f = part if f is None else f + part
        x1_s[...] = _layer_norm(DN_ALPHA * x_ref[...] + f, g1_ref[...], b1_ref[...])
        acc_s[...] = jnp.zeros(acc_s.shape, F32)

    w1c = w1_ref[...].astype(BF16)
    w2c = w2_ref[...].astype(BF16)
    w1b_ref[...] = w1c
    w2b_ref[...] = w2c
    h = jnp.maximum(_dot(x1_s[...].astype(BF16), w1c), 0.0)
    acc_s[...] += _dot((h * h).astype(BF16), w2c)

    @pl.when(c == pl.num_programs(0) - 1)
    def _():
        out_ref[...] = _layer_norm(DN_ALPHA * x1_s[...] + acc_s[...], g2_ref[...], b2_ref[...])


def _mix_ffn_sample(x2d, acts, pre_w, wos, ln1_g, ln1_b, ln2_g, ln2_b, w1_all, w2_all, layer, name):
    n_tok = x2d.shape[0]
    has_pre = pre_w is not None
    full = lambda a: pl.BlockSpec(a.shape, lambda c: (0,) * a.ndim)
    args = [x2d] + list(acts) + ([pre_w] if has_pre else []) + list(wos) + [ln1_g, ln1_b, ln2_g, ln2_b]
    in_specs = [full(a) for a in args]
    in_specs += [pl.BlockSpec((None, D_MODEL, FFN_HC), lambda c: (layer, 0, c)),
                 pl.BlockSpec((None, FFN_HC, D_MODEL), lambda c: (layer, c, 0))]
    args += [w1_all, w2_all]
    return pl.pallas_call(
        functools.partial(_mix_ffn_sample_kernel, n_act=len(acts), has_pre=has_pre),
        grid=(FFN_HIDDEN // FFN_HC,),
        in_specs=in_specs,
        out_specs=[pl.BlockSpec((n_tok, D_MODEL), lambda c: (0, 0)),
                   pl.BlockSpec((D_MODEL, FFN_HC), lambda c: (0, c)),
                   pl.BlockSpec((FFN_HC, D_MODEL), lambda c: (c, 0))],
        out_shape=[jax.ShapeDtypeStruct((n_tok, D_MODEL), F32),
                   jax.ShapeDtypeStruct((D_MODEL, FFN_HIDDEN), BF16),
                   jax.ShapeDtypeStruct((FFN_HIDDEN, D_MODEL), BF16)],
        scratch_shapes=[pltpu.VMEM((n_tok, D_MODEL), F32), pltpu.VMEM((n_tok, D_MODEL), F32)],
        compiler_params=_cparams(("arbitrary",)),
        name=name,
    )(*args)


def _even_inproj_sample_kernel(x_ref, w_ref, kvn_ref, gmg_ref, gmb_ref, ws0_ref, bs0_ref,
                               cos_ref, sina_ref, sinb_ref, qn_ref, wqn_ref, wqp_ref, wuk_ref,
                               ckv_ref, kpe_ref, gate_ref, vn_ref, qlat_ref, qpe_ref):
    z = _dot(x_ref[...].astype(BF16), w_ref[...])
    o_c, o_u, o_v, o_k = Q_RANK, Q_RANK + KV_RANK, Q_RANK + KV_RANK + GM_WIDTH, Q_RANK + KV_RANK + 2 * GM_WIDTH
    cos_t, sin_a, sin_b = cos_ref[...], sina_ref[...], sinb_ref[...]
    c_n = _rms_norm(z[:, o_c:o_u], kvn_ref[...])
    ckv_ref[...] = c_n
    kp = _rope128(z[:, o_k:o_k + LANES], cos_t, sin_a, sin_b)
    kpe_ref[...] = kp[:, :ROPE_DIM]
    gu = jax.nn.gelu(z[:, o_u:o_v])
    v_n = _layer_norm(jax.nn.gelu(z[:, o_v:o_k]), gmg_ref[...], gmb_ref[...])
    vn_ref[...] = v_n
    mixed = ws0_ref[...].astype(F32) * v_n.astype(BF16).astype(F32) + bs0_ref[...]
    gate_ref[...] = (gu * mixed).astype(BF16)
    lat, pes = _queries(z[:, :o_c], qn_ref[...], wqn_ref[...], wqp_ref[...], wuk_ref,
                        cos_t, sin_a, sin_b)
    qlat_ref[...] = lat
    qpe_ref[...] = jnp.concatenate(pes, axis=1)


def _even_inproj_sample(xs, w_in_p, kv_norm, gm_g, gm_b, ws0, bs0, cos_t, sin_a, sin_b,
                        qn_g, wq_nope, wq_pe, w_uk):
    n = xs.shape[0]
    args = (xs, w_in_p, kv_norm, gm_g, gm_b, ws0, bs0, cos_t, sin_a, sin_b, qn_g, wq_nope, wq_pe, w_uk)
    full = lambda shape: pl.BlockSpec(shape, lambda i: (0,) * len(shape))
    widths = [(KV_RANK, F32), (ROPE_DIM, F32), (GM_WIDTH, BF16), (GM_WIDTH, F32),
              (MLA_HEADS * LANES, F32), (MLA_HEADS * LANES, F32)]
    return pl.pallas_call(
        _even_inproj_sample_kernel,
        grid=(1,),
        in_specs=[full(a.shape) for a in args],
        out_specs=[full((n, w)) for w, _ in widths],
        out_shape=[jax.ShapeDtypeStruct((n, w), dt) for w, dt in widths],
        compiler_params=_cparams(("arbitrary",)),
        name="even_inproj_sample",
    )(*args)


def _decode_attn_kernel(pt_ref, qlat_ref, qpe_ref, ckvs_ref, kpes_ref, ckv_hbm, kpe_hbm, out_ref,
                        ckv_buf, kpe_buf, m_buf, sem, *, n_pg, n_groups):
    b = pl.program_id(0)
    n_b = pl.num_programs(0)
    assert n_groups % DEC_SLOTS == 0 and DEC_AHEAD + 2 <= DEC_SLOTS

    def group_copies(seq, grp):
        slot = grp % DEC_SLOTS
        cps = []
        for k in range(n_pg):
            page = pt_ref[seq, grp * n_pg + k]
            keys = pl.ds(k * PAGE_SIZE, PAGE_SIZE)
            cps.append(pltpu.make_async_copy(ckv_hbm.at[page], ckv_buf.at[slot, keys, :], sem.at[0, slot]))
            cps.append(pltpu.make_async_copy(kpe_hbm.at[page], kpe_buf.at[slot, :, keys], sem.at[1, slot]))
        return cps

    def start_group(seq, grp):
        for i, cp in enumerate(group_copies(seq, grp)):
            cp.start(priority=(i // 2) % DMA_PRIORITIES)

    @pl.when(b == 0)
    def _():
        for g in range(DEC_AHEAD):
            start_group(0, g)

    heads, span = MLA_HEADS, DEC_SPAN_PAGES * PAGE_SIZE
    half_span = span // 2
    n_span = n_pg // DEC_SPAN_PAGES
    qlat = qlat_ref[...]
    zero = jnp.zeros_like(qlat)
    q2 = jnp.concatenate([jnp.concatenate([qlat, zero], axis=1),
                          jnp.concatenate([zero, qlat], axis=1)], axis=0).astype(BF16)
    qpe = qpe_ref[...][:, :ROPE_DIM].astype(BF16)

    def scores(slot):
        s_pe = _dot(qpe, kpe_buf[slot].astype(BF16))
        pieces = []
        for j in range(n_span):
            m_buf[slot, j] = jnp.concatenate(
                [ckv_buf[slot, pl.ds(j * span, half_span), :],
                 ckv_buf[slot, pl.ds(j * span + half_span, half_span), :]], axis=1).astype(BF16)
            s2 = _dot_nt(q2, m_buf[slot, j])
            pieces += [s2[:heads] + s_pe[:, j * span:j * span + half_span],
                       s2[heads:] + s_pe[:, j * span + half_span:(j + 1) * span]]
        return jnp.concatenate(pieces, axis=1) * ATTN_SCALE

    def values(p, grp):
        slot = grp % DEC_SLOTS
        o2 = None
        for j in range(n_span):
            lhs = jnp.concatenate([p[:, j * span:j * span + half_span],
                                   p[:, j * span + half_span:(j + 1) * span]], axis=0).astype(BF16)
            part = _dot(lhs, m_buf[slot, j])
            o2 = part if o2 is None else o2 + part
        return o2[:heads, :KV_RANK] + o2[heads:, KV_RANK:]

    state = dict(m_run=None, m=None, l=None, acc=None)
    raw = {}
    probs = {}

    def softmax(k):
        s = raw.pop(k)
        m_cur = jnp.max(s, axis=-1, keepdims=True)
        m_new = m_cur if state["m_run"] is None else jnp.maximum(state["m_run"], m_cur)
        state["m_run"] = m_new
        probs[k] = (jnp.exp(s - m_new), m_new)

    def fold(k):
        p, m_k = probs.pop(k)
        pv = values(p, k)
        p_sum = jnp.sum(p, axis=-1, keepdims=True)
        if state["m"] is None:
            state["l"], state["acc"] = p_sum, pv
        else:
            alpha = jnp.exp(state["m"] - m_k)
            state["l"], state["acc"] = alpha * state["l"] + p_sum, alpha * state["acc"] + pv
        state["m"] = m_k

    for g in range(n_groups):
        if g + DEC_AHEAD < n_groups:
            start_group(b, g + DEC_AHEAD)
        else:
            @pl.when(b + 1 < n_b)
            def _():
                start_group(b + 1, g + DEC_AHEAD - n_groups)
        for cp in group_copies(b, g):
            cp.wait()
        raw[g] = scores(g % DEC_SLOTS)
        if g >= 1:
            softmax(g - 1)
        if g >= 2:
            fold(g - 2)
    softmax(n_groups - 1)
    fold(n_groups - 2)
    fold(n_groups - 1)
    m, l, acc = state["m"], state["l"], state["acc"]

    r = lambda a: a.astype(BF16).astype(F32)
    kv = r(ckvs_ref[...])
    s_self = (jnp.sum(r(qlat_ref[...]) * kv, axis=-1, keepdims=True)
              + jnp.sum(r(qpe_ref[...]) * r(kpes_ref[...]), axis=-1, keepdims=True)) * ATTN_SCALE
    m_n = jnp.maximum(m, s_self)
    a = jnp.exp(m - m_n)
    p_self = jnp.exp(s_self - m_n)
    out_ref[...] = (a * acc + r(p_self) * kv) / (a * l + p_self)


def _decode_attn(page_table, qlat2, qpe2, ckv_s3, kpe_s3, cache_ckv_e, cache_kpe_t):
    dec_batch, n_pages = page_table.shape
    n_pg = DEC_PG
    qspec = pl.BlockSpec((MLA_HEADS, LANES), lambda b, pt: (b, 0))
    self_spec = pl.BlockSpec((None, 1, LANES), lambda b, pt: (b, 0, 0))
    hbm = pl.BlockSpec(memory_space=pl.ANY)
    grid_spec = pltpu.PrefetchScalarGridSpec(
        num_scalar_prefetch=1,
        grid=(dec_batch,),
        in_specs=[qspec, qspec, self_spec, self_spec, hbm, hbm],
        out_specs=qspec,
        scratch_shapes=[pltpu.VMEM((DEC_SLOTS, n_pg * PAGE_SIZE, KV_RANK), F32),
                        pltpu.VMEM((DEC_SLOTS, ROPE_DIM, n_pg * PAGE_SIZE), F32),
                        pltpu.VMEM((DEC_SLOTS, n_pg // DEC_SPAN_PAGES, DEC_SPAN_PAGES // 2 * PAGE_SIZE, 2 * KV_RANK),
                                   BF16),
                        pltpu.SemaphoreType.DMA((2, DEC_SLOTS))],
    )
    return pl.pallas_call(
        functools.partial(_decode_attn_kernel, n_pg=n_pg, n_groups=n_pages // n_pg),
        grid_spec=grid_spec,
        out_shape=jax.ShapeDtypeStruct((dec_batch * MLA_HEADS, KV_RANK), F32),
        compiler_params=_cparams(("arbitrary",)),
        name="decode_attn",
    )(page_table, qlat2, qpe2, ckv_s3, kpe_s3, cache_ckv_e, cache_kpe_t)


def _s5_readout(hcat_blocks, u, cw_ref, d_ref, wglu_ref):
    y = jnp.concatenate([_dot(hcat_blocks[k], cw_ref[k]) for k in range(S5_NB)], axis=1)
    y = jax.nn.gelu(y + d_ref[...] * u)
    return y * jax.nn.sigmoid(_dot(y.astype(BF16), wglu_ref[...]))


def _s5_prompt_kernel(x_ref, win_ref, bw2_ref, a2_ref, cw2_ref, d_ref, wglu_ref,
                      yg_ref, hre_ref, him_ref, u_s, y_s, st_s, h_s):
    n_b, seg, _ = x_ref.shape
    half = S5_SLABS // 2
    sub = S5_SUB
    n_sub = seg // sub
    rows_sub = sub * n_b
    n_pair = sub // 2
    rows_pair = n_pair * n_b
    reg = rows_pair + n_b
    q_slabs = S5_WIDTH // LANES

    @pl.when(pl.program_id(0) == 0)
    def _():
        h_s[...] = jnp.zeros(h_s.shape, F32)

    def steps_of(slab_ref, s, q, parity):
        return jnp.concatenate([slab_ref[s, q, pl.ds((2 * j + parity) * n_b, n_b), :] for j in range(n_pair)],
                               axis=0)

    def project(s):
        xs = jnp.concatenate([x_ref[b, s * sub:(s + 1) * sub, :] for b in range(n_b)], axis=0)
        u = _dot(xs.astype(BF16), win_ref[...])
        for b in range(n_b):
            for q in range(q_slabs):
                u_s[s, q, pl.ds(b, sub, stride=n_b), :] = u[b * sub:(b + 1) * sub, q * LANES:(q + 1) * LANES]
        rows = slice(s * reg + n_b, (s + 1) * reg)
        for k in range(S5_NB):
            lhs = jnp.concatenate([steps_of(u_s, s, k, 0), steps_of(u_s, s, k, 1)], axis=1).astype(BF16)
            r = _dot(lhs, bw2_ref[k])
            for q in range(S5_BLK_SLABS):
                st_s[S5_BLK_SLABS * k + q, rows, :] = r[:, q * LANES:(q + 1) * LANES]
                st_s[half + S5_BLK_SLABS * k + q, rows, :] = r[:, (S5_BLK_SLABS + q) * LANES:
                                                                (S5_BLK_SLABS + q + 1) * LANES]

    def recur(s, h):
        st_s[:, s * reg:s * reg + n_b, :] = h
        for j in range(n_pair):
            rows = slice(s * reg + (j + 1) * n_b, s * reg + (j + 2) * n_b)
            inc = st_s[:, rows, :]
            ar, ai = a2_ref[:half], a2_ref[half:]
            hr, hi = h[:half], h[half:]
            h = jnp.concatenate([ar * hr - ai * hi + inc[:half], ar * hi + ai * hr + inc[half:]], axis=0)
            st_s[:, rows, :] = h
        return h

    def read_out(s):
        rows = slice(s * reg, s * reg + rows_pair)
        ys = []
        for k in range(S5_NB):
            lhs = jnp.concatenate(
                [st_s[S5_BLK_SLABS * k + q, rows, :] for q in range(S5_BLK_SLABS)]
                + [st_s[half + S5_BLK_SLABS * k + q, rows, :] for q in range(S5_BLK_SLABS)]
                + [steps_of(u_s, s, k, 0), steps_of(u_s, s, k, 1)], axis=1).astype(BF16)
            y2 = _dot(lhs, cw2_ref[k])
            ys.append(jnp.concatenate(
                [y2[j * n_b:(j + 1) * n_b, par * LANES:(par + 1) * LANES] for j in range(n_pair) for par in range(2)],
                axis=0))
        u_t = jnp.concatenate([u_s[s, q] for q in range(q_slabs)], axis=1)
        y = jax.nn.gelu(jnp.concatenate(ys, axis=1) + d_ref[...] * u_t)
        y = y * jax.nn.sigmoid(_dot(y.astype(BF16), wglu_ref[...]))
        for q in range(q_slabs):
            y_s[s, q] = y[:, q * LANES:(q + 1) * LANES]
        for b in range(n_b):
            yg_ref[b, s * sub:(s + 1) * sub, :] = jnp.concatenate(
                [y_s[s, q, pl.ds(b, sub, stride=n_b), :] for q in range(q_slabs)], axis=1).astype(BF16)

    h = h_s[...]
    project(0)
    for s in range(n_sub):
        if s + 1 < n_sub:
            project(s + 1)
        h = recur(s, h)
        if s >= 1:
            read_out(s - 1)
    read_out(n_sub - 1)
    h_s[...] = h
    hre_ref[...] = jnp.concatenate([h[s] for s in range(half)], axis=1)
    him_ref[...] = jnp.concatenate([h[half + s] for s in range(half)], axis=1)


def _s5_prompt(x3, w_in, bw2, a2_b, cw2, d_row, w_glu):
    batch, seq, _ = x3.shape
    seg = S5_L
    n_sub = seg // S5_SUB
    slab_rows = (n_sub, S5_WIDTH // LANES, S5_SUB * batch, LANES)
    return pl.pallas_call(
        _s5_prompt_kernel,
        grid=(seq // seg,),
        in_specs=[pl.BlockSpec((batch, seg, D_MODEL), lambda c: (0, c, 0)),
                  _const_spec(w_in.shape), _const_spec(bw2.shape), _const_spec(a2_b.shape),
                  _const_spec(cw2.shape), _const_spec(d_row.shape), _const_spec(w_glu.shape)],
        out_specs=[pl.BlockSpec((batch, seg, S5_WIDTH), lambda c: (0, c, 0)),
                   pl.BlockSpec((batch, S5_STATES), lambda c: (0, 0)),
                   pl.BlockSpec((batch, S5_STATES), lambda c: (0, 0))],
        out_shape=[jax.ShapeDtypeStruct((batch, seq, S5_WIDTH), BF16),
                   jax.ShapeDtypeStruct((batch, S5_STATES), F32),
                   jax.ShapeDtypeStruct((batch, S5_STATES), F32)],
        scratch_shapes=[pltpu.VMEM(slab_rows, F32), pltpu.VMEM(slab_rows, F32),
                        pltpu.VMEM((S5_SLABS, n_sub * (S5_SUB // 2 + 1) * batch, LANES), F32),
                        pltpu.VMEM((S5_SLABS, batch, LANES), F32)],
        compiler_params=_cparams(("arbitrary",)),
        name="s5_prompt",
    )(x3, w_in, bw2, a2_b, cw2, d_row, w_glu)


def _s5_sample_kernel(x_ref, h0r_ref, h0i_ref, win_ref, bw_ref, ar_ref, ai_ref, cw_ref, d_ref, wglu_ref,
                      yg_ref, hre_ref, him_ref):
    u = _dot(x_ref[...].astype(BF16), win_ref[...])
    ub = u.astype(BF16)
    bu = [_dot(ub[:, k * LANES:(k + 1) * LANES], bw_ref[k]) for k in range(S5_NB)]
    w = S5_KB * S5_STATE
    bu_r = jnp.concatenate([r[:, :w] for r in bu], axis=1)
    bu_i = jnp.concatenate([r[:, w:] for r in bu], axis=1)
    ar, ai, h0r, h0i = ar_ref[...], ai_ref[...], h0r_ref[...], h0i_ref[...]
    hr = bu_r + (ar * h0r - ai * h0i)
    hi = bu_i + (ar * h0i + ai * h0r)
    hre_ref[...] = hr
    him_ref[...] = hi
    blocks = [jnp.concatenate([hr[:, k * w:(k + 1) * w], hi[:, k * w:(k + 1) * w]], axis=1).astype(BF16)
              for k in range(S5_NB)]
    yg_ref[...] = _s5_readout(blocks, u, cw_ref, d_ref, wglu_ref).astype(BF16)


def _s5_sample(xs, h0r, h0i, w_in, bw, a_r, a_i, cw, d_row, w_glu):
    n = xs.shape[0]
    args = (xs, h0r, h0i, w_in, bw, a_r, a_i, cw, d_row, w_glu)
    full = lambda shape: pl.BlockSpec(shape, lambda i: (0,) * len(shape))
    return pl.pallas_call(
        _s5_sample_kernel,
        grid=(1,),
        in_specs=[full(a.shape) for a in args],
        out_specs=[full((n, S5_WIDTH)), full((n, S5_STATES)), full((n, S5_STATES))],
        out_shape=[jax.ShapeDtypeStruct((n, S5_WIDTH), BF16),
                   jax.ShapeDtypeStruct((n, S5_STATES), F32),
                   jax.ShapeDtypeStruct((n, S5_STATES), F32)],
        compiler_params=_cparams(("arbitrary",)),
        name="s5_sample",
    )(*args)


def _rope_tables(pos):
    freqs = ROPE_THETA ** (-jnp.arange(ROPE_HALF, dtype=F32) / ROPE_HALF)
    ang = pos.astype(F32)[:, None] * freqs
    cos, sin = jnp.cos(ang), jnp.sin(ang)
    zero = jnp.zeros_like(cos)
    pad = jnp.zeros((pos.shape[0], LANES - ROPE_DIM), F32)
    cos_t = jnp.concatenate([cos, cos, pad], axis=1)
    sin_a = jnp.concatenate([zero, sin, pad], axis=1)
    sin_b = jnp.concatenate([-sin, zero, pad], axis=1)
    return cos_t, sin_a, sin_b


def _block_diag(blocks):
    *lead, n, r, c = blocks.shape
    eye = jnp.eye(n, dtype=blocks.dtype)
    return jnp.einsum('...nrc,nm->...nrmc', blocks, eye).reshape(*lead, n * r, n * c)


def _even_params(w_in, q_norm, w_q_b, kv_norm, w_kv_b, gm_g, gm_b, gm_w_s, gm_b_s, w_out):
    i1, i2, i3, i4 = Q_RANK, Q_RANK + KV_RANK, Q_RANK + KV_RANK + ROPE_DIM, Q_RANK + KV_RANK + ROPE_DIM + GM_WIDTH
    w_in_p = jnp.concatenate([w_in[:, :i2], w_in[:, i3:], w_in[:, i2:i3],
                              jnp.zeros((D_MODEL, LANES - ROPE_DIM), F32)], axis=1).astype(BF16)
    wq_nope = w_q_b[:, :, :NOPE_DIM].reshape(Q_RANK, MLA_HEADS * NOPE_DIM).astype(BF16)
    pe = w_q_b[:, :, NOPE_DIM:]
    pe_sw = jnp.concatenate([pe[:, :, ROPE_HALF:], pe[:, :, :ROPE_HALF]], axis=2)
    lane_pad = jnp.zeros((Q_RANK, MLA_HEADS, LANES - ROPE_DIM), F32)
    wq_pe = jnp.concatenate(
        [jnp.concatenate([w, lane_pad], axis=2).reshape(Q_RANK, MLA_HEADS * LANES) for w in (pe, pe_sw)],
        axis=1).astype(BF16)
    uk = jnp.transpose(w_kv_b[:, :, :NOPE_DIM], (1, 2, 0))
    uv = jnp.transpose(w_kv_b[:, :, NOPE_DIM:], (1, 0, 2))
    pairs = MLA_HEADS // 2
    w_uk = _block_diag(uk.reshape(pairs, 2, NOPE_DIM, KV_RANK)).astype(BF16)
    w_uv_p = _block_diag(uv.reshape(pairs, 2, KV_RANK, V_DIM)).astype(BF16)
    w_uv = _block_diag(uv).astype(BF16)
    causal = jnp.tril(jnp.ones((GM_CHUNK, GM_CHUNK), dtype=bool))
    ws_m = jnp.where(causal[None], gm_w_s, 0).astype(BF16)
    bs_full = jnp.repeat(gm_b_s.T, GM_HEAD_DIM, axis=1)
    ws0 = jnp.repeat(gm_w_s[:, 0, 0], GM_HEAD_DIM)[None, :].astype(BF16)
    bs0 = bs_full[:1]
    n_attn = MLA_HEADS * V_DIM
    return dict(w_in_p=w_in_p, qn=q_norm[None, :], wq_nope=wq_nope, wq_pe=wq_pe, w_uk=w_uk, w_uv=w_uv, w_uv_p=w_uv_p,
                kvn=kv_norm[None, :], gm_g=gm_g[None, :], gm_b=gm_b[None, :], ws_m=ws_m, bs_full=bs_full,
                ws0=ws0, bs0=bs0, wo_attn=w_out[:n_attn].astype(BF16), wo_gate=w_out[n_attn:].astype(BF16))


def _s5_params(w_in, a_re, a_im, b_re, b_im, c_re, c_im, d, log_dt, w_glu, w_out, batch):
    dt = jnp.exp(log_dt)[:, None]
    ld_r, ld_i = a_re * dt, a_im * dt
    mag = jnp.exp(ld_r)
    ab_r, ab_i = mag * jnp.cos(ld_i), mag * jnp.sin(ld_i)
    den = a_re * a_re + a_im * a_im
    cr = ((ab_r - 1.0) * a_re + ab_i * a_im) / den
    ci = (ab_i * a_re - (ab_r - 1.0) * a_im) / den
    bb_r = cr[..., None] * b_re - ci[..., None] * b_im
    bb_i = cr[..., None] * b_im + ci[..., None] * b_re

    def in_blocks(bb):
        return _block_diag(jnp.swapaxes(bb, 1, 2).reshape(S5_NB, S5_KB, S5_GROUP_DIM, S5_STATE))

    def out_blocks(cc):
        return _block_diag(jnp.swapaxes(cc, 1, 2).reshape(S5_NB, S5_KB, S5_STATE, S5_GROUP_DIM))

    bw = jnp.concatenate([in_blocks(bb_r), in_blocks(bb_i)], axis=2).astype(BF16)
    cw = jnp.concatenate([out_blocks(c_re), out_blocks(-c_im)], axis=1).astype(BF16)
    a_r, a_i = ab_r.reshape(1, S5_STATES), ab_i.reshape(1, S5_STATES)
    half = S5_SLABS // 2
    cmul = lambda xr, xi, yr, yi: (xr * yr - xi * yi, xr * yi + xi * yr)
    a2_r, a2_i = cmul(ab_r, ab_i, ab_r, ab_i)
    abb_r, abb_i = cmul(ab_r[..., None], ab_i[..., None], bb_r, bb_i)
    ca_r, ca_i = cmul(c_re, c_im, ab_r[:, None, :], ab_i[:, None, :])
    ca2_r, ca2_i = cmul(c_re, c_im, a2_r[:, None, :], a2_i[:, None, :])
    cb = jnp.einsum('gcp,gpd->gcd', c_re, bb_r) - jnp.einsum('gcp,gpd->gcd', c_im, bb_i)
    cab = jnp.einsum('gcp,gpd->gcd', ca_r, bb_r) - jnp.einsum('gcp,gpd->gcd', ca_i, bb_i)
    io_blocks = lambda m: _block_diag(jnp.swapaxes(m, 1, 2).reshape(S5_NB, S5_KB, S5_GROUP_DIM, S5_GROUP_DIM))
    bw2 = jnp.concatenate([jnp.concatenate([in_blocks(abb_r), in_blocks(abb_i)], axis=2),
                           jnp.concatenate([in_blocks(bb_r), in_blocks(bb_i)], axis=2)],
                          axis=1).astype(BF16)
    cw2 = jnp.concatenate(
        [jnp.concatenate([out_blocks(ca_r), out_blocks(ca2_r)], axis=2),
         jnp.concatenate([out_blocks(-ca_i), out_blocks(-ca2_i)], axis=2),
         jnp.concatenate([io_blocks(cb), io_blocks(cab)], axis=2),
         jnp.concatenate([jnp.zeros((S5_NB, LANES, LANES), F32), io_blocks(cb)], axis=2)],
        axis=1).astype(BF16)
    a2_b = jnp.concatenate([jnp.broadcast_to(a2_r.reshape(half, 1, LANES), (half, batch, LANES)),
                            jnp.broadcast_to(a2_i.reshape(half, 1, LANES), (half, batch, LANES))], axis=0)
    return dict(w_in=w_in.astype(BF16), bw=bw, cw=cw, a_r=a_r, a_i=a_i, bw2=bw2, cw2=cw2, a2_b=a2_b,
                d=d.reshape(1, S5_WIDTH), w_glu=w_glu.astype(BF16), w_out=w_out.astype(BF16))


def kernel(x_prompt, x_sample, cache_ckv, cache_kpe, state_s5_re, state_s5_im, page_table, w_in_even, mla_q_norm, mla_w_q_b, mla_kv_norm, mla_w_kv_b, gm_norm_g, gm_norm_b, gm_w_s, gm_b_s, w_out_even, w_in_odd, s5_a_re, s5_a_im, s5_b_re, s5_b_im, s5_c_re, s5_c_im, s5_d, s5_log_dt, s5_w_glu, w_out_odd, ln_mix_g, ln_mix_b, ln_ffn_g, ln_ffn_b, ffn_w1, ffn_w2):
    batch, seq, _ = x_prompt.shape
    dec_batch, dec_seq, _ = x_sample.shape
    assert dec_seq == 1 and seq % INPROJ_TM == 0 and seq % ATTN_TQ == 0 and seq % S5_L == 0
    assert (seq // ATTN_TQ) % 2 == 0
    assert (batch * seq) % FFN_TM == 0 and page_table.shape[1] % DEC_PG == 0
    assert batch == SUBLANES

    xp = x_prompt.reshape(batch * seq, D_MODEL)
    xs = x_sample.reshape(dec_batch, D_MODEL)
    tabs_p = _rope_tables(jnp.arange(seq, dtype=jnp.int32))
    tabs_s = _rope_tables(PAST_LEN + jnp.arange(dec_seq, dtype=jnp.int32))

    outs = {k: [] for k in ("ckv_p", "kpe_p", "ckv_s", "kpe_s", "gmv_p", "gmv_s",
                            "s5re_p", "s5im_p", "s5re_s", "s5im_s")}
    for layer in range(DEPTH):
        ln = (ln_mix_g[layer][None, :], ln_mix_b[layer][None, :],
              ln_ffn_g[layer][None, :], ln_ffn_b[layer][None, :])
        if layer % 2 == 0:
            e = layer // 2
            p = _even_params(w_in_even[e], mla_q_norm[e], mla_w_q_b[e], mla_kv_norm[e], mla_w_kv_b[e],
                             gm_norm_g[e], gm_norm_b[e], gm_w_s[e], gm_b_s[e], w_out_even[e])
            q_a, ckv, kpe, kcat, gate, gmv = _even_inproj(
                xp, p["w_in_p"], p["kvn"], p["gm_g"], p["gm_b"], p["ws_m"], p["bs_full"], *tabs_p, batch, seq)
            attn = _mla_attn(q_a, kcat, p["qn"], p["wq_nope"], p["wq_pe"], p["w_uk"], p["w_uv_p"], *tabs_p,
                             batch, seq)
            outs["ckv_p"].append(ckv.reshape(batch, seq, KV_RANK))
            outs["kpe_p"].append(jnp.swapaxes(kpe, 1, 2))
            outs["gmv_p"].append(gmv)
            ckv_s, kpe_s, gate_s, vn_s, qlat_s, qpe_s = _even_inproj_sample(
                xs, p["w_in_p"], p["kvn"], p["gm_g"], p["gm_b"], p["ws0"], p["bs0"], *tabs_s,
                p["qn"], p["wq_nope"], p["wq_pe"], p["w_uk"])
            kpe_pad = jnp.concatenate([kpe_s, jnp.zeros((dec_batch, LANES - ROPE_DIM), F32)], axis=1)
            o_lat = _decode_attn(page_table,
                                 qlat_s.reshape(dec_batch * MLA_HEADS, LANES),
                                 qpe_s.reshape(dec_batch * MLA_HEADS, LANES),
                                 ckv_s.reshape(dec_batch, 1, KV_RANK), kpe_pad.reshape(dec_batch, 1, LANES),
                                 cache_ckv[e], jnp.swapaxes(cache_kpe[e], 1, 2))
            xs, w1_b, w2_b = _mix_ffn_sample(
                xs, [o_lat.reshape(dec_batch, MLA_HEADS * KV_RANK), gate_s], p["w_uv"],
                [p["wo_attn"], p["wo_gate"]], *ln, ffn_w1, ffn_w2, layer, name="even_ffn_sample")
            xp = _mix_ffn(xp, [attn, gate], None, [p["wo_attn"], p["wo_gate"]], *ln, w1_b, w2_b,
                          name="even_ffn_prompt")
            outs["ckv_s"].append(ckv_s.reshape(dec_batch, 1, KV_RANK))
            outs["kpe_s"].append(kpe_s.reshape(dec_batch, 1, ROPE_DIM))
            outs["gmv_s"].append(vn_s.reshape(dec_batch, 1, GM_WIDTH))
        else:
            o = layer // 2
            p = _s5_params(w_in_odd[o], s5_a_re[o], s5_a_im[o], s5_b_re[o], s5_b_im[o], s5_c_re[o],
                           s5_c_im[o], s5_d[o], s5_log_dt[o], s5_w_glu[o], w_out_odd[o], batch)
            yg, hre, him = _s5_prompt(xp.reshape(batch, seq, D_MODEL), p["w_in"], p["bw2"], p["a2_b"],
                                      p["cw2"], p["d"], p["w_glu"])
            outs["s5re_p"].append(hre.reshape(batch, S5_GROUPS, S5_STATE))
            outs["s5im_p"].append(him.reshape(batch, S5_GROUPS, S5_STATE))
            yg_s, hre_s, him_s = _s5_sample(
                xs, state_s5_re[o].reshape(dec_batch, S5_STATES), state_s5_im[o].reshape(dec_batch, S5_STATES),
                p["w_in"], p["bw"], p["a_r"], p["a_i"], p["cw"], p["d"], p["w_glu"])
            xs, w1_b, w2_b = _mix_ffn_sample(xs, [yg_s], None, [p["w_out"]], *ln, ffn_w1, ffn_w2, layer,
                                             name="odd_ffn_sample")
            xp = _mix_ffn(xp, [yg.reshape(batch * seq, S5_WIDTH)], None, [p["w_out"]], *ln, w1_b, w2_b,
                          name="odd_ffn_prompt")
            outs["s5re_s"].append(hre_s.reshape(dec_batch, S5_GROUPS, S5_STATE))
            outs["s5im_s"].append(him_s.reshape(dec_batch, S5_GROUPS, S5_STATE))

    st = jnp.stack
    return (xp.reshape(batch, seq, D_MODEL), xs.reshape(dec_batch, dec_seq, D_MODEL),
            st(outs["ckv_p"]), st(outs["kpe_p"]), st(outs["ckv_s"]), st(outs["kpe_s"]),
            st(outs["gmv_p"]), st(outs["gmv_s"]),
            st(outs["s5re_p"]), st(outs["s5im_p"]), st(outs["s5re_s"]), st(outs["s5im_s"]))
```

```python
import functools
import math

import jax
import jax.numpy as jnp
from jax import lax
from jax.experimental import pallas as pl
from jax.experimental.pallas import tpu as pltpu

F32 = jnp.float32
BF16 = jnp.bfloat16

D_MODEL = 1024
DEPTH = 2
PAST_LEN = 16384
PAGE_SIZE = 128
MLA_HEADS = 8
NOPE_DIM = 64
ROPE_DIM = 32
ROPE_HALF = ROPE_DIM // 2
V_DIM = 64
Q_RANK = 256
KV_RANK = 128
ROPE_THETA = 10000.0
ATTN_SCALE = 1.0 / math.sqrt(NOPE_DIM + ROPE_DIM)
ATTN_SCALE_LOG2 = ATTN_SCALE * math.log2(math.e)
GM_HEADS = 8
GM_HEAD_DIM = 64
GM_WIDTH = GM_HEADS * GM_HEAD_DIM
GM_CHUNK = 128
S5_GROUP_DIM = 16
S5_GROUPS = 32
S5_STATE = 64
S5_WIDTH = S5_GROUPS * S5_GROUP_DIM
S5_STATES = S5_GROUPS * S5_STATE
FFN_HIDDEN = 4 * D_MODEL
DN_ALPHA = (2 * DEPTH) ** 0.25
LN_EPS = 1e-5
RMS_EPS = 1e-6

LANES = 128
SUBLANES = 8
VMEM_LIMIT_BYTES = 56 * 1024 * 1024

INPROJ_TM = 512
INPROJ_ROW_BLOCK = 256
ATTN_TQ = 256
ATTN_ROW_BLOCK = 256
FFN_TM = 1024
FFN_HC = 1024
FFN_ROW_BLOCK = 256
S5_L = 128
S5_SUB = 32
DEC_PG = 32
DEC_SLOTS = 4
DEC_AHEAD = 2
DEC_SPAN_PAGES = 4
S5_KB = 8
S5_NB = S5_GROUPS // S5_KB
S5_SLABS = 2 * S5_STATES // LANES
S5_BLK_SLABS = S5_KB * S5_STATE // LANES


def _cparams(sem, n_fusible_inputs=0):
    fuse = dict(allow_input_fusion=[True] * n_fusible_inputs) if n_fusible_inputs else {}
    return pltpu.CompilerParams(dimension_semantics=sem, vmem_limit_bytes=VMEM_LIMIT_BYTES, **fuse)


def _const_spec(shape):
    n = len(shape)
    return pl.BlockSpec(shape, lambda *_: (0,) * n, pipeline_mode=pl.Buffered(1))


def _layer_norm(x, g, b):
    mu = jnp.mean(x, axis=-1, keepdims=True)
    xc = x - mu
    var = jnp.mean(xc * xc, axis=-1, keepdims=True)
    return xc * lax.rsqrt(var + LN_EPS) * g + b


def _rms_norm(x, g):
    return x * lax.rsqrt(jnp.mean(x * x, axis=-1, keepdims=True) + RMS_EPS) * g


def _rope128(x, cos_t, sin_a, sin_b):
    return (x * cos_t + pltpu.roll(x, ROPE_HALF, 1) * sin_a
            + pltpu.roll(x, LANES - ROPE_HALF, 1) * sin_b)


def _dot(a, b):
    return jnp.dot(a, b, preferred_element_type=F32)


def _dot_nt(a, b):
    return lax.dot_general(a, b, (((1,), (1,)), ((), ())), preferred_element_type=F32)


def _even_inproj_kernel(x_ref, w_ref, kvn_ref, gmg_ref, gmb_ref, ws_ref, bs_ref,
                        cos_ref, sina_ref, sinb_ref,
                        qa_ref, ckv_ref, kpe_ref, kcat_ref, gate_ref, gmv_ref):
    tm = x_ref.shape[0]
    o_c, o_u, o_v, o_k = Q_RANK, Q_RANK + KV_RANK, Q_RANK + KV_RANK + GM_WIDTH, Q_RANK + KV_RANK + 2 * GM_WIDTH
    rb = min(INPROJ_ROW_BLOCK, tm)
    blocks = [slice(r0, r0 + rb) for r0 in range(0, tm, rb)]
    zs = [_dot(x_ref[rows, :].astype(BF16), w_ref[...]) for rows in blocks]

    gus, vns = [], []
    for rows, z in zip(blocks, zs):
        qa_ref[rows, :] = z[:, :o_c]
        c_n = _rms_norm(z[:, o_c:o_u], kvn_ref[...])
        ckv_ref[rows, :] = c_n
        kp = _rope128(z[:, o_k:o_k + LANES], cos_ref[rows, :], sina_ref[rows, :], sinb_ref[rows, :])
        kpe_ref[:, rows] = kp.T[:ROPE_DIM, :]
        kcat_ref[rows, :] = jnp.concatenate([c_n, kp], axis=1).astype(BF16)
        gus.append(jax.nn.gelu(z[:, o_u:o_v]))
        vns.append(_layer_norm(jax.nn.gelu(z[:, o_v:o_k]), gmg_ref[...], gmb_ref[...]))

    gmv_ref[0] = vns[-1][rb - GM_CHUNK:, :]

    lane = lax.broadcasted_iota(jnp.int32, (GM_CHUNK, LANES), 1)
    lo = lane < GM_HEAD_DIM
    bs = bs_ref[...]
    for rows, gu, v_n in zip(blocks, gus, vns):
        for ci in range(rb // GM_CHUNK):
            chunk = slice(ci * GM_CHUNK, (ci + 1) * GM_CHUNK)
            pieces = []
            for pr in range(GM_WIDTH // LANES):
                r = v_n[chunk, pr * LANES:(pr + 1) * LANES]
                m0 = _dot(ws_ref[2 * pr], jnp.where(lo, r, 0.0).astype(BF16))
                m1 = _dot(ws_ref[2 * pr + 1], jnp.where(lo, 0.0, r).astype(BF16))
                pieces.append(m0 + m1)
            mixed = jnp.concatenate(pieces, axis=1) + bs
            gate_ref[pl.ds(rows.start + ci * GM_CHUNK, GM_CHUNK), :] = (gu[chunk, :] * mixed).astype(BF16)


def _even_inproj(x2d, w_in_p, kv_norm, gm_g, gm_b, ws_m, bs_full, cos_t, sin_a, sin_b, batch, seq):
    n_tok = x2d.shape[0]
    tm = INPROJ_TM
    tiles_per_seq = seq // tm
    row = lambda w: pl.BlockSpec((tm, w), lambda i: (i, 0))
    tab = pl.BlockSpec((tm, LANES), lambda i: (i % tiles_per_seq, 0))
    n_in = w_in_p.shape[1]
    return pl.pallas_call(
        _even_inproj_kernel,
        grid=(n_tok // tm,),
        in_specs=[row(D_MODEL), _const_spec((D_MODEL, n_in)), _const_spec((1, KV_RANK)),
                  _const_spec((1, GM_WIDTH)), _const_spec((1, GM_WIDTH)),
                  _const_spec((GM_HEADS, GM_CHUNK, GM_CHUNK)), _const_spec((GM_CHUNK, GM_WIDTH)),
                  tab, tab, tab],
        out_specs=[row(Q_RANK), row(KV_RANK),
                   pl.BlockSpec((None, ROPE_DIM, tm), lambda i: (i // tiles_per_seq, 0, i % tiles_per_seq)),
                   row(2 * LANES), row(GM_WIDTH),
                   pl.BlockSpec((1, GM_CHUNK, GM_WIDTH), lambda i: (i // tiles_per_seq, 0, 0))],
        out_shape=[jax.ShapeDtypeStruct((n_tok, Q_RANK), F32),
                   jax.ShapeDtypeStruct((n_tok, KV_RANK), F32),
                   jax.ShapeDtypeStruct((batch, ROPE_DIM, seq), F32),
                   jax.ShapeDtypeStruct((n_tok, 2 * LANES), BF16),
                   jax.ShapeDtypeStruct((n_tok, GM_WIDTH), BF16),
                   jax.ShapeDtypeStruct((batch, GM_CHUNK, GM_WIDTH), F32)],
        compiler_params=_cparams(("arbitrary",), n_fusible_inputs=10),
        name="even_inproj",
    )(x2d, w_in_p, kv_norm, gm_g, gm_b, ws_m, bs_full, cos_t, sin_a, sin_b)


def _queries(q_a, qn_g, wq_nope, wq_pe, wuk_ref, cos_t, sin_a, sin_b):
    qn = _rms_norm(q_a, qn_g).astype(BF16)
    nope = _dot(qn, wq_nope).astype(BF16)
    lat = jnp.concatenate([_dot(nope[:, pr * LANES:(pr + 1) * LANES], wuk_ref[pr])
                           for pr in range(MLA_HEADS // 2)], axis=1)
    pe = _dot(qn, wq_pe)
    n = MLA_HEADS * LANES
    sin_s = sin_a + sin_b
    pes = [pe[:, h * LANES:(h + 1) * LANES] * cos_t + pe[:, n + h * LANES:n + (h + 1) * LANES] * sin_s
           for h in range(MLA_HEADS)]
    return lat, pes


def _mla_attn_kernel(qa_lo_ref, qa_hi_ref, kcat_ref, qn_ref, wqn_ref, wqp_ref, wuk_ref, wuv_ref,
                     cos_lo_ref, sina_lo_ref, sinb_lo_ref, cos_hi_ref, sina_hi_ref, sinb_hi_ref,
                     out_hbm, qcat_s, m_s, acc_s, obuf, osem, *, nq):
    tq = qa_lo_ref.shape[0]
    n_rows = MLA_HEADS * tq
    b, i = pl.program_id(0), pl.program_id(1)
    step = b * pl.num_programs(1) + i
    n_steps = pl.num_programs(0) * pl.num_programs(1)
    q_tiles = (i, nq - 1 - i)

    def out_copies():
        return [pltpu.make_async_copy(
            obuf.at[t], out_hbm.at[pl.ds(pl.multiple_of((b * nq + q_tiles[t]) * tq, tq), tq), :], osem.at[t])
            for t in range(2)]

    qa = (qa_lo_ref, qa_hi_ref)
    tabs = ((cos_lo_ref, sina_lo_ref, sinb_lo_ref), (cos_hi_ref, sina_hi_ref, sinb_hi_ref))
    for t in range(2):
        lat, pes = _queries(qa[t][...], qn_ref[...], wqn_ref[...], wqp_ref[...], wuk_ref,
                            *(r[...] for r in tabs[t]))
        for h in range(MLA_HEADS):
            qcat_s[t, h * tq:(h + 1) * tq, :] = (ATTN_SCALE_LOG2 * jnp.concatenate(
                [lat[:, h * LANES:(h + 1) * LANES], pes[h]], axis=1)).astype(BF16)

    lanes_x = lambda a, n: jnp.concatenate([a] * n, axis=1)
    rb = ATTN_ROW_BLOCK
    ones = jnp.ones((tq, LANES), BF16)

    def keys(kv_tile):
        k = kcat_ref[pl.ds(pl.multiple_of(kv_tile * tq, tq), tq), :]
        return k, jnp.concatenate([k[:, :KV_RANK], ones], axis=1)

    q_pos = lax.broadcasted_iota(jnp.int32, (rb, tq), 0) & (tq - 1)
    k_pos = lax.broadcasted_iota(jnp.int32, (rb, tq), 1)
    causal = k_pos <= q_pos
    for t in range(2):
        k, v1 = keys(q_tiles[t])
        for r0 in range(0, n_rows, rb):
            rows = slice(r0, r0 + rb)
            s = jnp.where(causal, _dot_nt(qcat_s[t, rows, :], k), -jnp.inf)
            m0 = jnp.broadcast_to(jnp.max(s, axis=-1, keepdims=True), (rb, LANES))
            m_s[t, rows, :] = m0
            acc_s[t, rows, :] = _dot(jnp.exp2(s - lanes_x(m0, tq // LANES)).astype(BF16), v1)

    for u in range(nq - 1):
        hi = u >= i
        t = hi.astype(jnp.int32)
        k, v1 = keys(jnp.where(hi, u - i, u))
        for r0 in range(0, n_rows, rb):
            rows = slice(r0, r0 + rb)
            s = _dot_nt(qcat_s[t, rows, :], k)
            m_old = m_s[t, rows, :]
            m_new = jnp.maximum(m_old, jnp.max(s, axis=-1, keepdims=True))
            alpha = jnp.exp2(m_old - m_new)
            p = jnp.exp2(s - lanes_x(m_new, tq // LANES))
            acc_s[t, rows, :] = lanes_x(alpha, 2) * acc_s[t, rows, :] + _dot(p.astype(BF16), v1)
            m_s[t, rows, :] = m_new

    @pl.when(step > 0)
    def _():
        for cp in out_copies():
            cp.wait()

    for t in range(2):
        o_all = jnp.concatenate(
            [acc_s[t, h * tq:(h + 1) * tq, :KV_RANK] / acc_s[t, h * tq:(h + 1) * tq, KV_RANK:]
             for h in range(MLA_HEADS)], axis=1).astype(BF16)
        obuf[t] = jnp.concatenate(
            [_dot(o_all[:, pr * 2 * LANES:(pr + 1) * 2 * LANES], wuv_ref[pr]) for pr in range(MLA_HEADS // 2)],
            axis=1).astype(BF16)
    for cp in out_copies():
        cp.start()

    @pl.when(step == n_steps - 1)
    def _():
        for cp in out_copies():
            cp.wait()


def _mla_attn(q_a, kcat, qn_g, wq_nope, wq_pe, w_uk, w_uv, cos_t, sin_a, sin_b, batch, seq):
    tq = ATTN_TQ
    nq = seq // tq
    n_attn = MLA_HEADS * V_DIM
    lo = lambda b, i: i
    hi = lambda b, i: nq - 1 - i
    tab = lambda sel: pl.BlockSpec((tq, LANES), lambda b, i: (sel(b, i), 0))
    qa_spec = lambda sel: pl.BlockSpec((tq, Q_RANK), lambda b, i: (b * nq + sel(b, i), 0))
    return pl.pallas_call(
        functools.partial(_mla_attn_kernel, nq=nq),
        grid=(batch, nq // 2),
        in_specs=[qa_spec(lo), qa_spec(hi),
                  pl.BlockSpec((None, seq, 2 * LANES), lambda b, i: (b, 0, 0)),
                  _const_spec((1, Q_RANK)), _const_spec(wq_nope.shape), _const_spec(wq_pe.shape),
                  _const_spec(w_uk.shape), _const_spec(w_uv.shape),
                  tab(lo), tab(lo), tab(lo), tab(hi), tab(hi), tab(hi)],
        out_specs=pl.BlockSpec(memory_space=pl.ANY),
        out_shape=jax.ShapeDtypeStruct((batch * seq, n_attn), BF16),
        scratch_shapes=[pltpu.VMEM((2, MLA_HEADS * tq, 2 * LANES), BF16),
                        pltpu.VMEM((2, MLA_HEADS * tq, LANES), F32),
                        pltpu.VMEM((2, MLA_HEADS * tq, 2 * KV_RANK), F32),
                        pltpu.VMEM((2, tq, n_attn), BF16),
                        pltpu.SemaphoreType.DMA((2,))],
        compiler_params=_cparams(("arbitrary", "arbitrary"), n_fusible_inputs=14),
        name="mla_attn",
    )(q_a, q_a, kcat.reshape(batch, seq, 2 * LANES), qn_g, wq_nope, wq_pe, w_uk, w_uv,
      cos_t, sin_a, sin_b, cos_t, sin_a, sin_b)


def _mix_ffn_kernel(*refs, n_act, has_pre):
    x_ref = refs[0]
    act_refs = refs[1:1 + n_act]
    k = 1 + n_act
    pre_ref = refs[k] if has_pre else None
    k += int(has_pre)
    wo_refs = refs[k:k + n_act]
    k += n_act
    g1_ref, b1_ref, g2_ref, b2_ref, w1_ref, w2_ref, out_ref = refs[k:k + 7]

    tm = x_ref.shape[0]
    rb = min(FFN_ROW_BLOCK, tm)
    blocks = [slice(r0, r0 + rb) for r0 in range(0, tm, rb)]
    fs = []
    for rows in blocks:
        f = None
        for idx in range(n_act):
            a = act_refs[idx][rows, :]
            if has_pre and idx == 0:
                a = _dot(a.astype(BF16), pre_ref[...])
            part = _dot(a.astype(BF16), wo_refs[idx][...])
            f = part if f is None else f + part
        fs.append(f)
    x1s = [_layer_norm(DN_ALPHA * x_ref[rows, :] + f, g1_ref[...], b1_ref[...]) for rows, f in zip(blocks, fs)]
    x1bs = [x1.astype(BF16) for x1 in x1s]
    accs = [None] * len(blocks)
    for c in range(FFN_HIDDEN // FFN_HC):
        cols = slice(c * FFN_HC, (c + 1) * FFN_HC)
        for i, x1b in enumerate(x1bs):
            h = jnp.maximum(_dot(x1b, w1_ref[:, cols]), 0.0)
            part = _dot((h * h).astype(BF16), w2_ref[cols, :])
            accs[i] = part if accs[i] is None else accs[i] + part
    for rows, x1, acc in zip(blocks, x1s, accs):
        out_ref[rows, :] = _layer_norm(DN_ALPHA * x1 + acc, g2_ref[...], b2_ref[...])


def _mix_ffn(x2d, acts, pre_w, wos, ln1_g, ln1_b, ln2_g, ln2_b, w1, w2, name):
    n_tok = x2d.shape[0]
    tm = min(FFN_TM, n_tok)
    row = lambda w: pl.BlockSpec((tm, w), lambda i: (i, 0))
    has_pre = pre_w is not None
    in_specs = [row(D_MODEL)] + [row(a.shape[1]) for a in acts]
    args = [x2d] + list(acts)
    if has_pre:
        in_specs.append(_const_spec(pre_w.shape))
        args.append(pre_w)
    in_specs += [_const_spec(w.shape) for w in wos]
    args += list(wos)
    in_specs += [_const_spec((1, D_MODEL))] * 4 + [_const_spec(w1.shape), _const_spec(w2.shape)]
    args += [ln1_g, ln1_b, ln2_g, ln2_b, w1, w2]
    return pl.pallas_call(
        functools.partial(_mix_ffn_kernel, n_act=len(acts), has_pre=has_pre),
        grid=(n_tok // tm,),
        in_specs=in_specs,
        out_specs=row(D_MODEL),
        out_shape=jax.ShapeDtypeStruct((n_tok, D_MODEL), F32),
        compiler_params=_cparams(("arbitrary",), n_fusible_inputs=len(args)),
        name=name,
    )(*args)


def _mix_ffn_sample_kernel(*refs, n_act, has_pre):
    x_ref = refs[0]
    act_refs = refs[1:1 + n_act]
    k = 1 + n_act
    pre_ref = refs[k] if has_pre else None
    k += int(has_pre)
    wo_refs = refs[k:k + n_act]
    k += n_act
    g1_ref, b1_ref, g2_ref, b2_ref, w1_ref, w2_ref, out_ref, w1b_ref, w2b_ref, x1_s, acc_s = refs[k:k + 11]
    c = pl.program_id(0)

    @pl.when(c == 0)
    def _():
        f = None
        for idx in range(n_act):
            a = act_refs[idx][...]
            if has_pre and idx == 0:
                a = _dot(a.astype(BF16), pre_ref[...])
            part = _dot(a.astype(BF16), wo_refs[idx][...])
            f = part if f is None else f + part
        x1_s[...] = _layer_norm(DN_ALPHA * x_ref[...] + f, g1_ref[...], b1_ref[...])
        acc_s[...] = jnp.zeros(acc_s.shape, F32)

    w1c = w1_ref[...].astype(BF16)
    w2c = w2_ref[...].astype(BF16)
    w1b_ref[...] = w1c
    w2b_ref[...] = w2c
    h = jnp.maximum(_dot(x1_s[...].astype(BF16), w1c), 0.0)
    acc_s[...] += _dot((h * h).astype(BF16), w2c)

    @pl.when(c == pl.num_programs(0) - 1)
    def _():
        out_ref[...] = _layer_norm(DN_ALPHA * x1_s[...] + acc_s[...], g2_ref[...], b2_ref[...])


def _mix_ffn_sample(x2d, acts, pre_w, wos, ln1_g, ln1_b, ln2_g, ln2_b, w1_all, w2_all, layer, name):
    n_tok = x2d.shape[0]
    has_pre = pre_w is not None
    full = lambda a: pl.BlockSpec(a.shape, lambda c: (0,) * a.ndim)
    args = [x2d] + list(acts) + ([pre_w] if has_pre else []) + list(wos) + [ln1_g, ln1_b, ln2_g, ln2_b]
    in_specs = [full(a) for a in args]
    in_specs += [pl.BlockSpec((None, D_MODEL, FFN_HC), lambda c: (layer, 0, c)),
                 pl.BlockSpec((None, FFN_HC, D_MODEL), lambda c: (layer, c, 0))]
    args += [w1_all, w2_all]
    return pl.pallas_call(
        functools.partial(_mix_ffn_sample_kernel, n_act=len(acts), has_pre=has_pre),
        grid=(FFN_HIDDEN // FFN_HC,),
        in_specs=in_specs,
        out_specs=[pl.BlockSpec((n_tok, D_MODEL), lambda c: (0, 0)),
                   pl.BlockSpec((D_MODEL, FFN_HC), lambda c: (0, c)),
                   pl.BlockSpec((FFN_HC, D_MODEL), lambda c: (c, 0))],
        out_shape=[jax.ShapeDtypeStruct((n_tok, D_MODEL), F32),
                   jax.ShapeDtypeStruct((D_MODEL, FFN_HIDDEN), BF16),
                   jax.ShapeDtypeStruct((FFN_HIDDEN, D_MODEL), BF16)],
        scratch_shapes=[pltpu.VMEM((n_tok, D_MODEL), F32), pltpu.VMEM((n_tok, D_MODEL), F32)],
        compiler_params=_cparams(("arbitrary",)),
        name=name,
    )(*args)


def _even_inproj_sample_kernel(x_ref, w_ref, kvn_ref, gmg_ref, gmb_ref, ws0_ref, bs0_ref,
                               cos_ref, sina_ref, sinb_ref, qn_ref, wqn_ref, wqp_ref, wuk_ref,
                               ckv_ref, kpe_ref, gate_ref, vn_ref, qlat_ref, qpe_ref):
    z = _dot(x_ref[...].astype(BF16), w_ref[...])
    o_c, o_u, o_v, o_k = Q_RANK, Q_RANK + KV_RANK, Q_RANK + KV_RANK + GM_WIDTH, Q_RANK + KV_RANK + 2 * GM_WIDTH
    cos_t, sin_a, sin_b = cos_ref[...], sina_ref[...], sinb_ref[...]
    c_n = _rms_norm(z[:, o_c:o_u], kvn_ref[...])
    ckv_ref[...] = c_n
    kp = _rope128(z[:, o_k:o_k + LANES], cos_t, sin_a, sin_b)
    kpe_ref[...] = kp[:, :ROPE_DIM]
    gu = jax.nn.gelu(z[:, o_u:o_v])
    v_n = _layer_norm(jax.nn.gelu(z[:, o_v:o_k]), gmg_ref[...], gmb_ref[...])
    vn_ref[...] = v_n
    mixed = ws0_ref[...].astype(F32) * v_n.astype(BF16).astype(F32) + bs0_ref[...]
    gate_ref[...] = (gu * mixed).astype(BF16)
    lat, pes = _queries(z[:, :o_c], qn_ref[...], wqn_ref[...], wqp_ref[...], wuk_ref,
                        cos_t, sin_a, sin_b)
    qlat_ref[...] = lat
    qpe_ref[...] = jnp.concatenate(pes, axis=1)


def _even_inproj_sample(xs, w_in_p, kv_norm, gm_g, gm_b, ws0, bs0, cos_t, sin_a, sin_b,
                        qn_g, wq_nope, wq_pe, w_uk):
    n = xs.shape[0]
    args = (xs, w_in_p, kv_norm, gm_g, gm_b, ws0, bs0, cos_t, sin_a, sin_b, qn_g, wq_nope, wq_pe, w_uk)
    full = lambda shape: pl.BlockSpec(shape, lambda i: (0,) * len(shape))
    widths = [(KV_RANK, F32), (ROPE_DIM, F32), (GM_WIDTH, BF16), (GM_WIDTH, F32),
              (MLA_HEADS * LANES, F32), (MLA_HEADS * LANES, F32)]
    return pl.pallas_call(
        _even_inproj_sample_kernel,
        grid=(1,),
        in_specs=[full(a.shape) for a in args],
        out_specs=[full((n, w)) for w, _ in widths],
        out_shape=[jax.ShapeDtypeStruct((n, w), dt) for w, dt in widths],
        compiler_params=_cparams(("arbitrary",), n_fusible_inputs=len(args)),
        name="even_inproj_sample",
    )(*args)


def _decode_attn_kernel(pt_ref, qlat_ref, qpe_ref, ckvs_ref, kpes_ref, ckv_hbm, kpe_hbm, out_ref,
                        ckv_buf, kpe_buf, m_buf, sem, *, n_pg, n_groups):
    b = pl.program_id(0)
    n_b = pl.num_programs(0)
    assert n_groups % DEC_SLOTS == 0 and DEC_AHEAD + 2 <= DEC_SLOTS

    def group_copies(seq, grp):
        slot = grp % DEC_SLOTS
        cps = []
        for k in range(n_pg):
            page = pt_ref[seq, grp * n_pg + k]
            keys = pl.ds(k * PAGE_SIZE, PAGE_SIZE)
            cps.append(pltpu.make_async_copy(ckv_hbm.at[page], ckv_buf.at[slot, keys, :], sem.at[0, slot]))
            cps.append(pltpu.make_async_copy(kpe_hbm.at[page], kpe_buf.at[slot, :, keys], sem.at[1, slot]))
        return cps

    @pl.when(b == 0)
    def _():
        for g in range(DEC_AHEAD):
            for cp in group_copies(0, g):
                cp.start()

    heads, span = MLA_HEADS, DEC_SPAN_PAGES * PAGE_SIZE
    half_span = span // 2
    n_span = n_pg // DEC_SPAN_PAGES
    qlat = qlat_ref[...]
    zero = jnp.zeros_like(qlat)
    q2 = jnp.concatenate([jnp.concatenate([qlat, zero], axis=1),
                          jnp.concatenate([zero, qlat], axis=1)], axis=0).astype(BF16)
    qpe = qpe_ref[...][:, :ROPE_DIM].astype(BF16)

    def scores(slot):
        s_pe = _dot(qpe, kpe_buf[slot].astype(BF16))
        pieces = []
        for j in range(n_span):
            m_buf[slot, j] = jnp.concatenate(
                [ckv_buf[slot, pl.ds(j * span, half_span), :],
                 ckv_buf[slot, pl.ds(j * span + half_span, half_span), :]], axis=1).astype(BF16)
            s2 = _dot_nt(q2, m_buf[slot, j])
            pieces += [s2[:heads] + s_pe[:, j * span:j * span + half_span],
                       s2[heads:] + s_pe[:, j * span + half_span:(j + 1) * span]]
        return jnp.concatenate(pieces, axis=1) * ATTN_SCALE

    def values(p, grp):
        slot = grp % DEC_SLOTS
        o2 = None
        for j in range(n_span):
            lhs = jnp.concatenate([p[:, j * span:j * span + half_span],
                                   p[:, j * span + half_span:(j + 1) * span]], axis=0).astype(BF16)
            part = _dot(lhs, m_buf[slot, j])
            o2 = part if o2 is None else o2 + part
        return o2[:heads, :KV_RANK] + o2[heads:, KV_RANK:]

    state = dict(m_run=None, m=None, l=None, acc=None)
    raw = {}
    probs = {}

    def softmax(k):
        s = raw.pop(k)
        m_cur = jnp.max(s, axis=-1, keepdims=True)
        m_new = m_cur if state["m_run"] is None else jnp.maximum(state["m_run"], m_cur)
        state["m_run"] = m_new
        probs[k] = (jnp.exp(s - m_new), m_new)

    def fold(k):
        p, m_k = probs.pop(k)
        pv = values(p, k)
        p_sum = jnp.sum(p, axis=-1, keepdims=True)
        if state["m"] is None:
            state["l"], state["acc"] = p_sum, pv
        else:
            alpha = jnp.exp(state["m"] - m_k)
            state["l"], state["acc"] = alpha * state["l"] + p_sum, alpha * state["acc"] + pv
        state["m"] = m_k

    for g in range(n_groups):
        if g + DEC_AHEAD < n_groups:
            for cp in group_copies(b, g + DEC_AHEAD):
                cp.start()
        else:
            @pl.when(b + 1 < n_b)
            def _():
                for cp in group_copies(b + 1, g + DEC_AHEAD - n_groups):
                    cp.start()
        for cp in group_copies(b, g):
            cp.wait()
        raw[g] = scores(g % DEC_SLOTS)
        if g >= 1:
            softmax(g - 1)
        if g >= 2:
            fold(g - 2)
    softmax(n_groups - 1)
    fold(n_groups - 2)
    fold(n_groups - 1)
    m, l, acc = state["m"], state["l"], state["acc"]

    r = lambda a: a.astype(BF16).astype(F32)
    kv = r(ckvs_ref[...])
    s_self = (jnp.sum(r(qlat_ref[...]) * kv, axis=-1, keepdims=True)
              + jnp.sum(r(qpe_ref[...]) * r(kpes_ref[...]), axis=-1, keepdims=True)) * ATTN_SCALE
    m_n = jnp.maximum(m, s_self)
    a = jnp.exp(m - m_n)
    p_self = jnp.exp(s_self - m_n)
    out_ref[...] = (a * acc + r(p_self) * kv) / (a * l + p_self)


def _decode_attn(page_table, qlat2, qpe2, ckv_s3, kpe_s3, cache_ckv_e, cache_kpe_t):
    dec_batch, n_pages = page_table.shape
    n_pg = DEC_PG
    qspec = pl.BlockSpec((MLA_HEADS, LANES), lambda b, pt: (b, 0))
    self_spec = pl.BlockSpec((None, 1, LANES), lambda b, pt: (b, 0, 0))
    hbm = pl.BlockSpec(memory_space=pl.ANY)
    grid_spec = pltpu.PrefetchScalarGridSpec(
        num_scalar_prefetch=1,
        grid=(dec_batch,),
        in_specs=[qspec, qspec, self_spec, self_spec, hbm, hbm],
        out_specs=qspec,
        scratch_shapes=[pltpu.VMEM((DEC_SLOTS, n_pg * PAGE_SIZE, KV_RANK), F32),
                        pltpu.VMEM((DEC_SLOTS, ROPE_DIM, n_pg * PAGE_SIZE), F32),
                        pltpu.VMEM((DEC_SLOTS, n_pg // DEC_SPAN_PAGES, DEC_SPAN_PAGES // 2 * PAGE_SIZE, 2 * KV_RANK),
                                   BF16),
                        pltpu.SemaphoreType.DMA((2, DEC_SLOTS))],
    )
    return pl.pallas_call(
        functools.partial(_decode_attn_kernel, n_pg=n_pg, n_groups=n_pages // n_pg),
        grid_spec=grid_spec,
        out_shape=jax.ShapeDtypeStruct((dec_batch * MLA_HEADS, KV_RANK), F32),
        compiler_params=_cparams(("arbitrary",)),
        name="decode_attn",
    )(page_table, qlat2, qpe2, ckv_s3, kpe_s3, cache_ckv_e, cache_kpe_t)


def _s5_readout(hcat_blocks, u, cw_ref, d_ref, wglu_ref):
    y = jnp.concatenate([_dot(hcat_blocks[k], cw_ref[k]) for k in range(S5_NB)], axis=1)
    y = jax.nn.gelu(y + d_ref[...] * u)
    return y * jax.nn.sigmoid(_dot(y.astype(BF16), wglu_ref[...]))


def _s5_prompt_kernel(x_ref, win_ref, bw2_ref, a2_ref, cw2_ref, d_ref, wglu_ref,
                      yg_ref, hre_ref, him_ref, u_s, y_s, st_s, h_s):
    n_b, seg, _ = x_ref.shape
    half = S5_SLABS // 2
    sub = S5_SUB
    n_sub = seg // sub
    rows_sub = sub * n_b
    n_pair = sub // 2
    rows_pair = n_pair * n_b
    reg = rows_pair + n_b
    q_slabs = S5_WIDTH // LANES

    @pl.when(pl.program_id(0) == 0)
    def _():
        h_s[...] = jnp.zeros(h_s.shape, F32)

    def steps_of(slab_ref, s, q, parity):
        return jnp.concatenate([slab_ref[s, q, pl.ds((2 * j + parity) * n_b, n_b), :] for j in range(n_pair)],
                               axis=0)

    def project(s):
        xs = jnp.concatenate([x_ref[b, s * sub:(s + 1) * sub, :] for b in range(n_b)], axis=0)
        u = _dot(xs.astype(BF16), win_ref[...])
        for b in range(n_b):
            for q in range(q_slabs):
                u_s[s, q, pl.ds(b, sub, stride=n_b), :] = u[b * sub:(b + 1) * sub, q * LANES:(q + 1) * LANES]
        rows = slice(s * reg + n_b, (s + 1) * reg)
        for k in range(S5_NB):
            lhs = jnp.concatenate([steps_of(u_s, s, k, 0), steps_of(u_s, s, k, 1)], axis=1).astype(BF16)
            r = _dot(lhs, bw2_ref[k])
            for q in range(S5_BLK_SLABS):
                st_s[S5_BLK_SLABS * k + q, rows, :] = r[:, q * LANES:(q + 1) * LANES]
                st_s[half + S5_BLK_SLABS * k + q, rows, :] = r[:, (S5_BLK_SLABS + q) * LANES:
                                                                (S5_BLK_SLABS + q + 1) * LANES]

    def recur(s, h):
        st_s[:, s * reg:s * reg + n_b, :] = h
        for j in range(n_pair):
            rows = slice(s * reg + (j + 1) * n_b, s * reg + (j + 2) * n_b)
            inc = st_s[:, rows, :]
            ar, ai = a2_ref[:half], a2_ref[half:]
            hr, hi = h[:half], h[half:]
            h = jnp.concatenate([ar * hr - ai * hi + inc[:half], ar * hi + ai * hr + inc[half:]], axis=0)
            st_s[:, rows, :] = h
        return h

    def read_out(s):
        rows = slice(s * reg, s * reg + rows_pair)
        ys = []
        for k in range(S5_NB):
            lhs = jnp.concatenate(
                [st_s[S5_BLK_SLABS * k + q, rows, :] for q in range(S5_BLK_SLABS)]
                + [st_s[half + S5_BLK_SLABS * k + q, rows, :] for q in range(S5_BLK_SLABS)]
                + [steps_of(u_s, s, k, 0), steps_of(u_s, s, k, 1)], axis=1).astype(BF16)
            y2 = _dot(lhs, cw2_ref[k])
            ys.append(jnp.concatenate(
                [y2[j * n_b:(j + 1) * n_b, par * LANES:(par + 1) * LANES] for j in range(n_pair) for par in range(2)],
                axis=0))
        u_t = jnp.concatenate([u_s[s, q] for q in range(q_slabs)], axis=1)
        y = jax.nn.gelu(jnp.concatenate(ys, axis=1) + d_ref[...] * u_t)
        y = y * jax.nn.sigmoid(_dot(y.astype(BF16), wglu_ref[...]))
        for q in range(q_slabs):
            y_s[s, q] = y[:, q * LANES:(q + 1) * LANES]
        for b in range(n_b):
            yg_ref[b, s * sub:(s + 1) * sub, :] = jnp.concatenate(
                [y_s[s, q, pl.ds(b, sub, stride=n_b), :] for q in range(q_slabs)], axis=1).astype(BF16)

    h = h_s[...]
    project(0)
    for s in range(n_sub):
        if s + 1 < n_sub:
            project(s + 1)
        h = recur(s, h)
        if s >= 1:
            read_out(s - 1)
    read_out(n_sub - 1)
    h_s[...] = h
    hre_ref[...] = jnp.concatenate([h[s] for s in range(half)], axis=1)
    him_ref[...] = jnp.concatenate([h[half + s] for s in range(half)], axis=1)


def _s5_prompt(x3, w_in, bw2, a2_b, cw2, d_row, w_glu):
    batch, seq, _ = x3.shape
    seg = S5_L
    n_sub = seg // S5_SUB
    slab_rows = (n_sub, S5_WIDTH // LANES, S5_SUB * batch, LANES)
    return pl.pallas_call(
        _s5_prompt_kernel,
        grid=(seq // seg,),
        in_specs=[pl.BlockSpec((batch, seg, D_MODEL), lambda c: (0, c, 0)),
                  _const_spec(w_in.shape), _const_spec(bw2.shape), _const_spec(a2_b.shape),
                  _const_spec(cw2.shape), _const_spec(d_row.shape), _const_spec(w_glu.shape)],
        out_specs=[pl.BlockSpec((batch, seg, S5_WIDTH), lambda c: (0, c, 0)),
                   pl.BlockSpec((batch, S5_STATES), lambda c: (0, 0)),
                   pl.BlockSpec((batch, S5_STATES), lambda c: (0, 0))],
        out_shape=[jax.ShapeDtypeStruct((batch, seq, S5_WIDTH), BF16),
                   jax.ShapeDtypeStruct((batch, S5_STATES), F32),
                   jax.ShapeDtypeStruct((batch, S5_STATES), F32)],
        scratch_shapes=[pltpu.VMEM(slab_rows, F32), pltpu.VMEM(slab_rows, F32),
                        pltpu.VMEM((S5_SLABS, n_sub * (S5_SUB // 2 + 1) * batch, LANES), F32),
                        pltpu.VMEM((S5_SLABS, batch, LANES), F32)],
        compiler_params=_cparams(("arbitrary",), n_fusible_inputs=7),
        name="s5_prompt",
    )(x3, w_in, bw2, a2_b, cw2, d_row, w_glu)


def _s5_sample_kernel(x_ref, h0r_ref, h0i_ref, win_ref, bw_ref, ar_ref, ai_ref, cw_ref, d_ref, wglu_ref,
                      yg_ref, hre_ref, him_ref):
    u = _dot(x_ref[...].astype(BF16), win_ref[...])
    ub = u.astype(BF16)
    bu = [_dot(ub[:, k * LANES:(k + 1) * LANES], bw_ref[k]) for k in range(S5_NB)]
    w = S5_KB * S5_STATE
    bu_r = jnp.concatenate([r[:, :w] for r in bu], axis=1)
    bu_i = jnp.concatenate([r[:, w:] for r in bu], axis=1)
    ar, ai, h0r, h0i = ar_ref[...], ai_ref[...], h0r_ref[...], h0i_ref[...]
    hr = bu_r + (ar * h0r - ai * h0i)
    hi = bu_i + (ar * h0i + ai * h0r)
    hre_ref[...] = hr
    him_ref[...] = hi
    blocks = [jnp.concatenate([hr[:, k * w:(k + 1) * w], hi[:, k * w:(k + 1) * w]], axis=1).astype(BF16)
              for k in range(S5_NB)]
    yg_ref[...] = _s5_readout(blocks, u, cw_ref, d_ref, wglu_ref).astype(BF16)


def _s5_sample(xs, h0r, h0i, w_in, bw, a_r, a_i, cw, d_row, w_glu):
    n = xs.shape[0]
    args = (xs, h0r, h0i, w_in, bw, a_r, a_i, cw, d_row, w_glu)
    full = lambda shape: pl.BlockSpec(shape, lambda i: (0,) * len(shape))
    return pl.pallas_call(
        _s5_sample_kernel,
        grid=(1,),
        in_specs=[full(a.shape) for a in args],
        out_specs=[full((n, S5_WIDTH)), full((n, S5_STATES)), full((n, S5_STATES))],
        out_shape=[jax.ShapeDtypeStruct((n, S5_WIDTH), BF16),
                   jax.ShapeDtypeStruct((n, S5_STATES), F32),
                   jax.ShapeDtypeStruct((n, S5_STATES), F32)],
        compiler_params=_cparams(("arbitrary",), n_fusible_inputs=len(args)),
        name="s5_sample",
    )(*args)


def _rope_tables(pos):
    freqs = ROPE_THETA ** (-jnp.arange(ROPE_HALF, dtype=F32) / ROPE_HALF)
    ang = pos.astype(F32)[:, None] * freqs
    cos, sin = jnp.cos(ang), jnp.sin(ang)
    zero = jnp.zeros_like(cos)
    pad = jnp.zeros((pos.shape[0], LANES - ROPE_DIM), F32)
    cos_t = jnp.concatenate([cos, cos, pad], axis=1)
    sin_a = jnp.concatenate([zero, sin, pad], axis=1)
    sin_b = jnp.concatenate([-sin, zero, pad], axis=1)
    return cos_t, sin_a, sin_b


def _block_diag(blocks):
    *lead, n, r, c = blocks.shape
    eye = jnp.eye(n, dtype=blocks.dtype)
    return jnp.einsum('...nrc,nm->...nrmc', blocks, eye).reshape(*lead, n * r, n * c)


def _even_params(w_in, q_norm, w_q_b, kv_norm, w_kv_b, gm_g, gm_b, gm_w_s, gm_b_s, w_out):
    i1, i2, i3, i4 = Q_RANK, Q_RANK + KV_RANK, Q_RANK + KV_RANK + ROPE_DIM, Q_RANK + KV_RANK + ROPE_DIM + GM_WIDTH
    w_in_p = jnp.concatenate([w_in[:, :i2], w_in[:, i3:], w_in[:, i2:i3],
                              jnp.zeros((D_MODEL, LANES - ROPE_DIM), F32)], axis=1).astype(BF16)
    wq_nope = w_q_b[:, :, :NOPE_DIM].reshape(Q_RANK, MLA_HEADS * NOPE_DIM).astype(BF16)
    pe = w_q_b[:, :, NOPE_DIM:]
    pe_sw = jnp.concatenate([pe[:, :, ROPE_HALF:], pe[:, :, :ROPE_HALF]], axis=2)
    lane_pad = jnp.zeros((Q_RANK, MLA_HEADS, LANES - ROPE_DIM), F32)
    wq_pe = jnp.concatenate(
        [jnp.concatenate([w, lane_pad], axis=2).reshape(Q_RANK, MLA_HEADS * LANES) for w in (pe, pe_sw)],
        axis=1).astype(BF16)
    uk = jnp.transpose(w_kv_b[:, :, :NOPE_DIM], (1, 2, 0))
    uv = jnp.transpose(w_kv_b[:, :, NOPE_DIM:], (1, 0, 2))
    pairs = MLA_HEADS // 2
    w_uk = _block_diag(uk.reshape(pairs, 2, NOPE_DIM, KV_RANK)).astype(BF16)
    w_uv_p = _block_diag(uv.reshape(pairs, 2, KV_RANK, V_DIM)).astype(BF16)
    w_uv = _block_diag(uv).astype(BF16)
    causal = jnp.tril(jnp.ones((GM_CHUNK, GM_CHUNK), dtype=bool))
    ws_m = jnp.where(causal[None], gm_w_s, 0).astype(BF16)
    bs_full = jnp.repeat(gm_b_s.T, GM_HEAD_DIM, axis=1)
    ws0 = jnp.repeat(gm_w_s[:, 0, 0], GM_HEAD_DIM)[None, :].astype(BF16)
    bs0 = bs_full[:1]
    n_attn = MLA_HEADS * V_DIM
    return dict(w_in_p=w_in_p, qn=q_norm[None, :], wq_nope=wq_nope, wq_pe=wq_pe, w_uk=w_uk, w_uv=w_uv, w_uv_p=w_uv_p,
                kvn=kv_norm[None, :], gm_g=gm_g[None, :], gm_b=gm_b[None, :], ws_m=ws_m, bs_full=bs_full,
                ws0=ws0, bs0=bs0, wo_attn=w_out[:n_attn].astype(BF16), wo_gate=w_out[n_attn:].astype(BF16))


def _s5_params(w_in, a_re, a_im, b_re, b_im, c_re, c_im, d, log_dt, w_glu, w_out, batch):
    dt = jnp.exp(log_dt)[:, None]
    ld_r, ld_i = a_re * dt, a_im * dt
    mag = jnp.exp(ld_r)
    ab_r, ab_i = mag * jnp.cos(ld_i), mag * jnp.sin(ld_i)
    den = a_re * a_re + a_im * a_im
    cr = ((ab_r - 1.0) * a_re + ab_i * a_im) / den
    ci = (ab_i * a_re - (ab_r - 1.0) * a_im) / den
    bb_r = cr[..., None] * b_re - ci[..., None] * b_im
    bb_i = cr[..., None] * b_im + ci[..., None] * b_re

    def in_blocks(bb):
        return _block_diag(jnp.swapaxes(bb, 1, 2).reshape(S5_NB, S5_KB, S5_GROUP_DIM, S5_STATE))

    def out_blocks(cc):
        return _block_diag(jnp.swapaxes(cc, 1, 2).reshape(S5_NB, S5_KB, S5_STATE, S5_GROUP_DIM))

    bw = jnp.concatenate([in_blocks(bb_r), in_blocks(bb_i)], axis=2).astype(BF16)
    cw = jnp.concatenate([out_blocks(c_re), out_blocks(-c_im)], axis=1).astype(BF16)
    a_r, a_i = ab_r.reshape(1, S5_STATES), ab_i.reshape(1, S5_STATES)
    half = S5_SLABS // 2
    cmul = lambda xr, xi, yr, yi: (xr * yr - xi * yi, xr * yi + xi * yr)
    a2_r, a2_i = cmul(ab_r, ab_i, ab_r, ab_i)
    abb_r, abb_i = cmul(ab_r[..., None], ab_i[..., None], bb_r, bb_i)
    ca_r, ca_i = cmul(c_re, c_im, ab_r[:, None, :], ab_i[:, None, :])
    ca2_r, ca2_i = cmul(c_re, c_im, a2_r[:, None, :], a2_i[:, None, :])
    cb = jnp.einsum('gcp,gpd->gcd', c_re, bb_r) - jnp.einsum('gcp,gpd->gcd', c_im, bb_i)
    cab = jnp.einsum('gcp,gpd->gcd', ca_r, bb_r) - jnp.einsum('gcp,gpd->gcd', ca_i, bb_i)
    io_blocks = lambda m: _block_diag(jnp.swapaxes(m, 1, 2).reshape(S5_NB, S5_KB, S5_GROUP_DIM, S5_GROUP_DIM))
    bw2 = jnp.concatenate([jnp.concatenate([in_blocks(abb_r), in_blocks(abb_i)], axis=2),
                           jnp.concatenate([in_blocks(bb_r), in_blocks(bb_i)], axis=2)],
                          axis=1).astype(BF16)
    cw2 = jnp.concatenate(
        [jnp.concatenate([out_blocks(ca_r), out_blocks(ca2_r)], axis=2),
         jnp.concatenate([out_blocks(-ca_i), out_blocks(-ca2_i)], axis=2),
         jnp.concatenate([io_blocks(cb), io_blocks(cab)], axis=2),
         jnp.concatenate([jnp.zeros((S5_NB, LANES, LANES), F32), io_blocks(cb)], axis=2)],
        axis=1).astype(BF16)
    a2_b = jnp.concatenate([jnp.broadcast_to(a2_r.reshape(half, 1, LANES), (half, batch, LANES)),
                            jnp.broadcast_to(a2_i.reshape(half, 1, LANES), (half, batch, LANES))], axis=0)
    return dict(w_in=w_in.astype(BF16), bw=bw, cw=cw, a_r=a_r, a_i=a_i, bw2=bw2, cw2=cw2, a2_b=a2_b,
                d=d.reshape(1, S5_WIDTH), w_glu=w_glu.astype(BF16), w_out=w_out.astype(BF16))


def kernel(x_prompt, x_sample, cache_ckv, cache_kpe, state_s5_re, state_s5_im, page_table, w_in_even, mla_q_norm, mla_w_q_b, mla_kv_norm, mla_w_kv_b, gm_norm_g, gm_norm_b, gm_w_s, gm_b_s, w_out_even, w_in_odd, s5_a_re, s5_a_im, s5_b_re, s5_b_im, s5_c_re, s5_c_im, s5_d, s5_log_dt, s5_w_glu, w_out_odd, ln_mix_g, ln_mix_b, ln_ffn_g, ln_ffn_b, ffn_w1, ffn_w2):
    batch, seq, _ = x_prompt.shape
    dec_batch, dec_seq, _ = x_sample.shape
    assert dec_seq == 1 and seq % INPROJ_TM == 0 and seq % ATTN_TQ == 0 and seq % S5_L == 0
    assert (seq // ATTN_TQ) % 2 == 0
    assert (batch * seq) % FFN_TM == 0 and page_table.shape[1] % DEC_PG == 0
    assert batch == SUBLANES

    xp = x_prompt.reshape(batch * seq, D_MODEL)
    xs = x_sample.reshape(dec_batch, D_MODEL)
    tabs_p = _rope_tables(jnp.arange(seq, dtype=jnp.int32))
    tabs_s = _rope_tables(PAST_LEN + jnp.arange(dec_seq, dtype=jnp.int32))

    outs = {k: [] for k in ("ckv_p", "kpe_p", "ckv_s", "kpe_s", "gmv_p", "gmv_s",
                            "s5re_p", "s5im_p", "s5re_s", "s5im_s")}
    for layer in range(DEPTH):
        ln = (ln_mix_g[layer][None, :], ln_mix_b[layer][None, :],
              ln_ffn_g[layer][None, :], ln_ffn_b[layer][None, :])
        if layer % 2 == 0:
            e = layer // 2
            p = _even_params(w_in_even[e], mla_q_norm[e], mla_w_q_b[e], mla_kv_norm[e], mla_w_kv_b[e],
                             gm_norm_g[e], gm_norm_b[e], gm_w_s[e], gm_b_s[e], w_out_even[e])
            q_a, ckv, kpe, kcat, gate, gmv = _even_inproj(
                xp, p["w_in_p"], p["kvn"], p["gm_g"], p["gm_b"], p["ws_m"], p["bs_full"], *tabs_p, batch, seq)
            attn = _mla_attn(q_a, kcat, p["qn"], p["wq_nope"], p["wq_pe"], p["w_uk"], p["w_uv_p"], *tabs_p,
                             batch, seq)
            outs["ckv_p"].append(ckv.reshape(batch, seq, KV_RANK))
            outs["kpe_p"].append(jnp.swapaxes(kpe, 1, 2))
            outs["gmv_p"].append(gmv)
            ckv_s, kpe_s, gate_s, vn_s, qlat_s, qpe_s = _even_inproj_sample(
                xs, p["w_in_p"], p["kvn"], p["gm_g"], p["gm_b"], p["ws0"], p["bs0"], *tabs_s,
                p["qn"], p["wq_nope"], p["wq_pe"], p["w_uk"])
            kpe_pad = jnp.concatenate([kpe_s, jnp.zeros((dec_batch, LANES - ROPE_DIM), F32)], axis=1)
            o_lat = _decode_attn(page_table,
                                 qlat_s.reshape(dec_batch * MLA_HEADS, LANES),
                                 qpe_s.reshape(dec_batch * MLA_HEADS, LANES),
                                 ckv_s.reshape(dec_batch, 1, KV_RANK), kpe_pad.reshape(dec_batch, 1, LANES),
                                 cache_ckv[e], jnp.swapaxes(cache_kpe[e], 1, 2))
            xs, w1_b, w2_b = _mix_ffn_sample(
                xs, [o_lat.reshape(dec_batch, MLA_HEADS * KV_RANK), gate_s], p["w_uv"],
                [p["wo_attn"], p["wo_gate"]], *ln, ffn_w1, ffn_w2, layer, name="even_ffn_sample")
            xp = _mix_ffn(xp, [attn, gate], None, [p["wo_attn"], p["wo_gate"]], *ln, w1_b, w2_b,
                          name="even_ffn_prompt")
            outs["ckv_s"].append(ckv_s.reshape(dec_batch, 1, KV_RANK))
            outs["kpe_s"].append(kpe_s.reshape(dec_batch, 1, ROPE_DIM))
            outs["gmv_s"].append(vn_s.reshape(dec_batch, 1, GM_WIDTH))
        else:
            o = layer // 2
            p = _s5_params(w_in_odd[o], s5_a_re[o], s5_a_im[o], s5_b_re[o], s5_b_im[o], s5_c_re[o],
                           s5_c_im[o], s5_d[o], s5_log_dt[o], s5_w_glu[o], w_out_odd[o], batch)
            yg, hre, him = _s5_prompt(xp.reshape(batch, seq, D_MODEL), p["w_in"], p["bw2"], p["a2_b"],
                                      p["cw2"], p["d"], p["w_glu"])
            outs["s5re_p"].append(hre.reshape(batch, S5_GROUPS, S5_STATE))
            outs["s5im_p"].append(him.reshape(batch, S5_GROUPS, S5_STATE))
            yg_s, hre_s, him_s = _s5_sample(
                xs, state_s5_re[o].reshape(dec_batch, S5_STATES), state_s5_im[o].reshape(dec_batch, S5_STATES),
                p["w_in"], p["bw"], p["a_r"], p["a_i"], p["cw"], p["d"], p["w_glu"])
            xs, w1_b, w2_b = _mix_ffn_sample(xs, [yg_s], None, [p["w_out"]], *ln, ffn_w1, ffn_w2, layer,
                                             name="odd_ffn_sample")
            xp = _mix_ffn(xp, [yg.reshape(batch * seq, S5_WIDTH)], None, [p["w_out"]], *ln, w1_b, w2_b,
                          name="odd_ffn_prompt")
            outs["s5re_s"].append(hre_s.reshape(dec_batch, S5_GROUPS, S5_STATE))
            outs["s5im_s"].append(him_s.reshape(dec_batch, S5_GROUPS, S5_STATE))

    st = jnp.stack
    return (xp.reshape(batch, seq, D_MODEL), xs.reshape(dec_batch, dec_seq, D_MODEL),
            st(outs["ckv_p"]), st(outs["kpe_p"]), st(outs["ckv_s"]), st(outs["kpe_s"]),
            st(outs["gmv_p"]), st(outs["gmv_s"]),
            st(outs["s5re_p"]), st(outs["s5im_p"]), st(outs["s5re_s"]), st(outs["s5im_s"]))
```
